```python
import math
import jax, jax.numpy as jnp
from jax import lax
import numpy as np

D_MODEL = 2048
BATCH = 8
SEQ = 4096
DEPTH = 1

D_MIX = D_MODEL
HEAD_DIM = 128
ATTN_WIDTH = D_MIX // 2
ATTN_HEADS = ATTN_WIDTH // HEAD_DIM
SSM_WIDTH = D_MIX - ATTN_WIDTH
SSM_GROUP = 16
SSM_GROUPS = SSM_WIDTH // SSM_GROUP
SSM_STATE = 64
DILATED_PATTERNS = ((128, 1), (512, 4), (2048, 16))
BLK = 128
D_FF = 4 * D_MODEL
PLE_DIM = 256
RMS_EPS = 1e-6
NEG_INF = -1e30

kernel_name = "hybrid_dilated_attn_s5_block"


def _rmsnorm(x, g):
    xf = x.astype(jnp.float32)
    y = xf * lax.rsqrt(jnp.mean(xf * xf, axis=-1, keepdims=True) + RMS_EPS)
    return (y * g.astype(jnp.float32)).astype(x.dtype)


def _dilated_branch(q, k, v, window, dilation):
    B, S, H, E = q.shape
    M = S // dilation
    n_keys = window // dilation
    nb = -(-M // BLK)
    Mp = nb * BLK

    def blocks(t):
        t = t.reshape(B, M, dilation, H, E)
        t = jnp.pad(t, ((0, 0), (0, Mp - M), (0, 0), (0, 0), (0, 0)))
        return t.reshape(B, nb, BLK, dilation, H, E)

    def with_prev(t):
        prev = jnp.pad(t[:, :-1], ((0, 0), (1, 0), (0, 0), (0, 0), (0, 0), (0, 0)))
        return jnp.concatenate([prev, t], axis=2)

    qb = blocks(q)
    kk = with_prev(blocks(k))
    vv = with_prev(blocks(v))

    scale = 1.0 / math.sqrt(E)
    s = jnp.einsum('bnqrhe,bnkrhe->bnrhqk', qb, kk,
                   preferred_element_type=jnp.float32) * scale
    i = jnp.arange(BLK)
    j = jnp.arange(2 * BLK)
    blk = jnp.arange(nb)
    dist = BLK + i[:, None] - j[None, :]
    kpos = (blk[:, None] - 1) * BLK + j[None, :]
    mask = ((dist >= 0) & (dist <= n_keys))[None] & (kpos >= 0)[:, None, :]
    s = jnp.where(mask[None, :, None, None], s, NEG_INF)
    lse = jax.nn.logsumexp(s, axis=-1)
    prob = jnp.exp(s - lse[..., None])
    o = jnp.einsum('bnrhqk,bnkrhe->bnqrhe', prob, vv.astype(jnp.float32))
    o = o.reshape(B, Mp, dilation, H, E)[:, :M].reshape(B, S, H, E)
    lse = lse.transpose(0, 1, 4, 2, 3).reshape(B, Mp, dilation, H)[:, :M].reshape(B, S, H)
    return o, lse


def _dilated_attention(q, k, v):
    outs, lses = [], []
    for window, dilation in DILATED_PATTERNS:
        o, l = _dilated_branch(q, k, v, window, dilation)
        outs.append(o)
        lses.append(l)
    w = jax.nn.softmax(jnp.stack(lses, axis=0), axis=0)
    o = jnp.sum(w[..., None] * jnp.stack(outs, axis=0), axis=0)
    B, S, H, E = q.shape
    return o.reshape(B, S, H * E).astype(q.dtype)


def _ssm_combine(e1, e2):
    a1r, a1i, b1r, b1i = e1
    a2r, a2i, b2r, b2i = e2
    ar = a1r * a2r - a1i * a2i
    ai = a1r * a2i + a1i * a2r
    br = a2r * b1r - a2i * b1i + b2r
    bi = a2r * b1i + a2i * b1r + b2i
    return (ar, ai, br, bi)


def _s5(u, lam_re, lam_im, log_dt, b_re, b_im, c_re, c_im, d_skip, w_glu, b_glu):
    B, S, _ = u.shape
    uf = u.astype(jnp.float32).reshape(B, S, SSM_GROUPS, SSM_GROUP)
    lr0 = lam_re.astype(jnp.float32)
    li0 = lam_im.astype(jnp.float32)
    dt = jnp.exp(log_dt.astype(jnp.float32))[:, None]
    mag = jnp.exp(lr0 * dt)
    abar_r = mag * jnp.cos(li0 * dt)
    abar_i = mag * jnp.sin(li0 * dt)
    num_r = abar_r - 1.0
    num_i = abar_i
    den = lr0 * lr0 + li0 * li0
    coef_r = ((num_r * lr0 + num_i * li0) / den)[..., None]
    coef_i = ((num_i * lr0 - num_r * li0) / den)[..., None]
    br = b_re.astype(jnp.float32)
    bi = b_im.astype(jnp.float32)
    bbar_r = coef_r * br - coef_i * bi
    bbar_i = coef_r * bi + coef_i * br
    bu_r = jnp.einsum('bsgc,gpc->bsgp', uf, bbar_r)
    bu_i = jnp.einsum('bsgc,gpc->bsgp', uf, bbar_i)
    a_r = jnp.broadcast_to(abar_r[None, None], (1, S, SSM_GROUPS, SSM_STATE))
    a_i = jnp.broadcast_to(abar_i[None, None], (1, S, SSM_GROUPS, SSM_STATE))
    _, _, st_r, st_i = lax.associative_scan(_ssm_combine, (a_r, a_i, bu_r, bu_i), axis=1)
    y = (jnp.einsum('gcp,bsgp->bsgc', c_re.astype(jnp.float32), st_r)
         - jnp.einsum('gcp,bsgp->bsgc', c_im.astype(jnp.float32), st_i))
    y = y.reshape(B, S, SSM_WIDTH) + d_skip.astype(jnp.float32) * u.astype(jnp.float32)
    y = jax.nn.gelu(y).astype(u.dtype)
    gate = jax.nn.sigmoid(y @ w_glu + b_glu)
    return y * gate


def _fwd_setup_inputs(seed: int = 0) -> dict:
    key = jax.random.key(seed)
    ks = jax.random.split(key, 32)
    f32 = jnp.float32
    L = DEPTH

    def nrm(k, shape, scale):
        return jax.random.normal(k, shape, f32) * scale

    def gain(k, shape):
        return 1.0 + 0.02 * jax.random.normal(k, shape, f32)

    G, P, C = SSM_GROUPS, SSM_STATE, SSM_GROUP
    n_idx = jnp.arange(P, dtype=f32)
    return {
        "x": nrm(ks[0], (BATCH, SEQ, D_MODEL), 1.0),
        "p": nrm(ks[1], (DEPTH, BATCH, SEQ, PLE_DIM), 1.0),
        "mix_norm_pre": gain(ks[2], (L, D_MODEL)),
        "w_in": nrm(ks[3], (L, D_MODEL, 3 * ATTN_WIDTH + SSM_WIDTH), D_MODEL ** -0.5),
        "lam_re": -0.5 + 0.01 * jax.random.normal(ks[4], (L, G, P), f32),
        "lam_im": math.pi * n_idx + 0.01 * jax.random.normal(ks[5], (L, G, P), f32),
        "log_dt": jax.random.uniform(ks[6], (L, G), f32, math.log(1e-3), math.log(1e-1)),
        "ssm_b_re": nrm(ks[7], (L, G, P, C), (2 * C) ** -0.5),
        "ssm_b_im": nrm(ks[8], (L, G, P, C), (2 * C) ** -0.5),
        "ssm_c_re": nrm(ks[9], (L, G, C, P), (2 * P) ** -0.5),
        "ssm_c_im": nrm(ks[10], (L, G, C, P), (2 * P) ** -0.5),
        "ssm_d": nrm(ks[11], (L, SSM_WIDTH), 1.0),
        "w_glu": nrm(ks[12], (L, SSM_WIDTH, SSM_WIDTH), SSM_WIDTH ** -0.5),
        "b_glu": nrm(ks[13], (L, SSM_WIDTH), 0.01),
        "attn_out_norm": gain(ks[14], (L, ATTN_WIDTH)),
        "ssm_out_norm": gain(ks[15], (L, SSM_WIDTH)),
        "w_out": nrm(ks[16], (L, D_MIX, D_MODEL), D_MIX ** -0.5),
        "mix_norm_post": gain(ks[17], (L, D_MODEL)),
        "mlp_norm_pre": gain(ks[18], (L, D_MODEL)),
        "w_up": nrm(ks[19], (L, D_MODEL, D_FF), D_MODEL ** -0.5),
        "w_down": nrm(ks[20], (L, D_FF, D_MODEL), D_FF ** -0.5),
        "mlp_norm_post": gain(ks[21], (L, D_MODEL)),
        "ple_norm_pre": gain(ks[22], (L, D_MODEL)),
        "w_ple_gate": nrm(ks[23], (L, D_MODEL, D_MODEL), D_MODEL ** -0.5),
        "w_ple_proj": nrm(ks[24], (L, PLE_DIM, D_MODEL), PLE_DIM ** -0.5),
        "ple_norm_post": gain(ks[25], (L, D_MODEL)),
    }


def _fwd_reference(x, p, mix_norm_pre, w_in, lam_re, lam_im, log_dt, ssm_b_re, ssm_b_im,
              ssm_c_re, ssm_c_im, ssm_d, w_glu, b_glu, attn_out_norm, ssm_out_norm,
              w_out, mix_norm_post, mlp_norm_pre, w_up, w_down, mlp_norm_post,
              ple_norm_pre, w_ple_gate, w_ple_proj, ple_norm_post):
    B, S, _ = x.shape
    h = x
    for i in range(DEPTH):
        hn = _rmsnorm(h, mix_norm_pre[i])
        proj = hn @ w_in[i]
        q = proj[..., :ATTN_WIDTH].reshape(B, S, ATTN_HEADS, HEAD_DIM)
        k = proj[..., ATTN_WIDTH:2 * ATTN_WIDTH].reshape(B, S, ATTN_HEADS, HEAD_DIM)
        v = proj[..., 2 * ATTN_WIDTH:3 * ATTN_WIDTH].reshape(B, S, ATTN_HEADS, HEAD_DIM)
        u = proj[..., 3 * ATTN_WIDTH:]
        attn = _dilated_attention(q, k, v)
        ssm = _s5(u, lam_re[i], lam_im[i], log_dt[i], ssm_b_re[i], ssm_b_im[i],
                  ssm_c_re[i], ssm_c_im[i], ssm_d[i], w_glu[i], b_glu[i])
        mixed = jnp.concatenate([_rmsnorm(attn, attn_out_norm[i]),
                                 _rmsnorm(ssm, ssm_out_norm[i])], axis=-1)
        h = h + _rmsnorm(mixed @ w_out[i], mix_norm_post[i])
        hn = _rmsnorm(h, mlp_norm_pre[i])
        ff = jnp.square(jax.nn.relu(hn @ w_up[i])) @ w_down[i]
        h = h + _rmsnorm(ff, mlp_norm_post[i])
        gate = jax.nn.sigmoid(_rmsnorm(h, ple_norm_pre[i]) @ w_ple_gate[i])
        e = p[i] @ w_ple_proj[i]
        h = h + _rmsnorm(gate * e, ple_norm_post[i])
    return h


import jax as _jax
import jax.numpy as _jnp

TWIN_FORMAT = 'train_step'
FWD_PARAMS = ['x', 'p', 'mix_norm_pre', 'w_in', 'lam_re', 'lam_im', 'log_dt', 'ssm_b_re', 'ssm_b_im', 'ssm_c_re', 'ssm_c_im', 'ssm_d', 'w_glu', 'b_glu', 'attn_out_norm', 'ssm_out_norm', 'w_out', 'mix_norm_post', 'mlp_norm_pre', 'w_up', 'w_down', 'mlp_norm_post', 'ple_norm_pre', 'w_ple_gate', 'w_ple_proj', 'ple_norm_post']
TWIN_WEIGHTS = ['mix_norm_pre', 'w_in', 'lam_re', 'lam_im', 'log_dt', 'ssm_b_re', 'ssm_b_im', 'ssm_c_re', 'ssm_c_im', 'ssm_d', 'w_glu', 'b_glu', 'attn_out_norm', 'ssm_out_norm', 'w_out', 'mix_norm_post', 'mlp_norm_pre', 'w_up', 'w_down', 'mlp_norm_post', 'ple_norm_pre', 'w_ple_gate', 'w_ple_proj', 'ple_norm_post']
TWIN_DIFF_INPUT = 'x'
TWIN_INPUTS = ['x', 'p', 'mix_norm_pre', 'w_in', 'lam_re', 'lam_im', 'log_dt', 'ssm_b_re', 'ssm_b_im', 'ssm_c_re', 'ssm_c_im', 'ssm_d', 'w_glu', 'b_glu', 'attn_out_norm', 'ssm_out_norm', 'w_out', 'mix_norm_post', 'mlp_norm_pre', 'w_up', 'w_down', 'mlp_norm_post', 'ple_norm_pre', 'w_ple_gate', 'w_ple_proj', 'ple_norm_post', 'loss_target', 'm_mix_norm_pre', 'm_w_in', 'm_lam_re', 'm_lam_im', 'm_log_dt', 'm_ssm_b_re', 'm_ssm_b_im', 'm_ssm_c_re', 'm_ssm_c_im', 'm_ssm_d', 'm_w_glu', 'm_b_glu', 'm_attn_out_norm', 'm_ssm_out_norm', 'm_w_out', 'm_mix_norm_post', 'm_mlp_norm_pre', 'm_w_up', 'm_w_down', 'm_mlp_norm_post', 'm_ple_norm_pre', 'm_w_ple_gate', 'm_w_ple_proj', 'm_ple_norm_post', 'v_mix_norm_pre', 'v_w_in', 'v_lam_re', 'v_lam_im', 'v_log_dt', 'v_ssm_b_re', 'v_ssm_b_im', 'v_ssm_c_re', 'v_ssm_c_im', 'v_ssm_d', 'v_w_glu', 'v_b_glu', 'v_attn_out_norm', 'v_ssm_out_norm', 'v_w_out', 'v_mix_norm_post', 'v_mlp_norm_pre', 'v_w_up', 'v_w_down', 'v_mlp_norm_post', 'v_ple_norm_pre', 'v_w_ple_gate', 'v_w_ple_proj', 'v_ple_norm_post']
TWIN_OUTPUTS = ['loss', 'grad_x', 'grad_mix_norm_pre', 'grad_w_in', 'grad_lam_re', 'grad_lam_im', 'grad_log_dt', 'grad_ssm_b_re', 'grad_ssm_b_im', 'grad_ssm_c_re', 'grad_ssm_c_im', 'grad_ssm_d', 'grad_w_glu', 'grad_b_glu', 'grad_attn_out_norm', 'grad_ssm_out_norm', 'grad_w_out', 'grad_mix_norm_post', 'grad_mlp_norm_pre', 'grad_w_up', 'grad_w_down', 'grad_mlp_norm_post', 'grad_ple_norm_pre', 'grad_w_ple_gate', 'grad_w_ple_proj', 'grad_ple_norm_post', 'delta_mix_norm_pre', 'delta_w_in', 'delta_lam_re', 'delta_lam_im', 'delta_log_dt', 'delta_ssm_b_re', 'delta_ssm_b_im', 'delta_ssm_c_re', 'delta_ssm_c_im', 'delta_ssm_d', 'delta_w_glu', 'delta_b_glu', 'delta_attn_out_norm', 'delta_ssm_out_norm', 'delta_w_out', 'delta_mix_norm_post', 'delta_mlp_norm_pre', 'delta_w_up', 'delta_w_down', 'delta_mlp_norm_post', 'delta_ple_norm_pre', 'delta_w_ple_gate', 'delta_w_ple_proj', 'delta_ple_norm_post', 'new_m_mix_norm_pre', 'new_m_w_in', 'new_m_lam_re', 'new_m_lam_im', 'new_m_log_dt', 'new_m_ssm_b_re', 'new_m_ssm_b_im', 'new_m_ssm_c_re', 'new_m_ssm_c_im', 'new_m_ssm_d', 'new_m_w_glu', 'new_m_b_glu', 'new_m_attn_out_norm', 'new_m_ssm_out_norm', 'new_m_w_out', 'new_m_mix_norm_post', 'new_m_mlp_norm_pre', 'new_m_w_up', 'new_m_w_down', 'new_m_mlp_norm_post', 'new_m_ple_norm_pre', 'new_m_w_ple_gate', 'new_m_w_ple_proj', 'new_m_ple_norm_post', 'new_v_mix_norm_pre', 'new_v_w_in', 'new_v_lam_re', 'new_v_lam_im', 'new_v_log_dt', 'new_v_ssm_b_re', 'new_v_ssm_b_im', 'new_v_ssm_c_re', 'new_v_ssm_c_im', 'new_v_ssm_d', 'new_v_w_glu', 'new_v_b_glu', 'new_v_attn_out_norm', 'new_v_ssm_out_norm', 'new_v_w_out', 'new_v_mix_norm_post', 'new_v_mlp_norm_pre', 'new_v_w_up', 'new_v_w_down', 'new_v_mlp_norm_post', 'new_v_ple_norm_pre', 'new_v_w_ple_gate', 'new_v_w_ple_proj', 'new_v_ple_norm_post']
TWIN_LEAF_KINDS = {'loss': 'loss', 'grad_x': 'grad_x', 'grad_mix_norm_pre': 'grad_w', 'grad_w_in': 'grad_w', 'grad_lam_re': 'grad_w', 'grad_lam_im': 'grad_w', 'grad_log_dt': 'grad_w', 'grad_ssm_b_re': 'grad_w', 'grad_ssm_b_im': 'grad_w', 'grad_ssm_c_re': 'grad_w', 'grad_ssm_c_im': 'grad_w', 'grad_ssm_d': 'grad_w', 'grad_w_glu': 'grad_w', 'grad_b_glu': 'grad_w', 'grad_attn_out_norm': 'grad_w', 'grad_ssm_out_norm': 'grad_w', 'grad_w_out': 'grad_w', 'grad_mix_norm_post': 'grad_w', 'grad_mlp_norm_pre': 'grad_w', 'grad_w_up': 'grad_w', 'grad_w_down': 'grad_w', 'grad_mlp_norm_post': 'grad_w', 'grad_ple_norm_pre': 'grad_w', 'grad_w_ple_gate': 'grad_w', 'grad_w_ple_proj': 'grad_w', 'grad_ple_norm_post': 'grad_w', 'delta_mix_norm_pre': 'delta_w', 'delta_w_in': 'delta_w', 'delta_lam_re': 'delta_w', 'delta_lam_im': 'delta_w', 'delta_log_dt': 'delta_w', 'delta_ssm_b_re': 'delta_w', 'delta_ssm_b_im': 'delta_w', 'delta_ssm_c_re': 'delta_w', 'delta_ssm_c_im': 'delta_w', 'delta_ssm_d': 'delta_w', 'delta_w_glu': 'delta_w', 'delta_b_glu': 'delta_w', 'delta_attn_out_norm': 'delta_w', 'delta_ssm_out_norm': 'delta_w', 'delta_w_out': 'delta_w', 'delta_mix_norm_post': 'delta_w', 'delta_mlp_norm_pre': 'delta_w', 'delta_w_up': 'delta_w', 'delta_w_down': 'delta_w', 'delta_mlp_norm_post': 'delta_w', 'delta_ple_norm_pre': 'delta_w', 'delta_w_ple_gate': 'delta_w', 'delta_w_ple_proj': 'delta_w', 'delta_ple_norm_post': 'delta_w', 'new_m_mix_norm_pre': 'new_m', 'new_m_w_in': 'new_m', 'new_m_lam_re': 'new_m', 'new_m_lam_im': 'new_m', 'new_m_log_dt': 'new_m', 'new_m_ssm_b_re': 'new_m', 'new_m_ssm_b_im': 'new_m', 'new_m_ssm_c_re': 'new_m', 'new_m_ssm_c_im': 'new_m', 'new_m_ssm_d': 'new_m', 'new_m_w_glu': 'new_m', 'new_m_b_glu': 'new_m', 'new_m_attn_out_norm': 'new_m', 'new_m_ssm_out_norm': 'new_m', 'new_m_w_out': 'new_m', 'new_m_mix_norm_post': 'new_m', 'new_m_mlp_norm_pre': 'new_m', 'new_m_w_up': 'new_m', 'new_m_w_down': 'new_m', 'new_m_mlp_norm_post': 'new_m', 'new_m_ple_norm_pre': 'new_m', 'new_m_w_ple_gate': 'new_m', 'new_m_w_ple_proj': 'new_m', 'new_m_ple_norm_post': 'new_m', 'new_v_mix_norm_pre': 'new_v', 'new_v_w_in': 'new_v', 'new_v_lam_re': 'new_v', 'new_v_lam_im': 'new_v', 'new_v_log_dt': 'new_v', 'new_v_ssm_b_re': 'new_v', 'new_v_ssm_b_im': 'new_v', 'new_v_ssm_c_re': 'new_v', 'new_v_ssm_c_im': 'new_v', 'new_v_ssm_d': 'new_v', 'new_v_w_glu': 'new_v', 'new_v_b_glu': 'new_v', 'new_v_attn_out_norm': 'new_v', 'new_v_ssm_out_norm': 'new_v', 'new_v_w_out': 'new_v', 'new_v_mix_norm_post': 'new_v', 'new_v_mlp_norm_pre': 'new_v', 'new_v_w_up': 'new_v', 'new_v_w_down': 'new_v', 'new_v_mlp_norm_post': 'new_v', 'new_v_ple_norm_pre': 'new_v', 'new_v_w_ple_gate': 'new_v', 'new_v_w_ple_proj': 'new_v', 'new_v_ple_norm_post': 'new_v'}


def _forward(args):
    return _fwd_reference(*[args[k] for k in FWD_PARAMS])


def _output_shape():
    def fwd():
        inp = _fwd_setup_inputs(0)
        return _fwd_reference(*[inp[k] for k in FWD_PARAMS])
    out = _jax.eval_shape(fwd)
    return out.shape, out.dtype

N_MICROBATCH = 1
ADAM_LR = 0.001
ADAM_B1 = 0.9
ADAM_B2 = 0.999
ADAM_EPS = 1e-08
ADAM_WD = 0.01
ADAM_STEP = 10
PER_EXAMPLE_BATCH_AXIS = {'x': 0, 'p': 1, 'loss_target': 0}
SHARED_INPUTS = []
_WEIGHT_DTYPES = {'mix_norm_pre': _jnp.float32, 'w_in': _jnp.float32, 'lam_re': _jnp.float32, 'lam_im': _jnp.float32, 'log_dt': _jnp.float32, 'ssm_b_re': _jnp.float32, 'ssm_b_im': _jnp.float32, 'ssm_c_re': _jnp.float32, 'ssm_c_im': _jnp.float32, 'ssm_d': _jnp.float32, 'w_glu': _jnp.float32, 'b_glu': _jnp.float32, 'attn_out_norm': _jnp.float32, 'ssm_out_norm': _jnp.float32, 'w_out': _jnp.float32, 'mix_norm_post': _jnp.float32, 'mlp_norm_pre': _jnp.float32, 'w_up': _jnp.float32, 'w_down': _jnp.float32, 'mlp_norm_post': _jnp.float32, 'ple_norm_pre': _jnp.float32, 'w_ple_gate': _jnp.float32, 'w_ple_proj': _jnp.float32, 'ple_norm_post': _jnp.float32}
MOMENT_SCALE = {'mix_norm_pre': 5.510919e-01, 'w_in': 3.529891e-01, 'lam_re': 1.649103e-02, 'lam_im': 1.287121e-02, 'log_dt': 8.196339e+00, 'ssm_b_re': 9.797173e-03, 'ssm_b_im': 9.794247e-03, 'ssm_c_re': 1.993904e-02, 'ssm_c_im': 1.928379e-02, 'ssm_d': 3.695440e+00, 'w_glu': 4.634277e-01, 'b_glu': 1.477913e+00, 'attn_out_norm': 5.582613e-01, 'ssm_out_norm': 3.300860e+00, 'w_out': 2.350228e+00, 'mix_norm_post': 1.625451e+01, 'mlp_norm_pre': 8.751525e-01, 'w_up': 4.294851e-01, 'w_down': 2.261810e+00, 'mlp_norm_post': 1.657878e+01, 'ple_norm_pre': 8.612148e-02, 'w_ple_gate': 8.506550e-02, 'w_ple_proj': 1.792941e-01, 'ple_norm_post': 1.642579e+01}


def _to_microbatches(a, axis):
    t = _jnp.moveaxis(a, axis, 0)
    t = t.reshape((N_MICROBATCH, t.shape[0] // N_MICROBATCH) + t.shape[1:])
    return _jnp.moveaxis(t, 1, axis + 1)


def setup_inputs(seed: int = 0) -> dict:
    inp = _fwd_setup_inputs(seed)
    key = _jax.random.fold_in(_jax.random.key(seed), 7919)
    shape, _ = _output_shape()
    out = dict(inp)
    out["loss_target"] = _jax.random.normal(_jax.random.fold_in(key, 0), shape, _jnp.float32)
    for i, name in enumerate(TWIN_WEIGHTS):
        w = inp[name].astype(_jnp.float32)
        if MOMENT_SCALE is None:
            s = _jnp.sqrt(_jnp.mean(_jnp.square(w)) + 1e-30)
        else:
            s = MOMENT_SCALE[name]
        km, kv = _jax.random.split(_jax.random.fold_in(key, i + 1))
        out[name] = w
        out["m_" + name] = s * _jax.random.normal(km, w.shape, _jnp.float32)
        out["v_" + name] = (s * s) * _jax.random.uniform(kv, w.shape, _jnp.float32, 0.5, 1.5)
    if N_MICROBATCH > 1:
        for name, axis in PER_EXAMPLE_BATCH_AXIS.items():
            out[name] = _to_microbatches(out[name], axis)
    return {'x': out['x'], 'p': out['p'], 'mix_norm_pre': out['mix_norm_pre'], 'w_in': out['w_in'], 'lam_re': out['lam_re'], 'lam_im': out['lam_im'], 'log_dt': out['log_dt'], 'ssm_b_re': out['ssm_b_re'], 'ssm_b_im': out['ssm_b_im'], 'ssm_c_re': out['ssm_c_re'], 'ssm_c_im': out['ssm_c_im'], 'ssm_d': out['ssm_d'], 'w_glu': out['w_glu'], 'b_glu': out['b_glu'], 'attn_out_norm': out['attn_out_norm'], 'ssm_out_norm': out['ssm_out_norm'], 'w_out': out['w_out'], 'mix_norm_post': out['mix_norm_post'], 'mlp_norm_pre': out['mlp_norm_pre'], 'w_up': out['w_up'], 'w_down': out['w_down'], 'mlp_norm_post': out['mlp_norm_post'], 'ple_norm_pre': out['ple_norm_pre'], 'w_ple_gate': out['w_ple_gate'], 'w_ple_proj': out['w_ple_proj'], 'ple_norm_post': out['ple_norm_post'], 'loss_target': out['loss_target'], 'm_mix_norm_pre': out['m_mix_norm_pre'], 'm_w_in': out['m_w_in'], 'm_lam_re': out['m_lam_re'], 'm_lam_im': out['m_lam_im'], 'm_log_dt': out['m_log_dt'], 'm_ssm_b_re': out['m_ssm_b_re'], 'm_ssm_b_im': out['m_ssm_b_im'], 'm_ssm_c_re': out['m_ssm_c_re'], 'm_ssm_c_im': out['m_ssm_c_im'], 'm_ssm_d': out['m_ssm_d'], 'm_w_glu': out['m_w_glu'], 'm_b_glu': out['m_b_glu'], 'm_attn_out_norm': out['m_attn_out_norm'], 'm_ssm_out_norm': out['m_ssm_out_norm'], 'm_w_out': out['m_w_out'], 'm_mix_norm_post': out['m_mix_norm_post'], 'm_mlp_norm_pre': out['m_mlp_norm_pre'], 'm_w_up': out['m_w_up'], 'm_w_down': out['m_w_down'], 'm_mlp_norm_post': out['m_mlp_norm_post'], 'm_ple_norm_pre': out['m_ple_norm_pre'], 'm_w_ple_gate': out['m_w_ple_gate'], 'm_w_ple_proj': out['m_w_ple_proj'], 'm_ple_norm_post': out['m_ple_norm_post'], 'v_mix_norm_pre': out['v_mix_norm_pre'], 'v_w_in': out['v_w_in'], 'v_lam_re': out['v_lam_re'], 'v_lam_im': out['v_lam_im'], 'v_log_dt': out['v_log_dt'], 'v_ssm_b_re': out['v_ssm_b_re'], 'v_ssm_b_im': out['v_ssm_b_im'], 'v_ssm_c_re': out['v_ssm_c_re'], 'v_ssm_c_im': out['v_ssm_c_im'], 'v_ssm_d': out['v_ssm_d'], 'v_w_glu': out['v_w_glu'], 'v_b_glu': out['v_b_glu'], 'v_attn_out_norm': out['v_attn_out_norm'], 'v_ssm_out_norm': out['v_ssm_out_norm'], 'v_w_out': out['v_w_out'], 'v_mix_norm_post': out['v_mix_norm_post'], 'v_mlp_norm_pre': out['v_mlp_norm_pre'], 'v_w_up': out['v_w_up'], 'v_w_down': out['v_w_down'], 'v_mlp_norm_post': out['v_mlp_norm_post'], 'v_ple_norm_pre': out['v_ple_norm_pre'], 'v_w_ple_gate': out['v_w_ple_gate'], 'v_w_ple_proj': out['v_w_ple_proj'], 'v_ple_norm_post': out['v_ple_norm_post']}


def _loss(weights, diff, rest, loss_target):
    with _jax.named_scope("forward"):
        args = {**rest, TWIN_DIFF_INPUT: diff, **{k: w.astype(_WEIGHT_DTYPES[k]) for k, w in weights.items()}}
        y = _forward(args)
    with _jax.named_scope("loss_head"):
        err = _jnp.square(y.astype(_jnp.float32) - loss_target)
        return 0.5 * _jnp.sum(_jnp.mean(err, axis=-1)) if err.ndim else 0.5 * err


def _adamw(w, g, m, v):
    m = ADAM_B1 * m + (1.0 - ADAM_B1) * g
    v = ADAM_B2 * v + (1.0 - ADAM_B2) * _jnp.square(g)
    m_hat = m / (1.0 - ADAM_B1 ** ADAM_STEP)
    v_hat = v / (1.0 - ADAM_B2 ** ADAM_STEP)
    delta = -ADAM_LR * (m_hat / (_jnp.sqrt(v_hat) + ADAM_EPS) + ADAM_WD * w)
    return delta, m, v


def reference(x, p, mix_norm_pre, w_in, lam_re, lam_im, log_dt, ssm_b_re, ssm_b_im, ssm_c_re, ssm_c_im, ssm_d, w_glu, b_glu, attn_out_norm, ssm_out_norm, w_out, mix_norm_post, mlp_norm_pre, w_up, w_down, mlp_norm_post, ple_norm_pre, w_ple_gate, w_ple_proj, ple_norm_post, loss_target, m_mix_norm_pre, m_w_in, m_lam_re, m_lam_im, m_log_dt, m_ssm_b_re, m_ssm_b_im, m_ssm_c_re, m_ssm_c_im, m_ssm_d, m_w_glu, m_b_glu, m_attn_out_norm, m_ssm_out_norm, m_w_out, m_mix_norm_post, m_mlp_norm_pre, m_w_up, m_w_down, m_mlp_norm_post, m_ple_norm_pre, m_w_ple_gate, m_w_ple_proj, m_ple_norm_post, v_mix_norm_pre, v_w_in, v_lam_re, v_lam_im, v_log_dt, v_ssm_b_re, v_ssm_b_im, v_ssm_c_re, v_ssm_c_im, v_ssm_d, v_w_glu, v_b_glu, v_attn_out_norm, v_ssm_out_norm, v_w_out, v_mix_norm_post, v_mlp_norm_pre, v_w_up, v_w_down, v_mlp_norm_post, v_ple_norm_pre, v_w_ple_gate, v_w_ple_proj, v_ple_norm_post):
    given = dict(x=x, p=p, mix_norm_pre=mix_norm_pre, w_in=w_in, lam_re=lam_re, lam_im=lam_im, log_dt=log_dt, ssm_b_re=ssm_b_re, ssm_b_im=ssm_b_im, ssm_c_re=ssm_c_re, ssm_c_im=ssm_c_im, ssm_d=ssm_d, w_glu=w_glu, b_glu=b_glu, attn_out_norm=attn_out_norm, ssm_out_norm=ssm_out_norm, w_out=w_out, mix_norm_post=mix_norm_post, mlp_norm_pre=mlp_norm_pre, w_up=w_up, w_down=w_down, mlp_norm_post=mlp_norm_post, ple_norm_pre=ple_norm_pre, w_ple_gate=w_ple_gate, w_ple_proj=w_ple_proj, ple_norm_post=ple_norm_post, loss_target=loss_target, m_mix_norm_pre=m_mix_norm_pre, m_w_in=m_w_in, m_lam_re=m_lam_re, m_lam_im=m_lam_im, m_log_dt=m_log_dt, m_ssm_b_re=m_ssm_b_re, m_ssm_b_im=m_ssm_b_im, m_ssm_c_re=m_ssm_c_re, m_ssm_c_im=m_ssm_c_im, m_ssm_d=m_ssm_d, m_w_glu=m_w_glu, m_b_glu=m_b_glu, m_attn_out_norm=m_attn_out_norm, m_ssm_out_norm=m_ssm_out_norm, m_w_out=m_w_out, m_mix_norm_post=m_mix_norm_post, m_mlp_norm_pre=m_mlp_norm_pre, m_w_up=m_w_up, m_w_down=m_w_down, m_mlp_norm_post=m_mlp_norm_post, m_ple_norm_pre=m_ple_norm_pre, m_w_ple_gate=m_w_ple_gate, m_w_ple_proj=m_w_ple_proj, m_ple_norm_post=m_ple_norm_post, v_mix_norm_pre=v_mix_norm_pre, v_w_in=v_w_in, v_lam_re=v_lam_re, v_lam_im=v_lam_im, v_log_dt=v_log_dt, v_ssm_b_re=v_ssm_b_re, v_ssm_b_im=v_ssm_b_im, v_ssm_c_re=v_ssm_c_re, v_ssm_c_im=v_ssm_c_im, v_ssm_d=v_ssm_d, v_w_glu=v_w_glu, v_b_glu=v_b_glu, v_attn_out_norm=v_attn_out_norm, v_ssm_out_norm=v_ssm_out_norm, v_w_out=v_w_out, v_mix_norm_post=v_mix_norm_post, v_mlp_norm_pre=v_mlp_norm_pre, v_w_up=v_w_up, v_w_down=v_w_down, v_mlp_norm_post=v_mlp_norm_post, v_ple_norm_pre=v_ple_norm_pre, v_w_ple_gate=v_w_ple_gate, v_w_ple_proj=v_w_ple_proj, v_ple_norm_post=v_ple_norm_post)
    weights = {n: given[n] for n in TWIN_WEIGHTS}
    shared = {n: given[n] for n in SHARED_INPUTS}
    per_example = {n: given[n] for n in ['x', 'p']}
    grad_fn = _jax.value_and_grad(_loss, argnums=(0, 1))

    def one_microbatch(ex, loss_target):
        ex = dict(ex)
        diff = ex.pop(TWIN_DIFF_INPUT)
        return grad_fn(weights, diff, {**shared, **ex}, loss_target)

    if N_MICROBATCH == 1:
        loss, (grad_w, grad_x) = one_microbatch(per_example, given["loss_target"])
    else:
        def body(carry, xs):
            loss_sum, grad_sum = carry
            l_k, (gw_k, gx_k) = one_microbatch(xs[0], xs[1])
            with _jax.named_scope("update"):
                return (loss_sum + l_k, _jax.tree.map(_jnp.add, grad_sum, gw_k)), gx_k

        init = (_jnp.zeros((), _jnp.float32), _jax.tree.map(_jnp.zeros_like, weights))
        (loss, grad_w), grad_x = _jax.lax.scan(body, init, (per_example, given["loss_target"]))
    with _jax.named_scope("update"):
        delta_w, new_m, new_v = {}, {}, {}
        for n in TWIN_WEIGHTS:
            delta_w[n], new_m[n], new_v[n] = _adamw(weights[n], grad_w[n], given["m_" + n], given["v_" + n])
    return (loss, grad_x, *[grad_w[n] for n in TWIN_WEIGHTS], *[delta_w[n] for n in TWIN_WEIGHTS],
            *[new_m[n] for n in TWIN_WEIGHTS], *[new_v[n] for n in TWIN_WEIGHTS])
```

```python
import functools
import math

import jax
import jax.numpy as jnp
from jax import lax
from jax.experimental import pallas as pl
from jax.experimental.pallas import tpu as pltpu

F32 = jnp.float32
BF16 = jnp.bfloat16
MESH = pl.DeviceIdType.MESH

N_DEV = 8
LANES = 128
SUBLANES = 8
VMEM_LIMIT = 48 * 1024 * 1024
VMEM_LIMIT_SCAN = 60 * 1024 * 1024

HEAD_DIM = 128
BLK = 128
DILATIONS = (1, 4, 16)
SSM_GROUP = 16
SSM_STATE = 64
SLAB_GROUPS = LANES // SSM_GROUP
SLAB_STATES = SLAB_GROUPS * SSM_STATE
SEGMENTS = SUBLANES
RMS_EPS = 1e-6
NEG_INF = -1e30

ADAM_LR = 0.001
ADAM_B1 = 0.9
ADAM_B2 = 0.999
ADAM_EPS = 1e-08
ADAM_WD = 0.01
ADAM_STEP = 10


def _tile(n, pref, unit=LANES):
    if n <= pref:
        return n
    t = (pref // unit) * unit
    while t > unit and n % t:
        t -= unit
    assert n % t == 0, (n, pref, unit)
    return t


def _params(sem=None, vmem=VMEM_LIMIT):
    return pltpu.CompilerParams(dimension_semantics=sem, vmem_limit_bytes=vmem)


_NN = (((1,), (0,)), ((), ()))
_NT = (((1,), (1,)), ((), ()))
_TN = (((0,), (0,)), ((), ()))


def _mm_call(dims, nk, n_extra, n_out, epi, **kw):
    def body(*refs):
        a_ref, b_ref = refs[0], refs[1]
        extra = refs[2:2 + n_extra]
        outs = refs[2 + n_extra:2 + n_extra + n_out]
        acc = refs[-1]
        k = pl.program_id(2)

        @pl.when(k == 0)
        def _():
            acc[...] = jnp.zeros_like(acc)

        acc[...] += lax.dot_general(a_ref[...], b_ref[...], dims, preferred_element_type=F32)

        @pl.when(k == nk - 1)
        def _():
            res = epi(acc[...], *[e[...] for e in extra])
            for o, r in zip(outs, res):
                o[...] = r.astype(o.dtype)

    return pl.pallas_call(body, **kw)


def _identity_epi(acc):
    return (acc,)


def mm_nn(a, w, out_dtypes, *, name, epi=_identity_epi, bias=None, tm=512, tn=512, tk=2048):
    M, K = a.shape
    J, K2, n = w.shape
    assert K == K2
    tm, tn, tk = _tile(M, tm, 16), _tile(n, tn), _tile(K, tk)
    npj = n // tn
    nk = K // tk
    in_specs = [pl.BlockSpec((tm, tk), lambda i, j, k: (i, k)),
                pl.BlockSpec((None, tk, tn), lambda i, j, k: (j // npj, k, j % npj))]
    args = [a, w]
    if bias is not None:
        in_specs.append(pl.BlockSpec((1, tn), lambda i, j, k: (0, j)))
        args.append(bias)
    return _mm_call(
        _NN, nk, len(args) - 2, len(out_dtypes), epi,
        out_shape=[jax.ShapeDtypeStruct((M, J * n), d) for d in out_dtypes],
        grid=(M // tm, J * npj, nk), in_specs=in_specs,
        out_specs=[pl.BlockSpec((tm, tn), lambda i, j, k: (i, j)) for _ in out_dtypes],
        scratch_shapes=[pltpu.VMEM((tm, tn), F32)],
        compiler_params=_params(("parallel", "parallel", "arbitrary")), name=name)(*args)


def mm_nt(a, w, out_dtype, *, name, epi=_identity_epi, extra=None, tm=512, tko=512, tnr=2048):
    M, N = a.shape
    J, K, n = w.shape
    assert N == J * n
    tm, tko, tnr = _tile(M, tm, 16), _tile(K, tko), _tile(n, tnr)
    npj = n // tnr
    nk = N // tnr
    in_specs = [pl.BlockSpec((tm, tnr), lambda i, j, k: (i, k)),
                pl.BlockSpec((None, tko, tnr), lambda i, j, k: (k // npj, j, k % npj))]
    args = [a, w]
    if extra is not None:
        in_specs.append(pl.BlockSpec((tm, tko), lambda i, j, k: (i, j)))
        args.append(extra)
    return _mm_call(
        _NT, nk, len(args) - 2, 1, epi,
        out_shape=[jax.ShapeDtypeStruct((M, K), out_dtype)],
        grid=(M // tm, K // tko, nk), in_specs=in_specs,
        out_specs=[pl.BlockSpec((tm, tko), lambda i, j, k: (i, j))],
        scratch_shapes=[pltpu.VMEM((tm, tko), F32)],
        compiler_params=_params(("parallel", "parallel", "arbitrary")), name=name)(*args)[0]


def mm_tn(a, b, J, *, name, tko=1024, tn=1024, ts=512):
    S, K = a.shape
    S2, N = b.shape
    assert S == S2 and N % J == 0
    n = N // J
    tko, tn, ts = _tile(K, tko), _tile(n, tn), _tile(S, ts)
    npj = n // tn
    nk = S // ts
    return _mm_call(
        _TN, nk, 0, 1, _identity_epi,
        out_shape=[jax.ShapeDtypeStruct((J, K, n), BF16)],
        grid=(K // tko, J * npj, nk),
        in_specs=[pl.BlockSpec((ts, tko), lambda i, j, k: (k, i)),
                  pl.BlockSpec((ts, tn), lambda i, j, k: (k, j))],
        out_specs=[pl.BlockSpec((None, tko, tn), lambda i, j, k: (j // npj, i, j % npj))],
        scratch_shapes=[pltpu.VMEM((tko, tn), F32)],
        compiler_params=_params(("parallel", "parallel", "arbitrary")), name=name)(a, b)[0]


def rowwise(fn, rows, vecs, outs, accs=(), *, name, ts=256):
    rows = [r if isinstance(r, tuple) else (r, r.shape[1], 0) for r in rows]
    S = rows[0][0].shape[0]
    ts = _tile(S, ts, 16)
    nr, nv, no = len(rows), len(vecs), len(outs)

    def body(*refs):
        r, v = refs[:nr], refs[nr:nr + nv]
        o, a = refs[nr + nv:nr + nv + no], refs[nr + nv + no:]
        res = fn(*[t[...] for t in r], *[t[...] for t in v])
        for ref, val in zip(o, res[:no]):
            ref[...] = val.astype(ref.dtype)
        if a:
            @pl.when(pl.program_id(0) == 0)
            def _():
                for ref in a:
                    ref[...] = jnp.zeros_like(ref)

            for ref, val in zip(a, res[no:]):
                ref[...] += val

    in_specs = [pl.BlockSpec((ts, w), functools.partial(lambda i, cb: (i, cb), cb=cb)) for _, w, cb in rows]
    in_specs += [pl.BlockSpec(v.shape, lambda i: (0, 0)) for v in vecs]
    out_shape = [jax.ShapeDtypeStruct((S, w), d) for w, d in outs]
    out_shape += [jax.ShapeDtypeStruct((1, w), F32) for w in accs]
    out_specs = [pl.BlockSpec((ts, w), lambda i: (i, 0)) for w, _ in outs]
    out_specs += [pl.BlockSpec((1, w), lambda i: (0, 0)) for w in accs]
    return pl.pallas_call(body, out_shape=out_shape, grid=(S // ts,), in_specs=in_specs, out_specs=out_specs,
                          compiler_params=_params(("arbitrary",)), name=name)(*[r[0] for r in rows], *vecs)


def _rms(x, g):
    r = lax.rsqrt(jnp.mean(x * x, axis=-1, keepdims=True) + RMS_EPS)
    return x * r * g


def _rms_bwd(dy, x, g):
    r = lax.rsqrt(jnp.mean(x * x, axis=-1, keepdims=True) + RMS_EPS)
    xh = x * r
    dxh = dy * g
    dx = r * (dxh - xh * jnp.mean(dxh * xh, axis=-1, keepdims=True))
    return dx, jnp.sum(dy * xh, axis=0, keepdims=True)


def _sigmoid(x):
    return 1.0 / (1.0 + jnp.exp(-x))


_GELU_C = math.sqrt(2.0 / math.pi)


def _gelu(x):
    return 0.5 * x * (1.0 + jnp.tanh(_GELU_C * (x + 0.044715 * x * x * x)))


def _gelu_grad(x):
    t = jnp.tanh(_GELU_C * (x + 0.044715 * x * x * x))
    return 0.5 * (1.0 + t) + 0.5 * x * (1.0 - t * t) * _GELU_C * (1.0 + 3.0 * 0.044715 * x * x)


def _masks(n):
    ii = lax.broadcasted_iota(jnp.int32, (BLK, BLK), 0)
    jj = lax.broadcasted_iota(jnp.int32, (BLK, BLK), 1)
    return (jj >= ii) & (n > 0), jj <= ii


def attn_fwd(proj, d, n_heads, *, name):
    S, WP = proj.shape
    M = S // d
    assert S % d == 0 and M % BLK == 0
    nb = M // BLK
    AW = n_heads * HEAD_DIM
    cpp = WP // HEAD_DIM
    scale = 1.0 / math.sqrt(HEAD_DIM)
    view = proj.reshape(M, d * WP)

    def body(q_ref, k_ref, v_ref, o_ref, l_ref):
        def step(n, carry):
            r0 = pl.multiple_of(n * BLK, BLK)
            rp = pl.multiple_of(jnp.maximum(n - 1, 0) * BLK, BLK)
            m_prev, m_cur = _masks(n)
            q = q_ref[pl.ds(r0, BLK), :]
            s_c = lax.dot_general(q, k_ref[pl.ds(r0, BLK), :], _NT, preferred_element_type=F32) * scale
            s_p = lax.dot_general(q, k_ref[pl.ds(rp, BLK), :], _NT, preferred_element_type=F32) * scale
            s_c = jnp.where(m_cur, s_c, NEG_INF)
            s_p = jnp.where(m_prev, s_p, NEG_INF)
            m = jnp.maximum(jnp.max(s_c, axis=-1, keepdims=True), jnp.max(s_p, axis=-1, keepdims=True))
            p_c = jnp.exp(s_c - m)
            p_p = jnp.exp(s_p - m)
            l = jnp.sum(p_c, axis=-1, keepdims=True) + jnp.sum(p_p, axis=-1, keepdims=True)
            o = (jnp.dot(p_c.astype(BF16), v_ref[pl.ds(r0, BLK), :], preferred_element_type=F32)
                 + jnp.dot(p_p.astype(BF16), v_ref[pl.ds(rp, BLK), :], preferred_element_type=F32))
            o_ref[pl.ds(r0, BLK), :] = o / l
            l_ref[pl.ds(r0, BLK), :] = jnp.broadcast_to(m + jnp.log(l), (BLK, HEAD_DIM))
            return carry

        lax.fori_loop(0, nb, step, 0)

    def col(off):
        return pl.BlockSpec((M, HEAD_DIM), lambda r, h: (0, r * cpp + off + h))

    ospec = pl.BlockSpec((M, HEAD_DIM), lambda r, h: (0, r * n_heads + h))
    o, l = pl.pallas_call(
        body, out_shape=[jax.ShapeDtypeStruct((M, d * AW), F32)] * 2, grid=(d, n_heads),
        in_specs=[col(0), col(n_heads), col(2 * n_heads)], out_specs=[ospec, ospec],
        compiler_params=_params(("parallel", "parallel")), name=name)(view, view, view)
    return o.reshape(S, AW), l.reshape(S, AW)


def attn_bwd(proj, do, lse, delta, d, n_heads, *, name):
    S, WP = proj.shape
    M = S // d
    nb = M // BLK
    AW = n_heads * HEAD_DIM
    cpp = WP // HEAD_DIM
    scale = 1.0 / math.sqrt(HEAD_DIM)
    view = proj.reshape(M, d * WP)

    def body(q_ref, k_ref, v_ref, do_ref, l_ref, dl_ref, dq_ref, dk_ref, dv_ref):
        dk_ref[...] = jnp.zeros_like(dk_ref)
        dv_ref[...] = jnp.zeros_like(dv_ref)

        def step(n, carry):
            r0 = pl.multiple_of(n * BLK, BLK)
            rp = pl.multiple_of(jnp.maximum(n - 1, 0) * BLK, BLK)
            q = q_ref[pl.ds(r0, BLK), :]
            g = do_ref[pl.ds(r0, BLK), :]
            lse_n = l_ref[pl.ds(r0, BLK), :][:, :1]
            dl_n = dl_ref[pl.ds(r0, BLK), :][:, :1]
            dq = jnp.zeros((BLK, HEAD_DIM), F32)
            for rows, mask in zip((rp, r0), _masks(n)):
                kb = k_ref[pl.ds(rows, BLK), :]
                vb = v_ref[pl.ds(rows, BLK), :]
                s = lax.dot_general(q, kb, _NT, preferred_element_type=F32) * scale
                p = jnp.where(mask, jnp.exp(s - lse_n), 0.0)
                dp = lax.dot_general(g, vb, _NT, preferred_element_type=F32)
                ds = (p * (dp - dl_n) * scale).astype(BF16)
                dq = dq + jnp.dot(ds, kb, preferred_element_type=F32)
                dk_ref[pl.ds(rows, BLK), :] += lax.dot_general(ds, q, _TN, preferred_element_type=F32)
                dv_ref[pl.ds(rows, BLK), :] += lax.dot_general(p.astype(BF16), g, _TN, preferred_element_type=F32)
            dq_ref[pl.ds(r0, BLK), :] = dq
            return carry

        lax.fori_loop(0, nb, step, 0)

    def col(off):
        return pl.BlockSpec((M, HEAD_DIM), lambda r, h: (0, r * cpp + off + h))

    ospec = pl.BlockSpec((M, HEAD_DIM), lambda r, h: (0, r * n_heads + h))
    va = lambda t: t.reshape(M, d * AW)
    dq, dk, dv = pl.pallas_call(
        body, out_shape=[jax.ShapeDtypeStruct((M, d * AW), F32)] * 3, grid=(d, n_heads),
        in_specs=[col(0), col(n_heads), col(2 * n_heads), ospec, ospec, ospec], out_specs=[ospec] * 3,
        compiler_params=_params(("parallel", "parallel")), name=name)(view, view, view, va(do), va(lse), va(delta))
    return dq.reshape(S, AW), dk.reshape(S, AW), dv.reshape(S, AW)


def _to_segments(t):
    S, W = t.shape
    return t.reshape(SEGMENTS, S // SEGMENTS, W).swapaxes(0, 1).reshape(S, W)


def _from_segments(t):
    S, W = t.shape
    return t.reshape(S // SEGMENTS, SEGMENTS, W).swapaxes(0, 1).reshape(S, W)


def _cmul(ar, ai, br, bi):
    return ar * br - ai * bi, ar * bi + ai * br


def _power(ar, ai, log2n):
    for _ in range(log2n):
        ar, ai = _cmul(ar, ai, ar, ai)
    return ar, ai


def _shift_rows(x, up):
    row = lax.broadcasted_iota(jnp.int32, x.shape, 0)
    if up:
        return jnp.where(row == SEGMENTS - 1, 0.0, pltpu.roll(x, SEGMENTS - 1, 0))
    return jnp.where(row == 0, 0.0, pltpu.roll(x, 1, 0))


def _segment_carries(er, ei, pr, pi, up):
    cr = jnp.zeros_like(er)
    ci = jnp.zeros_like(ei)
    for _ in range(SEGMENTS - 1):
        tr, ti = _cmul(pr, pi, cr, ci)
        cr, ci = _shift_rows(er + tr, up), _shift_rows(ei + ti, up)
    return cr, ci


def _scan_states(sr, si, ar, ai, T, reverse):
    ns = sr.shape[1]
    ar8 = jnp.broadcast_to(ar, (SEGMENTS, ns))
    ai8 = jnp.broadcast_to(ai, (SEGMENTS, ns))

    def rows(t):
        k = (T - 1 - t) if reverse else t
        return pl.ds(pl.multiple_of(k * SEGMENTS, SEGMENTS), SEGMENTS)

    def advance(t, c):
        tr, ti = _cmul(ar8, ai8, c[0], c[1])
        return tr + sr[rows(t), :], ti + si[rows(t), :]

    zero = jnp.zeros((SEGMENTS, ns), F32)
    er, ei = lax.fori_loop(0, T, advance, (zero, zero), unroll=4)
    pr, pi = _power(ar, ai, T.bit_length() - 1)
    cr, ci = _segment_carries(er, ei, jnp.broadcast_to(pr, (SEGMENTS, ns)), jnp.broadcast_to(pi, (SEGMENTS, ns)), reverse)

    def store(t, c):
        nr, ni = advance(t, c)
        sr[rows(t), :] = nr
        si[rows(t), :] = ni
        return nr, ni

    lax.fori_loop(0, T, store, (cr, ci), unroll=4)
    return cr, ci


def _slab_specs(ns):
    return [pl.BlockSpec((None, LANES, ns), lambda g: (g, 0, 0)),
            pl.BlockSpec((None, LANES, ns), lambda g: (g, 0, 0)),
            pl.BlockSpec((None, 1, ns), lambda g: (g, 0, 0)),
            pl.BlockSpec((None, 1, ns), lambda g: (g, 0, 0)),
            pl.BlockSpec((None, ns, LANES), lambda g: (g, 0, 0)),
            pl.BlockSpec((None, ns, LANES), lambda g: (g, 0, 0)),
            pl.BlockSpec((1, LANES), lambda g: (0, g))]


def _chunks(S):
    rc = _tile(S, 512, 16)
    return rc, S // rc


def ssm_fwd(u, bbr, bbi, ar, ai, cbr, cbi, dsk, *, name):
    S, SW = u.shape
    nslab, _, ns = bbr.shape
    T = S // SEGMENTS
    assert T & (T - 1) == 0
    rc, nc = _chunks(S)

    def body(u_ref, br_ref, bi_ref, ar_ref, ai_ref, cr_ref, ci_ref, d_ref, y_ref, sr, si):
        def inputs(c, carry):
            r = pl.ds(pl.multiple_of(c * rc, rc), rc)
            sr[r, :] = jnp.dot(u_ref[r, :], br_ref[...], preferred_element_type=F32)
            si[r, :] = jnp.dot(u_ref[r, :], bi_ref[...], preferred_element_type=F32)
            return carry

        lax.fori_loop(0, nc, inputs, 0)
        _scan_states(sr, si, ar_ref[...], ai_ref[...], T, False)

        def outputs(c, carry):
            r = pl.ds(pl.multiple_of(c * rc, rc), rc)
            y_ref[r, :] = (jnp.dot(sr[r, :].astype(BF16), cr_ref[...], preferred_element_type=F32)
                           - jnp.dot(si[r, :].astype(BF16), ci_ref[...], preferred_element_type=F32)
                           + d_ref[...] * u_ref[r, :].astype(F32))
            return carry

        lax.fori_loop(0, nc, outputs, 0)

    slab = pl.BlockSpec((S, LANES), lambda g: (0, g))
    return pl.pallas_call(
        body, out_shape=jax.ShapeDtypeStruct((S, SW), F32), grid=(nslab,),
        in_specs=[slab] + _slab_specs(ns), out_specs=slab,
        scratch_shapes=[pltpu.VMEM((S, ns), F32)] * 2,
        compiler_params=_params(("parallel",), VMEM_LIMIT_SCAN), name=name)(u, bbr, bbi, ar, ai, cbr, cbi, dsk)


def ssm_bwd(u, dy, bbr, bbi, ar, ai, cbr, cbi, dsk, *, name):
    S, SW = u.shape
    nslab, _, ns = bbr.shape
    T = S // SEGMENTS
    rc, nc = _chunks(S)

    def body(u_ref, dy_ref, br_ref, bi_ref, ar_ref, ai_ref, cr_ref, ci_ref, d_ref,
             du_ref, dbr_ref, dbi_ref, dcr_ref, dci_ref, dar_ref, dai_ref, sr, si, lr, li):
        def inputs(c, carry):
            r = pl.ds(pl.multiple_of(c * rc, rc), rc)
            ub = u_ref[r, :]
            gb = dy_ref[r, :].astype(BF16)
            sr[r, :] = jnp.dot(ub, br_ref[...], preferred_element_type=F32)
            si[r, :] = jnp.dot(ub, bi_ref[...], preferred_element_type=F32)
            lr[r, :] = lax.dot_general(gb, cr_ref[...], _NT, preferred_element_type=F32)
            li[r, :] = -lax.dot_general(gb, ci_ref[...], _NT, preferred_element_type=F32)
            return carry

        lax.fori_loop(0, nc, inputs, 0)
        ar, ai = ar_ref[...], ai_ref[...]
        s0r, s0i = _scan_states(sr, si, ar, ai, T, False)
        _scan_states(lr, li, ar, -ai, T, True)

        def pair(k, c):
            now = pl.ds(pl.multiple_of(k * SEGMENTS, SEGMENTS), SEGMENTS)
            prev = pl.ds(pl.multiple_of((k - 1) * SEGMENTS, SEGMENTS), SEGMENTS)
            return (c[0] + lr[now, :] * sr[prev, :] + li[now, :] * si[prev, :],
                    c[1] - lr[now, :] * si[prev, :] + li[now, :] * sr[prev, :])

        first = pl.ds(0, SEGMENTS)
        acc = (lr[first, :] * s0r + li[first, :] * s0i, -lr[first, :] * s0i + li[first, :] * s0r)
        acc = lax.fori_loop(1, T, pair, acc, unroll=4)
        dar_ref[...] = jnp.sum(acc[0], axis=0, keepdims=True)
        dai_ref[...] = jnp.sum(acc[1], axis=0, keepdims=True)

        dbr_ref[...] = jnp.zeros_like(dbr_ref)
        dbi_ref[...] = jnp.zeros_like(dbi_ref)
        dcr_ref[...] = jnp.zeros_like(dcr_ref)
        dci_ref[...] = jnp.zeros_like(dci_ref)

        def outputs(c, carry):
            r = pl.ds(pl.multiple_of(c * rc, rc), rc)
            ub = u_ref[r, :]
            g = dy_ref[r, :]
            gb = g.astype(BF16)
            lrb = lr[r, :].astype(BF16)
            lib = li[r, :].astype(BF16)
            du_ref[r, :] = (lax.dot_general(lrb, br_ref[...], _NT, preferred_element_type=F32)
                            + lax.dot_general(lib, bi_ref[...], _NT, preferred_element_type=F32)
                            + d_ref[...] * g).astype(BF16)
            dbr_ref[...] += lax.dot_general(ub, lrb, _TN, preferred_element_type=F32)
            dbi_ref[...] += lax.dot_general(ub, lib, _TN, preferred_element_type=F32)
            dcr_ref[...] += lax.dot_general(sr[r, :].astype(BF16), gb, _TN, preferred_element_type=F32)
            dci_ref[...] -= lax.dot_general(si[r, :].astype(BF16), gb, _TN, preferred_element_type=F32)
            return carry

        lax.fori_loop(0, nc, outputs, 0)

    slab = pl.BlockSpec((S, LANES), lambda g: (0, g))
    bspec = pl.BlockSpec((None, LANES, ns), lambda g: (g, 0, 0))
    cspec = pl.BlockSpec((None, ns, LANES), lambda g: (g, 0, 0))
    aspec = pl.BlockSpec((None, 1, ns), lambda g: (g, 0, 0))
    return pl.pallas_call(
        body,
        out_shape=[jax.ShapeDtypeStruct((S, SW), BF16),
                   jax.ShapeDtypeStruct((nslab, LANES, ns), F32), jax.ShapeDtypeStruct((nslab, LANES, ns), F32),
                   jax.ShapeDtypeStruct((nslab, ns, LANES), F32), jax.ShapeDtypeStruct((nslab, ns, LANES), F32),
                   jax.ShapeDtypeStruct((nslab, 1, ns), F32), jax.ShapeDtypeStruct((nslab, 1, ns), F32)],
        grid=(nslab,), in_specs=[slab, slab] + _slab_specs(ns),
        out_specs=[slab, bspec, bspec, cspec, cspec, aspec, aspec],
        scratch_shapes=[pltpu.VMEM((S, ns), F32)] * 4,
        compiler_params=_params(("parallel",), VMEM_LIMIT_SCAN), name=name)(u, dy, bbr, bbi, ar, ai, cbr, cbi, dsk)


def _discretise(lam_re, lam_im, log_dt, b_re, b_im):
    dt = jnp.exp(log_dt)[:, None]
    mag = jnp.exp(lam_re * dt)
    ar = mag * jnp.cos(lam_im * dt)
    ai = mag * jnp.sin(lam_im * dt)
    nr, ni = ar - 1.0, ai
    den = lam_re * lam_re + lam_im * lam_im
    cr = ((nr * lam_re + ni * lam_im) / den)[..., None]
    ci = ((ni * lam_re - nr * lam_im) / den)[..., None]
    return ar, ai, cr * b_re - ci * b_im, cr * b_im + ci * b_re


def _block_diag(t, nslab):
    G, R, C = t.shape
    eye = jnp.eye(SLAB_GROUPS, dtype=t.dtype)
    t = t.reshape(nslab, SLAB_GROUPS, R, C)
    return jnp.einsum('sgrc,gh->sgrhc', t, eye).reshape(nslab, SLAB_GROUPS * R, SLAB_GROUPS * C)


def _block_diag_part(t, R, C):
    nslab = t.shape[0]
    eye = jnp.eye(SLAB_GROUPS, dtype=t.dtype)
    t = t.reshape(nslab, SLAB_GROUPS, R, SLAB_GROUPS, C)
    return jnp.einsum('sgrhc,gh->sgrc', t, eye).reshape(nslab * SLAB_GROUPS, R, C)


def _place():
    return lax.axis_index("x"), lax.axis_index("y"), lax.axis_index("c")


def all_gather(shards, *, name):
    nw = len(shards)

    def body(*refs):
        ins, outs = refs[:nw], refs[nw:2 * nw]
        send_sems, recv_sems, local_sems = refs[2 * nw:]
        x, y, c = _place()
        me, sibling = (x, y, c), (x, y, 1 - c)
        chips = [(1 - x, y), (x, 1 - y), (1 - x, 1 - y)]

        def copy(w, k, block, to, own):
            px, py, pc = block
            slot = outs[w].at[4 * px + 2 * py + pc]
            return pltpu.make_async_remote_copy(
                src_ref=ins[w] if own else slot, dst_ref=slot, send_sem=send_sems.at[w, k],
                recv_sem=recv_sems.at[w, k], device_id=to, device_id_type=MESH)

        mine = [pltpu.make_async_copy(ins[w], outs[w].at[4 * x + 2 * y + c], local_sems.at[w]) for w in range(nw)]
        for cp in mine:
            cp.start()
        first = []
        for w in range(nw):
            first.append(copy(w, 0, me, sibling, True))
            first += [copy(w, 1 + j, me, (*chip, c), True) for j, chip in enumerate(chips)]
        for cp in first:
            cp.start()
        passed = []
        for j, chip in enumerate(chips):
            for w in range(nw):
                copy(w, 1 + j, (*chip, c), me, False).wait_recv()
                cp = copy(w, 4 + j, (*chip, c), sibling, False)
                cp.start()
                passed.append(cp)
        for w in range(nw):
            copy(w, 0, sibling, me, False).wait_recv()
            for j, chip in enumerate(chips):
                copy(w, 4 + j, (*chip, 1 - c), me, False).wait_recv()
        for cp in first + passed:
            cp.wait_send()
        for cp in mine:
            cp.wait()

    anyspec = pl.BlockSpec(memory_space=pl.ANY)
    return pl.pallas_call(
        body, out_shape=[jax.ShapeDtypeStruct((N_DEV,) + s.shape, s.dtype) for s in shards],
        in_specs=[anyspec] * nw, out_specs=[anyspec] * nw,
        scratch_shapes=[pltpu.SemaphoreType.DMA((nw, 7)), pltpu.SemaphoreType.DMA((nw, 7)),
                        pltpu.SemaphoreType.DMA((nw,))],
        compiler_params=pltpu.CompilerParams(has_side_effects=True), name=name)(*shards)


def exchange_cores(grads, *, name):
    nw = len(grads)

    def body(*refs):
        ins, outs = refs[:nw], refs[nw:2 * nw]
        send_sems, recv_sems = refs[2 * nw:]
        x, y, c = _place()
        copies = [pltpu.make_async_remote_copy(
            src_ref=ins[w].at[:, 1 - c], dst_ref=outs[w], send_sem=send_sems.at[w], recv_sem=recv_sems.at[w],
            device_id=(x, y, 1 - c), device_id_type=MESH) for w in range(nw)]
        for cp in copies:
            cp.start()
        for cp in copies:
            cp.wait()

    anyspec = pl.BlockSpec(memory_space=pl.ANY)
    return pl.pallas_call(
        body, out_shape=[jax.ShapeDtypeStruct((4,) + g.shape[2:], g.dtype) for g in grads],
        in_specs=[anyspec] * nw, out_specs=[anyspec] * nw,
        scratch_shapes=[pltpu.SemaphoreType.DMA((nw,)), pltpu.SemaphoreType.DMA((nw,))],
        compiler_params=pltpu.CompilerParams(has_side_effects=True), name=name)(*grads)


def exchange_chips(parts, *, name):
    nw = len(parts)

    def body(*refs):
        ins, outs = refs[:nw], refs[nw:2 * nw]
        send_sems, recv_sems = refs[2 * nw:]
        x, y, c = _place()
        chips = [(1 - x, y), (x, 1 - y), (1 - x, 1 - y)]
        copies = [pltpu.make_async_remote_copy(
            src_ref=ins[w].at[2 * cx + cy], dst_ref=outs[w].at[j], send_sem=send_sems.at[w, j],
            recv_sem=recv_sems.at[w, j], device_id=(cx, cy, c), device_id_type=MESH)
            for w in range(nw) for j, (cx, cy) in enumerate(chips)]
        for cp in copies:
            cp.start()
        for cp in copies:
            cp.wait()

    anyspec = pl.BlockSpec(memory_space=pl.ANY)
    return pl.pallas_call(
        body, out_shape=[jax.ShapeDtypeStruct((3,) + p.shape[1:], p.dtype) for p in parts],
        in_specs=[anyspec] * nw, out_specs=[anyspec] * nw,
        scratch_shapes=[pltpu.SemaphoreType.DMA((nw, 3)), pltpu.SemaphoreType.DMA((nw, 3))],
        compiler_params=pltpu.CompilerParams(has_side_effects=True), name=name)(*parts)


def _blocked(fn, ins, outs, *, name, tr=128):
    k, n = outs[0][0]
    tr = _tile(k, tr, 16)
    specs = []
    args = []
    for a in ins:
        if isinstance(a, tuple):
            arr, lead = a
            specs.append(pl.BlockSpec((None, tr, n), functools.partial(lambda i, lead: (lead, i, 0), lead=lead)))
            args.append(arr)
        else:
            specs.append(pl.BlockSpec((tr, n), lambda i: (i, 0)))
            args.append(a)
    nin = len(args)

    def body(*refs):
        res = fn(*[r[...] for r in refs[:nin]])
        for ref, val in zip(refs[nin:], res):
            ref[...] = val.astype(ref.dtype)

    return pl.pallas_call(
        body, out_shape=[jax.ShapeDtypeStruct(s, d) for s, d in outs], grid=(k // tr,), in_specs=specs,
        out_specs=[pl.BlockSpec((tr, n), lambda i: (i, 0)) for _ in outs],
        compiler_params=_params(("parallel",)), name=name)(*args)


def _adamw(w, g, m, v):
    m = ADAM_B1 * m + (1.0 - ADAM_B1) * g
    v = ADAM_B2 * v + (1.0 - ADAM_B2) * (g * g)
    m_hat = m / (1.0 - ADAM_B1 ** ADAM_STEP)
    v_hat = v / (1.0 - ADAM_B2 ** ADAM_STEP)
    delta = -ADAM_LR * (m_hat / (jnp.sqrt(v_hat) + ADAM_EPS) + ADAM_WD * w)
    return delta, m, v


def kernel(x, p, mix_norm_pre, w_in, lam_re, lam_im, log_dt, ssm_b_re, ssm_b_im, ssm_c_re, ssm_c_im, ssm_d, w_glu, b_glu, attn_out_norm, ssm_out_norm, w_out, mix_norm_post, mlp_norm_pre, w_up, w_down, mlp_norm_post, ple_norm_pre, w_ple_gate, w_ple_proj, ple_norm_post, loss_target, m_mix_norm_pre, m_w_in, m_lam_re, m_lam_im, m_log_dt, m_ssm_b_re, m_ssm_b_im, m_ssm_c_re, m_ssm_c_im, m_ssm_d, m_w_glu, m_b_glu, m_attn_out_norm, m_ssm_out_norm, m_w_out, m_mix_norm_post, m_mlp_norm_pre, m_w_up, m_w_down, m_mlp_norm_post, m_ple_norm_pre, m_w_ple_gate, m_w_ple_proj, m_ple_norm_post, v_mix_norm_pre, v_w_in, v_lam_re, v_lam_im, v_log_dt, v_ssm_b_re, v_ssm_b_im, v_ssm_c_re, v_ssm_c_im, v_ssm_d, v_w_glu, v_b_glu, v_attn_out_norm, v_ssm_out_norm, v_w_out, v_mix_norm_post, v_mlp_norm_pre, v_w_up, v_w_down, v_mlp_norm_post, v_ple_norm_pre, v_w_ple_gate, v_w_ple_proj, v_ple_norm_post):
    weights = dict(mix_norm_pre=mix_norm_pre, w_in=w_in, lam_re=lam_re, lam_im=lam_im, log_dt=log_dt, ssm_b_re=ssm_b_re, ssm_b_im=ssm_b_im, ssm_c_re=ssm_c_re, ssm_c_im=ssm_c_im, ssm_d=ssm_d, w_glu=w_glu, b_glu=b_glu, attn_out_norm=attn_out_norm, ssm_out_norm=ssm_out_norm, w_out=w_out, mix_norm_post=mix_norm_post, mlp_norm_pre=mlp_norm_pre, w_up=w_up, w_down=w_down, mlp_norm_post=mlp_norm_post, ple_norm_pre=ple_norm_pre, w_ple_gate=w_ple_gate, w_ple_proj=w_ple_proj, ple_norm_post=ple_norm_post)
    mom_m = dict(mix_norm_pre=m_mix_norm_pre, w_in=m_w_in, lam_re=m_lam_re, lam_im=m_lam_im, log_dt=m_log_dt, ssm_b_re=m_ssm_b_re, ssm_b_im=m_ssm_b_im, ssm_c_re=m_ssm_c_re, ssm_c_im=m_ssm_c_im, ssm_d=m_ssm_d, w_glu=m_w_glu, b_glu=m_b_glu, attn_out_norm=m_attn_out_norm, ssm_out_norm=m_ssm_out_norm, w_out=m_w_out, mix_norm_post=m_mix_norm_post, mlp_norm_pre=m_mlp_norm_pre, w_up=m_w_up, w_down=m_w_down, mlp_norm_post=m_mlp_norm_post, ple_norm_pre=m_ple_norm_pre, w_ple_gate=m_w_ple_gate, w_ple_proj=m_w_ple_proj, ple_norm_post=m_ple_norm_post)
    mom_v = dict(mix_norm_pre=v_mix_norm_pre, w_in=v_w_in, lam_re=v_lam_re, lam_im=v_lam_im, log_dt=v_log_dt, ssm_b_re=v_ssm_b_re, ssm_b_im=v_ssm_b_im, ssm_c_re=v_ssm_c_re, ssm_c_im=v_ssm_c_im, ssm_d=v_ssm_d, w_glu=v_w_glu, b_glu=v_b_glu, attn_out_norm=v_attn_out_norm, ssm_out_norm=v_ssm_out_norm, w_out=v_w_out, mix_norm_post=v_mix_norm_post, mlp_norm_pre=v_mlp_norm_pre, w_up=v_w_up, w_down=v_w_down, mlp_norm_post=v_mlp_norm_post, ple_norm_pre=v_ple_norm_pre, w_ple_gate=v_w_ple_gate, w_ple_proj=v_w_ple_proj, ple_norm_post=v_ple_norm_post)
    order = list(weights)
    big = ["w_in", "w_glu", "w_out", "w_up", "w_down", "w_ple_gate", "w_ple_proj"]
    col_sharded = {"w_in", "w_up", "w_ple_proj"}
    small = [n for n in order if n not in big]

    _, S, D = x.shape
    xs = x[0]
    tgt = loss_target[0]
    AW = attn_out_norm.shape[1]
    SW = ssm_d.shape[1]
    H = AW // HEAD_DIM
    G = SW // SSM_GROUP
    nslab = G // SLAB_GROUPS
    P_, C_ = SSM_STATE, SSM_GROUP

    shards = [weights[n][0].astype(BF16) for n in big]
    gathered = all_gather(shards, name="gather_weights")
    W = {}
    for n, g in zip(big, gathered):
        W[n] = g if n in col_sharded else g.reshape(1, N_DEV * g.shape[1], g.shape[2])

    g1, g2, g3, g4, g5, g6 = (weights[n] for n in ("mix_norm_pre", "mix_norm_post", "mlp_norm_pre",
                                                      "mlp_norm_post", "ple_norm_pre", "ple_norm_post"))
    ga, gs = attn_out_norm, ssm_out_norm
    (hn1,) = rowwise(lambda a, g: (_rms(a, g),), [xs], [g1], [(D, BF16)], name="norm_in")
    (proj,) = mm_nn(hn1, W["w_in"], [BF16], name="proj_in")
    WP = proj.shape[1]

    branch = [attn_fwd(proj, d, H, name=f"attn_fwd_d{d}") for d in DILATIONS]

    def combine(o1, o2, o3, l1, l2, l3, g):
        m = jnp.maximum(jnp.maximum(l1, l2), l3)
        e1, e2, e3 = jnp.exp(l1 - m), jnp.exp(l2 - m), jnp.exp(l3 - m)
        den = e1 + e2 + e3
        attn = (e1 * o1 + e2 * o2 + e3 * o3) / den
        return attn, m + jnp.log(den), _rms(attn, g)

    attn, lse, mix_a = rowwise(combine, [b[0] for b in branch] + [b[1] for b in branch], [ga],
                               [(AW, F32), (AW, F32), (AW, BF16)], name="attn_combine")

    a_r, a_i, bb_r, bb_i = _discretise(lam_re[0], lam_im[0], log_dt[0], ssm_b_re[0], ssm_b_im[0])
    ssm_consts = (_block_diag(bb_r.swapaxes(1, 2), nslab).astype(BF16), _block_diag(bb_i.swapaxes(1, 2), nslab).astype(BF16),
                  a_r.reshape(nslab, 1, SLAB_STATES), a_i.reshape(nslab, 1, SLAB_STATES),
                  _block_diag(ssm_c_re[0].swapaxes(1, 2), nslab).astype(BF16),
                  _block_diag(ssm_c_im[0].swapaxes(1, 2), nslab).astype(BF16), ssm_d)
    u_seg = _to_segments(proj[:, 3 * AW:])
    y_pre = ssm_fwd(u_seg, *ssm_consts, name="ssm_fwd")
    (yg,) = rowwise(lambda a: (_gelu(a),), [y_pre], [], [(SW, BF16)], name="ssm_gelu")
    (gl1,) = mm_nn(yg, W["w_glu"], [F32], epi=lambda acc, b: (acc + b,), bias=b_glu, name="glu_gate")
    (mix_s,) = rowwise(lambda yp, gl, g: (_rms(_gelu(yp) * _sigmoid(gl), g),), [y_pre, gl1], [gs], [(SW, BF16)],
                       name="ssm_glu_norm")
    mixed = jnp.concatenate([mix_a, _from_segments(mix_s)], axis=1)
    (mo,) = mm_nn(mixed, W["w_out"], [F32], name="mix_out")

    def resid_norm(h, t, gpost, gpre):
        hh = h + _rms(t, gpost)
        return hh, _rms(hh, gpre)

    h1, hn2 = rowwise(resid_norm, [xs, mo], [g2, g3], [(D, F32), (D, BF16)], name="resid_mix")

    def relu2(acc):
        r = jnp.maximum(acc, 0.0)
        return acc, r * r

    up, act = mm_nn(hn2, W["w_up"], [BF16, BF16], epi=relu2, name="mlp_up")
    (ff,) = mm_nn(act, W["w_down"], [F32], name="mlp_down")
    h2, hn3 = rowwise(resid_norm, [h1, ff], [g4, g5], [(D, F32), (D, BF16)], name="resid_mlp")
    (gl2,) = mm_nn(hn3, W["w_ple_gate"], [F32], name="ple_gate")
    pb = p[0, 0].astype(BF16)
    (emb,) = mm_nn(pb, W["w_ple_proj"], [F32], name="ple_proj")

    def head(h, gl, e, t, g):
        sg = _sigmoid(gl)
        ge = sg * e
        err = h + _rms(ge, g) - t
        dh = err * (1.0 / D)
        dge, dg = _rms_bwd(dh, ge, g)
        return dh, dge * e * sg * (1.0 - sg), dge * sg, jnp.sum(err * err, axis=0, keepdims=True), dg

    dh3, dgl2, demb, loss_part, dg6 = rowwise(head, [h2, gl2, emb, tgt], [g6], [(D, F32), (D, BF16), (D, BF16)],
                                             [D, D], name="ple_loss_head")
    loss = lax.psum(0.5 / D * jnp.sum(loss_part), ("x", "y", "c"))

    grads = {}
    grads["w_ple_proj"] = mm_tn(pb, demb, N_DEV, name="grad_w_ple_proj")
    dhn3 = mm_nt(dgl2, W["w_ple_gate"], F32, name="back_ple_gate")
    grads["w_ple_gate"] = mm_tn(hn3, dgl2, 1, name="grad_w_ple_gate")

    def back_resid(dh, dhn, h, t, gpre, gpost):
        d1, dgpre = _rms_bwd(dhn, h, gpre)
        dhh = dh + d1
        dt, dgpost = _rms_bwd(dhh, t, gpost)
        return dhh, dt, dgpre, dgpost

    dh2, dff, dg5, dg4 = rowwise(back_resid, [dh3, dhn3, h2, ff], [g5, g4], [(D, F32), (D, BF16)], [D, D],
                                 name="back_resid_mlp")
    dup = mm_nt(dff, W["w_down"], BF16, epi=lambda acc, u_: (acc * 2.0 * jnp.maximum(u_.astype(F32), 0.0),),
                extra=up, name="back_mlp_down")
    grads["w_down"] = mm_tn(act, dff, 1, name="grad_w_down")
    dhn2 = mm_nt(dup, W["w_up"], F32, name="back_mlp_up")
    grads["w_up"] = mm_tn(hn2, dup, N_DEV, name="grad_w_up")
    dh1, dmo, dg3, dg2 = rowwise(back_resid, [dh2, dhn2, h1, mo], [g3, g2], [(D, F32), (D, BF16)], [D, D],
                                 name="back_resid_mix")
    dmixed = mm_nt(dmo, W["w_out"], F32, name="back_mix_out")
    grads["w_out"] = mm_tn(mixed, dmo, 1, name="grad_w_out")

    def back_glu(dm, yp, gl, g):
        ygf = _gelu(yp)
        sg = _sigmoid(gl)
        dssm, dg = _rms_bwd(dm, ygf * sg, g)
        dgl = dssm * ygf * sg * (1.0 - sg)
        return dgl, dssm * sg, dg, jnp.sum(dgl, axis=0, keepdims=True)

    dgl1, dyg_direct, dgs, db_glu = rowwise(back_glu, [_to_segments(dmixed[:, AW:]), y_pre, gl1], [gs],
                                            [(SW, BF16), (SW, F32)], [SW, SW], name="back_glu")
    dyg_gate = mm_nt(dgl1, W["w_glu"], F32, name="back_glu_gate")
    grads["w_glu"] = mm_tn(yg, dgl1, 1, name="grad_w_glu")

    def back_gelu(d1, d2, yp, u_):
        dy = (d1 + d2) * _gelu_grad(yp)
        return dy, jnp.sum(dy * u_.astype(F32), axis=0, keepdims=True)

    dy_pre, d_skip = rowwise(back_gelu, [dyg_direct, dyg_gate, y_pre, u_seg], [], [(SW, F32)], [SW], name="back_gelu")
    du_seg, dbb_r, dbb_i, dcb_r, dcb_i, da_r, da_i = ssm_bwd(u_seg, dy_pre, *ssm_consts, name="ssm_bwd")

    def back_attn_norm(dm, a, g):
        da, dg = _rms_bwd(dm, a, g)
        prod = da * a
        delta = jnp.concatenate(
            [jnp.broadcast_to(jnp.sum(prod[:, h * HEAD_DIM:(h + 1) * HEAD_DIM], axis=-1, keepdims=True),
                              (prod.shape[0], HEAD_DIM)) for h in range(H)], axis=1)
        return da, delta, dg

    dattn, delta, dga = rowwise(back_attn_norm, [(dmixed, AW, 0), attn], [ga], [(AW, BF16), (AW, F32)], [AW],
                                name="back_attn_norm")
    dqkv = [attn_bwd(proj, dattn, lse, delta, d, H, name=f"attn_bwd_d{d}") for d in DILATIONS]

    def add3(*t):
        return tuple(t[3 * i] + t[3 * i + 1] + t[3 * i + 2] for i in range(3))

    dq, dk, dv = rowwise(add3, [b[i] for i in range(3) for b in dqkv], [], [(AW, BF16)] * 3, name="sum_branches")
    dproj = jnp.concatenate([dq, dk, dv, _from_segments(du_seg)], axis=1)
    dhn1 = mm_nt(dproj, W["w_in"], F32, name="back_proj_in")
    grads["w_in"] = mm_tn(hn1, dproj, N_DEV, name="grad_w_in")

    def back_in(dh, dhn, a, g):
        d1, dg = _rms_bwd(dhn, a, g)
        return dh + d1, dg

    grad_x, dg1 = rowwise(back_in, [dh1, dhn1, xs], [g1], [(D, F32)], [D], name="back_norm_in")

    x_i, y_i, c_i = _place()
    chunked = []
    for n in big:
        g = grads[n]
        g = g if n in col_sharded else g.reshape(N_DEV, g.shape[1] // N_DEV, g.shape[2])
        chunked.append(g.reshape(4, 2, g.shape[1], g.shape[2]))
    from_sibling = exchange_cores(chunked, name="grads_to_sibling")
    chip_sums = []
    for n, g, r in zip(big, chunked, from_sibling):
        k, nn = g.shape[2], g.shape[3]
        mine = lax.dynamic_index_in_dim(g, c_i, axis=1, keepdims=False).reshape(4 * k, nn)
        (s,) = _blocked(lambda a, b: (a.astype(F32) + b.astype(F32),), [mine, r.reshape(4 * k, nn)],
                        [((4 * k, nn), BF16)], name=f"chip_sum_{n}")
        chip_sums.append(s.reshape(4, k, nn))
    from_chips = exchange_chips(chip_sums, name="grads_to_chips")

    out_g, out_d, out_m, out_v = {}, {}, {}, {}

    def update(w_, m_, v_, own, r0, r1, r2):
        g = own.astype(F32) + r0.astype(F32) + r1.astype(F32) + r2.astype(F32)
        return (g,) + _adamw(w_, g, m_, v_)

    for n, s, r in zip(big, chip_sums, from_chips):
        own = lax.dynamic_index_in_dim(s, 2 * x_i + y_i, axis=0, keepdims=False)
        shp = weights[n].shape
        res = _blocked(update, [weights[n][0], mom_m[n][0], mom_v[n][0], own, (r, 0), (r, 1), (r, 2)],
                       [(shp[1:], F32)] * 4, name=f"adamw_{n}")
        out_g[n], out_d[n], out_m[n], out_v[n] = (t.reshape(shp) for t in res)

    nb = SLAB_GROUPS
    cot = dict(
        mix_norm_pre=dg1, mix_norm_post=dg2, mlp_norm_pre=dg3, mlp_norm_post=dg4, ple_norm_pre=dg5, ple_norm_post=dg6,
        attn_out_norm=dga, ssm_out_norm=dgs, b_glu=db_glu, ssm_d=d_skip,
        ssm_c_re=_block_diag_part(dcb_r, P_, C_).swapaxes(1, 2), ssm_c_im=_block_diag_part(dcb_i, P_, C_).swapaxes(1, 2),
        a_r=da_r.reshape(G, P_), a_i=da_i.reshape(G, P_),
        bb_r=_block_diag_part(dbb_r, C_, P_).swapaxes(1, 2), bb_i=_block_diag_part(dbb_i, C_, P_).swapaxes(1, 2))
    names = list(cot)
    flat = jnp.concatenate([cot[n].reshape(-1) for n in names])
    total = flat.shape[0]
    rows_ = -(-total // (LANES * 16)) * 16
    flat = jnp.pad(flat, (0, rows_ * LANES - total)).reshape(rows_, LANES)
    (every,) = all_gather([flat], name="gather_small_grads")
    (summed,) = _blocked(lambda *t: (functools.reduce(lambda a, b: a + b, t),), [(every, j) for j in range(N_DEV)],
                         [((rows_, LANES), F32)], name="sum_small_grads")
    summed = summed.reshape(-1)
    red, off = {}, 0
    for n in names:
        sz = cot[n].size
        red[n] = summed[off:off + sz].reshape(cot[n].shape)
        off += sz
    _, pull = jax.vjp(_discretise, lam_re[0], lam_im[0], log_dt[0], ssm_b_re[0], ssm_b_im[0])
    d_lre, d_lim, d_ldt, d_bre, d_bim = pull((red["a_r"], red["a_i"], red["bb_r"], red["bb_i"]))
    red.update(lam_re=d_lre, lam_im=d_lim, log_dt=d_ldt, ssm_b_re=d_bre, ssm_b_im=d_bim)

    def pack(d):
        t = jnp.concatenate([d[n].reshape(-1) for n in small])
        r_ = -(-t.shape[0] // (LANES * 16)) * 16
        return jnp.pad(t, (0, r_ * LANES - t.shape[0])).reshape(r_, LANES)

    sw, sg_, sm, sv = pack(weights), pack(red), pack(mom_m), pack(mom_v)
    sd, snm, snv = _blocked(lambda w_, g_, m_, v_: _adamw(w_, g_, m_, v_), [sw, sg_, sm, sv],
                            [(sw.shape, F32)] * 3, name="adamw_small")
    off = 0
    for n in small:
        sz = weights[n].size
        shp = weights[n].shape
        out_g[n] = red[n].reshape(shp)
        out_d[n] = sd.reshape(-1)[off:off + sz].reshape(shp)
        out_m[n] = snm.reshape(-1)[off:off + sz].reshape(shp)
        out_v[n] = snv.reshape(-1)[off:off + sz].reshape(shp)
        off += sz

    return (loss, grad_x[None], *[out_g[n] for n in order], *[out_d[n] for n in order],
            *[out_m[n] for n in order], *[out_v[n] for n in order])
```

```python
import functools
import math

import jax
import jax.numpy as jnp
from jax import lax
from jax.experimental import pallas as pl
from jax.experimental.pallas import tpu as pltpu

F32 = jnp.float32
BF16 = jnp.bfloat16
MESH = pl.DeviceIdType.MESH

N_DEV = 8
LANES = 128
SUBLANES = 8
VMEM_LIMIT = 48 * 1024 * 1024
VMEM_LIMIT_SCAN = 60 * 1024 * 1024

HEAD_DIM = 128
BLK = 128
DILATIONS = (1, 4, 16)
SSM_GROUP = 16
SSM_STATE = 64
SLAB_GROUPS = LANES // SSM_GROUP
SLAB_STATES = SLAB_GROUPS * SSM_STATE
SEGMENTS = SUBLANES
RMS_EPS = 1e-6
NEG_INF = -1e30

ADAM_LR = 0.001
ADAM_B1 = 0.9
ADAM_B2 = 0.999
ADAM_EPS = 1e-08
ADAM_WD = 0.01
ADAM_STEP = 10


def _tile(n, pref, unit=LANES):
    if n <= pref:
        return n
    t = (pref // unit) * unit
    while t > unit and n % t:
        t -= unit
    assert n % t == 0, (n, pref, unit)
    return t


def _params(sem=None, vmem=VMEM_LIMIT):
    return pltpu.CompilerParams(dimension_semantics=sem, vmem_limit_bytes=vmem)


_NN = (((1,), (0,)), ((), ()))
_NT = (((1,), (1,)), ((), ()))
_TN = (((0,), (0,)), ((), ()))


def _mm_call(dims, nk, n_extra, n_out, epi, **kw):
    def single(*refs):
        extra = refs[2:2 + n_extra]
        res = epi(lax.dot_general(refs[0][...], refs[1][...], dims, preferred_element_type=F32),
                  *[e[...] for e in extra])
        for o, r in zip(refs[2 + n_extra:2 + n_extra + n_out], res):
            o[...] = r.astype(o.dtype)

    if nk == 1:
        kw["scratch_shapes"] = []
        return pl.pallas_call(single, **kw)

    def body(*refs):
        a_ref, b_ref = refs[0], refs[1]
        extra = refs[2:2 + n_extra]
        outs = refs[2 + n_extra:2 + n_extra + n_out]
        acc = refs[-1]
        k = pl.program_id(2)

        @pl.when(k == 0)
        def _():
            acc[...] = jnp.zeros_like(acc)

        acc[...] += lax.dot_general(a_ref[...], b_ref[...], dims, preferred_element_type=F32)

        @pl.when(k == nk - 1)
        def _():
            res = epi(acc[...], *[e[...] for e in extra])
            for o, r in zip(outs, res):
                o[...] = r.astype(o.dtype)

    return pl.pallas_call(body, **kw)


def _identity_epi(acc):
    return (acc,)


def mm_nn(a, w, out_dtypes, *, name, epi=_identity_epi, bias=None, tm=512, tn=512, tk=2048):
    M, K = a.shape
    J, K2, n = w.shape
    assert K == K2
    tm, tn, tk = _tile(M, tm, 16), _tile(n, tn), _tile(K, tk)
    npj = n // tn
    nk = K // tk
    in_specs = [pl.BlockSpec((tm, tk), lambda i, j, k: (i, k)),
                pl.BlockSpec((None, tk, tn), lambda i, j, k: (j // npj, k, j % npj))]
    args = [a, w]
    if bias is not None:
        in_specs.append(pl.BlockSpec((1, tn), lambda i, j, k: (0, j)))
        args.append(bias)
    return _mm_call(
        _NN, nk, len(args) - 2, len(out_dtypes), epi,
        out_shape=[jax.ShapeDtypeStruct((M, J * n), d) for d in out_dtypes],
        grid=(M // tm, J * npj, nk), in_specs=in_specs,
        out_specs=[pl.BlockSpec((tm, tn), lambda i, j, k: (i, j)) for _ in out_dtypes],
        scratch_shapes=[pltpu.VMEM((tm, tn), F32)],
        compiler_params=_params(("parallel", "parallel", "arbitrary")), name=name)(*args)


def mm_nt(a, w, out_dtype, *, name, epi=_identity_epi, extra=None, tm=512, tko=512, tnr=2048):
    M, N = a.shape
    J, K, n = w.shape
    assert N == J * n
    tm, tko, tnr = _tile(M, tm, 16), _tile(K, tko), _tile(n, tnr)
    npj = n // tnr
    nk = N // tnr
    in_specs = [pl.BlockSpec((tm, tnr), lambda i, j, k: (i, k)),
                pl.BlockSpec((None, tko, tnr), lambda i, j, k: (k // npj, j, k % npj))]
    args = [a, w]
    if extra is not None:
        in_specs.append(pl.BlockSpec((tm, tko), lambda i, j, k: (i, j)))
        args.append(extra)
    return _mm_call(
        _NT, nk, len(args) - 2, 1, epi,
        out_shape=[jax.ShapeDtypeStruct((M, K), out_dtype)],
        grid=(M // tm, K // tko, nk), in_specs=in_specs,
        out_specs=[pl.BlockSpec((tm, tko), lambda i, j, k: (i, j))],
        scratch_shapes=[pltpu.VMEM((tm, tko), F32)],
        compiler_params=_params(("parallel", "parallel", "arbitrary")), name=name)(*args)[0]


def mm_tn(a, b, J, *, name, tko=1024, tn=1024, ts=1024):
    S, K = a.shape
    S2, N = b.shape
    assert S == S2 and N % J == 0
    n = N // J
    tko, tn, ts = _tile(K, tko), _tile(n, tn), _tile(S, ts)
    npj = n // tn
    nk = S // ts
    return _mm_call(
        _TN, nk, 0, 1, _identity_epi,
        out_shape=[jax.ShapeDtypeStruct((J, K, n), BF16)],
        grid=(K // tko, J * npj, nk),
        in_specs=[pl.BlockSpec((ts, tko), lambda i, j, k: (k, i)),
                  pl.BlockSpec((ts, tn), lambda i, j, k: (k, j))],
        out_specs=[pl.BlockSpec((None, tko, tn), lambda i, j, k: (j // npj, i, j % npj))],
        scratch_shapes=[pltpu.VMEM((tko, tn), F32)],
        compiler_params=_params(("parallel", "parallel", "arbitrary")), name=name)(a, b)[0]


def rowwise(fn, rows, vecs, outs, accs=(), *, name, ts=256):
    rows = [r if isinstance(r, tuple) else (r, r.shape[1], 0) for r in rows]
    S = rows[0][0].shape[0]
    ts = _tile(S, ts, 16)
    nr, nv, no = len(rows), len(vecs), len(outs)

    def body(*refs):
        r, v = refs[:nr], refs[nr:nr + nv]
        o, a = refs[nr + nv:nr + nv + no], refs[nr + nv + no:]
        res = fn(*[t[...] for t in r], *[t[...] for t in v])
        for ref, val in zip(o, res[:no]):
            ref[...] = val.astype(ref.dtype)
        if a:
            @pl.when(pl.program_id(0) == 0)
            def _():
                for ref in a:
                    ref[...] = jnp.zeros_like(ref)

            for ref, val in zip(a, res[no:]):
                ref[...] += val

    in_specs = [pl.BlockSpec((ts, w), functools.partial(lambda i, cb: (i, cb), cb=cb)) for _, w, cb in rows]
    in_specs += [pl.BlockSpec(v.shape, lambda i: (0, 0)) for v in vecs]
    out_shape = [jax.ShapeDtypeStruct((S, w), d) for w, d in outs]
    out_shape += [jax.ShapeDtypeStruct((1, w), F32) for w in accs]
    out_specs = [pl.BlockSpec((ts, w), lambda i: (i, 0)) for w, _ in outs]
    out_specs += [pl.BlockSpec((1, w), lambda i: (0, 0)) for w in accs]
    return pl.pallas_call(body, out_shape=out_shape, grid=(S // ts,), in_specs=in_specs, out_specs=out_specs,
                          compiler_params=_params(("arbitrary",)), name=name)(*[r[0] for r in rows], *vecs)


def _rms(x, g):
    r = lax.rsqrt(jnp.mean(x * x, axis=-1, keepdims=True) + RMS_EPS)
    return x * r * g


def _rms_bwd(dy, x, g):
    r = lax.rsqrt(jnp.mean(x * x, axis=-1, keepdims=True) + RMS_EPS)
    xh = x * r
    dxh = dy * g
    dx = r * (dxh - xh * jnp.mean(dxh * xh, axis=-1, keepdims=True))
    return dx, jnp.sum(dy * xh, axis=0, keepdims=True)


def _sigmoid(x):
    return 1.0 / (1.0 + jnp.exp(-x))


_GELU_C = math.sqrt(2.0 / math.pi)


def _gelu(x):
    return 0.5 * x * (1.0 + jnp.tanh(_GELU_C * (x + 0.044715 * x * x * x)))


def _gelu_grad(x):
    t = jnp.tanh(_GELU_C * (x + 0.044715 * x * x * x))
    return 0.5 * (1.0 + t) + 0.5 * x * (1.0 - t * t) * _GELU_C * (1.0 + 3.0 * 0.044715 * x * x)


def _masks(n):
    ii = lax.broadcasted_iota(jnp.int32, (BLK, BLK), 0)
    jj = lax.broadcasted_iota(jnp.int32, (BLK, BLK), 1)
    return (jj >= ii) & (n > 0), jj <= ii


ATTN_INTERLEAVE = 2


def _block_rows(idx, d, nb):
    r, n = idx // nb, idx % nb
    cur = r + n * (BLK * d)
    prev = r + jnp.maximum(n - 1, 0) * (BLK * d)
    if d == 1:
        return n, pl.ds(pl.multiple_of(cur, BLK), BLK), pl.ds(pl.multiple_of(prev, BLK), BLK)
    return n, pl.ds(cur, BLK, stride=d), pl.ds(prev, BLK, stride=d)


def attn_fwd(proj, n_heads, *, name):
    S, WP = proj.shape
    assert S % (BLK * max(DILATIONS)) == 0
    nblk = S // BLK
    AW = n_heads * HEAD_DIM
    scale = 1.0 / math.sqrt(HEAD_DIM)

    def body(q_ref, k_ref, v_ref, o_ref, l_ref, acc, mrun, lrun):
        for first, d in zip((True, False, False), DILATIONS):
            nb = nblk // d

            def step(it, carry, d=d, nb=nb, first=first):
                blocks = [_block_rows(it + j * (nblk // ATTN_INTERLEAVE), d, nb) for j in range(ATTN_INTERLEAVE)]
                qs = [q_ref[cur, :].astype(BF16) for _, cur, _ in blocks]
                s_c = [lax.dot_general(q, k_ref[cur, :].astype(BF16), _NT, preferred_element_type=F32) * scale
                       for q, (_, cur, _) in zip(qs, blocks)]
                s_p = [lax.dot_general(q, k_ref[prev, :].astype(BF16), _NT, preferred_element_type=F32) * scale
                       for q, (_, _, prev) in zip(qs, blocks)]
                for j, (n, cur, prev) in enumerate(blocks):
                    m_prev, m_cur = _masks(n)
                    sc = jnp.where(m_cur, s_c[j], NEG_INF)
                    sp = jnp.where(m_prev, s_p[j], NEG_INF)
                    m = jnp.maximum(jnp.max(sc, axis=-1, keepdims=True), jnp.max(sp, axis=-1, keepdims=True))
                    p_c = jnp.exp(sc - m)
                    p_p = jnp.exp(sp - m)
                    l = jnp.sum(p_c, axis=-1, keepdims=True) + jnp.sum(p_p, axis=-1, keepdims=True)
                    o = (jnp.dot(p_c.astype(BF16), v_ref[cur, :].astype(BF16), preferred_element_type=F32)
                         + jnp.dot(p_p.astype(BF16), v_ref[prev, :].astype(BF16), preferred_element_type=F32))
                    m = jnp.broadcast_to(m, (BLK, HEAD_DIM))
                    l = jnp.broadcast_to(l, (BLK, HEAD_DIM))
                    if first:
                        acc[cur, :], mrun[cur, :], lrun[cur, :] = o, m, l
                    else:
                        m_old = mrun[cur, :]
                        m_new = jnp.maximum(m_old, m)
                        w_old, w_blk = jnp.exp(m_old - m_new), jnp.exp(m - m_new)
                        acc[cur, :] = w_old * acc[cur, :] + w_blk * o
                        lrun[cur, :] = w_old * lrun[cur, :] + w_blk * l
                        mrun[cur, :] = m_new
                return carry

            lax.fori_loop(0, nblk // ATTN_INTERLEAVE, step, 0)

        def finish(c, carry):
            r = pl.ds(pl.multiple_of(c * BLK, BLK), BLK)
            o_ref[r, :] = acc[r, :] / lrun[r, :]
            l_ref[r, :] = mrun[r, :] + jnp.log(lrun[r, :])
            return carry

        lax.fori_loop(0, nblk, finish, 0)

    def col(off):
        return pl.BlockSpec((S, HEAD_DIM), lambda h: (0, off + h))

    ospec = pl.BlockSpec((S, HEAD_DIM), lambda h: (0, h))
    return pl.pallas_call(
        body, out_shape=[jax.ShapeDtypeStruct((S, AW), F32)] * 2, grid=(n_heads,),
        in_specs=[col(0), col(n_heads), col(2 * n_heads)], out_specs=[ospec, ospec],
        scratch_shapes=[pltpu.VMEM((S, HEAD_DIM), F32)] * 3,
        compiler_params=_params(("parallel",)), name=name)(proj, proj, proj)


def attn_bwd(proj, do, lse, delta, n_heads, *, name):
    S, WP = proj.shape
    nblk = S // BLK
    AW = n_heads * HEAD_DIM
    scale = 1.0 / math.sqrt(HEAD_DIM)

    def body(q_ref, k_ref, v_ref, do_ref, l_ref, dl_ref, dq_ref, dk_ref, dv_ref, dq_sc, dk_sc, dv_sc):
        dq_sc[...] = jnp.zeros_like(dq_sc)
        dk_sc[...] = jnp.zeros_like(dk_sc)
        dv_sc[...] = jnp.zeros_like(dv_sc)
        for d in DILATIONS:
            nb = nblk // d

            def step(it, carry, d=d, nb=nb):
                blocks = [_block_rows(it + j * (nblk // ATTN_INTERLEAVE), d, nb) for j in range(ATTN_INTERLEAVE)]
                for n, cur, prev in blocks:
                    q = q_ref[cur, :].astype(BF16)
                    g = do_ref[cur, :].astype(BF16)
                    lse_n = l_ref[cur, :][:, :1]
                    dl_n = dl_ref[cur, :][:, :1]
                    dq = jnp.zeros((BLK, HEAD_DIM), F32)
                    for rows, mask in zip((prev, cur), _masks(n)):
                        kb = k_ref[rows, :].astype(BF16)
                        vb = v_ref[rows, :].astype(BF16)
                        s = lax.dot_general(q, kb, _NT, preferred_element_type=F32) * scale
                        p = jnp.where(mask, jnp.exp(s - lse_n), 0.0)
                        dp = lax.dot_general(g, vb, _NT, preferred_element_type=F32)
                        ds = (p * (dp - dl_n) * scale).astype(BF16)
                        dq = dq + jnp.dot(ds, kb, preferred_element_type=F32)
                        dk_sc[rows, :] += lax.dot_general(ds, q, _TN, preferred_element_type=F32)
                        dv_sc[rows, :] += lax.dot_general(p.astype(BF16), g, _TN, preferred_element_type=F32)
                    dq_sc[cur, :] += dq
                return carry

            lax.fori_loop(0, nblk // ATTN_INTERLEAVE, step, 0)
        dq_ref[...] = dq_sc[...].astype(BF16)
        dk_ref[...] = dk_sc[...].astype(BF16)
        dv_ref[...] = dv_sc[...].astype(BF16)

    def col(off):
        return pl.BlockSpec((S, HEAD_DIM), lambda h: (0, off + h))

    ospec = pl.BlockSpec((S, HEAD_DIM), lambda h: (0, h))
    return pl.pallas_call(
        body, out_shape=[jax.ShapeDtypeStruct((S, AW), BF16)] * 3, grid=(n_heads,),
        in_specs=[col(0), col(n_heads), col(2 * n_heads), ospec, ospec, ospec], out_specs=[ospec] * 3,
        scratch_shapes=[pltpu.VMEM((S, HEAD_DIM), F32)] * 3,
        compiler_params=_params(("parallel",)), name=name)(proj, proj, proj, do, lse, delta)


def _to_segments(t):
    S, W = t.shape
    return t.reshape(SEGMENTS, S // SEGMENTS, W).swapaxes(0, 1).reshape(S, W)


def _from_segments(t):
    S, W = t.shape
    return t.reshape(S // SEGMENTS, SEGMENTS, W).swapaxes(0, 1).reshape(S, W)


def _cmul(ar, ai, br, bi):
    return ar * br - ai * bi, ar * bi + ai * br


def _power(ar, ai, log2n):
    for _ in range(log2n):
        ar, ai = _cmul(ar, ai, ar, ai)
    return ar, ai


def _shift_rows(x, up):
    row = lax.broadcasted_iota(jnp.int32, x.shape, 0)
    if up:
        return jnp.where(row == SEGMENTS - 1, 0.0, pltpu.roll(x, SEGMENTS - 1, 0))
    return jnp.where(row == 0, 0.0, pltpu.roll(x, 1, 0))


def _segment_carries(er, ei, pr, pi, up):
    cr = jnp.zeros_like(er)
    ci = jnp.zeros_like(ei)
    for _ in range(SEGMENTS - 1):
        tr, ti = _cmul(pr, pi, cr, ci)
        cr, ci = _shift_rows(er + tr, up), _shift_rows(ei + ti, up)
    return cr, ci


def _scan_states(sr, si, ar, ai, T, reverse):
    ns = sr.shape[1]
    ar8 = jnp.broadcast_to(ar, (SEGMENTS, ns))
    ai8 = jnp.broadcast_to(ai, (SEGMENTS, ns))

    def rows(t):
        k = (T - 1 - t) if reverse else t
        return pl.ds(pl.multiple_of(k * SEGMENTS, SEGMENTS), SEGMENTS)

    def advance(t, c):
        tr, ti = _cmul(ar8, ai8, c[0], c[1])
        return tr + sr[rows(t), :], ti + si[rows(t), :]

    zero = jnp.zeros((SEGMENTS, ns), F32)
    er, ei = lax.fori_loop(0, T, advance, (zero, zero), unroll=4)
    pr, pi = _power(ar, ai, T.bit_length() - 1)
    cr, ci = _segment_carries(er, ei, jnp.broadcast_to(pr, (SEGMENTS, ns)), jnp.broadcast_to(pi, (SEGMENTS, ns)), reverse)

    def store(t, c):
        nr, ni = advance(t, c)
        sr[rows(t), :] = nr
        si[rows(t), :] = ni
        return nr, ni

    lax.fori_loop(0, T, store, (cr, ci), unroll=4)
    return cr, ci


def _slab_specs(ns):
    return [pl.BlockSpec((None, LANES, ns), lambda g: (g, 0, 0)),
            pl.BlockSpec((None, LANES, ns), lambda g: (g, 0, 0)),
            pl.BlockSpec((None, 1, ns), lambda g: (g, 0, 0)),
            pl.BlockSpec((None, 1, ns), lambda g: (g, 0, 0)),
            pl.BlockSpec((None, ns, LANES), lambda g: (g, 0, 0)),
            pl.BlockSpec((None, ns, LANES), lambda g: (g, 0, 0)),
            pl.BlockSpec((1, LANES), lambda g: (0, g))]


def _chunks(S):
    rc = _tile(S, 512, 16)
    return rc, S // rc


def ssm_fwd(u, bbr, bbi, ar, ai, cbr, cbi, dsk, *, name):
    S, SW = u.shape
    nslab, _, ns = bbr.shape
    T = S // SEGMENTS
    assert T & (T - 1) == 0
    rc, nc = _chunks(S)

    def body(u_ref, br_ref, bi_ref, ar_ref, ai_ref, cr_ref, ci_ref, d_ref, y_ref, sr, si):
        def inputs(c, carry):
            r = pl.ds(pl.multiple_of(c * rc, rc), rc)
            sr[r, :] = jnp.dot(u_ref[r, :], br_ref[...], preferred_element_type=F32)
            si[r, :] = jnp.dot(u_ref[r, :], bi_ref[...], preferred_element_type=F32)
            return carry

        lax.fori_loop(0, nc, inputs, 0)
        _scan_states(sr, si, ar_ref[...], ai_ref[...], T, False)

        def outputs(c, carry):
            r = pl.ds(pl.multiple_of(c * rc, rc), rc)
            y_ref[r, :] = (jnp.dot(sr[r, :].astype(BF16), cr_ref[...], preferred_element_type=F32)
                           - jnp.dot(si[r, :].astype(BF16), ci_ref[...], preferred_element_type=F32)
                           + d_ref[...] * u_ref[r, :].astype(F32))
            return carry

        lax.fori_loop(0, nc, outputs, 0)

    slab = pl.BlockSpec((S, LANES), lambda g: (0, g))
    return pl.pallas_call(
        body, out_shape=jax.ShapeDtypeStruct((S, SW), F32), grid=(nslab,),
        in_specs=[slab] + _slab_specs(ns), out_specs=slab,
        scratch_shapes=[pltpu.VMEM((S, ns), F32)] * 2,
        compiler_params=_params(("parallel",), VMEM_LIMIT_SCAN), name=name)(u, bbr, bbi, ar, ai, cbr, cbi, dsk)


def ssm_bwd(u, dy, bbr, bbi, ar, ai, cbr, cbi, dsk, *, name):
    S, SW = u.shape
    nslab, _, ns = bbr.shape
    T = S // SEGMENTS
    rc, nc = _chunks(S)

    def body(u_ref, dy_ref, br_ref, bi_ref, ar_ref, ai_ref, cr_ref, ci_ref, d_ref,
             du_ref, dbr_ref, dbi_ref, dcr_ref, dci_ref, dar_ref, dai_ref, sr, si, lr, li):
        def inputs(c, carry):
            r = pl.ds(pl.multiple_of(c * rc, rc), rc)
            ub = u_ref[r, :]
            gb = dy_ref[r, :].astype(BF16)
            sr[r, :] = jnp.dot(ub, br_ref[...], preferred_element_type=F32)
            si[r, :] = jnp.dot(ub, bi_ref[...], preferred_element_type=F32)
            lr[r, :] = lax.dot_general(gb, cr_ref[...], _NT, preferred_element_type=F32)
            li[r, :] = -lax.dot_general(gb, ci_ref[...], _NT, preferred_element_type=F32)
            return carry

        lax.fori_loop(0, nc, inputs, 0)
        ar, ai = ar_ref[...], ai_ref[...]
        s0r, s0i = _scan_states(sr, si, ar, ai, T, False)
        _scan_states(lr, li, ar, -ai, T, True)

        def pair(k, c):
            now = pl.ds(pl.multiple_of(k * SEGMENTS, SEGMENTS), SEGMENTS)
            prev = pl.ds(pl.multiple_of((k - 1) * SEGMENTS, SEGMENTS), SEGMENTS)
            return (c[0] + lr[now, :] * sr[prev, :] + li[now, :] * si[prev, :],
                    c[1] - lr[now, :] * si[prev, :] + li[now, :] * sr[prev, :])

        first = pl.ds(0, SEGMENTS)
        acc = (lr[first, :] * s0r + li[first, :] * s0i, -lr[first, :] * s0i + li[first, :] * s0r)
        acc = lax.fori_loop(1, T, pair, acc, unroll=4)
        dar_ref[...] = jnp.sum(acc[0], axis=0, keepdims=True)
        dai_ref[...] = jnp.sum(acc[1], axis=0, keepdims=True)

        dbr_ref[...] = jnp.zeros_like(dbr_ref)
        dbi_ref[...] = jnp.zeros_like(dbi_ref)
        dcr_ref[...] = jnp.zeros_like(dcr_ref)
        dci_ref[...] = jnp.zeros_like(dci_ref)

        def outputs(c, carry):
            r = pl.ds(pl.multiple_of(c * rc, rc), rc)
            ub = u_ref[r, :]
            g = dy_ref[r, :]
            gb = g.astype(BF16)
            lrb = lr[r, :].astype(BF16)
            lib = li[r, :].astype(BF16)
            du_ref[r, :] = (lax.dot_general(lrb, br_ref[...], _NT, preferred_element_type=F32)
                            + lax.dot_general(lib, bi_ref[...], _NT, preferred_element_type=F32)
                            + d_ref[...] * g).astype(BF16)
            dbr_ref[...] += lax.dot_general(ub, lrb, _TN, preferred_element_type=F32)
            dbi_ref[...] += lax.dot_general(ub, lib, _TN, preferred_element_type=F32)
            dcr_ref[...] += lax.dot_general(sr[r, :].astype(BF16), gb, _TN, preferred_element_type=F32)
            dci_ref[...] -= lax.dot_general(si[r, :].astype(BF16), gb, _TN, preferred_element_type=F32)
            return carry

        lax.fori_loop(0, nc, outputs, 0)

    slab = pl.BlockSpec((S, LANES), lambda g: (0, g))
    bspec = pl.BlockSpec((None, LANES, ns), lambda g: (g, 0, 0))
    cspec = pl.BlockSpec((None, ns, LANES), lambda g: (g, 0, 0))
    aspec = pl.BlockSpec((None, 1, ns), lambda g: (g, 0, 0))
    return pl.pallas_call(
        body,
        out_shape=[jax.ShapeDtypeStruct((S, SW), BF16),
                   jax.ShapeDtypeStruct((nslab, LANES, ns), F32), jax.ShapeDtypeStruct((nslab, LANES, ns), F32),
                   jax.ShapeDtypeStruct((nslab, ns, LANES), F32), jax.ShapeDtypeStruct((nslab, ns, LANES), F32),
                   jax.ShapeDtypeStruct((nslab, 1, ns), F32), jax.ShapeDtypeStruct((nslab, 1, ns), F32)],
        grid=(nslab,), in_specs=[slab, slab] + _slab_specs(ns),
        out_specs=[slab, bspec, bspec, cspec, cspec, aspec, aspec],
        scratch_shapes=[pltpu.VMEM((S, ns), F32)] * 4,
        compiler_params=_params(("parallel",), VMEM_LIMIT_SCAN), name=name)(u, dy, bbr, bbi, ar, ai, cbr, cbi, dsk)


def _discretise(lam_re, lam_im, log_dt, b_re, b_im):
    dt = jnp.exp(log_dt)[:, None]
    mag = jnp.exp(lam_re * dt)
    ar = mag * jnp.cos(lam_im * dt)
    ai = mag * jnp.sin(lam_im * dt)
    nr, ni = ar - 1.0, ai
    den = lam_re * lam_re + lam_im * lam_im
    cr = ((nr * lam_re + ni * lam_im) / den)[..., None]
    ci = ((ni * lam_re - nr * lam_im) / den)[..., None]
    return ar, ai, cr * b_re - ci * b_im, cr * b_im + ci * b_re


def _block_diag(t, nslab):
    G, R, C = t.shape
    eye = jnp.eye(SLAB_GROUPS, dtype=t.dtype)
    t = t.reshape(nslab, SLAB_GROUPS, R, C)
    return jnp.einsum('sgrc,gh->sgrhc', t, eye).reshape(nslab, SLAB_GROUPS * R, SLAB_GROUPS * C)


def _block_diag_part(t, R, C):
    nslab = t.shape[0]
    eye = jnp.eye(SLAB_GROUPS, dtype=t.dtype)
    t = t.reshape(nslab, SLAB_GROUPS, R, SLAB_GROUPS, C)
    return jnp.einsum('sgrhc,gh->sgrc', t, eye).reshape(nslab * SLAB_GROUPS, R, C)


def _place():
    return lax.axis_index("x"), lax.axis_index("y"), lax.axis_index("c")


def all_gather(shards, *, name):
    nw = len(shards)

    def body(*refs):
        ins, outs = refs[:nw], refs[nw:2 * nw]
        send_sems, recv_sems, local_sems = refs[2 * nw:]
        x, y, c = _place()
        me, sibling = (x, y, c), (x, y, 1 - c)
        chips = [(1 - x, y), (x, 1 - y), (1 - x, 1 - y)]

        def copy(w, k, block, to, own):
            px, py, pc = block
            slot = outs[w].at[4 * px + 2 * py + pc]
            return pltpu.make_async_remote_copy(
                src_ref=ins[w] if own else slot, dst_ref=slot, send_sem=send_sems.at[w, k],
                recv_sem=recv_sems.at[w, k], device_id=to, device_id_type=MESH)

        mine = [pltpu.make_async_copy(ins[w], outs[w].at[4 * x + 2 * y + c], local_sems.at[w]) for w in range(nw)]
        for cp in mine:
            cp.start()
        first = []
        for w in range(nw):
            first.append(copy(w, 0, me, sibling, True))
            first += [copy(w, 1 + j, me, (*chip, c), True) for j, chip in enumerate(chips)]
        for cp in first:
            cp.start()
        passed = []
        for j, chip in enumerate(chips):
            for w in range(nw):
                copy(w, 1 + j, (*chip, c), me, False).wait_recv()
                cp = copy(w, 4 + j, (*chip, c), sibling, False)
                cp.start()
                passed.append(cp)
        for w in range(nw):
            copy(w, 0, sibling, me, False).wait_recv()
            for j, chip in enumerate(chips):
                copy(w, 4 + j, (*chip, 1 - c), me, False).wait_recv()
        for cp in first + passed:
            cp.wait_send()
        for cp in mine:
            cp.wait()

    anyspec = pl.BlockSpec(memory_space=pl.ANY)
    return pl.pallas_call(
        body, out_shape=[jax.ShapeDtypeStruct((N_DEV,) + s.shape, s.dtype) for s in shards],
        in_specs=[anyspec] * nw, out_specs=[anyspec] * nw,
        scratch_shapes=[pltpu.SemaphoreType.DMA((nw, 7)), pltpu.SemaphoreType.DMA((nw, 7)),
                        pltpu.SemaphoreType.DMA((nw,))],
        compiler_params=pltpu.CompilerParams(has_side_effects=True), name=name)(*shards)


def exchange_cores(grads, *, name):
    nw = len(grads)

    def body(*refs):
        ins, outs = refs[:nw], refs[nw:2 * nw]
        send_sems, recv_sems = refs[2 * nw:]
        x, y, c = _place()
        copies = [pltpu.make_async_remote_copy(
            src_ref=ins[w].at[:, 1 - c], dst_ref=outs[w], send_sem=send_sems.at[w], recv_sem=recv_sems.at[w],
            device_id=(x, y, 1 - c), device_id_type=MESH) for w in range(nw)]
        for cp in copies:
            cp.start()
        for cp in copies:
            cp.wait()

    anyspec = pl.BlockSpec(memory_space=pl.ANY)
    return pl.pallas_call(
        body, out_shape=[jax.ShapeDtypeStruct((4,) + g.shape[2:], g.dtype) for g in grads],
        in_specs=[anyspec] * nw, out_specs=[anyspec] * nw,
        scratch_shapes=[pltpu.SemaphoreType.DMA((nw,)), pltpu.SemaphoreType.DMA((nw,))],
        compiler_params=pltpu.CompilerParams(has_side_effects=True), name=name)(*grads)


def exchange_chips(parts, *, name):
    nw = len(parts)

    def body(*refs):
        ins, outs = refs[:nw], refs[nw:2 * nw]
        send_sems, recv_sems = refs[2 * nw:]
        x, y, c = _place()
        chips = [(1 - x, y), (x, 1 - y), (1 - x, 1 - y)]
        copies = [pltpu.make_async_remote_copy(
            src_ref=ins[w].at[2 * cx + cy], dst_ref=outs[w].at[j], send_sem=send_sems.at[w, j],
            recv_sem=recv_sems.at[w, j], device_id=(cx, cy, c), device_id_type=MESH)
            for w in range(nw) for j, (cx, cy) in enumerate(chips)]
        for cp in copies:
            cp.start()
        for cp in copies:
            cp.wait()

    anyspec = pl.BlockSpec(memory_space=pl.ANY)
    return pl.pallas_call(
        body, out_shape=[jax.ShapeDtypeStruct((3,) + p.shape[1:], p.dtype) for p in parts],
        in_specs=[anyspec] * nw, out_specs=[anyspec] * nw,
        scratch_shapes=[pltpu.SemaphoreType.DMA((nw, 3)), pltpu.SemaphoreType.DMA((nw, 3))],
        compiler_params=pltpu.CompilerParams(has_side_effects=True), name=name)(*parts)


def _blocked(fn, ins, outs, *, name, tr=128):
    k, n = outs[0][0]
    tr = _tile(k, tr, 16)
    specs = []
    args = []
    for a in ins:
        if isinstance(a, tuple):
            arr, lead = a
            specs.append(pl.BlockSpec((None, tr, n), functools.partial(lambda i, lead: (lead, i, 0), lead=lead)))
            args.append(arr)
        else:
            specs.append(pl.BlockSpec((tr, n), lambda i: (i, 0)))
            args.append(a)
    nin = len(args)

    def body(*refs):
        res = fn(*[r[...] for r in refs[:nin]])
        for ref, val in zip(refs[nin:], res):
            ref[...] = val.astype(ref.dtype)

    return pl.pallas_call(
        body, out_shape=[jax.ShapeDtypeStruct(s, d) for s, d in outs], grid=(k // tr,), in_specs=specs,
        out_specs=[pl.BlockSpec((tr, n), lambda i: (i, 0)) for _ in outs],
        compiler_params=_params(("parallel",)), name=name)(*args)


def _adamw(w, g, m, v):
    m = ADAM_B1 * m + (1.0 - ADAM_B1) * g
    v = ADAM_B2 * v + (1.0 - ADAM_B2) * (g * g)
    m_hat = m / (1.0 - ADAM_B1 ** ADAM_STEP)
    v_hat = v / (1.0 - ADAM_B2 ** ADAM_STEP)
    delta = -ADAM_LR * (m_hat / (jnp.sqrt(v_hat) + ADAM_EPS) + ADAM_WD * w)
    return delta, m, v


def kernel(x, p, mix_norm_pre, w_in, lam_re, lam_im, log_dt, ssm_b_re, ssm_b_im, ssm_c_re, ssm_c_im, ssm_d, w_glu, b_glu, attn_out_norm, ssm_out_norm, w_out, mix_norm_post, mlp_norm_pre, w_up, w_down, mlp_norm_post, ple_norm_pre, w_ple_gate, w_ple_proj, ple_norm_post, loss_target, m_mix_norm_pre, m_w_in, m_lam_re, m_lam_im, m_log_dt, m_ssm_b_re, m_ssm_b_im, m_ssm_c_re, m_ssm_c_im, m_ssm_d, m_w_glu, m_b_glu, m_attn_out_norm, m_ssm_out_norm, m_w_out, m_mix_norm_post, m_mlp_norm_pre, m_w_up, m_w_down, m_mlp_norm_post, m_ple_norm_pre, m_w_ple_gate, m_w_ple_proj, m_ple_norm_post, v_mix_norm_pre, v_w_in, v_lam_re, v_lam_im, v_log_dt, v_ssm_b_re, v_ssm_b_im, v_ssm_c_re, v_ssm_c_im, v_ssm_d, v_w_glu, v_b_glu, v_attn_out_norm, v_ssm_out_norm, v_w_out, v_mix_norm_post, v_mlp_norm_pre, v_w_up, v_w_down, v_mlp_norm_post, v_ple_norm_pre, v_w_ple_gate, v_w_ple_proj, v_ple_norm_post):
    weights = dict(mix_norm_pre=mix_norm_pre, w_in=w_in, lam_re=lam_re, lam_im=lam_im, log_dt=log_dt, ssm_b_re=ssm_b_re, ssm_b_im=ssm_b_im, ssm_c_re=ssm_c_re, ssm_c_im=ssm_c_im, ssm_d=ssm_d, w_glu=w_glu, b_glu=b_glu, attn_out_norm=attn_out_norm, ssm_out_norm=ssm_out_norm, w_out=w_out, mix_norm_post=mix_norm_post, mlp_norm_pre=mlp_norm_pre, w_up=w_up, w_down=w_down, mlp_norm_post=mlp_norm_post, ple_norm_pre=ple_norm_pre, w_ple_gate=w_ple_gate, w_ple_proj=w_ple_proj, ple_norm_post=ple_norm_post)
    mom_m = dict(mix_norm_pre=m_mix_norm_pre, w_in=m_w_in, lam_re=m_lam_re, lam_im=m_lam_im, log_dt=m_log_dt, ssm_b_re=m_ssm_b_re, ssm_b_im=m_ssm_b_im, ssm_c_re=m_ssm_c_re, ssm_c_im=m_ssm_c_im, ssm_d=m_ssm_d, w_glu=m_w_glu, b_glu=m_b_glu, attn_out_norm=m_attn_out_norm, ssm_out_norm=m_ssm_out_norm, w_out=m_w_out, mix_norm_post=m_mix_norm_post, mlp_norm_pre=m_mlp_norm_pre, w_up=m_w_up, w_down=m_w_down, mlp_norm_post=m_mlp_norm_post, ple_norm_pre=m_ple_norm_pre, w_ple_gate=m_w_ple_gate, w_ple_proj=m_w_ple_proj, ple_norm_post=m_ple_norm_post)
    mom_v = dict(mix_norm_pre=v_mix_norm_pre, w_in=v_w_in, lam_re=v_lam_re, lam_im=v_lam_im, log_dt=v_log_dt, ssm_b_re=v_ssm_b_re, ssm_b_im=v_ssm_b_im, ssm_c_re=v_ssm_c_re, ssm_c_im=v_ssm_c_im, ssm_d=v_ssm_d, w_glu=v_w_glu, b_glu=v_b_glu, attn_out_norm=v_attn_out_norm, ssm_out_norm=v_ssm_out_norm, w_out=v_w_out, mix_norm_post=v_mix_norm_post, mlp_norm_pre=v_mlp_norm_pre, w_up=v_w_up, w_down=v_w_down, mlp_norm_post=v_mlp_norm_post, ple_norm_pre=v_ple_norm_pre, w_ple_gate=v_w_ple_gate, w_ple_proj=v_w_ple_proj, ple_norm_post=v_ple_norm_post)
    order = list(weights)
    big = ["w_in", "w_glu", "w_out", "w_up", "w_down", "w_ple_gate", "w_ple_proj"]
    col_sharded = {"w_in", "w_up", "w_ple_proj"}
    small = [n for n in order if n not in big]

    _, S, D = x.shape
    xs = x[0]
    tgt = loss_target[0]
    AW = attn_out_norm.shape[1]
    SW = ssm_d.shape[1]
    H = AW // HEAD_DIM
    G = SW // SSM_GROUP
    nslab = G // SLAB_GROUPS
    P_, C_ = SSM_STATE, SSM_GROUP

    shards = [weights[n][0].astype(BF16) for n in big]
    gathered = all_gather(shards, name="gather_weights")
    W, WT = {}, {}
    for n, g in zip(big, gathered):
        W[n] = g if n in col_sharded else g.reshape(1, N_DEV * g.shape[1], g.shape[2])
    for n in ("w_in", "w_up"):
        g = W[n]
        WT[n] = jnp.swapaxes(g, 1, 2).reshape(1, g.shape[0] * g.shape[2], g.shape[1])

    g1, g2, g3, g4, g5, g6 = (weights[n] for n in ("mix_norm_pre", "mix_norm_post", "mlp_norm_pre",
                                                      "mlp_norm_post", "ple_norm_pre", "ple_norm_post"))
    ga, gs = attn_out_norm, ssm_out_norm
    (hn1,) = rowwise(lambda a, g: (_rms(a, g),), [xs], [g1], [(D, BF16)], name="norm_in")
    (proj,) = mm_nn(hn1, W["w_in"], [F32], name="proj_in")
    attn, lse = attn_fwd(proj, H, name="attn_fwd")
    (mix_a,) = rowwise(lambda a, g: (_rms(a, g),), [attn], [ga], [(AW, BF16)], name="attn_norm")

    a_r, a_i, bb_r, bb_i = _discretise(lam_re[0], lam_im[0], log_dt[0], ssm_b_re[0], ssm_b_im[0])
    ssm_consts = (_block_diag(bb_r.swapaxes(1, 2), nslab).astype(BF16), _block_diag(bb_i.swapaxes(1, 2), nslab).astype(BF16),
                  a_r.reshape(nslab, 1, SLAB_STATES), a_i.reshape(nslab, 1, SLAB_STATES),
                  _block_diag(ssm_c_re[0].swapaxes(1, 2), nslab).astype(BF16),
                  _block_diag(ssm_c_im[0].swapaxes(1, 2), nslab).astype(BF16), ssm_d)
    u_seg = _to_segments(proj[:, 3 * AW:]).astype(BF16)
    y_pre = ssm_fwd(u_seg, *ssm_consts, name="ssm_fwd")
    (yg,) = rowwise(lambda a: (_gelu(a),), [y_pre], [], [(SW, BF16)], name="ssm_gelu")
    (gl1,) = mm_nn(yg, W["w_glu"], [F32], epi=lambda acc, b: (acc + b,), bias=b_glu, name="glu_gate")
    (mix_s,) = rowwise(lambda yp, gl, g: (_rms(_gelu(yp) * _sigmoid(gl), g),), [y_pre, gl1], [gs], [(SW, BF16)],
                       name="ssm_glu_norm")
    mixed = jnp.concatenate([mix_a, _from_segments(mix_s)], axis=1)
    (mo,) = mm_nn(mixed, W["w_out"], [F32], name="mix_out")

    def resid_norm(h, t, gpost, gpre):
        hh = h + _rms(t, gpost)
        return hh, _rms(hh, gpre)

    h1, hn2 = rowwise(resid_norm, [xs, mo], [g2, g3], [(D, F32), (D, BF16)], name="resid_mix")

    def relu2(acc):
        r = jnp.maximum(acc, 0.0)
        return acc, r * r

    up, act = mm_nn(hn2, W["w_up"], [BF16, BF16], epi=relu2, name="mlp_up")
    (ff,) = mm_nn(act, W["w_down"], [F32], name="mlp_down")
    h2, hn3 = rowwise(resid_norm, [h1, ff], [g4, g5], [(D, F32), (D, BF16)], name="resid_mlp")
    (gl2,) = mm_nn(hn3, W["w_ple_gate"], [F32], name="ple_gate")
    pb = p[0, 0].astype(BF16)
    (emb,) = mm_nn(pb, W["w_ple_proj"], [F32], name="ple_proj")

    def head(h, gl, e, t, g):
        sg = _sigmoid(gl)
        ge = sg * e
        err = h + _rms(ge, g) - t
        dh = err * (1.0 / D)
        dge, dg = _rms_bwd(dh, ge, g)
        return dh, dge * e * sg * (1.0 - sg), dge * sg, jnp.sum(err * err, axis=0, keepdims=True), dg

    dh3, dgl2, demb, loss_part, dg6 = rowwise(head, [h2, gl2, emb, tgt], [g6], [(D, F32), (D, BF16), (D, BF16)],
                                             [D, D], name="ple_loss_head")
    loss = lax.psum(0.5 / D * jnp.sum(loss_part), ("x", "y", "c"))

    grads = {}
    grads["w_ple_proj"] = mm_tn(pb, demb, N_DEV, name="grad_w_ple_proj")
    dhn3 = mm_nt(dgl2, W["w_ple_gate"], F32, name="back_ple_gate")
    grads["w_ple_gate"] = mm_tn(hn3, dgl2, 1, name="grad_w_ple_gate")

    def back_resid(dh, dhn, h, t, gpre, gpost):
        d1, dgpre = _rms_bwd(dhn, h, gpre)
        dhh = dh + d1
        dt, dgpost = _rms_bwd(dhh, t, gpost)
        return dhh, dt, dgpre, dgpost

    dh2, dff, dg5, dg4 = rowwise(back_resid, [dh3, dhn3, h2, ff], [g5, g4], [(D, F32), (D, BF16)], [D, D],
                                 name="back_resid_mlp")
    dup = mm_nt(dff, W["w_down"], BF16, epi=lambda acc, u_: (acc * 2.0 * jnp.maximum(u_.astype(F32), 0.0),),
                extra=up, name="back_mlp_down")
    grads["w_down"] = mm_tn(act, dff, 1, name="grad_w_down")
    (dhn2,) = mm_nn(dup, WT["w_up"], [F32], name="back_mlp_up")
    grads["w_up"] = mm_tn(hn2, dup, N_DEV, name="grad_w_up")
    dh1, dmo, dg3, dg2 = rowwise(back_resid, [dh2, dhn2, h1, mo], [g3, g2], [(D, F32), (D, BF16)], [D, D],
                                 name="back_resid_mix")
    dmixed = mm_nt(dmo, W["w_out"], F32, name="back_mix_out")
    grads["w_out"] = mm_tn(mixed, dmo, 1, name="grad_w_out")

    def back_glu(dm, yp, gl, g):
        ygf = _gelu(yp)
        sg = _sigmoid(gl)
        dssm, dg = _rms_bwd(dm, ygf * sg, g)
        dgl = dssm * ygf * sg * (1.0 - sg)
        return dgl, dssm * sg, dg, jnp.sum(dgl, axis=0, keepdims=True)

    dgl1, dyg_direct, dgs, db_glu = rowwise(back_glu, [_to_segments(dmixed[:, AW:]), y_pre, gl1], [gs],
                                            [(SW, BF16), (SW, F32)], [SW, SW], name="back_glu")
    dyg_gate = mm_nt(dgl1, W["w_glu"], F32, name="back_glu_gate")
    grads["w_glu"] = mm_tn(yg, dgl1, 1, name="grad_w_glu")

    def back_gelu(d1, d2, yp, u_):
        dy = (d1 + d2) * _gelu_grad(yp)
        return dy, jnp.sum(dy * u_.astype(F32), axis=0, keepdims=True)

    dy_pre, d_skip = rowwise(back_gelu, [dyg_direct, dyg_gate, y_pre, u_seg], [], [(SW, F32)], [SW], name="back_gelu")
    du_seg, dbb_r, dbb_i, dcb_r, dcb_i, da_r, da_i = ssm_bwd(u_seg, dy_pre, *ssm_consts, name="ssm_bwd")

    def back_attn_norm(dm, a, g):
        da, dg = _rms_bwd(dm, a, g)
        prod = da * a
        delta = jnp.concatenate(
            [jnp.broadcast_to(jnp.sum(prod[:, h * HEAD_DIM:(h + 1) * HEAD_DIM], axis=-1, keepdims=True),
                              (prod.shape[0], HEAD_DIM)) for h in range(H)], axis=1)
        return da, delta, dg

    dattn, delta, dga = rowwise(back_attn_norm, [(dmixed, AW, 0), attn], [ga], [(AW, F32), (AW, F32)], [AW],
                                name="back_attn_norm")
    dq, dk, dv = attn_bwd(proj, dattn, lse, delta, H, name="attn_bwd")
    dproj = jnp.concatenate([dq, dk, dv, _from_segments(du_seg)], axis=1)
    (dhn1,) = mm_nn(dproj, WT["w_in"], [F32], name="back_proj_in")
    grads["w_in"] = mm_tn(hn1, dproj, N_DEV, name="grad_w_in")

    def back_in(dh, dhn, a, g):
        d1, dg = _rms_bwd(dhn, a, g)
        return dh + d1, dg

    grad_x, dg1 = rowwise(back_in, [dh1, dhn1, xs], [g1], [(D, F32)], [D], name="back_norm_in")

    x_i, y_i, c_i = _place()
    chunked = []
    for n in big:
        g = grads[n]
        g = g if n in col_sharded else g.reshape(N_DEV, g.shape[1] // N_DEV, g.shape[2])
        chunked.append(g.reshape(4, 2, g.shape[1], g.shape[2]))
    from_sibling = exchange_cores(chunked, name="grads_to_sibling")
    chip_sums = []
    for n, g, r in zip(big, chunked, from_sibling):
        k, nn = g.shape[2], g.shape[3]
        mine = lax.dynamic_index_in_dim(g, c_i, axis=1, keepdims=False).reshape(4 * k, nn)
        (s,) = _blocked(lambda a, b: (a.astype(F32) + b.astype(F32),), [mine, r.reshape(4 * k, nn)],
                        [((4 * k, nn), BF16)], name=f"chip_sum_{n}")
        chip_sums.append(s.reshape(4, k, nn))
    from_chips = exchange_chips(chip_sums, name="grads_to_chips")

    out_g, out_d, out_m, out_v = {}, {}, {}, {}

    def update(w_, m_, v_, own, r0, r1, r2):
        g = own.astype(F32) + r0.astype(F32) + r1.astype(F32) + r2.astype(F32)
        return (g,) + _adamw(w_, g, m_, v_)

    for n, s, r in zip(big, chip_sums, from_chips):
        own = lax.dynamic_index_in_dim(s, 2 * x_i + y_i, axis=0, keepdims=False)
        shp = weights[n].shape
        res = _blocked(update, [weights[n][0], mom_m[n][0], mom_v[n][0], own, (r, 0), (r, 1), (r, 2)],
                       [(shp[1:], F32)] * 4, name=f"adamw_{n}")
        out_g[n], out_d[n], out_m[n], out_v[n] = (t.reshape(shp) for t in res)

    nb = SLAB_GROUPS
    cot = dict(
        mix_norm_pre=dg1, mix_norm_post=dg2, mlp_norm_pre=dg3, mlp_norm_post=dg4, ple_norm_pre=dg5, ple_norm_post=dg6,
        attn_out_norm=dga, ssm_out_norm=dgs, b_glu=db_glu, ssm_d=d_skip,
        ssm_c_re=_block_diag_part(dcb_r, P_, C_).swapaxes(1, 2), ssm_c_im=_block_diag_part(dcb_i, P_, C_).swapaxes(1, 2),
        a_r=da_r.reshape(G, P_), a_i=da_i.reshape(G, P_),
        bb_r=_block_diag_part(dbb_r, C_, P_).swapaxes(1, 2), bb_i=_block_diag_part(dbb_i, C_, P_).swapaxes(1, 2))
    names = list(cot)
    flat = jnp.concatenate([cot[n].reshape(-1) for n in names])
    total = flat.shape[0]
    rows_ = -(-total // (LANES * 16)) * 16
    flat = jnp.pad(flat, (0, rows_ * LANES - total)).reshape(rows_, LANES)
    (every,) = all_gather([flat], name="gather_small_grads")
    (summed,) = _blocked(lambda *t: (functools.reduce(lambda a, b: a + b, t),), [(every, j) for j in range(N_DEV)],
                         [((rows_, LANES), F32)], name="sum_small_grads")
    summed = summed.reshape(-1)
    red, off = {}, 0
    for n in names:
        sz = cot[n].size
        red[n] = summed[off:off + sz].reshape(cot[n].shape)
        off += sz
    _, pull = jax.vjp(_discretise, lam_re[0], lam_im[0], log_dt[0], ssm_b_re[0], ssm_b_im[0])
    d_lre, d_lim, d_ldt, d_bre, d_bim = pull((red["a_r"], red["a_i"], red["bb_r"], red["bb_i"]))
    red.update(lam_re=d_lre, lam_im=d_lim, log_dt=d_ldt, ssm_b_re=d_bre, ssm_b_im=d_bim)

    def pack(d):
        t = jnp.concatenate([d[n].reshape(-1) for n in small])
        r_ = -(-t.shape[0] // (LANES * 16)) * 16
        return jnp.pad(t, (0, r_ * LANES - t.shape[0])).reshape(r_, LANES)

    sw, sg_, sm, sv = pack(weights), pack(red), pack(mom_m), pack(mom_v)
    sd, snm, snv = _blocked(lambda w_, g_, m_, v_: _adamw(w_, g_, m_, v_), [sw, sg_, sm, sv],
                            [(sw.shape, F32)] * 3, name="adamw_small")
    off = 0
    for n in small:
        sz = weights[n].size
        shp = weights[n].shape
        out_g[n] = red[n].reshape(shp)
        out_d[n] = sd.reshape(-1)[off:off + sz].reshape(shp)
        out_m[n] = snm.reshape(-1)[off:off + sz].reshape(shp)
        out_v[n] = snv.reshape(-1)[off:off + sz].reshape(shp)
        off += sz

    return (loss, grad_x[None], *[out_g[n] for n in order], *[out_d[n] for n in order],
            *[out_m[n] for n in order], *[out_v[n] for n in order])
```

```python
import functools
import math

import jax
import jax.numpy as jnp
from jax import lax
from jax.experimental import pallas as pl
from jax.experimental.pallas import tpu as pltpu

F32 = jnp.float32
BF16 = jnp.bfloat16
MESH = pl.DeviceIdType.MESH

N_DEV = 8
LANES = 128
SUBLANES = 8
VMEM_LIMIT = 48 * 1024 * 1024
VMEM_LIMIT_SCAN = 60 * 1024 * 1024

HEAD_DIM = 128
BLK = 128
DILATIONS = (1, 4, 16)
SSM_GROUP = 16
SSM_STATE = 64
SLAB_GROUPS = LANES // SSM_GROUP
SLAB_STATES = SLAB_GROUPS * SSM_STATE
SEGMENTS = SUBLANES
RMS_EPS = 1e-6
NEG_INF = -1e30

ADAM_LR = 0.001
ADAM_B1 = 0.9
ADAM_B2 = 0.999
ADAM_EPS = 1e-08
ADAM_WD = 0.01
ADAM_STEP = 10


def _tile(n, pref, unit=LANES):
    if n <= pref:
        return n
    t = (pref // unit) * unit
    while t > unit and n % t:
        t -= unit
    assert n % t == 0, (n, pref, unit)
    return t


def _params(sem=None, vmem=VMEM_LIMIT):
    return pltpu.CompilerParams(dimension_semantics=sem, vmem_limit_bytes=vmem)


_NN = (((1,), (0,)), ((), ()))
_NT = (((1,), (1,)), ((), ()))
_TN = (((0,), (0,)), ((), ()))


_ANY = pl.BlockSpec(memory_space=pl.ANY)


def _mm_call(dims, nk, n_extra, n_dep, n_out, epi, **kw):
    first_out = 2 + n_extra + n_dep
    kw["in_specs"] = list(kw["in_specs"]) + [_ANY] * n_dep

    def single(*refs):
        extra = refs[2:2 + n_extra]
        res = epi(lax.dot_general(refs[0][...], refs[1][...], dims, preferred_element_type=F32),
                  *[e[...] for e in extra])
        for o, r in zip(refs[first_out:first_out + n_out], res):
            o[...] = r.astype(o.dtype)

    if nk == 1:
        kw["scratch_shapes"] = []
        return pl.pallas_call(single, **kw)

    def body(*refs):
        a_ref, b_ref = refs[0], refs[1]
        extra = refs[2:2 + n_extra]
        outs = refs[first_out:first_out + n_out]
        acc = refs[-1]
        k = pl.program_id(2)

        @pl.when(k == 0)
        def _():
            acc[...] = jnp.zeros_like(acc)

        acc[...] += lax.dot_general(a_ref[...], b_ref[...], dims, preferred_element_type=F32)

        @pl.when(k == nk - 1)
        def _():
            res = epi(acc[...], *[e[...] for e in extra])
            for o, r in zip(outs, res):
                o[...] = r.astype(o.dtype)

    return pl.pallas_call(body, **kw)


def _identity_epi(acc):
    return (acc,)


def mm_nn(a, w, out_dtypes, *, name, epi=_identity_epi, bias=None, deps=(), tm=512, tn=512, tk=2048):
    M, K = a.shape
    J, K2, n = w.shape
    assert K == K2
    tm, tn, tk = _tile(M, tm, 16), _tile(n, tn), _tile(K, tk)
    npj = n // tn
    nk = K // tk
    in_specs = [pl.BlockSpec((tm, tk), lambda i, j, k: (i, k)),
                pl.BlockSpec((None, tk, tn), lambda i, j, k: (j // npj, k, j % npj))]
    args = [a, w]
    if bias is not None:
        in_specs.append(pl.BlockSpec((1, tn), lambda i, j, k: (0, j)))
        args.append(bias)
    return _mm_call(
        _NN, nk, len(args) - 2, len(deps), len(out_dtypes), epi,
        out_shape=[jax.ShapeDtypeStruct((M, J * n), d) for d in out_dtypes],
        grid=(M // tm, J * npj, nk), in_specs=in_specs,
        out_specs=[pl.BlockSpec((tm, tn), lambda i, j, k: (i, j)) for _ in out_dtypes],
        scratch_shapes=[pltpu.VMEM((tm, tn), F32)],
        compiler_params=_params(("parallel", "parallel", "arbitrary")), name=name)(*args, *deps)


def mm_nt(a, w, out_dtype, *, name, epi=_identity_epi, extra=None, tm=512, tko=512, tnr=2048):
    M, N = a.shape
    J, K, n = w.shape
    assert N == J * n
    tm, tko, tnr = _tile(M, tm, 16), _tile(K, tko), _tile(n, tnr)
    npj = n // tnr
    nk = N // tnr
    in_specs = [pl.BlockSpec((tm, tnr), lambda i, j, k: (i, k)),
                pl.BlockSpec((None, tko, tnr), lambda i, j, k: (k // npj, j, k % npj))]
    args = [a, w]
    if extra is not None:
        in_specs.append(pl.BlockSpec((tm, tko), lambda i, j, k: (i, j)))
        args.append(extra)
    return _mm_call(
        _NT, nk, len(args) - 2, 0, 1, epi,
        out_shape=[jax.ShapeDtypeStruct((M, K), out_dtype)],
        grid=(M // tm, K // tko, nk), in_specs=in_specs,
        out_specs=[pl.BlockSpec((tm, tko), lambda i, j, k: (i, j))],
        scratch_shapes=[pltpu.VMEM((tm, tko), F32)],
        compiler_params=_params(("parallel", "parallel", "arbitrary")), name=name)(*args)[0]


def mm_tn(a, b, J, *, name, deps=(), tko=1024, tn=1024, ts=1024):
    S, K = a.shape
    S2, N = b.shape
    assert S == S2 and N % J == 0
    n = N // J
    tko, tn, ts = _tile(K, tko), _tile(n, tn), _tile(S, ts)
    npj = n // tn
    nk = S // ts
    return _mm_call(
        _TN, nk, 0, len(deps), 1, _identity_epi,
        out_shape=[jax.ShapeDtypeStruct((J, K, n), BF16)],
        grid=(K // tko, J * npj, nk),
        in_specs=[pl.BlockSpec((ts, tko), lambda i, j, k: (k, i)),
                  pl.BlockSpec((ts, tn), lambda i, j, k: (k, j))],
        out_specs=[pl.BlockSpec((None, tko, tn), lambda i, j, k: (j // npj, i, j % npj))],
        scratch_shapes=[pltpu.VMEM((tko, tn), F32)],
        compiler_params=_params(("parallel", "parallel", "arbitrary")), name=name)(a, b, *deps)[0]


def rowwise(fn, rows, vecs, outs, accs=(), *, name, deps=(), ts=256):
    rows = [r if isinstance(r, tuple) else (r, r.shape[1], 0) for r in rows]
    S = rows[0][0].shape[0]
    ts = _tile(S, ts, 16)
    nr, nv, no, nd = len(rows), len(vecs), len(outs), len(deps)

    def body(*refs):
        r, v = refs[:nr], refs[nr:nr + nv]
        o, a = refs[nr + nv + nd:nr + nv + nd + no], refs[nr + nv + nd + no:]
        res = fn(*[t[...] for t in r], *[t[...] for t in v])
        for ref, val in zip(o, res[:no]):
            ref[...] = val.astype(ref.dtype)
        if a:
            @pl.when(pl.program_id(0) == 0)
            def _():
                for ref in a:
                    ref[...] = jnp.zeros_like(ref)

            for ref, val in zip(a, res[no:]):
                ref[...] += val

    in_specs = [pl.BlockSpec((ts, w), functools.partial(lambda i, cb: (i, cb), cb=cb)) for _, w, cb in rows]
    in_specs += [pl.BlockSpec(v.shape, lambda i: (0, 0)) for v in vecs] + [_ANY] * nd
    out_shape = [jax.ShapeDtypeStruct((S, w), d) for w, d in outs]
    out_shape += [jax.ShapeDtypeStruct((1, w), F32) for w in accs]
    out_specs = [pl.BlockSpec((ts, w), lambda i: (i, 0)) for w, _ in outs]
    out_specs += [pl.BlockSpec((1, w), lambda i: (0, 0)) for w in accs]
    return pl.pallas_call(body, out_shape=out_shape, grid=(S // ts,), in_specs=in_specs, out_specs=out_specs,
                          compiler_params=_params(("arbitrary",)), name=name)(*[r[0] for r in rows], *vecs, *deps)


def _rms(x, g):
    r = lax.rsqrt(jnp.mean(x * x, axis=-1, keepdims=True) + RMS_EPS)
    return x * r * g


def _rms_bwd(dy, x, g):
    r = lax.rsqrt(jnp.mean(x * x, axis=-1, keepdims=True) + RMS_EPS)
    xh = x * r
    dxh = dy * g
    dx = r * (dxh - xh * jnp.mean(dxh * xh, axis=-1, keepdims=True))
    return dx, jnp.sum(dy * xh, axis=0, keepdims=True)


def _sigmoid(x):
    return 1.0 / (1.0 + jnp.exp(-x))


_GELU_C = math.sqrt(2.0 / math.pi)


def _gelu(x):
    return 0.5 * x * (1.0 + jnp.tanh(_GELU_C * (x + 0.044715 * x * x * x)))


def _gelu_grad(x):
    t = jnp.tanh(_GELU_C * (x + 0.044715 * x * x * x))
    return 0.5 * (1.0 + t) + 0.5 * x * (1.0 - t * t) * _GELU_C * (1.0 + 3.0 * 0.044715 * x * x)


def _masks(n):
    ii = lax.broadcasted_iota(jnp.int32, (BLK, BLK), 0)
    jj = lax.broadcasted_iota(jnp.int32, (BLK, BLK), 1)
    return (jj >= ii) & (n > 0), jj <= ii


ATTN_INTERLEAVE = 2


def _block_rows(idx, d, nb):
    r, n = idx // nb, idx % nb
    cur = r + n * (BLK * d)
    prev = r + jnp.maximum(n - 1, 0) * (BLK * d)
    if d == 1:
        return n, pl.ds(pl.multiple_of(cur, BLK), BLK), pl.ds(pl.multiple_of(prev, BLK), BLK)
    return n, pl.ds(cur, BLK, stride=d), pl.ds(prev, BLK, stride=d)


def attn_fwd(proj, n_heads, *, name):
    S, WP = proj.shape
    assert S % (BLK * max(DILATIONS)) == 0
    nblk = S // BLK
    AW = n_heads * HEAD_DIM
    scale = 1.0 / math.sqrt(HEAD_DIM)

    def body(q_ref, k_ref, v_ref, o_ref, l_ref, acc, mrun, lrun):
        for first, d in zip((True, False, False), DILATIONS):
            nb = nblk // d

            def step(it, carry, d=d, nb=nb, first=first):
                blocks = [_block_rows(it + j * (nblk // ATTN_INTERLEAVE), d, nb) for j in range(ATTN_INTERLEAVE)]
                qs = [q_ref[cur, :].astype(BF16) for _, cur, _ in blocks]
                s_c = [lax.dot_general(q, k_ref[cur, :].astype(BF16), _NT, preferred_element_type=F32) * scale
                       for q, (_, cur, _) in zip(qs, blocks)]
                s_p = [lax.dot_general(q, k_ref[prev, :].astype(BF16), _NT, preferred_element_type=F32) * scale
                       for q, (_, _, prev) in zip(qs, blocks)]
                for j, (n, cur, prev) in enumerate(blocks):
                    m_prev, m_cur = _masks(n)
                    sc = jnp.where(m_cur, s_c[j], NEG_INF)
                    sp = jnp.where(m_prev, s_p[j], NEG_INF)
                    m = jnp.maximum(jnp.max(sc, axis=-1, keepdims=True), jnp.max(sp, axis=-1, keepdims=True))
                    p_c = jnp.exp(sc - m)
                    p_p = jnp.exp(sp - m)
                    l = jnp.sum(p_c, axis=-1, keepdims=True) + jnp.sum(p_p, axis=-1, keepdims=True)
                    o = (jnp.dot(p_c.astype(BF16), v_ref[cur, :].astype(BF16), preferred_element_type=F32)
                         + jnp.dot(p_p.astype(BF16), v_ref[prev, :].astype(BF16), preferred_element_type=F32))
                    m = jnp.broadcast_to(m, (BLK, HEAD_DIM))
                    l = jnp.broadcast_to(l, (BLK, HEAD_DIM))
                    if first:
                        acc[cur, :], mrun[cur, :], lrun[cur, :] = o, m, l
                    else:
                        m_old = mrun[cur, :]
                        m_new = jnp.maximum(m_old, m)
                        w_old, w_blk = jnp.exp(m_old - m_new), jnp.exp(m - m_new)
                        acc[cur, :] = w_old * acc[cur, :] + w_blk * o
                        lrun[cur, :] = w_old * lrun[cur, :] + w_blk * l
                        mrun[cur, :] = m_new
                return carry

            lax.fori_loop(0, nblk // ATTN_INTERLEAVE, step, 0)

        def finish(c, carry):
            r = pl.ds(pl.multiple_of(c * BLK, BLK), BLK)
            o_ref[r, :] = acc[r, :] / lrun[r, :]
            l_ref[r, :] = mrun[r, :] + jnp.log(lrun[r, :])
            return carry

        lax.fori_loop(0, nblk, finish, 0)

    def col(off):
        return pl.BlockSpec((S, HEAD_DIM), lambda h: (0, off + h))

    ospec = pl.BlockSpec((S, HEAD_DIM), lambda h: (0, h))
    return pl.pallas_call(
        body, out_shape=[jax.ShapeDtypeStruct((S, AW), F32)] * 2, grid=(n_heads,),
        in_specs=[col(0), col(n_heads), col(2 * n_heads)], out_specs=[ospec, ospec],
        scratch_shapes=[pltpu.VMEM((S, HEAD_DIM), F32)] * 3,
        compiler_params=_params(("parallel",)), name=name)(proj, proj, proj)


def attn_bwd(proj, do, lse, delta, n_heads, *, name):
    S, WP = proj.shape
    nblk = S // BLK
    AW = n_heads * HEAD_DIM
    scale = 1.0 / math.sqrt(HEAD_DIM)

    def body(q_ref, k_ref, v_ref, do_ref, l_ref, dl_ref, dq_ref, dk_ref, dv_ref, dq_sc, dk_sc, dv_sc):
        dq_sc[...] = jnp.zeros_like(dq_sc)
        dk_sc[...] = jnp.zeros_like(dk_sc)
        dv_sc[...] = jnp.zeros_like(dv_sc)
        for d in DILATIONS:
            nb = nblk // d

            def step(it, carry, d=d, nb=nb):
                blocks = [_block_rows(it + j * (nblk // ATTN_INTERLEAVE), d, nb) for j in range(ATTN_INTERLEAVE)]
                for n, cur, prev in blocks:
                    q = q_ref[cur, :].astype(BF16)
                    g = do_ref[cur, :].astype(BF16)
                    lse_n = l_ref[cur, :][:, :1]
                    dl_n = dl_ref[cur, :][:, :1]
                    dq = jnp.zeros((BLK, HEAD_DIM), F32)
                    for rows, mask in zip((prev, cur), _masks(n)):
                        kb = k_ref[rows, :].astype(BF16)
                        vb = v_ref[rows, :].astype(BF16)
                        s = lax.dot_general(q, kb, _NT, preferred_element_type=F32) * scale
                        p = jnp.where(mask, jnp.exp(s - lse_n), 0.0)
                        dp = lax.dot_general(g, vb, _NT, preferred_element_type=F32)
                        ds = (p * (dp - dl_n) * scale).astype(BF16)
                        dq = dq + jnp.dot(ds, kb, preferred_element_type=F32)
                        dk_sc[rows, :] += lax.dot_general(ds, q, _TN, preferred_element_type=F32)
                        dv_sc[rows, :] += lax.dot_general(p.astype(BF16), g, _TN, preferred_element_type=F32)
                    dq_sc[cur, :] += dq
                return carry

            lax.fori_loop(0, nblk // ATTN_INTERLEAVE, step, 0)
        dq_ref[...] = dq_sc[...].astype(BF16)
        dk_ref[...] = dk_sc[...].astype(BF16)
        dv_ref[...] = dv_sc[...].astype(BF16)

    def col(off):
        return pl.BlockSpec((S, HEAD_DIM), lambda h: (0, off + h))

    ospec = pl.BlockSpec((S, HEAD_DIM), lambda h: (0, h))
    return pl.pallas_call(
        body, out_shape=[jax.ShapeDtypeStruct((S, AW), BF16)] * 3, grid=(n_heads,),
        in_specs=[col(0), col(n_heads), col(2 * n_heads), ospec, ospec, ospec], out_specs=[ospec] * 3,
        scratch_shapes=[pltpu.VMEM((S, HEAD_DIM), F32)] * 3,
        compiler_params=_params(("parallel",)), name=name)(proj, proj, proj, do, lse, delta)


def _to_segments(t):
    S, W = t.shape
    return t.reshape(SEGMENTS, S // SEGMENTS, W).swapaxes(0, 1).reshape(S, W)


def _from_segments(t):
    S, W = t.shape
    return t.reshape(S // SEGMENTS, SEGMENTS, W).swapaxes(0, 1).reshape(S, W)


def _cmul(ar, ai, br, bi):
    return ar * br - ai * bi, ar * bi + ai * br


def _power(ar, ai, log2n):
    for _ in range(log2n):
        ar, ai = _cmul(ar, ai, ar, ai)
    return ar, ai


def _shift_rows(x, up):
    row = lax.broadcasted_iota(jnp.int32, x.shape, 0)
    if up:
        return jnp.where(row == SEGMENTS - 1, 0.0, pltpu.roll(x, SEGMENTS - 1, 0))
    return jnp.where(row == 0, 0.0, pltpu.roll(x, 1, 0))


def _segment_carries(er, ei, pr, pi, up):
    cr = jnp.zeros_like(er)
    ci = jnp.zeros_like(ei)
    for _ in range(SEGMENTS - 1):
        tr, ti = _cmul(pr, pi, cr, ci)
        cr, ci = _shift_rows(er + tr, up), _shift_rows(ei + ti, up)
    return cr, ci


def _scan_states(sr, si, ar, ai, T, reverse):
    ns = sr.shape[1]
    ar8 = jnp.broadcast_to(ar, (SEGMENTS, ns))
    ai8 = jnp.broadcast_to(ai, (SEGMENTS, ns))

    def rows(t):
        k = (T - 1 - t) if reverse else t
        return pl.ds(pl.multiple_of(k * SEGMENTS, SEGMENTS), SEGMENTS)

    def advance(t, c):
        tr, ti = _cmul(ar8, ai8, c[0], c[1])
        return tr + sr[rows(t), :], ti + si[rows(t), :]

    zero = jnp.zeros((SEGMENTS, ns), F32)
    er, ei = lax.fori_loop(0, T, advance, (zero, zero), unroll=4)
    pr, pi = _power(ar, ai, T.bit_length() - 1)
    cr, ci = _segment_carries(er, ei, jnp.broadcast_to(pr, (SEGMENTS, ns)), jnp.broadcast_to(pi, (SEGMENTS, ns)), reverse)

    def store(t, c):
        nr, ni = advance(t, c)
        sr[rows(t), :] = nr
        si[rows(t), :] = ni
        return nr, ni

    lax.fori_loop(0, T, store, (cr, ci), unroll=4)
    return cr, ci


def _slab_specs(ns):
    return [pl.BlockSpec((None, LANES, ns), lambda g: (g, 0, 0)),
            pl.BlockSpec((None, LANES, ns), lambda g: (g, 0, 0)),
            pl.BlockSpec((None, 1, ns), lambda g: (g, 0, 0)),
            pl.BlockSpec((None, 1, ns), lambda g: (g, 0, 0)),
            pl.BlockSpec((None, ns, LANES), lambda g: (g, 0, 0)),
            pl.BlockSpec((None, ns, LANES), lambda g: (g, 0, 0)),
            pl.BlockSpec((1, LANES), lambda g: (0, g))]


def _chunks(S):
    rc = _tile(S, 512, 16)
    return rc, S // rc


def ssm_fwd(u, bbr, bbi, ar, ai, cbr, cbi, dsk, *, name):
    S, SW = u.shape
    nslab, _, ns = bbr.shape
    T = S // SEGMENTS
    assert T & (T - 1) == 0
    rc, nc = _chunks(S)

    def body(u_ref, br_ref, bi_ref, ar_ref, ai_ref, cr_ref, ci_ref, d_ref, y_ref, sr, si):
        def inputs(c, carry):
            r = pl.ds(pl.multiple_of(c * rc, rc), rc)
            sr[r, :] = jnp.dot(u_ref[r, :], br_ref[...], preferred_element_type=F32)
            si[r, :] = jnp.dot(u_ref[r, :], bi_ref[...], preferred_element_type=F32)
            return carry

        lax.fori_loop(0, nc, inputs, 0)
        _scan_states(sr, si, ar_ref[...], ai_ref[...], T, False)

        def outputs(c, carry):
            r = pl.ds(pl.multiple_of(c * rc, rc), rc)
            y_ref[r, :] = (jnp.dot(sr[r, :].astype(BF16), cr_ref[...], preferred_element_type=F32)
                           - jnp.dot(si[r, :].astype(BF16), ci_ref[...], preferred_element_type=F32)
                           + d_ref[...] * u_ref[r, :].astype(F32))
            return carry

        lax.fori_loop(0, nc, outputs, 0)

    slab = pl.BlockSpec((S, LANES), lambda g: (0, g))
    return pl.pallas_call(
        body, out_shape=jax.ShapeDtypeStruct((S, SW), F32), grid=(nslab,),
        in_specs=[slab] + _slab_specs(ns), out_specs=slab,
        scratch_shapes=[pltpu.VMEM((S, ns), F32)] * 2,
        compiler_params=_params(("parallel",), VMEM_LIMIT_SCAN), name=name)(u, bbr, bbi, ar, ai, cbr, cbi, dsk)


def ssm_bwd(u, dy, bbr, bbi, ar, ai, cbr, cbi, dsk, *, name):
    S, SW = u.shape
    nslab, _, ns = bbr.shape
    T = S // SEGMENTS
    rc, nc = _chunks(S)

    def body(u_ref, dy_ref, br_ref, bi_ref, ar_ref, ai_ref, cr_ref, ci_ref, d_ref,
             du_ref, dbr_ref, dbi_ref, dcr_ref, dci_ref, dar_ref, dai_ref, sr, si, lr, li):
        def inputs(c, carry):
            r = pl.ds(pl.multiple_of(c * rc, rc), rc)
            ub = u_ref[r, :]
            gb = dy_ref[r, :].astype(BF16)
            sr[r, :] = jnp.dot(ub, br_ref[...], preferred_element_type=F32)
            si[r, :] = jnp.dot(ub, bi_ref[...], preferred_element_type=F32)
            lr[r, :] = lax.dot_general(gb, cr_ref[...], _NT, preferred_element_type=F32)
            li[r, :] = -lax.dot_general(gb, ci_ref[...], _NT, preferred_element_type=F32)
            return carry

        lax.fori_loop(0, nc, inputs, 0)
        ar, ai = ar_ref[...], ai_ref[...]
        s0r, s0i = _scan_states(sr, si, ar, ai, T, False)
        _scan_states(lr, li, ar, -ai, T, True)

        def pair(k, c):
            now = pl.ds(pl.multiple_of(k * SEGMENTS, SEGMENTS), SEGMENTS)
            prev = pl.ds(pl.multiple_of((k - 1) * SEGMENTS, SEGMENTS), SEGMENTS)
            return (c[0] + lr[now, :] * sr[prev, :] + li[now, :] * si[prev, :],
                    c[1] - lr[now, :] * si[prev, :] + li[now, :] * sr[prev, :])

        first = pl.ds(0, SEGMENTS)
        acc = (lr[first, :] * s0r + li[first, :] * s0i, -lr[first, :] * s0i + li[first, :] * s0r)
        acc = lax.fori_loop(1, T, pair, acc, unroll=4)
        dar_ref[...] = jnp.sum(acc[0], axis=0, keepdims=True)
        dai_ref[...] = jnp.sum(acc[1], axis=0, keepdims=True)

        dbr_ref[...] = jnp.zeros_like(dbr_ref)
        dbi_ref[...] = jnp.zeros_like(dbi_ref)
        dcr_ref[...] = jnp.zeros_like(dcr_ref)
        dci_ref[...] = jnp.zeros_like(dci_ref)

        def outputs(c, carry):
            r = pl.ds(pl.multiple_of(c * rc, rc), rc)
            ub = u_ref[r, :]
            g = dy_ref[r, :]
            gb = g.astype(BF16)
            lrb = lr[r, :].astype(BF16)
            lib = li[r, :].astype(BF16)
            du_ref[r, :] = (lax.dot_general(lrb, br_ref[...], _NT, preferred_element_type=F32)
                            + lax.dot_general(lib, bi_ref[...], _NT, preferred_element_type=F32)
                            + d_ref[...] * g).astype(BF16)
            dbr_ref[...] += lax.dot_general(ub, lrb, _TN, preferred_element_type=F32)
            dbi_ref[...] += lax.dot_general(ub, lib, _TN, preferred_element_type=F32)
            dcr_ref[...] += lax.dot_general(sr[r, :].astype(BF16), gb, _TN, preferred_element_type=F32)
            dci_ref[...] -= lax.dot_general(si[r, :].astype(BF16), gb, _TN, preferred_element_type=F32)
            return carry

        lax.fori_loop(0, nc, outputs, 0)

    slab = pl.BlockSpec((S, LANES), lambda g: (0, g))
    bspec = pl.BlockSpec((None, LANES, ns), lambda g: (g, 0, 0))
    cspec = pl.BlockSpec((None, ns, LANES), lambda g: (g, 0, 0))
    aspec = pl.BlockSpec((None, 1, ns), lambda g: (g, 0, 0))
    return pl.pallas_call(
        body,
        out_shape=[jax.ShapeDtypeStruct((S, SW), BF16),
                   jax.ShapeDtypeStruct((nslab, LANES, ns), F32), jax.ShapeDtypeStruct((nslab, LANES, ns), F32),
                   jax.ShapeDtypeStruct((nslab, ns, LANES), F32), jax.ShapeDtypeStruct((nslab, ns, LANES), F32),
                   jax.ShapeDtypeStruct((nslab, 1, ns), F32), jax.ShapeDtypeStruct((nslab, 1, ns), F32)],
        grid=(nslab,), in_specs=[slab, slab] + _slab_specs(ns),
        out_specs=[slab, bspec, bspec, cspec, cspec, aspec, aspec],
        scratch_shapes=[pltpu.VMEM((S, ns), F32)] * 4,
        compiler_params=_params(("parallel",), VMEM_LIMIT_SCAN), name=name)(u, dy, bbr, bbi, ar, ai, cbr, cbi, dsk)


def _discretise(lam_re, lam_im, log_dt, b_re, b_im):
    dt = jnp.exp(log_dt)[:, None]
    mag = jnp.exp(lam_re * dt)
    ar = mag * jnp.cos(lam_im * dt)
    ai = mag * jnp.sin(lam_im * dt)
    nr, ni = ar - 1.0, ai
    den = lam_re * lam_re + lam_im * lam_im
    cr = ((nr * lam_re + ni * lam_im) / den)[..., None]
    ci = ((ni * lam_re - nr * lam_im) / den)[..., None]
    return ar, ai, cr * b_re - ci * b_im, cr * b_im + ci * b_re


def _block_diag(t, nslab):
    G, R, C = t.shape
    eye = jnp.eye(SLAB_GROUPS, dtype=t.dtype)
    t = t.reshape(nslab, SLAB_GROUPS, R, C)
    return jnp.einsum('sgrc,gh->sgrhc', t, eye).reshape(nslab, SLAB_GROUPS * R, SLAB_GROUPS * C)


def _block_diag_part(t, R, C):
    nslab = t.shape[0]
    eye = jnp.eye(SLAB_GROUPS, dtype=t.dtype)
    t = t.reshape(nslab, SLAB_GROUPS, R, SLAB_GROUPS, C)
    return jnp.einsum('sgrhc,gh->sgrc', t, eye).reshape(nslab * SLAB_GROUPS, R, C)


def _place():
    return lax.axis_index("x"), lax.axis_index("y"), lax.axis_index("c")


def all_gather(shards, *, name):
    nw = len(shards)

    def body(*refs):
        ins, outs = refs[:nw], refs[nw:2 * nw]
        send_sems, recv_sems, local_sems = refs[2 * nw:]
        x, y, c = _place()
        me, sibling = (x, y, c), (x, y, 1 - c)
        chips = [(1 - x, y), (x, 1 - y), (1 - x, 1 - y)]

        def copy(w, k, block, to, own):
            px, py, pc = block
            slot = outs[w].at[4 * px + 2 * py + pc]
            return pltpu.make_async_remote_copy(
                src_ref=ins[w] if own else slot, dst_ref=slot, send_sem=send_sems.at[w, k],
                recv_sem=recv_sems.at[w, k], device_id=to, device_id_type=MESH)

        mine = [pltpu.make_async_copy(ins[w], outs[w].at[4 * x + 2 * y + c], local_sems.at[w]) for w in range(nw)]
        for cp in mine:
            cp.start()
        first = []
        for w in range(nw):
            first.append(copy(w, 0, me, sibling, True))
            first += [copy(w, 1 + j, me, (*chip, c), True) for j, chip in enumerate(chips)]
        for cp in first:
            cp.start()
        passed = []
        for j, chip in enumerate(chips):
            for w in range(nw):
                copy(w, 1 + j, (*chip, c), me, False).wait_recv()
                cp = copy(w, 4 + j, (*chip, c), sibling, False)
                cp.start()
                passed.append(cp)
        for w in range(nw):
            copy(w, 0, sibling, me, False).wait_recv()
            for j, chip in enumerate(chips):
                copy(w, 4 + j, (*chip, 1 - c), me, False).wait_recv()
        for cp in first + passed:
            cp.wait_send()
        for cp in mine:
            cp.wait()

    anyspec = pl.BlockSpec(memory_space=pl.ANY)
    return pl.pallas_call(
        body, out_shape=[jax.ShapeDtypeStruct((N_DEV,) + s.shape, s.dtype) for s in shards],
        in_specs=[anyspec] * nw, out_specs=[anyspec] * nw,
        scratch_shapes=[pltpu.SemaphoreType.DMA((nw, 7)), pltpu.SemaphoreType.DMA((nw, 7)),
                        pltpu.SemaphoreType.DMA((nw,))],
        compiler_params=pltpu.CompilerParams(has_side_effects=True), name=name)(*shards)


_HBM = pl.BlockSpec(memory_space=pltpu.HBM)
_SEM = pl.BlockSpec(memory_space=pltpu.SEMAPHORE)
_ORDERED_EFFECT = pltpu.SideEffectType.DATAFLOW_SIDE_EFFECTING


def _split_call(name, bufs, sems_in, n_new, body_fn, after):
    nb, ns, nn = len(bufs), len(sems_in), len(n_new)

    def body(*refs):
        outs = refs[nb + ns + 1:]
        body_fn(refs[:nb], refs[nb:nb + ns], outs[:nn])
        outs[nn + nb][...] = jnp.zeros((SUBLANES, LANES), F32)

    res = pl.pallas_call(
        body, name=name,
        out_shape=([pltpu.SemaphoreType.DMA((n,)) for n in n_new] + [pltpu.HBM(b.shape, b.dtype) for b in bufs]
                   + [jax.ShapeDtypeStruct((SUBLANES, LANES), F32)]),
        in_specs=[_HBM] * nb + [_SEM] * ns + [_ANY],
        out_specs=[_SEM] * nn + [_HBM] * nb + [pl.BlockSpec(memory_space=pltpu.VMEM)],
        input_output_aliases={i: nn + i for i in range(nb)},
        compiler_params=pltpu.CompilerParams(has_side_effects=_ORDERED_EFFECT))(
            *[pltpu.with_memory_space_constraint(b, pltpu.HBM) for b in bufs], *sems_in, after)
    return list(res[:nn]), list(res[nn:nn + nb]), res[-1]


def _mesh_peers():
    x, y, c = _place()
    return x, y, c, (x, y, 1 - c), [(1 - x, y), (x, 1 - y), (1 - x, 1 - y)]


def gather_start(shards, after, *, name):
    nw = len(shards)
    x, y, c = _place()
    zones = [lax.dynamic_update_slice(lax.empty((N_DEV,) + s.shape, s.dtype), s[None], (4 * x + 2 * y + c, 0, 0))
             for s in shards]

    def body(bufs, taken, new):
        for cp in _gather_first(bufs, nw, new[0], new[1]):
            cp.start()

    sems, bufs, token = _split_call(name, list(shards) + zones, [], [4 * nw, 4 * nw], body, after)
    return nw, sems, bufs, token


def _gather_first(bufs, nw, send, recv):
    x, y, c, sibling, chips = _mesh_peers()
    out = []
    for w in range(nw):
        slot = bufs[nw + w].at[4 * x + 2 * y + c]
        for k, to in enumerate([sibling] + [(*ch, c) for ch in chips]):
            out.append(pltpu.make_async_remote_copy(
                src_ref=bufs[w], dst_ref=slot, send_sem=send.at[4 * w + k], recv_sem=recv.at[4 * w + k],
                device_id=to, device_id_type=MESH))
    return out


def _gather_slot_copy(bufs, nw, w, block, send_sem, recv_sem, to):
    px, py, pc = block
    slot = bufs[nw + w].at[4 * px + 2 * py + pc]
    return pltpu.make_async_remote_copy(src_ref=slot, dst_ref=slot, send_sem=send_sem, recv_sem=recv_sem,
                                        device_id=to, device_id_type=MESH)


def gather_forward(state, after, *, name):
    nw, sems, bufs, _ = state

    def body(bufs, taken, new):
        x, y, c, sibling, chips = _mesh_peers()
        for j, ch in enumerate(chips):
            for w in range(nw):
                k = 4 * w + 1 + j
                _gather_slot_copy(bufs, nw, w, (*ch, c), taken[0].at[k], taken[1].at[k], (*ch, c)).wait_recv()
                _gather_slot_copy(bufs, nw, w, (*ch, c), new[0].at[3 * w + j], new[1].at[3 * w + j], sibling).start()
        for w in range(nw):
            _gather_slot_copy(bufs, nw, w, sibling, taken[0].at[4 * w], taken[1].at[4 * w], sibling).wait_recv()
        for cp in _gather_first(bufs, nw, taken[0], taken[1]):
            cp.wait_send()

    sems, bufs, token = _split_call(name, bufs, sems, [3 * nw, 3 * nw], body, after)
    return nw, sems, bufs, token


def gather_finish(state, after, *, name):
    nw, sems, bufs, _ = state

    def body(bufs, taken, new):
        x, y, c, sibling, chips = _mesh_peers()
        for w in range(nw):
            for j, ch in enumerate(chips):
                cp = _gather_slot_copy(bufs, nw, w, (*ch, 1 - c), taken[0].at[3 * w + j], taken[1].at[3 * w + j], sibling)
                cp.wait_send()
                cp.wait_recv()

    _, bufs, _ = _split_call(name, bufs, sems, [], body, after)
    return bufs[nw:]


def exchange_start(srcs, zone_shapes, copies, n, after, *, name):
    nw = len(srcs)
    zones = [lax.empty(z, s.dtype) for z, s in zip(zone_shapes, srcs)]

    def body(bufs, taken, new):
        for cp in copies(bufs[:nw], bufs[nw:], new[0], new[1]):
            cp.start()

    sems, bufs, token = _split_call(name, list(srcs) + zones, [], [n, n], body, after)
    return nw, copies, sems, bufs, token


def exchange_wait(state, after, *, name):
    nw, copies, sems, bufs, _ = state

    def body(bufs, taken, new):
        for cp in copies(bufs[:nw], bufs[nw:], taken[0], taken[1]):
            cp.wait_send()
            cp.wait_recv()

    _, bufs, _ = _split_call(name, bufs, sems, [], body, after)
    return bufs[nw:]


def _core_copies(srcs, zones, send, recv):
    x, y, c = _place()
    return [pltpu.make_async_remote_copy(
        src_ref=srcs[w].at[:, 1 - c], dst_ref=zones[w], send_sem=send.at[w], recv_sem=recv.at[w],
        device_id=(x, y, 1 - c), device_id_type=MESH) for w in range(len(srcs))]


def _chip_copies(srcs, zones, send, recv):
    x, y, c = _place()
    chips = [(1 - x, y), (x, 1 - y), (1 - x, 1 - y)]
    return [pltpu.make_async_remote_copy(
        src_ref=srcs[w].at[2 * cx + cy], dst_ref=zones[w].at[j], send_sem=send.at[3 * w + j],
        recv_sem=recv.at[3 * w + j], device_id=(cx, cy, c), device_id_type=MESH)
        for w in range(len(srcs)) for j, (cx, cy) in enumerate(chips)]


def _blocked(fn, ins, outs, *, name, tr=128):
    k, n = outs[0][0]
    tr = _tile(k, tr, 16)
    specs = []
    args = []
    for a in ins:
        if isinstance(a, tuple):
            arr, lead = a
            specs.append(pl.BlockSpec((None, tr, n), functools.partial(lambda i, lead: (lead, i, 0), lead=lead)))
            args.append(arr)
        else:
            specs.append(pl.BlockSpec((tr, n), lambda i: (i, 0)))
            args.append(a)
    nin = len(args)

    def body(*refs):
        res = fn(*[r[...] for r in refs[:nin]])
        for ref, val in zip(refs[nin:], res):
            ref[...] = val.astype(ref.dtype)

    return pl.pallas_call(
        body, out_shape=[jax.ShapeDtypeStruct(s, d) for s, d in outs], grid=(k // tr,), in_specs=specs,
        out_specs=[pl.BlockSpec((tr, n), lambda i: (i, 0)) for _ in outs],
        compiler_params=_params(("parallel",)), name=name)(*args)


def _adamw(w, g, m, v):
    m = ADAM_B1 * m + (1.0 - ADAM_B1) * g
    v = ADAM_B2 * v + (1.0 - ADAM_B2) * (g * g)
    m_hat = m / (1.0 - ADAM_B1 ** ADAM_STEP)
    v_hat = v / (1.0 - ADAM_B2 ** ADAM_STEP)
    delta = -ADAM_LR * (m_hat / (jnp.sqrt(v_hat) + ADAM_EPS) + ADAM_WD * w)
    return delta, m, v


def kernel(x, p, mix_norm_pre, w_in, lam_re, lam_im, log_dt, ssm_b_re, ssm_b_im, ssm_c_re, ssm_c_im, ssm_d, w_glu, b_glu, attn_out_norm, ssm_out_norm, w_out, mix_norm_post, mlp_norm_pre, w_up, w_down, mlp_norm_post, ple_norm_pre, w_ple_gate, w_ple_proj, ple_norm_post, loss_target, m_mix_norm_pre, m_w_in, m_lam_re, m_lam_im, m_log_dt, m_ssm_b_re, m_ssm_b_im, m_ssm_c_re, m_ssm_c_im, m_ssm_d, m_w_glu, m_b_glu, m_attn_out_norm, m_ssm_out_norm, m_w_out, m_mix_norm_post, m_mlp_norm_pre, m_w_up, m_w_down, m_mlp_norm_post, m_ple_norm_pre, m_w_ple_gate, m_w_ple_proj, m_ple_norm_post, v_mix_norm_pre, v_w_in, v_lam_re, v_lam_im, v_log_dt, v_ssm_b_re, v_ssm_b_im, v_ssm_c_re, v_ssm_c_im, v_ssm_d, v_w_glu, v_b_glu, v_attn_out_norm, v_ssm_out_norm, v_w_out, v_mix_norm_post, v_mlp_norm_pre, v_w_up, v_w_down, v_mlp_norm_post, v_ple_norm_pre, v_w_ple_gate, v_w_ple_proj, v_ple_norm_post):
    weights = dict(mix_norm_pre=mix_norm_pre, w_in=w_in, lam_re=lam_re, lam_im=lam_im, log_dt=log_dt, ssm_b_re=ssm_b_re, ssm_b_im=ssm_b_im, ssm_c_re=ssm_c_re, ssm_c_im=ssm_c_im, ssm_d=ssm_d, w_glu=w_glu, b_glu=b_glu, attn_out_norm=attn_out_norm, ssm_out_norm=ssm_out_norm, w_out=w_out, mix_norm_post=mix_norm_post, mlp_norm_pre=mlp_norm_pre, w_up=w_up, w_down=w_down, mlp_norm_post=mlp_norm_post, ple_norm_pre=ple_norm_pre, w_ple_gate=w_ple_gate, w_ple_proj=w_ple_proj, ple_norm_post=ple_norm_post)
    mom_m = dict(mix_norm_pre=m_mix_norm_pre, w_in=m_w_in, lam_re=m_lam_re, lam_im=m_lam_im, log_dt=m_log_dt, ssm_b_re=m_ssm_b_re, ssm_b_im=m_ssm_b_im, ssm_c_re=m_ssm_c_re, ssm_c_im=m_ssm_c_im, ssm_d=m_ssm_d, w_glu=m_w_glu, b_glu=m_b_glu, attn_out_norm=m_attn_out_norm, ssm_out_norm=m_ssm_out_norm, w_out=m_w_out, mix_norm_post=m_mix_norm_post, mlp_norm_pre=m_mlp_norm_pre, w_up=m_w_up, w_down=m_w_down, mlp_norm_post=m_mlp_norm_post, ple_norm_pre=m_ple_norm_pre, w_ple_gate=m_w_ple_gate, w_ple_proj=m_w_ple_proj, ple_norm_post=m_ple_norm_post)
    mom_v = dict(mix_norm_pre=v_mix_norm_pre, w_in=v_w_in, lam_re=v_lam_re, lam_im=v_lam_im, log_dt=v_log_dt, ssm_b_re=v_ssm_b_re, ssm_b_im=v_ssm_b_im, ssm_c_re=v_ssm_c_re, ssm_c_im=v_ssm_c_im, ssm_d=v_ssm_d, w_glu=v_w_glu, b_glu=v_b_glu, attn_out_norm=v_attn_out_norm, ssm_out_norm=v_ssm_out_norm, w_out=v_w_out, mix_norm_post=v_mix_norm_post, mlp_norm_pre=v_mlp_norm_pre, w_up=v_w_up, w_down=v_w_down, mlp_norm_post=v_mlp_norm_post, ple_norm_pre=v_ple_norm_pre, w_ple_gate=v_w_ple_gate, w_ple_proj=v_w_ple_proj, ple_norm_post=v_ple_norm_post)
    order = list(weights)
    big = ["w_in", "w_glu", "w_out", "w_up", "w_down", "w_ple_gate", "w_ple_proj"]
    col_sharded = {"w_in", "w_up", "w_ple_proj"}
    small = [n for n in order if n not in big]

    _, S, D = x.shape
    xs = x[0]
    tgt = loss_target[0]
    AW = attn_out_norm.shape[1]
    SW = ssm_d.shape[1]
    H = AW // HEAD_DIM
    G = SW // SSM_GROUP
    nslab = G // SLAB_GROUPS
    P_, C_ = SSM_STATE, SSM_GROUP

    shard = {n: weights[n][0].astype(BF16) for n in big}
    W, WT = {}, {}

    def arrived(names, gathered):
        for n, g in zip(names, gathered):
            W[n] = g if n in col_sharded else g.reshape(1, N_DEV * g.shape[1], g.shape[2])

    def transposed(g):
        return jnp.swapaxes(g, 1, 2).reshape(1, g.shape[0] * g.shape[2], g.shape[1])

    arrived(["w_in"], all_gather([shard["w_in"]], name="gather_w_in"))
    WT["w_in"] = transposed(W["w_in"])
    early, late = ["w_glu", "w_out"], ["w_up", "w_down", "w_ple_gate", "w_ple_proj"]
    gather_early = gather_start([shard[n] for n in early], W["w_in"], name="gather_early_start")
    gather_late = gather_start([shard[n] for n in late], gather_early[-1], name="gather_late_start")

    g1, g2, g3, g4, g5, g6 = (weights[n] for n in ("mix_norm_pre", "mix_norm_post", "mlp_norm_pre",
                                                      "mlp_norm_post", "ple_norm_pre", "ple_norm_post"))
    ga, gs = attn_out_norm, ssm_out_norm
    (hn1,) = rowwise(lambda a, g: (_rms(a, g),), [xs], [g1], [(D, BF16)], deps=(gather_late[-1],), name="norm_in")
    (proj,) = mm_nn(hn1, W["w_in"], [F32], name="proj_in")
    attn, lse = attn_fwd(proj, H, name="attn_fwd")
    gather_early = gather_forward(gather_early, attn, name="gather_early_forward")
    (mix_a,) = rowwise(lambda a, g: (_rms(a, g),), [attn], [ga], [(AW, BF16)], deps=(gather_early[-1],),
                       name="attn_norm")
    arrived(early, gather_finish(gather_early, mix_a, name="gather_early_finish"))

    a_r, a_i, bb_r, bb_i = _discretise(lam_re[0], lam_im[0], log_dt[0], ssm_b_re[0], ssm_b_im[0])
    ssm_consts = (_block_diag(bb_r.swapaxes(1, 2), nslab).astype(BF16), _block_diag(bb_i.swapaxes(1, 2), nslab).astype(BF16),
                  a_r.reshape(nslab, 1, SLAB_STATES), a_i.reshape(nslab, 1, SLAB_STATES),
                  _block_diag(ssm_c_re[0].swapaxes(1, 2), nslab).astype(BF16),
                  _block_diag(ssm_c_im[0].swapaxes(1, 2), nslab).astype(BF16), ssm_d)
    u_seg = _to_segments(proj[:, 3 * AW:]).astype(BF16)
    y_pre = ssm_fwd(u_seg, *ssm_consts, name="ssm_fwd")
    gather_late = gather_forward(gather_late, y_pre, name="gather_late_forward")
    (yg,) = rowwise(lambda a: (_gelu(a),), [y_pre], [], [(SW, BF16)], deps=(gather_late[-1],), name="ssm_gelu")
    (gl1,) = mm_nn(yg, W["w_glu"], [F32], epi=lambda acc, b: (acc + b,), bias=b_glu, name="glu_gate")
    (mix_s,) = rowwise(lambda yp, gl, g: (_rms(_gelu(yp) * _sigmoid(gl), g),), [y_pre, gl1], [gs], [(SW, BF16)],
                       name="ssm_glu_norm")
    mixed = jnp.concatenate([mix_a, _from_segments(mix_s)], axis=1)
    (mo,) = mm_nn(mixed, W["w_out"], [F32], name="mix_out")

    def resid_norm(h, t, gpost, gpre):
        hh = h + _rms(t, gpost)
        return hh, _rms(hh, gpre)

    h1, hn2 = rowwise(resid_norm, [xs, mo], [g2, g3], [(D, F32), (D, BF16)], name="resid_mix")
    arrived(late, gather_finish(gather_late, hn2, name="gather_late_finish"))
    WT["w_up"] = transposed(W["w_up"])

    def relu2(acc):
        r = jnp.maximum(acc, 0.0)
        return acc, r * r

    up, act = mm_nn(hn2, W["w_up"], [BF16, BF16], epi=relu2, name="mlp_up")
    (ff,) = mm_nn(act, W["w_down"], [F32], name="mlp_down")
    h2, hn3 = rowwise(resid_norm, [h1, ff], [g4, g5], [(D, F32), (D, BF16)], name="resid_mlp")
    (gl2,) = mm_nn(hn3, W["w_ple_gate"], [F32], name="ple_gate")
    pb = p[0, 0].astype(BF16)
    (emb,) = mm_nn(pb, W["w_ple_proj"], [F32], name="ple_proj")

    def head(h, gl, e, t, g):
        sg = _sigmoid(gl)
        ge = sg * e
        err = h + _rms(ge, g) - t
        dh = err * (1.0 / D)
        dge, dg = _rms_bwd(dh, ge, g)
        return dh, dge * e * sg * (1.0 - sg), dge * sg, jnp.sum(err * err, axis=0, keepdims=True), dg

    dh3, dgl2, demb, loss_part, dg6 = rowwise(head, [h2, gl2, emb, tgt], [g6], [(D, F32), (D, BF16), (D, BF16)],
                                             [D, D], name="ple_loss_head")
    loss = lax.psum(0.5 / D * jnp.sum(loss_part), ("x", "y", "c"))

    x_i, y_i, c_i = _place()
    grads, out_g, out_d, out_m, out_v = {}, {}, {}, {}, {}

    def to_sibling(names, after, tag):
        chunks = []
        for n in names:
            g = grads[n]
            g = g if n in col_sharded else g.reshape(N_DEV, g.shape[1] // N_DEV, g.shape[2])
            chunks.append(g.reshape(4, 2, g.shape[1], g.shape[2]))
        return chunks, exchange_start(chunks, [(4,) + g.shape[2:] for g in chunks], _core_copies, len(chunks), after,
                                      name=f"grads_to_sibling_{tag}")

    def to_chips(names, sent, after, tag):
        chunks, state = sent
        sums = []
        for n, g, r in zip(names, chunks, exchange_wait(state, after, name=f"grads_from_sibling_{tag}")):
            k, nn = g.shape[2], g.shape[3]
            mine = lax.dynamic_index_in_dim(g, c_i, axis=1, keepdims=False).reshape(4 * k, nn)
            (s,) = _blocked(lambda a, b: (a.astype(F32) + b.astype(F32),), [mine, r.reshape(4 * k, nn)],
                            [((4 * k, nn), BF16)], name=f"chip_sum_{n}")
            sums.append(s.reshape(4, k, nn))
        return sums, exchange_start(sums, [(3,) + s.shape[1:] for s in sums], _chip_copies, 3 * len(sums), sums[-1],
                                    name=f"grads_to_chips_{tag}")

    def update(w_, m_, v_, own, r0, r1, r2):
        g = own.astype(F32) + r0.astype(F32) + r1.astype(F32) + r2.astype(F32)
        return (g,) + _adamw(w_, g, m_, v_)

    def finish(names, sent, after, tag):
        sums, state = sent
        for n, s, r in zip(names, sums, exchange_wait(state, after, name=f"grads_from_chips_{tag}")):
            own = lax.dynamic_index_in_dim(s, 2 * x_i + y_i, axis=0, keepdims=False)
            shp = weights[n].shape
            res = _blocked(update, [weights[n][0], mom_m[n][0], mom_v[n][0], own, (r, 0), (r, 1), (r, 2)],
                           [(shp[1:], F32)] * 4, name=f"adamw_{n}")
            out_g[n], out_d[n], out_m[n], out_v[n] = (t.reshape(shp) for t in res)
        return out_v[names[-1]]

    grads["w_ple_proj"] = mm_tn(pb, demb, N_DEV, name="grad_w_ple_proj")
    dhn3 = mm_nt(dgl2, W["w_ple_gate"], F32, name="back_ple_gate")
    grads["w_ple_gate"] = mm_tn(hn3, dgl2, 1, name="grad_w_ple_gate")

    def back_resid(dh, dhn, h, t, gpre, gpost):
        d1, dgpre = _rms_bwd(dhn, h, gpre)
        dhh = dh + d1
        dt, dgpost = _rms_bwd(dhh, t, gpost)
        return dhh, dt, dgpre, dgpost

    dh2, dff, dg5, dg4 = rowwise(back_resid, [dh3, dhn3, h2, ff], [g5, g4], [(D, F32), (D, BF16)], [D, D],
                                 name="back_resid_mlp")
    dup = mm_nt(dff, W["w_down"], BF16, epi=lambda acc, u_: (acc * 2.0 * jnp.maximum(u_.astype(F32), 0.0),),
                extra=up, name="back_mlp_down")
    grads["w_down"] = mm_tn(act, dff, 1, name="grad_w_down")
    group_a = ["w_ple_proj", "w_ple_gate", "w_down"]
    sent_a = to_sibling(group_a, grads["w_down"], "a")
    (dhn2,) = mm_nn(dup, WT["w_up"], [F32], deps=(sent_a[1][-1],), name="back_mlp_up")
    sent_a = to_chips(group_a, sent_a, dhn2, "a")
    grads["w_up"] = mm_tn(hn2, dup, N_DEV, deps=(sent_a[1][-1],), name="grad_w_up")
    dh1, dmo, dg3, dg2 = rowwise(back_resid, [dh2, dhn2, h1, mo], [g3, g2], [(D, F32), (D, BF16)], [D, D],
                                 name="back_resid_mix")
    dmixed = mm_nt(dmo, W["w_out"], F32, name="back_mix_out")
    grads["w_out"] = mm_tn(mixed, dmo, 1, name="grad_w_out")

    def back_glu(dm, yp, gl, g):
        ygf = _gelu(yp)
        sg = _sigmoid(gl)
        dssm, dg = _rms_bwd(dm, ygf * sg, g)
        dgl = dssm * ygf * sg * (1.0 - sg)
        return dgl, dssm * sg, dg, jnp.sum(dgl, axis=0, keepdims=True)

    dgl1, dyg_direct, dgs, db_glu = rowwise(back_glu, [_to_segments(dmixed[:, AW:]), y_pre, gl1], [gs],
                                            [(SW, BF16), (SW, F32)], [SW, SW], name="back_glu")
    dyg_gate = mm_nt(dgl1, W["w_glu"], F32, name="back_glu_gate")
    grads["w_glu"] = mm_tn(yg, dgl1, 1, name="grad_w_glu")
    group_b = ["w_up", "w_out", "w_glu"]
    sent_b = to_sibling(group_b, grads["w_glu"], "b")
    done_a = finish(group_a, sent_a, sent_b[1][-1], "a")

    def back_gelu(d1, d2, yp, u_):
        dy = (d1 + d2) * _gelu_grad(yp)
        return dy, jnp.sum(dy * u_.astype(F32), axis=0, keepdims=True)

    dy_pre, d_skip = rowwise(back_gelu, [dyg_direct, dyg_gate, y_pre, u_seg], [], [(SW, F32)], [SW], deps=(done_a,),
                             name="back_gelu")
    du_seg, dbb_r, dbb_i, dcb_r, dcb_i, da_r, da_i = ssm_bwd(u_seg, dy_pre, *ssm_consts, name="ssm_bwd")
    sent_b = to_chips(group_b, sent_b, du_seg, "b")

    def back_attn_norm(dm, a, g):
        da, dg = _rms_bwd(dm, a, g)
        prod = da * a
        delta = jnp.concatenate(
            [jnp.broadcast_to(jnp.sum(prod[:, h * HEAD_DIM:(h + 1) * HEAD_DIM], axis=-1, keepdims=True),
                              (prod.shape[0], HEAD_DIM)) for h in range(H)], axis=1)
        return da, delta, dg

    dattn, delta, dga = rowwise(back_attn_norm, [(dmixed, AW, 0), attn], [ga], [(AW, F32), (AW, F32)], [AW],
                                deps=(sent_b[1][-1],), name="back_attn_norm")
    dq, dk, dv = attn_bwd(proj, dattn, lse, delta, H, name="attn_bwd")
    dproj = jnp.concatenate([dq, dk, dv, _from_segments(du_seg)], axis=1)
    (dhn1,) = mm_nn(dproj, WT["w_in"], [F32], name="back_proj_in")

    def back_in(dh, dhn, a, g):
        d1, dg = _rms_bwd(dhn, a, g)
        return dh + d1, dg

    grad_x, dg1 = rowwise(back_in, [dh1, dhn1, xs], [g1], [(D, F32)], [D], name="back_norm_in")

    cot = dict(
        mix_norm_pre=dg1, mix_norm_post=dg2, mlp_norm_pre=dg3, mlp_norm_post=dg4, ple_norm_pre=dg5, ple_norm_post=dg6,
        attn_out_norm=dga, ssm_out_norm=dgs, b_glu=db_glu, ssm_d=d_skip,
        ssm_c_re=_block_diag_part(dcb_r, P_, C_).swapaxes(1, 2), ssm_c_im=_block_diag_part(dcb_i, P_, C_).swapaxes(1, 2),
        a_r=da_r.reshape(G, P_), a_i=da_i.reshape(G, P_),
        bb_r=_block_diag_part(dbb_r, C_, P_).swapaxes(1, 2), bb_i=_block_diag_part(dbb_i, C_, P_).swapaxes(1, 2))
    names = list(cot)
    flat = jnp.concatenate([cot[n].reshape(-1) for n in names])
    total = flat.shape[0]
    rows_ = -(-total // (LANES * 16)) * 16
    flat = jnp.pad(flat, (0, rows_ * LANES - total)).reshape(rows_, LANES)
    gather_small = gather_start([flat], flat, name="gather_small_start")
    grads["w_in"] = mm_tn(hn1, dproj, N_DEV, deps=(gather_small[-1],), name="grad_w_in")
    group_c = ["w_in"]
    sent_c = to_sibling(group_c, grads["w_in"], "c")
    done_b = finish(group_b, sent_b, sent_c[1][-1], "b")
    sent_c = to_chips(group_c, sent_c, done_b, "c")
    gather_small = gather_forward(gather_small, sent_c[1][-1], name="gather_small_forward")
    (every,) = gather_finish(gather_small, gather_small[-1], name="gather_small_finish")
    (summed,) = _blocked(lambda *t: (functools.reduce(lambda a, b: a + b, t),), [(every, j) for j in range(N_DEV)],
                         [((rows_, LANES), F32)], name="sum_small_grads")
    summed = summed.reshape(-1)
    red, off = {}, 0
    for n in names:
        sz = cot[n].size
        red[n] = summed[off:off + sz].reshape(cot[n].shape)
        off += sz
    _, pull = jax.vjp(_discretise, lam_re[0], lam_im[0], log_dt[0], ssm_b_re[0], ssm_b_im[0])
    d_lre, d_lim, d_ldt, d_bre, d_bim = pull((red["a_r"], red["a_i"], red["bb_r"], red["bb_i"]))
    red.update(lam_re=d_lre, lam_im=d_lim, log_dt=d_ldt, ssm_b_re=d_bre, ssm_b_im=d_bim)

    def pack(d):
        t = jnp.concatenate([d[n].reshape(-1) for n in small])
        r_ = -(-t.shape[0] // (LANES * 16)) * 16
        return jnp.pad(t, (0, r_ * LANES - t.shape[0])).reshape(r_, LANES)

    sw, sg_, sm, sv = pack(weights), pack(red), pack(mom_m), pack(mom_v)
    sd, snm, snv = _blocked(lambda w_, g_, m_, v_: _adamw(w_, g_, m_, v_), [sw, sg_, sm, sv],
                            [(sw.shape, F32)] * 3, name="adamw_small")
    finish(group_c, sent_c, snv, "c")
    off = 0
    for n in small:
        sz = weights[n].size
        shp = weights[n].shape
        out_g[n] = red[n].reshape(shp)
        out_d[n] = sd.reshape(-1)[off:off + sz].reshape(shp)
        out_m[n] = snm.reshape(-1)[off:off + sz].reshape(shp)
        out_v[n] = snv.reshape(-1)[off:off + sz].reshape(shp)
        off += sz

    return (loss, grad_x[None], *[out_g[n] for n in order], *[out_d[n] for n in order],
            *[out_m[n] for n in order], *[out_v[n] for n in order])
```

```python
import functools
import math

import jax
import jax.numpy as jnp
from jax import lax
from jax.experimental import pallas as pl
from jax.experimental.pallas import tpu as pltpu

F32 = jnp.float32
BF16 = jnp.bfloat16
MESH = pl.DeviceIdType.MESH

N_DEV = 8
LANES = 128
SUBLANES = 8
VMEM_LIMIT = 48 * 1024 * 1024
VMEM_LIMIT_SCAN = 60 * 1024 * 1024

HEAD_DIM = 128
BLK = 128
DILATIONS = (1, 4, 16)
SSM_GROUP = 16
SSM_STATE = 64
SLAB_GROUPS = LANES // SSM_GROUP
SLAB_STATES = SLAB_GROUPS * SSM_STATE
SEGMENTS = SUBLANES
RMS_EPS = 1e-6
NEG_INF = -1e30

ADAM_LR = 0.001
ADAM_B1 = 0.9
ADAM_B2 = 0.999
ADAM_EPS = 1e-08
ADAM_WD = 0.01
ADAM_STEP = 10


def _tile(n, pref, unit=LANES):
    if n <= pref:
        return n
    t = (pref // unit) * unit
    while t > unit and n % t:
        t -= unit
    assert n % t == 0, (n, pref, unit)
    return t


def _params(sem=None, vmem=VMEM_LIMIT):
    return pltpu.CompilerParams(dimension_semantics=sem, vmem_limit_bytes=vmem)


_NN = (((1,), (0,)), ((), ()))
_NT = (((1,), (1,)), ((), ()))
_TN = (((0,), (0,)), ((), ()))


_ANY = pl.BlockSpec(memory_space=pl.ANY)


def _mm_call(dims, nk, n_extra, n_dep, n_out, epi, **kw):
    first_out = 2 + n_extra + n_dep
    kw["in_specs"] = list(kw["in_specs"]) + [_ANY] * n_dep

    def single(*refs):
        extra = refs[2:2 + n_extra]
        res = epi(lax.dot_general(refs[0][...], refs[1][...], dims, preferred_element_type=F32),
                  *[e[...] for e in extra])
        for o, r in zip(refs[first_out:first_out + n_out], res):
            o[...] = r.astype(o.dtype)

    if nk == 1:
        kw["scratch_shapes"] = []
        return pl.pallas_call(single, **kw)

    def body(*refs):
        a_ref, b_ref = refs[0], refs[1]
        extra = refs[2:2 + n_extra]
        outs = refs[first_out:first_out + n_out]
        acc = refs[-1]
        k = pl.program_id(2)

        @pl.when(k == 0)
        def _():
            acc[...] = jnp.zeros_like(acc)

        acc[...] += lax.dot_general(a_ref[...], b_ref[...], dims, preferred_element_type=F32)

        @pl.when(k == nk - 1)
        def _():
            res = epi(acc[...], *[e[...] for e in extra])
            for o, r in zip(outs, res):
                o[...] = r.astype(o.dtype)

    return pl.pallas_call(body, **kw)


def _identity_epi(acc):
    return (acc,)


def mm_nn(a, w, out_dtypes, *, name, epi=_identity_epi, bias=None, deps=(), tm=1024, tn=512, tk=2048):
    M, K = a.shape
    J, K2, n = w.shape
    assert K == K2
    tm, tn, tk = _tile(M, tm, 16), _tile(n, tn), _tile(K, tk)
    npj = n // tn
    nk = K // tk
    in_specs = [pl.BlockSpec((tm, tk), lambda i, j, k: (i, k)),
                pl.BlockSpec((None, tk, tn), lambda i, j, k: (j // npj, k, j % npj))]
    args = [a, w]
    if bias is not None:
        in_specs.append(pl.BlockSpec((1, tn), lambda i, j, k: (0, j)))
        args.append(bias)
    return _mm_call(
        _NN, nk, len(args) - 2, len(deps), len(out_dtypes), epi,
        out_shape=[jax.ShapeDtypeStruct((M, J * n), d) for d in out_dtypes],
        grid=(M // tm, J * npj, nk), in_specs=in_specs,
        out_specs=[pl.BlockSpec((tm, tn), lambda i, j, k: (i, j)) for _ in out_dtypes],
        scratch_shapes=[pltpu.VMEM((tm, tn), F32)],
        compiler_params=_params(("parallel", "parallel", "arbitrary")), name=name)(*args, *deps)


def mm_nt(a, w, out_dtype, *, name, epi=_identity_epi, extra=None, tm=1024, tko=512, tnr=2048):
    M, N = a.shape
    J, K, n = w.shape
    assert N == J * n
    tm, tko, tnr = _tile(M, tm, 16), _tile(K, tko), _tile(n, tnr)
    npj = n // tnr
    nk = N // tnr
    in_specs = [pl.BlockSpec((tm, tnr), lambda i, j, k: (i, k)),
                pl.BlockSpec((None, tko, tnr), lambda i, j, k: (k // npj, j, k % npj))]
    args = [a, w]
    if extra is not None:
        in_specs.append(pl.BlockSpec((tm, tko), lambda i, j, k: (i, j)))
        args.append(extra)
    return _mm_call(
        _NT, nk, len(args) - 2, 0, 1, epi,
        out_shape=[jax.ShapeDtypeStruct((M, K), out_dtype)],
        grid=(M // tm, K // tko, nk), in_specs=in_specs,
        out_specs=[pl.BlockSpec((tm, tko), lambda i, j, k: (i, j))],
        scratch_shapes=[pltpu.VMEM((tm, tko), F32)],
        compiler_params=_params(("parallel", "parallel", "arbitrary")), name=name)(*args)[0]


def mm_tn(a, b, J, *, name, deps=(), tko=1024, tn=1024, ts=1024):
    S, K = a.shape
    S2, N = b.shape
    assert S == S2 and N % J == 0
    n = N // J
    tko, tn, ts = _tile(K, tko), _tile(n, tn), _tile(S, ts)
    npj = n // tn
    nk = S // ts
    return _mm_call(
        _TN, nk, 0, len(deps), 1, _identity_epi,
        out_shape=[jax.ShapeDtypeStruct((J, K, n), BF16)],
        grid=(K // tko, J * npj, nk),
        in_specs=[pl.BlockSpec((ts, tko), lambda i, j, k: (k, i)),
                  pl.BlockSpec((ts, tn), lambda i, j, k: (k, j))],
        out_specs=[pl.BlockSpec((None, tko, tn), lambda i, j, k: (j // npj, i, j % npj))],
        scratch_shapes=[pltpu.VMEM((tko, tn), F32)],
        compiler_params=_params(("parallel", "parallel", "arbitrary")), name=name)(a, b, *deps)[0]


def rowwise(fn, rows, vecs, outs, accs=(), *, name, deps=(), ts=256):
    rows = [r if isinstance(r, tuple) else (r, r.shape[1], 0) for r in rows]
    S = rows[0][0].shape[0]
    ts = _tile(S, ts, 16)
    nr, nv, no, nd = len(rows), len(vecs), len(outs), len(deps)

    def body(*refs):
        r, v = refs[:nr], refs[nr:nr + nv]
        o, a = refs[nr + nv + nd:nr + nv + nd + no], refs[nr + nv + nd + no:]
        res = fn(*[t[...] for t in r], *[t[...] for t in v])
        for ref, val in zip(o, res[:no]):
            ref[...] = val.astype(ref.dtype)
        if a:
            @pl.when(pl.program_id(0) == 0)
            def _():
                for ref in a:
                    ref[...] = jnp.zeros_like(ref)

            for ref, val in zip(a, res[no:]):
                ref[...] += val

    in_specs = [pl.BlockSpec((ts, w), functools.partial(lambda i, cb: (i, cb), cb=cb)) for _, w, cb in rows]
    in_specs += [pl.BlockSpec(v.shape, lambda i: (0, 0)) for v in vecs] + [_ANY] * nd
    out_shape = [jax.ShapeDtypeStruct((S, w), d) for w, d in outs]
    out_shape += [jax.ShapeDtypeStruct((1, w), F32) for w in accs]
    out_specs = [pl.BlockSpec((ts, w), lambda i: (i, 0)) for w, _ in outs]
    out_specs += [pl.BlockSpec((1, w), lambda i: (0, 0)) for w in accs]
    return pl.pallas_call(body, out_shape=out_shape, grid=(S // ts,), in_specs=in_specs, out_specs=out_specs,
                          compiler_params=_params(("arbitrary",)), name=name)(*[r[0] for r in rows], *vecs, *deps)


def _rms(x, g):
    r = lax.rsqrt(jnp.mean(x * x, axis=-1, keepdims=True) + RMS_EPS)
    return x * r * g


def _rms_bwd(dy, x, g):
    r = lax.rsqrt(jnp.mean(x * x, axis=-1, keepdims=True) + RMS_EPS)
    xh = x * r
    dxh = dy * g
    dx = r * (dxh - xh * jnp.mean(dxh * xh, axis=-1, keepdims=True))
    return dx, jnp.sum(dy * xh, axis=0, keepdims=True)


def _sigmoid(x):
    return 1.0 / (1.0 + jnp.exp(-x))


_GELU_C = math.sqrt(2.0 / math.pi)


def _gelu(x):
    return 0.5 * x * (1.0 + jnp.tanh(_GELU_C * (x + 0.044715 * x * x * x)))


def _gelu_grad(x):
    t = jnp.tanh(_GELU_C * (x + 0.044715 * x * x * x))
    return 0.5 * (1.0 + t) + 0.5 * x * (1.0 - t * t) * _GELU_C * (1.0 + 3.0 * 0.044715 * x * x)


ATTN_INTERLEAVE = 2
KEY_PAD = BLK * max(DILATIONS)


def _key_mask(n):
    ii = lax.broadcasted_iota(jnp.int32, (BLK, 2 * BLK), 0)
    jj = lax.broadcasted_iota(jnp.int32, (BLK, 2 * BLK), 1)
    return ((jj < BLK) & (jj >= ii) & (n > 0)) | ((jj >= BLK) & (jj - BLK <= ii))


def _block_rows(idx, d, nb):
    r, n = idx // nb, idx % nb
    cur = r + n * (BLK * d)
    keys = cur + (KEY_PAD - BLK * d)
    if d == 1:
        return n, pl.ds(pl.multiple_of(cur, BLK), BLK), pl.ds(pl.multiple_of(keys, BLK), 2 * BLK)
    return n, pl.ds(cur, BLK, stride=d), pl.ds(keys, 2 * BLK, stride=d)


def _pad_keys(dst, src):
    dst[pl.ds(0, KEY_PAD), :] = jnp.zeros((KEY_PAD, dst.shape[1]), F32)

    def copy(c, carry):
        dst[pl.ds(pl.multiple_of(KEY_PAD + c * BLK, BLK), BLK), :] = src[pl.ds(pl.multiple_of(c * BLK, BLK), BLK), :]
        return carry

    lax.fori_loop(0, src.shape[0] // BLK, copy, 0)


def attn_fwd(proj, n_heads, *, name):
    S, WP = proj.shape
    assert S % (BLK * max(DILATIONS)) == 0
    nblk = S // BLK
    AW = n_heads * HEAD_DIM
    scale = 1.0 / math.sqrt(HEAD_DIM)

    def body(q_ref, k_ref, v_ref, o_ref, l_ref, acc, mrun, lrun, kp, vp):
        _pad_keys(kp, k_ref)
        _pad_keys(vp, v_ref)
        for first, d in zip((True, False, False), DILATIONS):
            nb = nblk // d

            def step(it, carry, d=d, nb=nb, first=first):
                blocks = [_block_rows(it + j * (nblk // ATTN_INTERLEAVE), d, nb) for j in range(ATTN_INTERLEAVE)]
                ss = [lax.dot_general(q_ref[cur, :].astype(BF16), kp[keys, :].astype(BF16), _NT,
                                      preferred_element_type=F32) * scale for _, cur, keys in blocks]
                for j, (n, cur, keys) in enumerate(blocks):
                    s = jnp.where(_key_mask(n), ss[j], NEG_INF)
                    m = jnp.max(s, axis=-1, keepdims=True)
                    p = jnp.exp(s - m)
                    l = jnp.sum(p, axis=-1, keepdims=True)
                    o = jnp.dot(p.astype(BF16), vp[keys, :].astype(BF16), preferred_element_type=F32)
                    m = jnp.broadcast_to(m, (BLK, HEAD_DIM))
                    l = jnp.broadcast_to(l, (BLK, HEAD_DIM))
                    if first:
                        acc[cur, :], mrun[cur, :], lrun[cur, :] = o, m, l
                    else:
                        m_old = mrun[cur, :]
                        m_new = jnp.maximum(m_old, m)
                        w_old, w_blk = jnp.exp(m_old - m_new), jnp.exp(m - m_new)
                        acc[cur, :] = w_old * acc[cur, :] + w_blk * o
                        lrun[cur, :] = w_old * lrun[cur, :] + w_blk * l
                        mrun[cur, :] = m_new
                return carry

            lax.fori_loop(0, nblk // ATTN_INTERLEAVE, step, 0)

        def finish(c, carry):
            r = pl.ds(pl.multiple_of(c * BLK, BLK), BLK)
            o_ref[r, :] = acc[r, :] / lrun[r, :]
            l_ref[r, :] = mrun[r, :] + jnp.log(lrun[r, :])
            return carry

        lax.fori_loop(0, nblk, finish, 0)

    def col(off):
        return pl.BlockSpec((S, HEAD_DIM), lambda h: (0, off + h))

    ospec = pl.BlockSpec((S, HEAD_DIM), lambda h: (0, h))
    return pl.pallas_call(
        body, out_shape=[jax.ShapeDtypeStruct((S, AW), F32)] * 2, grid=(n_heads,),
        in_specs=[col(0), col(n_heads), col(2 * n_heads)], out_specs=[ospec, ospec],
        scratch_shapes=[pltpu.VMEM((S, HEAD_DIM), F32)] * 3 + [pltpu.VMEM((KEY_PAD + S, HEAD_DIM), F32)] * 2,
        compiler_params=_params(("parallel",)), name=name)(proj, proj, proj)


def attn_bwd(proj, do, lse, delta, n_heads, *, name):
    S, WP = proj.shape
    nblk = S // BLK
    AW = n_heads * HEAD_DIM
    scale = 1.0 / math.sqrt(HEAD_DIM)

    def body(q_ref, k_ref, v_ref, do_ref, l_ref, dl_ref, dq_ref, dk_ref, dv_ref, dq_sc, dk_sc, dv_sc, kp, vp):
        _pad_keys(kp, k_ref)
        _pad_keys(vp, v_ref)
        dq_sc[...] = jnp.zeros_like(dq_sc)
        dk_sc[...] = jnp.zeros_like(dk_sc)
        dv_sc[...] = jnp.zeros_like(dv_sc)
        for d in DILATIONS:
            nb = nblk // d

            def step(it, carry, d=d, nb=nb):
                blocks = [_block_rows(it + j * (nblk // ATTN_INTERLEAVE), d, nb) for j in range(ATTN_INTERLEAVE)]
                for n, cur, keys in blocks:
                    q = q_ref[cur, :].astype(BF16)
                    g = do_ref[cur, :].astype(BF16)
                    kb = kp[keys, :].astype(BF16)
                    vb = vp[keys, :].astype(BF16)
                    s = lax.dot_general(q, kb, _NT, preferred_element_type=F32) * scale
                    p = jnp.where(_key_mask(n), jnp.exp(s - l_ref[cur, :][:, :1]), 0.0)
                    dp = lax.dot_general(g, vb, _NT, preferred_element_type=F32)
                    ds = (p * (dp - dl_ref[cur, :][:, :1]) * scale).astype(BF16)
                    dq_sc[cur, :] += jnp.dot(ds, kb, preferred_element_type=F32)
                    dk_sc[keys, :] += lax.dot_general(ds, q, _TN, preferred_element_type=F32)
                    dv_sc[keys, :] += lax.dot_general(p.astype(BF16), g, _TN, preferred_element_type=F32)
                return carry

            lax.fori_loop(0, nblk // ATTN_INTERLEAVE, step, 0)
        rows = pl.ds(KEY_PAD, S)
        dq_ref[...] = dq_sc[...].astype(BF16)
        dk_ref[...] = dk_sc[rows, :].astype(BF16)
        dv_ref[...] = dv_sc[rows, :].astype(BF16)

    def col(off):
        return pl.BlockSpec((S, HEAD_DIM), lambda h: (0, off + h))

    ospec = pl.BlockSpec((S, HEAD_DIM), lambda h: (0, h))
    return pl.pallas_call(
        body, out_shape=[jax.ShapeDtypeStruct((S, AW), BF16)] * 3, grid=(n_heads,),
        in_specs=[col(0), col(n_heads), col(2 * n_heads), ospec, ospec, ospec], out_specs=[ospec] * 3,
        scratch_shapes=[pltpu.VMEM((S, HEAD_DIM), F32)] + [pltpu.VMEM((KEY_PAD + S, HEAD_DIM), F32)] * 4,
        compiler_params=_params(("parallel",), VMEM_LIMIT_SCAN), name=name)(proj, proj, proj, do, lse, delta)


def _to_segments(t):
    S, W = t.shape
    return t.reshape(SEGMENTS, S // SEGMENTS, W).swapaxes(0, 1).reshape(S, W)


def _from_segments(t):
    S, W = t.shape
    return t.reshape(S // SEGMENTS, SEGMENTS, W).swapaxes(0, 1).reshape(S, W)


def _cmul(ar, ai, br, bi):
    return ar * br - ai * bi, ar * bi + ai * br


def _power(ar, ai, log2n):
    for _ in range(log2n):
        ar, ai = _cmul(ar, ai, ar, ai)
    return ar, ai


def _shift_rows(x, up):
    row = lax.broadcasted_iota(jnp.int32, x.shape, 0)
    if up:
        return jnp.where(row == SEGMENTS - 1, 0.0, pltpu.roll(x, SEGMENTS - 1, 0))
    return jnp.where(row == 0, 0.0, pltpu.roll(x, 1, 0))


def _segment_carries(er, ei, pr, pi, up):
    cr = jnp.zeros_like(er)
    ci = jnp.zeros_like(ei)
    for _ in range(SEGMENTS - 1):
        tr, ti = _cmul(pr, pi, cr, ci)
        cr, ci = _shift_rows(er + tr, up), _shift_rows(ei + ti, up)
    return cr, ci


def _scan_states(sr, si, ar, ai, T, reverse):
    ns = sr.shape[1]
    ar8 = jnp.broadcast_to(ar, (SEGMENTS, ns))
    ai8 = jnp.broadcast_to(ai, (SEGMENTS, ns))

    def rows(t):
        k = (T - 1 - t) if reverse else t
        return pl.ds(pl.multiple_of(k * SEGMENTS, SEGMENTS), SEGMENTS)

    def advance(t, c):
        tr, ti = _cmul(ar8, ai8, c[0], c[1])
        return tr + sr[rows(t), :], ti + si[rows(t), :]

    zero = jnp.zeros((SEGMENTS, ns), F32)
    er, ei = lax.fori_loop(0, T, advance, (zero, zero), unroll=4)
    pr, pi = _power(ar, ai, T.bit_length() - 1)
    cr, ci = _segment_carries(er, ei, jnp.broadcast_to(pr, (SEGMENTS, ns)), jnp.broadcast_to(pi, (SEGMENTS, ns)), reverse)

    def store(t, c):
        nr, ni = advance(t, c)
        sr[rows(t), :] = nr
        si[rows(t), :] = ni
        return nr, ni

    lax.fori_loop(0, T, store, (cr, ci), unroll=4)
    return cr, ci


def _slab_specs(ns):
    return [pl.BlockSpec((None, LANES, ns), lambda g: (g, 0, 0)),
            pl.BlockSpec((None, LANES, ns), lambda g: (g, 0, 0)),
            pl.BlockSpec((None, 1, ns), lambda g: (g, 0, 0)),
            pl.BlockSpec((None, 1, ns), lambda g: (g, 0, 0)),
            pl.BlockSpec((None, ns, LANES), lambda g: (g, 0, 0)),
            pl.BlockSpec((None, ns, LANES), lambda g: (g, 0, 0)),
            pl.BlockSpec((1, LANES), lambda g: (0, g))]


def _chunks(S):
    rc = _tile(S, 512, 16)
    return rc, S // rc


def ssm_fwd(u, bbr, bbi, ar, ai, cbr, cbi, dsk, *, name):
    S, SW = u.shape
    nslab, _, ns = bbr.shape
    T = S // SEGMENTS
    assert T & (T - 1) == 0
    rc, nc = _chunks(S)

    def body(u_ref, br_ref, bi_ref, ar_ref, ai_ref, cr_ref, ci_ref, d_ref, y_ref, sr, si):
        def inputs(c, carry):
            r = pl.ds(pl.multiple_of(c * rc, rc), rc)
            sr[r, :] = jnp.dot(u_ref[r, :], br_ref[...], preferred_element_type=F32)
            si[r, :] = jnp.dot(u_ref[r, :], bi_ref[...], preferred_element_type=F32)
            return carry

        lax.fori_loop(0, nc, inputs, 0)
        _scan_states(sr, si, ar_ref[...], ai_ref[...], T, False)

        def outputs(c, carry):
            r = pl.ds(pl.multiple_of(c * rc, rc), rc)
            y_ref[r, :] = (jnp.dot(sr[r, :].astype(BF16), cr_ref[...], preferred_element_type=F32)
                           - jnp.dot(si[r, :].astype(BF16), ci_ref[...], preferred_element_type=F32)
                           + d_ref[...] * u_ref[r, :].astype(F32))
            return carry

        lax.fori_loop(0, nc, outputs, 0)

    slab = pl.BlockSpec((S, LANES), lambda g: (0, g))
    return pl.pallas_call(
        body, out_shape=jax.ShapeDtypeStruct((S, SW), F32), grid=(nslab,),
        in_specs=[slab] + _slab_specs(ns), out_specs=slab,
        scratch_shapes=[pltpu.VMEM((S, ns), F32)] * 2,
        compiler_params=_params(("parallel",), VMEM_LIMIT_SCAN), name=name)(u, bbr, bbi, ar, ai, cbr, cbi, dsk)


def ssm_bwd(u, dy, bbr, bbi, ar, ai, cbr, cbi, dsk, *, name):
    S, SW = u.shape
    nslab, _, ns = bbr.shape
    T = S // SEGMENTS
    rc, nc = _chunks(S)

    def body(u_ref, dy_ref, br_ref, bi_ref, ar_ref, ai_ref, cr_ref, ci_ref, d_ref,
             du_ref, dbr_ref, dbi_ref, dcr_ref, dci_ref, dar_ref, dai_ref, sr, si, lr, li):
        def inputs(c, carry):
            r = pl.ds(pl.multiple_of(c * rc, rc), rc)
            ub = u_ref[r, :]
            gb = dy_ref[r, :].astype(BF16)
            sr[r, :] = jnp.dot(ub, br_ref[...], preferred_element_type=F32)
            si[r, :] = jnp.dot(ub, bi_ref[...], preferred_element_type=F32)
            lr[r, :] = lax.dot_general(gb, cr_ref[...], _NT, preferred_element_type=F32)
            li[r, :] = -lax.dot_general(gb, ci_ref[...], _NT, preferred_element_type=F32)
            return carry

        lax.fori_loop(0, nc, inputs, 0)
        ar, ai = ar_ref[...], ai_ref[...]
        s0r, s0i = _scan_states(sr, si, ar, ai, T, False)
        _scan_states(lr, li, ar, -ai, T, True)

        def pair(k, c):
            now = pl.ds(pl.multiple_of(k * SEGMENTS, SEGMENTS), SEGMENTS)
            prev = pl.ds(pl.multiple_of((k - 1) * SEGMENTS, SEGMENTS), SEGMENTS)
            return (c[0] + lr[now, :] * sr[prev, :] + li[now, :] * si[prev, :],
                    c[1] - lr[now, :] * si[prev, :] + li[now, :] * sr[prev, :])

        first = pl.ds(0, SEGMENTS)
        acc = (lr[first, :] * s0r + li[first, :] * s0i, -lr[first, :] * s0i + li[first, :] * s0r)
        acc = lax.fori_loop(1, T, pair, acc, unroll=4)
        dar_ref[...] = jnp.sum(acc[0], axis=0, keepdims=True)
        dai_ref[...] = jnp.sum(acc[1], axis=0, keepdims=True)

        dbr_ref[...] = jnp.zeros_like(dbr_ref)
        dbi_ref[...] = jnp.zeros_like(dbi_ref)
        dcr_ref[...] = jnp.zeros_like(dcr_ref)
        dci_ref[...] = jnp.zeros_like(dci_ref)

        def outputs(c, carry):
            r = pl.ds(pl.multiple_of(c * rc, rc), rc)
            ub = u_ref[r, :]
            g = dy_ref[r, :]
            gb = g.astype(BF16)
            lrb = lr[r, :].astype(BF16)
            lib = li[r, :].astype(BF16)
            du_ref[r, :] = (lax.dot_general(lrb, br_ref[...], _NT, preferred_element_type=F32)
                            + lax.dot_general(lib, bi_ref[...], _NT, preferred_element_type=F32)
                            + d_ref[...] * g).astype(BF16)
            dbr_ref[...] += lax.dot_general(ub, lrb, _TN, preferred_element_type=F32)
            dbi_ref[...] += lax.dot_general(ub, lib, _TN, preferred_element_type=F32)
            dcr_ref[...] += lax.dot_general(sr[r, :].astype(BF16), gb, _TN, preferred_element_type=F32)
            dci_ref[...] -= lax.dot_general(si[r, :].astype(BF16), gb, _TN, preferred_element_type=F32)
            return carry

        lax.fori_loop(0, nc, outputs, 0)

    slab = pl.BlockSpec((S, LANES), lambda g: (0, g))
    bspec = pl.BlockSpec((None, LANES, ns), lambda g: (g, 0, 0))
    cspec = pl.BlockSpec((None, ns, LANES), lambda g: (g, 0, 0))
    aspec = pl.BlockSpec((None, 1, ns), lambda g: (g, 0, 0))
    return pl.pallas_call(
        body,
        out_shape=[jax.ShapeDtypeStruct((S, SW), BF16),
                   jax.ShapeDtypeStruct((nslab, LANES, ns), F32), jax.ShapeDtypeStruct((nslab, LANES, ns), F32),
                   jax.ShapeDtypeStruct((nslab, ns, LANES), F32), jax.ShapeDtypeStruct((nslab, ns, LANES), F32),
                   jax.ShapeDtypeStruct((nslab, 1, ns), F32), jax.ShapeDtypeStruct((nslab, 1, ns), F32)],
        grid=(nslab,), in_specs=[slab, slab] + _slab_specs(ns),
        out_specs=[slab, bspec, bspec, cspec, cspec, aspec, aspec],
        scratch_shapes=[pltpu.VMEM((S, ns), F32)] * 4,
        compiler_params=_params(("parallel",), VMEM_LIMIT_SCAN), name=name)(u, dy, bbr, bbi, ar, ai, cbr, cbi, dsk)


def _discretise(lam_re, lam_im, log_dt, b_re, b_im):
    dt = jnp.exp(log_dt)[:, None]
    mag = jnp.exp(lam_re * dt)
    ar = mag * jnp.cos(lam_im * dt)
    ai = mag * jnp.sin(lam_im * dt)
    nr, ni = ar - 1.0, ai
    den = lam_re * lam_re + lam_im * lam_im
    cr = ((nr * lam_re + ni * lam_im) / den)[..., None]
    ci = ((ni * lam_re - nr * lam_im) / den)[..., None]
    return ar, ai, cr * b_re - ci * b_im, cr * b_im + ci * b_re


def _block_diag(t, nslab):
    G, R, C = t.shape
    eye = jnp.eye(SLAB_GROUPS, dtype=t.dtype)
    t = t.reshape(nslab, SLAB_GROUPS, R, C)
    return jnp.einsum('sgrc,gh->sgrhc', t, eye).reshape(nslab, SLAB_GROUPS * R, SLAB_GROUPS * C)


def _block_diag_part(t, R, C):
    nslab = t.shape[0]
    eye = jnp.eye(SLAB_GROUPS, dtype=t.dtype)
    t = t.reshape(nslab, SLAB_GROUPS, R, SLAB_GROUPS, C)
    return jnp.einsum('sgrhc,gh->sgrc', t, eye).reshape(nslab * SLAB_GROUPS, R, C)


def _place():
    return lax.axis_index("x"), lax.axis_index("y"), lax.axis_index("c")


def all_gather(shards, *, name):
    nw = len(shards)

    def body(*refs):
        ins, outs = refs[:nw], refs[nw:2 * nw]
        send_sems, recv_sems, local_sems = refs[2 * nw:]
        x, y, c = _place()
        me, sibling = (x, y, c), (x, y, 1 - c)
        chips = [(1 - x, y), (x, 1 - y), (1 - x, 1 - y)]

        def copy(w, k, block, to, own):
            px, py, pc = block
            slot = outs[w].at[4 * px + 2 * py + pc]
            return pltpu.make_async_remote_copy(
                src_ref=ins[w] if own else slot, dst_ref=slot, send_sem=send_sems.at[w, k],
                recv_sem=recv_sems.at[w, k], device_id=to, device_id_type=MESH)

        mine = [pltpu.make_async_copy(ins[w], outs[w].at[4 * x + 2 * y + c], local_sems.at[w]) for w in range(nw)]
        for cp in mine:
            cp.start()
        first = []
        for w in range(nw):
            first.append(copy(w, 0, me, sibling, True))
            first += [copy(w, 1 + j, me, (*chip, c), True) for j, chip in enumerate(chips)]
        for cp in first:
            cp.start()
        passed = []
        for j, chip in enumerate(chips):
            for w in range(nw):
                copy(w, 1 + j, (*chip, c), me, False).wait_recv()
                cp = copy(w, 4 + j, (*chip, c), sibling, False)
                cp.start()
                passed.append(cp)
        for w in range(nw):
            copy(w, 0, sibling, me, False).wait_recv()
            for j, chip in enumerate(chips):
                copy(w, 4 + j, (*chip, 1 - c), me, False).wait_recv()
        for cp in first + passed:
            cp.wait_send()
        for cp in mine:
            cp.wait()

    anyspec = pl.BlockSpec(memory_space=pl.ANY)
    return pl.pallas_call(
        body, out_shape=[jax.ShapeDtypeStruct((N_DEV,) + s.shape, s.dtype) for s in shards],
        in_specs=[anyspec] * nw, out_specs=[anyspec] * nw,
        scratch_shapes=[pltpu.SemaphoreType.DMA((nw, 7)), pltpu.SemaphoreType.DMA((nw, 7)),
                        pltpu.SemaphoreType.DMA((nw,))],
        compiler_params=pltpu.CompilerParams(has_side_effects=True), name=name)(*shards)


_HBM = pl.BlockSpec(memory_space=pltpu.HBM)
_SEM = pl.BlockSpec(memory_space=pltpu.SEMAPHORE)
_ORDERED_EFFECT = pltpu.SideEffectType.DATAFLOW_SIDE_EFFECTING


def _split_call(name, bufs, sems_in, n_new, body_fn, after):
    nb, ns, nn = len(bufs), len(sems_in), len(n_new)

    def body(*refs):
        outs = refs[nb + ns + 1:]
        body_fn(refs[:nb], refs[nb:nb + ns], outs[:nn])
        outs[nn + nb][...] = jnp.zeros((SUBLANES, LANES), F32)

    res = pl.pallas_call(
        body, name=name,
        out_shape=([pltpu.SemaphoreType.DMA((n,)) for n in n_new] + [pltpu.HBM(b.shape, b.dtype) for b in bufs]
                   + [jax.ShapeDtypeStruct((SUBLANES, LANES), F32)]),
        in_specs=[_HBM] * nb + [_SEM] * ns + [_ANY],
        out_specs=[_SEM] * nn + [_HBM] * nb + [pl.BlockSpec(memory_space=pltpu.VMEM)],
        input_output_aliases={i: nn + i for i in range(nb)},
        compiler_params=pltpu.CompilerParams(has_side_effects=_ORDERED_EFFECT))(
            *[pltpu.with_memory_space_constraint(b, pltpu.HBM) for b in bufs], *sems_in, after)
    return list(res[:nn]), list(res[nn:nn + nb]), res[-1]


def _mesh_peers():
    x, y, c = _place()
    return x, y, c, (x, y, 1 - c), [(1 - x, y), (x, 1 - y), (1 - x, 1 - y)]


def gather_start(shards, after, *, name):
    nw = len(shards)
    x, y, c = _place()
    zones = [lax.dynamic_update_slice(lax.empty((N_DEV,) + s.shape, s.dtype), s[None], (4 * x + 2 * y + c, 0, 0))
             for s in shards]

    def body(bufs, taken, new):
        for cp in _gather_first(bufs, nw, new[0], new[1]):
            cp.start()

    sems, bufs, token = _split_call(name, list(shards) + zones, [], [4 * nw, 4 * nw], body, after)
    return nw, sems, bufs, token


def _gather_first(bufs, nw, send, recv):
    x, y, c, sibling, chips = _mesh_peers()
    out = []
    for w in range(nw):
        slot = bufs[nw + w].at[4 * x + 2 * y + c]
        for k, to in enumerate([sibling] + [(*ch, c) for ch in chips]):
            out.append(pltpu.make_async_remote_copy(
                src_ref=bufs[w], dst_ref=slot, send_sem=send.at[4 * w + k], recv_sem=recv.at[4 * w + k],
                device_id=to, device_id_type=MESH))
    return out


def _gather_slot_copy(bufs, nw, w, block, send_sem, recv_sem, to):
    px, py, pc = block
    slot = bufs[nw + w].at[4 * px + 2 * py + pc]
    return pltpu.make_async_remote_copy(src_ref=slot, dst_ref=slot, send_sem=send_sem, recv_sem=recv_sem,
                                        device_id=to, device_id_type=MESH)


def gather_forward(state, after, *, name):
    nw, sems, bufs, _ = state

    def body(bufs, taken, new):
        x, y, c, sibling, chips = _mesh_peers()
        for j, ch in enumerate(chips):
            for w in range(nw):
                k = 4 * w + 1 + j
                _gather_slot_copy(bufs, nw, w, (*ch, c), taken[0].at[k], taken[1].at[k], (*ch, c)).wait_recv()
                _gather_slot_copy(bufs, nw, w, (*ch, c), new[0].at[3 * w + j], new[1].at[3 * w + j], sibling).start()
        for w in range(nw):
            _gather_slot_copy(bufs, nw, w, sibling, taken[0].at[4 * w], taken[1].at[4 * w], sibling).wait_recv()
        for cp in _gather_first(bufs, nw, taken[0], taken[1]):
            cp.wait_send()

    sems, bufs, token = _split_call(name, bufs, sems, [3 * nw, 3 * nw], body, after)
    return nw, sems, bufs, token


def gather_finish(state, after, *, name):
    nw, sems, bufs, _ = state

    def body(bufs, taken, new):
        x, y, c, sibling, chips = _mesh_peers()
        for w in range(nw):
            for j, ch in enumerate(chips):
                cp = _gather_slot_copy(bufs, nw, w, (*ch, 1 - c), taken[0].at[3 * w + j], taken[1].at[3 * w + j], sibling)
                cp.wait_send()
                cp.wait_recv()

    _, bufs, _ = _split_call(name, bufs, sems, [], body, after)
    return bufs[nw:]


def exchange_start(srcs, zone_shapes, copies, n, after, *, name):
    nw = len(srcs)
    zones = [lax.empty(z, s.dtype) for z, s in zip(zone_shapes, srcs)]

    def body(bufs, taken, new):
        for cp in copies(bufs[:nw], bufs[nw:], new[0], new[1]):
            cp.start()

    sems, bufs, token = _split_call(name, list(srcs) + zones, [], [n, n], body, after)
    return nw, copies, sems, bufs, token


def exchange_wait(state, after, *, name):
    nw, copies, sems, bufs, _ = state

    def body(bufs, taken, new):
        for cp in copies(bufs[:nw], bufs[nw:], taken[0], taken[1]):
            cp.wait_send()
            cp.wait_recv()

    _, bufs, _ = _split_call(name, bufs, sems, [], body, after)
    return bufs[nw:]


def _core_copies(srcs, zones, send, recv):
    x, y, c = _place()
    return [pltpu.make_async_remote_copy(
        src_ref=srcs[w].at[:, 1 - c], dst_ref=zones[w], send_sem=send.at[w], recv_sem=recv.at[w],
        device_id=(x, y, 1 - c), device_id_type=MESH) for w in range(len(srcs))]


def _chip_copies(srcs, zones, send, recv):
    x, y, c = _place()
    chips = [(1 - x, y), (x, 1 - y), (1 - x, 1 - y)]
    return [pltpu.make_async_remote_copy(
        src_ref=srcs[w].at[2 * cx + cy], dst_ref=zones[w].at[j], send_sem=send.at[3 * w + j],
        recv_sem=recv.at[3 * w + j], device_id=(cx, cy, c), device_id_type=MESH)
        for w in range(len(srcs)) for j, (cx, cy) in enumerate(chips)]


def _blocked(fn, ins, outs, *, name, place=None, tr=256):
    k, n = outs[0][0]
    tr = _tile(k, tr, 16)
    if place is None:
        place = jnp.zeros((1,), jnp.int32)
    specs = []
    args = []
    for a in ins:
        if isinstance(a, tuple):
            arr, lead = a
            specs.append(pl.BlockSpec((None, tr, n), functools.partial(lambda i, s, lead: (*lead(i, s), 0), lead=lead)))
            args.append(arr)
        else:
            specs.append(pl.BlockSpec((tr, n), lambda i, s: (i, 0)))
            args.append(a)
    nin = len(args)

    def body(place_ref, *refs):
        res = fn(*[r[...] for r in refs[:nin]])
        for ref, val in zip(refs[nin:], res):
            ref[...] = val.astype(ref.dtype)

    return pl.pallas_call(
        body, out_shape=[jax.ShapeDtypeStruct(s, d) for s, d in outs],
        grid_spec=pltpu.PrefetchScalarGridSpec(
            num_scalar_prefetch=1, grid=(k // tr,), in_specs=specs,
            out_specs=[pl.BlockSpec((tr, n), lambda i, s: (i, 0)) for _ in outs]),
        compiler_params=_params(("parallel",)), name=name)(place, *args)


def _adamw(w, g, m, v):
    m = ADAM_B1 * m + (1.0 - ADAM_B1) * g
    v = ADAM_B2 * v + (1.0 - ADAM_B2) * (g * g)
    m_hat = m / (1.0 - ADAM_B1 ** ADAM_STEP)
    v_hat = v / (1.0 - ADAM_B2 ** ADAM_STEP)
    delta = -ADAM_LR * (m_hat / (jnp.sqrt(v_hat) + ADAM_EPS) + ADAM_WD * w)
    return delta, m, v


def kernel(x, p, mix_norm_pre, w_in, lam_re, lam_im, log_dt, ssm_b_re, ssm_b_im, ssm_c_re, ssm_c_im, ssm_d, w_glu, b_glu, attn_out_norm, ssm_out_norm, w_out, mix_norm_post, mlp_norm_pre, w_up, w_down, mlp_norm_post, ple_norm_pre, w_ple_gate, w_ple_proj, ple_norm_post, loss_target, m_mix_norm_pre, m_w_in, m_lam_re, m_lam_im, m_log_dt, m_ssm_b_re, m_ssm_b_im, m_ssm_c_re, m_ssm_c_im, m_ssm_d, m_w_glu, m_b_glu, m_attn_out_norm, m_ssm_out_norm, m_w_out, m_mix_norm_post, m_mlp_norm_pre, m_w_up, m_w_down, m_mlp_norm_post, m_ple_norm_pre, m_w_ple_gate, m_w_ple_proj, m_ple_norm_post, v_mix_norm_pre, v_w_in, v_lam_re, v_lam_im, v_log_dt, v_ssm_b_re, v_ssm_b_im, v_ssm_c_re, v_ssm_c_im, v_ssm_d, v_w_glu, v_b_glu, v_attn_out_norm, v_ssm_out_norm, v_w_out, v_mix_norm_post, v_mlp_norm_pre, v_w_up, v_w_down, v_mlp_norm_post, v_ple_norm_pre, v_w_ple_gate, v_w_ple_proj, v_ple_norm_post):
    weights = dict(mix_norm_pre=mix_norm_pre, w_in=w_in, lam_re=lam_re, lam_im=lam_im, log_dt=log_dt, ssm_b_re=ssm_b_re, ssm_b_im=ssm_b_im, ssm_c_re=ssm_c_re, ssm_c_im=ssm_c_im, ssm_d=ssm_d, w_glu=w_glu, b_glu=b_glu, attn_out_norm=attn_out_norm, ssm_out_norm=ssm_out_norm, w_out=w_out, mix_norm_post=mix_norm_post, mlp_norm_pre=mlp_norm_pre, w_up=w_up, w_down=w_down, mlp_norm_post=mlp_norm_post, ple_norm_pre=ple_norm_pre, w_ple_gate=w_ple_gate, w_ple_proj=w_ple_proj, ple_norm_post=ple_norm_post)
    mom_m = dict(mix_norm_pre=m_mix_norm_pre, w_in=m_w_in, lam_re=m_lam_re, lam_im=m_lam_im, log_dt=m_log_dt, ssm_b_re=m_ssm_b_re, ssm_b_im=m_ssm_b_im, ssm_c_re=m_ssm_c_re, ssm_c_im=m_ssm_c_im, ssm_d=m_ssm_d, w_glu=m_w_glu, b_glu=m_b_glu, attn_out_norm=m_attn_out_norm, ssm_out_norm=m_ssm_out_norm, w_out=m_w_out, mix_norm_post=m_mix_norm_post, mlp_norm_pre=m_mlp_norm_pre, w_up=m_w_up, w_down=m_w_down, mlp_norm_post=m_mlp_norm_post, ple_norm_pre=m_ple_norm_pre, w_ple_gate=m_w_ple_gate, w_ple_proj=m_w_ple_proj, ple_norm_post=m_ple_norm_post)
    mom_v = dict(mix_norm_pre=v_mix_norm_pre, w_in=v_w_in, lam_re=v_lam_re, lam_im=v_lam_im, log_dt=v_log_dt, ssm_b_re=v_ssm_b_re, ssm_b_im=v_ssm_b_im, ssm_c_re=v_ssm_c_re, ssm_c_im=v_ssm_c_im, ssm_d=v_ssm_d, w_glu=v_w_glu, b_glu=v_b_glu, attn_out_norm=v_attn_out_norm, ssm_out_norm=v_ssm_out_norm, w_out=v_w_out, mix_norm_post=v_mix_norm_post, mlp_norm_pre=v_mlp_norm_pre, w_up=v_w_up, w_down=v_w_down, mlp_norm_post=v_mlp_norm_post, ple_norm_pre=v_ple_norm_pre, w_ple_gate=v_w_ple_gate, w_ple_proj=v_w_ple_proj, ple_norm_post=v_ple_norm_post)
    order = list(weights)
    big = ["w_in", "w_glu", "w_out", "w_up", "w_down", "w_ple_gate", "w_ple_proj"]
    col_sharded = {"w_in", "w_up", "w_ple_proj"}
    small = [n for n in order if n not in big]

    _, S, D = x.shape
    xs = x[0]
    tgt = loss_target[0]
    AW = attn_out_norm.shape[1]
    SW = ssm_d.shape[1]
    H = AW // HEAD_DIM
    G = SW // SSM_GROUP
    nslab = G // SLAB_GROUPS
    P_, C_ = SSM_STATE, SSM_GROUP

    shard = {n: weights[n][0].astype(BF16) for n in big}
    W, WT = {}, {}

    def arrived(names, gathered):
        for n, g in zip(names, gathered):
            W[n] = g if n in col_sharded else g.reshape(1, N_DEV * g.shape[1], g.shape[2])

    def transposed(g):
        return jnp.swapaxes(g, 1, 2).reshape(1, g.shape[0] * g.shape[2], g.shape[1])

    arrived(["w_in"], all_gather([shard["w_in"]], name="gather_w_in"))
    WT["w_in"] = transposed(W["w_in"])
    early, late = ["w_glu", "w_out"], ["w_up", "w_down", "w_ple_gate", "w_ple_proj"]
    gather_early = gather_start([shard[n] for n in early], W["w_in"], name="gather_early_start")
    gather_late = gather_start([shard[n] for n in late], gather_early[-1], name="gather_late_start")

    g1, g2, g3, g4, g5, g6 = (weights[n] for n in ("mix_norm_pre", "mix_norm_post", "mlp_norm_pre",
                                                      "mlp_norm_post", "ple_norm_pre", "ple_norm_post"))
    ga, gs = attn_out_norm, ssm_out_norm
    (hn1,) = rowwise(lambda a, g: (_rms(a, g),), [xs], [g1], [(D, BF16)], deps=(gather_late[-1],), name="norm_in")
    (proj,) = mm_nn(hn1, W["w_in"], [F32], name="proj_in")
    attn, lse = attn_fwd(proj, H, name="attn_fwd")
    gather_early = gather_forward(gather_early, attn, name="gather_early_forward")
    (mix_a,) = rowwise(lambda a, g: (_rms(a, g),), [attn], [ga], [(AW, BF16)], deps=(gather_early[-1],),
                       name="attn_norm")
    arrived(early, gather_finish(gather_early, mix_a, name="gather_early_finish"))

    a_r, a_i, bb_r, bb_i = _discretise(lam_re[0], lam_im[0], log_dt[0], ssm_b_re[0], ssm_b_im[0])
    ssm_consts = (_block_diag(bb_r.swapaxes(1, 2), nslab).astype(BF16), _block_diag(bb_i.swapaxes(1, 2), nslab).astype(BF16),
                  a_r.reshape(nslab, 1, SLAB_STATES), a_i.reshape(nslab, 1, SLAB_STATES),
                  _block_diag(ssm_c_re[0].swapaxes(1, 2), nslab).astype(BF16),
                  _block_diag(ssm_c_im[0].swapaxes(1, 2), nslab).astype(BF16), ssm_d)
    u_seg = _to_segments(proj[:, 3 * AW:]).astype(BF16)
    y_pre = ssm_fwd(u_seg, *ssm_consts, name="ssm_fwd")
    gather_late = gather_forward(gather_late, y_pre, name="gather_late_forward")
    (yg,) = rowwise(lambda a: (_gelu(a),), [y_pre], [], [(SW, BF16)], deps=(gather_late[-1],), name="ssm_gelu")
    (gl1,) = mm_nn(yg, W["w_glu"], [F32], epi=lambda acc, b: (acc + b,), bias=b_glu, name="glu_gate")
    (mix_s,) = rowwise(lambda yp, gl, g: (_rms(_gelu(yp) * _sigmoid(gl), g),), [y_pre, gl1], [gs], [(SW, BF16)],
                       name="ssm_glu_norm")
    mixed = jnp.concatenate([mix_a, _from_segments(mix_s)], axis=1)
    (mo,) = mm_nn(mixed, W["w_out"], [F32], name="mix_out")

    def resid_norm(h, t, gpost, gpre):
        hh = h + _rms(t, gpost)
        return hh, _rms(hh, gpre)

    h1, hn2 = rowwise(resid_norm, [xs, mo], [g2, g3], [(D, F32), (D, BF16)], name="resid_mix")
    arrived(late, gather_finish(gather_late, hn2, name="gather_late_finish"))
    WT["w_up"] = transposed(W["w_up"])

    def relu2(acc):
        r = jnp.maximum(acc, 0.0)
        return acc, r * r

    up, act = mm_nn(hn2, W["w_up"], [BF16, BF16], epi=relu2, name="mlp_up")
    (ff,) = mm_nn(act, W["w_down"], [F32], name="mlp_down")
    h2, hn3 = rowwise(resid_norm, [h1, ff], [g4, g5], [(D, F32), (D, BF16)], name="resid_mlp")
    (gl2,) = mm_nn(hn3, W["w_ple_gate"], [F32], name="ple_gate")
    pb = p[0, 0].astype(BF16)
    (emb,) = mm_nn(pb, W["w_ple_proj"], [F32], name="ple_proj")

    def head(h, gl, e, t, g):
        sg = _sigmoid(gl)
        ge = sg * e
        err = h + _rms(ge, g) - t
        dh = err * (1.0 / D)
        dge, dg = _rms_bwd(dh, ge, g)
        return dh, dge * e * sg * (1.0 - sg), dge * sg, jnp.sum(err * err, axis=0, keepdims=True), dg

    dh3, dgl2, demb, loss_part, dg6 = rowwise(head, [h2, gl2, emb, tgt], [g6], [(D, F32), (D, BF16), (D, BF16)],
                                             [D, D], name="ple_loss_head")
    loss = lax.psum(0.5 / D * jnp.sum(loss_part), ("x", "y", "c"))

    x_i, y_i, c_i = _place()
    place = jnp.stack([c_i, 2 * x_i + y_i]).astype(jnp.int32)
    grads, out_g, out_d, out_m, out_v = {}, {}, {}, {}, {}

    def to_sibling(names, after, tag):
        chunks = []
        for n in names:
            g = grads[n]
            g = g if n in col_sharded else g.reshape(N_DEV, g.shape[1] // N_DEV, g.shape[2])
            chunks.append(g.reshape(4, 2, g.shape[1], g.shape[2]))
        return chunks, exchange_start(chunks, [(4,) + g.shape[2:] for g in chunks], _core_copies, len(chunks), after,
                                      name=f"grads_to_sibling_{tag}")

    def to_chips(names, sent, after, tag):
        chunks, state = sent
        sums = []
        for n, g, r in zip(names, chunks, exchange_wait(state, after, name=f"grads_from_sibling_{tag}")):
            k, nn = g.shape[2], g.shape[3]
            kb = k // _tile(k, 512, 16)

            def mine(i, s, kb=kb):
                return 2 * (i // kb) + s[0], i % kb

            (s,) = _blocked(lambda a, b: (a.astype(F32) + b.astype(F32),),
                            [(g.reshape(N_DEV, k, nn), mine), r.reshape(4 * k, nn)],
                            [((4 * k, nn), BF16)], place=place, tr=k // kb, name=f"chip_sum_{n}")
            sums.append(s.reshape(4, k, nn))
        return sums, exchange_start(sums, [(3,) + s.shape[1:] for s in sums], _chip_copies, 3 * len(sums), sums[-1],
                                    name=f"grads_to_chips_{tag}")

    def update(w_, m_, v_, own, r0, r1, r2):
        g = own.astype(F32) + r0.astype(F32) + r1.astype(F32) + r2.astype(F32)
        return (g,) + _adamw(w_, g, m_, v_)

    def finish(names, sent, after, tag):
        sums, state = sent
        for n, s, r in zip(names, sums, exchange_wait(state, after, name=f"grads_from_chips_{tag}")):
            shp = weights[n].shape
            res = _blocked(update, [weights[n][0], mom_m[n][0], mom_v[n][0], (s, lambda i, p_: (p_[1], i)),
                                    (r, lambda i, p_: (0, i)), (r, lambda i, p_: (1, i)), (r, lambda i, p_: (2, i))],
                           [(shp[1:], F32)] * 4, place=place, name=f"adamw_{n}")
            out_g[n], out_d[n], out_m[n], out_v[n] = (t.reshape(shp) for t in res)
        return out_v[names[-1]]

    grads["w_ple_proj"] = mm_tn(pb, demb, N_DEV, name="grad_w_ple_proj")
    dhn3 = mm_nt(dgl2, W["w_ple_gate"], F32, name="back_ple_gate")
    grads["w_ple_gate"] = mm_tn(hn3, dgl2, 1, name="grad_w_ple_gate")

    def back_resid(dh, dhn, h, t, gpre, gpost):
        d1, dgpre = _rms_bwd(dhn, h, gpre)
        dhh = dh + d1
        dt, dgpost = _rms_bwd(dhh, t, gpost)
        return dhh, dt, dgpre, dgpost

    dh2, dff, dg5, dg4 = rowwise(back_resid, [dh3, dhn3, h2, ff], [g5, g4], [(D, F32), (D, BF16)], [D, D],
                                 name="back_resid_mlp")
    dup = mm_nt(dff, W["w_down"], BF16, epi=lambda acc, u_: (acc * 2.0 * jnp.maximum(u_.astype(F32), 0.0),),
                extra=up, name="back_mlp_down")
    grads["w_down"] = mm_tn(act, dff, 1, name="grad_w_down")
    group_a = ["w_ple_proj", "w_ple_gate", "w_down"]
    sent_a = to_sibling(group_a, grads["w_down"], "a")
    (dhn2,) = mm_nn(dup, WT["w_up"], [F32], deps=(sent_a[1][-1],), name="back_mlp_up")
    sent_a = to_chips(group_a, sent_a, dhn2, "a")
    grads["w_up"] = mm_tn(hn2, dup, N_DEV, deps=(sent_a[1][-1],), name="grad_w_up")
    dh1, dmo, dg3, dg2 = rowwise(back_resid, [dh2, dhn2, h1, mo], [g3, g2], [(D, F32), (D, BF16)], [D, D],
                                 name="back_resid_mix")
    dmixed = mm_nt(dmo, W["w_out"], F32, name="back_mix_out")
    grads["w_out"] = mm_tn(mixed, dmo, 1, name="grad_w_out")

    def back_glu(dm, yp, gl, g):
        ygf = _gelu(yp)
        sg = _sigmoid(gl)
        dssm, dg = _rms_bwd(dm, ygf * sg, g)
        dgl = dssm * ygf * sg * (1.0 - sg)
        return dgl, dssm * sg, dg, jnp.sum(dgl, axis=0, keepdims=True)

    dgl1, dyg_direct, dgs, db_glu = rowwise(back_glu, [_to_segments(dmixed[:, AW:]), y_pre, gl1], [gs],
                                            [(SW, BF16), (SW, F32)], [SW, SW], name="back_glu")
    dyg_gate = mm_nt(dgl1, W["w_glu"], F32, name="back_glu_gate")
    grads["w_glu"] = mm_tn(yg, dgl1, 1, name="grad_w_glu")
    group_b = ["w_up", "w_out", "w_glu"]
    sent_b = to_sibling(group_b, grads["w_glu"], "b")
    done_a = finish(group_a, sent_a, sent_b[1][-1], "a")

    def back_gelu(d1, d2, yp, u_):
        dy = (d1 + d2) * _gelu_grad(yp)
        return dy, jnp.sum(dy * u_.astype(F32), axis=0, keepdims=True)

    dy_pre, d_skip = rowwise(back_gelu, [dyg_direct, dyg_gate, y_pre, u_seg], [], [(SW, F32)], [SW], deps=(done_a,),
                             name="back_gelu")
    du_seg, dbb_r, dbb_i, dcb_r, dcb_i, da_r, da_i = ssm_bwd(u_seg, dy_pre, *ssm_consts, name="ssm_bwd")
    sent_b = to_chips(group_b, sent_b, du_seg, "b")

    def back_attn_norm(dm, a, g):
        da, dg = _rms_bwd(dm, a, g)
        prod = da * a
        delta = jnp.concatenate(
            [jnp.broadcast_to(jnp.sum(prod[:, h * HEAD_DIM:(h + 1) * HEAD_DIM], axis=-1, keepdims=True),
                              (prod.shape[0], HEAD_DIM)) for h in range(H)], axis=1)
        return da, delta, dg

    dattn, delta, dga = rowwise(back_attn_norm, [(dmixed, AW, 0), attn], [ga], [(AW, F32), (AW, F32)], [AW],
                                deps=(sent_b[1][-1],), name="back_attn_norm")
    dq, dk, dv = attn_bwd(proj, dattn, lse, delta, H, name="attn_bwd")
    dproj = jnp.concatenate([dq, dk, dv, _from_segments(du_seg)], axis=1)
    (dhn1,) = mm_nn(dproj, WT["w_in"], [F32], name="back_proj_in")

    def back_in(dh, dhn, a, g):
        d1, dg = _rms_bwd(dhn, a, g)
        return dh + d1, dg

    grad_x, dg1 = rowwise(back_in, [dh1, dhn1, xs], [g1], [(D, F32)], [D], name="back_norm_in")

    cot = dict(
        mix_norm_pre=dg1, mix_norm_post=dg2, mlp_norm_pre=dg3, mlp_norm_post=dg4, ple_norm_pre=dg5, ple_norm_post=dg6,
        attn_out_norm=dga, ssm_out_norm=dgs, b_glu=db_glu, ssm_d=d_skip,
        ssm_c_re=_block_diag_part(dcb_r, P_, C_).swapaxes(1, 2), ssm_c_im=_block_diag_part(dcb_i, P_, C_).swapaxes(1, 2),
        a_r=da_r.reshape(G, P_), a_i=da_i.reshape(G, P_),
        bb_r=_block_diag_part(dbb_r, C_, P_).swapaxes(1, 2), bb_i=_block_diag_part(dbb_i, C_, P_).swapaxes(1, 2))
    names = list(cot)
    flat = jnp.concatenate([cot[n].reshape(-1) for n in names])
    total = flat.shape[0]
    rows_ = -(-total // (LANES * 16)) * 16
    flat = jnp.pad(flat, (0, rows_ * LANES - total)).reshape(rows_, LANES)
    gather_small = gather_start([flat], flat, name="gather_small_start")
    grads["w_in"] = mm_tn(hn1, dproj, N_DEV, deps=(gather_small[-1],), name="grad_w_in")
    group_c = ["w_in"]
    sent_c = to_sibling(group_c, grads["w_in"], "c")
    done_b = finish(group_b, sent_b, sent_c[1][-1], "b")
    sent_c = to_chips(group_c, sent_c, done_b, "c")
    gather_small = gather_forward(gather_small, sent_c[1][-1], name="gather_small_forward")
    (every,) = gather_finish(gather_small, gather_small[-1], name="gather_small_finish")
    (summed,) = _blocked(lambda *t: (functools.reduce(lambda a, b: a + b, t),),
                         [(every, functools.partial(lambda i, p_, j: (j, i), j=j)) for j in range(N_DEV)],
                         [((rows_, LANES), F32)], name="sum_small_grads")
    summed = summed.reshape(-1)
    red, off = {}, 0
    for n in names:
        sz = cot[n].size
        red[n] = summed[off:off + sz].reshape(cot[n].shape)
        off += sz
    _, pull = jax.vjp(_discretise, lam_re[0], lam_im[0], log_dt[0], ssm_b_re[0], ssm_b_im[0])
    d_lre, d_lim, d_ldt, d_bre, d_bim = pull((red["a_r"], red["a_i"], red["bb_r"], red["bb_i"]))
    red.update(lam_re=d_lre, lam_im=d_lim, log_dt=d_ldt, ssm_b_re=d_bre, ssm_b_im=d_bim)

    def pack(d):
        t = jnp.concatenate([d[n].reshape(-1) for n in small])
        r_ = -(-t.shape[0] // (LANES * 16)) * 16
        return jnp.pad(t, (0, r_ * LANES - t.shape[0])).reshape(r_, LANES)

    sw, sg_, sm, sv = pack(weights), pack(red), pack(mom_m), pack(mom_v)
    sd, snm, snv = _blocked(lambda w_, g_, m_, v_: _adamw(w_, g_, m_, v_), [sw, sg_, sm, sv],
                            [(sw.shape, F32)] * 3, name="adamw_small")
    finish(group_c, sent_c, snv, "c")
    off = 0
    for n in small:
        sz = weights[n].size
        shp = weights[n].shape
        out_g[n] = red[n].reshape(shp)
        out_d[n] = sd.reshape(-1)[off:off + sz].reshape(shp)
        out_m[n] = snm.reshape(-1)[off:off + sz].reshape(shp)
        out_v[n] = snv.reshape(-1)[off:off + sz].reshape(shp)
        off += sz

    return (loss, grad_x[None], *[out_g[n] for n in order], *[out_d[n] for n in order],
            *[out_m[n] for n in order], *[out_v[n] for n in order])
```

```python
import functools
import math

import jax
import jax.numpy as jnp
from jax import lax
from jax.experimental import pallas as pl
from jax.experimental.pallas import tpu as pltpu

F32 = jnp.float32
BF16 = jnp.bfloat16
MESH = pl.DeviceIdType.MESH

N_DEV = 8
LANES = 128
SUBLANES = 8
VMEM_LIMIT = 48 * 1024 * 1024
VMEM_LIMIT_SCAN = 60 * 1024 * 1024

HEAD_DIM = 128
BLK = 128
DILATIONS = (1, 4, 16)
SSM_GROUP = 16
SSM_STATE = 64
SLAB_GROUPS = LANES // SSM_GROUP
SLAB_STATES = SLAB_GROUPS * SSM_STATE
SEGMENTS = SUBLANES
SCAN_UNROLL = 4
RMS_EPS = 1e-6
NEG_INF = -1e30

ADAM_LR = 0.001
ADAM_B1 = 0.9
ADAM_B2 = 0.999
ADAM_EPS = 1e-08
ADAM_WD = 0.01
ADAM_STEP = 10


def _tile(n, pref, unit=LANES):
    if n <= pref:
        return n
    t = (pref // unit) * unit
    while t > unit and n % t:
        t -= unit
    assert n % t == 0, (n, pref, unit)
    return t


def _params(sem=None, vmem=VMEM_LIMIT):
    return pltpu.CompilerParams(dimension_semantics=sem, vmem_limit_bytes=vmem)


_NN = (((1,), (0,)), ((), ()))
_NT = (((1,), (1,)), ((), ()))
_TN = (((0,), (0,)), ((), ()))


_ANY = pl.BlockSpec(memory_space=pl.ANY)


def _mm_call(dims, nk, n_extra, n_dep, n_out, epi, **kw):
    first_out = 2 + n_extra + n_dep
    kw["in_specs"] = list(kw["in_specs"]) + [_ANY] * n_dep

    def single(*refs):
        extra = refs[2:2 + n_extra]
        res = epi(lax.dot_general(refs[0][...], refs[1][...], dims, preferred_element_type=F32),
                  *[e[...] for e in extra])
        for o, r in zip(refs[first_out:first_out + n_out], res):
            o[...] = r.astype(o.dtype)

    if nk == 1:
        kw["scratch_shapes"] = []
        return pl.pallas_call(single, **kw)

    def body(*refs):
        a_ref, b_ref = refs[0], refs[1]
        extra = refs[2:2 + n_extra]
        outs = refs[first_out:first_out + n_out]
        acc = refs[-1]
        k = pl.program_id(2)

        @pl.when(k == 0)
        def _():
            acc[...] = jnp.zeros_like(acc)

        acc[...] += lax.dot_general(a_ref[...], b_ref[...], dims, preferred_element_type=F32)

        @pl.when(k == nk - 1)
        def _():
            res = epi(acc[...], *[e[...] for e in extra])
            for o, r in zip(outs, res):
                o[...] = r.astype(o.dtype)

    return pl.pallas_call(body, **kw)


def _identity_epi(acc):
    return (acc,)


def mm_nn(a, w, out_dtypes, *, name, epi=_identity_epi, bias=None, deps=(), tm=1024, tn=512, tk=2048):
    M, K = a.shape
    J, K2, n = w.shape
    assert K == K2
    tm, tn, tk = _tile(M, tm, 16), _tile(n, tn), _tile(K, tk)
    npj = n // tn
    nk = K // tk
    in_specs = [pl.BlockSpec((tm, tk), lambda i, j, k: (i, k)),
                pl.BlockSpec((None, tk, tn), lambda i, j, k: (j // npj, k, j % npj))]
    args = [a, w]
    if bias is not None:
        in_specs.append(pl.BlockSpec((1, tn), lambda i, j, k: (0, j)))
        args.append(bias)
    return _mm_call(
        _NN, nk, len(args) - 2, len(deps), len(out_dtypes), epi,
        out_shape=[jax.ShapeDtypeStruct((M, J * n), d) for d in out_dtypes],
        grid=(M // tm, J * npj, nk), in_specs=in_specs,
        out_specs=[pl.BlockSpec((tm, tn), lambda i, j, k: (i, j)) for _ in out_dtypes],
        scratch_shapes=[pltpu.VMEM((tm, tn), F32)],
        compiler_params=_params(("parallel", "parallel", "arbitrary")), name=name)(*args, *deps)


def mm_nt(a, w, out_dtype, *, name, epi=_identity_epi, extra=None, tm=1024, tko=512, tnr=2048):
    M, N = a.shape
    J, K, n = w.shape
    assert N == J * n
    tm, tko, tnr = _tile(M, tm, 16), _tile(K, tko), _tile(n, tnr)
    npj = n // tnr
    nk = N // tnr
    in_specs = [pl.BlockSpec((tm, tnr), lambda i, j, k: (i, k)),
                pl.BlockSpec((None, tko, tnr), lambda i, j, k: (k // npj, j, k % npj))]
    args = [a, w]
    if extra is not None:
        in_specs.append(pl.BlockSpec((tm, tko), lambda i, j, k: (i, j)))
        args.append(extra)
    return _mm_call(
        _NT, nk, len(args) - 2, 0, 1, epi,
        out_shape=[jax.ShapeDtypeStruct((M, K), out_dtype)],
        grid=(M // tm, K // tko, nk), in_specs=in_specs,
        out_specs=[pl.BlockSpec((tm, tko), lambda i, j, k: (i, j))],
        scratch_shapes=[pltpu.VMEM((tm, tko), F32)],
        compiler_params=_params(("parallel", "parallel", "arbitrary")), name=name)(*args)[0]


def mm_tn(a, b, J, *, name, deps=(), tko=1024, tn=1024, ts=1024):
    S, K = a.shape
    S2, N = b.shape
    assert S == S2 and N % J == 0
    n = N // J
    tko, tn, ts = _tile(K, tko), _tile(n, tn), _tile(S, ts)
    npj = n // tn
    nk = S // ts
    return _mm_call(
        _TN, nk, 0, len(deps), 1, _identity_epi,
        out_shape=[jax.ShapeDtypeStruct((J, K, n), BF16)],
        grid=(K // tko, J * npj, nk),
        in_specs=[pl.BlockSpec((ts, tko), lambda i, j, k: (k, i)),
                  pl.BlockSpec((ts, tn), lambda i, j, k: (k, j))],
        out_specs=[pl.BlockSpec((None, tko, tn), lambda i, j, k: (j // npj, i, j % npj))],
        scratch_shapes=[pltpu.VMEM((tko, tn), F32)],
        compiler_params=_params(("parallel", "parallel", "arbitrary")), name=name)(a, b, *deps)[0]


def rowwise(fn, rows, vecs, outs, accs=(), *, name, deps=(), ts=256):
    rows = [r if isinstance(r, tuple) else (r, r.shape[1], 0) for r in rows]
    S = rows[0][0].shape[0]
    ts = _tile(S, ts, 16)
    nr, nv, no, nd = len(rows), len(vecs), len(outs), len(deps)

    def body(*refs):
        r, v = refs[:nr], refs[nr:nr + nv]
        o, a = refs[nr + nv + nd:nr + nv + nd + no], refs[nr + nv + nd + no:]
        res = fn(*[t[...] for t in r], *[t[...] for t in v])
        for ref, val in zip(o, res[:no]):
            ref[...] = val.astype(ref.dtype)
        if a:
            @pl.when(pl.program_id(0) == 0)
            def _():
                for ref in a:
                    ref[...] = jnp.zeros_like(ref)

            for ref, val in zip(a, res[no:]):
                ref[...] += val

    in_specs = [pl.BlockSpec((ts, w), functools.partial(lambda i, cb: (i, cb), cb=cb)) for _, w, cb in rows]
    in_specs += [pl.BlockSpec(v.shape, lambda i: (0, 0)) for v in vecs] + [_ANY] * nd
    out_shape = [jax.ShapeDtypeStruct((S, w), d) for w, d in outs]
    out_shape += [jax.ShapeDtypeStruct((1, w), F32) for w in accs]
    out_specs = [pl.BlockSpec((ts, w), lambda i: (i, 0)) for w, _ in outs]
    out_specs += [pl.BlockSpec((1, w), lambda i: (0, 0)) for w in accs]
    return pl.pallas_call(body, out_shape=out_shape, grid=(S // ts,), in_specs=in_specs, out_specs=out_specs,
                          compiler_params=_params(("arbitrary",)), name=name)(*[r[0] for r in rows], *vecs, *deps)


def _rms(x, g):
    r = lax.rsqrt(jnp.mean(x * x, axis=-1, keepdims=True) + RMS_EPS)
    return x * r * g


def _rms_bwd(dy, x, g):
    r = lax.rsqrt(jnp.mean(x * x, axis=-1, keepdims=True) + RMS_EPS)
    xh = x * r
    dxh = dy * g
    dx = r * (dxh - xh * jnp.mean(dxh * xh, axis=-1, keepdims=True))
    return dx, jnp.sum(dy * xh, axis=0, keepdims=True)


def _sigmoid(x):
    return 1.0 / (1.0 + jnp.exp(-x))


_GELU_C = math.sqrt(2.0 / math.pi)


def _gelu(x):
    return 0.5 * x * (1.0 + jnp.tanh(_GELU_C * (x + 0.044715 * x * x * x)))


def _gelu_grad(x):
    t = jnp.tanh(_GELU_C * (x + 0.044715 * x * x * x))
    return 0.5 * (1.0 + t) + 0.5 * x * (1.0 - t * t) * _GELU_C * (1.0 + 3.0 * 0.044715 * x * x)


ATTN_INTERLEAVE = 2
KEY_PAD = BLK * max(DILATIONS)


def _key_mask(n):
    ii = lax.broadcasted_iota(jnp.int32, (BLK, 2 * BLK), 0)
    jj = lax.broadcasted_iota(jnp.int32, (BLK, 2 * BLK), 1)
    return ((jj < BLK) & (jj >= ii) & (n > 0)) | ((jj >= BLK) & (jj - BLK <= ii))


def _block_rows(idx, d, nb):
    r, n = idx // nb, idx % nb
    cur = r + n * (BLK * d)
    keys = cur + (KEY_PAD - BLK * d)
    if d == 1:
        return n, pl.ds(pl.multiple_of(cur, BLK), BLK), pl.ds(pl.multiple_of(keys, BLK), 2 * BLK)
    return n, pl.ds(cur, BLK, stride=d), pl.ds(keys, 2 * BLK, stride=d)


def _pad_keys(dst, src):
    dst[pl.ds(0, KEY_PAD), :] = jnp.zeros((KEY_PAD, dst.shape[1]), F32)

    def copy(c, carry):
        dst[pl.ds(pl.multiple_of(KEY_PAD + c * BLK, BLK), BLK), :] = src[pl.ds(pl.multiple_of(c * BLK, BLK), BLK), :]
        return carry

    lax.fori_loop(0, src.shape[0] // BLK, copy, 0)


def attn_fwd(proj, n_heads, *, name):
    S, WP = proj.shape
    assert S % (BLK * max(DILATIONS)) == 0
    nblk = S // BLK
    AW = n_heads * HEAD_DIM
    scale = 1.0 / math.sqrt(HEAD_DIM)

    def body(q_ref, k_ref, v_ref, o_ref, l_ref, acc, mrun, lrun, kp, vp):
        _pad_keys(kp, k_ref)
        _pad_keys(vp, v_ref)
        for first, d in zip((True, False, False), DILATIONS):
            nb = nblk // d

            def step(it, carry, d=d, nb=nb, first=first):
                blocks = [_block_rows(it + j * (nblk // ATTN_INTERLEAVE), d, nb) for j in range(ATTN_INTERLEAVE)]
                ss = [lax.dot_general(q_ref[cur, :].astype(BF16), kp[keys, :].astype(BF16), _NT,
                                      preferred_element_type=F32) * scale for _, cur, keys in blocks]
                for j, (n, cur, keys) in enumerate(blocks):
                    s = jnp.where(_key_mask(n), ss[j], NEG_INF)
                    m = jnp.max(s, axis=-1, keepdims=True)
                    p = jnp.exp(s - m)
                    l = jnp.sum(p, axis=-1, keepdims=True)
                    o = jnp.dot(p.astype(BF16), vp[keys, :].astype(BF16), preferred_element_type=F32)
                    m = jnp.broadcast_to(m, (BLK, HEAD_DIM))
                    l = jnp.broadcast_to(l, (BLK, HEAD_DIM))
                    if first:
                        acc[cur, :], mrun[cur, :], lrun[cur, :] = o, m, l
                    else:
                        m_old = mrun[cur, :]
                        m_new = jnp.maximum(m_old, m)
                        w_old, w_blk = jnp.exp(m_old - m_new), jnp.exp(m - m_new)
                        acc[cur, :] = w_old * acc[cur, :] + w_blk * o
                        lrun[cur, :] = w_old * lrun[cur, :] + w_blk * l
                        mrun[cur, :] = m_new
                return carry

            lax.fori_loop(0, nblk // ATTN_INTERLEAVE, step, 0)

        def finish(c, carry):
            r = pl.ds(pl.multiple_of(c * BLK, BLK), BLK)
            o_ref[r, :] = acc[r, :] / lrun[r, :]
            l_ref[r, :] = mrun[r, :] + jnp.log(lrun[r, :])
            return carry

        lax.fori_loop(0, nblk, finish, 0)

    def col(off):
        return pl.BlockSpec((S, HEAD_DIM), lambda h: (0, off + h))

    ospec = pl.BlockSpec((S, HEAD_DIM), lambda h: (0, h))
    return pl.pallas_call(
        body, out_shape=[jax.ShapeDtypeStruct((S, AW), F32)] * 2, grid=(n_heads,),
        in_specs=[col(0), col(n_heads), col(2 * n_heads)], out_specs=[ospec, ospec],
        scratch_shapes=[pltpu.VMEM((S, HEAD_DIM), F32)] * 3 + [pltpu.VMEM((KEY_PAD + S, HEAD_DIM), F32)] * 2,
        compiler_params=_params(("parallel",)), name=name)(proj, proj, proj)


def attn_bwd(proj, do, lse, delta, n_heads, *, name):
    S, WP = proj.shape
    nblk = S // BLK
    AW = n_heads * HEAD_DIM
    scale = 1.0 / math.sqrt(HEAD_DIM)

    def body(q_ref, k_ref, v_ref, do_ref, l_ref, dl_ref, dq_ref, dk_ref, dv_ref, dq_sc, dk_sc, dv_sc, kp, vp):
        _pad_keys(kp, k_ref)
        _pad_keys(vp, v_ref)
        dq_sc[...] = jnp.zeros_like(dq_sc)
        dk_sc[...] = jnp.zeros_like(dk_sc)
        dv_sc[...] = jnp.zeros_like(dv_sc)
        for d in DILATIONS:
            nb = nblk // d

            def step(it, carry, d=d, nb=nb):
                blocks = [_block_rows(it + j * (nblk // ATTN_INTERLEAVE), d, nb) for j in range(ATTN_INTERLEAVE)]
                for n, cur, keys in blocks:
                    q = q_ref[cur, :].astype(BF16)
                    g = do_ref[cur, :].astype(BF16)
                    kb = kp[keys, :].astype(BF16)
                    vb = vp[keys, :].astype(BF16)
                    s = lax.dot_general(q, kb, _NT, preferred_element_type=F32) * scale
                    p = jnp.where(_key_mask(n), jnp.exp(s - l_ref[cur, :][:, :1]), 0.0)
                    dp = lax.dot_general(g, vb, _NT, preferred_element_type=F32)
                    ds = (p * (dp - dl_ref[cur, :][:, :1]) * scale).astype(BF16)
                    dq_sc[cur, :] += jnp.dot(ds, kb, preferred_element_type=F32)
                    dk_sc[keys, :] += lax.dot_general(ds, q, _TN, preferred_element_type=F32)
                    dv_sc[keys, :] += lax.dot_general(p.astype(BF16), g, _TN, preferred_element_type=F32)
                return carry

            lax.fori_loop(0, nblk // ATTN_INTERLEAVE, step, 0)
        rows = pl.ds(KEY_PAD, S)
        dq_ref[...] = dq_sc[...].astype(BF16)
        dk_ref[...] = dk_sc[rows, :].astype(BF16)
        dv_ref[...] = dv_sc[rows, :].astype(BF16)

    def col(off):
        return pl.BlockSpec((S, HEAD_DIM), lambda h: (0, off + h))

    ospec = pl.BlockSpec((S, HEAD_DIM), lambda h: (0, h))
    return pl.pallas_call(
        body, out_shape=[jax.ShapeDtypeStruct((S, AW), BF16)] * 3, grid=(n_heads,),
        in_specs=[col(0), col(n_heads), col(2 * n_heads), ospec, ospec, ospec], out_specs=[ospec] * 3,
        scratch_shapes=[pltpu.VMEM((S, HEAD_DIM), F32)] + [pltpu.VMEM((KEY_PAD + S, HEAD_DIM), F32)] * 4,
        compiler_params=_params(("parallel",), VMEM_LIMIT_SCAN), name=name)(proj, proj, proj, do, lse, delta)


def _to_segments(t):
    S, W = t.shape
    return t.reshape(SEGMENTS, S // SEGMENTS, W).swapaxes(0, 1).reshape(S, W)


def _from_segments(t):
    S, W = t.shape
    return t.reshape(S // SEGMENTS, SEGMENTS, W).swapaxes(0, 1).reshape(S, W)


def _cmul(ar, ai, br, bi):
    return ar * br - ai * bi, ar * bi + ai * br


def _power(ar, ai, log2n):
    for _ in range(log2n):
        ar, ai = _cmul(ar, ai, ar, ai)
    return ar, ai


def _shift_rows(x, up):
    row = lax.broadcasted_iota(jnp.int32, x.shape, 0)
    if up:
        return jnp.where(row == SEGMENTS - 1, 0.0, pltpu.roll(x, SEGMENTS - 1, 0))
    return jnp.where(row == 0, 0.0, pltpu.roll(x, 1, 0))


def _segment_carries(er, ei, pr, pi, up):
    cr = jnp.zeros_like(er)
    ci = jnp.zeros_like(ei)
    for _ in range(SEGMENTS - 1):
        tr, ti = _cmul(pr, pi, cr, ci)
        cr, ci = _shift_rows(er + tr, up), _shift_rows(ei + ti, up)
    return cr, ci


def _scan_states(sr, si, ar, ai, T, reverse):
    ns = sr.shape[1]
    ar8 = jnp.broadcast_to(ar, (SEGMENTS, ns))
    ai8 = jnp.broadcast_to(ai, (SEGMENTS, ns))

    def rows(t):
        k = (T - 1 - t) if reverse else t
        return pl.ds(pl.multiple_of(k * SEGMENTS, SEGMENTS), SEGMENTS)

    def advance(t, c):
        tr, ti = _cmul(ar8, ai8, c[0], c[1])
        return tr + sr[rows(t), :], ti + si[rows(t), :]

    def several(step):
        def trip(t, c):
            for j in range(SCAN_UNROLL):
                c = step(t * SCAN_UNROLL + j, c)
            return c
        return trip

    zero = jnp.zeros((SEGMENTS, ns), F32)
    er, ei = lax.fori_loop(0, T // SCAN_UNROLL, several(advance), (zero, zero))
    pr, pi = _power(ar, ai, T.bit_length() - 1)
    cr, ci = _segment_carries(er, ei, jnp.broadcast_to(pr, (SEGMENTS, ns)), jnp.broadcast_to(pi, (SEGMENTS, ns)), reverse)

    def store(t, c):
        nr, ni = advance(t, c)
        sr[rows(t), :] = nr
        si[rows(t), :] = ni
        return nr, ni

    lax.fori_loop(0, T // SCAN_UNROLL, several(store), (cr, ci))
    return cr, ci


def _slab_specs(ns):
    return [pl.BlockSpec((None, LANES, ns), lambda g: (g, 0, 0)),
            pl.BlockSpec((None, LANES, ns), lambda g: (g, 0, 0)),
            pl.BlockSpec((None, 1, ns), lambda g: (g, 0, 0)),
            pl.BlockSpec((None, 1, ns), lambda g: (g, 0, 0)),
            pl.BlockSpec((None, ns, LANES), lambda g: (g, 0, 0)),
            pl.BlockSpec((None, ns, LANES), lambda g: (g, 0, 0)),
            pl.BlockSpec((1, LANES), lambda g: (0, g))]


def _chunks(S):
    rc = _tile(S, 512, 16)
    return rc, S // rc


def ssm_fwd(u, bbr, bbi, ar, ai, cbr, cbi, dsk, *, name):
    S, SW = u.shape
    nslab, _, ns = bbr.shape
    T = S // SEGMENTS
    assert T & (T - 1) == 0
    rc, nc = _chunks(S)

    def body(u_ref, br_ref, bi_ref, ar_ref, ai_ref, cr_ref, ci_ref, d_ref, y_ref, sr, si):
        def inputs(c, carry):
            r = pl.ds(pl.multiple_of(c * rc, rc), rc)
            sr[r, :] = jnp.dot(u_ref[r, :], br_ref[...], preferred_element_type=F32)
            si[r, :] = jnp.dot(u_ref[r, :], bi_ref[...], preferred_element_type=F32)
            return carry

        lax.fori_loop(0, nc, inputs, 0)
        _scan_states(sr, si, ar_ref[...], ai_ref[...], T, False)

        def outputs(c, carry):
            r = pl.ds(pl.multiple_of(c * rc, rc), rc)
            y_ref[r, :] = (jnp.dot(sr[r, :].astype(BF16), cr_ref[...], preferred_element_type=F32)
                           - jnp.dot(si[r, :].astype(BF16), ci_ref[...], preferred_element_type=F32)
                           + d_ref[...] * u_ref[r, :].astype(F32))
            return carry

        lax.fori_loop(0, nc, outputs, 0)

    slab = pl.BlockSpec((S, LANES), lambda g: (0, g))
    return pl.pallas_call(
        body, out_shape=jax.ShapeDtypeStruct((S, SW), F32), grid=(nslab,),
        in_specs=[slab] + _slab_specs(ns), out_specs=slab,
        scratch_shapes=[pltpu.VMEM((S, ns), F32)] * 2,
        compiler_params=_params(("parallel",), VMEM_LIMIT_SCAN), name=name)(u, bbr, bbi, ar, ai, cbr, cbi, dsk)


def ssm_bwd(u, dy, bbr, bbi, ar, ai, cbr, cbi, dsk, *, name):
    S, SW = u.shape
    nslab, _, ns = bbr.shape
    T = S // SEGMENTS
    rc, nc = _chunks(S)

    def body(u_ref, dy_ref, br_ref, bi_ref, ar_ref, ai_ref, cr_ref, ci_ref, d_ref,
             du_ref, dbr_ref, dbi_ref, dcr_ref, dci_ref, dar_ref, dai_ref, sr, si, lr, li):
        def inputs(c, carry):
            r = pl.ds(pl.multiple_of(c * rc, rc), rc)
            ub = u_ref[r, :]
            gb = dy_ref[r, :].astype(BF16)
            sr[r, :] = jnp.dot(ub, br_ref[...], preferred_element_type=F32)
            si[r, :] = jnp.dot(ub, bi_ref[...], preferred_element_type=F32)
            lr[r, :] = lax.dot_general(gb, cr_ref[...], _NT, preferred_element_type=F32)
            li[r, :] = -lax.dot_general(gb, ci_ref[...], _NT, preferred_element_type=F32)
            return carry

        lax.fori_loop(0, nc, inputs, 0)
        ar, ai = ar_ref[...], ai_ref[...]
        s0r, s0i = _scan_states(sr, si, ar, ai, T, False)
        _scan_states(lr, li, ar, -ai, T, True)

        def pair(k, c):
            aligned = (lambda v: v) if isinstance(k, int) else (lambda v: pl.multiple_of(v, SEGMENTS))
            now = pl.ds(aligned(k * SEGMENTS), SEGMENTS)
            prev = pl.ds(aligned((k - 1) * SEGMENTS), SEGMENTS)
            return (c[0] + lr[now, :] * sr[prev, :] + li[now, :] * si[prev, :],
                    c[1] - lr[now, :] * si[prev, :] + li[now, :] * sr[prev, :])

        first = pl.ds(0, SEGMENTS)
        acc = (lr[first, :] * s0r + li[first, :] * s0i, -lr[first, :] * s0i + li[first, :] * s0r)

        def pairs(t, c):
            for j in range(SCAN_UNROLL):
                c = pair(1 + t * SCAN_UNROLL + j, c)
            return c

        whole = (T - 1) // SCAN_UNROLL
        acc = lax.fori_loop(0, whole, pairs, acc)
        for k in range(1 + whole * SCAN_UNROLL, T):
            acc = pair(k, acc)
        dar_ref[...] = jnp.sum(acc[0], axis=0, keepdims=True)
        dai_ref[...] = jnp.sum(acc[1], axis=0, keepdims=True)

        dbr_ref[...] = jnp.zeros_like(dbr_ref)
        dbi_ref[...] = jnp.zeros_like(dbi_ref)
        dcr_ref[...] = jnp.zeros_like(dcr_ref)
        dci_ref[...] = jnp.zeros_like(dci_ref)

        def outputs(c, carry):
            r = pl.ds(pl.multiple_of(c * rc, rc), rc)
            ub = u_ref[r, :]
            g = dy_ref[r, :]
            gb = g.astype(BF16)
            lrb = lr[r, :].astype(BF16)
            lib = li[r, :].astype(BF16)
            du_ref[r, :] = (lax.dot_general(lrb, br_ref[...], _NT, preferred_element_type=F32)
                            + lax.dot_general(lib, bi_ref[...], _NT, preferred_element_type=F32)
                            + d_ref[...] * g).astype(BF16)
            dbr_ref[...] += lax.dot_general(ub, lrb, _TN, preferred_element_type=F32)
            dbi_ref[...] += lax.dot_general(ub, lib, _TN, preferred_element_type=F32)
            dcr_ref[...] += lax.dot_general(sr[r, :].astype(BF16), gb, _TN, preferred_element_type=F32)
            dci_ref[...] -= lax.dot_general(si[r, :].astype(BF16), gb, _TN, preferred_element_type=F32)
            return carry

        lax.fori_loop(0, nc, outputs, 0)

    slab = pl.BlockSpec((S, LANES), lambda g: (0, g))
    bspec = pl.BlockSpec((None, LANES, ns), lambda g: (g, 0, 0))
    cspec = pl.BlockSpec((None, ns, LANES), lambda g: (g, 0, 0))
    aspec = pl.BlockSpec((None, 1, ns), lambda g: (g, 0, 0))
    return pl.pallas_call(
        body,
        out_shape=[jax.ShapeDtypeStruct((S, SW), BF16),
                   jax.ShapeDtypeStruct((nslab, LANES, ns), F32), jax.ShapeDtypeStruct((nslab, LANES, ns), F32),
                   jax.ShapeDtypeStruct((nslab, ns, LANES), F32), jax.ShapeDtypeStruct((nslab, ns, LANES), F32),
                   jax.ShapeDtypeStruct((nslab, 1, ns), F32), jax.ShapeDtypeStruct((nslab, 1, ns), F32)],
        grid=(nslab,), in_specs=[slab, slab] + _slab_specs(ns),
        out_specs=[slab, bspec, bspec, cspec, cspec, aspec, aspec],
        scratch_shapes=[pltpu.VMEM((S, ns), F32)] * 4,
        compiler_params=_params(("parallel",), VMEM_LIMIT_SCAN), name=name)(u, dy, bbr, bbi, ar, ai, cbr, cbi, dsk)


def _discretise(lam_re, lam_im, log_dt, b_re, b_im):
    dt = jnp.exp(log_dt)[:, None]
    mag = jnp.exp(lam_re * dt)
    ar = mag * jnp.cos(lam_im * dt)
    ai = mag * jnp.sin(lam_im * dt)
    nr, ni = ar - 1.0, ai
    den = lam_re * lam_re + lam_im * lam_im
    cr = ((nr * lam_re + ni * lam_im) / den)[..., None]
    ci = ((ni * lam_re - nr * lam_im) / den)[..., None]
    return ar, ai, cr * b_re - ci * b_im, cr * b_im + ci * b_re


def _block_diag(t, nslab):
    G, R, C = t.shape
    eye = jnp.eye(SLAB_GROUPS, dtype=t.dtype)
    t = t.reshape(nslab, SLAB_GROUPS, R, C)
    return jnp.einsum('sgrc,gh->sgrhc', t, eye).reshape(nslab, SLAB_GROUPS * R, SLAB_GROUPS * C)


def _block_diag_part(t, R, C):
    nslab = t.shape[0]
    eye = jnp.eye(SLAB_GROUPS, dtype=t.dtype)
    t = t.reshape(nslab, SLAB_GROUPS, R, SLAB_GROUPS, C)
    return jnp.einsum('sgrhc,gh->sgrc', t, eye).reshape(nslab * SLAB_GROUPS, R, C)


def _place():
    return lax.axis_index("x"), lax.axis_index("y"), lax.axis_index("c")


def all_gather(shards, *, name):
    nw = len(shards)

    def body(*refs):
        ins, outs = refs[:nw], refs[nw:2 * nw]
        send_sems, recv_sems, local_sems = refs[2 * nw:]
        x, y, c = _place()
        me, sibling = (x, y, c), (x, y, 1 - c)
        chips = [(1 - x, y), (x, 1 - y), (1 - x, 1 - y)]

        def copy(w, k, block, to, own):
            px, py, pc = block
            slot = outs[w].at[4 * px + 2 * py + pc]
            return pltpu.make_async_remote_copy(
                src_ref=ins[w] if own else slot, dst_ref=slot, send_sem=send_sems.at[w, k],
                recv_sem=recv_sems.at[w, k], device_id=to, device_id_type=MESH)

        mine = [pltpu.make_async_copy(ins[w], outs[w].at[4 * x + 2 * y + c], local_sems.at[w]) for w in range(nw)]
        for cp in mine:
            cp.start()
        first = []
        for w in range(nw):
            first.append(copy(w, 0, me, sibling, True))
            first += [copy(w, 1 + j, me, (*chip, c), True) for j, chip in enumerate(chips)]
        for cp in first:
            cp.start()
        passed = []
        for j, chip in enumerate(chips):
            for w in range(nw):
                copy(w, 1 + j, (*chip, c), me, False).wait_recv()
                cp = copy(w, 4 + j, (*chip, c), sibling, False)
                cp.start()
                passed.append(cp)
        for w in range(nw):
            copy(w, 0, sibling, me, False).wait_recv()
            for j, chip in enumerate(chips):
                copy(w, 4 + j, (*chip, 1 - c), me, False).wait_recv()
        for cp in first + passed:
            cp.wait_send()
        for cp in mine:
            cp.wait()

    anyspec = pl.BlockSpec(memory_space=pl.ANY)
    return pl.pallas_call(
        body, out_shape=[jax.ShapeDtypeStruct((N_DEV,) + s.shape, s.dtype) for s in shards],
        in_specs=[anyspec] * nw, out_specs=[anyspec] * nw,
        scratch_shapes=[pltpu.SemaphoreType.DMA((nw, 7)), pltpu.SemaphoreType.DMA((nw, 7)),
                        pltpu.SemaphoreType.DMA((nw,))],
        compiler_params=pltpu.CompilerParams(has_side_effects=True), name=name)(*shards)


_HBM = pl.BlockSpec(memory_space=pltpu.HBM)
_SEM = pl.BlockSpec(memory_space=pltpu.SEMAPHORE)
_ORDERED_EFFECT = pltpu.SideEffectType.DATAFLOW_SIDE_EFFECTING


def _split_call(name, srcs, zones, sems_in, n_new, body_fn, after):
    nsrc, nz, ns, nn = len(srcs), len(zones), len(sems_in), len(n_new)
    nb = nsrc + nz

    def body(*refs):
        outs = refs[nb + ns + 1:]
        body_fn(refs[:nb], refs[nb:nb + ns], outs[:nn])
        outs[nn + nz][...] = jnp.zeros((SUBLANES, LANES), F32)

    res = pl.pallas_call(
        body, name=name,
        out_shape=([pltpu.SemaphoreType.DMA((n,)) for n in n_new] + [pltpu.HBM(b.shape, b.dtype) for b in zones]
                   + [jax.ShapeDtypeStruct((SUBLANES, LANES), F32)]),
        in_specs=[_HBM] * nb + [_SEM] * ns + [_ANY],
        out_specs=[_SEM] * nn + [_HBM] * nz + [pl.BlockSpec(memory_space=pltpu.VMEM)],
        input_output_aliases={nsrc + i: nn + i for i in range(nz)},
        compiler_params=pltpu.CompilerParams(has_side_effects=_ORDERED_EFFECT))(
            *[pltpu.with_memory_space_constraint(b, pltpu.HBM) for b in list(srcs) + list(zones)], *sems_in, after)
    return list(res[:nn]), list(res[nn:nn + nz]), res[-1]


def _mesh_peers():
    x, y, c = _place()
    return x, y, c, (x, y, 1 - c), [(1 - x, y), (x, 1 - y), (1 - x, 1 - y)]


def gather_start(shards, after, *, name):
    nw = len(shards)
    x, y, c = _place()
    zones = [lax.dynamic_update_slice(lax.empty((N_DEV,) + s.shape, s.dtype), s[None], (4 * x + 2 * y + c, 0, 0))
             for s in shards]

    def body(bufs, taken, new):
        for cp in _gather_first(bufs, nw, new[0], new[1]):
            cp.start()

    sems, zones, token = _split_call(name, shards, zones, [], [4 * nw, 4 * nw], body, after)
    return shards, sems, zones, token


def _gather_first(bufs, nw, send, recv):
    x, y, c, sibling, chips = _mesh_peers()
    out = []
    for w in range(nw):
        slot = bufs[nw + w].at[4 * x + 2 * y + c]
        for k, to in enumerate([sibling] + [(*ch, c) for ch in chips]):
            out.append(pltpu.make_async_remote_copy(
                src_ref=bufs[w], dst_ref=slot, send_sem=send.at[4 * w + k], recv_sem=recv.at[4 * w + k],
                device_id=to, device_id_type=MESH))
    return out


def _gather_slot_copy(bufs, nw, w, block, send_sem, recv_sem, to):
    px, py, pc = block
    slot = bufs[nw + w].at[4 * px + 2 * py + pc]
    return pltpu.make_async_remote_copy(src_ref=slot, dst_ref=slot, send_sem=send_sem, recv_sem=recv_sem,
                                        device_id=to, device_id_type=MESH)


def gather_forward(state, after, *, name):
    shards, sems, zones, _ = state
    nw = len(shards)

    def body(bufs, taken, new):
        x, y, c, sibling, chips = _mesh_peers()
        for j, ch in enumerate(chips):
            for w in range(nw):
                k = 4 * w + 1 + j
                _gather_slot_copy(bufs, nw, w, (*ch, c), taken[0].at[k], taken[1].at[k], (*ch, c)).wait_recv()
                _gather_slot_copy(bufs, nw, w, (*ch, c), new[0].at[3 * w + j], new[1].at[3 * w + j], sibling).start()
        for w in range(nw):
            _gather_slot_copy(bufs, nw, w, sibling, taken[0].at[4 * w], taken[1].at[4 * w], sibling).wait_recv()
        for cp in _gather_first(bufs, nw, taken[0], taken[1]):
            cp.wait_send()

    sems, zones, token = _split_call(name, shards, zones, sems, [3 * nw, 3 * nw], body, after)
    return shards, sems, zones, token


def gather_finish(state, after, *, name):
    shards, sems, zones, _ = state
    nw = len(shards)

    def body(bufs, taken, new):
        x, y, c, sibling, chips = _mesh_peers()
        for w in range(nw):
            for j, ch in enumerate(chips):
                cp = _gather_slot_copy(bufs, nw, w, (*ch, 1 - c), taken[0].at[3 * w + j], taken[1].at[3 * w + j], sibling)
                cp.wait_send()
                cp.wait_recv()

    _, zones, _ = _split_call(name, shards, zones, sems, [], body, after)
    return zones


def exchange_start(srcs, zone_shapes, copies, n, after, *, name):
    nw = len(srcs)
    zones = [lax.empty(z, s.dtype) for z, s in zip(zone_shapes, srcs)]

    def body(bufs, taken, new):
        for cp in copies(bufs[:nw], bufs[nw:], new[0], new[1]):
            cp.start()

    sems, zones, token = _split_call(name, srcs, zones, [], [n, n], body, after)
    return srcs, copies, sems, zones, token


def exchange_wait(state, after, *, name):
    srcs, copies, sems, zones, _ = state
    nw = len(srcs)

    def body(bufs, taken, new):
        for cp in copies(bufs[:nw], bufs[nw:], taken[0], taken[1]):
            cp.wait_send()
            cp.wait_recv()

    _, zones, _ = _split_call(name, srcs, zones, sems, [], body, after)
    return zones


def _core_copies(srcs, zones, send, recv):
    x, y, c = _place()
    return [pltpu.make_async_remote_copy(
        src_ref=srcs[w].at[:, 1 - c], dst_ref=zones[w], send_sem=send.at[w], recv_sem=recv.at[w],
        device_id=(x, y, 1 - c), device_id_type=MESH) for w in range(len(srcs))]


def _chip_copies(srcs, zones, send, recv):
    x, y, c = _place()
    chips = [(1 - x, y), (x, 1 - y), (1 - x, 1 - y)]
    return [pltpu.make_async_remote_copy(
        src_ref=srcs[w].at[2 * cx + cy], dst_ref=zones[w].at[j], send_sem=send.at[3 * w + j],
        recv_sem=recv.at[3 * w + j], device_id=(cx, cy, c), device_id_type=MESH)
        for w in range(len(srcs)) for j, (cx, cy) in enumerate(chips)]


def _blocked(fn, ins, outs, *, name, place=None, tr=256):
    k, n = outs[0][0]
    tr = _tile(k, tr, 16)
    if place is None:
        place = jnp.zeros((1,), jnp.int32)
    specs = []
    args = []
    for a in ins:
        if isinstance(a, tuple):
            arr, lead = a
            specs.append(pl.BlockSpec((None, tr, n), functools.partial(lambda i, s, lead: (*lead(i, s), 0), lead=lead)))
            args.append(arr)
        else:
            specs.append(pl.BlockSpec((tr, n), lambda i, s: (i, 0)))
            args.append(a)
    nin = len(args)

    def body(place_ref, *refs):
        res = fn(*[r[...] for r in refs[:nin]])
        for ref, val in zip(refs[nin:], res):
            ref[...] = val.astype(ref.dtype)

    return pl.pallas_call(
        body, out_shape=[jax.ShapeDtypeStruct(s, d) for s, d in outs],
        grid_spec=pltpu.PrefetchScalarGridSpec(
            num_scalar_prefetch=1, grid=(k // tr,), in_specs=specs,
            out_specs=[pl.BlockSpec((tr, n), lambda i, s: (i, 0)) for _ in outs]),
        compiler_params=_params(("parallel",)), name=name)(place, *args)


def _adamw(w, g, m, v):
    m = ADAM_B1 * m + (1.0 - ADAM_B1) * g
    v = ADAM_B2 * v + (1.0 - ADAM_B2) * (g * g)
    m_hat = m / (1.0 - ADAM_B1 ** ADAM_STEP)
    v_hat = v / (1.0 - ADAM_B2 ** ADAM_STEP)
    delta = -ADAM_LR * (m_hat / (jnp.sqrt(v_hat) + ADAM_EPS) + ADAM_WD * w)
    return delta, m, v


def kernel(x, p, mix_norm_pre, w_in, lam_re, lam_im, log_dt, ssm_b_re, ssm_b_im, ssm_c_re, ssm_c_im, ssm_d, w_glu, b_glu, attn_out_norm, ssm_out_norm, w_out, mix_norm_post, mlp_norm_pre, w_up, w_down, mlp_norm_post, ple_norm_pre, w_ple_gate, w_ple_proj, ple_norm_post, loss_target, m_mix_norm_pre, m_w_in, m_lam_re, m_lam_im, m_log_dt, m_ssm_b_re, m_ssm_b_im, m_ssm_c_re, m_ssm_c_im, m_ssm_d, m_w_glu, m_b_glu, m_attn_out_norm, m_ssm_out_norm, m_w_out, m_mix_norm_post, m_mlp_norm_pre, m_w_up, m_w_down, m_mlp_norm_post, m_ple_norm_pre, m_w_ple_gate, m_w_ple_proj, m_ple_norm_post, v_mix_norm_pre, v_w_in, v_lam_re, v_lam_im, v_log_dt, v_ssm_b_re, v_ssm_b_im, v_ssm_c_re, v_ssm_c_im, v_ssm_d, v_w_glu, v_b_glu, v_attn_out_norm, v_ssm_out_norm, v_w_out, v_mix_norm_post, v_mlp_norm_pre, v_w_up, v_w_down, v_mlp_norm_post, v_ple_norm_pre, v_w_ple_gate, v_w_ple_proj, v_ple_norm_post):
    weights = dict(mix_norm_pre=mix_norm_pre, w_in=w_in, lam_re=lam_re, lam_im=lam_im, log_dt=log_dt, ssm_b_re=ssm_b_re, ssm_b_im=ssm_b_im, ssm_c_re=ssm_c_re, ssm_c_im=ssm_c_im, ssm_d=ssm_d, w_glu=w_glu, b_glu=b_glu, attn_out_norm=attn_out_norm, ssm_out_norm=ssm_out_norm, w_out=w_out, mix_norm_post=mix_norm_post, mlp_norm_pre=mlp_norm_pre, w_up=w_up, w_down=w_down, mlp_norm_post=mlp_norm_post, ple_norm_pre=ple_norm_pre, w_ple_gate=w_ple_gate, w_ple_proj=w_ple_proj, ple_norm_post=ple_norm_post)
    mom_m = dict(mix_norm_pre=m_mix_norm_pre, w_in=m_w_in, lam_re=m_lam_re, lam_im=m_lam_im, log_dt=m_log_dt, ssm_b_re=m_ssm_b_re, ssm_b_im=m_ssm_b_im, ssm_c_re=m_ssm_c_re, ssm_c_im=m_ssm_c_im, ssm_d=m_ssm_d, w_glu=m_w_glu, b_glu=m_b_glu, attn_out_norm=m_attn_out_norm, ssm_out_norm=m_ssm_out_norm, w_out=m_w_out, mix_norm_post=m_mix_norm_post, mlp_norm_pre=m_mlp_norm_pre, w_up=m_w_up, w_down=m_w_down, mlp_norm_post=m_mlp_norm_post, ple_norm_pre=m_ple_norm_pre, w_ple_gate=m_w_ple_gate, w_ple_proj=m_w_ple_proj, ple_norm_post=m_ple_norm_post)
    mom_v = dict(mix_norm_pre=v_mix_norm_pre, w_in=v_w_in, lam_re=v_lam_re, lam_im=v_lam_im, log_dt=v_log_dt, ssm_b_re=v_ssm_b_re, ssm_b_im=v_ssm_b_im, ssm_c_re=v_ssm_c_re, ssm_c_im=v_ssm_c_im, ssm_d=v_ssm_d, w_glu=v_w_glu, b_glu=v_b_glu, attn_out_norm=v_attn_out_norm, ssm_out_norm=v_ssm_out_norm, w_out=v_w_out, mix_norm_post=v_mix_norm_post, mlp_norm_pre=v_mlp_norm_pre, w_up=v_w_up, w_down=v_w_down, mlp_norm_post=v_mlp_norm_post, ple_norm_pre=v_ple_norm_pre, w_ple_gate=v_w_ple_gate, w_ple_proj=v_w_ple_proj, ple_norm_post=v_ple_norm_post)
    order = list(weights)
    big = ["w_in", "w_glu", "w_out", "w_up", "w_down", "w_ple_gate", "w_ple_proj"]
    col_sharded = {"w_in", "w_up", "w_ple_proj"}
    small = [n for n in order if n not in big]

    _, S, D = x.shape
    xs = x[0]
    tgt = loss_target[0]
    AW = attn_out_norm.shape[1]
    SW = ssm_d.shape[1]
    H = AW // HEAD_DIM
    G = SW // SSM_GROUP
    nslab = G // SLAB_GROUPS
    P_, C_ = SSM_STATE, SSM_GROUP

    shard = {n: weights[n][0].astype(BF16) for n in big}
    W, WT = {}, {}

    def arrived(names, gathered):
        for n, g in zip(names, gathered):
            W[n] = g if n in col_sharded else g.reshape(1, N_DEV * g.shape[1], g.shape[2])

    def transposed(g):
        return jnp.swapaxes(g, 1, 2).reshape(1, g.shape[0] * g.shape[2], g.shape[1])

    arrived(["w_in"], all_gather([shard["w_in"]], name="gather_w_in"))
    WT["w_in"] = transposed(W["w_in"])
    early, late = ["w_glu", "w_out"], ["w_up", "w_down", "w_ple_gate", "w_ple_proj"]
    gather_early = gather_start([shard[n] for n in early], W["w_in"], name="gather_early_start")
    gather_late = gather_start([shard[n] for n in late], gather_early[-1], name="gather_late_start")

    g1, g2, g3, g4, g5, g6 = (weights[n] for n in ("mix_norm_pre", "mix_norm_post", "mlp_norm_pre",
                                                      "mlp_norm_post", "ple_norm_pre", "ple_norm_post"))
    ga, gs = attn_out_norm, ssm_out_norm
    (hn1,) = rowwise(lambda a, g: (_rms(a, g),), [xs], [g1], [(D, BF16)], deps=(gather_late[-1],), name="norm_in")
    (proj,) = mm_nn(hn1, W["w_in"], [F32], name="proj_in")
    attn, lse = attn_fwd(proj, H, name="attn_fwd")
    gather_early = gather_forward(gather_early, attn, name="gather_early_forward")
    (mix_a,) = rowwise(lambda a, g: (_rms(a, g),), [attn], [ga], [(AW, BF16)], deps=(gather_early[-1],),
                       name="attn_norm")
    arrived(early, gather_finish(gather_early, mix_a, name="gather_early_finish"))

    a_r, a_i, bb_r, bb_i = _discretise(lam_re[0], lam_im[0], log_dt[0], ssm_b_re[0], ssm_b_im[0])
    ssm_consts = (_block_diag(bb_r.swapaxes(1, 2), nslab).astype(BF16), _block_diag(bb_i.swapaxes(1, 2), nslab).astype(BF16),
                  a_r.reshape(nslab, 1, SLAB_STATES), a_i.reshape(nslab, 1, SLAB_STATES),
                  _block_diag(ssm_c_re[0].swapaxes(1, 2), nslab).astype(BF16),
                  _block_diag(ssm_c_im[0].swapaxes(1, 2), nslab).astype(BF16), ssm_d)
    u_seg = _to_segments(proj[:, 3 * AW:]).astype(BF16)
    y_pre = ssm_fwd(u_seg, *ssm_consts, name="ssm_fwd")
    gather_late = gather_forward(gather_late, y_pre, name="gather_late_forward")
    (yg,) = rowwise(lambda a: (_gelu(a),), [y_pre], [], [(SW, BF16)], deps=(gather_late[-1],), name="ssm_gelu")
    (gl1,) = mm_nn(yg, W["w_glu"], [F32], epi=lambda acc, b: (acc + b,), bias=b_glu, name="glu_gate")
    (mix_s,) = rowwise(lambda yp, gl, g: (_rms(_gelu(yp) * _sigmoid(gl), g),), [y_pre, gl1], [gs], [(SW, BF16)],
                       name="ssm_glu_norm")
    mixed = jnp.concatenate([mix_a, _from_segments(mix_s)], axis=1)
    (mo,) = mm_nn(mixed, W["w_out"], [F32], name="mix_out")

    def resid_norm(h, t, gpost, gpre):
        hh = h + _rms(t, gpost)
        return hh, _rms(hh, gpre)

    h1, hn2 = rowwise(resid_norm, [xs, mo], [g2, g3], [(D, F32), (D, BF16)], name="resid_mix")
    arrived(late, gather_finish(gather_late, hn2, name="gather_late_finish"))
    WT["w_up"] = transposed(W["w_up"])

    def relu2(acc):
        r = jnp.maximum(acc, 0.0)
        return acc, r * r

    up, act = mm_nn(hn2, W["w_up"], [BF16, BF16], epi=relu2, tn=1024, name="mlp_up")
    (ff,) = mm_nn(act, W["w_down"], [F32], tm=2048, name="mlp_down")
    h2, hn3 = rowwise(resid_norm, [h1, ff], [g4, g5], [(D, F32), (D, BF16)], name="resid_mlp")
    (gl2,) = mm_nn(hn3, W["w_ple_gate"], [F32], name="ple_gate")
    pb = p[0, 0].astype(BF16)
    (emb,) = mm_nn(pb, W["w_ple_proj"], [F32], name="ple_proj")

    def head(h, gl, e, t, g):
        sg = _sigmoid(gl)
        ge = sg * e
        err = h + _rms(ge, g) - t
        dh = err * (1.0 / D)
        dge, dg = _rms_bwd(dh, ge, g)
        return dh, dge * e * sg * (1.0 - sg), dge * sg, jnp.sum(err * err, axis=0, keepdims=True), dg

    dh3, dgl2, demb, loss_part, dg6 = rowwise(head, [h2, gl2, emb, tgt], [g6], [(D, F32), (D, BF16), (D, BF16)],
                                             [D, D], name="ple_loss_head")
    loss = lax.psum(0.5 / D * jnp.sum(loss_part), ("x", "y", "c"))

    x_i, y_i, c_i = _place()
    place = jnp.stack([c_i, 2 * x_i + y_i]).astype(jnp.int32)
    grads, out_g, out_d, out_m, out_v = {}, {}, {}, {}, {}

    def to_sibling(names, after, tag):
        chunks = []
        for n in names:
            g = grads[n]
            g = g if n in col_sharded else g.reshape(N_DEV, g.shape[1] // N_DEV, g.shape[2])
            chunks.append(g.reshape(4, 2, g.shape[1], g.shape[2]))
        return chunks, exchange_start(chunks, [(4,) + g.shape[2:] for g in chunks], _core_copies, len(chunks), after,
                                      name=f"grads_to_sibling_{tag}")

    def to_chips(names, sent, after, tag):
        chunks, state = sent
        sums = []
        for n, g, r in zip(names, chunks, exchange_wait(state, after, name=f"grads_from_sibling_{tag}")):
            k, nn = g.shape[2], g.shape[3]
            kb = k // _tile(k, 512, 16)

            def mine(i, s, kb=kb):
                return 2 * (i // kb) + s[0], i % kb

            (s,) = _blocked(lambda a, b: (a.astype(F32) + b.astype(F32),),
                            [(g.reshape(N_DEV, k, nn), mine), r.reshape(4 * k, nn)],
                            [((4 * k, nn), BF16)], place=place, tr=k // kb, name=f"chip_sum_{n}")
            sums.append(s.reshape(4, k, nn))
        return sums, exchange_start(sums, [(3,) + s.shape[1:] for s in sums], _chip_copies, 3 * len(sums), sums[-1],
                                    name=f"grads_to_chips_{tag}")

    def update(w_, m_, v_, own, r0, r1, r2):
        g = own.astype(F32) + r0.astype(F32) + r1.astype(F32) + r2.astype(F32)
        return (g,) + _adamw(w_, g, m_, v_)

    def finish(names, sent, after, tag):
        sums, state = sent
        for n, s, r in zip(names, sums, exchange_wait(state, after, name=f"grads_from_chips_{tag}")):
            shp = weights[n].shape
            res = _blocked(update, [weights[n][0], mom_m[n][0], mom_v[n][0], (s, lambda i, p_: (p_[1], i)),
                                    (r, lambda i, p_: (0, i)), (r, lambda i, p_: (1, i)), (r, lambda i, p_: (2, i))],
                           [(shp[1:], F32)] * 4, place=place, name=f"adamw_{n}")
            out_g[n], out_d[n], out_m[n], out_v[n] = (t.reshape(shp) for t in res)
        return out_v[names[-1]]

    grads["w_ple_proj"] = mm_tn(pb, demb, N_DEV, name="grad_w_ple_proj")
    dhn3 = mm_nt(dgl2, W["w_ple_gate"], F32, name="back_ple_gate")
    grads["w_ple_gate"] = mm_tn(hn3, dgl2, 1, name="grad_w_ple_gate")

    def back_resid(dh, dhn, h, t, gpre, gpost):
        d1, dgpre = _rms_bwd(dhn, h, gpre)
        dhh = dh + d1
        dt, dgpost = _rms_bwd(dhh, t, gpost)
        return dhh, dt, dgpre, dgpost

    dh2, dff, dg5, dg4 = rowwise(back_resid, [dh3, dhn3, h2, ff], [g5, g4], [(D, F32), (D, BF16)], [D, D],
                                 name="back_resid_mlp")
    dup = mm_nt(dff, W["w_down"], BF16, epi=lambda acc, u_: (acc * 2.0 * jnp.maximum(u_.astype(F32), 0.0),),
                extra=up, tm=2048, name="back_mlp_down")
    grads["w_down"] = mm_tn(act, dff, 1, name="grad_w_down")
    group_a = ["w_ple_proj", "w_ple_gate", "w_down"]
    sent_a = to_sibling(group_a, grads["w_down"], "a")
    (dhn2,) = mm_nn(dup, WT["w_up"], [F32], deps=(sent_a[1][-1],), name="back_mlp_up")
    sent_a = to_chips(group_a, sent_a, dhn2, "a")
    grads["w_up"] = mm_tn(hn2, dup, N_DEV, deps=(sent_a[1][-1],), name="grad_w_up")
    dh1, dmo, dg3, dg2 = rowwise(back_resid, [dh2, dhn2, h1, mo], [g3, g2], [(D, F32), (D, BF16)], [D, D],
                                 name="back_resid_mix")
    dmixed = mm_nt(dmo, W["w_out"], F32, name="back_mix_out")
    grads["w_out"] = mm_tn(mixed, dmo, 1, name="grad_w_out")

    def back_glu(dm, yp, gl, g):
        ygf = _gelu(yp)
        sg = _sigmoid(gl)
        dssm, dg = _rms_bwd(dm, ygf * sg, g)
        dgl = dssm * ygf * sg * (1.0 - sg)
        return dgl, dssm * sg, dg, jnp.sum(dgl, axis=0, keepdims=True)

    dgl1, dyg_direct, dgs, db_glu = rowwise(back_glu, [_to_segments(dmixed[:, AW:]), y_pre, gl1], [gs],
                                            [(SW, BF16), (SW, F32)], [SW, SW], name="back_glu")
    dyg_gate = mm_nt(dgl1, W["w_glu"], F32, name="back_glu_gate")
    grads["w_glu"] = mm_tn(yg, dgl1, 1, name="grad_w_glu")
    group_b = ["w_up", "w_out", "w_glu"]
    sent_b = to_sibling(group_b, grads["w_glu"], "b")
    done_a = finish(group_a, sent_a, sent_b[1][-1], "a")

    def back_gelu(d1, d2, yp, u_):
        dy = (d1 + d2) * _gelu_grad(yp)
        return dy, jnp.sum(dy * u_.astype(F32), axis=0, keepdims=True)

    dy_pre, d_skip = rowwise(back_gelu, [dyg_direct, dyg_gate, y_pre, u_seg], [], [(SW, F32)], [SW], deps=(done_a,),
                             name="back_gelu")
    du_seg, dbb_r, dbb_i, dcb_r, dcb_i, da_r, da_i = ssm_bwd(u_seg, dy_pre, *ssm_consts, name="ssm_bwd")
    sent_b = to_chips(group_b, sent_b, du_seg, "b")

    def back_attn_norm(dm, a, g):
        da, dg = _rms_bwd(dm, a, g)
        prod = da * a
        delta = jnp.concatenate(
            [jnp.broadcast_to(jnp.sum(prod[:, h * HEAD_DIM:(h + 1) * HEAD_DIM], axis=-1, keepdims=True),
                              (prod.shape[0], HEAD_DIM)) for h in range(H)], axis=1)
        return da, delta, dg

    dattn, delta, dga = rowwise(back_attn_norm, [(dmixed, AW, 0), attn], [ga], [(AW, F32), (AW, F32)], [AW],
                                deps=(sent_b[1][-1],), name="back_attn_norm")
    dq, dk, dv = attn_bwd(proj, dattn, lse, delta, H, name="attn_bwd")
    dproj = jnp.concatenate([dq, dk, dv, _from_segments(du_seg)], axis=1)
    (dhn1,) = mm_nn(dproj, WT["w_in"], [F32], name="back_proj_in")

    def back_in(dh, dhn, a, g):
        d1, dg = _rms_bwd(dhn, a, g)
        return dh + d1, dg

    grad_x, dg1 = rowwise(back_in, [dh1, dhn1, xs], [g1], [(D, F32)], [D], name="back_norm_in")

    cot = dict(
        mix_norm_pre=dg1, mix_norm_post=dg2, mlp_norm_pre=dg3, mlp_norm_post=dg4, ple_norm_pre=dg5, ple_norm_post=dg6,
        attn_out_norm=dga, ssm_out_norm=dgs, b_glu=db_glu, ssm_d=d_skip,
        ssm_c_re=_block_diag_part(dcb_r, P_, C_).swapaxes(1, 2), ssm_c_im=_block_diag_part(dcb_i, P_, C_).swapaxes(1, 2),
        a_r=da_r.reshape(G, P_), a_i=da_i.reshape(G, P_),
        bb_r=_block_diag_part(dbb_r, C_, P_).swapaxes(1, 2), bb_i=_block_diag_part(dbb_i, C_, P_).swapaxes(1, 2))
    names = list(cot)
    flat = jnp.concatenate([cot[n].reshape(-1) for n in names])
    total = flat.shape[0]
    rows_ = -(-total // (LANES * 16)) * 16
    flat = jnp.pad(flat, (0, rows_ * LANES - total)).reshape(rows_, LANES)
    gather_small = gather_start([flat], flat, name="gather_small_start")
    grads["w_in"] = mm_tn(hn1, dproj, N_DEV, deps=(gather_small[-1],), name="grad_w_in")
    group_c = ["w_in"]
    sent_c = to_sibling(group_c, grads["w_in"], "c")
    done_b = finish(group_b, sent_b, sent_c[1][-1], "b")
    sent_c = to_chips(group_c, sent_c, done_b, "c")
    gather_small = gather_forward(gather_small, sent_c[1][-1], name="gather_small_forward")
    (every,) = gather_finish(gather_small, gather_small[-1], name="gather_small_finish")
    (summed,) = _blocked(lambda *t: (functools.reduce(lambda a, b: a + b, t),),
                         [(every, functools.partial(lambda i, p_, j: (j, i), j=j)) for j in range(N_DEV)],
                         [((rows_, LANES), F32)], name="sum_small_grads")
    summed = summed.reshape(-1)
    red, off = {}, 0
    for n in names:
        sz = cot[n].size
        red[n] = summed[off:off + sz].reshape(cot[n].shape)
        off += sz
    _, pull = jax.vjp(_discretise, lam_re[0], lam_im[0], log_dt[0], ssm_b_re[0], ssm_b_im[0])
    d_lre, d_lim, d_ldt, d_bre, d_bim = pull((red["a_r"], red["a_i"], red["bb_r"], red["bb_i"]))
    red.update(lam_re=d_lre, lam_im=d_lim, log_dt=d_ldt, ssm_b_re=d_bre, ssm_b_im=d_bim)

    def pack(d):
        t = jnp.concatenate([d[n].reshape(-1) for n in small])
        r_ = -(-t.shape[0] // (LANES * 16)) * 16
        return jnp.pad(t, (0, r_ * LANES - t.shape[0])).reshape(r_, LANES)

    sw, sg_, sm, sv = pack(weights), pack(red), pack(mom_m), pack(mom_v)
    sd, snm, snv = _blocked(lambda w_, g_, m_, v_: _adamw(w_, g_, m_, v_), [sw, sg_, sm, sv],
                            [(sw.shape, F32)] * 3, name="adamw_small")
    finish(group_c, sent_c, snv, "c")
    off = 0
    for n in small:
        sz = weights[n].size
        shp = weights[n].shape
        out_g[n] = red[n].reshape(shp)
        out_d[n] = sd.reshape(-1)[off:off + sz].reshape(shp)
        out_m[n] = snm.reshape(-1)[off:off + sz].reshape(shp)
        out_v[n] = snv.reshape(-1)[off:off + sz].reshape(shp)
        off += sz

    return (loss, grad_x[None], *[out_g[n] for n in order], *[out_d[n] for n in order],
            *[out_m[n] for n in order], *[out_v[n] for n in order])
```

```python
import functools
import math

import jax
import jax.numpy as jnp
from jax import lax
from jax.experimental import pallas as pl
from jax.experimental.pallas import tpu as pltpu

F32 = jnp.float32
BF16 = jnp.bfloat16
MESH = pl.DeviceIdType.MESH

N_DEV = 8
LANES = 128
SUBLANES = 8
VMEM_LIMIT = 48 * 1024 * 1024
VMEM_LIMIT_SCAN = 60 * 1024 * 1024

HEAD_DIM = 128
BLK = 128
DILATIONS = (1, 4, 16)
SSM_GROUP = 16
SSM_STATE = 64
SLAB_GROUPS = LANES // SSM_GROUP
SLAB_STATES = SLAB_GROUPS * SSM_STATE
SEGMENTS = SUBLANES
SCAN_UNROLL = 4
RMS_EPS = 1e-6
NEG_INF = -1e30

ADAM_LR = 0.001
ADAM_B1 = 0.9
ADAM_B2 = 0.999
ADAM_EPS = 1e-08
ADAM_WD = 0.01
ADAM_STEP = 10


def _tile(n, pref, unit=LANES):
    if n <= pref:
        return n
    t = (pref // unit) * unit
    while t > unit and n % t:
        t -= unit
    assert n % t == 0, (n, pref, unit)
    return t


def _params(sem=None, vmem=VMEM_LIMIT):
    return pltpu.CompilerParams(dimension_semantics=sem, vmem_limit_bytes=vmem)


_NN = (((1,), (0,)), ((), ()))
_NT = (((1,), (1,)), ((), ()))
_TN = (((0,), (0,)), ((), ()))


_ANY = pl.BlockSpec(memory_space=pl.ANY)


def _mm_call(dims, nk, n_extra, n_dep, n_out, epi, **kw):
    first_out = 2 + n_extra + n_dep
    kw["in_specs"] = list(kw["in_specs"]) + [_ANY] * n_dep

    def single(*refs):
        extra = refs[2:2 + n_extra]
        res = epi(lax.dot_general(refs[0][...], refs[1][...], dims, preferred_element_type=F32),
                  *[e[...] for e in extra])
        for o, r in zip(refs[first_out:first_out + n_out], res):
            o[...] = r.astype(o.dtype)

    if nk == 1:
        kw["scratch_shapes"] = []
        return pl.pallas_call(single, **kw)

    def body(*refs):
        a_ref, b_ref = refs[0], refs[1]
        extra = refs[2:2 + n_extra]
        outs = refs[first_out:first_out + n_out]
        acc = refs[-1]
        k = pl.program_id(2)

        @pl.when(k == 0)
        def _():
            acc[...] = jnp.zeros_like(acc)

        acc[...] += lax.dot_general(a_ref[...], b_ref[...], dims, preferred_element_type=F32)

        @pl.when(k == nk - 1)
        def _():
            res = epi(acc[...], *[e[...] for e in extra])
            for o, r in zip(outs, res):
                o[...] = r.astype(o.dtype)

    return pl.pallas_call(body, **kw)


def _identity_epi(acc):
    return (acc,)


def mm_nn(a, w, out_dtypes, *, name, epi=_identity_epi, bias=None, deps=(), tm=2048, tn=512, tk=2048):
    M, K = a.shape
    J, K2, n = w.shape
    assert K == K2
    tm, tn, tk = _tile(M, tm, 16), _tile(n, tn), _tile(K, tk)
    npj = n // tn
    nk = K // tk
    in_specs = [pl.BlockSpec((tm, tk), lambda i, j, k: (i, k)),
                pl.BlockSpec((None, tk, tn), lambda i, j, k: (j // npj, k, j % npj))]
    args = [a, w]
    if bias is not None:
        in_specs.append(pl.BlockSpec((1, tn), lambda i, j, k: (0, j)))
        args.append(bias)
    return _mm_call(
        _NN, nk, len(args) - 2, len(deps), len(out_dtypes), epi,
        out_shape=[jax.ShapeDtypeStruct((M, J * n), d) for d in out_dtypes],
        grid=(M // tm, J * npj, nk), in_specs=in_specs,
        out_specs=[pl.BlockSpec((tm, tn), lambda i, j, k: (i, j)) for _ in out_dtypes],
        scratch_shapes=[pltpu.VMEM((tm, tn), F32)],
        compiler_params=_params(("parallel", "parallel", "arbitrary")), name=name)(*args, *deps)


def mm_nt(a, w, out_dtype, *, name, epi=_identity_epi, extra=None, tm=2048, tko=512, tnr=2048):
    M, N = a.shape
    J, K, n = w.shape
    assert N == J * n
    tm, tko, tnr = _tile(M, tm, 16), _tile(K, tko), _tile(n, tnr)
    npj = n // tnr
    nk = N // tnr
    in_specs = [pl.BlockSpec((tm, tnr), lambda i, j, k: (i, k)),
                pl.BlockSpec((None, tko, tnr), lambda i, j, k: (k // npj, j, k % npj))]
    args = [a, w]
    if extra is not None:
        in_specs.append(pl.BlockSpec((tm, tko), lambda i, j, k: (i, j)))
        args.append(extra)
    return _mm_call(
        _NT, nk, len(args) - 2, 0, 1, epi,
        out_shape=[jax.ShapeDtypeStruct((M, K), out_dtype)],
        grid=(M // tm, K // tko, nk), in_specs=in_specs,
        out_specs=[pl.BlockSpec((tm, tko), lambda i, j, k: (i, j))],
        scratch_shapes=[pltpu.VMEM((tm, tko), F32)],
        compiler_params=_params(("parallel", "parallel", "arbitrary")), name=name)(*args)[0]


def mm_tn(a, b, J, *, name, deps=(), tko=1024, tn=1024, ts=2048):
    S, K = a.shape
    S2, N = b.shape
    assert S == S2 and N % J == 0
    n = N // J
    tko, tn, ts = _tile(K, tko), _tile(n, tn), _tile(S, ts)
    npj = n // tn
    nk = S // ts
    return _mm_call(
        _TN, nk, 0, len(deps), 1, _identity_epi,
        out_shape=[jax.ShapeDtypeStruct((J, K, n), BF16)],
        grid=(K // tko, J * npj, nk),
        in_specs=[pl.BlockSpec((ts, tko), lambda i, j, k: (k, i)),
                  pl.BlockSpec((ts, tn), lambda i, j, k: (k, j))],
        out_specs=[pl.BlockSpec((None, tko, tn), lambda i, j, k: (j // npj, i, j % npj))],
        scratch_shapes=[pltpu.VMEM((tko, tn), F32)],
        compiler_params=_params(("parallel", "parallel", "arbitrary")), name=name)(a, b, *deps)[0]


def rowwise(fn, rows, vecs, outs, accs=(), *, name, deps=(), ts=256):
    rows = [r if isinstance(r, tuple) else (r, r.shape[1], 0) for r in rows]
    S = rows[0][0].shape[0]
    ts = _tile(S, ts, 16)
    nr, nv, no, nd = len(rows), len(vecs), len(outs), len(deps)

    def body(*refs):
        r, v = refs[:nr], refs[nr:nr + nv]
        o, a = refs[nr + nv + nd:nr + nv + nd + no], refs[nr + nv + nd + no:]
        res = fn(*[t[...] for t in r], *[t[...] for t in v])
        for ref, val in zip(o, res[:no]):
            ref[...] = val.astype(ref.dtype)
        if a:
            @pl.when(pl.program_id(0) == 0)
            def _():
                for ref in a:
                    ref[...] = jnp.zeros_like(ref)

            for ref, val in zip(a, res[no:]):
                ref[...] += val

    in_specs = [pl.BlockSpec((ts, w), functools.partial(lambda i, cb: (i, cb), cb=cb)) for _, w, cb in rows]
    in_specs += [pl.BlockSpec(v.shape, lambda i: (0, 0)) for v in vecs] + [_ANY] * nd
    out_shape = [jax.ShapeDtypeStruct((S, w), d) for w, d in outs]
    out_shape += [jax.ShapeDtypeStruct((1, w), F32) for w in accs]
    out_specs = [pl.BlockSpec((ts, w), lambda i: (i, 0)) for w, _ in outs]
    out_specs += [pl.BlockSpec((1, w), lambda i: (0, 0)) for w in accs]
    return pl.pallas_call(body, out_shape=out_shape, grid=(S // ts,), in_specs=in_specs, out_specs=out_specs,
                          compiler_params=_params(("arbitrary",)), name=name)(*[r[0] for r in rows], *vecs, *deps)


def _rms(x, g):
    r = lax.rsqrt(jnp.mean(x * x, axis=-1, keepdims=True) + RMS_EPS)
    return x * r * g


def _rms_bwd(dy, x, g):
    r = lax.rsqrt(jnp.mean(x * x, axis=-1, keepdims=True) + RMS_EPS)
    xh = x * r
    dxh = dy * g
    dx = r * (dxh - xh * jnp.mean(dxh * xh, axis=-1, keepdims=True))
    return dx, jnp.sum(dy * xh, axis=0, keepdims=True)


def _sigmoid(x):
    return 1.0 / (1.0 + jnp.exp(-x))


_GELU_C = math.sqrt(2.0 / math.pi)


def _gelu(x):
    return 0.5 * x * (1.0 + jnp.tanh(_GELU_C * (x + 0.044715 * x * x * x)))


def _gelu_grad(x):
    t = jnp.tanh(_GELU_C * (x + 0.044715 * x * x * x))
    return 0.5 * (1.0 + t) + 0.5 * x * (1.0 - t * t) * _GELU_C * (1.0 + 3.0 * 0.044715 * x * x)


ATTN_INTERLEAVE = 4
KEY_PAD = BLK * max(DILATIONS)


def _key_mask(n):
    ii = lax.broadcasted_iota(jnp.int32, (BLK, 2 * BLK), 0)
    jj = lax.broadcasted_iota(jnp.int32, (BLK, 2 * BLK), 1)
    return ((jj < BLK) & (jj >= ii) & (n > 0)) | ((jj >= BLK) & (jj - BLK <= ii))


def _units(d, nblk):
    nb = nblk // d
    if nb == 2:
        def unit(idx):
            ii = lax.broadcasted_iota(jnp.int32, (2 * BLK, 2 * BLK), 0)
            jj = lax.broadcasted_iota(jnp.int32, (2 * BLK, 2 * BLK), 1)
            return pl.ds(idx, 2 * BLK, stride=d), pl.ds(KEY_PAD + idx, 2 * BLK, stride=d), (jj <= ii) & (ii - jj <= BLK)
        return d, max(1, ATTN_INTERLEAVE // 2), unit

    def unit(idx):
        r, n = idx // nb, idx % nb
        cur = r + n * (BLK * d)
        keys = cur + (KEY_PAD - BLK * d)
        if d == 1:
            return pl.ds(pl.multiple_of(cur, BLK), BLK), pl.ds(pl.multiple_of(keys, BLK), 2 * BLK), _key_mask(n)
        return pl.ds(cur, BLK, stride=d), pl.ds(keys, 2 * BLK, stride=d), _key_mask(n)
    return nblk, ATTN_INTERLEAVE, unit


def _pad_keys(dst, src):
    dst[pl.ds(0, KEY_PAD), :] = jnp.zeros((KEY_PAD, dst.shape[1]), F32)

    def copy(c, carry):
        dst[pl.ds(pl.multiple_of(KEY_PAD + c * BLK, BLK), BLK), :] = src[pl.ds(pl.multiple_of(c * BLK, BLK), BLK), :]
        return carry

    lax.fori_loop(0, src.shape[0] // BLK, copy, 0)


def attn_fwd(proj, n_heads, *, name):
    S, WP = proj.shape
    assert S % (BLK * max(DILATIONS)) == 0
    nblk = S // BLK
    AW = n_heads * HEAD_DIM
    scale = 1.0 / math.sqrt(HEAD_DIM)

    def body(q_ref, k_ref, v_ref, o_ref, l_ref, acc, mrun, lrun, kp, vp):
        _pad_keys(kp, k_ref)
        _pad_keys(vp, v_ref)
        for first, d in zip((True, False, False), reversed(DILATIONS)):
            n_units, per_step, unit = _units(d, nblk)

            def step(it, carry, first=first, n_units=n_units, per_step=per_step, unit=unit):
                units = [unit(it + j * (n_units // per_step)) for j in range(per_step)]
                ss = [lax.dot_general(q_ref[cur, :].astype(BF16), kp[keys, :].astype(BF16), _NT,
                                      preferred_element_type=F32) * scale for cur, keys, _ in units]
                for j, (cur, keys, mask) in enumerate(units):
                    s = jnp.where(mask, ss[j], NEG_INF)
                    m = jnp.max(s, axis=-1, keepdims=True)
                    p = jnp.exp(s - m)
                    l = jnp.sum(p, axis=-1, keepdims=True)
                    o = jnp.dot(p.astype(BF16), vp[keys, :].astype(BF16), preferred_element_type=F32)
                    m = jnp.broadcast_to(m, o.shape)
                    l = jnp.broadcast_to(l, o.shape)
                    if first:
                        acc[cur, :], mrun[cur, :], lrun[cur, :] = o, m, l
                    else:
                        m_old = mrun[cur, :]
                        m_new = jnp.maximum(m_old, m)
                        w_old, w_blk = jnp.exp(m_old - m_new), jnp.exp(m - m_new)
                        acc[cur, :] = w_old * acc[cur, :] + w_blk * o
                        lrun[cur, :] = w_old * lrun[cur, :] + w_blk * l
                        mrun[cur, :] = m_new
                return carry

            lax.fori_loop(0, n_units // per_step, step, 0)

        def finish(c, carry):
            r = pl.ds(pl.multiple_of(c * BLK, BLK), BLK)
            o_ref[r, :] = acc[r, :] / lrun[r, :]
            l_ref[r, :] = mrun[r, :] + jnp.log(lrun[r, :])
            return carry

        lax.fori_loop(0, nblk, finish, 0)

    def col(off):
        return pl.BlockSpec((S, HEAD_DIM), lambda h: (0, off + h))

    ospec = pl.BlockSpec((S, HEAD_DIM), lambda h: (0, h))
    return pl.pallas_call(
        body, out_shape=[jax.ShapeDtypeStruct((S, AW), F32)] * 2, grid=(n_heads,),
        in_specs=[col(0), col(n_heads), col(2 * n_heads)], out_specs=[ospec, ospec],
        scratch_shapes=[pltpu.VMEM((S, HEAD_DIM), F32)] * 3 + [pltpu.VMEM((KEY_PAD + S, HEAD_DIM), F32)] * 2,
        compiler_params=_params(("parallel",)), name=name)(proj, proj, proj)


def attn_bwd(proj, do, lse, delta, n_heads, *, name):
    S, WP = proj.shape
    nblk = S // BLK
    AW = n_heads * HEAD_DIM
    scale = 1.0 / math.sqrt(HEAD_DIM)

    def body(q_ref, k_ref, v_ref, do_ref, l_ref, dl_ref, dq_ref, dk_ref, dv_ref, dq_sc, dk_sc, dv_sc, kp, vp):
        _pad_keys(kp, k_ref)
        _pad_keys(vp, v_ref)
        dq_sc[...] = jnp.zeros_like(dq_sc)
        dk_sc[...] = jnp.zeros_like(dk_sc)
        dv_sc[...] = jnp.zeros_like(dv_sc)
        for d in DILATIONS:
            n_units, per_step, unit = _units(d, nblk)

            def step(it, carry, n_units=n_units, per_step=per_step, unit=unit):
                for cur, keys, mask in [unit(it + j * (n_units // per_step)) for j in range(per_step)]:
                    q = q_ref[cur, :].astype(BF16)
                    g = do_ref[cur, :].astype(BF16)
                    kb = kp[keys, :].astype(BF16)
                    vb = vp[keys, :].astype(BF16)
                    s = lax.dot_general(q, kb, _NT, preferred_element_type=F32) * scale
                    p = jnp.where(mask, jnp.exp(s - l_ref[cur, :][:, :1]), 0.0)
                    dp = lax.dot_general(g, vb, _NT, preferred_element_type=F32)
                    ds = (p * (dp - dl_ref[cur, :][:, :1]) * scale).astype(BF16)
                    dq_sc[cur, :] += jnp.dot(ds, kb, preferred_element_type=F32)
                    dk_sc[keys, :] += lax.dot_general(ds, q, _TN, preferred_element_type=F32)
                    dv_sc[keys, :] += lax.dot_general(p.astype(BF16), g, _TN, preferred_element_type=F32)
                return carry

            lax.fori_loop(0, n_units // per_step, step, 0)
        rows = pl.ds(KEY_PAD, S)
        dq_ref[...] = dq_sc[...].astype(BF16)
        dk_ref[...] = dk_sc[rows, :].astype(BF16)
        dv_ref[...] = dv_sc[rows, :].astype(BF16)

    def col(off):
        return pl.BlockSpec((S, HEAD_DIM), lambda h: (0, off + h))

    ospec = pl.BlockSpec((S, HEAD_DIM), lambda h: (0, h))
    return pl.pallas_call(
        body, out_shape=[jax.ShapeDtypeStruct((S, AW), BF16)] * 3, grid=(n_heads,),
        in_specs=[col(0), col(n_heads), col(2 * n_heads), ospec, ospec, ospec], out_specs=[ospec] * 3,
        scratch_shapes=[pltpu.VMEM((S, HEAD_DIM), F32)] + [pltpu.VMEM((KEY_PAD + S, HEAD_DIM), F32)] * 4,
        compiler_params=_params(("parallel",), VMEM_LIMIT_SCAN), name=name)(proj, proj, proj, do, lse, delta)


def _to_segments(t):
    S, W = t.shape
    return t.reshape(SEGMENTS, S // SEGMENTS, W).swapaxes(0, 1).reshape(S, W)


def _from_segments(t):
    S, W = t.shape
    return t.reshape(S // SEGMENTS, SEGMENTS, W).swapaxes(0, 1).reshape(S, W)


def _cmul(ar, ai, br, bi):
    return ar * br - ai * bi, ar * bi + ai * br


def _power(ar, ai, log2n):
    for _ in range(log2n):
        ar, ai = _cmul(ar, ai, ar, ai)
    return ar, ai


def _shift_rows(x, up):
    row = lax.broadcasted_iota(jnp.int32, x.shape, 0)
    if up:
        return jnp.where(row == SEGMENTS - 1, 0.0, pltpu.roll(x, SEGMENTS - 1, 0))
    return jnp.where(row == 0, 0.0, pltpu.roll(x, 1, 0))


def _segment_carries(er, ei, pr, pi, up):
    cr = jnp.zeros_like(er)
    ci = jnp.zeros_like(ei)
    for _ in range(SEGMENTS - 1):
        tr, ti = _cmul(pr, pi, cr, ci)
        cr, ci = _shift_rows(er + tr, up), _shift_rows(ei + ti, up)
    return cr, ci


def _scan_states(sr, si, ar, ai, T, reverse):
    ns = sr.shape[1]
    ar8 = jnp.broadcast_to(ar, (SEGMENTS, ns))
    ai8 = jnp.broadcast_to(ai, (SEGMENTS, ns))

    def rows(t):
        k = (T - 1 - t) if reverse else t
        return pl.ds(pl.multiple_of(k * SEGMENTS, SEGMENTS), SEGMENTS)

    def advance(t, c):
        tr, ti = _cmul(ar8, ai8, c[0], c[1])
        return tr + sr[rows(t), :], ti + si[rows(t), :]

    def several(step):
        def trip(t, c):
            for j in range(SCAN_UNROLL):
                c = step(t * SCAN_UNROLL + j, c)
            return c
        return trip

    zero = jnp.zeros((SEGMENTS, ns), F32)
    er, ei = lax.fori_loop(0, T // SCAN_UNROLL, several(advance), (zero, zero))
    pr, pi = _power(ar, ai, T.bit_length() - 1)
    cr, ci = _segment_carries(er, ei, jnp.broadcast_to(pr, (SEGMENTS, ns)), jnp.broadcast_to(pi, (SEGMENTS, ns)), reverse)

    def store(t, c):
        nr, ni = advance(t, c)
        sr[rows(t), :] = nr
        si[rows(t), :] = ni
        return nr, ni

    lax.fori_loop(0, T // SCAN_UNROLL, several(store), (cr, ci))
    return cr, ci


def _slab_specs(ns):
    return [pl.BlockSpec((None, LANES, ns), lambda g: (g, 0, 0)),
            pl.BlockSpec((None, LANES, ns), lambda g: (g, 0, 0)),
            pl.BlockSpec((None, 1, ns), lambda g: (g, 0, 0)),
            pl.BlockSpec((None, 1, ns), lambda g: (g, 0, 0)),
            pl.BlockSpec((None, ns, LANES), lambda g: (g, 0, 0)),
            pl.BlockSpec((None, ns, LANES), lambda g: (g, 0, 0)),
            pl.BlockSpec((1, LANES), lambda g: (0, g))]


def _chunks(S):
    rc = _tile(S, 512, 16)
    return rc, S // rc


def ssm_fwd(u, bbr, bbi, ar, ai, cbr, cbi, dsk, *, name):
    S, SW = u.shape
    nslab, _, ns = bbr.shape
    T = S // SEGMENTS
    assert T & (T - 1) == 0
    rc, nc = _chunks(S)

    def body(u_ref, br_ref, bi_ref, ar_ref, ai_ref, cr_ref, ci_ref, d_ref, y_ref, sr, si):
        def inputs(c, carry):
            r = pl.ds(pl.multiple_of(c * rc, rc), rc)
            sr[r, :] = jnp.dot(u_ref[r, :], br_ref[...], preferred_element_type=F32)
            si[r, :] = jnp.dot(u_ref[r, :], bi_ref[...], preferred_element_type=F32)
            return carry

        lax.fori_loop(0, nc, inputs, 0)
        _scan_states(sr, si, ar_ref[...], ai_ref[...], T, False)

        def outputs(c, carry):
            r = pl.ds(pl.multiple_of(c * rc, rc), rc)
            y_ref[r, :] = (jnp.dot(sr[r, :].astype(BF16), cr_ref[...], preferred_element_type=F32)
                           - jnp.dot(si[r, :].astype(BF16), ci_ref[...], preferred_element_type=F32)
                           + d_ref[...] * u_ref[r, :].astype(F32))
            return carry

        lax.fori_loop(0, nc, outputs, 0)

    slab = pl.BlockSpec((S, LANES), lambda g: (0, g))
    return pl.pallas_call(
        body, out_shape=jax.ShapeDtypeStruct((S, SW), F32), grid=(nslab,),
        in_specs=[slab] + _slab_specs(ns), out_specs=slab,
        scratch_shapes=[pltpu.VMEM((S, ns), F32)] * 2,
        compiler_params=_params(("parallel",), VMEM_LIMIT_SCAN), name=name)(u, bbr, bbi, ar, ai, cbr, cbi, dsk)


def ssm_bwd(u, dy, bbr, bbi, ar, ai, cbr, cbi, dsk, *, name):
    S, SW = u.shape
    nslab, _, ns = bbr.shape
    T = S // SEGMENTS
    rc, nc = _chunks(S)

    def body(u_ref, dy_ref, br_ref, bi_ref, ar_ref, ai_ref, cr_ref, ci_ref, d_ref,
             du_ref, dbr_ref, dbi_ref, dcr_ref, dci_ref, dar_ref, dai_ref, sr, si, lr, li):
        def inputs(c, carry):
            r = pl.ds(pl.multiple_of(c * rc, rc), rc)
            ub = u_ref[r, :]
            gb = dy_ref[r, :].astype(BF16)
            sr[r, :] = jnp.dot(ub, br_ref[...], preferred_element_type=F32)
            si[r, :] = jnp.dot(ub, bi_ref[...], preferred_element_type=F32)
            lr[r, :] = lax.dot_general(gb, cr_ref[...], _NT, preferred_element_type=F32)
            li[r, :] = -lax.dot_general(gb, ci_ref[...], _NT, preferred_element_type=F32)
            return carry

        lax.fori_loop(0, nc, inputs, 0)
        ar, ai = ar_ref[...], ai_ref[...]
        s0r, s0i = _scan_states(sr, si, ar, ai, T, False)
        _scan_states(lr, li, ar, -ai, T, True)

        def pair(k, c):
            aligned = (lambda v: v) if isinstance(k, int) else (lambda v: pl.multiple_of(v, SEGMENTS))
            now = pl.ds(aligned(k * SEGMENTS), SEGMENTS)
            prev = pl.ds(aligned((k - 1) * SEGMENTS), SEGMENTS)
            return (c[0] + lr[now, :] * sr[prev, :] + li[now, :] * si[prev, :],
                    c[1] - lr[now, :] * si[prev, :] + li[now, :] * sr[prev, :])

        first = pl.ds(0, SEGMENTS)
        acc = (lr[first, :] * s0r + li[first, :] * s0i, -lr[first, :] * s0i + li[first, :] * s0r)

        def pairs(t, c):
            for j in range(SCAN_UNROLL):
                c = pair(1 + t * SCAN_UNROLL + j, c)
            return c

        whole = (T - 1) // SCAN_UNROLL
        acc = lax.fori_loop(0, whole, pairs, acc)
        for k in range(1 + whole * SCAN_UNROLL, T):
            acc = pair(k, acc)
        dar_ref[...] = jnp.sum(acc[0], axis=0, keepdims=True)
        dai_ref[...] = jnp.sum(acc[1], axis=0, keepdims=True)

        dbr_ref[...] = jnp.zeros_like(dbr_ref)
        dbi_ref[...] = jnp.zeros_like(dbi_ref)
        dcr_ref[...] = jnp.zeros_like(dcr_ref)
        dci_ref[...] = jnp.zeros_like(dci_ref)

        def outputs(c, carry):
            r = pl.ds(pl.multiple_of(c * rc, rc), rc)
            ub = u_ref[r, :]
            g = dy_ref[r, :]
            gb = g.astype(BF16)
            lrb = lr[r, :].astype(BF16)
            lib = li[r, :].astype(BF16)
            du_ref[r, :] = (lax.dot_general(lrb, br_ref[...], _NT, preferred_element_type=F32)
                            + lax.dot_general(lib, bi_ref[...], _NT, preferred_element_type=F32)
                            + d_ref[...] * g).astype(BF16)
            dbr_ref[...] += lax.dot_general(ub, lrb, _TN, preferred_element_type=F32)
            dbi_ref[...] += lax.dot_general(ub, lib, _TN, preferred_element_type=F32)
            dcr_ref[...] += lax.dot_general(sr[r, :].astype(BF16), gb, _TN, preferred_element_type=F32)
            dci_ref[...] -= lax.dot_general(si[r, :].astype(BF16), gb, _TN, preferred_element_type=F32)
            return carry

        lax.fori_loop(0, nc, outputs, 0)

    slab = pl.BlockSpec((S, LANES), lambda g: (0, g))
    bspec = pl.BlockSpec((None, LANES, ns), lambda g: (g, 0, 0))
    cspec = pl.BlockSpec((None, ns, LANES), lambda g: (g, 0, 0))
    aspec = pl.BlockSpec((None, 1, ns), lambda g: (g, 0, 0))
    return pl.pallas_call(
        body,
        out_shape=[jax.ShapeDtypeStruct((S, SW), BF16),
                   jax.ShapeDtypeStruct((nslab, LANES, ns), F32), jax.ShapeDtypeStruct((nslab, LANES, ns), F32),
                   jax.ShapeDtypeStruct((nslab, ns, LANES), F32), jax.ShapeDtypeStruct((nslab, ns, LANES), F32),
                   jax.ShapeDtypeStruct((nslab, 1, ns), F32), jax.ShapeDtypeStruct((nslab, 1, ns), F32)],
        grid=(nslab,), in_specs=[slab, slab] + _slab_specs(ns),
        out_specs=[slab, bspec, bspec, cspec, cspec, aspec, aspec],
        scratch_shapes=[pltpu.VMEM((S, ns), F32)] * 4,
        compiler_params=_params(("parallel",), VMEM_LIMIT_SCAN), name=name)(u, dy, bbr, bbi, ar, ai, cbr, cbi, dsk)


def _discretise(lam_re, lam_im, log_dt, b_re, b_im):
    dt = jnp.exp(log_dt)[:, None]
    mag = jnp.exp(lam_re * dt)
    ar = mag * jnp.cos(lam_im * dt)
    ai = mag * jnp.sin(lam_im * dt)
    nr, ni = ar - 1.0, ai
    den = lam_re * lam_re + lam_im * lam_im
    cr = ((nr * lam_re + ni * lam_im) / den)[..., None]
    ci = ((ni * lam_re - nr * lam_im) / den)[..., None]
    return ar, ai, cr * b_re - ci * b_im, cr * b_im + ci * b_re


def _block_diag(t, nslab):
    G, R, C = t.shape
    eye = jnp.eye(SLAB_GROUPS, dtype=t.dtype)
    t = t.reshape(nslab, SLAB_GROUPS, R, C)
    return jnp.einsum('sgrc,gh->sgrhc', t, eye).reshape(nslab, SLAB_GROUPS * R, SLAB_GROUPS * C)


def _block_diag_part(t, R, C):
    nslab = t.shape[0]
    eye = jnp.eye(SLAB_GROUPS, dtype=t.dtype)
    t = t.reshape(nslab, SLAB_GROUPS, R, SLAB_GROUPS, C)
    return jnp.einsum('sgrhc,gh->sgrc', t, eye).reshape(nslab * SLAB_GROUPS, R, C)


def _place():
    return lax.axis_index("x"), lax.axis_index("y"), lax.axis_index("c")


def all_gather(shards, *, name):
    nw = len(shards)

    def body(*refs):
        ins, outs = refs[:nw], refs[nw:2 * nw]
        send_sems, recv_sems, local_sems = refs[2 * nw:]
        x, y, c = _place()
        me, sibling = (x, y, c), (x, y, 1 - c)
        chips = [(1 - x, y), (x, 1 - y), (1 - x, 1 - y)]

        def copy(w, k, block, to, own):
            px, py, pc = block
            slot = outs[w].at[4 * px + 2 * py + pc]
            return pltpu.make_async_remote_copy(
                src_ref=ins[w] if own else slot, dst_ref=slot, send_sem=send_sems.at[w, k],
                recv_sem=recv_sems.at[w, k], device_id=to, device_id_type=MESH)

        mine = [pltpu.make_async_copy(ins[w], outs[w].at[4 * x + 2 * y + c], local_sems.at[w]) for w in range(nw)]
        for cp in mine:
            cp.start()
        first = []
        for w in range(nw):
            first.append(copy(w, 0, me, sibling, True))
            first += [copy(w, 1 + j, me, (*chip, c), True) for j, chip in enumerate(chips)]
        for cp in first:
            cp.start()
        passed = []
        for j, chip in enumerate(chips):
            for w in range(nw):
                copy(w, 1 + j, (*chip, c), me, False).wait_recv()
                cp = copy(w, 4 + j, (*chip, c), sibling, False)
                cp.start()
                passed.append(cp)
        for w in range(nw):
            copy(w, 0, sibling, me, False).wait_recv()
            for j, chip in enumerate(chips):
                copy(w, 4 + j, (*chip, 1 - c), me, False).wait_recv()
        for cp in first + passed:
            cp.wait_send()
        for cp in mine:
            cp.wait()

    anyspec = pl.BlockSpec(memory_space=pl.ANY)
    return pl.pallas_call(
        body, out_shape=[jax.ShapeDtypeStruct((N_DEV,) + s.shape, s.dtype) for s in shards],
        in_specs=[anyspec] * nw, out_specs=[anyspec] * nw,
        scratch_shapes=[pltpu.SemaphoreType.DMA((nw, 7)), pltpu.SemaphoreType.DMA((nw, 7)),
                        pltpu.SemaphoreType.DMA((nw,))],
        compiler_params=pltpu.CompilerParams(has_side_effects=True), name=name)(*shards)


_HBM = pl.BlockSpec(memory_space=pltpu.HBM)
_SEM = pl.BlockSpec(memory_space=pltpu.SEMAPHORE)
_ORDERED_EFFECT = pltpu.SideEffectType.DATAFLOW_SIDE_EFFECTING


def _split_call(name, srcs, zones, sems_in, n_new, body_fn, after):
    nsrc, nz, ns, nn = len(srcs), len(zones), len(sems_in), len(n_new)
    nb = nsrc + nz

    def body(*refs):
        outs = refs[nb + ns + 1:]
        body_fn(refs[:nb], refs[nb:nb + ns], outs[:nn])
        outs[nn + nz][...] = jnp.zeros((SUBLANES, LANES), F32)

    res = pl.pallas_call(
        body, name=name,
        out_shape=([pltpu.SemaphoreType.DMA((n,)) for n in n_new] + [pltpu.HBM(b.shape, b.dtype) for b in zones]
                   + [jax.ShapeDtypeStruct((SUBLANES, LANES), F32)]),
        in_specs=[_HBM] * nb + [_SEM] * ns + [_ANY],
        out_specs=[_SEM] * nn + [_HBM] * nz + [pl.BlockSpec(memory_space=pltpu.VMEM)],
        input_output_aliases={nsrc + i: nn + i for i in range(nz)},
        compiler_params=pltpu.CompilerParams(has_side_effects=_ORDERED_EFFECT))(
            *[pltpu.with_memory_space_constraint(b, pltpu.HBM) for b in list(srcs) + list(zones)], *sems_in, after)
    return list(res[:nn]), list(res[nn:nn + nz]), res[-1]


def _mesh_peers():
    x, y, c = _place()
    return x, y, c, (x, y, 1 - c), [(1 - x, y), (x, 1 - y), (1 - x, 1 - y)]


def gather_start(shards, after, *, name):
    nw = len(shards)
    x, y, c = _place()
    zones = [lax.dynamic_update_slice(lax.empty((N_DEV,) + s.shape, s.dtype), s[None], (4 * x + 2 * y + c, 0, 0))
             for s in shards]

    def body(bufs, taken, new):
        for cp in _gather_first(bufs, nw, new[0], new[1]):
            cp.start()

    sems, zones, token = _split_call(name, shards, zones, [], [4 * nw, 4 * nw], body, after)
    return shards, sems, zones, token


def _gather_first(bufs, nw, send, recv):
    x, y, c, sibling, chips = _mesh_peers()
    out = []
    for w in range(nw):
        slot = bufs[nw + w].at[4 * x + 2 * y + c]
        for k, to in enumerate([sibling] + [(*ch, c) for ch in chips]):
            out.append(pltpu.make_async_remote_copy(
                src_ref=bufs[w], dst_ref=slot, send_sem=send.at[4 * w + k], recv_sem=recv.at[4 * w + k],
                device_id=to, device_id_type=MESH))
    return out


def _gather_slot_copy(bufs, nw, w, block, send_sem, recv_sem, to):
    px, py, pc = block
    slot = bufs[nw + w].at[4 * px + 2 * py + pc]
    return pltpu.make_async_remote_copy(src_ref=slot, dst_ref=slot, send_sem=send_sem, recv_sem=recv_sem,
                                        device_id=to, device_id_type=MESH)


def gather_forward(state, after, *, name):
    shards, sems, zones, _ = state
    nw = len(shards)

    def body(bufs, taken, new):
        x, y, c, sibling, chips = _mesh_peers()
        for j, ch in enumerate(chips):
            for w in range(nw):
                k = 4 * w + 1 + j
                _gather_slot_copy(bufs, nw, w, (*ch, c), taken[0].at[k], taken[1].at[k], (*ch, c)).wait_recv()
                _gather_slot_copy(bufs, nw, w, (*ch, c), new[0].at[3 * w + j], new[1].at[3 * w + j], sibling).start()
        for w in range(nw):
            _gather_slot_copy(bufs, nw, w, sibling, taken[0].at[4 * w], taken[1].at[4 * w], sibling).wait_recv()
        for cp in _gather_first(bufs, nw, taken[0], taken[1]):
            cp.wait_send()

    sems, zones, token = _split_call(name, shards, zones, sems, [3 * nw, 3 * nw], body, after)
    return shards, sems, zones, token


def gather_finish(state, after, *, name):
    shards, sems, zones, _ = state
    nw = len(shards)

    def body(bufs, taken, new):
        x, y, c, sibling, chips = _mesh_peers()
        for w in range(nw):
            for j, ch in enumerate(chips):
                cp = _gather_slot_copy(bufs, nw, w, (*ch, 1 - c), taken[0].at[3 * w + j], taken[1].at[3 * w + j], sibling)
                cp.wait_send()
                cp.wait_recv()

    _, zones, _ = _split_call(name, shards, zones, sems, [], body, after)
    return zones


def exchange_start(srcs, zone_shapes, copies, n, after, *, name):
    nw = len(srcs)
    zones = [lax.empty(z, s.dtype) for z, s in zip(zone_shapes, srcs)]

    def body(bufs, taken, new):
        for cp in copies(bufs[:nw], bufs[nw:], new[0], new[1]):
            cp.start()

    sems, zones, token = _split_call(name, srcs, zones, [], [n, n], body, after)
    return srcs, copies, sems, zones, token


def exchange_wait(state, after, *, name):
    srcs, copies, sems, zones, _ = state
    nw = len(srcs)

    def body(bufs, taken, new):
        for cp in copies(bufs[:nw], bufs[nw:], taken[0], taken[1]):
            cp.wait_send()
            cp.wait_recv()

    _, zones, _ = _split_call(name, srcs, zones, sems, [], body, after)
    return zones


def _core_copies(srcs, zones, send, recv):
    x, y, c = _place()
    return [pltpu.make_async_remote_copy(
        src_ref=srcs[w].at[:, 1 - c], dst_ref=zones[w], send_sem=send.at[w], recv_sem=recv.at[w],
        device_id=(x, y, 1 - c), device_id_type=MESH) for w in range(len(srcs))]


def _chip_copies(srcs, zones, send, recv):
    x, y, c = _place()
    chips = [(1 - x, y), (x, 1 - y), (1 - x, 1 - y)]
    return [pltpu.make_async_remote_copy(
        src_ref=srcs[w].at[2 * cx + cy], dst_ref=zones[w].at[j], send_sem=send.at[3 * w + j],
        recv_sem=recv.at[3 * w + j], device_id=(cx, cy, c), device_id_type=MESH)
        for w in range(len(srcs)) for j, (cx, cy) in enumerate(chips)]


def _blocked(fn, ins, outs, *, name, place=None, tr=256):
    k, n = outs[0][0]
    tr = _tile(k, tr, 16)
    if place is None:
        place = jnp.zeros((1,), jnp.int32)
    specs = []
    args = []
    for a in ins:
        if isinstance(a, tuple):
            arr, lead = a
            specs.append(pl.BlockSpec((None, tr, n), functools.partial(lambda i, s, lead: (*lead(i, s), 0), lead=lead)))
            args.append(arr)
        else:
            specs.append(pl.BlockSpec((tr, n), lambda i, s: (i, 0)))
            args.append(a)
    nin = len(args)

    def body(place_ref, *refs):
        res = fn(*[r[...] for r in refs[:nin]])
        for ref, val in zip(refs[nin:], res):
            ref[...] = val.astype(ref.dtype)

    return pl.pallas_call(
        body, out_shape=[jax.ShapeDtypeStruct(s, d) for s, d in outs],
        grid_spec=pltpu.PrefetchScalarGridSpec(
            num_scalar_prefetch=1, grid=(k // tr,), in_specs=specs,
            out_specs=[pl.BlockSpec((tr, n), lambda i, s: (i, 0)) for _ in outs]),
        compiler_params=_params(("parallel",)), name=name)(place, *args)


def _adamw(w, g, m, v):
    m = ADAM_B1 * m + (1.0 - ADAM_B1) * g
    v = ADAM_B2 * v + (1.0 - ADAM_B2) * (g * g)
    m_hat = m / (1.0 - ADAM_B1 ** ADAM_STEP)
    v_hat = v / (1.0 - ADAM_B2 ** ADAM_STEP)
    delta = -ADAM_LR * (m_hat / (jnp.sqrt(v_hat) + ADAM_EPS) + ADAM_WD * w)
    return delta, m, v


def kernel(x, p, mix_norm_pre, w_in, lam_re, lam_im, log_dt, ssm_b_re, ssm_b_im, ssm_c_re, ssm_c_im, ssm_d, w_glu, b_glu, attn_out_norm, ssm_out_norm, w_out, mix_norm_post, mlp_norm_pre, w_up, w_down, mlp_norm_post, ple_norm_pre, w_ple_gate, w_ple_proj, ple_norm_post, loss_target, m_mix_norm_pre, m_w_in, m_lam_re, m_lam_im, m_log_dt, m_ssm_b_re, m_ssm_b_im, m_ssm_c_re, m_ssm_c_im, m_ssm_d, m_w_glu, m_b_glu, m_attn_out_norm, m_ssm_out_norm, m_w_out, m_mix_norm_post, m_mlp_norm_pre, m_w_up, m_w_down, m_mlp_norm_post, m_ple_norm_pre, m_w_ple_gate, m_w_ple_proj, m_ple_norm_post, v_mix_norm_pre, v_w_in, v_lam_re, v_lam_im, v_log_dt, v_ssm_b_re, v_ssm_b_im, v_ssm_c_re, v_ssm_c_im, v_ssm_d, v_w_glu, v_b_glu, v_attn_out_norm, v_ssm_out_norm, v_w_out, v_mix_norm_post, v_mlp_norm_pre, v_w_up, v_w_down, v_mlp_norm_post, v_ple_norm_pre, v_w_ple_gate, v_w_ple_proj, v_ple_norm_post):
    weights = dict(mix_norm_pre=mix_norm_pre, w_in=w_in, lam_re=lam_re, lam_im=lam_im, log_dt=log_dt, ssm_b_re=ssm_b_re, ssm_b_im=ssm_b_im, ssm_c_re=ssm_c_re, ssm_c_im=ssm_c_im, ssm_d=ssm_d, w_glu=w_glu, b_glu=b_glu, attn_out_norm=attn_out_norm, ssm_out_norm=ssm_out_norm, w_out=w_out, mix_norm_post=mix_norm_post, mlp_norm_pre=mlp_norm_pre, w_up=w_up, w_down=w_down, mlp_norm_post=mlp_norm_post, ple_norm_pre=ple_norm_pre, w_ple_gate=w_ple_gate, w_ple_proj=w_ple_proj, ple_norm_post=ple_norm_post)
    mom_m = dict(mix_norm_pre=m_mix_norm_pre, w_in=m_w_in, lam_re=m_lam_re, lam_im=m_lam_im, log_dt=m_log_dt, ssm_b_re=m_ssm_b_re, ssm_b_im=m_ssm_b_im, ssm_c_re=m_ssm_c_re, ssm_c_im=m_ssm_c_im, ssm_d=m_ssm_d, w_glu=m_w_glu, b_glu=m_b_glu, attn_out_norm=m_attn_out_norm, ssm_out_norm=m_ssm_out_norm, w_out=m_w_out, mix_norm_post=m_mix_norm_post, mlp_norm_pre=m_mlp_norm_pre, w_up=m_w_up, w_down=m_w_down, mlp_norm_post=m_mlp_norm_post, ple_norm_pre=m_ple_norm_pre, w_ple_gate=m_w_ple_gate, w_ple_proj=m_w_ple_proj, ple_norm_post=m_ple_norm_post)
    mom_v = dict(mix_norm_pre=v_mix_norm_pre, w_in=v_w_in, lam_re=v_lam_re, lam_im=v_lam_im, log_dt=v_log_dt, ssm_b_re=v_ssm_b_re, ssm_b_im=v_ssm_b_im, ssm_c_re=v_ssm_c_re, ssm_c_im=v_ssm_c_im, ssm_d=v_ssm_d, w_glu=v_w_glu, b_glu=v_b_glu, attn_out_norm=v_attn_out_norm, ssm_out_norm=v_ssm_out_norm, w_out=v_w_out, mix_norm_post=v_mix_norm_post, mlp_norm_pre=v_mlp_norm_pre, w_up=v_w_up, w_down=v_w_down, mlp_norm_post=v_mlp_norm_post, ple_norm_pre=v_ple_norm_pre, w_ple_gate=v_w_ple_gate, w_ple_proj=v_w_ple_proj, ple_norm_post=v_ple_norm_post)
    order = list(weights)
    big = ["w_in", "w_glu", "w_out", "w_up", "w_down", "w_ple_gate", "w_ple_proj"]
    col_sharded = {"w_in", "w_up", "w_ple_proj"}
    small = [n for n in order if n not in big]

    _, S, D = x.shape
    xs = x[0]
    tgt = loss_target[0]
    AW = attn_out_norm.shape[1]
    SW = ssm_d.shape[1]
    H = AW // HEAD_DIM
    G = SW // SSM_GROUP
    nslab = G // SLAB_GROUPS
    P_, C_ = SSM_STATE, SSM_GROUP

    shard = {n: weights[n][0].astype(BF16) for n in big}
    W, WT = {}, {}

    def arrived(names, gathered):
        for n, g in zip(names, gathered):
            W[n] = g if n in col_sharded else g.reshape(1, N_DEV * g.shape[1], g.shape[2])

    def transposed(g):
        return jnp.swapaxes(g, 1, 2).reshape(1, g.shape[0] * g.shape[2], g.shape[1])

    arrived(["w_in"], all_gather([shard["w_in"]], name="gather_w_in"))
    WT["w_in"] = transposed(W["w_in"])
    early, late = ["w_glu", "w_out"], ["w_up", "w_down", "w_ple_gate", "w_ple_proj"]
    gather_early = gather_start([shard[n] for n in early], W["w_in"], name="gather_early_start")
    gather_late = gather_start([shard[n] for n in late], gather_early[-1], name="gather_late_start")

    g1, g2, g3, g4, g5, g6 = (weights[n] for n in ("mix_norm_pre", "mix_norm_post", "mlp_norm_pre",
                                                      "mlp_norm_post", "ple_norm_pre", "ple_norm_post"))
    ga, gs = attn_out_norm, ssm_out_norm
    (hn1,) = rowwise(lambda a, g: (_rms(a, g),), [xs], [g1], [(D, BF16)], deps=(gather_late[-1],), name="norm_in")
    (proj,) = mm_nn(hn1, W["w_in"], [F32], name="proj_in")
    attn, lse = attn_fwd(proj, H, name="attn_fwd")
    gather_early = gather_forward(gather_early, attn, name="gather_early_forward")
    (mix_a,) = rowwise(lambda a, g: (_rms(a, g),), [attn], [ga], [(AW, BF16)], deps=(gather_early[-1],),
                       name="attn_norm")
    arrived(early, gather_finish(gather_early, mix_a, name="gather_early_finish"))

    a_r, a_i, bb_r, bb_i = _discretise(lam_re[0], lam_im[0], log_dt[0], ssm_b_re[0], ssm_b_im[0])
    ssm_consts = (_block_diag(bb_r.swapaxes(1, 2), nslab).astype(BF16), _block_diag(bb_i.swapaxes(1, 2), nslab).astype(BF16),
                  a_r.reshape(nslab, 1, SLAB_STATES), a_i.reshape(nslab, 1, SLAB_STATES),
                  _block_diag(ssm_c_re[0].swapaxes(1, 2), nslab).astype(BF16),
                  _block_diag(ssm_c_im[0].swapaxes(1, 2), nslab).astype(BF16), ssm_d)
    u_seg = _to_segments(proj[:, 3 * AW:]).astype(BF16)
    y_pre = ssm_fwd(u_seg, *ssm_consts, name="ssm_fwd")
    gather_late = gather_forward(gather_late, y_pre, name="gather_late_forward")
    (yg,) = rowwise(lambda a: (_gelu(a),), [y_pre], [], [(SW, BF16)], deps=(gather_late[-1],), name="ssm_gelu")
    (gl1,) = mm_nn(yg, W["w_glu"], [F32], epi=lambda acc, b: (acc + b,), bias=b_glu, name="glu_gate")
    (mix_s,) = rowwise(lambda yp, gl, g: (_rms(_gelu(yp) * _sigmoid(gl), g),), [y_pre, gl1], [gs], [(SW, BF16)],
                       name="ssm_glu_norm")
    mixed = jnp.concatenate([mix_a, _from_segments(mix_s)], axis=1)
    (mo,) = mm_nn(mixed, W["w_out"], [F32], name="mix_out")

    def resid_norm(h, t, gpost, gpre):
        hh = h + _rms(t, gpost)
        return hh, _rms(hh, gpre)

    h1, hn2 = rowwise(resid_norm, [xs, mo], [g2, g3], [(D, F32), (D, BF16)], name="resid_mix")
    arrived(late, gather_finish(gather_late, hn2, name="gather_late_finish"))
    WT["w_up"] = transposed(W["w_up"])

    def relu2(acc):
        r = jnp.maximum(acc, 0.0)
        return acc, r * r

    up, act = mm_nn(hn2, W["w_up"], [BF16, BF16], epi=relu2, tm=1024, tn=1024, name="mlp_up")
    (ff,) = mm_nn(act, W["w_down"], [F32], name="mlp_down")
    h2, hn3 = rowwise(resid_norm, [h1, ff], [g4, g5], [(D, F32), (D, BF16)], name="resid_mlp")
    (gl2,) = mm_nn(hn3, W["w_ple_gate"], [F32], name="ple_gate")
    pb = p[0, 0].astype(BF16)
    (emb,) = mm_nn(pb, W["w_ple_proj"], [F32], name="ple_proj")

    def head(h, gl, e, t, g):
        sg = _sigmoid(gl)
        ge = sg * e
        err = h + _rms(ge, g) - t
        dh = err * (1.0 / D)
        dge, dg = _rms_bwd(dh, ge, g)
        return dh, dge * e * sg * (1.0 - sg), dge * sg, jnp.sum(err * err, axis=0, keepdims=True), dg

    dh3, dgl2, demb, loss_part, dg6 = rowwise(head, [h2, gl2, emb, tgt], [g6], [(D, F32), (D, BF16), (D, BF16)],
                                             [D, D], name="ple_loss_head")
    loss = lax.psum(0.5 / D * jnp.sum(loss_part), ("x", "y", "c"))

    x_i, y_i, c_i = _place()
    place = jnp.stack([c_i, 2 * x_i + y_i]).astype(jnp.int32)
    grads, out_g, out_d, out_m, out_v = {}, {}, {}, {}, {}

    def to_sibling(names, after, tag):
        chunks = []
        for n in names:
            g = grads[n]
            g = g if n in col_sharded else g.reshape(N_DEV, g.shape[1] // N_DEV, g.shape[2])
            chunks.append(g.reshape(4, 2, g.shape[1], g.shape[2]))
        return chunks, exchange_start(chunks, [(4,) + g.shape[2:] for g in chunks], _core_copies, len(chunks), after,
                                      name=f"grads_to_sibling_{tag}")

    def to_chips(names, sent, after, tag):
        chunks, state = sent
        sums = []
        for n, g, r in zip(names, chunks, exchange_wait(state, after, name=f"grads_from_sibling_{tag}")):
            k, nn = g.shape[2], g.shape[3]
            kb = k // _tile(k, 512, 16)

            def mine(i, s, kb=kb):
                return 2 * (i // kb) + s[0], i % kb

            (s,) = _blocked(lambda a, b: (a.astype(F32) + b.astype(F32),),
                            [(g.reshape(N_DEV, k, nn), mine), r.reshape(4 * k, nn)],
                            [((4 * k, nn), BF16)], place=place, tr=k // kb, name=f"chip_sum_{n}")
            sums.append(s.reshape(4, k, nn))
        return sums, exchange_start(sums, [(3,) + s.shape[1:] for s in sums], _chip_copies, 3 * len(sums), sums[-1],
                                    name=f"grads_to_chips_{tag}")

    def update(w_, m_, v_, own, r0, r1, r2):
        g = own.astype(F32) + r0.astype(F32) + r1.astype(F32) + r2.astype(F32)
        return (g,) + _adamw(w_, g, m_, v_)

    def finish(names, sent, after, tag):
        sums, state = sent
        for n, s, r in zip(names, sums, exchange_wait(state, after, name=f"grads_from_chips_{tag}")):
            shp = weights[n].shape
            res = _blocked(update, [weights[n][0], mom_m[n][0], mom_v[n][0], (s, lambda i, p_: (p_[1], i)),
                                    (r, lambda i, p_: (0, i)), (r, lambda i, p_: (1, i)), (r, lambda i, p_: (2, i))],
                           [(shp[1:], F32)] * 4, place=place, name=f"adamw_{n}")
            out_g[n], out_d[n], out_m[n], out_v[n] = (t.reshape(shp) for t in res)
        return out_v[names[-1]]

    grads["w_ple_proj"] = mm_tn(pb, demb, N_DEV, name="grad_w_ple_proj")
    dhn3 = mm_nt(dgl2, W["w_ple_gate"], F32, name="back_ple_gate")
    grads["w_ple_gate"] = mm_tn(hn3, dgl2, 1, name="grad_w_ple_gate")

    def back_resid(dh, dhn, h, t, gpre, gpost):
        d1, dgpre = _rms_bwd(dhn, h, gpre)
        dhh = dh + d1
        dt, dgpost = _rms_bwd(dhh, t, gpost)
        return dhh, dt, dgpre, dgpost

    dh2, dff, dg5, dg4 = rowwise(back_resid, [dh3, dhn3, h2, ff], [g5, g4], [(D, F32), (D, BF16)], [D, D],
                                 name="back_resid_mlp")
    dup = mm_nt(dff, W["w_down"], BF16, epi=lambda acc, u_: (acc * 2.0 * jnp.maximum(u_.astype(F32), 0.0),),
                extra=up, name="back_mlp_down")
    grads["w_down"] = mm_tn(act, dff, 1, name="grad_w_down")
    group_a = ["w_ple_proj", "w_ple_gate", "w_down"]
    sent_a = to_sibling(group_a, grads["w_down"], "a")
    (dhn2,) = mm_nn(dup, WT["w_up"], [F32], deps=(sent_a[1][-1],), name="back_mlp_up")
    sent_a = to_chips(group_a, sent_a, dhn2, "a")
    grads["w_up"] = mm_tn(hn2, dup, N_DEV, deps=(sent_a[1][-1],), name="grad_w_up")
    dh1, dmo, dg3, dg2 = rowwise(back_resid, [dh2, dhn2, h1, mo], [g3, g2], [(D, F32), (D, BF16)], [D, D],
                                 name="back_resid_mix")
    dmixed = mm_nt(dmo, W["w_out"], F32, name="back_mix_out")
    grads["w_out"] = mm_tn(mixed, dmo, 1, name="grad_w_out")

    def back_glu(dm, yp, gl, g):
        ygf = _gelu(yp)
        sg = _sigmoid(gl)
        dssm, dg = _rms_bwd(dm, ygf * sg, g)
        dgl = dssm * ygf * sg * (1.0 - sg)
        return dgl, dssm * sg, dg, jnp.sum(dgl, axis=0, keepdims=True)

    dgl1, dyg_direct, dgs, db_glu = rowwise(back_glu, [_to_segments(dmixed[:, AW:]), y_pre, gl1], [gs],
                                            [(SW, BF16), (SW, F32)], [SW, SW], name="back_glu")
    dyg_gate = mm_nt(dgl1, W["w_glu"], F32, name="back_glu_gate")
    grads["w_glu"] = mm_tn(yg, dgl1, 1, name="grad_w_glu")
    group_b = ["w_up", "w_out", "w_glu"]
    sent_b = to_sibling(group_b, grads["w_glu"], "b")
    done_a = finish(group_a, sent_a, sent_b[1][-1], "a")

    def back_gelu(d1, d2, yp, u_):
        dy = (d1 + d2) * _gelu_grad(yp)
        return dy, jnp.sum(dy * u_.astype(F32), axis=0, keepdims=True)

    dy_pre, d_skip = rowwise(back_gelu, [dyg_direct, dyg_gate, y_pre, u_seg], [], [(SW, F32)], [SW], deps=(done_a,),
                             name="back_gelu")
    du_seg, dbb_r, dbb_i, dcb_r, dcb_i, da_r, da_i = ssm_bwd(u_seg, dy_pre, *ssm_consts, name="ssm_bwd")
    sent_b = to_chips(group_b, sent_b, du_seg, "b")

    def back_attn_norm(dm, a, g):
        da, dg = _rms_bwd(dm, a, g)
        prod = da * a
        delta = jnp.concatenate(
            [jnp.broadcast_to(jnp.sum(prod[:, h * HEAD_DIM:(h + 1) * HEAD_DIM], axis=-1, keepdims=True),
                              (prod.shape[0], HEAD_DIM)) for h in range(H)], axis=1)
        return da, delta, dg

    dattn, delta, dga = rowwise(back_attn_norm, [(dmixed, AW, 0), attn], [ga], [(AW, F32), (AW, F32)], [AW],
                                deps=(sent_b[1][-1],), name="back_attn_norm")
    dq, dk, dv = attn_bwd(proj, dattn, lse, delta, H, name="attn_bwd")
    dproj = jnp.concatenate([dq, dk, dv, _from_segments(du_seg)], axis=1)
    (dhn1,) = mm_nn(dproj, WT["w_in"], [F32], name="back_proj_in")

    def back_in(dh, dhn, a, g):
        d1, dg = _rms_bwd(dhn, a, g)
        return dh + d1, dg

    grad_x, dg1 = rowwise(back_in, [dh1, dhn1, xs], [g1], [(D, F32)], [D], name="back_norm_in")

    cot = dict(
        mix_norm_pre=dg1, mix_norm_post=dg2, mlp_norm_pre=dg3, mlp_norm_post=dg4, ple_norm_pre=dg5, ple_norm_post=dg6,
        attn_out_norm=dga, ssm_out_norm=dgs, b_glu=db_glu, ssm_d=d_skip,
        ssm_c_re=_block_diag_part(dcb_r, P_, C_).swapaxes(1, 2), ssm_c_im=_block_diag_part(dcb_i, P_, C_).swapaxes(1, 2),
        a_r=da_r.reshape(G, P_), a_i=da_i.reshape(G, P_),
        bb_r=_block_diag_part(dbb_r, C_, P_).swapaxes(1, 2), bb_i=_block_diag_part(dbb_i, C_, P_).swapaxes(1, 2))
    names = list(cot)
    flat = jnp.concatenate([cot[n].reshape(-1) for n in names])
    total = flat.shape[0]
    rows_ = -(-total // (LANES * 16)) * 16
    flat = jnp.pad(flat, (0, rows_ * LANES - total)).reshape(rows_, LANES)
    gather_small = gather_start([flat], flat, name="gather_small_start")
    grads["w_in"] = mm_tn(hn1, dproj, N_DEV, deps=(gather_small[-1],), name="grad_w_in")
    group_c = ["w_in"]
    sent_c = to_sibling(group_c, grads["w_in"], "c")
    done_b = finish(group_b, sent_b, sent_c[1][-1], "b")
    sent_c = to_chips(group_c, sent_c, done_b, "c")
    gather_small = gather_forward(gather_small, sent_c[1][-1], name="gather_small_forward")
    (every,) = gather_finish(gather_small, gather_small[-1], name="gather_small_finish")
    (summed,) = _blocked(lambda *t: (functools.reduce(lambda a, b: a + b, t),),
                         [(every, functools.partial(lambda i, p_, j: (j, i), j=j)) for j in range(N_DEV)],
                         [((rows_, LANES), F32)], name="sum_small_grads")
    summed = summed.reshape(-1)
    red, off = {}, 0
    for n in names:
        sz = cot[n].size
        red[n] = summed[off:off + sz].reshape(cot[n].shape)
        off += sz
    _, pull = jax.vjp(_discretise, lam_re[0], lam_im[0], log_dt[0], ssm_b_re[0], ssm_b_im[0])
    d_lre, d_lim, d_ldt, d_bre, d_bim = pull((red["a_r"], red["a_i"], red["bb_r"], red["bb_i"]))
    red.update(lam_re=d_lre, lam_im=d_lim, log_dt=d_ldt, ssm_b_re=d_bre, ssm_b_im=d_bim)

    def pack(d):
        t = jnp.concatenate([d[n].reshape(-1) for n in small])
        r_ = -(-t.shape[0] // (LANES * 16)) * 16
        return jnp.pad(t, (0, r_ * LANES - t.shape[0])).reshape(r_, LANES)

    sw, sg_, sm, sv = pack(weights), pack(red), pack(mom_m), pack(mom_v)
    sd, snm, snv = _blocked(lambda w_, g_, m_, v_: _adamw(w_, g_, m_, v_), [sw, sg_, sm, sv],
                            [(sw.shape, F32)] * 3, name="adamw_small")
    finish(group_c, sent_c, snv, "c")
    off = 0
    for n in small:
        sz = weights[n].size
        shp = weights[n].shape
        out_g[n] = red[n].reshape(shp)
        out_d[n] = sd.reshape(-1)[off:off + sz].reshape(shp)
        out_m[n] = snm.reshape(-1)[off:off + sz].reshape(shp)
        out_v[n] = snv.reshape(-1)[off:off + sz].reshape(shp)
        off += sz

    return (loss, grad_x[None], *[out_g[n] for n in order], *[out_d[n] for n in order],
            *[out_m[n] for n in order], *[out_v[n] for n in order])
```

```python
import functools
import math

import jax
import jax.numpy as jnp
from jax import lax
from jax.experimental import pallas as pl
from jax.experimental.pallas import tpu as pltpu

F32 = jnp.float32
BF16 = jnp.bfloat16
MESH = pl.DeviceIdType.MESH

N_DEV = 8
LANES = 128
SUBLANES = 8
VMEM_LIMIT = 48 * 1024 * 1024
VMEM_LIMIT_SCAN = 60 * 1024 * 1024

HEAD_DIM = 128
BLK = 128
DILATIONS = (1, 4, 16)
SSM_GROUP = 16
SSM_STATE = 64
SLAB_GROUPS = LANES // SSM_GROUP
SLAB_STATES = SLAB_GROUPS * SSM_STATE
SEGMENTS = SUBLANES
SCAN_UNROLL = 4
RMS_EPS = 1e-6
NEG_INF = -1e30

ADAM_LR = 0.001
ADAM_B1 = 0.9
ADAM_B2 = 0.999
ADAM_EPS = 1e-08
ADAM_WD = 0.01
ADAM_STEP = 10


def _tile(n, pref, unit=LANES):
    if n <= pref:
        return n
    t = (pref // unit) * unit
    while t > unit and n % t:
        t -= unit
    assert n % t == 0, (n, pref, unit)
    return t


def _params(sem=None, vmem=VMEM_LIMIT):
    return pltpu.CompilerParams(dimension_semantics=sem, vmem_limit_bytes=vmem)


_NN = (((1,), (0,)), ((), ()))
_NT = (((1,), (1,)), ((), ()))
_TN = (((0,), (0,)), ((), ()))


_ANY = pl.BlockSpec(memory_space=pl.ANY)


def _mm_call(dims, nk, n_extra, n_dep, n_out, epi, **kw):
    first_out = 2 + n_extra + n_dep
    kw["in_specs"] = list(kw["in_specs"]) + [_ANY] * n_dep

    def single(*refs):
        extra = refs[2:2 + n_extra]
        res = epi(lax.dot_general(refs[0][...], refs[1][...], dims, preferred_element_type=F32),
                  *[e[...] for e in extra])
        for o, r in zip(refs[first_out:first_out + n_out], res):
            o[...] = r.astype(o.dtype)

    if nk == 1:
        kw["scratch_shapes"] = []
        return pl.pallas_call(single, **kw)

    def body(*refs):
        a_ref, b_ref = refs[0], refs[1]
        extra = refs[2:2 + n_extra]
        outs = refs[first_out:first_out + n_out]
        acc = refs[-1]
        k = pl.program_id(2)

        @pl.when(k == 0)
        def _():
            acc[...] = jnp.zeros_like(acc)

        acc[...] += lax.dot_general(a_ref[...], b_ref[...], dims, preferred_element_type=F32)

        @pl.when(k == nk - 1)
        def _():
            res = epi(acc[...], *[e[...] for e in extra])
            for o, r in zip(outs, res):
                o[...] = r.astype(o.dtype)

    return pl.pallas_call(body, **kw)


def _identity_epi(acc):
    return (acc,)


def mm_nn(a, w, out_dtypes, *, name, epi=_identity_epi, bias=None, deps=(), tm=2048, tn=512, tk=2048):
    M, K = a.shape
    J, K2, n = w.shape
    assert K == K2
    tm, tn, tk = _tile(M, tm, 16), _tile(n, tn), _tile(K, tk)
    npj = n // tn
    nk = K // tk
    in_specs = [pl.BlockSpec((tm, tk), lambda i, j, k: (i, k)),
                pl.BlockSpec((None, tk, tn), lambda i, j, k: (j // npj, k, j % npj))]
    args = [a, w]
    if bias is not None:
        in_specs.append(pl.BlockSpec((1, tn), lambda i, j, k: (0, j)))
        args.append(bias)
    return _mm_call(
        _NN, nk, len(args) - 2, len(deps), len(out_dtypes), epi,
        out_shape=[jax.ShapeDtypeStruct((M, J * n), d) for d in out_dtypes],
        grid=(M // tm, J * npj, nk), in_specs=in_specs,
        out_specs=[pl.BlockSpec((tm, tn), lambda i, j, k: (i, j)) for _ in out_dtypes],
        scratch_shapes=[pltpu.VMEM((tm, tn), F32)],
        compiler_params=_params(("parallel", "parallel", "arbitrary")), name=name)(*args, *deps)


def mm_nt(a, w, out_dtype, *, name, epi=_identity_epi, extra=None, tm=2048, tko=512, tnr=2048):
    M, N = a.shape
    J, K, n = w.shape
    assert N == J * n
    tm, tko, tnr = _tile(M, tm, 16), _tile(K, tko), _tile(n, tnr)
    npj = n // tnr
    nk = N // tnr
    in_specs = [pl.BlockSpec((tm, tnr), lambda i, j, k: (i, k)),
                pl.BlockSpec((None, tko, tnr), lambda i, j, k: (k // npj, j, k % npj))]
    args = [a, w]
    if extra is not None:
        in_specs.append(pl.BlockSpec((tm, tko), lambda i, j, k: (i, j)))
        args.append(extra)
    return _mm_call(
        _NT, nk, len(args) - 2, 0, 1, epi,
        out_shape=[jax.ShapeDtypeStruct((M, K), out_dtype)],
        grid=(M // tm, K // tko, nk), in_specs=in_specs,
        out_specs=[pl.BlockSpec((tm, tko), lambda i, j, k: (i, j))],
        scratch_shapes=[pltpu.VMEM((tm, tko), F32)],
        compiler_params=_params(("parallel", "parallel", "arbitrary")), name=name)(*args)[0]


def mm_tn(a, b, J, *, name, deps=(), tko=1024, tn=1024, ts=2048):
    S, K = a.shape
    S2, N = b.shape
    assert S == S2 and N % J == 0
    n = N // J
    tko, tn, ts = _tile(K, tko), _tile(n, tn), _tile(S, ts)
    npj = n // tn
    nk = S // ts
    return _mm_call(
        _TN, nk, 0, len(deps), 1, _identity_epi,
        out_shape=[jax.ShapeDtypeStruct((J, K, n), BF16)],
        grid=(K // tko, J * npj, nk),
        in_specs=[pl.BlockSpec((ts, tko), lambda i, j, k: (k, i)),
                  pl.BlockSpec((ts, tn), lambda i, j, k: (k, j))],
        out_specs=[pl.BlockSpec((None, tko, tn), lambda i, j, k: (j // npj, i, j % npj))],
        scratch_shapes=[pltpu.VMEM((tko, tn), F32)],
        compiler_params=_params(("parallel", "parallel", "arbitrary")), name=name)(a, b, *deps)[0]


def rowwise(fn, rows, vecs, outs, accs=(), *, name, deps=(), ts=256):
    rows = [r if isinstance(r, tuple) else (r, r.shape[1], 0) for r in rows]
    S = rows[0][0].shape[0]
    ts = _tile(S, ts, 16)
    nr, nv, no, nd = len(rows), len(vecs), len(outs), len(deps)

    def body(*refs):
        r, v = refs[:nr], refs[nr:nr + nv]
        o, a = refs[nr + nv + nd:nr + nv + nd + no], refs[nr + nv + nd + no:]
        res = fn(*[t[...].astype(F32) for t in r], *[t[...] for t in v])
        for ref, val in zip(o, res[:no]):
            ref[...] = val.astype(ref.dtype)
        if a:
            @pl.when(pl.program_id(0) == 0)
            def _():
                for ref in a:
                    ref[...] = jnp.zeros_like(ref)

            for ref, val in zip(a, res[no:]):
                ref[...] += val

    in_specs = [pl.BlockSpec((ts, w), functools.partial(lambda i, cb: (i, cb), cb=cb)) for _, w, cb in rows]
    in_specs += [pl.BlockSpec(v.shape, lambda i: (0, 0)) for v in vecs] + [_ANY] * nd
    out_shape = [jax.ShapeDtypeStruct((S, w), d) for w, d in outs]
    out_shape += [jax.ShapeDtypeStruct((1, w), F32) for w in accs]
    out_specs = [pl.BlockSpec((ts, w), lambda i: (i, 0)) for w, _ in outs]
    out_specs += [pl.BlockSpec((1, w), lambda i: (0, 0)) for w in accs]
    return pl.pallas_call(body, out_shape=out_shape, grid=(S // ts,), in_specs=in_specs, out_specs=out_specs,
                          compiler_params=_params(("arbitrary",)), name=name)(*[r[0] for r in rows], *vecs, *deps)


def _rms(x, g):
    r = lax.rsqrt(jnp.mean(x * x, axis=-1, keepdims=True) + RMS_EPS)
    return x * r * g


def _rms_bwd(dy, x, g):
    r = lax.rsqrt(jnp.mean(x * x, axis=-1, keepdims=True) + RMS_EPS)
    xh = x * r
    dxh = dy * g
    dx = r * (dxh - xh * jnp.mean(dxh * xh, axis=-1, keepdims=True))
    return dx, jnp.sum(dy * xh, axis=0, keepdims=True)


def _sigmoid(x):
    return 1.0 / (1.0 + jnp.exp(-x))


_GELU_C = math.sqrt(2.0 / math.pi)


def _gelu(x):
    return 0.5 * x * (1.0 + jnp.tanh(_GELU_C * (x + 0.044715 * x * x * x)))


def _gelu_grad(x):
    t = jnp.tanh(_GELU_C * (x + 0.044715 * x * x * x))
    return 0.5 * (1.0 + t) + 0.5 * x * (1.0 - t * t) * _GELU_C * (1.0 + 3.0 * 0.044715 * x * x)


ATTN_INTERLEAVE = 4
KEY_PAD = BLK * max(DILATIONS)


def _key_mask(n):
    ii = lax.broadcasted_iota(jnp.int32, (BLK, 2 * BLK), 0)
    jj = lax.broadcasted_iota(jnp.int32, (BLK, 2 * BLK), 1)
    return ((jj < BLK) & (jj >= ii) & (n > 0)) | ((jj >= BLK) & (jj - BLK <= ii))


def _units(d, nblk):
    nb = nblk // d
    if nb == 2:
        def unit(idx):
            ii = lax.broadcasted_iota(jnp.int32, (2 * BLK, 2 * BLK), 0)
            jj = lax.broadcasted_iota(jnp.int32, (2 * BLK, 2 * BLK), 1)
            return pl.ds(idx, 2 * BLK, stride=d), pl.ds(KEY_PAD + idx, 2 * BLK, stride=d), (jj <= ii) & (ii - jj <= BLK)
        return d, max(1, ATTN_INTERLEAVE // 2), unit

    def unit(idx):
        r, n = idx // nb, idx % nb
        cur = r + n * (BLK * d)
        keys = cur + (KEY_PAD - BLK * d)
        if d == 1:
            return pl.ds(pl.multiple_of(cur, BLK), BLK), pl.ds(pl.multiple_of(keys, BLK), 2 * BLK), _key_mask(n)
        return pl.ds(cur, BLK, stride=d), pl.ds(keys, 2 * BLK, stride=d), _key_mask(n)
    return nblk, ATTN_INTERLEAVE, unit


def _pad_keys(dst, src):
    dst[pl.ds(0, KEY_PAD), :] = jnp.zeros((KEY_PAD, dst.shape[1]), F32)

    def copy(c, carry):
        dst[pl.ds(pl.multiple_of(KEY_PAD + c * BLK, BLK), BLK), :] = src[pl.ds(pl.multiple_of(c * BLK, BLK), BLK), :]
        return carry

    lax.fori_loop(0, src.shape[0] // BLK, copy, 0)


def attn_fwd(proj, n_heads, *, name):
    S, WP = proj.shape
    assert S % (BLK * max(DILATIONS)) == 0
    nblk = S // BLK
    AW = n_heads * HEAD_DIM
    scale = 1.0 / math.sqrt(HEAD_DIM)

    def body(q_ref, k_ref, v_ref, o_ref, l_ref, acc, mrun, lrun, kp, vp):
        _pad_keys(kp, k_ref)
        _pad_keys(vp, v_ref)
        for first, d in zip((True, False, False), reversed(DILATIONS)):
            n_units, per_step, unit = _units(d, nblk)

            def step(it, carry, first=first, n_units=n_units, per_step=per_step, unit=unit):
                units = [unit(it + j * (n_units // per_step)) for j in range(per_step)]
                ss = [lax.dot_general(q_ref[cur, :].astype(BF16), kp[keys, :].astype(BF16), _NT,
                                      preferred_element_type=F32) * scale for cur, keys, _ in units]
                for j, (cur, keys, mask) in enumerate(units):
                    s = jnp.where(mask, ss[j], NEG_INF)
                    m = jnp.max(s, axis=-1, keepdims=True)
                    p = jnp.exp(s - m)
                    l = jnp.sum(p, axis=-1, keepdims=True)
                    o = jnp.dot(p.astype(BF16), vp[keys, :].astype(BF16), preferred_element_type=F32)
                    m = jnp.broadcast_to(m, o.shape)
                    l = jnp.broadcast_to(l, o.shape)
                    if first:
                        acc[cur, :], mrun[cur, :], lrun[cur, :] = o, m, l
                    else:
                        m_old = mrun[cur, :]
                        m_new = jnp.maximum(m_old, m)
                        w_old, w_blk = jnp.exp(m_old - m_new), jnp.exp(m - m_new)
                        acc[cur, :] = w_old * acc[cur, :] + w_blk * o
                        lrun[cur, :] = w_old * lrun[cur, :] + w_blk * l
                        mrun[cur, :] = m_new
                return carry

            lax.fori_loop(0, n_units // per_step, step, 0)

        def finish(c, carry):
            r = pl.ds(pl.multiple_of(c * BLK, BLK), BLK)
            o_ref[r, :] = acc[r, :] / lrun[r, :]
            l_ref[r, :] = mrun[r, :] + jnp.log(lrun[r, :])
            return carry

        lax.fori_loop(0, nblk, finish, 0)

    def col(off):
        return pl.BlockSpec((S, HEAD_DIM), lambda h: (0, off + h))

    ospec = pl.BlockSpec((S, HEAD_DIM), lambda h: (0, h))
    return pl.pallas_call(
        body, out_shape=[jax.ShapeDtypeStruct((S, AW), F32)] * 2, grid=(n_heads,),
        in_specs=[col(0), col(n_heads), col(2 * n_heads)], out_specs=[ospec, ospec],
        scratch_shapes=[pltpu.VMEM((S, HEAD_DIM), F32)] * 3 + [pltpu.VMEM((KEY_PAD + S, HEAD_DIM), F32)] * 2,
        compiler_params=_params(("parallel",)), name=name)(proj, proj, proj)


def attn_bwd(proj, do, lse, delta, n_heads, *, name):
    S, WP = proj.shape
    nblk = S // BLK
    AW = n_heads * HEAD_DIM
    scale = 1.0 / math.sqrt(HEAD_DIM)

    def body(q_ref, k_ref, v_ref, do_ref, l_ref, dl_ref, dq_ref, dk_ref, dv_ref, dq_sc, dk_sc, dv_sc, kp, vp):
        _pad_keys(kp, k_ref)
        _pad_keys(vp, v_ref)
        dq_sc[...] = jnp.zeros_like(dq_sc)
        dk_sc[...] = jnp.zeros_like(dk_sc)
        dv_sc[...] = jnp.zeros_like(dv_sc)
        for d in DILATIONS:
            n_units, per_step, unit = _units(d, nblk)

            def step(it, carry, n_units=n_units, per_step=per_step, unit=unit):
                for cur, keys, mask in [unit(it + j * (n_units // per_step)) for j in range(per_step)]:
                    q = q_ref[cur, :].astype(BF16)
                    g = do_ref[cur, :].astype(BF16)
                    kb = kp[keys, :].astype(BF16)
                    vb = vp[keys, :].astype(BF16)
                    s = lax.dot_general(q, kb, _NT, preferred_element_type=F32) * scale
                    p = jnp.where(mask, jnp.exp(s - l_ref[cur, :][:, :1]), 0.0)
                    dp = lax.dot_general(g, vb, _NT, preferred_element_type=F32)
                    ds = (p * (dp - dl_ref[cur, :][:, :1]) * scale).astype(BF16)
                    dq_sc[cur, :] += jnp.dot(ds, kb, preferred_element_type=F32)
                    dk_sc[keys, :] += lax.dot_general(ds, q, _TN, preferred_element_type=F32)
                    dv_sc[keys, :] += lax.dot_general(p.astype(BF16), g, _TN, preferred_element_type=F32)
                return carry

            lax.fori_loop(0, n_units // per_step, step, 0)
        rows = pl.ds(KEY_PAD, S)
        dq_ref[...] = dq_sc[...].astype(BF16)
        dk_ref[...] = dk_sc[rows, :].astype(BF16)
        dv_ref[...] = dv_sc[rows, :].astype(BF16)

    def col(off):
        return pl.BlockSpec((S, HEAD_DIM), lambda h: (0, off + h))

    ospec = pl.BlockSpec((S, HEAD_DIM), lambda h: (0, h))
    return pl.pallas_call(
        body, out_shape=[jax.ShapeDtypeStruct((S, AW), BF16)] * 3, grid=(n_heads,),
        in_specs=[col(0), col(n_heads), col(2 * n_heads), ospec, ospec, ospec], out_specs=[ospec] * 3,
        scratch_shapes=[pltpu.VMEM((S, HEAD_DIM), F32)] + [pltpu.VMEM((KEY_PAD + S, HEAD_DIM), F32)] * 4,
        compiler_params=_params(("parallel",), VMEM_LIMIT_SCAN), name=name)(proj, proj, proj, do, lse, delta)


def _to_segments(t):
    S, W = t.shape
    return t.reshape(SEGMENTS, S // SEGMENTS, W).swapaxes(0, 1).reshape(S, W)


def _from_segments(t):
    S, W = t.shape
    return t.reshape(S // SEGMENTS, SEGMENTS, W).swapaxes(0, 1).reshape(S, W)


def _cmul(ar, ai, br, bi):
    return ar * br - ai * bi, ar * bi + ai * br


def _power(ar, ai, log2n):
    for _ in range(log2n):
        ar, ai = _cmul(ar, ai, ar, ai)
    return ar, ai


def _shift_rows(x, up):
    row = lax.broadcasted_iota(jnp.int32, x.shape, 0)
    if up:
        return jnp.where(row == SEGMENTS - 1, 0.0, pltpu.roll(x, SEGMENTS - 1, 0))
    return jnp.where(row == 0, 0.0, pltpu.roll(x, 1, 0))


def _segment_carries(er, ei, pr, pi, up):
    cr = jnp.zeros_like(er)
    ci = jnp.zeros_like(ei)
    for _ in range(SEGMENTS - 1):
        tr, ti = _cmul(pr, pi, cr, ci)
        cr, ci = _shift_rows(er + tr, up), _shift_rows(ei + ti, up)
    return cr, ci


def _scan_states(sr, si, ar, ai, T, reverse):
    ns = sr.shape[1]
    ar8 = jnp.broadcast_to(ar, (SEGMENTS, ns))
    ai8 = jnp.broadcast_to(ai, (SEGMENTS, ns))

    def rows(t):
        k = (T - 1 - t) if reverse else t
        return pl.ds(pl.multiple_of(k * SEGMENTS, SEGMENTS), SEGMENTS)

    def advance(t, c):
        tr, ti = _cmul(ar8, ai8, c[0], c[1])
        return tr + sr[rows(t), :], ti + si[rows(t), :]

    def several(step):
        def trip(t, c):
            for j in range(SCAN_UNROLL):
                c = step(t * SCAN_UNROLL + j, c)
            return c
        return trip

    zero = jnp.zeros((SEGMENTS, ns), F32)
    er, ei = lax.fori_loop(0, T // SCAN_UNROLL, several(advance), (zero, zero))
    pr, pi = _power(ar, ai, T.bit_length() - 1)
    cr, ci = _segment_carries(er, ei, jnp.broadcast_to(pr, (SEGMENTS, ns)), jnp.broadcast_to(pi, (SEGMENTS, ns)), reverse)

    def store(t, c):
        nr, ni = advance(t, c)
        sr[rows(t), :] = nr
        si[rows(t), :] = ni
        return nr, ni

    lax.fori_loop(0, T // SCAN_UNROLL, several(store), (cr, ci))
    return cr, ci


def _slab_specs(ns):
    return [pl.BlockSpec((None, LANES, ns), lambda g: (g, 0, 0)),
            pl.BlockSpec((None, LANES, ns), lambda g: (g, 0, 0)),
            pl.BlockSpec((None, 1, ns), lambda g: (g, 0, 0)),
            pl.BlockSpec((None, 1, ns), lambda g: (g, 0, 0)),
            pl.BlockSpec((None, ns, LANES), lambda g: (g, 0, 0)),
            pl.BlockSpec((None, ns, LANES), lambda g: (g, 0, 0)),
            pl.BlockSpec((1, LANES), lambda g: (0, g))]


def _chunks(S):
    rc = _tile(S, 512, 16)
    return rc, S // rc


def ssm_fwd(u, bbr, bbi, ar, ai, cbr, cbi, dsk, *, name):
    S, SW = u.shape
    nslab, _, ns = bbr.shape
    T = S // SEGMENTS
    assert T & (T - 1) == 0
    rc, nc = _chunks(S)

    def body(u_ref, br_ref, bi_ref, ar_ref, ai_ref, cr_ref, ci_ref, d_ref, y_ref, sr, si):
        def inputs(c, carry):
            r = pl.ds(pl.multiple_of(c * rc, rc), rc)
            sr[r, :] = jnp.dot(u_ref[r, :], br_ref[...], preferred_element_type=F32)
            si[r, :] = jnp.dot(u_ref[r, :], bi_ref[...], preferred_element_type=F32)
            return carry

        lax.fori_loop(0, nc, inputs, 0)
        _scan_states(sr, si, ar_ref[...], ai_ref[...], T, False)

        def outputs(c, carry):
            r = pl.ds(pl.multiple_of(c * rc, rc), rc)
            y_ref[r, :] = (jnp.dot(sr[r, :].astype(BF16), cr_ref[...], preferred_element_type=F32)
                           - jnp.dot(si[r, :].astype(BF16), ci_ref[...], preferred_element_type=F32)
                           + d_ref[...] * u_ref[r, :].astype(F32))
            return carry

        lax.fori_loop(0, nc, outputs, 0)

    slab = pl.BlockSpec((S, LANES), lambda g: (0, g))
    return pl.pallas_call(
        body, out_shape=jax.ShapeDtypeStruct((S, SW), F32), grid=(nslab,),
        in_specs=[slab] + _slab_specs(ns), out_specs=slab,
        scratch_shapes=[pltpu.VMEM((S, ns), F32)] * 2,
        compiler_params=_params(("parallel",), VMEM_LIMIT_SCAN), name=name)(u, bbr, bbi, ar, ai, cbr, cbi, dsk)


def ssm_bwd(u, dy, bbr, bbi, ar, ai, cbr, cbi, dsk, *, name):
    S, SW = u.shape
    nslab, _, ns = bbr.shape
    T = S // SEGMENTS
    rc, nc = _chunks(S)

    def body(u_ref, dy_ref, br_ref, bi_ref, ar_ref, ai_ref, cr_ref, ci_ref, d_ref,
             du_ref, dbr_ref, dbi_ref, dcr_ref, dci_ref, dar_ref, dai_ref, sr, si, lr, li):
        def inputs(c, carry):
            r = pl.ds(pl.multiple_of(c * rc, rc), rc)
            ub = u_ref[r, :]
            gb = dy_ref[r, :].astype(BF16)
            sr[r, :] = jnp.dot(ub, br_ref[...], preferred_element_type=F32)
            si[r, :] = jnp.dot(ub, bi_ref[...], preferred_element_type=F32)
            lr[r, :] = lax.dot_general(gb, cr_ref[...], _NT, preferred_element_type=F32)
            li[r, :] = -lax.dot_general(gb, ci_ref[...], _NT, preferred_element_type=F32)
            return carry

        lax.fori_loop(0, nc, inputs, 0)
        ar, ai = ar_ref[...], ai_ref[...]
        s0r, s0i = _scan_states(sr, si, ar, ai, T, False)
        _scan_states(lr, li, ar, -ai, T, True)

        def pair(k, c):
            aligned = (lambda v: v) if isinstance(k, int) else (lambda v: pl.multiple_of(v, SEGMENTS))
            now = pl.ds(aligned(k * SEGMENTS), SEGMENTS)
            prev = pl.ds(aligned((k - 1) * SEGMENTS), SEGMENTS)
            return (c[0] + lr[now, :] * sr[prev, :] + li[now, :] * si[prev, :],
                    c[1] - lr[now, :] * si[prev, :] + li[now, :] * sr[prev, :])

        first = pl.ds(0, SEGMENTS)
        acc = (lr[first, :] * s0r + li[first, :] * s0i, -lr[first, :] * s0i + li[first, :] * s0r)

        def pairs(t, c):
            for j in range(SCAN_UNROLL):
                c = pair(1 + t * SCAN_UNROLL + j, c)
            return c

        whole = (T - 1) // SCAN_UNROLL
        acc = lax.fori_loop(0, whole, pairs, acc)
        for k in range(1 + whole * SCAN_UNROLL, T):
            acc = pair(k, acc)
        dar_ref[...] = jnp.sum(acc[0], axis=0, keepdims=True)
        dai_ref[...] = jnp.sum(acc[1], axis=0, keepdims=True)

        dbr_ref[...] = jnp.zeros_like(dbr_ref)
        dbi_ref[...] = jnp.zeros_like(dbi_ref)
        dcr_ref[...] = jnp.zeros_like(dcr_ref)
        dci_ref[...] = jnp.zeros_like(dci_ref)

        def outputs(c, carry):
            r = pl.ds(pl.multiple_of(c * rc, rc), rc)
            ub = u_ref[r, :]
            g = dy_ref[r, :]
            gb = g.astype(BF16)
            lrb = lr[r, :].astype(BF16)
            lib = li[r, :].astype(BF16)
            du_ref[r, :] = (lax.dot_general(lrb, br_ref[...], _NT, preferred_element_type=F32)
                            + lax.dot_general(lib, bi_ref[...], _NT, preferred_element_type=F32)
                            + d_ref[...] * g).astype(BF16)
            dbr_ref[...] += lax.dot_general(ub, lrb, _TN, preferred_element_type=F32)
            dbi_ref[...] += lax.dot_general(ub, lib, _TN, preferred_element_type=F32)
            dcr_ref[...] += lax.dot_general(sr[r, :].astype(BF16), gb, _TN, preferred_element_type=F32)
            dci_ref[...] -= lax.dot_general(si[r, :].astype(BF16), gb, _TN, preferred_element_type=F32)
            return carry

        lax.fori_loop(0, nc, outputs, 0)

    slab = pl.BlockSpec((S, LANES), lambda g: (0, g))
    bspec = pl.BlockSpec((None, LANES, ns), lambda g: (g, 0, 0))
    cspec = pl.BlockSpec((None, ns, LANES), lambda g: (g, 0, 0))
    aspec = pl.BlockSpec((None, 1, ns), lambda g: (g, 0, 0))
    return pl.pallas_call(
        body,
        out_shape=[jax.ShapeDtypeStruct((S, SW), BF16),
                   jax.ShapeDtypeStruct((nslab, LANES, ns), F32), jax.ShapeDtypeStruct((nslab, LANES, ns), F32),
                   jax.ShapeDtypeStruct((nslab, ns, LANES), F32), jax.ShapeDtypeStruct((nslab, ns, LANES), F32),
                   jax.ShapeDtypeStruct((nslab, 1, ns), F32), jax.ShapeDtypeStruct((nslab, 1, ns), F32)],
        grid=(nslab,), in_specs=[slab, slab] + _slab_specs(ns),
        out_specs=[slab, bspec, bspec, cspec, cspec, aspec, aspec],
        scratch_shapes=[pltpu.VMEM((S, ns), F32)] * 4,
        compiler_params=_params(("parallel",), VMEM_LIMIT_SCAN), name=name)(u, dy, bbr, bbi, ar, ai, cbr, cbi, dsk)


def _discretise(lam_re, lam_im, log_dt, b_re, b_im):
    dt = jnp.exp(log_dt)[:, None]
    mag = jnp.exp(lam_re * dt)
    ar = mag * jnp.cos(lam_im * dt)
    ai = mag * jnp.sin(lam_im * dt)
    nr, ni = ar - 1.0, ai
    den = lam_re * lam_re + lam_im * lam_im
    cr = ((nr * lam_re + ni * lam_im) / den)[..., None]
    ci = ((ni * lam_re - nr * lam_im) / den)[..., None]
    return ar, ai, cr * b_re - ci * b_im, cr * b_im + ci * b_re


def _block_diag(t, nslab):
    G, R, C = t.shape
    eye = jnp.eye(SLAB_GROUPS, dtype=t.dtype)
    t = t.reshape(nslab, SLAB_GROUPS, R, C)
    return jnp.einsum('sgrc,gh->sgrhc', t, eye).reshape(nslab, SLAB_GROUPS * R, SLAB_GROUPS * C)


def _block_diag_part(t, R, C):
    nslab = t.shape[0]
    eye = jnp.eye(SLAB_GROUPS, dtype=t.dtype)
    t = t.reshape(nslab, SLAB_GROUPS, R, SLAB_GROUPS, C)
    return jnp.einsum('sgrhc,gh->sgrc', t, eye).reshape(nslab * SLAB_GROUPS, R, C)


def _place():
    return lax.axis_index("x"), lax.axis_index("y"), lax.axis_index("c")


def all_gather(shards, *, name):
    nw = len(shards)

    def body(*refs):
        ins, outs = refs[:nw], refs[nw:2 * nw]
        send_sems, recv_sems, local_sems = refs[2 * nw:]
        x, y, c = _place()
        me, sibling = (x, y, c), (x, y, 1 - c)
        chips = [(1 - x, y), (x, 1 - y), (1 - x, 1 - y)]

        def copy(w, k, block, to, own):
            px, py, pc = block
            slot = outs[w].at[4 * px + 2 * py + pc]
            return pltpu.make_async_remote_copy(
                src_ref=ins[w] if own else slot, dst_ref=slot, send_sem=send_sems.at[w, k],
                recv_sem=recv_sems.at[w, k], device_id=to, device_id_type=MESH)

        mine = [pltpu.make_async_copy(ins[w], outs[w].at[4 * x + 2 * y + c], local_sems.at[w]) for w in range(nw)]
        for cp in mine:
            cp.start()
        first = []
        for w in range(nw):
            first.append(copy(w, 0, me, sibling, True))
            first += [copy(w, 1 + j, me, (*chip, c), True) for j, chip in enumerate(chips)]
        for cp in first:
            cp.start()
        passed = []
        for j, chip in enumerate(chips):
            for w in range(nw):
                copy(w, 1 + j, (*chip, c), me, False).wait_recv()
                cp = copy(w, 4 + j, (*chip, c), sibling, False)
                cp.start()
                passed.append(cp)
        for w in range(nw):
            copy(w, 0, sibling, me, False).wait_recv()
            for j, chip in enumerate(chips):
                copy(w, 4 + j, (*chip, 1 - c), me, False).wait_recv()
        for cp in first + passed:
            cp.wait_send()
        for cp in mine:
            cp.wait()

    anyspec = pl.BlockSpec(memory_space=pl.ANY)
    return pl.pallas_call(
        body, out_shape=[jax.ShapeDtypeStruct((N_DEV,) + s.shape, s.dtype) for s in shards],
        in_specs=[anyspec] * nw, out_specs=[anyspec] * nw,
        scratch_shapes=[pltpu.SemaphoreType.DMA((nw, 7)), pltpu.SemaphoreType.DMA((nw, 7)),
                        pltpu.SemaphoreType.DMA((nw,))],
        compiler_params=pltpu.CompilerParams(has_side_effects=True), name=name)(*shards)


_HBM = pl.BlockSpec(memory_space=pltpu.HBM)
_SEM = pl.BlockSpec(memory_space=pltpu.SEMAPHORE)
_ORDERED_EFFECT = pltpu.SideEffectType.DATAFLOW_SIDE_EFFECTING


def _split_call(name, srcs, zones, sems_in, n_new, body_fn, after):
    nsrc, nz, ns, nn = len(srcs), len(zones), len(sems_in), len(n_new)
    nb = nsrc + nz

    def body(*refs):
        outs = refs[nb + ns + 1:]
        body_fn(refs[:nb], refs[nb:nb + ns], outs[:nn])
        outs[nn + nz][...] = jnp.zeros((SUBLANES, LANES), F32)

    res = pl.pallas_call(
        body, name=name,
        out_shape=([pltpu.SemaphoreType.DMA((n,)) for n in n_new] + [pltpu.HBM(b.shape, b.dtype) for b in zones]
                   + [jax.ShapeDtypeStruct((SUBLANES, LANES), F32)]),
        in_specs=[_HBM] * nb + [_SEM] * ns + [_ANY],
        out_specs=[_SEM] * nn + [_HBM] * nz + [pl.BlockSpec(memory_space=pltpu.VMEM)],
        input_output_aliases={nsrc + i: nn + i for i in range(nz)},
        compiler_params=pltpu.CompilerParams(has_side_effects=_ORDERED_EFFECT))(
            *[pltpu.with_memory_space_constraint(b, pltpu.HBM) for b in list(srcs) + list(zones)], *sems_in, after)
    return list(res[:nn]), list(res[nn:nn + nz]), res[-1]


def _mesh_peers():
    x, y, c = _place()
    return x, y, c, (x, y, 1 - c), [(1 - x, y), (x, 1 - y), (1 - x, 1 - y)]


def gather_start(shards, after, *, name):
    nw = len(shards)
    x, y, c = _place()
    zones = [lax.dynamic_update_slice(lax.empty((N_DEV,) + s.shape, s.dtype), s[None], (4 * x + 2 * y + c, 0, 0))
             for s in shards]

    def body(bufs, taken, new):
        for cp in _gather_first(bufs, nw, new[0], new[1]):
            cp.start()

    sems, zones, token = _split_call(name, shards, zones, [], [4 * nw, 4 * nw], body, after)
    return shards, sems, zones, token


def _gather_first(bufs, nw, send, recv):
    x, y, c, sibling, chips = _mesh_peers()
    out = []
    for w in range(nw):
        slot = bufs[nw + w].at[4 * x + 2 * y + c]
        for k, to in enumerate([sibling] + [(*ch, c) for ch in chips]):
            out.append(pltpu.make_async_remote_copy(
                src_ref=bufs[w], dst_ref=slot, send_sem=send.at[4 * w + k], recv_sem=recv.at[4 * w + k],
                device_id=to, device_id_type=MESH))
    return out


def _gather_slot_copy(bufs, nw, w, block, send_sem, recv_sem, to):
    px, py, pc = block
    slot = bufs[nw + w].at[4 * px + 2 * py + pc]
    return pltpu.make_async_remote_copy(src_ref=slot, dst_ref=slot, send_sem=send_sem, recv_sem=recv_sem,
                                        device_id=to, device_id_type=MESH)


def gather_forward(state, after, *, name):
    shards, sems, zones, _ = state
    nw = len(shards)

    def body(bufs, taken, new):
        x, y, c, sibling, chips = _mesh_peers()
        for j, ch in enumerate(chips):
            for w in range(nw):
                k = 4 * w + 1 + j
                _gather_slot_copy(bufs, nw, w, (*ch, c), taken[0].at[k], taken[1].at[k], (*ch, c)).wait_recv()
                _gather_slot_copy(bufs, nw, w, (*ch, c), new[0].at[3 * w + j], new[1].at[3 * w + j], sibling).start()
        for w in range(nw):
            _gather_slot_copy(bufs, nw, w, sibling, taken[0].at[4 * w], taken[1].at[4 * w], sibling).wait_recv()
        for cp in _gather_first(bufs, nw, taken[0], taken[1]):
            cp.wait_send()

    sems, zones, token = _split_call(name, shards, zones, sems, [3 * nw, 3 * nw], body, after)
    return shards, sems, zones, token


def gather_finish(state, after, *, name):
    shards, sems, zones, _ = state
    nw = len(shards)

    def body(bufs, taken, new):
        x, y, c, sibling, chips = _mesh_peers()
        for w in range(nw):
            for j, ch in enumerate(chips):
                cp = _gather_slot_copy(bufs, nw, w, (*ch, 1 - c), taken[0].at[3 * w + j], taken[1].at[3 * w + j], sibling)
                cp.wait_send()
                cp.wait_recv()

    _, zones, _ = _split_call(name, shards, zones, sems, [], body, after)
    return zones


def exchange_start(srcs, zone_shapes, copies, n, after, *, name):
    nw = len(srcs)
    zones = [lax.empty(z, s.dtype) for z, s in zip(zone_shapes, srcs)]

    def body(bufs, taken, new):
        for cp in copies(bufs[:nw], bufs[nw:], new[0], new[1]):
            cp.start()

    sems, zones, token = _split_call(name, srcs, zones, [], [n, n], body, after)
    return srcs, copies, sems, zones, token


def exchange_wait(state, after, *, name):
    srcs, copies, sems, zones, _ = state
    nw = len(srcs)

    def body(bufs, taken, new):
        for cp in copies(bufs[:nw], bufs[nw:], taken[0], taken[1]):
            cp.wait_send()
            cp.wait_recv()

    _, zones, _ = _split_call(name, srcs, zones, sems, [], body, after)
    return zones


def _core_copies(srcs, zones, send, recv):
    x, y, c = _place()
    return [pltpu.make_async_remote_copy(
        src_ref=srcs[w].at[:, 1 - c], dst_ref=zones[w], send_sem=send.at[w], recv_sem=recv.at[w],
        device_id=(x, y, 1 - c), device_id_type=MESH) for w in range(len(srcs))]


def _chip_copies(srcs, zones, send, recv):
    x, y, c = _place()
    chips = [(1 - x, y), (x, 1 - y), (1 - x, 1 - y)]
    return [pltpu.make_async_remote_copy(
        src_ref=srcs[w].at[2 * cx + cy], dst_ref=zones[w].at[j], send_sem=send.at[3 * w + j],
        recv_sem=recv.at[3 * w + j], device_id=(cx, cy, c), device_id_type=MESH)
        for w in range(len(srcs)) for j, (cx, cy) in enumerate(chips)]


def _blocked(fn, ins, outs, *, name, place=None, tr=256):
    k, n = outs[0][0]
    tr = _tile(k, tr, 16)
    if place is None:
        place = jnp.zeros((1,), jnp.int32)
    specs = []
    args = []
    for a in ins:
        if isinstance(a, tuple):
            arr, lead = a
            specs.append(pl.BlockSpec((None, tr, n), functools.partial(lambda i, s, lead: (*lead(i, s), 0), lead=lead)))
            args.append(arr)
        else:
            specs.append(pl.BlockSpec((tr, n), lambda i, s: (i, 0)))
            args.append(a)
    nin = len(args)

    def body(place_ref, *refs):
        res = fn(*[r[...] for r in refs[:nin]])
        for ref, val in zip(refs[nin:], res):
            ref[...] = val.astype(ref.dtype)

    return pl.pallas_call(
        body, out_shape=[jax.ShapeDtypeStruct(s, d) for s, d in outs],
        grid_spec=pltpu.PrefetchScalarGridSpec(
            num_scalar_prefetch=1, grid=(k // tr,), in_specs=specs,
            out_specs=[pl.BlockSpec((tr, n), lambda i, s: (i, 0)) for _ in outs]),
        compiler_params=_params(("parallel",)), name=name)(place, *args)


def _adamw(w, g, m, v):
    m = ADAM_B1 * m + (1.0 - ADAM_B1) * g
    v = ADAM_B2 * v + (1.0 - ADAM_B2) * (g * g)
    m_hat = m / (1.0 - ADAM_B1 ** ADAM_STEP)
    v_hat = v / (1.0 - ADAM_B2 ** ADAM_STEP)
    delta = -ADAM_LR * (m_hat * pl.reciprocal(jnp.sqrt(v_hat) + ADAM_EPS, approx=True) + ADAM_WD * w)
    return delta, m, v


def kernel(x, p, mix_norm_pre, w_in, lam_re, lam_im, log_dt, ssm_b_re, ssm_b_im, ssm_c_re, ssm_c_im, ssm_d, w_glu, b_glu, attn_out_norm, ssm_out_norm, w_out, mix_norm_post, mlp_norm_pre, w_up, w_down, mlp_norm_post, ple_norm_pre, w_ple_gate, w_ple_proj, ple_norm_post, loss_target, m_mix_norm_pre, m_w_in, m_lam_re, m_lam_im, m_log_dt, m_ssm_b_re, m_ssm_b_im, m_ssm_c_re, m_ssm_c_im, m_ssm_d, m_w_glu, m_b_glu, m_attn_out_norm, m_ssm_out_norm, m_w_out, m_mix_norm_post, m_mlp_norm_pre, m_w_up, m_w_down, m_mlp_norm_post, m_ple_norm_pre, m_w_ple_gate, m_w_ple_proj, m_ple_norm_post, v_mix_norm_pre, v_w_in, v_lam_re, v_lam_im, v_log_dt, v_ssm_b_re, v_ssm_b_im, v_ssm_c_re, v_ssm_c_im, v_ssm_d, v_w_glu, v_b_glu, v_attn_out_norm, v_ssm_out_norm, v_w_out, v_mix_norm_post, v_mlp_norm_pre, v_w_up, v_w_down, v_mlp_norm_post, v_ple_norm_pre, v_w_ple_gate, v_w_ple_proj, v_ple_norm_post):
    weights = dict(mix_norm_pre=mix_norm_pre, w_in=w_in, lam_re=lam_re, lam_im=lam_im, log_dt=log_dt, ssm_b_re=ssm_b_re, ssm_b_im=ssm_b_im, ssm_c_re=ssm_c_re, ssm_c_im=ssm_c_im, ssm_d=ssm_d, w_glu=w_glu, b_glu=b_glu, attn_out_norm=attn_out_norm, ssm_out_norm=ssm_out_norm, w_out=w_out, mix_norm_post=mix_norm_post, mlp_norm_pre=mlp_norm_pre, w_up=w_up, w_down=w_down, mlp_norm_post=mlp_norm_post, ple_norm_pre=ple_norm_pre, w_ple_gate=w_ple_gate, w_ple_proj=w_ple_proj, ple_norm_post=ple_norm_post)
    mom_m = dict(mix_norm_pre=m_mix_norm_pre, w_in=m_w_in, lam_re=m_lam_re, lam_im=m_lam_im, log_dt=m_log_dt, ssm_b_re=m_ssm_b_re, ssm_b_im=m_ssm_b_im, ssm_c_re=m_ssm_c_re, ssm_c_im=m_ssm_c_im, ssm_d=m_ssm_d, w_glu=m_w_glu, b_glu=m_b_glu, attn_out_norm=m_attn_out_norm, ssm_out_norm=m_ssm_out_norm, w_out=m_w_out, mix_norm_post=m_mix_norm_post, mlp_norm_pre=m_mlp_norm_pre, w_up=m_w_up, w_down=m_w_down, mlp_norm_post=m_mlp_norm_post, ple_norm_pre=m_ple_norm_pre, w_ple_gate=m_w_ple_gate, w_ple_proj=m_w_ple_proj, ple_norm_post=m_ple_norm_post)
    mom_v = dict(mix_norm_pre=v_mix_norm_pre, w_in=v_w_in, lam_re=v_lam_re, lam_im=v_lam_im, log_dt=v_log_dt, ssm_b_re=v_ssm_b_re, ssm_b_im=v_ssm_b_im, ssm_c_re=v_ssm_c_re, ssm_c_im=v_ssm_c_im, ssm_d=v_ssm_d, w_glu=v_w_glu, b_glu=v_b_glu, attn_out_norm=v_attn_out_norm, ssm_out_norm=v_ssm_out_norm, w_out=v_w_out, mix_norm_post=v_mix_norm_post, mlp_norm_pre=v_mlp_norm_pre, w_up=v_w_up, w_down=v_w_down, mlp_norm_post=v_mlp_norm_post, ple_norm_pre=v_ple_norm_pre, w_ple_gate=v_w_ple_gate, w_ple_proj=v_w_ple_proj, ple_norm_post=v_ple_norm_post)
    order = list(weights)
    big = ["w_in", "w_glu", "w_out", "w_up", "w_down", "w_ple_gate", "w_ple_proj"]
    col_sharded = {"w_in", "w_up", "w_ple_proj"}
    small = [n for n in order if n not in big]

    _, S, D = x.shape
    xs = x[0]
    tgt = loss_target[0]
    AW = attn_out_norm.shape[1]
    SW = ssm_d.shape[1]
    H = AW // HEAD_DIM
    G = SW // SSM_GROUP
    nslab = G // SLAB_GROUPS
    P_, C_ = SSM_STATE, SSM_GROUP

    shard = {n: weights[n][0].astype(BF16) for n in big}
    W, WT = {}, {}

    def arrived(names, gathered):
        for n, g in zip(names, gathered):
            W[n] = g if n in col_sharded else g.reshape(1, N_DEV * g.shape[1], g.shape[2])

    def transposed(g):
        return jnp.swapaxes(g, 1, 2).reshape(1, g.shape[0] * g.shape[2], g.shape[1])

    arrived(["w_in"], all_gather([shard["w_in"]], name="gather_w_in"))
    WT["w_in"] = transposed(W["w_in"])
    early, mid, late = ["w_glu", "w_out"], ["w_up"], ["w_down", "w_ple_gate", "w_ple_proj"]
    gather_early = gather_start([shard[n] for n in early], W["w_in"], name="gather_early_start")
    gather_mid = gather_start([shard[n] for n in mid], gather_early[-1], name="gather_mid_start")
    gather_late = gather_start([shard[n] for n in late], gather_mid[-1], name="gather_late_start")

    g1, g2, g3, g4, g5, g6 = (weights[n] for n in ("mix_norm_pre", "mix_norm_post", "mlp_norm_pre",
                                                      "mlp_norm_post", "ple_norm_pre", "ple_norm_post"))
    ga, gs = attn_out_norm, ssm_out_norm
    (hn1,) = rowwise(lambda a, g: (_rms(a, g),), [xs], [g1], [(D, BF16)], deps=(gather_late[-1],), name="norm_in")
    (proj,) = mm_nn(hn1, W["w_in"], [F32], name="proj_in")
    attn, lse = attn_fwd(proj, H, name="attn_fwd")
    gather_early = gather_forward(gather_early, attn, name="gather_early_forward")
    (mix_a,) = rowwise(lambda a, g: (_rms(a, g),), [attn], [ga], [(AW, BF16)], deps=(gather_early[-1],),
                       name="attn_norm")
    arrived(early, gather_finish(gather_early, mix_a, name="gather_early_finish"))

    a_r, a_i, bb_r, bb_i = _discretise(lam_re[0], lam_im[0], log_dt[0], ssm_b_re[0], ssm_b_im[0])
    ssm_consts = (_block_diag(bb_r.swapaxes(1, 2), nslab).astype(BF16), _block_diag(bb_i.swapaxes(1, 2), nslab).astype(BF16),
                  a_r.reshape(nslab, 1, SLAB_STATES), a_i.reshape(nslab, 1, SLAB_STATES),
                  _block_diag(ssm_c_re[0].swapaxes(1, 2), nslab).astype(BF16),
                  _block_diag(ssm_c_im[0].swapaxes(1, 2), nslab).astype(BF16), ssm_d)
    u_seg = _to_segments(proj[:, 3 * AW:]).astype(BF16)
    y_pre = ssm_fwd(u_seg, *ssm_consts, name="ssm_fwd")
    gather_mid = gather_forward(gather_mid, y_pre, name="gather_mid_forward")
    (yg,) = rowwise(lambda a: (_gelu(a),), [y_pre], [], [(SW, BF16)], deps=(gather_mid[-1],), name="ssm_gelu")
    (gl1,) = mm_nn(yg, W["w_glu"], [BF16], epi=lambda acc, b: (acc + b,), bias=b_glu, name="glu_gate")
    (mix_s,) = rowwise(lambda yp, gl, g: (_rms(_gelu(yp) * _sigmoid(gl), g),), [y_pre, gl1], [gs], [(SW, BF16)],
                       name="ssm_glu_norm")
    mixed = jnp.concatenate([mix_a, _from_segments(mix_s)], axis=1)
    (mo,) = mm_nn(mixed, W["w_out"], [BF16], name="mix_out")

    def resid_norm(h, t, gpost, gpre):
        hh = h + _rms(t, gpost)
        return hh, _rms(hh, gpre)

    h1, hn2 = rowwise(resid_norm, [xs, mo], [g2, g3], [(D, F32), (D, BF16)], name="resid_mix")
    arrived(mid, gather_finish(gather_mid, hn2, name="gather_mid_finish"))
    gather_late = gather_forward(gather_late, W["w_up"], name="gather_late_forward")
    WT["w_up"] = transposed(W["w_up"])

    def relu2(acc):
        r = jnp.maximum(acc, 0.0)
        return acc, r * r

    up, act = mm_nn(hn2, W["w_up"], [BF16, BF16], epi=relu2, deps=(gather_late[-1],), tm=1024, tn=1024, name="mlp_up")
    arrived(late, gather_finish(gather_late, act, name="gather_late_finish"))
    (ff,) = mm_nn(act, W["w_down"], [BF16], name="mlp_down")
    h2, hn3 = rowwise(resid_norm, [h1, ff], [g4, g5], [(D, F32), (D, BF16)], name="resid_mlp")
    (gl2,) = mm_nn(hn3, W["w_ple_gate"], [BF16], name="ple_gate")
    pb = p[0, 0].astype(BF16)
    (emb,) = mm_nn(pb, W["w_ple_proj"], [BF16], name="ple_proj")

    def head(h, gl, e, t, g):
        sg = _sigmoid(gl)
        ge = sg * e
        err = h + _rms(ge, g) - t
        dh = err * (1.0 / D)
        dge, dg = _rms_bwd(dh, ge, g)
        return dh, dge * e * sg * (1.0 - sg), dge * sg, jnp.sum(err * err, axis=0, keepdims=True), dg

    dh3, dgl2, demb, loss_part, dg6 = rowwise(head, [h2, gl2, emb, tgt], [g6], [(D, F32), (D, BF16), (D, BF16)],
                                             [D, D], name="ple_loss_head")
    loss = lax.psum(0.5 / D * jnp.sum(loss_part), ("x", "y", "c"))

    x_i, y_i, c_i = _place()
    place = jnp.stack([c_i, 2 * x_i + y_i]).astype(jnp.int32)
    grads, out_g, out_d, out_m, out_v = {}, {}, {}, {}, {}

    def to_sibling(names, after, tag):
        chunks = []
        for n in names:
            g = grads[n]
            g = g if n in col_sharded else g.reshape(N_DEV, g.shape[1] // N_DEV, g.shape[2])
            chunks.append(g.reshape(4, 2, g.shape[1], g.shape[2]))
        return chunks, exchange_start(chunks, [(4,) + g.shape[2:] for g in chunks], _core_copies, len(chunks), after,
                                      name=f"grads_to_sibling_{tag}")

    def to_chips(names, sent, after, tag):
        chunks, state = sent
        sums = []
        for n, g, r in zip(names, chunks, exchange_wait(state, after, name=f"grads_from_sibling_{tag}")):
            k, nn = g.shape[2], g.shape[3]
            kb = k // _tile(k, 512, 16)

            def mine(i, s, kb=kb):
                return 2 * (i // kb) + s[0], i % kb

            (s,) = _blocked(lambda a, b: (a.astype(F32) + b.astype(F32),),
                            [(g.reshape(N_DEV, k, nn), mine), r.reshape(4 * k, nn)],
                            [((4 * k, nn), BF16)], place=place, tr=k // kb, name=f"chip_sum_{n}")
            sums.append(s.reshape(4, k, nn))
        return sums, exchange_start(sums, [(3,) + s.shape[1:] for s in sums], _chip_copies, 3 * len(sums), sums[-1],
                                    name=f"grads_to_chips_{tag}")

    def update(w_, m_, v_, own, r0, r1, r2):
        g = own.astype(F32) + r0.astype(F32) + r1.astype(F32) + r2.astype(F32)
        return (g,) + _adamw(w_, g, m_, v_)

    def finish(names, sent, after, tag):
        sums, state = sent
        for n, s, r in zip(names, sums, exchange_wait(state, after, name=f"grads_from_chips_{tag}")):
            shp = weights[n].shape
            res = _blocked(update, [weights[n][0], mom_m[n][0], mom_v[n][0], (s, lambda i, p_: (p_[1], i)),
                                    (r, lambda i, p_: (0, i)), (r, lambda i, p_: (1, i)), (r, lambda i, p_: (2, i))],
                           [(shp[1:], F32)] * 4, place=place, name=f"adamw_{n}")
            out_g[n], out_d[n], out_m[n], out_v[n] = (t.reshape(shp) for t in res)
        return out_v[names[-1]]

    grads["w_ple_proj"] = mm_tn(pb, demb, N_DEV, name="grad_w_ple_proj")
    dhn3 = mm_nt(dgl2, W["w_ple_gate"], BF16, name="back_ple_gate")
    grads["w_ple_gate"] = mm_tn(hn3, dgl2, 1, name="grad_w_ple_gate")

    def back_resid(dh, dhn, h, t, gpre, gpost):
        d1, dgpre = _rms_bwd(dhn, h, gpre)
        dhh = dh + d1
        dt, dgpost = _rms_bwd(dhh, t, gpost)
        return dhh, dt, dgpre, dgpost

    dh2, dff, dg5, dg4 = rowwise(back_resid, [dh3, dhn3, h2, ff], [g5, g4], [(D, F32), (D, BF16)], [D, D],
                                 name="back_resid_mlp")
    dup = mm_nt(dff, W["w_down"], BF16, epi=lambda acc, u_: (acc * 2.0 * jnp.maximum(u_.astype(F32), 0.0),),
                extra=up, name="back_mlp_down")
    grads["w_down"] = mm_tn(act, dff, 1, name="grad_w_down")
    group_a = ["w_ple_proj", "w_ple_gate", "w_down"]
    sent_a = to_sibling(group_a, grads["w_down"], "a")
    (dhn2,) = mm_nn(dup, WT["w_up"], [BF16], deps=(sent_a[1][-1],), name="back_mlp_up")
    sent_a = to_chips(group_a, sent_a, dhn2, "a")
    grads["w_up"] = mm_tn(hn2, dup, N_DEV, deps=(sent_a[1][-1],), name="grad_w_up")
    dh1, dmo, dg3, dg2 = rowwise(back_resid, [dh2, dhn2, h1, mo], [g3, g2], [(D, F32), (D, BF16)], [D, D],
                                 name="back_resid_mix")
    dmixed = mm_nt(dmo, W["w_out"], BF16, name="back_mix_out")
    grads["w_out"] = mm_tn(mixed, dmo, 1, name="grad_w_out")

    def back_glu(dm, yp, gl, g):
        ygf = _gelu(yp)
        sg = _sigmoid(gl)
        dssm, dg = _rms_bwd(dm, ygf * sg, g)
        dgl = dssm * ygf * sg * (1.0 - sg)
        return dgl, dssm * sg, dg, jnp.sum(dgl, axis=0, keepdims=True)

    dgl1, dyg_direct, dgs, db_glu = rowwise(back_glu, [_to_segments(dmixed[:, AW:]), y_pre, gl1], [gs],
                                            [(SW, BF16), (SW, F32)], [SW, SW], name="back_glu")
    dyg_gate = mm_nt(dgl1, W["w_glu"], BF16, name="back_glu_gate")
    grads["w_glu"] = mm_tn(yg, dgl1, 1, name="grad_w_glu")
    group_b = ["w_up", "w_out", "w_glu"]
    sent_b = to_sibling(group_b, grads["w_glu"], "b")
    done_a = finish(group_a, sent_a, sent_b[1][-1], "a")

    def back_gelu(d1, d2, yp, u_):
        dy = (d1 + d2) * _gelu_grad(yp)
        return dy, jnp.sum(dy * u_.astype(F32), axis=0, keepdims=True)

    dy_pre, d_skip = rowwise(back_gelu, [dyg_direct, dyg_gate, y_pre, u_seg], [], [(SW, F32)], [SW], deps=(done_a,),
                             name="back_gelu")
    du_seg, dbb_r, dbb_i, dcb_r, dcb_i, da_r, da_i = ssm_bwd(u_seg, dy_pre, *ssm_consts, name="ssm_bwd")
    sent_b = to_chips(group_b, sent_b, du_seg, "b")

    def back_attn_norm(dm, a, g):
        da, dg = _rms_bwd(dm, a, g)
        prod = da * a
        delta = jnp.concatenate(
            [jnp.broadcast_to(jnp.sum(prod[:, h * HEAD_DIM:(h + 1) * HEAD_DIM], axis=-1, keepdims=True),
                              (prod.shape[0], HEAD_DIM)) for h in range(H)], axis=1)
        return da, delta, dg

    dattn, delta, dga = rowwise(back_attn_norm, [(dmixed, AW, 0), attn], [ga], [(AW, F32), (AW, F32)], [AW],
                                deps=(sent_b[1][-1],), name="back_attn_norm")
    dq, dk, dv = attn_bwd(proj, dattn, lse, delta, H, name="attn_bwd")
    dproj = jnp.concatenate([dq, dk, dv, _from_segments(du_seg)], axis=1)
    (dhn1,) = mm_nn(dproj, WT["w_in"], [BF16], name="back_proj_in")

    def back_in(dh, dhn, a, g):
        d1, dg = _rms_bwd(dhn, a, g)
        return dh + d1, dg

    grad_x, dg1 = rowwise(back_in, [dh1, dhn1, xs], [g1], [(D, F32)], [D], name="back_norm_in")

    cot = dict(
        mix_norm_pre=dg1, mix_norm_post=dg2, mlp_norm_pre=dg3, mlp_norm_post=dg4, ple_norm_pre=dg5, ple_norm_post=dg6,
        attn_out_norm=dga, ssm_out_norm=dgs, b_glu=db_glu, ssm_d=d_skip,
        ssm_c_re=_block_diag_part(dcb_r, P_, C_).swapaxes(1, 2), ssm_c_im=_block_diag_part(dcb_i, P_, C_).swapaxes(1, 2),
        a_r=da_r.reshape(G, P_), a_i=da_i.reshape(G, P_),
        bb_r=_block_diag_part(dbb_r, C_, P_).swapaxes(1, 2), bb_i=_block_diag_part(dbb_i, C_, P_).swapaxes(1, 2))
    names = list(cot)
    flat = jnp.concatenate([cot[n].reshape(-1) for n in names])
    total = flat.shape[0]
    rows_ = -(-total // (LANES * 16)) * 16
    flat = jnp.pad(flat, (0, rows_ * LANES - total)).reshape(rows_, LANES)
    gather_small = gather_start([flat], flat, name="gather_small_start")
    grads["w_in"] = mm_tn(hn1, dproj, N_DEV, deps=(gather_small[-1],), name="grad_w_in")
    group_c = ["w_in"]
    sent_c = to_sibling(group_c, grads["w_in"], "c")
    done_b = finish(group_b, sent_b, sent_c[1][-1], "b")
    sent_c = to_chips(group_c, sent_c, done_b, "c")
    gather_small = gather_forward(gather_small, sent_c[1][-1], name="gather_small_forward")
    (every,) = gather_finish(gather_small, gather_small[-1], name="gather_small_finish")
    (summed,) = _blocked(lambda *t: (functools.reduce(lambda a, b: a + b, t),),
                         [(every, functools.partial(lambda i, p_, j: (j, i), j=j)) for j in range(N_DEV)],
                         [((rows_, LANES), F32)], name="sum_small_grads")
    summed = summed.reshape(-1)
    red, off = {}, 0
    for n in names:
        sz = cot[n].size
        red[n] = summed[off:off + sz].reshape(cot[n].shape)
        off += sz
    _, pull = jax.vjp(_discretise, lam_re[0], lam_im[0], log_dt[0], ssm_b_re[0], ssm_b_im[0])
    d_lre, d_lim, d_ldt, d_bre, d_bim = pull((red["a_r"], red["a_i"], red["bb_r"], red["bb_i"]))
    red.update(lam_re=d_lre, lam_im=d_lim, log_dt=d_ldt, ssm_b_re=d_bre, ssm_b_im=d_bim)

    def pack(d):
        t = jnp.concatenate([d[n].reshape(-1) for n in small])
        r_ = -(-t.shape[0] // (LANES * 16)) * 16
        return jnp.pad(t, (0, r_ * LANES - t.shape[0])).reshape(r_, LANES)

    sw, sg_, sm, sv = pack(weights), pack(red), pack(mom_m), pack(mom_v)
    sd, snm, snv = _blocked(lambda w_, g_, m_, v_: _adamw(w_, g_, m_, v_), [sw, sg_, sm, sv],
                            [(sw.shape, F32)] * 3, name="adamw_small")
    finish(group_c, sent_c, snv, "c")
    off = 0
    for n in small:
        sz = weights[n].size
        shp = weights[n].shape
        out_g[n] = red[n].reshape(shp)
        out_d[n] = sd.reshape(-1)[off:off + sz].reshape(shp)
        out_m[n] = snm.reshape(-1)[off:off + sz].reshape(shp)
        out_v[n] = snv.reshape(-1)[off:off + sz].reshape(shp)
        off += sz

    return (loss, grad_x[None], *[out_g[n] for n in order], *[out_d[n] for n in order],
            *[out_m[n] for n in order], *[out_v[n] for n in order])
```

```python
import functools
import math

import jax
import jax.numpy as jnp
from jax import lax
from jax.experimental import pallas as pl
from jax.experimental.pallas import tpu as pltpu

F32 = jnp.float32
BF16 = jnp.bfloat16
MESH = pl.DeviceIdType.MESH

N_DEV = 8
LANES = 128
SUBLANES = 8
VMEM_LIMIT = 48 * 1024 * 1024
VMEM_LIMIT_SCAN = 60 * 1024 * 1024

HEAD_DIM = 128
BLK = 128
DILATIONS = (1, 4, 16)
SSM_GROUP = 16
SSM_STATE = 64
SLAB_GROUPS = LANES // SSM_GROUP
SLAB_STATES = SLAB_GROUPS * SSM_STATE
SEGMENTS = SUBLANES
SCAN_UNROLL = 4
RMS_EPS = 1e-6
NEG_INF = -1e30

ADAM_LR = 0.001
ADAM_B1 = 0.9
ADAM_B2 = 0.999
ADAM_EPS = 1e-08
ADAM_WD = 0.01
ADAM_STEP = 10


def _tile(n, pref, unit=LANES):
    if n <= pref:
        return n
    t = (pref // unit) * unit
    while t > unit and n % t:
        t -= unit
    assert n % t == 0, (n, pref, unit)
    return t


def _params(sem=None, vmem=VMEM_LIMIT):
    return pltpu.CompilerParams(dimension_semantics=sem, vmem_limit_bytes=vmem)


_NN = (((1,), (0,)), ((), ()))
_NT = (((1,), (1,)), ((), ()))
_TN = (((0,), (0,)), ((), ()))


_ANY = pl.BlockSpec(memory_space=pl.ANY)


def _mm_call(dims, nk, n_extra, n_dep, n_out, epi, **kw):
    first_out = 2 + n_extra + n_dep
    kw["in_specs"] = list(kw["in_specs"]) + [_ANY] * n_dep

    def single(*refs):
        extra = refs[2:2 + n_extra]
        res = epi(lax.dot_general(refs[0][...], refs[1][...], dims, preferred_element_type=F32),
                  *[e[...] for e in extra])
        for o, r in zip(refs[first_out:first_out + n_out], res):
            o[...] = r.astype(o.dtype)

    if nk == 1:
        kw["scratch_shapes"] = []
        return pl.pallas_call(single, **kw)

    def body(*refs):
        a_ref, b_ref = refs[0], refs[1]
        extra = refs[2:2 + n_extra]
        outs = refs[first_out:first_out + n_out]
        acc = refs[-1]
        k = pl.program_id(2)

        @pl.when(k == 0)
        def _():
            acc[...] = jnp.zeros_like(acc)

        acc[...] += lax.dot_general(a_ref[...], b_ref[...], dims, preferred_element_type=F32)

        @pl.when(k == nk - 1)
        def _():
            res = epi(acc[...], *[e[...] for e in extra])
            for o, r in zip(outs, res):
                o[...] = r.astype(o.dtype)

    return pl.pallas_call(body, **kw)


def _identity_epi(acc):
    return (acc,)


def mm_nn(a, w, out_dtypes, *, name, epi=_identity_epi, bias=None, deps=(), tm=2048, tn=512, tk=2048):
    M, K = a.shape
    J, K2, n = w.shape
    assert K == K2
    tm, tn, tk = _tile(M, tm, 16), _tile(n, tn), _tile(K, tk)
    npj = n // tn
    nk = K // tk
    in_specs = [pl.BlockSpec((tm, tk), lambda i, j, k: (i, k)),
                pl.BlockSpec((None, tk, tn), lambda i, j, k: (j // npj, k, j % npj))]
    args = [a, w]
    if bias is not None:
        in_specs.append(pl.BlockSpec((1, tn), lambda i, j, k: (0, j)))
        args.append(bias)
    return _mm_call(
        _NN, nk, len(args) - 2, len(deps), len(out_dtypes), epi,
        out_shape=[jax.ShapeDtypeStruct((M, J * n), d) for d in out_dtypes],
        grid=(M // tm, J * npj, nk), in_specs=in_specs,
        out_specs=[pl.BlockSpec((tm, tn), lambda i, j, k: (i, j)) for _ in out_dtypes],
        scratch_shapes=[pltpu.VMEM((tm, tn), F32)],
        compiler_params=_params(("parallel", "parallel", "arbitrary")), name=name)(*args, *deps)


def mm_nt(a, w, out_dtype, *, name, epi=_identity_epi, extra=None, tm=2048, tko=512, tnr=2048):
    M, N = a.shape
    J, K, n = w.shape
    assert N == J * n
    tm, tko, tnr = _tile(M, tm, 16), _tile(K, tko), _tile(n, tnr)
    npj = n // tnr
    nk = N // tnr
    in_specs = [pl.BlockSpec((tm, tnr), lambda i, j, k: (i, k)),
                pl.BlockSpec((None, tko, tnr), lambda i, j, k: (k // npj, j, k % npj))]
    args = [a, w]
    if extra is not None:
        in_specs.append(pl.BlockSpec((tm, tko), lambda i, j, k: (i, j)))
        args.append(extra)
    return _mm_call(
        _NT, nk, len(args) - 2, 0, 1, epi,
        out_shape=[jax.ShapeDtypeStruct((M, K), out_dtype)],
        grid=(M // tm, K // tko, nk), in_specs=in_specs,
        out_specs=[pl.BlockSpec((tm, tko), lambda i, j, k: (i, j))],
        scratch_shapes=[pltpu.VMEM((tm, tko), F32)],
        compiler_params=_params(("parallel", "parallel", "arbitrary")), name=name)(*args)[0]


def mm_tn(a, b, J, *, name, deps=(), tko=1024, tn=1024, ts=2048):
    S, K = a.shape
    S2, N = b.shape
    assert S == S2 and N % J == 0
    n = N // J
    tko, tn, ts = _tile(K, tko), _tile(n, tn), _tile(S, ts)
    npj = n // tn
    nk = S // ts
    return _mm_call(
        _TN, nk, 0, len(deps), 1, _identity_epi,
        out_shape=[jax.ShapeDtypeStruct((J, K, n), BF16)],
        grid=(K // tko, J * npj, nk),
        in_specs=[pl.BlockSpec((ts, tko), lambda i, j, k: (k, i)),
                  pl.BlockSpec((ts, tn), lambda i, j, k: (k, j))],
        out_specs=[pl.BlockSpec((None, tko, tn), lambda i, j, k: (j // npj, i, j % npj))],
        scratch_shapes=[pltpu.VMEM((tko, tn), F32)],
        compiler_params=_params(("parallel", "parallel", "arbitrary")), name=name)(a, b, *deps)[0]


def rowwise(fn, rows, vecs, outs, accs=(), *, name, deps=(), ts=256):
    rows = [r if isinstance(r, tuple) else (r, r.shape[1], 0) for r in rows]
    S = rows[0][0].shape[0]
    ts = _tile(S, ts, 16)
    nr, nv, no, nd = len(rows), len(vecs), len(outs), len(deps)

    def body(*refs):
        r, v = refs[:nr], refs[nr:nr + nv]
        o, a = refs[nr + nv + nd:nr + nv + nd + no], refs[nr + nv + nd + no:]
        res = fn(*[t[...].astype(F32) for t in r], *[t[...] for t in v])
        for ref, val in zip(o, res[:no]):
            ref[...] = val.astype(ref.dtype)
        if a:
            @pl.when(pl.program_id(0) == 0)
            def _():
                for ref in a:
                    ref[...] = jnp.zeros_like(ref)

            for ref, val in zip(a, res[no:]):
                ref[...] += val

    in_specs = [pl.BlockSpec((ts, w), functools.partial(lambda i, cb: (i, cb), cb=cb)) for _, w, cb in rows]
    in_specs += [pl.BlockSpec(v.shape, lambda i: (0, 0)) for v in vecs] + [_ANY] * nd
    out_shape = [jax.ShapeDtypeStruct((S, w), d) for w, d in outs]
    out_shape += [jax.ShapeDtypeStruct((1, w), F32) for w in accs]
    out_specs = [pl.BlockSpec((ts, w), lambda i: (i, 0)) for w, _ in outs]
    out_specs += [pl.BlockSpec((1, w), lambda i: (0, 0)) for w in accs]
    return pl.pallas_call(body, out_shape=out_shape, grid=(S // ts,), in_specs=in_specs, out_specs=out_specs,
                          compiler_params=_params(("arbitrary",)), name=name)(*[r[0] for r in rows], *vecs, *deps)


def _rms(x, g):
    r = lax.rsqrt(jnp.mean(x * x, axis=-1, keepdims=True) + RMS_EPS)
    return x * r * g


def _rms_bwd(dy, x, g):
    r = lax.rsqrt(jnp.mean(x * x, axis=-1, keepdims=True) + RMS_EPS)
    xh = x * r
    dxh = dy * g
    dx = r * (dxh - xh * jnp.mean(dxh * xh, axis=-1, keepdims=True))
    return dx, jnp.sum(dy * xh, axis=0, keepdims=True)


def _sigmoid(x):
    return pl.reciprocal(1.0 + jnp.exp(-x), approx=True)


_GELU_C = math.sqrt(2.0 / math.pi)


def _gelu(x):
    return 0.5 * x * (1.0 + jnp.tanh(_GELU_C * (x + 0.044715 * x * x * x)))


def _gelu_grad(x):
    t = jnp.tanh(_GELU_C * (x + 0.044715 * x * x * x))
    return 0.5 * (1.0 + t) + 0.5 * x * (1.0 - t * t) * _GELU_C * (1.0 + 3.0 * 0.044715 * x * x)


ATTN_INTERLEAVE = 8
KEY_PAD = BLK * max(DILATIONS)


def _key_mask(n):
    ii = lax.broadcasted_iota(jnp.int32, (BLK, 2 * BLK), 0)
    jj = lax.broadcasted_iota(jnp.int32, (BLK, 2 * BLK), 1)
    return ((jj < BLK) & (jj >= ii) & (n > 0)) | ((jj >= BLK) & (jj - BLK <= ii))


def _units(d, nblk):
    nb = nblk // d
    if nb == 2:
        def unit(idx):
            ii = lax.broadcasted_iota(jnp.int32, (2 * BLK, 2 * BLK), 0)
            jj = lax.broadcasted_iota(jnp.int32, (2 * BLK, 2 * BLK), 1)
            return pl.ds(idx, 2 * BLK, stride=d), pl.ds(KEY_PAD + idx, 2 * BLK, stride=d), (jj <= ii) & (ii - jj <= BLK)
        return d, max(1, ATTN_INTERLEAVE // 2), unit

    def unit(idx):
        r, n = idx // nb, idx % nb
        cur = r + n * (BLK * d)
        keys = cur + (KEY_PAD - BLK * d)
        if d == 1:
            return pl.ds(pl.multiple_of(cur, BLK), BLK), pl.ds(pl.multiple_of(keys, BLK), 2 * BLK), _key_mask(n)
        return pl.ds(cur, BLK, stride=d), pl.ds(keys, 2 * BLK, stride=d), _key_mask(n)
    return nblk, ATTN_INTERLEAVE, unit


def _pad_keys(dst, src):
    dst[pl.ds(0, KEY_PAD), :] = jnp.zeros((KEY_PAD, dst.shape[1]), F32)

    def copy(c, carry):
        dst[pl.ds(pl.multiple_of(KEY_PAD + c * BLK, BLK), BLK), :] = src[pl.ds(pl.multiple_of(c * BLK, BLK), BLK), :]
        return carry

    lax.fori_loop(0, src.shape[0] // BLK, copy, 0)


def attn_fwd(proj, n_heads, *, name):
    S, WP = proj.shape
    assert S % (BLK * max(DILATIONS)) == 0
    nblk = S // BLK
    AW = n_heads * HEAD_DIM
    scale = 1.0 / math.sqrt(HEAD_DIM)

    def body(q_ref, k_ref, v_ref, o_ref, l_ref, acc, mrun, lrun, kp, vp):
        _pad_keys(kp, k_ref)
        _pad_keys(vp, v_ref)
        for first, d in zip((True, False, False), reversed(DILATIONS)):
            n_units, per_step, unit = _units(d, nblk)

            def step(it, carry, first=first, n_units=n_units, per_step=per_step, unit=unit):
                units = [unit(it + j * (n_units // per_step)) for j in range(per_step)]
                ss = [lax.dot_general(q_ref[cur, :].astype(BF16), kp[keys, :].astype(BF16), _NT,
                                      preferred_element_type=F32) * scale for cur, keys, _ in units]
                for j, (cur, keys, mask) in enumerate(units):
                    s = jnp.where(mask, ss[j], NEG_INF)
                    m = jnp.max(s, axis=-1, keepdims=True)
                    p = jnp.exp(s - m)
                    l = jnp.sum(p, axis=-1, keepdims=True)
                    o = jnp.dot(p.astype(BF16), vp[keys, :].astype(BF16), preferred_element_type=F32)
                    m = jnp.broadcast_to(m, o.shape)
                    l = jnp.broadcast_to(l, o.shape)
                    if first:
                        acc[cur, :], mrun[cur, :], lrun[cur, :] = o, m, l
                    else:
                        m_old = mrun[cur, :]
                        m_new = jnp.maximum(m_old, m)
                        w_old, w_blk = jnp.exp(m_old - m_new), jnp.exp(m - m_new)
                        acc[cur, :] = w_old * acc[cur, :] + w_blk * o
                        lrun[cur, :] = w_old * lrun[cur, :] + w_blk * l
                        mrun[cur, :] = m_new
                return carry

            lax.fori_loop(0, n_units // per_step, step, 0)

        def finish(c, carry):
            r = pl.ds(pl.multiple_of(c * BLK, BLK), BLK)
            o_ref[r, :] = acc[r, :] / lrun[r, :]
            l_ref[r, :] = mrun[r, :] + jnp.log(lrun[r, :])
            return carry

        lax.fori_loop(0, nblk, finish, 0)

    def col(off):
        return pl.BlockSpec((S, HEAD_DIM), lambda h: (0, off + h))

    ospec = pl.BlockSpec((S, HEAD_DIM), lambda h: (0, h))
    return pl.pallas_call(
        body, out_shape=[jax.ShapeDtypeStruct((S, AW), F32)] * 2, grid=(n_heads,),
        in_specs=[col(0), col(n_heads), col(2 * n_heads)], out_specs=[ospec, ospec],
        scratch_shapes=[pltpu.VMEM((S, HEAD_DIM), F32)] * 3 + [pltpu.VMEM((KEY_PAD + S, HEAD_DIM), F32)] * 2,
        compiler_params=_params(("parallel",)), name=name)(proj, proj, proj)


def attn_bwd(proj, do, lse, delta, n_heads, *, name):
    S, WP = proj.shape
    nblk = S // BLK
    AW = n_heads * HEAD_DIM
    scale = 1.0 / math.sqrt(HEAD_DIM)

    def body(q_ref, k_ref, v_ref, do_ref, l_ref, dl_ref, dq_ref, dk_ref, dv_ref, dq_sc, dk_sc, dv_sc, kp, vp):
        _pad_keys(kp, k_ref)
        _pad_keys(vp, v_ref)
        dq_sc[...] = jnp.zeros_like(dq_sc)
        dk_sc[...] = jnp.zeros_like(dk_sc)
        dv_sc[...] = jnp.zeros_like(dv_sc)
        for d in DILATIONS:
            n_units, per_step, unit = _units(d, nblk)

            def step(it, carry, n_units=n_units, per_step=per_step, unit=unit):
                for cur, keys, mask in [unit(it + j * (n_units // per_step)) for j in range(per_step)]:
                    q = q_ref[cur, :].astype(BF16)
                    g = do_ref[cur, :].astype(BF16)
                    kb = kp[keys, :].astype(BF16)
                    vb = vp[keys, :].astype(BF16)
                    s = lax.dot_general(q, kb, _NT, preferred_element_type=F32) * scale
                    p = jnp.where(mask, jnp.exp(s - l_ref[cur, :][:, :1]), 0.0)
                    dp = lax.dot_general(g, vb, _NT, preferred_element_type=F32)
                    ds = (p * (dp - dl_ref[cur, :][:, :1]) * scale).astype(BF16)
                    dq_sc[cur, :] += jnp.dot(ds, kb, preferred_element_type=F32)
                    dk_sc[keys, :] += lax.dot_general(ds, q, _TN, preferred_element_type=F32)
                    dv_sc[keys, :] += lax.dot_general(p.astype(BF16), g, _TN, preferred_element_type=F32)
                return carry

            lax.fori_loop(0, n_units // per_step, step, 0)
        rows = pl.ds(KEY_PAD, S)
        dq_ref[...] = dq_sc[...].astype(BF16)
        dk_ref[...] = dk_sc[rows, :].astype(BF16)
        dv_ref[...] = dv_sc[rows, :].astype(BF16)

    def col(off):
        return pl.BlockSpec((S, HEAD_DIM), lambda h: (0, off + h))

    ospec = pl.BlockSpec((S, HEAD_DIM), lambda h: (0, h))
    return pl.pallas_call(
        body, out_shape=[jax.ShapeDtypeStruct((S, AW), BF16)] * 3, grid=(n_heads,),
        in_specs=[col(0), col(n_heads), col(2 * n_heads), ospec, ospec, ospec], out_specs=[ospec] * 3,
        scratch_shapes=[pltpu.VMEM((S, HEAD_DIM), F32)] + [pltpu.VMEM((KEY_PAD + S, HEAD_DIM), F32)] * 4,
        compiler_params=_params(("parallel",), VMEM_LIMIT_SCAN), name=name)(proj, proj, proj, do, lse, delta)


def _to_segments(t):
    S, W = t.shape
    return t.reshape(SEGMENTS, S // SEGMENTS, W).swapaxes(0, 1).reshape(S, W)


def _from_segments(t):
    S, W = t.shape
    return t.reshape(S // SEGMENTS, SEGMENTS, W).swapaxes(0, 1).reshape(S, W)


def _cmul(ar, ai, br, bi):
    return ar * br - ai * bi, ar * bi + ai * br


def _power(ar, ai, log2n):
    for _ in range(log2n):
        ar, ai = _cmul(ar, ai, ar, ai)
    return ar, ai


def _shift_rows(x, up):
    row = lax.broadcasted_iota(jnp.int32, x.shape, 0)
    if up:
        return jnp.where(row == SEGMENTS - 1, 0.0, pltpu.roll(x, SEGMENTS - 1, 0))
    return jnp.where(row == 0, 0.0, pltpu.roll(x, 1, 0))


def _segment_carries(er, ei, pr, pi, up):
    cr = jnp.zeros_like(er)
    ci = jnp.zeros_like(ei)
    for _ in range(SEGMENTS - 1):
        tr, ti = _cmul(pr, pi, cr, ci)
        cr, ci = _shift_rows(er + tr, up), _shift_rows(ei + ti, up)
    return cr, ci


def _scan_states(sr, si, ar, ai, T, reverse):
    ns = sr.shape[1]
    ar8 = jnp.broadcast_to(ar, (SEGMENTS, ns))
    ai8 = jnp.broadcast_to(ai, (SEGMENTS, ns))

    def rows(t):
        k = (T - 1 - t) if reverse else t
        return pl.ds(pl.multiple_of(k * SEGMENTS, SEGMENTS), SEGMENTS)

    def advance(t, c):
        tr, ti = _cmul(ar8, ai8, c[0], c[1])
        return tr + sr[rows(t), :], ti + si[rows(t), :]

    def several(step):
        def trip(t, c):
            for j in range(SCAN_UNROLL):
                c = step(t * SCAN_UNROLL + j, c)
            return c
        return trip

    zero = jnp.zeros((SEGMENTS, ns), F32)
    er, ei = lax.fori_loop(0, T // SCAN_UNROLL, several(advance), (zero, zero))
    pr, pi = _power(ar, ai, T.bit_length() - 1)
    cr, ci = _segment_carries(er, ei, jnp.broadcast_to(pr, (SEGMENTS, ns)), jnp.broadcast_to(pi, (SEGMENTS, ns)), reverse)

    def store(t, c):
        nr, ni = advance(t, c)
        sr[rows(t), :] = nr
        si[rows(t), :] = ni
        return nr, ni

    lax.fori_loop(0, T // SCAN_UNROLL, several(store), (cr, ci))
    return cr, ci


def _slab_specs(ns):
    return [pl.BlockSpec((None, LANES, ns), lambda g: (g, 0, 0)),
            pl.BlockSpec((None, LANES, ns), lambda g: (g, 0, 0)),
            pl.BlockSpec((None, 1, ns), lambda g: (g, 0, 0)),
            pl.BlockSpec((None, 1, ns), lambda g: (g, 0, 0)),
            pl.BlockSpec((None, ns, LANES), lambda g: (g, 0, 0)),
            pl.BlockSpec((None, ns, LANES), lambda g: (g, 0, 0)),
            pl.BlockSpec((1, LANES), lambda g: (0, g))]


def _chunks(S):
    rc = _tile(S, 512, 16)
    return rc, S // rc


def ssm_fwd(u, bbr, bbi, ar, ai, cbr, cbi, dsk, *, name):
    S, SW = u.shape
    nslab, _, ns = bbr.shape
    T = S // SEGMENTS
    assert T & (T - 1) == 0
    rc, nc = _chunks(S)

    def body(u_ref, br_ref, bi_ref, ar_ref, ai_ref, cr_ref, ci_ref, d_ref, y_ref, sr, si):
        def inputs(c, carry):
            r = pl.ds(pl.multiple_of(c * rc, rc), rc)
            sr[r, :] = jnp.dot(u_ref[r, :], br_ref[...], preferred_element_type=F32)
            si[r, :] = jnp.dot(u_ref[r, :], bi_ref[...], preferred_element_type=F32)
            return carry

        lax.fori_loop(0, nc, inputs, 0)
        _scan_states(sr, si, ar_ref[...], ai_ref[...], T, False)

        def outputs(c, carry):
            r = pl.ds(pl.multiple_of(c * rc, rc), rc)
            y_ref[r, :] = (jnp.dot(sr[r, :].astype(BF16), cr_ref[...], preferred_element_type=F32)
                           - jnp.dot(si[r, :].astype(BF16), ci_ref[...], preferred_element_type=F32)
                           + d_ref[...] * u_ref[r, :].astype(F32))
            return carry

        lax.fori_loop(0, nc, outputs, 0)

    slab = pl.BlockSpec((S, LANES), lambda g: (0, g))
    return pl.pallas_call(
        body, out_shape=jax.ShapeDtypeStruct((S, SW), F32), grid=(nslab,),
        in_specs=[slab] + _slab_specs(ns), out_specs=slab,
        scratch_shapes=[pltpu.VMEM((S, ns), F32)] * 2,
        compiler_params=_params(("parallel",), VMEM_LIMIT_SCAN), name=name)(u, bbr, bbi, ar, ai, cbr, cbi, dsk)


def ssm_bwd(u, dy, bbr, bbi, ar, ai, cbr, cbi, dsk, *, name):
    S, SW = u.shape
    nslab, _, ns = bbr.shape
    T = S // SEGMENTS
    rc, nc = _chunks(S)

    def body(u_ref, dy_ref, br_ref, bi_ref, ar_ref, ai_ref, cr_ref, ci_ref, d_ref,
             du_ref, dbr_ref, dbi_ref, dcr_ref, dci_ref, dar_ref, dai_ref, sr, si, lr, li):
        def inputs(c, carry):
            r = pl.ds(pl.multiple_of(c * rc, rc), rc)
            ub = u_ref[r, :]
            gb = dy_ref[r, :].astype(BF16)
            sr[r, :] = jnp.dot(ub, br_ref[...], preferred_element_type=F32)
            si[r, :] = jnp.dot(ub, bi_ref[...], preferred_element_type=F32)
            lr[r, :] = lax.dot_general(gb, cr_ref[...], _NT, preferred_element_type=F32)
            li[r, :] = -lax.dot_general(gb, ci_ref[...], _NT, preferred_element_type=F32)
            return carry

        lax.fori_loop(0, nc, inputs, 0)
        ar, ai = ar_ref[...], ai_ref[...]
        s0r, s0i = _scan_states(sr, si, ar, ai, T, False)
        _scan_states(lr, li, ar, -ai, T, True)

        def pair(k, c):
            aligned = (lambda v: v) if isinstance(k, int) else (lambda v: pl.multiple_of(v, SEGMENTS))
            now = pl.ds(aligned(k * SEGMENTS), SEGMENTS)
            prev = pl.ds(aligned((k - 1) * SEGMENTS), SEGMENTS)
            return (c[0] + lr[now, :] * sr[prev, :] + li[now, :] * si[prev, :],
                    c[1] - lr[now, :] * si[prev, :] + li[now, :] * sr[prev, :])

        first = pl.ds(0, SEGMENTS)
        acc = (lr[first, :] * s0r + li[first, :] * s0i, -lr[first, :] * s0i + li[first, :] * s0r)

        def pairs(t, c):
            for j in range(SCAN_UNROLL):
                c = pair(1 + t * SCAN_UNROLL + j, c)
            return c

        whole = (T - 1) // SCAN_UNROLL
        acc = lax.fori_loop(0, whole, pairs, acc)
        for k in range(1 + whole * SCAN_UNROLL, T):
            acc = pair(k, acc)
        dar_ref[...] = jnp.sum(acc[0], axis=0, keepdims=True)
        dai_ref[...] = jnp.sum(acc[1], axis=0, keepdims=True)

        dbr_ref[...] = jnp.zeros_like(dbr_ref)
        dbi_ref[...] = jnp.zeros_like(dbi_ref)
        dcr_ref[...] = jnp.zeros_like(dcr_ref)
        dci_ref[...] = jnp.zeros_like(dci_ref)

        def outputs(c, carry):
            r = pl.ds(pl.multiple_of(c * rc, rc), rc)
            ub = u_ref[r, :]
            g = dy_ref[r, :]
            gb = g.astype(BF16)
            lrb = lr[r, :].astype(BF16)
            lib = li[r, :].astype(BF16)
            du_ref[r, :] = (lax.dot_general(lrb, br_ref[...], _NT, preferred_element_type=F32)
                            + lax.dot_general(lib, bi_ref[...], _NT, preferred_element_type=F32)
                            + d_ref[...] * g).astype(BF16)
            dbr_ref[...] += lax.dot_general(ub, lrb, _TN, preferred_element_type=F32)
            dbi_ref[...] += lax.dot_general(ub, lib, _TN, preferred_element_type=F32)
            dcr_ref[...] += lax.dot_general(sr[r, :].astype(BF16), gb, _TN, preferred_element_type=F32)
            dci_ref[...] -= lax.dot_general(si[r, :].astype(BF16), gb, _TN, preferred_element_type=F32)
            return carry

        lax.fori_loop(0, nc, outputs, 0)

    slab = pl.BlockSpec((S, LANES), lambda g: (0, g))
    bspec = pl.BlockSpec((None, LANES, ns), lambda g: (g, 0, 0))
    cspec = pl.BlockSpec((None, ns, LANES), lambda g: (g, 0, 0))
    aspec = pl.BlockSpec((None, 1, ns), lambda g: (g, 0, 0))
    return pl.pallas_call(
        body,
        out_shape=[jax.ShapeDtypeStruct((S, SW), BF16),
                   jax.ShapeDtypeStruct((nslab, LANES, ns), F32), jax.ShapeDtypeStruct((nslab, LANES, ns), F32),
                   jax.ShapeDtypeStruct((nslab, ns, LANES), F32), jax.ShapeDtypeStruct((nslab, ns, LANES), F32),
                   jax.ShapeDtypeStruct((nslab, 1, ns), F32), jax.ShapeDtypeStruct((nslab, 1, ns), F32)],
        grid=(nslab,), in_specs=[slab, slab] + _slab_specs(ns),
        out_specs=[slab, bspec, bspec, cspec, cspec, aspec, aspec],
        scratch_shapes=[pltpu.VMEM((S, ns), F32)] * 4,
        compiler_params=_params(("parallel",), VMEM_LIMIT_SCAN), name=name)(u, dy, bbr, bbi, ar, ai, cbr, cbi, dsk)


def _discretise(lam_re, lam_im, log_dt, b_re, b_im):
    dt = jnp.exp(log_dt)[:, None]
    mag = jnp.exp(lam_re * dt)
    ar = mag * jnp.cos(lam_im * dt)
    ai = mag * jnp.sin(lam_im * dt)
    nr, ni = ar - 1.0, ai
    den = lam_re * lam_re + lam_im * lam_im
    cr = ((nr * lam_re + ni * lam_im) / den)[..., None]
    ci = ((ni * lam_re - nr * lam_im) / den)[..., None]
    return ar, ai, cr * b_re - ci * b_im, cr * b_im + ci * b_re


def _block_diag(t, nslab):
    G, R, C = t.shape
    eye = jnp.eye(SLAB_GROUPS, dtype=t.dtype)
    t = t.reshape(nslab, SLAB_GROUPS, R, C)
    return jnp.einsum('sgrc,gh->sgrhc', t, eye).reshape(nslab, SLAB_GROUPS * R, SLAB_GROUPS * C)


def _block_diag_part(t, R, C):
    nslab = t.shape[0]
    eye = jnp.eye(SLAB_GROUPS, dtype=t.dtype)
    t = t.reshape(nslab, SLAB_GROUPS, R, SLAB_GROUPS, C)
    return jnp.einsum('sgrhc,gh->sgrc', t, eye).reshape(nslab * SLAB_GROUPS, R, C)


def _place():
    return lax.axis_index("x"), lax.axis_index("y"), lax.axis_index("c")


def all_gather(shards, *, name):
    nw = len(shards)

    def body(*refs):
        ins, outs = refs[:nw], refs[nw:2 * nw]
        send_sems, recv_sems, local_sems = refs[2 * nw:]
        x, y, c = _place()
        me, sibling = (x, y, c), (x, y, 1 - c)
        chips = [(1 - x, y), (x, 1 - y), (1 - x, 1 - y)]

        def copy(w, k, block, to, own):
            px, py, pc = block
            slot = outs[w].at[4 * px + 2 * py + pc]
            return pltpu.make_async_remote_copy(
                src_ref=ins[w] if own else slot, dst_ref=slot, send_sem=send_sems.at[w, k],
                recv_sem=recv_sems.at[w, k], device_id=to, device_id_type=MESH)

        mine = [pltpu.make_async_copy(ins[w], outs[w].at[4 * x + 2 * y + c], local_sems.at[w]) for w in range(nw)]
        for cp in mine:
            cp.start()
        first = []
        for w in range(nw):
            first.append(copy(w, 0, me, sibling, True))
            first += [copy(w, 1 + j, me, (*chip, c), True) for j, chip in enumerate(chips)]
        for cp in first:
            cp.start()
        passed = []
        for j, chip in enumerate(chips):
            for w in range(nw):
                copy(w, 1 + j, (*chip, c), me, False).wait_recv()
                cp = copy(w, 4 + j, (*chip, c), sibling, False)
                cp.start()
                passed.append(cp)
        for w in range(nw):
            copy(w, 0, sibling, me, False).wait_recv()
            for j, chip in enumerate(chips):
                copy(w, 4 + j, (*chip, 1 - c), me, False).wait_recv()
        for cp in first + passed:
            cp.wait_send()
        for cp in mine:
            cp.wait()

    anyspec = pl.BlockSpec(memory_space=pl.ANY)
    return pl.pallas_call(
        body, out_shape=[jax.ShapeDtypeStruct((N_DEV,) + s.shape, s.dtype) for s in shards],
        in_specs=[anyspec] * nw, out_specs=[anyspec] * nw,
        scratch_shapes=[pltpu.SemaphoreType.DMA((nw, 7)), pltpu.SemaphoreType.DMA((nw, 7)),
                        pltpu.SemaphoreType.DMA((nw,))],
        compiler_params=pltpu.CompilerParams(has_side_effects=True), name=name)(*shards)


_HBM = pl.BlockSpec(memory_space=pltpu.HBM)
_SEM = pl.BlockSpec(memory_space=pltpu.SEMAPHORE)
_ORDERED_EFFECT = pltpu.SideEffectType.DATAFLOW_SIDE_EFFECTING


def _split_call(name, srcs, zones, sems_in, n_new, body_fn, after):
    nsrc, nz, ns, nn = len(srcs), len(zones), len(sems_in), len(n_new)
    nb = nsrc + nz

    def body(*refs):
        outs = refs[nb + ns + 1:]
        body_fn(refs[:nb], refs[nb:nb + ns], outs[:nn])
        outs[nn + nz][...] = jnp.zeros((SUBLANES, LANES), F32)

    res = pl.pallas_call(
        body, name=name,
        out_shape=([pltpu.SemaphoreType.DMA((n,)) for n in n_new] + [pltpu.HBM(b.shape, b.dtype) for b in zones]
                   + [jax.ShapeDtypeStruct((SUBLANES, LANES), F32)]),
        in_specs=[_HBM] * nb + [_SEM] * ns + [_ANY],
        out_specs=[_SEM] * nn + [_HBM] * nz + [pl.BlockSpec(memory_space=pltpu.VMEM)],
        input_output_aliases={nsrc + i: nn + i for i in range(nz)},
        compiler_params=pltpu.CompilerParams(has_side_effects=_ORDERED_EFFECT))(
            *[pltpu.with_memory_space_constraint(b, pltpu.HBM) for b in list(srcs) + list(zones)], *sems_in, after)
    return list(res[:nn]), list(res[nn:nn + nz]), res[-1]


def _mesh_peers():
    x, y, c = _place()
    return x, y, c, (x, y, 1 - c), [(1 - x, y), (x, 1 - y), (1 - x, 1 - y)]


def gather_start(shards, after, *, name):
    nw = len(shards)
    x, y, c = _place()
    zones = [lax.dynamic_update_slice(lax.empty((N_DEV,) + s.shape, s.dtype), s[None], (4 * x + 2 * y + c, 0, 0))
             for s in shards]

    def body(bufs, taken, new):
        for cp in _gather_first(bufs, nw, new[0], new[1]):
            cp.start()

    sems, zones, token = _split_call(name, shards, zones, [], [4 * nw, 4 * nw], body, after)
    return shards, sems, zones, token


def _gather_first(bufs, nw, send, recv):
    x, y, c, sibling, chips = _mesh_peers()
    out = []
    for w in range(nw):
        slot = bufs[nw + w].at[4 * x + 2 * y + c]
        for k, to in enumerate([sibling] + [(*ch, c) for ch in chips]):
            out.append(pltpu.make_async_remote_copy(
                src_ref=bufs[w], dst_ref=slot, send_sem=send.at[4 * w + k], recv_sem=recv.at[4 * w + k],
                device_id=to, device_id_type=MESH))
    return out


def _gather_slot_copy(bufs, nw, w, block, send_sem, recv_sem, to):
    px, py, pc = block
    slot = bufs[nw + w].at[4 * px + 2 * py + pc]
    return pltpu.make_async_remote_copy(src_ref=slot, dst_ref=slot, send_sem=send_sem, recv_sem=recv_sem,
                                        device_id=to, device_id_type=MESH)


def gather_forward(state, after, *, name):
    shards, sems, zones, _ = state
    nw = len(shards)

    def body(bufs, taken, new):
        x, y, c, sibling, chips = _mesh_peers()
        for j, ch in enumerate(chips):
            for w in range(nw):
                k = 4 * w + 1 + j
                _gather_slot_copy(bufs, nw, w, (*ch, c), taken[0].at[k], taken[1].at[k], (*ch, c)).wait_recv()
                _gather_slot_copy(bufs, nw, w, (*ch, c), new[0].at[3 * w + j], new[1].at[3 * w + j], sibling).start()
        for w in range(nw):
            _gather_slot_copy(bufs, nw, w, sibling, taken[0].at[4 * w], taken[1].at[4 * w], sibling).wait_recv()
        for cp in _gather_first(bufs, nw, taken[0], taken[1]):
            cp.wait_send()

    sems, zones, token = _split_call(name, shards, zones, sems, [3 * nw, 3 * nw], body, after)
    return shards, sems, zones, token


def gather_finish(state, after, *, name):
    shards, sems, zones, _ = state
    nw = len(shards)

    def body(bufs, taken, new):
        x, y, c, sibling, chips = _mesh_peers()
        for w in range(nw):
            for j, ch in enumerate(chips):
                cp = _gather_slot_copy(bufs, nw, w, (*ch, 1 - c), taken[0].at[3 * w + j], taken[1].at[3 * w + j], sibling)
                cp.wait_send()
                cp.wait_recv()

    _, zones, _ = _split_call(name, shards, zones, sems, [], body, after)
    return zones


def exchange_start(srcs, zone_shapes, copies, n, after, *, name):
    nw = len(srcs)
    zones = [lax.empty(z, s.dtype) for z, s in zip(zone_shapes, srcs)]

    def body(bufs, taken, new):
        for cp in copies(bufs[:nw], bufs[nw:], new[0], new[1]):
            cp.start()

    sems, zones, token = _split_call(name, srcs, zones, [], [n, n], body, after)
    return srcs, copies, sems, zones, token


def exchange_wait(state, after, *, name):
    srcs, copies, sems, zones, _ = state
    nw = len(srcs)

    def body(bufs, taken, new):
        for cp in copies(bufs[:nw], bufs[nw:], taken[0], taken[1]):
            cp.wait_send()
            cp.wait_recv()

    _, zones, _ = _split_call(name, srcs, zones, sems, [], body, after)
    return zones


def _core_copies(srcs, zones, send, recv):
    x, y, c = _place()
    return [pltpu.make_async_remote_copy(
        src_ref=srcs[w].at[:, 1 - c], dst_ref=zones[w], send_sem=send.at[w], recv_sem=recv.at[w],
        device_id=(x, y, 1 - c), device_id_type=MESH) for w in range(len(srcs))]


def _chip_copies(srcs, zones, send, recv):
    x, y, c = _place()
    chips = [(1 - x, y), (x, 1 - y), (1 - x, 1 - y)]
    return [pltpu.make_async_remote_copy(
        src_ref=srcs[w].at[2 * cx + cy], dst_ref=zones[w].at[j], send_sem=send.at[3 * w + j],
        recv_sem=recv.at[3 * w + j], device_id=(cx, cy, c), device_id_type=MESH)
        for w in range(len(srcs)) for j, (cx, cy) in enumerate(chips)]


def _blocked(fn, ins, outs, *, name, place=None, tr=256):
    k, n = outs[0][0]
    tr = _tile(k, tr, 16)
    if place is None:
        place = jnp.zeros((1,), jnp.int32)
    specs = []
    args = []
    for a in ins:
        if isinstance(a, tuple):
            arr, lead = a
            specs.append(pl.BlockSpec((None, tr, n), functools.partial(lambda i, s, lead: (*lead(i, s), 0), lead=lead)))
            args.append(arr)
        else:
            specs.append(pl.BlockSpec((tr, n), lambda i, s: (i, 0)))
            args.append(a)
    nin = len(args)

    def body(place_ref, *refs):
        res = fn(*[r[...] for r in refs[:nin]])
        for ref, val in zip(refs[nin:], res):
            ref[...] = val.astype(ref.dtype)

    return pl.pallas_call(
        body, out_shape=[jax.ShapeDtypeStruct(s, d) for s, d in outs],
        grid_spec=pltpu.PrefetchScalarGridSpec(
            num_scalar_prefetch=1, grid=(k // tr,), in_specs=specs,
            out_specs=[pl.BlockSpec((tr, n), lambda i, s: (i, 0)) for _ in outs]),
        compiler_params=_params(("parallel",)), name=name)(place, *args)


def _adamw(w, g, m, v):
    m = ADAM_B1 * m + (1.0 - ADAM_B1) * g
    v = ADAM_B2 * v + (1.0 - ADAM_B2) * (g * g)
    m_hat = m / (1.0 - ADAM_B1 ** ADAM_STEP)
    v_hat = v / (1.0 - ADAM_B2 ** ADAM_STEP)
    delta = -ADAM_LR * (m_hat * pl.reciprocal(jnp.sqrt(v_hat) + ADAM_EPS, approx=True) + ADAM_WD * w)
    return delta, m, v


def kernel(x, p, mix_norm_pre, w_in, lam_re, lam_im, log_dt, ssm_b_re, ssm_b_im, ssm_c_re, ssm_c_im, ssm_d, w_glu, b_glu, attn_out_norm, ssm_out_norm, w_out, mix_norm_post, mlp_norm_pre, w_up, w_down, mlp_norm_post, ple_norm_pre, w_ple_gate, w_ple_proj, ple_norm_post, loss_target, m_mix_norm_pre, m_w_in, m_lam_re, m_lam_im, m_log_dt, m_ssm_b_re, m_ssm_b_im, m_ssm_c_re, m_ssm_c_im, m_ssm_d, m_w_glu, m_b_glu, m_attn_out_norm, m_ssm_out_norm, m_w_out, m_mix_norm_post, m_mlp_norm_pre, m_w_up, m_w_down, m_mlp_norm_post, m_ple_norm_pre, m_w_ple_gate, m_w_ple_proj, m_ple_norm_post, v_mix_norm_pre, v_w_in, v_lam_re, v_lam_im, v_log_dt, v_ssm_b_re, v_ssm_b_im, v_ssm_c_re, v_ssm_c_im, v_ssm_d, v_w_glu, v_b_glu, v_attn_out_norm, v_ssm_out_norm, v_w_out, v_mix_norm_post, v_mlp_norm_pre, v_w_up, v_w_down, v_mlp_norm_post, v_ple_norm_pre, v_w_ple_gate, v_w_ple_proj, v_ple_norm_post):
    weights = dict(mix_norm_pre=mix_norm_pre, w_in=w_in, lam_re=lam_re, lam_im=lam_im, log_dt=log_dt, ssm_b_re=ssm_b_re, ssm_b_im=ssm_b_im, ssm_c_re=ssm_c_re, ssm_c_im=ssm_c_im, ssm_d=ssm_d, w_glu=w_glu, b_glu=b_glu, attn_out_norm=attn_out_norm, ssm_out_norm=ssm_out_norm, w_out=w_out, mix_norm_post=mix_norm_post, mlp_norm_pre=mlp_norm_pre, w_up=w_up, w_down=w_down, mlp_norm_post=mlp_norm_post, ple_norm_pre=ple_norm_pre, w_ple_gate=w_ple_gate, w_ple_proj=w_ple_proj, ple_norm_post=ple_norm_post)
    mom_m = dict(mix_norm_pre=m_mix_norm_pre, w_in=m_w_in, lam_re=m_lam_re, lam_im=m_lam_im, log_dt=m_log_dt, ssm_b_re=m_ssm_b_re, ssm_b_im=m_ssm_b_im, ssm_c_re=m_ssm_c_re, ssm_c_im=m_ssm_c_im, ssm_d=m_ssm_d, w_glu=m_w_glu, b_glu=m_b_glu, attn_out_norm=m_attn_out_norm, ssm_out_norm=m_ssm_out_norm, w_out=m_w_out, mix_norm_post=m_mix_norm_post, mlp_norm_pre=m_mlp_norm_pre, w_up=m_w_up, w_down=m_w_down, mlp_norm_post=m_mlp_norm_post, ple_norm_pre=m_ple_norm_pre, w_ple_gate=m_w_ple_gate, w_ple_proj=m_w_ple_proj, ple_norm_post=m_ple_norm_post)
    mom_v = dict(mix_norm_pre=v_mix_norm_pre, w_in=v_w_in, lam_re=v_lam_re, lam_im=v_lam_im, log_dt=v_log_dt, ssm_b_re=v_ssm_b_re, ssm_b_im=v_ssm_b_im, ssm_c_re=v_ssm_c_re, ssm_c_im=v_ssm_c_im, ssm_d=v_ssm_d, w_glu=v_w_glu, b_glu=v_b_glu, attn_out_norm=v_attn_out_norm, ssm_out_norm=v_ssm_out_norm, w_out=v_w_out, mix_norm_post=v_mix_norm_post, mlp_norm_pre=v_mlp_norm_pre, w_up=v_w_up, w_down=v_w_down, mlp_norm_post=v_mlp_norm_post, ple_norm_pre=v_ple_norm_pre, w_ple_gate=v_w_ple_gate, w_ple_proj=v_w_ple_proj, ple_norm_post=v_ple_norm_post)
    order = list(weights)
    big = ["w_in", "w_glu", "w_out", "w_up", "w_down", "w_ple_gate", "w_ple_proj"]
    col_sharded = {"w_in", "w_up", "w_ple_proj"}
    small = [n for n in order if n not in big]

    _, S, D = x.shape
    xs = x[0]
    tgt = loss_target[0]
    AW = attn_out_norm.shape[1]
    SW = ssm_d.shape[1]
    H = AW // HEAD_DIM
    G = SW // SSM_GROUP
    nslab = G // SLAB_GROUPS
    P_, C_ = SSM_STATE, SSM_GROUP

    shard = {n: weights[n][0].astype(BF16) for n in big}
    W, WT = {}, {}

    def arrived(names, gathered):
        for n, g in zip(names, gathered):
            W[n] = g if n in col_sharded else g.reshape(1, N_DEV * g.shape[1], g.shape[2])

    def transposed(g):
        return jnp.swapaxes(g, 1, 2).reshape(1, g.shape[0] * g.shape[2], g.shape[1])

    arrived(["w_in"], all_gather([shard["w_in"]], name="gather_w_in"))
    WT["w_in"] = transposed(W["w_in"])
    early, mid, late = ["w_glu", "w_out"], ["w_up"], ["w_down", "w_ple_gate", "w_ple_proj"]
    gather_early = gather_start([shard[n] for n in early], W["w_in"], name="gather_early_start")
    gather_mid = gather_start([shard[n] for n in mid], gather_early[-1], name="gather_mid_start")
    gather_late = gather_start([shard[n] for n in late], gather_mid[-1], name="gather_late_start")

    g1, g2, g3, g4, g5, g6 = (weights[n] for n in ("mix_norm_pre", "mix_norm_post", "mlp_norm_pre",
                                                      "mlp_norm_post", "ple_norm_pre", "ple_norm_post"))
    ga, gs = attn_out_norm, ssm_out_norm
    (hn1,) = rowwise(lambda a, g: (_rms(a, g),), [xs], [g1], [(D, BF16)], deps=(gather_late[-1],), name="norm_in")
    (proj,) = mm_nn(hn1, W["w_in"], [F32], name="proj_in")
    attn, lse = attn_fwd(proj, H, name="attn_fwd")
    gather_early = gather_forward(gather_early, attn, name="gather_early_forward")
    (mix_a,) = rowwise(lambda a, g: (_rms(a, g),), [attn], [ga], [(AW, BF16)], deps=(gather_early[-1],),
                       name="attn_norm")
    arrived(early, gather_finish(gather_early, mix_a, name="gather_early_finish"))

    a_r, a_i, bb_r, bb_i = _discretise(lam_re[0], lam_im[0], log_dt[0], ssm_b_re[0], ssm_b_im[0])
    ssm_consts = (_block_diag(bb_r.swapaxes(1, 2), nslab).astype(BF16), _block_diag(bb_i.swapaxes(1, 2), nslab).astype(BF16),
                  a_r.reshape(nslab, 1, SLAB_STATES), a_i.reshape(nslab, 1, SLAB_STATES),
                  _block_diag(ssm_c_re[0].swapaxes(1, 2), nslab).astype(BF16),
                  _block_diag(ssm_c_im[0].swapaxes(1, 2), nslab).astype(BF16), ssm_d)
    u_seg = _to_segments(proj[:, 3 * AW:]).astype(BF16)
    y_pre = ssm_fwd(u_seg, *ssm_consts, name="ssm_fwd")
    gather_mid = gather_forward(gather_mid, y_pre, name="gather_mid_forward")
    (yg,) = rowwise(lambda a: (_gelu(a),), [y_pre], [], [(SW, BF16)], deps=(gather_mid[-1],), name="ssm_gelu")
    (gl1,) = mm_nn(yg, W["w_glu"], [BF16], epi=lambda acc, b: (acc + b,), bias=b_glu, name="glu_gate")
    (mix_s,) = rowwise(lambda yp, gl, g: (_rms(_gelu(yp) * _sigmoid(gl), g),), [y_pre, gl1], [gs], [(SW, BF16)],
                       name="ssm_glu_norm")
    mixed = jnp.concatenate([mix_a, _from_segments(mix_s)], axis=1)
    (mo,) = mm_nn(mixed, W["w_out"], [BF16], name="mix_out")

    def resid_norm(h, t, gpost, gpre):
        hh = h + _rms(t, gpost)
        return hh, _rms(hh, gpre)

    h1, hn2 = rowwise(resid_norm, [xs, mo], [g2, g3], [(D, F32), (D, BF16)], name="resid_mix")
    arrived(mid, gather_finish(gather_mid, hn2, name="gather_mid_finish"))
    gather_late = gather_forward(gather_late, W["w_up"], name="gather_late_forward")
    WT["w_up"] = transposed(W["w_up"])

    def relu2(acc):
        r = jnp.maximum(acc, 0.0)
        return acc, r * r

    up, act = mm_nn(hn2, W["w_up"], [BF16, BF16], epi=relu2, deps=(gather_late[-1],), tm=1024, tn=1024, name="mlp_up")
    arrived(late, gather_finish(gather_late, act, name="gather_late_finish"))
    (ff,) = mm_nn(act, W["w_down"], [BF16], name="mlp_down")
    h2, hn3 = rowwise(resid_norm, [h1, ff], [g4, g5], [(D, F32), (D, BF16)], name="resid_mlp")
    (gl2,) = mm_nn(hn3, W["w_ple_gate"], [BF16], name="ple_gate")
    pb = p[0, 0].astype(BF16)
    (emb,) = mm_nn(pb, W["w_ple_proj"], [BF16], name="ple_proj")

    def head(h, gl, e, t, g):
        sg = _sigmoid(gl)
        ge = sg * e
        err = h + _rms(ge, g) - t
        dh = err * (1.0 / D)
        dge, dg = _rms_bwd(dh, ge, g)
        return dh, dge * e * sg * (1.0 - sg), dge * sg, jnp.sum(err * err, axis=0, keepdims=True), dg

    dh3, dgl2, demb, loss_part, dg6 = rowwise(head, [h2, gl2, emb, tgt], [g6], [(D, F32), (D, BF16), (D, BF16)],
                                             [D, D], name="ple_loss_head")
    loss = lax.psum(0.5 / D * jnp.sum(loss_part), ("x", "y", "c"))

    x_i, y_i, c_i = _place()
    place = jnp.stack([c_i, 2 * x_i + y_i]).astype(jnp.int32)
    grads, out_g, out_d, out_m, out_v = {}, {}, {}, {}, {}

    def to_sibling(names, after, tag):
        chunks = []
        for n in names:
            g = grads[n]
            g = g if n in col_sharded else g.reshape(N_DEV, g.shape[1] // N_DEV, g.shape[2])
            chunks.append(g.reshape(4, 2, g.shape[1], g.shape[2]))
        return chunks, exchange_start(chunks, [(4,) + g.shape[2:] for g in chunks], _core_copies, len(chunks), after,
                                      name=f"grads_to_sibling_{tag}")

    def to_chips(names, sent, after, tag):
        chunks, state = sent
        sums = []
        for n, g, r in zip(names, chunks, exchange_wait(state, after, name=f"grads_from_sibling_{tag}")):
            k, nn = g.shape[2], g.shape[3]
            kb = k // _tile(k, 512, 16)

            def mine(i, s, kb=kb):
                return 2 * (i // kb) + s[0], i % kb

            (s,) = _blocked(lambda a, b: (a.astype(F32) + b.astype(F32),),
                            [(g.reshape(N_DEV, k, nn), mine), r.reshape(4 * k, nn)],
                            [((4 * k, nn), BF16)], place=place, tr=k // kb, name=f"chip_sum_{n}")
            sums.append(s.reshape(4, k, nn))
        return sums, exchange_start(sums, [(3,) + s.shape[1:] for s in sums], _chip_copies, 3 * len(sums), sums[-1],
                                    name=f"grads_to_chips_{tag}")

    def update(w_, m_, v_, own, r0, r1, r2):
        g = own.astype(F32) + r0.astype(F32) + r1.astype(F32) + r2.astype(F32)
        return (g,) + _adamw(w_, g, m_, v_)

    def finish(names, sent, after, tag):
        sums, state = sent
        for n, s, r in zip(names, sums, exchange_wait(state, after, name=f"grads_from_chips_{tag}")):
            shp = weights[n].shape
            res = _blocked(update, [weights[n][0], mom_m[n][0], mom_v[n][0], (s, lambda i, p_: (p_[1], i)),
                                    (r, lambda i, p_: (0, i)), (r, lambda i, p_: (1, i)), (r, lambda i, p_: (2, i))],
                           [(shp[1:], F32)] * 4, place=place, name=f"adamw_{n}")
            out_g[n], out_d[n], out_m[n], out_v[n] = (t.reshape(shp) for t in res)
        return out_v[names[-1]]

    grads["w_ple_proj"] = mm_tn(pb, demb, N_DEV, name="grad_w_ple_proj")
    dhn3 = mm_nt(dgl2, W["w_ple_gate"], BF16, name="back_ple_gate")
    grads["w_ple_gate"] = mm_tn(hn3, dgl2, 1, name="grad_w_ple_gate")

    def back_resid(dh, dhn, h, t, gpre, gpost):
        d1, dgpre = _rms_bwd(dhn, h, gpre)
        dhh = dh + d1
        dt, dgpost = _rms_bwd(dhh, t, gpost)
        return dhh, dt, dgpre, dgpost

    dh2, dff, dg5, dg4 = rowwise(back_resid, [dh3, dhn3, h2, ff], [g5, g4], [(D, F32), (D, BF16)], [D, D],
                                 name="back_resid_mlp")
    dup = mm_nt(dff, W["w_down"], BF16, epi=lambda acc, u_: (acc * 2.0 * jnp.maximum(u_.astype(F32), 0.0),),
                extra=up, name="back_mlp_down")
    grads["w_down"] = mm_tn(act, dff, 1, name="grad_w_down")
    group_a = ["w_ple_proj", "w_ple_gate", "w_down"]
    sent_a = to_sibling(group_a, grads["w_down"], "a")
    (dhn2,) = mm_nn(dup, WT["w_up"], [BF16], deps=(sent_a[1][-1],), name="back_mlp_up")
    sent_a = to_chips(group_a, sent_a, dhn2, "a")
    grads["w_up"] = mm_tn(hn2, dup, N_DEV, deps=(sent_a[1][-1],), name="grad_w_up")
    dh1, dmo, dg3, dg2 = rowwise(back_resid, [dh2, dhn2, h1, mo], [g3, g2], [(D, F32), (D, BF16)], [D, D],
                                 name="back_resid_mix")
    dmixed = mm_nt(dmo, W["w_out"], BF16, name="back_mix_out")
    grads["w_out"] = mm_tn(mixed, dmo, 1, name="grad_w_out")

    def back_glu(dm, yp, gl, g):
        ygf = _gelu(yp)
        sg = _sigmoid(gl)
        dssm, dg = _rms_bwd(dm, ygf * sg, g)
        dgl = dssm * ygf * sg * (1.0 - sg)
        return dgl, dssm * sg, dg, jnp.sum(dgl, axis=0, keepdims=True)

    dgl1, dyg_direct, dgs, db_glu = rowwise(back_glu, [_to_segments(dmixed[:, AW:]), y_pre, gl1], [gs],
                                            [(SW, BF16), (SW, F32)], [SW, SW], name="back_glu")
    dyg_gate = mm_nt(dgl1, W["w_glu"], BF16, name="back_glu_gate")
    grads["w_glu"] = mm_tn(yg, dgl1, 1, name="grad_w_glu")
    group_b = ["w_up", "w_out", "w_glu"]
    sent_b = to_sibling(group_b, grads["w_glu"], "b")
    done_a = finish(group_a, sent_a, sent_b[1][-1], "a")

    def back_gelu(d1, d2, yp, u_):
        dy = (d1 + d2) * _gelu_grad(yp)
        return dy, jnp.sum(dy * u_.astype(F32), axis=0, keepdims=True)

    dy_pre, d_skip = rowwise(back_gelu, [dyg_direct, dyg_gate, y_pre, u_seg], [], [(SW, F32)], [SW], deps=(done_a,),
                             name="back_gelu")
    du_seg, dbb_r, dbb_i, dcb_r, dcb_i, da_r, da_i = ssm_bwd(u_seg, dy_pre, *ssm_consts, name="ssm_bwd")
    sent_b = to_chips(group_b, sent_b, du_seg, "b")

    def back_attn_norm(dm, a, g):
        da, dg = _rms_bwd(dm, a, g)
        prod = da * a
        delta = jnp.concatenate(
            [jnp.broadcast_to(jnp.sum(prod[:, h * HEAD_DIM:(h + 1) * HEAD_DIM], axis=-1, keepdims=True),
                              (prod.shape[0], HEAD_DIM)) for h in range(H)], axis=1)
        return da, delta, dg

    dattn, delta, dga = rowwise(back_attn_norm, [(dmixed, AW, 0), attn], [ga], [(AW, F32), (AW, F32)], [AW],
                                deps=(sent_b[1][-1],), name="back_attn_norm")
    dq, dk, dv = attn_bwd(proj, dattn, lse, delta, H, name="attn_bwd")
    dproj = jnp.concatenate([dq, dk, dv, _from_segments(du_seg)], axis=1)
    (dhn1,) = mm_nn(dproj, WT["w_in"], [BF16], name="back_proj_in")

    def back_in(dh, dhn, a, g):
        d1, dg = _rms_bwd(dhn, a, g)
        return dh + d1, dg

    grad_x, dg1 = rowwise(back_in, [dh1, dhn1, xs], [g1], [(D, F32)], [D], name="back_norm_in")

    cot = dict(
        mix_norm_pre=dg1, mix_norm_post=dg2, mlp_norm_pre=dg3, mlp_norm_post=dg4, ple_norm_pre=dg5, ple_norm_post=dg6,
        attn_out_norm=dga, ssm_out_norm=dgs, b_glu=db_glu, ssm_d=d_skip,
        ssm_c_re=_block_diag_part(dcb_r, P_, C_).swapaxes(1, 2), ssm_c_im=_block_diag_part(dcb_i, P_, C_).swapaxes(1, 2),
        a_r=da_r.reshape(G, P_), a_i=da_i.reshape(G, P_),
        bb_r=_block_diag_part(dbb_r, C_, P_).swapaxes(1, 2), bb_i=_block_diag_part(dbb_i, C_, P_).swapaxes(1, 2))
    names = list(cot)
    flat = jnp.concatenate([cot[n].reshape(-1) for n in names])
    total = flat.shape[0]
    rows_ = -(-total // (LANES * 16)) * 16
    flat = jnp.pad(flat, (0, rows_ * LANES - total)).reshape(rows_, LANES)
    gather_small = gather_start([flat], flat, name="gather_small_start")
    grads["w_in"] = mm_tn(hn1, dproj, N_DEV, deps=(gather_small[-1],), tko=2048, name="grad_w_in")
    group_c = ["w_in"]
    sent_c = to_sibling(group_c, grads["w_in"], "c")
    done_b = finish(group_b, sent_b, sent_c[1][-1], "b")
    sent_c = to_chips(group_c, sent_c, done_b, "c")
    gather_small = gather_forward(gather_small, sent_c[1][-1], name="gather_small_forward")
    (every,) = gather_finish(gather_small, gather_small[-1], name="gather_small_finish")
    (summed,) = _blocked(lambda *t: (functools.reduce(lambda a, b: a + b, t),),
                         [(every, functools.partial(lambda i, p_, j: (j, i), j=j)) for j in range(N_DEV)],
                         [((rows_, LANES), F32)], name="sum_small_grads")
    summed = summed.reshape(-1)
    red, off = {}, 0
    for n in names:
        sz = cot[n].size
        red[n] = summed[off:off + sz].reshape(cot[n].shape)
        off += sz
    _, pull = jax.vjp(_discretise, lam_re[0], lam_im[0], log_dt[0], ssm_b_re[0], ssm_b_im[0])
    d_lre, d_lim, d_ldt, d_bre, d_bim = pull((red["a_r"], red["a_i"], red["bb_r"], red["bb_i"]))
    red.update(lam_re=d_lre, lam_im=d_lim, log_dt=d_ldt, ssm_b_re=d_bre, ssm_b_im=d_bim)

    def pack(d):
        t = jnp.concatenate([d[n].reshape(-1) for n in small])
        r_ = -(-t.shape[0] // (LANES * 16)) * 16
        return jnp.pad(t, (0, r_ * LANES - t.shape[0])).reshape(r_, LANES)

    sw, sg_, sm, sv = pack(weights), pack(red), pack(mom_m), pack(mom_v)
    sd, snm, snv = _blocked(lambda w_, g_, m_, v_: _adamw(w_, g_, m_, v_), [sw, sg_, sm, sv],
                            [(sw.shape, F32)] * 3, name="adamw_small")
    finish(group_c, sent_c, snv, "c")
    off = 0
    for n in small:
        sz = weights[n].size
        shp = weights[n].shape
        out_g[n] = red[n].reshape(shp)
        out_d[n] = sd.reshape(-1)[off:off + sz].reshape(shp)
        out_m[n] = snm.reshape(-1)[off:off + sz].reshape(shp)
        out_v[n] = snv.reshape(-1)[off:off + sz].reshape(shp)
        off += sz

    return (loss, grad_x[None], *[out_g[n] for n in order], *[out_d[n] for n in order],
            *[out_m[n] for n in order], *[out_v[n] for n in order])
```

```python
import functools
import math

import jax
import jax.numpy as jnp
from jax import lax
from jax.experimental import pallas as pl
from jax.experimental.pallas import tpu as pltpu

F32 = jnp.float32
BF16 = jnp.bfloat16
MESH = pl.DeviceIdType.MESH

N_DEV = 8
LANES = 128
SUBLANES = 8
VMEM_LIMIT = 48 * 1024 * 1024
VMEM_LIMIT_SCAN = 60 * 1024 * 1024

HEAD_DIM = 128
BLK = 128
DILATIONS = (1, 4, 16)
SSM_GROUP = 16
SSM_STATE = 64
SLAB_GROUPS = LANES // SSM_GROUP
SLAB_STATES = SLAB_GROUPS * SSM_STATE
SEGMENTS = SUBLANES
SCAN_UNROLL = 4
RMS_EPS = 1e-6
NEG_INF = -1e30

ADAM_LR = 0.001
ADAM_B1 = 0.9
ADAM_B2 = 0.999
ADAM_EPS = 1e-08
ADAM_WD = 0.01
ADAM_STEP = 10


def _tile(n, pref, unit=LANES):
    if n <= pref:
        return n
    t = (pref // unit) * unit
    while t > unit and n % t:
        t -= unit
    assert n % t == 0, (n, pref, unit)
    return t


def _params(sem=None, vmem=VMEM_LIMIT):
    return pltpu.CompilerParams(dimension_semantics=sem, vmem_limit_bytes=vmem)


_NN = (((1,), (0,)), ((), ()))
_NT = (((1,), (1,)), ((), ()))
_TN = (((0,), (0,)), ((), ()))


_ANY = pl.BlockSpec(memory_space=pl.ANY)


def _mm_call(dims, nk, n_extra, n_dep, n_out, epi, **kw):
    first_out = 2 + n_extra + n_dep
    kw["in_specs"] = list(kw["in_specs"]) + [_ANY] * n_dep

    def single(*refs):
        extra = refs[2:2 + n_extra]
        res = epi(lax.dot_general(refs[0][...], refs[1][...], dims, preferred_element_type=F32),
                  *[e[...] for e in extra])
        for o, r in zip(refs[first_out:first_out + n_out], res):
            o[...] = r.astype(o.dtype)

    if nk == 1:
        kw["scratch_shapes"] = []
        return pl.pallas_call(single, **kw)

    def body(*refs):
        a_ref, b_ref = refs[0], refs[1]
        extra = refs[2:2 + n_extra]
        outs = refs[first_out:first_out + n_out]
        acc = refs[-1]
        k = pl.program_id(2)

        @pl.when(k == 0)
        def _():
            acc[...] = jnp.zeros_like(acc)

        acc[...] += lax.dot_general(a_ref[...], b_ref[...], dims, preferred_element_type=F32)

        @pl.when(k == nk - 1)
        def _():
            res = epi(acc[...], *[e[...] for e in extra])
            for o, r in zip(outs, res):
                o[...] = r.astype(o.dtype)

    return pl.pallas_call(body, **kw)


def _identity_epi(acc):
    return (acc,)


def mm_nn(a, w, out_dtypes, *, name, epi=_identity_epi, bias=None, deps=(), tm=2048, tn=512, tk=2048):
    M, K = a.shape
    J, K2, n = w.shape
    assert K == K2
    tm, tn, tk = _tile(M, tm, 16), _tile(n, tn), _tile(K, tk)
    npj = n // tn
    nk = K // tk
    in_specs = [pl.BlockSpec((tm, tk), lambda i, j, k: (i, k)),
                pl.BlockSpec((None, tk, tn), lambda i, j, k: (j // npj, k, j % npj))]
    args = [a, w]
    if bias is not None:
        in_specs.append(pl.BlockSpec((1, tn), lambda i, j, k: (0, j)))
        args.append(bias)
    return _mm_call(
        _NN, nk, len(args) - 2, len(deps), len(out_dtypes), epi,
        out_shape=[jax.ShapeDtypeStruct((M, J * n), d) for d in out_dtypes],
        grid=(M // tm, J * npj, nk), in_specs=in_specs,
        out_specs=[pl.BlockSpec((tm, tn), lambda i, j, k: (i, j)) for _ in out_dtypes],
        scratch_shapes=[pltpu.VMEM((tm, tn), F32)],
        compiler_params=_params(("parallel", "parallel", "arbitrary")), name=name)(*args, *deps)


def mm_nt(a, w, out_dtype, *, name, epi=_identity_epi, extra=None, tm=2048, tko=512, tnr=2048):
    M, N = a.shape
    J, K, n = w.shape
    assert N == J * n
    tm, tko, tnr = _tile(M, tm, 16), _tile(K, tko), _tile(n, tnr)
    npj = n // tnr
    nk = N // tnr
    in_specs = [pl.BlockSpec((tm, tnr), lambda i, j, k: (i, k)),
                pl.BlockSpec((None, tko, tnr), lambda i, j, k: (k // npj, j, k % npj))]
    args = [a, w]
    if extra is not None:
        in_specs.append(pl.BlockSpec((tm, tko), lambda i, j, k: (i, j)))
        args.append(extra)
    return _mm_call(
        _NT, nk, len(args) - 2, 0, 1, epi,
        out_shape=[jax.ShapeDtypeStruct((M, K), out_dtype)],
        grid=(M // tm, K // tko, nk), in_specs=in_specs,
        out_specs=[pl.BlockSpec((tm, tko), lambda i, j, k: (i, j))],
        scratch_shapes=[pltpu.VMEM((tm, tko), F32)],
        compiler_params=_params(("parallel", "parallel", "arbitrary")), name=name)(*args)[0]


def mm_tn(a, b, J, *, name, deps=(), tko=1024, tn=1024, ts=2048):
    S, K = a.shape
    S2, N = b.shape
    assert S == S2 and N % J == 0
    n = N // J
    tko, tn, ts = _tile(K, tko), _tile(n, tn), _tile(S, ts)
    npj = n // tn
    nk = S // ts
    return _mm_call(
        _TN, nk, 0, len(deps), 1, _identity_epi,
        out_shape=[jax.ShapeDtypeStruct((J, K, n), BF16)],
        grid=(K // tko, J * npj, nk),
        in_specs=[pl.BlockSpec((ts, tko), lambda i, j, k: (k, i)),
                  pl.BlockSpec((ts, tn), lambda i, j, k: (k, j))],
        out_specs=[pl.BlockSpec((None, tko, tn), lambda i, j, k: (j // npj, i, j % npj))],
        scratch_shapes=[pltpu.VMEM((tko, tn), F32)],
        compiler_params=_params(("parallel", "parallel", "arbitrary")), name=name)(a, b, *deps)[0]


def rowwise(fn, rows, vecs, outs, accs=(), *, name, deps=(), ts=256):
    rows = [r if isinstance(r, tuple) else (r, r.shape[1], 0) for r in rows]
    S = rows[0][0].shape[0]
    ts = _tile(S, ts, 16)
    nr, nv, no, nd = len(rows), len(vecs), len(outs), len(deps)

    def body(*refs):
        r, v = refs[:nr], refs[nr:nr + nv]
        o, a = refs[nr + nv + nd:nr + nv + nd + no], refs[nr + nv + nd + no:]
        res = fn(*[t[...].astype(F32) for t in r], *[t[...] for t in v])
        for ref, val in zip(o, res[:no]):
            ref[...] = val.astype(ref.dtype)
        if a:
            @pl.when(pl.program_id(0) == 0)
            def _():
                for ref in a:
                    ref[...] = jnp.zeros_like(ref)

            for ref, val in zip(a, res[no:]):
                ref[...] += val

    in_specs = [pl.BlockSpec((ts, w), functools.partial(lambda i, cb: (i, cb), cb=cb)) for _, w, cb in rows]
    in_specs += [pl.BlockSpec(v.shape, lambda i: (0, 0)) for v in vecs] + [_ANY] * nd
    out_shape = [jax.ShapeDtypeStruct((S, w), d) for w, d in outs]
    out_shape += [jax.ShapeDtypeStruct((1, w), F32) for w in accs]
    out_specs = [pl.BlockSpec((ts, w), lambda i: (i, 0)) for w, _ in outs]
    out_specs += [pl.BlockSpec((1, w), lambda i: (0, 0)) for w in accs]
    return pl.pallas_call(body, out_shape=out_shape, grid=(S // ts,), in_specs=in_specs, out_specs=out_specs,
                          compiler_params=_params(("arbitrary",)), name=name)(*[r[0] for r in rows], *vecs, *deps)


def _rms(x, g):
    r = lax.rsqrt(jnp.mean(x * x, axis=-1, keepdims=True) + RMS_EPS)
    return x * r * g


def _rms_bwd(dy, x, g):
    r = lax.rsqrt(jnp.mean(x * x, axis=-1, keepdims=True) + RMS_EPS)
    xh = x * r
    dxh = dy * g
    dx = r * (dxh - xh * jnp.mean(dxh * xh, axis=-1, keepdims=True))
    return dx, jnp.sum(dy * xh, axis=0, keepdims=True)


def _sigmoid(x):
    return pl.reciprocal(1.0 + jnp.exp(-x), approx=True)


_GELU_C = math.sqrt(2.0 / math.pi)


def _gelu(x):
    return 0.5 * x * (1.0 + jnp.tanh(_GELU_C * (x + 0.044715 * x * x * x)))


def _gelu_grad(x):
    t = jnp.tanh(_GELU_C * (x + 0.044715 * x * x * x))
    return 0.5 * (1.0 + t) + 0.5 * x * (1.0 - t * t) * _GELU_C * (1.0 + 3.0 * 0.044715 * x * x)


ATTN_INTERLEAVE = 8
KEY_PAD = BLK * max(DILATIONS)


def _key_mask(n):
    ii = lax.broadcasted_iota(jnp.int32, (BLK, 2 * BLK), 0)
    jj = lax.broadcasted_iota(jnp.int32, (BLK, 2 * BLK), 1)
    return ((jj < BLK) & (jj >= ii) & (n > 0)) | ((jj >= BLK) & (jj - BLK <= ii))


def _units(d, nblk):
    nb = nblk // d
    if nb == 2:
        def unit(idx):
            ii = lax.broadcasted_iota(jnp.int32, (2 * BLK, 2 * BLK), 0)
            jj = lax.broadcasted_iota(jnp.int32, (2 * BLK, 2 * BLK), 1)
            return pl.ds(idx, 2 * BLK, stride=d), pl.ds(KEY_PAD + idx, 2 * BLK, stride=d), (jj <= ii) & (ii - jj <= BLK)
        return d, max(1, ATTN_INTERLEAVE // 2), unit

    def unit(idx):
        r, n = idx // nb, idx % nb
        cur = r + n * (BLK * d)
        keys = cur + (KEY_PAD - BLK * d)
        if d == 1:
            return pl.ds(pl.multiple_of(cur, BLK), BLK), pl.ds(pl.multiple_of(keys, BLK), 2 * BLK), _key_mask(n)
        return pl.ds(cur, BLK, stride=d), pl.ds(keys, 2 * BLK, stride=d), _key_mask(n)
    return nblk, ATTN_INTERLEAVE, unit


def _pad_keys(dst, src):
    dst[pl.ds(0, KEY_PAD), :] = jnp.zeros((KEY_PAD, dst.shape[1]), F32)

    def copy(c, carry):
        dst[pl.ds(pl.multiple_of(KEY_PAD + c * BLK, BLK), BLK), :] = src[pl.ds(pl.multiple_of(c * BLK, BLK), BLK), :]
        return carry

    lax.fori_loop(0, src.shape[0] // BLK, copy, 0)


def attn_fwd(proj, n_heads, *, name):
    S, WP = proj.shape
    assert S % (BLK * max(DILATIONS)) == 0
    nblk = S // BLK
    AW = n_heads * HEAD_DIM
    scale = 1.0 / math.sqrt(HEAD_DIM)

    def body(q_ref, k_ref, v_ref, o_ref, l_ref, acc, mrun, lrun, kp, vp):
        _pad_keys(kp, k_ref)
        _pad_keys(vp, v_ref)
        for first, d in zip((True, False, False), reversed(DILATIONS)):
            n_units, per_step, unit = _units(d, nblk)

            def step(it, carry, first=first, n_units=n_units, per_step=per_step, unit=unit):
                units = [unit(it + j * (n_units // per_step)) for j in range(per_step)]
                ss = [lax.dot_general(q_ref[cur, :].astype(BF16), kp[keys, :].astype(BF16), _NT,
                                      preferred_element_type=F32) * scale for cur, keys, _ in units]
                for j, (cur, keys, mask) in enumerate(units):
                    s = jnp.where(mask, ss[j], NEG_INF)
                    m = jnp.max(s, axis=-1, keepdims=True)
                    p = jnp.exp(s - m)
                    l = jnp.sum(p, axis=-1, keepdims=True)
                    o = jnp.dot(p.astype(BF16), vp[keys, :].astype(BF16), preferred_element_type=F32)
                    m = jnp.broadcast_to(m, o.shape)
                    l = jnp.broadcast_to(l, o.shape)
                    if first:
                        acc[cur, :], mrun[cur, :], lrun[cur, :] = o, m, l
                    else:
                        m_old = mrun[cur, :]
                        m_new = jnp.maximum(m_old, m)
                        w_old, w_blk = jnp.exp(m_old - m_new), jnp.exp(m - m_new)
                        acc[cur, :] = w_old * acc[cur, :] + w_blk * o
                        lrun[cur, :] = w_old * lrun[cur, :] + w_blk * l
                        mrun[cur, :] = m_new
                return carry

            lax.fori_loop(0, n_units // per_step, step, 0)

        def finish(c, carry):
            r = pl.ds(pl.multiple_of(c * BLK, BLK), BLK)
            o_ref[r, :] = acc[r, :] / lrun[r, :]
            l_ref[r, :] = mrun[r, :] + jnp.log(lrun[r, :])
            return carry

        lax.fori_loop(0, nblk, finish, 0)

    def col(off):
        return pl.BlockSpec((S, HEAD_DIM), lambda h: (0, off + h))

    ospec = pl.BlockSpec((S, HEAD_DIM), lambda h: (0, h))
    return pl.pallas_call(
        body, out_shape=[jax.ShapeDtypeStruct((S, AW), F32)] * 2, grid=(n_heads,),
        in_specs=[col(0), col(n_heads), col(2 * n_heads)], out_specs=[ospec, ospec],
        scratch_shapes=[pltpu.VMEM((S, HEAD_DIM), F32)] * 3 + [pltpu.VMEM((KEY_PAD + S, HEAD_DIM), F32)] * 2,
        compiler_params=_params(("parallel",)), name=name)(proj, proj, proj)


def attn_bwd(proj, do, lse, delta, n_heads, *, name):
    S, WP = proj.shape
    nblk = S // BLK
    AW = n_heads * HEAD_DIM
    scale = 1.0 / math.sqrt(HEAD_DIM)

    def body(q_ref, k_ref, v_ref, do_ref, l_ref, dl_ref, dq_ref, dk_ref, dv_ref, dq_sc, dk_sc, dv_sc, kp, vp):
        _pad_keys(kp, k_ref)
        _pad_keys(vp, v_ref)
        dq_sc[...] = jnp.zeros_like(dq_sc)
        dk_sc[...] = jnp.zeros_like(dk_sc)
        dv_sc[...] = jnp.zeros_like(dv_sc)
        for d in DILATIONS:
            n_units, per_step, unit = _units(d, nblk)

            def step(it, carry, n_units=n_units, per_step=per_step, unit=unit):
                for cur, keys, mask in [unit(it + j * (n_units // per_step)) for j in range(per_step)]:
                    q = q_ref[cur, :].astype(BF16)
                    g = do_ref[cur, :].astype(BF16)
                    kb = kp[keys, :].astype(BF16)
                    vb = vp[keys, :].astype(BF16)
                    s = lax.dot_general(q, kb, _NT, preferred_element_type=F32) * scale
                    p = jnp.where(mask, jnp.exp(s - l_ref[cur, :][:, :1]), 0.0)
                    dp = lax.dot_general(g, vb, _NT, preferred_element_type=F32)
                    ds = (p * (dp - dl_ref[cur, :][:, :1]) * scale).astype(BF16)
                    dq_sc[cur, :] += jnp.dot(ds, kb, preferred_element_type=F32)
                    dk_sc[keys, :] += lax.dot_general(ds, q, _TN, preferred_element_type=F32)
                    dv_sc[keys, :] += lax.dot_general(p.astype(BF16), g, _TN, preferred_element_type=F32)
                return carry

            lax.fori_loop(0, n_units // per_step, step, 0)
        rows = pl.ds(KEY_PAD, S)
        dq_ref[...] = dq_sc[...].astype(BF16)
        dk_ref[...] = dk_sc[rows, :].astype(BF16)
        dv_ref[...] = dv_sc[rows, :].astype(BF16)

    def col(off):
        return pl.BlockSpec((S, HEAD_DIM), lambda h: (0, off + h))

    ospec = pl.BlockSpec((S, HEAD_DIM), lambda h: (0, h))
    return pl.pallas_call(
        body, out_shape=[jax.ShapeDtypeStruct((S, AW), BF16)] * 3, grid=(n_heads,),
        in_specs=[col(0), col(n_heads), col(2 * n_heads), ospec, ospec, ospec], out_specs=[ospec] * 3,
        scratch_shapes=[pltpu.VMEM((S, HEAD_DIM), F32)] + [pltpu.VMEM((KEY_PAD + S, HEAD_DIM), F32)] * 4,
        compiler_params=_params(("parallel",), VMEM_LIMIT_SCAN), name=name)(proj, proj, proj, do, lse, delta)


def _to_segments(t):
    S, W = t.shape
    return t.reshape(SEGMENTS, S // SEGMENTS, W).swapaxes(0, 1).reshape(S, W)


def _from_segments(t):
    S, W = t.shape
    return t.reshape(S // SEGMENTS, SEGMENTS, W).swapaxes(0, 1).reshape(S, W)


def _cmul(ar, ai, br, bi):
    return ar * br - ai * bi, ar * bi + ai * br


def _power(ar, ai, log2n):
    for _ in range(log2n):
        ar, ai = _cmul(ar, ai, ar, ai)
    return ar, ai


def _shift_rows(x, up):
    row = lax.broadcasted_iota(jnp.int32, x.shape, 0)
    if up:
        return jnp.where(row == SEGMENTS - 1, 0.0, pltpu.roll(x, SEGMENTS - 1, 0))
    return jnp.where(row == 0, 0.0, pltpu.roll(x, 1, 0))


def _segment_carries(er, ei, pr, pi, up):
    cr = jnp.zeros_like(er)
    ci = jnp.zeros_like(ei)
    for _ in range(SEGMENTS - 1):
        tr, ti = _cmul(pr, pi, cr, ci)
        cr, ci = _shift_rows(er + tr, up), _shift_rows(ei + ti, up)
    return cr, ci


def _scan_states(sr, si, ar, ai, T, reverse):
    ns = sr.shape[1]
    ar8 = jnp.broadcast_to(ar, (SEGMENTS, ns))
    ai8 = jnp.broadcast_to(ai, (SEGMENTS, ns))

    def rows(t):
        k = (T - 1 - t) if reverse else t
        return pl.ds(pl.multiple_of(k * SEGMENTS, SEGMENTS), SEGMENTS)

    def advance(t, c):
        tr, ti = _cmul(ar8, ai8, c[0], c[1])
        return tr + sr[rows(t), :], ti + si[rows(t), :]

    def several(step):
        def trip(t, c):
            for j in range(SCAN_UNROLL):
                c = step(t * SCAN_UNROLL + j, c)
            return c
        return trip

    zero = jnp.zeros((SEGMENTS, ns), F32)
    er, ei = lax.fori_loop(0, T // SCAN_UNROLL, several(advance), (zero, zero))
    pr, pi = _power(ar, ai, T.bit_length() - 1)
    cr, ci = _segment_carries(er, ei, jnp.broadcast_to(pr, (SEGMENTS, ns)), jnp.broadcast_to(pi, (SEGMENTS, ns)), reverse)

    def store(t, c):
        nr, ni = advance(t, c)
        sr[rows(t), :] = nr
        si[rows(t), :] = ni
        return nr, ni

    lax.fori_loop(0, T // SCAN_UNROLL, several(store), (cr, ci))
    return cr, ci


def _slab_specs(ns):
    return [pl.BlockSpec((None, LANES, ns), lambda g: (g, 0, 0)),
            pl.BlockSpec((None, LANES, ns), lambda g: (g, 0, 0)),
            pl.BlockSpec((None, 1, ns), lambda g: (g, 0, 0)),
            pl.BlockSpec((None, 1, ns), lambda g: (g, 0, 0)),
            pl.BlockSpec((None, ns, LANES), lambda g: (g, 0, 0)),
            pl.BlockSpec((None, ns, LANES), lambda g: (g, 0, 0)),
            pl.BlockSpec((1, LANES), lambda g: (0, g))]


def _chunks(S):
    rc = _tile(S, 512, 16)
    return rc, S // rc


def ssm_fwd(u, bbr, bbi, ar, ai, cbr, cbi, dsk, *, name):
    S, SW = u.shape
    nslab, _, ns = bbr.shape
    T = S // SEGMENTS
    assert T & (T - 1) == 0
    rc, nc = _chunks(S)

    def body(u_ref, br_ref, bi_ref, ar_ref, ai_ref, cr_ref, ci_ref, d_ref, y_ref, str_ref, sti_ref, sr, si):
        def inputs(c, carry):
            r = pl.ds(pl.multiple_of(c * rc, rc), rc)
            sr[r, :] = jnp.dot(u_ref[r, :], br_ref[...], preferred_element_type=F32)
            si[r, :] = jnp.dot(u_ref[r, :], bi_ref[...], preferred_element_type=F32)
            return carry

        lax.fori_loop(0, nc, inputs, 0)
        _scan_states(sr, si, ar_ref[...], ai_ref[...], T, False)

        def outputs(c, carry):
            r = pl.ds(pl.multiple_of(c * rc, rc), rc)
            srb, sib = sr[r, :].astype(BF16), si[r, :].astype(BF16)
            str_ref[r, :] = srb
            sti_ref[r, :] = sib
            y_ref[r, :] = (jnp.dot(srb, cr_ref[...], preferred_element_type=F32)
                           - jnp.dot(sib, ci_ref[...], preferred_element_type=F32)
                           + d_ref[...] * u_ref[r, :].astype(F32))
            return carry

        lax.fori_loop(0, nc, outputs, 0)

    slab = pl.BlockSpec((S, LANES), lambda g: (0, g))
    states = pl.BlockSpec((S, ns), lambda g: (0, g))
    return pl.pallas_call(
        body, out_shape=[jax.ShapeDtypeStruct((S, SW), F32)] + [jax.ShapeDtypeStruct((S, nslab * ns), BF16)] * 2,
        grid=(nslab,), in_specs=[slab] + _slab_specs(ns), out_specs=[slab, states, states],
        scratch_shapes=[pltpu.VMEM((S, ns), F32)] * 2,
        compiler_params=_params(("parallel",), VMEM_LIMIT_SCAN), name=name)(u, bbr, bbi, ar, ai, cbr, cbi, dsk)


def ssm_bwd(u, dy, st_r, st_i, bbr, bbi, ar, ai, cbr, cbi, dsk, *, name):
    S, SW = u.shape
    nslab, _, ns = bbr.shape
    T = S // SEGMENTS
    rc, nc = _chunks(S)
    pair_rows = 2 * SEGMENTS

    def body(u_ref, dy_ref, sr_ref, si_ref, br_ref, bi_ref, ar_ref, ai_ref, cr_ref, ci_ref, d_ref,
             du_ref, dbr_ref, dbi_ref, dcr_ref, dci_ref, dar_ref, dai_ref, lr, li):
        def inputs(c, carry):
            r = pl.ds(pl.multiple_of(c * rc, rc), rc)
            gb = dy_ref[r, :].astype(BF16)
            lr[r, :] = lax.dot_general(gb, cr_ref[...], _NT, preferred_element_type=F32)
            li[r, :] = -lax.dot_general(gb, ci_ref[...], _NT, preferred_element_type=F32)
            return carry

        lax.fori_loop(0, nc, inputs, 0)
        _scan_states(lr, li, ar_ref[...], -ai_ref[...], T, True)

        def steps(j):
            rows = pl.ds(pl.multiple_of(j * pair_rows, pair_rows), pair_rows)
            tr, ti = sr_ref[rows, :].astype(F32), si_ref[rows, :].astype(F32)
            return tr[:SEGMENTS], tr[SEGMENTS:], ti[:SEGMENTS], ti[SEGMENTS:]

        def pair(j, c):
            acc_r, acc_i, pr, pi = c
            lo_r, hi_r, lo_i, hi_i = steps(j)
            first = pl.ds(pl.multiple_of(j * pair_rows, SEGMENTS), SEGMENTS)
            second = pl.ds(pl.multiple_of(j * pair_rows + SEGMENTS, SEGMENTS), SEGMENTS)
            la_r, la_i, lb_r, lb_i = lr[first, :], li[first, :], lr[second, :], li[second, :]
            return (acc_r + la_r * pr + la_i * pi + lb_r * lo_r + lb_i * lo_i,
                    acc_i - la_r * pi + la_i * pr - lb_r * lo_i + lb_i * lo_r, hi_r, hi_i)

        def pairs(t, c):
            return pair(2 * t + 1, pair(2 * t, c))

        _, end_r, _, end_i = steps(T // 2 - 1)
        zero = jnp.zeros((SEGMENTS, ns), F32)
        acc = lax.fori_loop(0, T // 4, pairs, (zero, zero, _shift_rows(end_r, False), _shift_rows(end_i, False)))
        dar_ref[...] = jnp.sum(acc[0], axis=0, keepdims=True)
        dai_ref[...] = jnp.sum(acc[1], axis=0, keepdims=True)

        dbr_ref[...] = jnp.zeros_like(dbr_ref)
        dbi_ref[...] = jnp.zeros_like(dbi_ref)
        dcr_ref[...] = jnp.zeros_like(dcr_ref)
        dci_ref[...] = jnp.zeros_like(dci_ref)

        def outputs(c, carry):
            r = pl.ds(pl.multiple_of(c * rc, rc), rc)
            ub = u_ref[r, :]
            g = dy_ref[r, :]
            gb = g.astype(BF16)
            lrb = lr[r, :].astype(BF16)
            lib = li[r, :].astype(BF16)
            du_ref[r, :] = (lax.dot_general(lrb, br_ref[...], _NT, preferred_element_type=F32)
                            + lax.dot_general(lib, bi_ref[...], _NT, preferred_element_type=F32)
                            + d_ref[...] * g).astype(BF16)
            dbr_ref[...] += lax.dot_general(ub, lrb, _TN, preferred_element_type=F32)
            dbi_ref[...] += lax.dot_general(ub, lib, _TN, preferred_element_type=F32)
            dcr_ref[...] += lax.dot_general(sr_ref[r, :], gb, _TN, preferred_element_type=F32)
            dci_ref[...] -= lax.dot_general(si_ref[r, :], gb, _TN, preferred_element_type=F32)
            return carry

        lax.fori_loop(0, nc, outputs, 0)

    slab = pl.BlockSpec((S, LANES), lambda g: (0, g))
    states = pl.BlockSpec((S, ns), lambda g: (0, g))
    bspec = pl.BlockSpec((None, LANES, ns), lambda g: (g, 0, 0))
    cspec = pl.BlockSpec((None, ns, LANES), lambda g: (g, 0, 0))
    aspec = pl.BlockSpec((None, 1, ns), lambda g: (g, 0, 0))
    return pl.pallas_call(
        body,
        out_shape=[jax.ShapeDtypeStruct((S, SW), BF16),
                   jax.ShapeDtypeStruct((nslab, LANES, ns), F32), jax.ShapeDtypeStruct((nslab, LANES, ns), F32),
                   jax.ShapeDtypeStruct((nslab, ns, LANES), F32), jax.ShapeDtypeStruct((nslab, ns, LANES), F32),
                   jax.ShapeDtypeStruct((nslab, 1, ns), F32), jax.ShapeDtypeStruct((nslab, 1, ns), F32)],
        grid=(nslab,), in_specs=[slab, slab, states, states] + _slab_specs(ns),
        out_specs=[slab, bspec, bspec, cspec, cspec, aspec, aspec],
        scratch_shapes=[pltpu.VMEM((S, ns), F32)] * 2,
        compiler_params=_params(("parallel",), VMEM_LIMIT_SCAN), name=name)(
            u, dy, st_r, st_i, bbr, bbi, ar, ai, cbr, cbi, dsk)


def _discretise(lam_re, lam_im, log_dt, b_re, b_im):
    dt = jnp.exp(log_dt)[:, None]
    mag = jnp.exp(lam_re * dt)
    ar = mag * jnp.cos(lam_im * dt)
    ai = mag * jnp.sin(lam_im * dt)
    nr, ni = ar - 1.0, ai
    den = lam_re * lam_re + lam_im * lam_im
    cr = ((nr * lam_re + ni * lam_im) / den)[..., None]
    ci = ((ni * lam_re - nr * lam_im) / den)[..., None]
    return ar, ai, cr * b_re - ci * b_im, cr * b_im + ci * b_re


def _block_diag(t, nslab):
    G, R, C = t.shape
    eye = jnp.eye(SLAB_GROUPS, dtype=t.dtype)
    t = t.reshape(nslab, SLAB_GROUPS, R, C)
    return jnp.einsum('sgrc,gh->sgrhc', t, eye).reshape(nslab, SLAB_GROUPS * R, SLAB_GROUPS * C)


def _block_diag_part(t, R, C):
    nslab = t.shape[0]
    eye = jnp.eye(SLAB_GROUPS, dtype=t.dtype)
    t = t.reshape(nslab, SLAB_GROUPS, R, SLAB_GROUPS, C)
    return jnp.einsum('sgrhc,gh->sgrc', t, eye).reshape(nslab * SLAB_GROUPS, R, C)


def _place():
    return lax.axis_index("x"), lax.axis_index("y"), lax.axis_index("c")


_HBM = pl.BlockSpec(memory_space=pltpu.HBM)
_SEM = pl.BlockSpec(memory_space=pltpu.SEMAPHORE)
_ORDERED_EFFECT = pltpu.SideEffectType.DATAFLOW_SIDE_EFFECTING


def _split_call(name, srcs, zones, sems_in, n_new, body_fn, after):
    nsrc, nz, ns, nn = len(srcs), len(zones), len(sems_in), len(n_new)
    nb = nsrc + nz

    def body(*refs):
        outs = refs[nb + ns + 1:]
        body_fn(refs[:nb], refs[nb:nb + ns], outs[:nn])
        outs[nn + nz][...] = jnp.zeros((SUBLANES, LANES), F32)

    res = pl.pallas_call(
        body, name=name,
        out_shape=([pltpu.SemaphoreType.DMA((n,)) for n in n_new] + [pltpu.HBM(b.shape, b.dtype) for b in zones]
                   + [jax.ShapeDtypeStruct((SUBLANES, LANES), F32)]),
        in_specs=[_HBM] * nb + [_SEM] * ns + [_ANY],
        out_specs=[_SEM] * nn + [_HBM] * nz + [pl.BlockSpec(memory_space=pltpu.VMEM)],
        input_output_aliases={nsrc + i: nn + i for i in range(nz)},
        compiler_params=pltpu.CompilerParams(has_side_effects=_ORDERED_EFFECT))(
            *[pltpu.with_memory_space_constraint(b, pltpu.HBM) for b in list(srcs) + list(zones)], *sems_in, after)
    return list(res[:nn]), list(res[nn:nn + nz]), res[-1]


def _mesh_peers():
    x, y, c = _place()
    return x, y, c, (x, y, 1 - c), [(1 - x, y), (x, 1 - y), (1 - x, 1 - y)]


def gather_start(shards, after, *, name):
    nw = len(shards)
    x, y, c = _place()
    zones = [lax.dynamic_update_slice(lax.empty((N_DEV,) + s.shape, s.dtype), s[None], (4 * x + 2 * y + c, 0, 0))
             for s in shards]

    def body(bufs, taken, new):
        for cp in _gather_first(bufs, nw, new[0], new[1]):
            cp.start()

    sems, zones, token = _split_call(name, shards, zones, [], [4 * nw, 4 * nw], body, after)
    return shards, sems, zones, token


def _gather_first(bufs, nw, send, recv):
    x, y, c, sibling, chips = _mesh_peers()
    out = []
    for w in range(nw):
        slot = bufs[nw + w].at[4 * x + 2 * y + c]
        for k, to in enumerate([sibling] + [(*ch, c) for ch in chips]):
            out.append(pltpu.make_async_remote_copy(
                src_ref=bufs[w], dst_ref=slot, send_sem=send.at[4 * w + k], recv_sem=recv.at[4 * w + k],
                device_id=to, device_id_type=MESH))
    return out


def _gather_slot_copy(bufs, nw, w, block, send_sem, recv_sem, to):
    px, py, pc = block
    slot = bufs[nw + w].at[4 * px + 2 * py + pc]
    return pltpu.make_async_remote_copy(src_ref=slot, dst_ref=slot, send_sem=send_sem, recv_sem=recv_sem,
                                        device_id=to, device_id_type=MESH)


def gather_forward(state, after, *, name):
    shards, sems, zones, _ = state
    nw = len(shards)

    def body(bufs, taken, new):
        x, y, c, sibling, chips = _mesh_peers()
        for j, ch in enumerate(chips):
            for w in range(nw):
                k = 4 * w + 1 + j
                _gather_slot_copy(bufs, nw, w, (*ch, c), taken[0].at[k], taken[1].at[k], (*ch, c)).wait_recv()
                _gather_slot_copy(bufs, nw, w, (*ch, c), new[0].at[3 * w + j], new[1].at[3 * w + j], sibling).start()
        for w in range(nw):
            _gather_slot_copy(bufs, nw, w, sibling, taken[0].at[4 * w], taken[1].at[4 * w], sibling).wait_recv()
        for cp in _gather_first(bufs, nw, taken[0], taken[1]):
            cp.wait_send()

    sems, zones, token = _split_call(name, shards, zones, sems, [3 * nw, 3 * nw], body, after)
    return shards, sems, zones, token


def gather_finish(state, after, *, name):
    shards, sems, zones, _ = state
    nw = len(shards)

    def body(bufs, taken, new):
        x, y, c, sibling, chips = _mesh_peers()
        for w in range(nw):
            for j, ch in enumerate(chips):
                cp = _gather_slot_copy(bufs, nw, w, (*ch, 1 - c), taken[0].at[3 * w + j], taken[1].at[3 * w + j], sibling)
                cp.wait_send()
                cp.wait_recv()

    _, zones, _ = _split_call(name, shards, zones, sems, [], body, after)
    return zones


def exchange_start(srcs, zone_shapes, copies, n, after, *, name):
    nw = len(srcs)
    zones = [lax.empty(z, s.dtype) for z, s in zip(zone_shapes, srcs)]

    def body(bufs, taken, new):
        for cp in copies(bufs[:nw], bufs[nw:], new[0], new[1]):
            cp.start()

    sems, zones, token = _split_call(name, srcs, zones, [], [n, n], body, after)
    return srcs, copies, sems, zones, token


def exchange_wait(state, after, *, name):
    srcs, copies, sems, zones, _ = state
    nw = len(srcs)

    def body(bufs, taken, new):
        for cp in copies(bufs[:nw], bufs[nw:], taken[0], taken[1]):
            cp.wait_send()
            cp.wait_recv()

    _, zones, _ = _split_call(name, srcs, zones, sems, [], body, after)
    return zones


def _core_copies(srcs, zones, send, recv):
    x, y, c = _place()
    return [pltpu.make_async_remote_copy(
        src_ref=srcs[w].at[:, 1 - c], dst_ref=zones[w], send_sem=send.at[w], recv_sem=recv.at[w],
        device_id=(x, y, 1 - c), device_id_type=MESH) for w in range(len(srcs))]


def _chip_copies(srcs, zones, send, recv):
    x, y, c = _place()
    chips = [(1 - x, y), (x, 1 - y), (1 - x, 1 - y)]
    return [pltpu.make_async_remote_copy(
        src_ref=srcs[w].at[2 * cx + cy], dst_ref=zones[w].at[j], send_sem=send.at[3 * w + j],
        recv_sem=recv.at[3 * w + j], device_id=(cx, cy, c), device_id_type=MESH)
        for w in range(len(srcs)) for j, (cx, cy) in enumerate(chips)]


def _blocked(fn, ins, outs, *, name, place=None, tr=256):
    k, n = outs[0][0]
    tr = _tile(k, tr, 16)
    if place is None:
        place = jnp.zeros((1,), jnp.int32)
    specs = []
    args = []
    for a in ins:
        if isinstance(a, tuple):
            arr, lead = a
            specs.append(pl.BlockSpec((None, tr, n), functools.partial(lambda i, s, lead: (*lead(i, s), 0), lead=lead)))
            args.append(arr)
        else:
            specs.append(pl.BlockSpec((tr, n), lambda i, s: (i, 0)))
            args.append(a)
    nin = len(args)

    def body(place_ref, *refs):
        res = fn(*[r[...] for r in refs[:nin]])
        for ref, val in zip(refs[nin:], res):
            ref[...] = val.astype(ref.dtype)

    return pl.pallas_call(
        body, out_shape=[jax.ShapeDtypeStruct(s, d) for s, d in outs],
        grid_spec=pltpu.PrefetchScalarGridSpec(
            num_scalar_prefetch=1, grid=(k // tr,), in_specs=specs,
            out_specs=[pl.BlockSpec((tr, n), lambda i, s: (i, 0)) for _ in outs]),
        compiler_params=_params(("parallel",)), name=name)(place, *args)


def _adamw(w, g, m, v):
    m = ADAM_B1 * m + (1.0 - ADAM_B1) * g
    v = ADAM_B2 * v + (1.0 - ADAM_B2) * (g * g)
    m_hat = m / (1.0 - ADAM_B1 ** ADAM_STEP)
    v_hat = v / (1.0 - ADAM_B2 ** ADAM_STEP)
    delta = -ADAM_LR * (m_hat * pl.reciprocal(jnp.sqrt(v_hat) + ADAM_EPS, approx=True) + ADAM_WD * w)
    return delta, m, v


def kernel(x, p, mix_norm_pre, w_in, lam_re, lam_im, log_dt, ssm_b_re, ssm_b_im, ssm_c_re, ssm_c_im, ssm_d, w_glu, b_glu, attn_out_norm, ssm_out_norm, w_out, mix_norm_post, mlp_norm_pre, w_up, w_down, mlp_norm_post, ple_norm_pre, w_ple_gate, w_ple_proj, ple_norm_post, loss_target, m_mix_norm_pre, m_w_in, m_lam_re, m_lam_im, m_log_dt, m_ssm_b_re, m_ssm_b_im, m_ssm_c_re, m_ssm_c_im, m_ssm_d, m_w_glu, m_b_glu, m_attn_out_norm, m_ssm_out_norm, m_w_out, m_mix_norm_post, m_mlp_norm_pre, m_w_up, m_w_down, m_mlp_norm_post, m_ple_norm_pre, m_w_ple_gate, m_w_ple_proj, m_ple_norm_post, v_mix_norm_pre, v_w_in, v_lam_re, v_lam_im, v_log_dt, v_ssm_b_re, v_ssm_b_im, v_ssm_c_re, v_ssm_c_im, v_ssm_d, v_w_glu, v_b_glu, v_attn_out_norm, v_ssm_out_norm, v_w_out, v_mix_norm_post, v_mlp_norm_pre, v_w_up, v_w_down, v_mlp_norm_post, v_ple_norm_pre, v_w_ple_gate, v_w_ple_proj, v_ple_norm_post):
    weights = dict(mix_norm_pre=mix_norm_pre, w_in=w_in, lam_re=lam_re, lam_im=lam_im, log_dt=log_dt, ssm_b_re=ssm_b_re, ssm_b_im=ssm_b_im, ssm_c_re=ssm_c_re, ssm_c_im=ssm_c_im, ssm_d=ssm_d, w_glu=w_glu, b_glu=b_glu, attn_out_norm=attn_out_norm, ssm_out_norm=ssm_out_norm, w_out=w_out, mix_norm_post=mix_norm_post, mlp_norm_pre=mlp_norm_pre, w_up=w_up, w_down=w_down, mlp_norm_post=mlp_norm_post, ple_norm_pre=ple_norm_pre, w_ple_gate=w_ple_gate, w_ple_proj=w_ple_proj, ple_norm_post=ple_norm_post)
    mom_m = dict(mix_norm_pre=m_mix_norm_pre, w_in=m_w_in, lam_re=m_lam_re, lam_im=m_lam_im, log_dt=m_log_dt, ssm_b_re=m_ssm_b_re, ssm_b_im=m_ssm_b_im, ssm_c_re=m_ssm_c_re, ssm_c_im=m_ssm_c_im, ssm_d=m_ssm_d, w_glu=m_w_glu, b_glu=m_b_glu, attn_out_norm=m_attn_out_norm, ssm_out_norm=m_ssm_out_norm, w_out=m_w_out, mix_norm_post=m_mix_norm_post, mlp_norm_pre=m_mlp_norm_pre, w_up=m_w_up, w_down=m_w_down, mlp_norm_post=m_mlp_norm_post, ple_norm_pre=m_ple_norm_pre, w_ple_gate=m_w_ple_gate, w_ple_proj=m_w_ple_proj, ple_norm_post=m_ple_norm_post)
    mom_v = dict(mix_norm_pre=v_mix_norm_pre, w_in=v_w_in, lam_re=v_lam_re, lam_im=v_lam_im, log_dt=v_log_dt, ssm_b_re=v_ssm_b_re, ssm_b_im=v_ssm_b_im, ssm_c_re=v_ssm_c_re, ssm_c_im=v_ssm_c_im, ssm_d=v_ssm_d, w_glu=v_w_glu, b_glu=v_b_glu, attn_out_norm=v_attn_out_norm, ssm_out_norm=v_ssm_out_norm, w_out=v_w_out, mix_norm_post=v_mix_norm_post, mlp_norm_pre=v_mlp_norm_pre, w_up=v_w_up, w_down=v_w_down, mlp_norm_post=v_mlp_norm_post, ple_norm_pre=v_ple_norm_pre, w_ple_gate=v_w_ple_gate, w_ple_proj=v_w_ple_proj, ple_norm_post=v_ple_norm_post)
    order = list(weights)
    big = ["w_in", "w_glu", "w_out", "w_up", "w_down", "w_ple_gate", "w_ple_proj"]
    col_sharded = {"w_in", "w_up", "w_ple_proj"}
    small = [n for n in order if n not in big]

    _, S, D = x.shape
    xs = x[0]
    tgt = loss_target[0]
    AW = attn_out_norm.shape[1]
    SW = ssm_d.shape[1]
    H = AW // HEAD_DIM
    G = SW // SSM_GROUP
    nslab = G // SLAB_GROUPS
    P_, C_ = SSM_STATE, SSM_GROUP

    shard = {n: weights[n][0].astype(BF16) for n in big}
    W, WT = {}, {}

    def arrived(names, gathered):
        for n, g in zip(names, gathered):
            W[n] = g if n in col_sharded else g.reshape(1, N_DEV * g.shape[1], g.shape[2])

    def transposed(g):
        return jnp.swapaxes(g, 1, 2).reshape(1, g.shape[0] * g.shape[2], g.shape[1])

    g1, g2, g3, g4, g5, g6 = (weights[n] for n in ("mix_norm_pre", "mix_norm_post", "mlp_norm_pre",
                                                      "mlp_norm_post", "ple_norm_pre", "ple_norm_post"))
    ga, gs = attn_out_norm, ssm_out_norm
    gather_in = gather_start([shard["w_in"]], shard["w_in"], name="gather_w_in_start")
    (hn1,) = rowwise(lambda a, g: (_rms(a, g),), [xs], [g1], [(D, BF16)], deps=(gather_in[-1],), name="norm_in")
    gather_in = gather_forward(gather_in, hn1, name="gather_w_in_forward")
    arrived(["w_in"], gather_finish(gather_in, gather_in[-1], name="gather_w_in_finish"))
    WT["w_in"] = transposed(W["w_in"])
    early, mid, late = ["w_glu", "w_out"], ["w_up"], ["w_down", "w_ple_gate", "w_ple_proj"]
    gather_early = gather_start([shard[n] for n in early], W["w_in"], name="gather_early_start")
    gather_mid = gather_start([shard[n] for n in mid], gather_early[-1], name="gather_mid_start")
    gather_late = gather_start([shard[n] for n in late], gather_mid[-1], name="gather_late_start")

    (proj,) = mm_nn(hn1, W["w_in"], [F32], deps=(gather_late[-1],), name="proj_in")
    attn, lse = attn_fwd(proj, H, name="attn_fwd")
    gather_early = gather_forward(gather_early, attn, name="gather_early_forward")
    (mix_a,) = rowwise(lambda a, g: (_rms(a, g),), [attn], [ga], [(AW, BF16)], deps=(gather_early[-1],),
                       name="attn_norm")
    arrived(early, gather_finish(gather_early, mix_a, name="gather_early_finish"))

    a_r, a_i, bb_r, bb_i = _discretise(lam_re[0], lam_im[0], log_dt[0], ssm_b_re[0], ssm_b_im[0])
    ssm_consts = (_block_diag(bb_r.swapaxes(1, 2), nslab).astype(BF16), _block_diag(bb_i.swapaxes(1, 2), nslab).astype(BF16),
                  a_r.reshape(nslab, 1, SLAB_STATES), a_i.reshape(nslab, 1, SLAB_STATES),
                  _block_diag(ssm_c_re[0].swapaxes(1, 2), nslab).astype(BF16),
                  _block_diag(ssm_c_im[0].swapaxes(1, 2), nslab).astype(BF16), ssm_d)
    u_seg = _to_segments(proj[:, 3 * AW:]).astype(BF16)
    y_pre, st_r, st_i = ssm_fwd(u_seg, *ssm_consts, name="ssm_fwd")
    gather_mid = gather_forward(gather_mid, y_pre, name="gather_mid_forward")
    (yg,) = rowwise(lambda a: (_gelu(a),), [y_pre], [], [(SW, BF16)], deps=(gather_mid[-1],), name="ssm_gelu")
    (gl1,) = mm_nn(yg, W["w_glu"], [BF16], epi=lambda acc, b: (acc + b,), bias=b_glu, name="glu_gate")
    (mix_s,) = rowwise(lambda yp, gl, g: (_rms(_gelu(yp) * _sigmoid(gl), g),), [y_pre, gl1], [gs], [(SW, BF16)],
                       name="ssm_glu_norm")
    mixed = jnp.concatenate([mix_a, _from_segments(mix_s)], axis=1)
    (mo,) = mm_nn(mixed, W["w_out"], [BF16], name="mix_out")

    def resid_norm(h, t, gpost, gpre):
        hh = h + _rms(t, gpost)
        return hh, _rms(hh, gpre)

    h1, hn2 = rowwise(resid_norm, [xs, mo], [g2, g3], [(D, F32), (D, BF16)], name="resid_mix")
    arrived(mid, gather_finish(gather_mid, hn2, name="gather_mid_finish"))
    gather_late = gather_forward(gather_late, W["w_up"], name="gather_late_forward")
    WT["w_up"] = transposed(W["w_up"])

    def relu2(acc):
        r = jnp.maximum(acc, 0.0)
        return acc, r * r

    up, act = mm_nn(hn2, W["w_up"], [BF16, BF16], epi=relu2, deps=(gather_late[-1],), tm=1024, tn=1024, name="mlp_up")
    arrived(late, gather_finish(gather_late, act, name="gather_late_finish"))
    (ff,) = mm_nn(act, W["w_down"], [BF16], name="mlp_down")
    h2, hn3 = rowwise(resid_norm, [h1, ff], [g4, g5], [(D, F32), (D, BF16)], name="resid_mlp")
    (gl2,) = mm_nn(hn3, W["w_ple_gate"], [BF16], name="ple_gate")
    pb = p[0, 0].astype(BF16)
    (emb,) = mm_nn(pb, W["w_ple_proj"], [BF16], name="ple_proj")

    def head(h, gl, e, t, g):
        sg = _sigmoid(gl)
        ge = sg * e
        err = h + _rms(ge, g) - t
        dh = err * (1.0 / D)
        dge, dg = _rms_bwd(dh, ge, g)
        return dh, dge * e * sg * (1.0 - sg), dge * sg, jnp.sum(err * err, axis=0, keepdims=True), dg

    dh3, dgl2, demb, loss_part, dg6 = rowwise(head, [h2, gl2, emb, tgt], [g6], [(D, F32), (D, BF16), (D, BF16)],
                                             [D, D], name="ple_loss_head")
    loss = lax.psum(0.5 / D * jnp.sum(loss_part), ("x", "y", "c"))

    x_i, y_i, c_i = _place()
    place = jnp.stack([c_i, 2 * x_i + y_i]).astype(jnp.int32)
    grads, out_g, out_d, out_m, out_v = {}, {}, {}, {}, {}

    def to_sibling(names, after, tag):
        chunks = []
        for n in names:
            g = grads[n]
            g = g if n in col_sharded else g.reshape(N_DEV, g.shape[1] // N_DEV, g.shape[2])
            chunks.append(g.reshape(4, 2, g.shape[1], g.shape[2]))
        return chunks, exchange_start(chunks, [(4,) + g.shape[2:] for g in chunks], _core_copies, len(chunks), after,
                                      name=f"grads_to_sibling_{tag}")

    def to_chips(names, sent, after, tag):
        chunks, state = sent
        sums = []
        for n, g, r in zip(names, chunks, exchange_wait(state, after, name=f"grads_from_sibling_{tag}")):
            k, nn = g.shape[2], g.shape[3]
            kb = k // _tile(k, 512, 16)

            def mine(i, s, kb=kb):
                return 2 * (i // kb) + s[0], i % kb

            (s,) = _blocked(lambda a, b: (a.astype(F32) + b.astype(F32),),
                            [(g.reshape(N_DEV, k, nn), mine), r.reshape(4 * k, nn)],
                            [((4 * k, nn), BF16)], place=place, tr=k // kb, name=f"chip_sum_{n}")
            sums.append(s.reshape(4, k, nn))
        return sums, exchange_start(sums, [(3,) + s.shape[1:] for s in sums], _chip_copies, 3 * len(sums), sums[-1],
                                    name=f"grads_to_chips_{tag}")

    def update(w_, m_, v_, own, r0, r1, r2):
        g = own.astype(F32) + r0.astype(F32) + r1.astype(F32) + r2.astype(F32)
        return (g,) + _adamw(w_, g, m_, v_)

    def finish(names, sent, after, tag):
        sums, state = sent
        for n, s, r in zip(names, sums, exchange_wait(state, after, name=f"grads_from_chips_{tag}")):
            shp = weights[n].shape
            res = _blocked(update, [weights[n][0], mom_m[n][0], mom_v[n][0], (s, lambda i, p_: (p_[1], i)),
                                    (r, lambda i, p_: (0, i)), (r, lambda i, p_: (1, i)), (r, lambda i, p_: (2, i))],
                           [(shp[1:], F32)] * 4, place=place, tr=max(16, min(shp[1] // 8, 262144 // shp[2])),
                           name=f"adamw_{n}")
            out_g[n], out_d[n], out_m[n], out_v[n] = (t.reshape(shp) for t in res)
        return out_v[names[-1]]

    grads["w_ple_proj"] = mm_tn(pb, demb, N_DEV, name="grad_w_ple_proj")
    dhn3 = mm_nt(dgl2, W["w_ple_gate"], BF16, name="back_ple_gate")
    grads["w_ple_gate"] = mm_tn(hn3, dgl2, 1, name="grad_w_ple_gate")

    def back_resid(dh, dhn, h, t, gpre, gpost):
        d1, dgpre = _rms_bwd(dhn, h, gpre)
        dhh = dh + d1
        dt, dgpost = _rms_bwd(dhh, t, gpost)
        return dhh, dt, dgpre, dgpost

    dh2, dff, dg5, dg4 = rowwise(back_resid, [dh3, dhn3, h2, ff], [g5, g4], [(D, F32), (D, BF16)], [D, D],
                                 name="back_resid_mlp")
    dup = mm_nt(dff, W["w_down"], BF16, epi=lambda acc, u_: (acc * 2.0 * jnp.maximum(u_.astype(F32), 0.0),),
                extra=up, name="back_mlp_down")
    grads["w_down"] = mm_tn(act, dff, 1, name="grad_w_down")
    group_a = ["w_ple_proj", "w_ple_gate", "w_down"]
    sent_a = to_sibling(group_a, grads["w_down"], "a")
    (dhn2,) = mm_nn(dup, WT["w_up"], [BF16], deps=(sent_a[1][-1],), name="back_mlp_up")
    sent_a = to_chips(group_a, sent_a, dhn2, "a")
    grads["w_up"] = mm_tn(hn2, dup, N_DEV, deps=(sent_a[1][-1],), name="grad_w_up")
    dh1, dmo, dg3, dg2 = rowwise(back_resid, [dh2, dhn2, h1, mo], [g3, g2], [(D, F32), (D, BF16)], [D, D],
                                 name="back_resid_mix")
    dmixed = mm_nt(dmo, W["w_out"], BF16, name="back_mix_out")
    grads["w_out"] = mm_tn(mixed, dmo, 1, name="grad_w_out")

    def back_glu(dm, yp, gl, g):
        ygf = _gelu(yp)
        sg = _sigmoid(gl)
        dssm, dg = _rms_bwd(dm, ygf * sg, g)
        dgl = dssm * ygf * sg * (1.0 - sg)
        return dgl, dssm * sg, dg, jnp.sum(dgl, axis=0, keepdims=True)

    dgl1, dyg_direct, dgs, db_glu = rowwise(back_glu, [_to_segments(dmixed[:, AW:]), y_pre, gl1], [gs],
                                            [(SW, BF16), (SW, F32)], [SW, SW], name="back_glu")
    dyg_gate = mm_nt(dgl1, W["w_glu"], BF16, name="back_glu_gate")
    grads["w_glu"] = mm_tn(yg, dgl1, 1, name="grad_w_glu")
    group_b = ["w_up", "w_out", "w_glu"]
    sent_b = to_sibling(group_b, grads["w_glu"], "b")
    done_a = finish(group_a, sent_a, sent_b[1][-1], "a")

    def back_gelu(d1, d2, yp, u_):
        dy = (d1 + d2) * _gelu_grad(yp)
        return dy, jnp.sum(dy * u_.astype(F32), axis=0, keepdims=True)

    dy_pre, d_skip = rowwise(back_gelu, [dyg_direct, dyg_gate, y_pre, u_seg], [], [(SW, F32)], [SW], deps=(done_a,),
                             name="back_gelu")
    du_seg, dbb_r, dbb_i, dcb_r, dcb_i, da_r, da_i = ssm_bwd(u_seg, dy_pre, st_r, st_i, *ssm_consts, name="ssm_bwd")
    sent_b = to_chips(group_b, sent_b, du_seg, "b")

    def back_attn_norm(dm, a, g):
        da, dg = _rms_bwd(dm, a, g)
        prod = da * a
        delta = jnp.concatenate(
            [jnp.broadcast_to(jnp.sum(prod[:, h * HEAD_DIM:(h + 1) * HEAD_DIM], axis=-1, keepdims=True),
                              (prod.shape[0], HEAD_DIM)) for h in range(H)], axis=1)
        return da, delta, dg

    dattn, delta, dga = rowwise(back_attn_norm, [(dmixed, AW, 0), attn], [ga], [(AW, F32), (AW, F32)], [AW],
                                deps=(sent_b[1][-1],), name="back_attn_norm")
    dq, dk, dv = attn_bwd(proj, dattn, lse, delta, H, name="attn_bwd")
    dproj = jnp.concatenate([dq, dk, dv, _from_segments(du_seg)], axis=1)
    (dhn1,) = mm_nn(dproj, WT["w_in"], [BF16], name="back_proj_in")

    def back_in(dh, dhn, a, g):
        d1, dg = _rms_bwd(dhn, a, g)
        return dh + d1, dg

    grad_x, dg1 = rowwise(back_in, [dh1, dhn1, xs], [g1], [(D, F32)], [D], name="back_norm_in")

    cot = dict(
        mix_norm_pre=dg1, mix_norm_post=dg2, mlp_norm_pre=dg3, mlp_norm_post=dg4, ple_norm_pre=dg5, ple_norm_post=dg6,
        attn_out_norm=dga, ssm_out_norm=dgs, b_glu=db_glu, ssm_d=d_skip,
        ssm_c_re=_block_diag_part(dcb_r, P_, C_).swapaxes(1, 2), ssm_c_im=_block_diag_part(dcb_i, P_, C_).swapaxes(1, 2),
        a_r=da_r.reshape(G, P_), a_i=da_i.reshape(G, P_),
        bb_r=_block_diag_part(dbb_r, C_, P_).swapaxes(1, 2), bb_i=_block_diag_part(dbb_i, C_, P_).swapaxes(1, 2))
    names = list(cot)
    flat = jnp.concatenate([cot[n].reshape(-1) for n in names])
    total = flat.shape[0]
    rows_ = -(-total // (LANES * 16)) * 16
    flat = jnp.pad(flat, (0, rows_ * LANES - total)).reshape(rows_, LANES)
    gather_small = gather_start([flat], flat, name="gather_small_start")
    grads["w_in"] = mm_tn(hn1, dproj, N_DEV, deps=(gather_small[-1],), tko=2048, name="grad_w_in")
    group_c = ["w_in"]
    sent_c = to_sibling(group_c, grads["w_in"], "c")
    done_b = finish(group_b, sent_b, sent_c[1][-1], "b")
    sent_c = to_chips(group_c, sent_c, done_b, "c")
    gather_small = gather_forward(gather_small, sent_c[1][-1], name="gather_small_forward")
    (every,) = gather_finish(gather_small, gather_small[-1], name="gather_small_finish")
    (summed,) = _blocked(lambda *t: (functools.reduce(lambda a, b: a + b, t),),
                         [(every, functools.partial(lambda i, p_, j: (j, i), j=j)) for j in range(N_DEV)],
                         [((rows_, LANES), F32)], name="sum_small_grads")
    summed = summed.reshape(-1)
    red, off = {}, 0
    for n in names:
        sz = cot[n].size
        red[n] = summed[off:off + sz].reshape(cot[n].shape)
        off += sz
    _, pull = jax.vjp(_discretise, lam_re[0], lam_im[0], log_dt[0], ssm_b_re[0], ssm_b_im[0])
    d_lre, d_lim, d_ldt, d_bre, d_bim = pull((red["a_r"], red["a_i"], red["bb_r"], red["bb_i"]))
    red.update(lam_re=d_lre, lam_im=d_lim, log_dt=d_ldt, ssm_b_re=d_bre, ssm_b_im=d_bim)

    def pack(d):
        t = jnp.concatenate([d[n].reshape(-1) for n in small])
        r_ = -(-t.shape[0] // (LANES * 16)) * 16
        return jnp.pad(t, (0, r_ * LANES - t.shape[0])).reshape(r_, LANES)

    sw, sg_, sm, sv = pack(weights), pack(red), pack(mom_m), pack(mom_v)
    sd, snm, snv = _blocked(lambda w_, g_, m_, v_: _adamw(w_, g_, m_, v_), [sw, sg_, sm, sv],
                            [(sw.shape, F32)] * 3, name="adamw_small")
    finish(group_c, sent_c, snv, "c")
    off = 0
    for n in small:
        sz = weights[n].size
        shp = weights[n].shape
        out_g[n] = red[n].reshape(shp)
        out_d[n] = sd.reshape(-1)[off:off + sz].reshape(shp)
        out_m[n] = snm.reshape(-1)[off:off + sz].reshape(shp)
        out_v[n] = snv.reshape(-1)[off:off + sz].reshape(shp)
        off += sz

    return (loss, grad_x[None], *[out_g[n] for n in order], *[out_d[n] for n in order],
            *[out_m[n] for n in order], *[out_v[n] for n in order])
```

```python
import functools
import math

import jax
import jax.numpy as jnp
from jax import lax
from jax.experimental import pallas as pl
from jax.experimental.pallas import tpu as pltpu

F32 = jnp.float32
BF16 = jnp.bfloat16
MESH = pl.DeviceIdType.MESH

N_DEV = 8
LANES = 128
SUBLANES = 8
VMEM_LIMIT = 48 * 1024 * 1024
VMEM_LIMIT_SCAN = 60 * 1024 * 1024

HEAD_DIM = 128
BLK = 128
DILATIONS = (1, 4, 16)
SSM_GROUP = 16
SSM_STATE = 64
SLAB_GROUPS = LANES // SSM_GROUP
SLAB_STATES = SLAB_GROUPS * SSM_STATE
SEGMENTS = SUBLANES
SCAN_UNROLL = 4
RMS_EPS = 1e-6
NEG_INF = -1e30

ADAM_LR = 0.001
ADAM_B1 = 0.9
ADAM_B2 = 0.999
ADAM_EPS = 1e-08
ADAM_WD = 0.01
ADAM_STEP = 10


def _tile(n, pref, unit=LANES):
    if n <= pref:
        return n
    t = (pref // unit) * unit
    while t > unit and n % t:
        t -= unit
    assert n % t == 0, (n, pref, unit)
    return t


def _params(sem=None, vmem=VMEM_LIMIT):
    return pltpu.CompilerParams(dimension_semantics=sem, vmem_limit_bytes=vmem)


_NN = (((1,), (0,)), ((), ()))
_NT = (((1,), (1,)), ((), ()))
_TN = (((0,), (0,)), ((), ()))


_ANY = pl.BlockSpec(memory_space=pl.ANY)


def _mm_call(dims, nk, n_extra, n_dep, n_out, epi, **kw):
    first_out = 2 + n_extra + n_dep
    kw["in_specs"] = list(kw["in_specs"]) + [_ANY] * n_dep

    def single(*refs):
        extra = refs[2:2 + n_extra]
        res = epi(lax.dot_general(refs[0][...], refs[1][...], dims, preferred_element_type=F32),
                  *[e[...] for e in extra])
        for o, r in zip(refs[first_out:first_out + n_out], res):
            o[...] = r.astype(o.dtype)

    if nk == 1:
        kw["scratch_shapes"] = []
        return pl.pallas_call(single, **kw)

    def body(*refs):
        a_ref, b_ref = refs[0], refs[1]
        extra = refs[2:2 + n_extra]
        outs = refs[first_out:first_out + n_out]
        acc = refs[-1]
        k = pl.program_id(2)

        @pl.when(k == 0)
        def _():
            acc[...] = jnp.zeros_like(acc)

        acc[...] += lax.dot_general(a_ref[...], b_ref[...], dims, preferred_element_type=F32)

        @pl.when(k == nk - 1)
        def _():
            res = epi(acc[...], *[e[...] for e in extra])
            for o, r in zip(outs, res):
                o[...] = r.astype(o.dtype)

    return pl.pallas_call(body, **kw)


def _identity_epi(acc):
    return (acc,)


def mm_nn(a, w, out_dtypes, *, name, epi=_identity_epi, bias=None, deps=(), tm=2048, tn=512, tk=2048):
    M, K = a.shape
    J, K2, n = w.shape
    assert K == K2
    tm, tn, tk = _tile(M, tm, 16), _tile(n, tn), _tile(K, tk)
    npj = n // tn
    nk = K // tk
    in_specs = [pl.BlockSpec((tm, tk), lambda i, j, k: (i, k)),
                pl.BlockSpec((None, tk, tn), lambda i, j, k: (j // npj, k, j % npj))]
    args = [a, w]
    if bias is not None:
        in_specs.append(pl.BlockSpec((1, tn), lambda i, j, k: (0, j)))
        args.append(bias)
    return _mm_call(
        _NN, nk, len(args) - 2, len(deps), len(out_dtypes), epi,
        out_shape=[jax.ShapeDtypeStruct((M, J * n), d) for d in out_dtypes],
        grid=(M // tm, J * npj, nk), in_specs=in_specs,
        out_specs=[pl.BlockSpec((tm, tn), lambda i, j, k: (i, j)) for _ in out_dtypes],
        scratch_shapes=[pltpu.VMEM((tm, tn), F32)],
        compiler_params=_params(("parallel", "parallel", "arbitrary")), name=name)(*args, *deps)


def mm_nt(a, w, out_dtype, *, name, epi=_identity_epi, extra=None, tm=2048, tko=512, tnr=2048):
    M, N = a.shape
    J, K, n = w.shape
    assert N == J * n
    tm, tko, tnr = _tile(M, tm, 16), _tile(K, tko), _tile(n, tnr)
    npj = n // tnr
    nk = N // tnr
    in_specs = [pl.BlockSpec((tm, tnr), lambda i, j, k: (i, k)),
                pl.BlockSpec((None, tko, tnr), lambda i, j, k: (k // npj, j, k % npj))]
    args = [a, w]
    if extra is not None:
        in_specs.append(pl.BlockSpec((tm, tko), lambda i, j, k: (i, j)))
        args.append(extra)
    return _mm_call(
        _NT, nk, len(args) - 2, 0, 1, epi,
        out_shape=[jax.ShapeDtypeStruct((M, K), out_dtype)],
        grid=(M // tm, K // tko, nk), in_specs=in_specs,
        out_specs=[pl.BlockSpec((tm, tko), lambda i, j, k: (i, j))],
        scratch_shapes=[pltpu.VMEM((tm, tko), F32)],
        compiler_params=_params(("parallel", "parallel", "arbitrary")), name=name)(*args)[0]


def mm_tn(a, b, J, *, name, deps=(), tko=1024, tn=1024, ts=2048):
    S, K = a.shape
    S2, N = b.shape
    assert S == S2 and N % J == 0
    n = N // J
    tko, tn, ts = _tile(K, tko), _tile(n, tn), _tile(S, ts)
    npj = n // tn
    nk = S // ts
    return _mm_call(
        _TN, nk, 0, len(deps), 1, _identity_epi,
        out_shape=[jax.ShapeDtypeStruct((J, K, n), BF16)],
        grid=(K // tko, J * npj, nk),
        in_specs=[pl.BlockSpec((ts, tko), lambda i, j, k: (k, i)),
                  pl.BlockSpec((ts, tn), lambda i, j, k: (k, j))],
        out_specs=[pl.BlockSpec((None, tko, tn), lambda i, j, k: (j // npj, i, j % npj))],
        scratch_shapes=[pltpu.VMEM((tko, tn), F32)],
        compiler_params=_params(("parallel", "parallel", "arbitrary")), name=name)(a, b, *deps)[0]


def rowwise(fn, rows, vecs, outs, accs=(), *, name, deps=(), ts=256):
    rows = [r if isinstance(r, tuple) else (r, r.shape[1], 0) for r in rows]
    S = rows[0][0].shape[0]
    ts = _tile(S, ts, 16)
    nr, nv, no, nd = len(rows), len(vecs), len(outs), len(deps)

    def body(*refs):
        r, v = refs[:nr], refs[nr:nr + nv]
        o, a = refs[nr + nv + nd:nr + nv + nd + no], refs[nr + nv + nd + no:]
        res = fn(*[t[...].astype(F32) for t in r], *[t[...] for t in v])
        for ref, val in zip(o, res[:no]):
            ref[...] = val.astype(ref.dtype)
        if a:
            @pl.when(pl.program_id(0) == 0)
            def _():
                for ref in a:
                    ref[...] = jnp.zeros_like(ref)

            for ref, val in zip(a, res[no:]):
                ref[...] += val

    in_specs = [pl.BlockSpec((ts, w), functools.partial(lambda i, cb: (i, cb), cb=cb)) for _, w, cb in rows]
    in_specs += [pl.BlockSpec(v.shape, lambda i: (0, 0)) for v in vecs] + [_ANY] * nd
    out_shape = [jax.ShapeDtypeStruct((S, w), d) for w, d in outs]
    out_shape += [jax.ShapeDtypeStruct((1, w), F32) for w in accs]
    out_specs = [pl.BlockSpec((ts, w), lambda i: (i, 0)) for w, _ in outs]
    out_specs += [pl.BlockSpec((1, w), lambda i: (0, 0)) for w in accs]
    return pl.pallas_call(body, out_shape=out_shape, grid=(S // ts,), in_specs=in_specs, out_specs=out_specs,
                          compiler_params=_params(("arbitrary",)), name=name)(*[r[0] for r in rows], *vecs, *deps)


def _rms(x, g):
    r = lax.rsqrt(jnp.mean(x * x, axis=-1, keepdims=True) + RMS_EPS)
    return x * r * g


def _rms_bwd(dy, x, g):
    r = lax.rsqrt(jnp.mean(x * x, axis=-1, keepdims=True) + RMS_EPS)
    xh = x * r
    dxh = dy * g
    dx = r * (dxh - xh * jnp.mean(dxh * xh, axis=-1, keepdims=True))
    return dx, jnp.sum(dy * xh, axis=0, keepdims=True)


def _sigmoid(x):
    return pl.reciprocal(1.0 + jnp.exp(-x), approx=True)


_GELU_C = math.sqrt(2.0 / math.pi)


def _gelu(x):
    return 0.5 * x * (1.0 + jnp.tanh(_GELU_C * (x + 0.044715 * x * x * x)))


def _gelu_grad(x):
    t = jnp.tanh(_GELU_C * (x + 0.044715 * x * x * x))
    return 0.5 * (1.0 + t) + 0.5 * x * (1.0 - t * t) * _GELU_C * (1.0 + 3.0 * 0.044715 * x * x)


ATTN_INTERLEAVE = 8
KEY_PAD = BLK * max(DILATIONS)


def _key_mask(n):
    ii = lax.broadcasted_iota(jnp.int32, (BLK, 2 * BLK), 0)
    jj = lax.broadcasted_iota(jnp.int32, (BLK, 2 * BLK), 1)
    return ((jj < BLK) & (jj >= ii) & (n > 0)) | ((jj >= BLK) & (jj - BLK <= ii))


def _units(d, nblk):
    nb = nblk // d
    if nb == 2:
        def unit(idx):
            ii = lax.broadcasted_iota(jnp.int32, (2 * BLK, 2 * BLK), 0)
            jj = lax.broadcasted_iota(jnp.int32, (2 * BLK, 2 * BLK), 1)
            return pl.ds(idx, 2 * BLK, stride=d), pl.ds(KEY_PAD + idx, 2 * BLK, stride=d), (jj <= ii) & (ii - jj <= BLK)
        return d, max(1, ATTN_INTERLEAVE // 4), unit

    def unit(idx):
        r, n = idx // nb, idx % nb
        cur = r + n * (BLK * d)
        keys = cur + (KEY_PAD - BLK * d)
        if d == 1:
            return pl.ds(pl.multiple_of(cur, BLK), BLK), pl.ds(pl.multiple_of(keys, BLK), 2 * BLK), _key_mask(n)
        return pl.ds(cur, BLK, stride=d), pl.ds(keys, 2 * BLK, stride=d), _key_mask(n)
    return nblk, ATTN_INTERLEAVE, unit


def _pad_keys(dst, src):
    dst[pl.ds(0, KEY_PAD), :] = jnp.zeros((KEY_PAD, dst.shape[1]), F32)

    def copy(c, carry):
        dst[pl.ds(pl.multiple_of(KEY_PAD + c * BLK, BLK), BLK), :] = src[pl.ds(pl.multiple_of(c * BLK, BLK), BLK), :]
        return carry

    lax.fori_loop(0, src.shape[0] // BLK, copy, 0)


def attn_fwd(proj, n_heads, *, name):
    S, WP = proj.shape
    assert S % (BLK * max(DILATIONS)) == 0
    nblk = S // BLK
    AW = n_heads * HEAD_DIM
    scale = 1.0 / math.sqrt(HEAD_DIM)

    def body(q_ref, k_ref, v_ref, o_ref, l_ref, acc, mrun, lrun, kp, vp):
        _pad_keys(kp, k_ref)
        _pad_keys(vp, v_ref)
        for first, d in zip((True, False, False), reversed(DILATIONS)):
            n_units, per_step, unit = _units(d, nblk)

            def step(it, carry, first=first, n_units=n_units, per_step=per_step, unit=unit):
                units = [unit(it + j * (n_units // per_step)) for j in range(per_step)]
                ss = [lax.dot_general(q_ref[cur, :].astype(BF16), kp[keys, :].astype(BF16), _NT,
                                      preferred_element_type=F32) * scale for cur, keys, _ in units]
                ss = [jnp.where(mask, s, NEG_INF) for s, (_, _, mask) in zip(ss, units)]
                ms = [jnp.max(s, axis=-1, keepdims=True) for s in ss]
                ps = [jnp.exp(s - m) for s, m in zip(ss, ms)]
                ls = [jnp.sum(p, axis=-1, keepdims=True) for p in ps]
                os_ = [jnp.dot(p.astype(BF16), vp[keys, :].astype(BF16), preferred_element_type=F32)
                       for p, (_, keys, _) in zip(ps, units)]
                for (cur, keys, mask), m, l, o in zip(units, ms, ls, os_):
                    m = jnp.broadcast_to(m, o.shape)
                    l = jnp.broadcast_to(l, o.shape)
                    if first:
                        acc[cur, :], mrun[cur, :], lrun[cur, :] = o, m, l
                    else:
                        m_old = mrun[cur, :]
                        m_new = jnp.maximum(m_old, m)
                        w_old, w_blk = jnp.exp(m_old - m_new), jnp.exp(m - m_new)
                        acc[cur, :] = w_old * acc[cur, :] + w_blk * o
                        lrun[cur, :] = w_old * lrun[cur, :] + w_blk * l
                        mrun[cur, :] = m_new
                return carry

            lax.fori_loop(0, n_units // per_step, step, 0)

        def finish(c, carry):
            r = pl.ds(pl.multiple_of(c * BLK, BLK), BLK)
            o_ref[r, :] = acc[r, :] / lrun[r, :]
            l_ref[r, :] = mrun[r, :] + jnp.log(lrun[r, :])
            return carry

        lax.fori_loop(0, nblk, finish, 0)

    def col(off):
        return pl.BlockSpec((S, HEAD_DIM), lambda h: (0, off + h))

    ospec = pl.BlockSpec((S, HEAD_DIM), lambda h: (0, h))
    return pl.pallas_call(
        body, out_shape=[jax.ShapeDtypeStruct((S, AW), F32)] * 2, grid=(n_heads,),
        in_specs=[col(0), col(n_heads), col(2 * n_heads)], out_specs=[ospec, ospec],
        scratch_shapes=[pltpu.VMEM((S, HEAD_DIM), F32)] * 3 + [pltpu.VMEM((KEY_PAD + S, HEAD_DIM), F32)] * 2,
        compiler_params=_params(("parallel",)), name=name)(proj, proj, proj)


def attn_bwd(proj, do, lse, delta, n_heads, *, name):
    S, WP = proj.shape
    nblk = S // BLK
    AW = n_heads * HEAD_DIM
    scale = 1.0 / math.sqrt(HEAD_DIM)

    def body(q_ref, k_ref, v_ref, do_ref, l_ref, dl_ref, dq_ref, dk_ref, dv_ref, dq_sc, dk_sc, dv_sc, kp, vp):
        _pad_keys(kp, k_ref)
        _pad_keys(vp, v_ref)
        dq_sc[...] = jnp.zeros_like(dq_sc)
        dk_sc[...] = jnp.zeros_like(dk_sc)
        dv_sc[...] = jnp.zeros_like(dv_sc)
        for d in DILATIONS:
            n_units, per_step, unit = _units(d, nblk)

            def step(it, carry, n_units=n_units, per_step=per_step, unit=unit):
                units = [unit(it + j * (n_units // per_step)) for j in range(per_step)]
                qs = [q_ref[cur, :].astype(BF16) for cur, _, _ in units]
                gs = [do_ref[cur, :].astype(BF16) for cur, _, _ in units]
                ks = [kp[keys, :].astype(BF16) for _, keys, _ in units]
                ss = [lax.dot_general(q, kb, _NT, preferred_element_type=F32) * scale for q, kb in zip(qs, ks)]
                dps = [lax.dot_general(g, vp[keys, :].astype(BF16), _NT, preferred_element_type=F32)
                       for g, (_, keys, _) in zip(gs, units)]
                ps = [jnp.where(mask, jnp.exp(s - l_ref[cur, :][:, :1]), 0.0) for s, (cur, _, mask) in zip(ss, units)]
                dss = [(p * (dp - dl_ref[cur, :][:, :1]) * scale).astype(BF16)
                       for p, dp, (cur, _, _) in zip(ps, dps, units)]
                for (cur, keys, _), q, g, kb, p, ds in zip(units, qs, gs, ks, ps, dss):
                    dq_sc[cur, :] += jnp.dot(ds, kb, preferred_element_type=F32)
                    dk_sc[keys, :] += lax.dot_general(ds, q, _TN, preferred_element_type=F32)
                    dv_sc[keys, :] += lax.dot_general(p.astype(BF16), g, _TN, preferred_element_type=F32)
                return carry

            lax.fori_loop(0, n_units // per_step, step, 0)
        rows = pl.ds(KEY_PAD, S)
        dq_ref[...] = dq_sc[...].astype(BF16)
        dk_ref[...] = dk_sc[rows, :].astype(BF16)
        dv_ref[...] = dv_sc[rows, :].astype(BF16)

    def col(off):
        return pl.BlockSpec((S, HEAD_DIM), lambda h: (0, off + h))

    ospec = pl.BlockSpec((S, HEAD_DIM), lambda h: (0, h))
    return pl.pallas_call(
        body, out_shape=[jax.ShapeDtypeStruct((S, AW), BF16)] * 3, grid=(n_heads,),
        in_specs=[col(0), col(n_heads), col(2 * n_heads), ospec, ospec, ospec], out_specs=[ospec] * 3,
        scratch_shapes=[pltpu.VMEM((S, HEAD_DIM), F32)] + [pltpu.VMEM((KEY_PAD + S, HEAD_DIM), F32)] * 4,
        compiler_params=_params(("parallel",), VMEM_LIMIT_SCAN), name=name)(proj, proj, proj, do, lse, delta)


def _to_segments(t):
    S, W = t.shape
    return t.reshape(SEGMENTS, S // SEGMENTS, W).swapaxes(0, 1).reshape(S, W)


def _from_segments(t):
    S, W = t.shape
    return t.reshape(S // SEGMENTS, SEGMENTS, W).swapaxes(0, 1).reshape(S, W)


def _cmul(ar, ai, br, bi):
    return ar * br - ai * bi, ar * bi + ai * br


def _power(ar, ai, log2n):
    for _ in range(log2n):
        ar, ai = _cmul(ar, ai, ar, ai)
    return ar, ai


def _shift_rows(x, up):
    row = lax.broadcasted_iota(jnp.int32, x.shape, 0)
    if up:
        return jnp.where(row == SEGMENTS - 1, 0.0, pltpu.roll(x, SEGMENTS - 1, 0))
    return jnp.where(row == 0, 0.0, pltpu.roll(x, 1, 0))


def _segment_carries(er, ei, pr, pi, up):
    cr = jnp.zeros_like(er)
    ci = jnp.zeros_like(ei)
    for _ in range(SEGMENTS - 1):
        tr, ti = _cmul(pr, pi, cr, ci)
        cr, ci = _shift_rows(er + tr, up), _shift_rows(ei + ti, up)
    return cr, ci


def _scan_states(sr, si, ar, ai, T, reverse):
    ns = sr.shape[1]
    ar8 = jnp.broadcast_to(ar, (SEGMENTS, ns))
    ai8 = jnp.broadcast_to(ai, (SEGMENTS, ns))

    def rows(t):
        k = (T - 1 - t) if reverse else t
        return pl.ds(pl.multiple_of(k * SEGMENTS, SEGMENTS), SEGMENTS)

    def advance(t, c):
        tr, ti = _cmul(ar8, ai8, c[0], c[1])
        return tr + sr[rows(t), :], ti + si[rows(t), :]

    def several(step):
        def trip(t, c):
            for j in range(SCAN_UNROLL):
                c = step(t * SCAN_UNROLL + j, c)
            return c
        return trip

    zero = jnp.zeros((SEGMENTS, ns), F32)
    er, ei = lax.fori_loop(0, T // SCAN_UNROLL, several(advance), (zero, zero))
    pr, pi = _power(ar, ai, T.bit_length() - 1)
    cr, ci = _segment_carries(er, ei, jnp.broadcast_to(pr, (SEGMENTS, ns)), jnp.broadcast_to(pi, (SEGMENTS, ns)), reverse)

    def store(t, c):
        nr, ni = advance(t, c)
        sr[rows(t), :] = nr
        si[rows(t), :] = ni
        return nr, ni

    lax.fori_loop(0, T // SCAN_UNROLL, several(store), (cr, ci))
    return cr, ci


def _slab_specs(ns):
    return [pl.BlockSpec((None, LANES, ns), lambda g: (g, 0, 0)),
            pl.BlockSpec((None, LANES, ns), lambda g: (g, 0, 0)),
            pl.BlockSpec((None, 1, ns), lambda g: (g, 0, 0)),
            pl.BlockSpec((None, 1, ns), lambda g: (g, 0, 0)),
            pl.BlockSpec((None, ns, LANES), lambda g: (g, 0, 0)),
            pl.BlockSpec((None, ns, LANES), lambda g: (g, 0, 0)),
            pl.BlockSpec((1, LANES), lambda g: (0, g))]


def _chunks(S):
    rc = _tile(S, 512, 16)
    return rc, S // rc


def ssm_fwd(u, bbr, bbi, ar, ai, cbr, cbi, dsk, *, name):
    S, SW = u.shape
    nslab, _, ns = bbr.shape
    T = S // SEGMENTS
    assert T & (T - 1) == 0
    rc, nc = _chunks(S)

    def body(u_ref, br_ref, bi_ref, ar_ref, ai_ref, cr_ref, ci_ref, d_ref, y_ref, str_ref, sti_ref, sr, si):
        def inputs(c, carry):
            r = pl.ds(pl.multiple_of(c * rc, rc), rc)
            sr[r, :] = jnp.dot(u_ref[r, :], br_ref[...], preferred_element_type=F32)
            si[r, :] = jnp.dot(u_ref[r, :], bi_ref[...], preferred_element_type=F32)
            return carry

        lax.fori_loop(0, nc, inputs, 0)
        _scan_states(sr, si, ar_ref[...], ai_ref[...], T, False)

        def outputs(c, carry):
            r = pl.ds(pl.multiple_of(c * rc, rc), rc)
            srb, sib = sr[r, :].astype(BF16), si[r, :].astype(BF16)
            str_ref[r, :] = srb
            sti_ref[r, :] = sib
            y_ref[r, :] = (jnp.dot(srb, cr_ref[...], preferred_element_type=F32)
                           - jnp.dot(sib, ci_ref[...], preferred_element_type=F32)
                           + d_ref[...] * u_ref[r, :].astype(F32))
            return carry

        lax.fori_loop(0, nc, outputs, 0)

    slab = pl.BlockSpec((S, LANES), lambda g: (0, g))
    states = pl.BlockSpec((S, ns), lambda g: (0, g))
    return pl.pallas_call(
        body, out_shape=[jax.ShapeDtypeStruct((S, SW), F32)] + [jax.ShapeDtypeStruct((S, nslab * ns), BF16)] * 2,
        grid=(nslab,), in_specs=[slab] + _slab_specs(ns), out_specs=[slab, states, states],
        scratch_shapes=[pltpu.VMEM((S, ns), F32)] * 2,
        compiler_params=_params(("parallel",), VMEM_LIMIT_SCAN), name=name)(u, bbr, bbi, ar, ai, cbr, cbi, dsk)


def ssm_bwd(u, dy, st_r, st_i, bbr, bbi, ar, ai, cbr, cbi, dsk, *, name):
    S, SW = u.shape
    nslab, _, ns = bbr.shape
    T = S // SEGMENTS
    rc, nc = _chunks(S)
    pair_rows = 2 * SEGMENTS

    def body(u_ref, dy_ref, sr_ref, si_ref, br_ref, bi_ref, ar_ref, ai_ref, cr_ref, ci_ref, d_ref,
             du_ref, dbr_ref, dbi_ref, dcr_ref, dci_ref, dar_ref, dai_ref, lr, li):
        def inputs(c, carry):
            r = pl.ds(pl.multiple_of(c * rc, rc), rc)
            gb = dy_ref[r, :].astype(BF16)
            lr[r, :] = lax.dot_general(gb, cr_ref[...], _NT, preferred_element_type=F32)
            li[r, :] = -lax.dot_general(gb, ci_ref[...], _NT, preferred_element_type=F32)
            return carry

        lax.fori_loop(0, nc, inputs, 0)
        _scan_states(lr, li, ar_ref[...], -ai_ref[...], T, True)

        def steps(j):
            rows = pl.ds(pl.multiple_of(j * pair_rows, pair_rows), pair_rows)
            tr, ti = sr_ref[rows, :].astype(F32), si_ref[rows, :].astype(F32)
            return tr[:SEGMENTS], tr[SEGMENTS:], ti[:SEGMENTS], ti[SEGMENTS:]

        def pair(j, c):
            acc_r, acc_i, pr, pi = c
            lo_r, hi_r, lo_i, hi_i = steps(j)
            first = pl.ds(pl.multiple_of(j * pair_rows, SEGMENTS), SEGMENTS)
            second = pl.ds(pl.multiple_of(j * pair_rows + SEGMENTS, SEGMENTS), SEGMENTS)
            la_r, la_i, lb_r, lb_i = lr[first, :], li[first, :], lr[second, :], li[second, :]
            return (acc_r + la_r * pr + la_i * pi + lb_r * lo_r + lb_i * lo_i,
                    acc_i - la_r * pi + la_i * pr - lb_r * lo_i + lb_i * lo_r, hi_r, hi_i)

        def pairs(t, c):
            return pair(2 * t + 1, pair(2 * t, c))

        _, end_r, _, end_i = steps(T // 2 - 1)
        zero = jnp.zeros((SEGMENTS, ns), F32)
        acc = lax.fori_loop(0, T // 4, pairs, (zero, zero, _shift_rows(end_r, False), _shift_rows(end_i, False)))
        dar_ref[...] = jnp.sum(acc[0], axis=0, keepdims=True)
        dai_ref[...] = jnp.sum(acc[1], axis=0, keepdims=True)

        dbr_ref[...] = jnp.zeros_like(dbr_ref)
        dbi_ref[...] = jnp.zeros_like(dbi_ref)
        dcr_ref[...] = jnp.zeros_like(dcr_ref)
        dci_ref[...] = jnp.zeros_like(dci_ref)

        def outputs(c, carry):
            r = pl.ds(pl.multiple_of(c * rc, rc), rc)
            ub = u_ref[r, :]
            g = dy_ref[r, :]
            gb = g.astype(BF16)
            lrb = lr[r, :].astype(BF16)
            lib = li[r, :].astype(BF16)
            du_ref[r, :] = (lax.dot_general(lrb, br_ref[...], _NT, preferred_element_type=F32)
                            + lax.dot_general(lib, bi_ref[...], _NT, preferred_element_type=F32)
                            + d_ref[...] * g).astype(BF16)
            dbr_ref[...] += lax.dot_general(ub, lrb, _TN, preferred_element_type=F32)
            dbi_ref[...] += lax.dot_general(ub, lib, _TN, preferred_element_type=F32)
            dcr_ref[...] += lax.dot_general(sr_ref[r, :], gb, _TN, preferred_element_type=F32)
            dci_ref[...] -= lax.dot_general(si_ref[r, :], gb, _TN, preferred_element_type=F32)
            return carry

        lax.fori_loop(0, nc, outputs, 0)

    slab = pl.BlockSpec((S, LANES), lambda g: (0, g))
    states = pl.BlockSpec((S, ns), lambda g: (0, g))
    bspec = pl.BlockSpec((None, LANES, ns), lambda g: (g, 0, 0))
    cspec = pl.BlockSpec((None, ns, LANES), lambda g: (g, 0, 0))
    aspec = pl.BlockSpec((None, 1, ns), lambda g: (g, 0, 0))
    return pl.pallas_call(
        body,
        out_shape=[jax.ShapeDtypeStruct((S, SW), BF16),
                   jax.ShapeDtypeStruct((nslab, LANES, ns), F32), jax.ShapeDtypeStruct((nslab, LANES, ns), F32),
                   jax.ShapeDtypeStruct((nslab, ns, LANES), F32), jax.ShapeDtypeStruct((nslab, ns, LANES), F32),
                   jax.ShapeDtypeStruct((nslab, 1, ns), F32), jax.ShapeDtypeStruct((nslab, 1, ns), F32)],
        grid=(nslab,), in_specs=[slab, slab, states, states] + _slab_specs(ns),
        out_specs=[slab, bspec, bspec, cspec, cspec, aspec, aspec],
        scratch_shapes=[pltpu.VMEM((S, ns), F32)] * 2,
        compiler_params=_params(("parallel",), VMEM_LIMIT_SCAN), name=name)(
            u, dy, st_r, st_i, bbr, bbi, ar, ai, cbr, cbi, dsk)


def _discretise(lam_re, lam_im, log_dt, b_re, b_im):
    dt = jnp.exp(log_dt)[:, None]
    mag = jnp.exp(lam_re * dt)
    ar = mag * jnp.cos(lam_im * dt)
    ai = mag * jnp.sin(lam_im * dt)
    nr, ni = ar - 1.0, ai
    den = lam_re * lam_re + lam_im * lam_im
    cr = ((nr * lam_re + ni * lam_im) / den)[..., None]
    ci = ((ni * lam_re - nr * lam_im) / den)[..., None]
    return ar, ai, cr * b_re - ci * b_im, cr * b_im + ci * b_re


def _block_diag(t, nslab):
    G, R, C = t.shape
    eye = jnp.eye(SLAB_GROUPS, dtype=t.dtype)
    t = t.reshape(nslab, SLAB_GROUPS, R, C)
    return jnp.einsum('sgrc,gh->sgrhc', t, eye).reshape(nslab, SLAB_GROUPS * R, SLAB_GROUPS * C)


def _block_diag_part(t, R, C):
    nslab = t.shape[0]
    eye = jnp.eye(SLAB_GROUPS, dtype=t.dtype)
    t = t.reshape(nslab, SLAB_GROUPS, R, SLAB_GROUPS, C)
    return jnp.einsum('sgrhc,gh->sgrc', t, eye).reshape(nslab * SLAB_GROUPS, R, C)


def _place():
    return lax.axis_index("x"), lax.axis_index("y"), lax.axis_index("c")


_HBM = pl.BlockSpec(memory_space=pltpu.HBM)
_SEM = pl.BlockSpec(memory_space=pltpu.SEMAPHORE)
_ORDERED_EFFECT = pltpu.SideEffectType.DATAFLOW_SIDE_EFFECTING


def _split_call(name, srcs, zones, sems_in, n_new, body_fn, after):
    nsrc, nz, ns, nn = len(srcs), len(zones), len(sems_in), len(n_new)
    nb = nsrc + nz

    def body(*refs):
        outs = refs[nb + ns + 1:]
        body_fn(refs[:nb], refs[nb:nb + ns], outs[:nn])
        outs[nn + nz][...] = jnp.zeros((SUBLANES, LANES), F32)

    res = pl.pallas_call(
        body, name=name,
        out_shape=([pltpu.SemaphoreType.DMA((n,)) for n in n_new] + [pltpu.HBM(b.shape, b.dtype) for b in zones]
                   + [jax.ShapeDtypeStruct((SUBLANES, LANES), F32)]),
        in_specs=[_HBM] * nb + [_SEM] * ns + [_ANY],
        out_specs=[_SEM] * nn + [_HBM] * nz + [pl.BlockSpec(memory_space=pltpu.VMEM)],
        input_output_aliases={nsrc + i: nn + i for i in range(nz)},
        compiler_params=pltpu.CompilerParams(has_side_effects=_ORDERED_EFFECT))(
            *[pltpu.with_memory_space_constraint(b, pltpu.HBM) for b in list(srcs) + list(zones)], *sems_in, after)
    return list(res[:nn]), list(res[nn:nn + nz]), res[-1]


def _mesh_peers():
    x, y, c = _place()
    return x, y, c, (x, y, 1 - c), [(1 - x, y), (x, 1 - y), (1 - x, 1 - y)]


def gather_start(shards, after, *, name):
    nw = len(shards)
    x, y, c = _place()
    zones = [lax.dynamic_update_slice(lax.empty((N_DEV,) + s.shape, s.dtype), s[None], (4 * x + 2 * y + c, 0, 0))
             for s in shards]

    def body(bufs, taken, new):
        for cp in _gather_first(bufs, nw, new[0], new[1]):
            cp.start()

    sems, zones, token = _split_call(name, shards, zones, [], [4 * nw, 4 * nw], body, after)
    return shards, sems, zones, token


def _gather_first(bufs, nw, send, recv):
    x, y, c, sibling, chips = _mesh_peers()
    out = []
    for w in range(nw):
        slot = bufs[nw + w].at[4 * x + 2 * y + c]
        for k, to in enumerate([sibling] + [(*ch, c) for ch in chips]):
            out.append(pltpu.make_async_remote_copy(
                src_ref=bufs[w], dst_ref=slot, send_sem=send.at[4 * w + k], recv_sem=recv.at[4 * w + k],
                device_id=to, device_id_type=MESH))
    return out


def _gather_slot_copy(bufs, nw, w, block, send_sem, recv_sem, to):
    px, py, pc = block
    slot = bufs[nw + w].at[4 * px + 2 * py + pc]
    return pltpu.make_async_remote_copy(src_ref=slot, dst_ref=slot, send_sem=send_sem, recv_sem=recv_sem,
                                        device_id=to, device_id_type=MESH)


def gather_forward(state, after, *, name):
    shards, sems, zones, _ = state
    nw = len(shards)

    def body(bufs, taken, new):
        x, y, c, sibling, chips = _mesh_peers()
        for j, ch in enumerate(chips):
            for w in range(nw):
                k = 4 * w + 1 + j
                _gather_slot_copy(bufs, nw, w, (*ch, c), taken[0].at[k], taken[1].at[k], (*ch, c)).wait_recv()
                _gather_slot_copy(bufs, nw, w, (*ch, c), new[0].at[3 * w + j], new[1].at[3 * w + j], sibling).start()
        for w in range(nw):
            _gather_slot_copy(bufs, nw, w, sibling, taken[0].at[4 * w], taken[1].at[4 * w], sibling).wait_recv()
        for cp in _gather_first(bufs, nw, taken[0], taken[1]):
            cp.wait_send()

    sems, zones, token = _split_call(name, shards, zones, sems, [3 * nw, 3 * nw], body, after)
    return shards, sems, zones, token


def gather_finish(state, after, *, name):
    shards, sems, zones, _ = state
    nw = len(shards)

    def body(bufs, taken, new):
        x, y, c, sibling, chips = _mesh_peers()
        for w in range(nw):
            for j, ch in enumerate(chips):
                cp = _gather_slot_copy(bufs, nw, w, (*ch, 1 - c), taken[0].at[3 * w + j], taken[1].at[3 * w + j], sibling)
                cp.wait_send()
                cp.wait_recv()

    _, zones, _ = _split_call(name, shards, zones, sems, [], body, after)
    return zones


def exchange_start(srcs, zone_shapes, copies, n, after, *, name):
    nw = len(srcs)
    zones = [lax.empty(z, s.dtype) for z, s in zip(zone_shapes, srcs)]

    def body(bufs, taken, new):
        for cp in copies(bufs[:nw], bufs[nw:], new[0], new[1]):
            cp.start()

    sems, zones, token = _split_call(name, srcs, zones, [], [n, n], body, after)
    return srcs, copies, sems, zones, token


def exchange_wait(state, after, *, name):
    srcs, copies, sems, zones, _ = state
    nw = len(srcs)

    def body(bufs, taken, new):
        for cp in copies(bufs[:nw], bufs[nw:], taken[0], taken[1]):
            cp.wait_send()
            cp.wait_recv()

    _, zones, _ = _split_call(name, srcs, zones, sems, [], body, after)
    return zones


def _core_copies(srcs, zones, send, recv):
    x, y, c = _place()
    return [pltpu.make_async_remote_copy(
        src_ref=srcs[w].at[:, 1 - c], dst_ref=zones[w], send_sem=send.at[w], recv_sem=recv.at[w],
        device_id=(x, y, 1 - c), device_id_type=MESH) for w in range(len(srcs))]


def _chip_copies(srcs, zones, send, recv):
    x, y, c = _place()
    chips = [(1 - x, y), (x, 1 - y), (1 - x, 1 - y)]
    return [pltpu.make_async_remote_copy(
        src_ref=srcs[w].at[2 * cx + cy], dst_ref=zones[w].at[j], send_sem=send.at[3 * w + j],
        recv_sem=recv.at[3 * w + j], device_id=(cx, cy, c), device_id_type=MESH)
        for w in range(len(srcs)) for j, (cx, cy) in enumerate(chips)]


def _blocked(fn, ins, outs, *, name, place=None, tr=256):
    k, n = outs[0][0]
    tr = _tile(k, tr, 16)
    if place is None:
        place = jnp.zeros((1,), jnp.int32)
    specs = []
    args = []
    for a in ins:
        if isinstance(a, tuple):
            arr, lead = a
            specs.append(pl.BlockSpec((None, tr, n), functools.partial(lambda i, s, lead: (*lead(i, s), 0), lead=lead)))
            args.append(arr)
        else:
            specs.append(pl.BlockSpec((tr, n), lambda i, s: (i, 0)))
            args.append(a)
    nin = len(args)

    def body(place_ref, *refs):
        res = fn(*[r[...] for r in refs[:nin]])
        for ref, val in zip(refs[nin:], res):
            ref[...] = val.astype(ref.dtype)

    return pl.pallas_call(
        body, out_shape=[jax.ShapeDtypeStruct(s, d) for s, d in outs],
        grid_spec=pltpu.PrefetchScalarGridSpec(
            num_scalar_prefetch=1, grid=(k // tr,), in_specs=specs,
            out_specs=[pl.BlockSpec((tr, n), lambda i, s: (i, 0)) for _ in outs]),
        compiler_params=_params(("parallel",)), name=name)(place, *args)


def _adamw(w, g, m, v):
    m = ADAM_B1 * m + (1.0 - ADAM_B1) * g
    v = ADAM_B2 * v + (1.0 - ADAM_B2) * (g * g)
    m_hat = m / (1.0 - ADAM_B1 ** ADAM_STEP)
    v_hat = v / (1.0 - ADAM_B2 ** ADAM_STEP)
    delta = -ADAM_LR * (m_hat * pl.reciprocal(jnp.sqrt(v_hat) + ADAM_EPS, approx=True) + ADAM_WD * w)
    return delta, m, v


def kernel(x, p, mix_norm_pre, w_in, lam_re, lam_im, log_dt, ssm_b_re, ssm_b_im, ssm_c_re, ssm_c_im, ssm_d, w_glu, b_glu, attn_out_norm, ssm_out_norm, w_out, mix_norm_post, mlp_norm_pre, w_up, w_down, mlp_norm_post, ple_norm_pre, w_ple_gate, w_ple_proj, ple_norm_post, loss_target, m_mix_norm_pre, m_w_in, m_lam_re, m_lam_im, m_log_dt, m_ssm_b_re, m_ssm_b_im, m_ssm_c_re, m_ssm_c_im, m_ssm_d, m_w_glu, m_b_glu, m_attn_out_norm, m_ssm_out_norm, m_w_out, m_mix_norm_post, m_mlp_norm_pre, m_w_up, m_w_down, m_mlp_norm_post, m_ple_norm_pre, m_w_ple_gate, m_w_ple_proj, m_ple_norm_post, v_mix_norm_pre, v_w_in, v_lam_re, v_lam_im, v_log_dt, v_ssm_b_re, v_ssm_b_im, v_ssm_c_re, v_ssm_c_im, v_ssm_d, v_w_glu, v_b_glu, v_attn_out_norm, v_ssm_out_norm, v_w_out, v_mix_norm_post, v_mlp_norm_pre, v_w_up, v_w_down, v_mlp_norm_post, v_ple_norm_pre, v_w_ple_gate, v_w_ple_proj, v_ple_norm_post):
    weights = dict(mix_norm_pre=mix_norm_pre, w_in=w_in, lam_re=lam_re, lam_im=lam_im, log_dt=log_dt, ssm_b_re=ssm_b_re, ssm_b_im=ssm_b_im, ssm_c_re=ssm_c_re, ssm_c_im=ssm_c_im, ssm_d=ssm_d, w_glu=w_glu, b_glu=b_glu, attn_out_norm=attn_out_norm, ssm_out_norm=ssm_out_norm, w_out=w_out, mix_norm_post=mix_norm_post, mlp_norm_pre=mlp_norm_pre, w_up=w_up, w_down=w_down, mlp_norm_post=mlp_norm_post, ple_norm_pre=ple_norm_pre, w_ple_gate=w_ple_gate, w_ple_proj=w_ple_proj, ple_norm_post=ple_norm_post)
    mom_m = dict(mix_norm_pre=m_mix_norm_pre, w_in=m_w_in, lam_re=m_lam_re, lam_im=m_lam_im, log_dt=m_log_dt, ssm_b_re=m_ssm_b_re, ssm_b_im=m_ssm_b_im, ssm_c_re=m_ssm_c_re, ssm_c_im=m_ssm_c_im, ssm_d=m_ssm_d, w_glu=m_w_glu, b_glu=m_b_glu, attn_out_norm=m_attn_out_norm, ssm_out_norm=m_ssm_out_norm, w_out=m_w_out, mix_norm_post=m_mix_norm_post, mlp_norm_pre=m_mlp_norm_pre, w_up=m_w_up, w_down=m_w_down, mlp_norm_post=m_mlp_norm_post, ple_norm_pre=m_ple_norm_pre, w_ple_gate=m_w_ple_gate, w_ple_proj=m_w_ple_proj, ple_norm_post=m_ple_norm_post)
    mom_v = dict(mix_norm_pre=v_mix_norm_pre, w_in=v_w_in, lam_re=v_lam_re, lam_im=v_lam_im, log_dt=v_log_dt, ssm_b_re=v_ssm_b_re, ssm_b_im=v_ssm_b_im, ssm_c_re=v_ssm_c_re, ssm_c_im=v_ssm_c_im, ssm_d=v_ssm_d, w_glu=v_w_glu, b_glu=v_b_glu, attn_out_norm=v_attn_out_norm, ssm_out_norm=v_ssm_out_norm, w_out=v_w_out, mix_norm_post=v_mix_norm_post, mlp_norm_pre=v_mlp_norm_pre, w_up=v_w_up, w_down=v_w_down, mlp_norm_post=v_mlp_norm_post, ple_norm_pre=v_ple_norm_pre, w_ple_gate=v_w_ple_gate, w_ple_proj=v_w_ple_proj, ple_norm_post=v_ple_norm_post)
    order = list(weights)
    big = ["w_in", "w_glu", "w_out", "w_up", "w_down", "w_ple_gate", "w_ple_proj"]
    col_sharded = {"w_in", "w_up", "w_ple_proj"}
    small = [n for n in order if n not in big]

    _, S, D = x.shape
    xs = x[0]
    tgt = loss_target[0]
    AW = attn_out_norm.shape[1]
    SW = ssm_d.shape[1]
    H = AW // HEAD_DIM
    G = SW // SSM_GROUP
    nslab = G // SLAB_GROUPS
    P_, C_ = SSM_STATE, SSM_GROUP

    shard = {n: weights[n][0].astype(BF16) for n in big}
    W, WT = {}, {}

    def arrived(names, gathered):
        for n, g in zip(names, gathered):
            W[n] = g if n in col_sharded else g.reshape(1, N_DEV * g.shape[1], g.shape[2])

    def transposed(g):
        return jnp.swapaxes(g, 1, 2).reshape(1, g.shape[0] * g.shape[2], g.shape[1])

    g1, g2, g3, g4, g5, g6 = (weights[n] for n in ("mix_norm_pre", "mix_norm_post", "mlp_norm_pre",
                                                      "mlp_norm_post", "ple_norm_pre", "ple_norm_post"))
    ga, gs = attn_out_norm, ssm_out_norm
    gather_in = gather_start([shard["w_in"]], shard["w_in"], name="gather_w_in_start")
    (hn1,) = rowwise(lambda a, g: (_rms(a, g),), [xs], [g1], [(D, BF16)], deps=(gather_in[-1],), name="norm_in")
    gather_in = gather_forward(gather_in, hn1, name="gather_w_in_forward")
    arrived(["w_in"], gather_finish(gather_in, gather_in[-1], name="gather_w_in_finish"))
    WT["w_in"] = transposed(W["w_in"])
    early, mid, late = ["w_glu", "w_out"], ["w_up"], ["w_down", "w_ple_gate", "w_ple_proj"]
    gather_early = gather_start([shard[n] for n in early], W["w_in"], name="gather_early_start")
    gather_mid = gather_start([shard[n] for n in mid], gather_early[-1], name="gather_mid_start")
    gather_late = gather_start([shard[n] for n in late], gather_mid[-1], name="gather_late_start")

    (proj,) = mm_nn(hn1, W["w_in"], [F32], deps=(gather_late[-1],), name="proj_in")
    attn, lse = attn_fwd(proj, H, name="attn_fwd")
    gather_early = gather_forward(gather_early, attn, name="gather_early_forward")
    (mix_a,) = rowwise(lambda a, g: (_rms(a, g),), [attn], [ga], [(AW, BF16)], deps=(gather_early[-1],),
                       name="attn_norm")
    arrived(early, gather_finish(gather_early, mix_a, name="gather_early_finish"))

    a_r, a_i, bb_r, bb_i = _discretise(lam_re[0], lam_im[0], log_dt[0], ssm_b_re[0], ssm_b_im[0])
    ssm_consts = (_block_diag(bb_r.swapaxes(1, 2), nslab).astype(BF16), _block_diag(bb_i.swapaxes(1, 2), nslab).astype(BF16),
                  a_r.reshape(nslab, 1, SLAB_STATES), a_i.reshape(nslab, 1, SLAB_STATES),
                  _block_diag(ssm_c_re[0].swapaxes(1, 2), nslab).astype(BF16),
                  _block_diag(ssm_c_im[0].swapaxes(1, 2), nslab).astype(BF16), ssm_d)
    u_seg = _to_segments(proj[:, 3 * AW:]).astype(BF16)
    y_pre, st_r, st_i = ssm_fwd(u_seg, *ssm_consts, name="ssm_fwd")
    gather_mid = gather_forward(gather_mid, y_pre, name="gather_mid_forward")
    (yg,) = rowwise(lambda a: (_gelu(a),), [y_pre], [], [(SW, BF16)], deps=(gather_mid[-1],), name="ssm_gelu")
    (gl1,) = mm_nn(yg, W["w_glu"], [BF16], epi=lambda acc, b: (acc + b,), bias=b_glu, name="glu_gate")
    (mix_s,) = rowwise(lambda yp, gl, g: (_rms(_gelu(yp) * _sigmoid(gl), g),), [y_pre, gl1], [gs], [(SW, BF16)],
                       name="ssm_glu_norm")
    mixed = jnp.concatenate([mix_a, _from_segments(mix_s)], axis=1)
    (mo,) = mm_nn(mixed, W["w_out"], [BF16], name="mix_out")

    def resid_norm(h, t, gpost, gpre):
        hh = h + _rms(t, gpost)
        return hh, _rms(hh, gpre)

    h1, hn2 = rowwise(resid_norm, [xs, mo], [g2, g3], [(D, F32), (D, BF16)], name="resid_mix")
    arrived(mid, gather_finish(gather_mid, hn2, name="gather_mid_finish"))
    gather_late = gather_forward(gather_late, W["w_up"], name="gather_late_forward")
    WT["w_up"] = transposed(W["w_up"])

    def relu2(acc):
        r = jnp.maximum(acc, 0.0)
        return acc, r * r

    up, act = mm_nn(hn2, W["w_up"], [BF16, BF16], epi=relu2, deps=(gather_late[-1],), tm=1024, tn=1024, name="mlp_up")
    arrived(late, gather_finish(gather_late, act, name="gather_late_finish"))
    (ff,) = mm_nn(act, W["w_down"], [BF16], name="mlp_down")
    h2, hn3 = rowwise(resid_norm, [h1, ff], [g4, g5], [(D, F32), (D, BF16)], name="resid_mlp")
    (gl2,) = mm_nn(hn3, W["w_ple_gate"], [BF16], name="ple_gate")
    pb = p[0, 0].astype(BF16)
    (emb,) = mm_nn(pb, W["w_ple_proj"], [BF16], name="ple_proj")

    def head(h, gl, e, t, g):
        sg = _sigmoid(gl)
        ge = sg * e
        err = h + _rms(ge, g) - t
        dh = err * (1.0 / D)
        dge, dg = _rms_bwd(dh, ge, g)
        return dh, dge * e * sg * (1.0 - sg), dge * sg, jnp.sum(err * err, axis=0, keepdims=True), dg

    dh3, dgl2, demb, loss_part, dg6 = rowwise(head, [h2, gl2, emb, tgt], [g6], [(D, F32), (D, BF16), (D, BF16)],
                                             [D, D], name="ple_loss_head")
    loss = lax.psum(0.5 / D * jnp.sum(loss_part), ("x", "y", "c"))

    x_i, y_i, c_i = _place()
    place = jnp.stack([c_i, 2 * x_i + y_i]).astype(jnp.int32)
    grads, out_g, out_d, out_m, out_v = {}, {}, {}, {}, {}

    def to_sibling(names, after, tag):
        chunks = []
        for n in names:
            g = grads[n]
            g = g if n in col_sharded else g.reshape(N_DEV, g.shape[1] // N_DEV, g.shape[2])
            chunks.append(g.reshape(4, 2, g.shape[1], g.shape[2]))
        return chunks, exchange_start(chunks, [(4,) + g.shape[2:] for g in chunks], _core_copies, len(chunks), after,
                                      name=f"grads_to_sibling_{tag}")

    def to_chips(names, sent, after, tag):
        chunks, state = sent
        sums = []
        for n, g, r in zip(names, chunks, exchange_wait(state, after, name=f"grads_from_sibling_{tag}")):
            k, nn = g.shape[2], g.shape[3]
            kb = k // _tile(k, 512, 16)

            def mine(i, s, kb=kb):
                return 2 * (i // kb) + s[0], i % kb

            (s,) = _blocked(lambda a, b: (a.astype(F32) + b.astype(F32),),
                            [(g.reshape(N_DEV, k, nn), mine), r.reshape(4 * k, nn)],
                            [((4 * k, nn), BF16)], place=place, tr=k // kb, name=f"chip_sum_{n}")
            sums.append(s.reshape(4, k, nn))
        return sums, exchange_start(sums, [(3,) + s.shape[1:] for s in sums], _chip_copies, 3 * len(sums), sums[-1],
                                    name=f"grads_to_chips_{tag}")

    def update(w_, m_, v_, own, r0, r1, r2):
        g = own.astype(F32) + r0.astype(F32) + r1.astype(F32) + r2.astype(F32)
        return (g,) + _adamw(w_, g, m_, v_)

    def finish(names, sent, after, tag):
        sums, state = sent
        for n, s, r in zip(names, sums, exchange_wait(state, after, name=f"grads_from_chips_{tag}")):
            shp = weights[n].shape
            res = _blocked(update, [weights[n][0], mom_m[n][0], mom_v[n][0], (s, lambda i, p_: (p_[1], i)),
                                    (r, lambda i, p_: (0, i)), (r, lambda i, p_: (1, i)), (r, lambda i, p_: (2, i))],
                           [(shp[1:], F32)] * 4, place=place, tr=max(16, min(shp[1] // 8, 262144 // shp[2])),
                           name=f"adamw_{n}")
            out_g[n], out_d[n], out_m[n], out_v[n] = (t.reshape(shp) for t in res)
        return out_v[names[-1]]

    grads["w_ple_proj"] = mm_tn(pb, demb, N_DEV, name="grad_w_ple_proj")
    dhn3 = mm_nt(dgl2, W["w_ple_gate"], BF16, name="back_ple_gate")
    grads["w_ple_gate"] = mm_tn(hn3, dgl2, 1, name="grad_w_ple_gate")

    def back_resid(dh, dhn, h, t, gpre, gpost):
        d1, dgpre = _rms_bwd(dhn, h, gpre)
        dhh = dh + d1
        dt, dgpost = _rms_bwd(dhh, t, gpost)
        return dhh, dt, dgpre, dgpost

    dh2, dff, dg5, dg4 = rowwise(back_resid, [dh3, dhn3, h2, ff], [g5, g4], [(D, F32), (D, BF16)], [D, D],
                                 name="back_resid_mlp")
    dup = mm_nt(dff, W["w_down"], BF16, epi=lambda acc, u_: (acc * 2.0 * jnp.maximum(u_.astype(F32), 0.0),),
                extra=up, name="back_mlp_down")
    grads["w_down"] = mm_tn(act, dff, 1, name="grad_w_down")
    group_a = ["w_ple_proj", "w_ple_gate", "w_down"]
    sent_a = to_sibling(group_a, grads["w_down"], "a")
    (dhn2,) = mm_nn(dup, WT["w_up"], [BF16], deps=(sent_a[1][-1],), name="back_mlp_up")
    sent_a = to_chips(group_a, sent_a, dhn2, "a")
    grads["w_up"] = mm_tn(hn2, dup, N_DEV, deps=(sent_a[1][-1],), name="grad_w_up")
    dh1, dmo, dg3, dg2 = rowwise(back_resid, [dh2, dhn2, h1, mo], [g3, g2], [(D, F32), (D, BF16)], [D, D],
                                 name="back_resid_mix")
    dmixed = mm_nt(dmo, W["w_out"], BF16, name="back_mix_out")
    grads["w_out"] = mm_tn(mixed, dmo, 1, name="grad_w_out")

    def back_glu(dm, yp, gl, g):
        ygf = _gelu(yp)
        sg = _sigmoid(gl)
        dssm, dg = _rms_bwd(dm, ygf * sg, g)
        dgl = dssm * ygf * sg * (1.0 - sg)
        return dgl, dssm * sg, dg, jnp.sum(dgl, axis=0, keepdims=True)

    dgl1, dyg_direct, dgs, db_glu = rowwise(back_glu, [_to_segments(dmixed[:, AW:]), y_pre, gl1], [gs],
                                            [(SW, BF16), (SW, F32)], [SW, SW], name="back_glu")
    dyg_gate = mm_nt(dgl1, W["w_glu"], BF16, name="back_glu_gate")
    grads["w_glu"] = mm_tn(yg, dgl1, 1, name="grad_w_glu")
    group_b = ["w_up", "w_out", "w_glu"]
    sent_b = to_sibling(group_b, grads["w_glu"], "b")
    done_a = finish(group_a, sent_a, sent_b[1][-1], "a")

    def back_gelu(d1, d2, yp, u_):
        dy = (d1 + d2) * _gelu_grad(yp)
        return dy, jnp.sum(dy * u_.astype(F32), axis=0, keepdims=True)

    dy_pre, d_skip = rowwise(back_gelu, [dyg_direct, dyg_gate, y_pre, u_seg], [], [(SW, F32)], [SW], deps=(done_a,),
                             name="back_gelu")
    du_seg, dbb_r, dbb_i, dcb_r, dcb_i, da_r, da_i = ssm_bwd(u_seg, dy_pre, st_r, st_i, *ssm_consts, name="ssm_bwd")
    sent_b = to_chips(group_b, sent_b, du_seg, "b")

    def back_attn_norm(dm, a, g):
        da, dg = _rms_bwd(dm, a, g)
        prod = da * a
        delta = jnp.concatenate(
            [jnp.broadcast_to(jnp.sum(prod[:, h * HEAD_DIM:(h + 1) * HEAD_DIM], axis=-1, keepdims=True),
                              (prod.shape[0], HEAD_DIM)) for h in range(H)], axis=1)
        return da, delta, dg

    dattn, delta, dga = rowwise(back_attn_norm, [(dmixed, AW, 0), attn], [ga], [(AW, F32), (AW, F32)], [AW],
                                deps=(sent_b[1][-1],), name="back_attn_norm")
    dq, dk, dv = attn_bwd(proj, dattn, lse, delta, H, name="attn_bwd")
    dproj = jnp.concatenate([dq, dk, dv, _from_segments(du_seg)], axis=1)
    (dhn1,) = mm_nn(dproj, WT["w_in"], [BF16], name="back_proj_in")

    def back_in(dh, dhn, a, g):
        d1, dg = _rms_bwd(dhn, a, g)
        return dh + d1, dg

    grad_x, dg1 = rowwise(back_in, [dh1, dhn1, xs], [g1], [(D, F32)], [D], name="back_norm_in")

    cot = dict(
        mix_norm_pre=dg1, mix_norm_post=dg2, mlp_norm_pre=dg3, mlp_norm_post=dg4, ple_norm_pre=dg5, ple_norm_post=dg6,
        attn_out_norm=dga, ssm_out_norm=dgs, b_glu=db_glu, ssm_d=d_skip,
        ssm_c_re=_block_diag_part(dcb_r, P_, C_).swapaxes(1, 2), ssm_c_im=_block_diag_part(dcb_i, P_, C_).swapaxes(1, 2),
        a_r=da_r.reshape(G, P_), a_i=da_i.reshape(G, P_),
        bb_r=_block_diag_part(dbb_r, C_, P_).swapaxes(1, 2), bb_i=_block_diag_part(dbb_i, C_, P_).swapaxes(1, 2))
    names = list(cot)
    flat = jnp.concatenate([cot[n].reshape(-1) for n in names])
    total = flat.shape[0]
    rows_ = -(-total // (LANES * 16)) * 16
    flat = jnp.pad(flat, (0, rows_ * LANES - total)).reshape(rows_, LANES)
    gather_small = gather_start([flat], flat, name="gather_small_start")
    grads["w_in"] = mm_tn(hn1, dproj, N_DEV, deps=(gather_small[-1],), tko=2048, name="grad_w_in")
    group_c = ["w_in"]
    sent_c = to_sibling(group_c, grads["w_in"], "c")
    done_b = finish(group_b, sent_b, sent_c[1][-1], "b")
    sent_c = to_chips(group_c, sent_c, done_b, "c")
    gather_small = gather_forward(gather_small, sent_c[1][-1], name="gather_small_forward")
    (every,) = gather_finish(gather_small, gather_small[-1], name="gather_small_finish")
    (summed,) = _blocked(lambda *t: (functools.reduce(lambda a, b: a + b, t),),
                         [(every, functools.partial(lambda i, p_, j: (j, i), j=j)) for j in range(N_DEV)],
                         [((rows_, LANES), F32)], name="sum_small_grads")
    summed = summed.reshape(-1)
    red, off = {}, 0
    for n in names:
        sz = cot[n].size
        red[n] = summed[off:off + sz].reshape(cot[n].shape)
        off += sz
    _, pull = jax.vjp(_discretise, lam_re[0], lam_im[0], log_dt[0], ssm_b_re[0], ssm_b_im[0])
    d_lre, d_lim, d_ldt, d_bre, d_bim = pull((red["a_r"], red["a_i"], red["bb_r"], red["bb_i"]))
    red.update(lam_re=d_lre, lam_im=d_lim, log_dt=d_ldt, ssm_b_re=d_bre, ssm_b_im=d_bim)

    def pack(d):
        t = jnp.concatenate([d[n].reshape(-1) for n in small])
        r_ = -(-t.shape[0] // (LANES * 16)) * 16
        return jnp.pad(t, (0, r_ * LANES - t.shape[0])).reshape(r_, LANES)

    sw, sg_, sm, sv = pack(weights), pack(red), pack(mom_m), pack(mom_v)
    sd, snm, snv = _blocked(lambda w_, g_, m_, v_: _adamw(w_, g_, m_, v_), [sw, sg_, sm, sv],
                            [(sw.shape, F32)] * 3, name="adamw_small")
    finish(group_c, sent_c, snv, "c")
    off = 0
    for n in small:
        sz = weights[n].size
        shp = weights[n].shape
        out_g[n] = red[n].reshape(shp)
        out_d[n] = sd.reshape(-1)[off:off + sz].reshape(shp)
        out_m[n] = snm.reshape(-1)[off:off + sz].reshape(shp)
        out_v[n] = snv.reshape(-1)[off:off + sz].reshape(shp)
        off += sz

    return (loss, grad_x[None], *[out_g[n] for n in order], *[out_d[n] for n in order],
            *[out_m[n] for n in order], *[out_v[n] for n in order])
```

```python
import functools
import math

import jax
import jax.numpy as jnp
from jax import lax
from jax.experimental import pallas as pl
from jax.experimental.pallas import tpu as pltpu

F32 = jnp.float32
BF16 = jnp.bfloat16
MESH = pl.DeviceIdType.MESH

N_DEV = 8
LANES = 128
SUBLANES = 8
VMEM_LIMIT = 48 * 1024 * 1024
VMEM_LIMIT_SCAN = 60 * 1024 * 1024

HEAD_DIM = 128
BLK = 128
DILATIONS = (1, 4, 16)
SSM_GROUP = 16
SSM_STATE = 64
SLAB_GROUPS = LANES // SSM_GROUP
SLAB_STATES = SLAB_GROUPS * SSM_STATE
SEGMENTS = SUBLANES
SCAN_UNROLL = 4
RMS_EPS = 1e-6
NEG_INF = -1e30

ADAM_LR = 0.001
ADAM_B1 = 0.9
ADAM_B2 = 0.999
ADAM_EPS = 1e-08
ADAM_WD = 0.01
ADAM_STEP = 10


def _tile(n, pref, unit=LANES):
    if n <= pref:
        return n
    t = (pref // unit) * unit
    while t > unit and n % t:
        t -= unit
    assert n % t == 0, (n, pref, unit)
    return t


def _params(sem=None, vmem=VMEM_LIMIT):
    return pltpu.CompilerParams(dimension_semantics=sem, vmem_limit_bytes=vmem)


_NN = (((1,), (0,)), ((), ()))
_NT = (((1,), (1,)), ((), ()))
_TN = (((0,), (0,)), ((), ()))


_ANY = pl.BlockSpec(memory_space=pl.ANY)


def _mm_call(dims, nk, n_extra, n_dep, n_out, epi, **kw):
    first_out = 2 + n_extra + n_dep
    kw["in_specs"] = list(kw["in_specs"]) + [_ANY] * n_dep

    def single(*refs):
        extra = refs[2:2 + n_extra]
        res = epi(lax.dot_general(refs[0][...], refs[1][...], dims, preferred_element_type=F32),
                  *[e[...] for e in extra])
        for o, r in zip(refs[first_out:first_out + n_out], res):
            o[...] = r.astype(o.dtype)

    if nk == 1:
        kw["scratch_shapes"] = []
        return pl.pallas_call(single, **kw)

    def body(*refs):
        a_ref, b_ref = refs[0], refs[1]
        extra = refs[2:2 + n_extra]
        outs = refs[first_out:first_out + n_out]
        acc = refs[-1]
        k = pl.program_id(2)

        @pl.when(k == 0)
        def _():
            acc[...] = jnp.zeros_like(acc)

        acc[...] += lax.dot_general(a_ref[...], b_ref[...], dims, preferred_element_type=F32)

        @pl.when(k == nk - 1)
        def _():
            res = epi(acc[...], *[e[...] for e in extra])
            for o, r in zip(outs, res):
                o[...] = r.astype(o.dtype)

    return pl.pallas_call(body, **kw)


def _identity_epi(acc):
    return (acc,)


def mm_nn(a, w, out_dtypes, *, name, epi=_identity_epi, bias=None, deps=(), tm=2048, tn=512, tk=2048):
    M, K = a.shape
    J, K2, n = w.shape
    assert K == K2
    tm, tn, tk = _tile(M, tm, 16), _tile(n, tn), _tile(K, tk)
    npj = n // tn
    nk = K // tk
    in_specs = [pl.BlockSpec((tm, tk), lambda i, j, k: (i, k)),
                pl.BlockSpec((None, tk, tn), lambda i, j, k: (j // npj, k, j % npj))]
    args = [a, w]
    if bias is not None:
        in_specs.append(pl.BlockSpec((1, tn), lambda i, j, k: (0, j)))
        args.append(bias)
    return _mm_call(
        _NN, nk, len(args) - 2, len(deps), len(out_dtypes), epi,
        out_shape=[jax.ShapeDtypeStruct((M, J * n), d) for d in out_dtypes],
        grid=(M // tm, J * npj, nk), in_specs=in_specs,
        out_specs=[pl.BlockSpec((tm, tn), lambda i, j, k: (i, j)) for _ in out_dtypes],
        scratch_shapes=[pltpu.VMEM((tm, tn), F32)],
        compiler_params=_params(("parallel", "parallel", "arbitrary")), name=name)(*args, *deps)


def mm_nt(a, w, out_dtype, *, name, epi=_identity_epi, extra=None, tm=2048, tko=512, tnr=2048):
    M, N = a.shape
    J, K, n = w.shape
    assert N == J * n
    tm, tko, tnr = _tile(M, tm, 16), _tile(K, tko), _tile(n, tnr)
    npj = n // tnr
    nk = N // tnr
    in_specs = [pl.BlockSpec((tm, tnr), lambda i, j, k: (i, k)),
                pl.BlockSpec((None, tko, tnr), lambda i, j, k: (k // npj, j, k % npj))]
    args = [a, w]
    if extra is not None:
        in_specs.append(pl.BlockSpec((tm, tko), lambda i, j, k: (i, j)))
        args.append(extra)
    return _mm_call(
        _NT, nk, len(args) - 2, 0, 1, epi,
        out_shape=[jax.ShapeDtypeStruct((M, K), out_dtype)],
        grid=(M // tm, K // tko, nk), in_specs=in_specs,
        out_specs=[pl.BlockSpec((tm, tko), lambda i, j, k: (i, j))],
        scratch_shapes=[pltpu.VMEM((tm, tko), F32)],
        compiler_params=_params(("parallel", "parallel", "arbitrary")), name=name)(*args)[0]


def mm_tn(a, b, J, *, name, deps=(), tko=1024, tn=1024, ts=2048):
    S, K = a.shape
    S2, N = b.shape
    assert S == S2 and N % J == 0
    n = N // J
    tko, tn, ts = _tile(K, tko), _tile(n, tn), _tile(S, ts)
    npj = n // tn
    nk = S // ts
    return _mm_call(
        _TN, nk, 0, len(deps), 1, _identity_epi,
        out_shape=[jax.ShapeDtypeStruct((J, K, n), BF16)],
        grid=(K // tko, J * npj, nk),
        in_specs=[pl.BlockSpec((ts, tko), lambda i, j, k: (k, i)),
                  pl.BlockSpec((ts, tn), lambda i, j, k: (k, j))],
        out_specs=[pl.BlockSpec((None, tko, tn), lambda i, j, k: (j // npj, i, j % npj))],
        scratch_shapes=[pltpu.VMEM((tko, tn), F32)],
        compiler_params=_params(("parallel", "parallel", "arbitrary")), name=name)(a, b, *deps)[0]


def rowwise(fn, rows, vecs, outs, accs=(), *, name, deps=(), ts=256):
    rows = [r if isinstance(r, tuple) else (r, r.shape[1], 0) for r in rows]
    S = rows[0][0].shape[0]
    ts = _tile(S, ts, 16)
    nr, nv, no, nd = len(rows), len(vecs), len(outs), len(deps)

    def body(*refs):
        r, v = refs[:nr], refs[nr:nr + nv]
        o, a = refs[nr + nv + nd:nr + nv + nd + no], refs[nr + nv + nd + no:]
        res = fn(*[t[...].astype(F32) for t in r], *[t[...] for t in v])
        for ref, val in zip(o, res[:no]):
            ref[...] = val.astype(ref.dtype)
        if a:
            @pl.when(pl.program_id(0) == 0)
            def _():
                for ref in a:
                    ref[...] = jnp.zeros_like(ref)

            for ref, val in zip(a, res[no:]):
                ref[...] += val

    in_specs = [pl.BlockSpec((ts, w), functools.partial(lambda i, cb: (i, cb), cb=cb)) for _, w, cb in rows]
    in_specs += [pl.BlockSpec(v.shape, lambda i: (0, 0)) for v in vecs] + [_ANY] * nd
    out_shape = [jax.ShapeDtypeStruct((S, w), d) for w, d in outs]
    out_shape += [jax.ShapeDtypeStruct((1, w), F32) for w in accs]
    out_specs = [pl.BlockSpec((ts, w), lambda i: (i, 0)) for w, _ in outs]
    out_specs += [pl.BlockSpec((1, w), lambda i: (0, 0)) for w in accs]
    return pl.pallas_call(body, out_shape=out_shape, grid=(S // ts,), in_specs=in_specs, out_specs=out_specs,
                          compiler_params=_params(("arbitrary",)), name=name)(*[r[0] for r in rows], *vecs, *deps)


def _rms(x, g):
    r = lax.rsqrt(jnp.mean(x * x, axis=-1, keepdims=True) + RMS_EPS)
    return x * r * g


def _rms_bwd(dy, x, g):
    r = lax.rsqrt(jnp.mean(x * x, axis=-1, keepdims=True) + RMS_EPS)
    xh = x * r
    dxh = dy * g
    dx = r * (dxh - xh * jnp.mean(dxh * xh, axis=-1, keepdims=True))
    return dx, jnp.sum(dy * xh, axis=0, keepdims=True)


def _sigmoid(x):
    return pl.reciprocal(1.0 + jnp.exp(-x), approx=True)


_GELU_C = math.sqrt(2.0 / math.pi)


def _gelu(x):
    return 0.5 * x * (1.0 + jnp.tanh(_GELU_C * (x + 0.044715 * x * x * x)))


def _gelu_grad(x):
    t = jnp.tanh(_GELU_C * (x + 0.044715 * x * x * x))
    return 0.5 * (1.0 + t) + 0.5 * x * (1.0 - t * t) * _GELU_C * (1.0 + 3.0 * 0.044715 * x * x)


ATTN_INTERLEAVE = 8
KEY_PAD = BLK * max(DILATIONS)


def _key_mask(n):
    ii = lax.broadcasted_iota(jnp.int32, (BLK, 2 * BLK), 0)
    jj = lax.broadcasted_iota(jnp.int32, (BLK, 2 * BLK), 1)
    return ((jj < BLK) & (jj >= ii) & (n > 0)) | ((jj >= BLK) & (jj - BLK <= ii))


def _units(d, nblk):
    nb = nblk // d
    if nb == 2:
        def unit(idx):
            ii = lax.broadcasted_iota(jnp.int32, (2 * BLK, 2 * BLK), 0)
            jj = lax.broadcasted_iota(jnp.int32, (2 * BLK, 2 * BLK), 1)
            return pl.ds(idx, 2 * BLK, stride=d), pl.ds(KEY_PAD + idx, 2 * BLK, stride=d), (jj <= ii) & (ii - jj <= BLK)
        return d, max(1, ATTN_INTERLEAVE // 4), unit

    def unit(idx):
        r, n = idx // nb, idx % nb
        cur = r + n * (BLK * d)
        keys = cur + (KEY_PAD - BLK * d)
        if d == 1:
            return pl.ds(pl.multiple_of(cur, BLK), BLK), pl.ds(pl.multiple_of(keys, BLK), 2 * BLK), _key_mask(n)
        return pl.ds(cur, BLK, stride=d), pl.ds(keys, 2 * BLK, stride=d), _key_mask(n)
    return nblk, ATTN_INTERLEAVE, unit


def _pad_keys(dst, src):
    dst[pl.ds(0, KEY_PAD), :] = jnp.zeros((KEY_PAD, dst.shape[1]), F32)

    def copy(c, carry):
        dst[pl.ds(pl.multiple_of(KEY_PAD + c * BLK, BLK), BLK), :] = src[pl.ds(pl.multiple_of(c * BLK, BLK), BLK), :]
        return carry

    lax.fori_loop(0, src.shape[0] // BLK, copy, 0)


def attn_fwd(proj, n_heads, *, name):
    S, WP = proj.shape
    assert S % (BLK * max(DILATIONS)) == 0
    nblk = S // BLK
    AW = n_heads * HEAD_DIM
    scale = 1.0 / math.sqrt(HEAD_DIM)

    def body(q_ref, k_ref, v_ref, o_ref, l_ref, acc, mrun, lrun, kp, vp):
        _pad_keys(kp, k_ref)
        _pad_keys(vp, v_ref)
        for first, d in zip((True, False, False), reversed(DILATIONS)):
            n_units, per_step, unit = _units(d, nblk)

            def step(it, carry, first=first, n_units=n_units, per_step=per_step, unit=unit):
                units = [unit(it + j * (n_units // per_step)) for j in range(per_step)]
                ss = [lax.dot_general(q_ref[cur, :].astype(BF16), kp[keys, :].astype(BF16), _NT,
                                      preferred_element_type=F32) * scale for cur, keys, _ in units]
                ss = [jnp.where(mask, s, NEG_INF) for s, (_, _, mask) in zip(ss, units)]
                ms = [jnp.max(s, axis=-1, keepdims=True) for s in ss]
                ps = [jnp.exp(s - m) for s, m in zip(ss, ms)]
                ls = [jnp.sum(p, axis=-1, keepdims=True) for p in ps]
                os_ = [jnp.dot(p.astype(BF16), vp[keys, :].astype(BF16), preferred_element_type=F32)
                       for p, (_, keys, _) in zip(ps, units)]
                for (cur, keys, mask), m, l, o in zip(units, ms, ls, os_):
                    m = jnp.broadcast_to(m, o.shape)
                    l = jnp.broadcast_to(l, o.shape)
                    if first:
                        acc[cur, :], mrun[cur, :], lrun[cur, :] = o, m, l
                    else:
                        m_old = mrun[cur, :]
                        m_new = jnp.maximum(m_old, m)
                        w_old, w_blk = jnp.exp(m_old - m_new), jnp.exp(m - m_new)
                        acc[cur, :] = w_old * acc[cur, :] + w_blk * o
                        lrun[cur, :] = w_old * lrun[cur, :] + w_blk * l
                        mrun[cur, :] = m_new
                return carry

            lax.fori_loop(0, n_units // per_step, step, 0)

        def finish(c, carry):
            r = pl.ds(pl.multiple_of(c * BLK, BLK), BLK)
            o_ref[r, :] = acc[r, :] / lrun[r, :]
            l_ref[r, :] = mrun[r, :] + jnp.log(lrun[r, :])
            return carry

        lax.fori_loop(0, nblk, finish, 0)

    def col(off):
        return pl.BlockSpec((S, HEAD_DIM), lambda h: (0, off + h))

    ospec = pl.BlockSpec((S, HEAD_DIM), lambda h: (0, h))
    return pl.pallas_call(
        body, out_shape=[jax.ShapeDtypeStruct((S, AW), F32)] * 2, grid=(n_heads,),
        in_specs=[col(0), col(n_heads), col(2 * n_heads)], out_specs=[ospec, ospec],
        scratch_shapes=[pltpu.VMEM((S, HEAD_DIM), F32)] * 3 + [pltpu.VMEM((KEY_PAD + S, HEAD_DIM), F32)] * 2,
        compiler_params=_params(("parallel",)), name=name)(proj, proj, proj)


def attn_bwd(proj, do, lse, delta, n_heads, *, name):
    S, WP = proj.shape
    nblk = S // BLK
    AW = n_heads * HEAD_DIM
    scale = 1.0 / math.sqrt(HEAD_DIM)

    def body(q_ref, k_ref, v_ref, do_ref, l_ref, dl_ref, dq_ref, dk_ref, dv_ref, dq_sc, dk_sc, dv_sc, kp, vp):
        _pad_keys(kp, k_ref)
        _pad_keys(vp, v_ref)
        dq_sc[...] = jnp.zeros_like(dq_sc)
        dk_sc[...] = jnp.zeros_like(dk_sc)
        dv_sc[...] = jnp.zeros_like(dv_sc)
        for d in DILATIONS:
            n_units, per_step, unit = _units(d, nblk)

            def step(it, carry, n_units=n_units, per_step=per_step, unit=unit):
                units = [unit(it + j * (n_units // per_step)) for j in range(per_step)]
                qs = [q_ref[cur, :].astype(BF16) for cur, _, _ in units]
                gs = [do_ref[cur, :].astype(BF16) for cur, _, _ in units]
                ks = [kp[keys, :].astype(BF16) for _, keys, _ in units]
                ss = [lax.dot_general(q, kb, _NT, preferred_element_type=F32) * scale for q, kb in zip(qs, ks)]
                dps = [lax.dot_general(g, vp[keys, :].astype(BF16), _NT, preferred_element_type=F32)
                       for g, (_, keys, _) in zip(gs, units)]
                ps = [jnp.where(mask, jnp.exp(s - l_ref[cur, :][:, :1]), 0.0) for s, (cur, _, mask) in zip(ss, units)]
                dss = [(p * (dp - dl_ref[cur, :][:, :1]) * scale).astype(BF16)
                       for p, dp, (cur, _, _) in zip(ps, dps, units)]
                for (cur, keys, _), q, g, kb, p, ds in zip(units, qs, gs, ks, ps, dss):
                    dq_sc[cur, :] += jnp.dot(ds, kb, preferred_element_type=F32)
                    dk_sc[keys, :] += lax.dot_general(ds, q, _TN, preferred_element_type=F32)
                    dv_sc[keys, :] += lax.dot_general(p.astype(BF16), g, _TN, preferred_element_type=F32)
                return carry

            lax.fori_loop(0, n_units // per_step, step, 0)
        rows = pl.ds(KEY_PAD, S)
        dq_ref[...] = dq_sc[...].astype(BF16)
        dk_ref[...] = dk_sc[rows, :].astype(BF16)
        dv_ref[...] = dv_sc[rows, :].astype(BF16)

    def col(off):
        return pl.BlockSpec((S, HEAD_DIM), lambda h: (0, off + h))

    ospec = pl.BlockSpec((S, HEAD_DIM), lambda h: (0, h))
    return pl.pallas_call(
        body, out_shape=[jax.ShapeDtypeStruct((S, AW), BF16)] * 3, grid=(n_heads,),
        in_specs=[col(0), col(n_heads), col(2 * n_heads), ospec, ospec, ospec], out_specs=[ospec] * 3,
        scratch_shapes=[pltpu.VMEM((S, HEAD_DIM), F32)] + [pltpu.VMEM((KEY_PAD + S, HEAD_DIM), F32)] * 4,
        compiler_params=_params(("parallel",), VMEM_LIMIT_SCAN), name=name)(proj, proj, proj, do, lse, delta)


def _to_segments(t):
    S, W = t.shape
    return t.reshape(SEGMENTS, S // SEGMENTS, W).swapaxes(0, 1).reshape(S, W)


def _from_segments(t):
    S, W = t.shape
    return t.reshape(S // SEGMENTS, SEGMENTS, W).swapaxes(0, 1).reshape(S, W)


def _cmul(ar, ai, br, bi):
    return ar * br - ai * bi, ar * bi + ai * br


def _power(ar, ai, log2n):
    for _ in range(log2n):
        ar, ai = _cmul(ar, ai, ar, ai)
    return ar, ai


def _shift_rows(x, up):
    row = lax.broadcasted_iota(jnp.int32, x.shape, 0)
    if up:
        return jnp.where(row == SEGMENTS - 1, 0.0, pltpu.roll(x, SEGMENTS - 1, 0))
    return jnp.where(row == 0, 0.0, pltpu.roll(x, 1, 0))


def _segment_carries(er, ei, pr, pi, up):
    cr = jnp.zeros_like(er)
    ci = jnp.zeros_like(ei)
    for _ in range(SEGMENTS - 1):
        tr, ti = _cmul(pr, pi, cr, ci)
        cr, ci = _shift_rows(er + tr, up), _shift_rows(ei + ti, up)
    return cr, ci


def _scan_states(sr, si, ar, ai, T, reverse):
    ns = sr.shape[1]
    ar8 = jnp.broadcast_to(ar, (SEGMENTS, ns))
    ai8 = jnp.broadcast_to(ai, (SEGMENTS, ns))

    def rows(t):
        k = (T - 1 - t) if reverse else t
        return pl.ds(pl.multiple_of(k * SEGMENTS, SEGMENTS), SEGMENTS)

    def advance(t, c):
        tr, ti = _cmul(ar8, ai8, c[0], c[1])
        return tr + sr[rows(t), :], ti + si[rows(t), :]

    def several(step):
        def trip(t, c):
            for j in range(SCAN_UNROLL):
                c = step(t * SCAN_UNROLL + j, c)
            return c
        return trip

    zero = jnp.zeros((SEGMENTS, ns), F32)
    er, ei = lax.fori_loop(0, T // SCAN_UNROLL, several(advance), (zero, zero))
    pr, pi = _power(ar, ai, T.bit_length() - 1)
    cr, ci = _segment_carries(er, ei, jnp.broadcast_to(pr, (SEGMENTS, ns)), jnp.broadcast_to(pi, (SEGMENTS, ns)), reverse)

    def store(t, c):
        nr, ni = advance(t, c)
        sr[rows(t), :] = nr
        si[rows(t), :] = ni
        return nr, ni

    lax.fori_loop(0, T // SCAN_UNROLL, several(store), (cr, ci))
    return cr, ci


def _slab_specs(ns):
    return [pl.BlockSpec((None, LANES, ns), lambda g: (g, 0, 0)),
            pl.BlockSpec((None, LANES, ns), lambda g: (g, 0, 0)),
            pl.BlockSpec((None, 1, ns), lambda g: (g, 0, 0)),
            pl.BlockSpec((None, 1, ns), lambda g: (g, 0, 0)),
            pl.BlockSpec((None, ns, LANES), lambda g: (g, 0, 0)),
            pl.BlockSpec((None, ns, LANES), lambda g: (g, 0, 0)),
            pl.BlockSpec((1, LANES), lambda g: (0, g))]


def _chunks(S):
    rc = _tile(S, 512, 16)
    return rc, S // rc


def ssm_fwd(u, bbr, bbi, ar, ai, cbr, cbi, dsk, *, name):
    S, SW = u.shape
    nslab, _, ns = bbr.shape
    T = S // SEGMENTS
    assert T & (T - 1) == 0
    rc, nc = _chunks(S)

    def body(u_ref, br_ref, bi_ref, ar_ref, ai_ref, cr_ref, ci_ref, d_ref, y_ref, str_ref, sti_ref, sr, si):
        def inputs(c, carry):
            r = pl.ds(pl.multiple_of(c * rc, rc), rc)
            sr[r, :] = jnp.dot(u_ref[r, :], br_ref[...], preferred_element_type=F32)
            si[r, :] = jnp.dot(u_ref[r, :], bi_ref[...], preferred_element_type=F32)
            return carry

        lax.fori_loop(0, nc, inputs, 0)
        _scan_states(sr, si, ar_ref[...], ai_ref[...], T, False)

        def outputs(c, carry):
            r = pl.ds(pl.multiple_of(c * rc, rc), rc)
            srb, sib = sr[r, :].astype(BF16), si[r, :].astype(BF16)
            str_ref[r, :] = srb
            sti_ref[r, :] = sib
            y_ref[r, :] = (jnp.dot(srb, cr_ref[...], preferred_element_type=F32)
                           - jnp.dot(sib, ci_ref[...], preferred_element_type=F32)
                           + d_ref[...] * u_ref[r, :].astype(F32))
            return carry

        lax.fori_loop(0, nc, outputs, 0)

    slab = pl.BlockSpec((S, LANES), lambda g: (0, g))
    states = pl.BlockSpec((S, ns), lambda g: (0, g))
    return pl.pallas_call(
        body, out_shape=[jax.ShapeDtypeStruct((S, SW), F32)] + [jax.ShapeDtypeStruct((S, nslab * ns), BF16)] * 2,
        grid=(nslab,), in_specs=[slab] + _slab_specs(ns), out_specs=[slab, states, states],
        scratch_shapes=[pltpu.VMEM((S, ns), F32)] * 2,
        compiler_params=_params(("parallel",), VMEM_LIMIT_SCAN), name=name)(u, bbr, bbi, ar, ai, cbr, cbi, dsk)


def ssm_bwd(u, dy, st_r, st_i, bbr, bbi, ar, ai, cbr, cbi, dsk, *, name):
    S, SW = u.shape
    nslab, _, ns = bbr.shape
    T = S // SEGMENTS
    rc, nc = _chunks(S)
    pair_rows = 2 * SEGMENTS

    def body(u_ref, dy_ref, sr_ref, si_ref, br_ref, bi_ref, ar_ref, ai_ref, cr_ref, ci_ref, d_ref,
             du_ref, dbr_ref, dbi_ref, dcr_ref, dci_ref, dar_ref, dai_ref, lr, li):
        def inputs(c, carry):
            r = pl.ds(pl.multiple_of(c * rc, rc), rc)
            gb = dy_ref[r, :].astype(BF16)
            lr[r, :] = lax.dot_general(gb, cr_ref[...], _NT, preferred_element_type=F32)
            li[r, :] = -lax.dot_general(gb, ci_ref[...], _NT, preferred_element_type=F32)
            return carry

        lax.fori_loop(0, nc, inputs, 0)
        _scan_states(lr, li, ar_ref[...], -ai_ref[...], T, True)

        def steps(j):
            rows = pl.ds(pl.multiple_of(j * pair_rows, pair_rows), pair_rows)
            tr, ti = sr_ref[rows, :].astype(F32), si_ref[rows, :].astype(F32)
            return tr[:SEGMENTS], tr[SEGMENTS:], ti[:SEGMENTS], ti[SEGMENTS:]

        def pair(j, c):
            acc_r, acc_i, pr, pi = c
            lo_r, hi_r, lo_i, hi_i = steps(j)
            first = pl.ds(pl.multiple_of(j * pair_rows, SEGMENTS), SEGMENTS)
            second = pl.ds(pl.multiple_of(j * pair_rows + SEGMENTS, SEGMENTS), SEGMENTS)
            la_r, la_i, lb_r, lb_i = lr[first, :], li[first, :], lr[second, :], li[second, :]
            return (acc_r + la_r * pr + la_i * pi + lb_r * lo_r + lb_i * lo_i,
                    acc_i - la_r * pi + la_i * pr - lb_r * lo_i + lb_i * lo_r, hi_r, hi_i)

        def pairs(t, c):
            return pair(2 * t + 1, pair(2 * t, c))

        _, end_r, _, end_i = steps(T // 2 - 1)
        zero = jnp.zeros((SEGMENTS, ns), F32)
        acc = lax.fori_loop(0, T // 4, pairs, (zero, zero, _shift_rows(end_r, False), _shift_rows(end_i, False)))
        dar_ref[...] = jnp.sum(acc[0], axis=0, keepdims=True)
        dai_ref[...] = jnp.sum(acc[1], axis=0, keepdims=True)

        dbr_ref[...] = jnp.zeros_like(dbr_ref)
        dbi_ref[...] = jnp.zeros_like(dbi_ref)
        dcr_ref[...] = jnp.zeros_like(dcr_ref)
        dci_ref[...] = jnp.zeros_like(dci_ref)

        def outputs(c, carry):
            r = pl.ds(pl.multiple_of(c * rc, rc), rc)
            ub = u_ref[r, :]
            g = dy_ref[r, :]
            gb = g.astype(BF16)
            lrb = lr[r, :].astype(BF16)
            lib = li[r, :].astype(BF16)
            du_ref[r, :] = (lax.dot_general(lrb, br_ref[...], _NT, preferred_element_type=F32)
                            + lax.dot_general(lib, bi_ref[...], _NT, preferred_element_type=F32)
                            + d_ref[...] * g).astype(BF16)
            dbr_ref[...] += lax.dot_general(ub, lrb, _TN, preferred_element_type=F32)
            dbi_ref[...] += lax.dot_general(ub, lib, _TN, preferred_element_type=F32)
            dcr_ref[...] += lax.dot_general(sr_ref[r, :], gb, _TN, preferred_element_type=F32)
            dci_ref[...] -= lax.dot_general(si_ref[r, :], gb, _TN, preferred_element_type=F32)
            return carry

        lax.fori_loop(0, nc, outputs, 0)

    slab = pl.BlockSpec((S, LANES), lambda g: (0, g))
    states = pl.BlockSpec((S, ns), lambda g: (0, g))
    bspec = pl.BlockSpec((None, LANES, ns), lambda g: (g, 0, 0))
    cspec = pl.BlockSpec((None, ns, LANES), lambda g: (g, 0, 0))
    aspec = pl.BlockSpec((None, 1, ns), lambda g: (g, 0, 0))
    return pl.pallas_call(
        body,
        out_shape=[jax.ShapeDtypeStruct((S, SW), BF16),
                   jax.ShapeDtypeStruct((nslab, LANES, ns), F32), jax.ShapeDtypeStruct((nslab, LANES, ns), F32),
                   jax.ShapeDtypeStruct((nslab, ns, LANES), F32), jax.ShapeDtypeStruct((nslab, ns, LANES), F32),
                   jax.ShapeDtypeStruct((nslab, 1, ns), F32), jax.ShapeDtypeStruct((nslab, 1, ns), F32)],
        grid=(nslab,), in_specs=[slab, slab, states, states] + _slab_specs(ns),
        out_specs=[slab, bspec, bspec, cspec, cspec, aspec, aspec],
        scratch_shapes=[pltpu.VMEM((S, ns), F32)] * 2,
        compiler_params=_params(("parallel",), VMEM_LIMIT_SCAN), name=name)(
            u, dy, st_r, st_i, bbr, bbi, ar, ai, cbr, cbi, dsk)


def _discretise(lam_re, lam_im, log_dt, b_re, b_im):
    dt = jnp.exp(log_dt)[:, None]
    mag = jnp.exp(lam_re * dt)
    ar = mag * jnp.cos(lam_im * dt)
    ai = mag * jnp.sin(lam_im * dt)
    nr, ni = ar - 1.0, ai
    den = lam_re * lam_re + lam_im * lam_im
    cr = ((nr * lam_re + ni * lam_im) / den)[..., None]
    ci = ((ni * lam_re - nr * lam_im) / den)[..., None]
    return ar, ai, cr * b_re - ci * b_im, cr * b_im + ci * b_re


def _block_diag(t, nslab):
    G, R, C = t.shape
    eye = jnp.eye(SLAB_GROUPS, dtype=t.dtype)
    t = t.reshape(nslab, SLAB_GROUPS, R, C)
    return jnp.einsum('sgrc,gh->sgrhc', t, eye).reshape(nslab, SLAB_GROUPS * R, SLAB_GROUPS * C)


def _block_diag_part(t, R, C):
    nslab = t.shape[0]
    eye = jnp.eye(SLAB_GROUPS, dtype=t.dtype)
    t = t.reshape(nslab, SLAB_GROUPS, R, SLAB_GROUPS, C)
    return jnp.einsum('sgrhc,gh->sgrc', t, eye).reshape(nslab * SLAB_GROUPS, R, C)


def _place():
    return lax.axis_index("x"), lax.axis_index("y"), lax.axis_index("c")


_HBM = pl.BlockSpec(memory_space=pltpu.HBM)
_SEM = pl.BlockSpec(memory_space=pltpu.SEMAPHORE)
_ORDERED_EFFECT = pltpu.SideEffectType.DATAFLOW_SIDE_EFFECTING


def _split_call(name, srcs, zones, sems_in, n_new, body_fn, after):
    nsrc, nz, ns, nn = len(srcs), len(zones), len(sems_in), len(n_new)
    nb = nsrc + nz

    def body(*refs):
        outs = refs[nb + ns + 1:]
        body_fn(refs[:nb], refs[nb:nb + ns], outs[:nn])
        outs[nn + nz][...] = jnp.zeros((SUBLANES, LANES), F32)

    res = pl.pallas_call(
        body, name=name,
        out_shape=([pltpu.SemaphoreType.DMA((n,)) for n in n_new] + [pltpu.HBM(b.shape, b.dtype) for b in zones]
                   + [jax.ShapeDtypeStruct((SUBLANES, LANES), F32)]),
        in_specs=[_HBM] * nb + [_SEM] * ns + [_ANY],
        out_specs=[_SEM] * nn + [_HBM] * nz + [pl.BlockSpec(memory_space=pltpu.VMEM)],
        input_output_aliases={nsrc + i: nn + i for i in range(nz)},
        compiler_params=pltpu.CompilerParams(has_side_effects=_ORDERED_EFFECT))(
            *[pltpu.with_memory_space_constraint(b, pltpu.HBM) for b in list(srcs) + list(zones)], *sems_in, after)
    return list(res[:nn]), list(res[nn:nn + nz]), res[-1]


def _mesh_peers():
    x, y, c = _place()
    return x, y, c, (x, y, 1 - c), [(1 - x, y), (x, 1 - y), (1 - x, 1 - y)]


def gather_start(shards, after, *, name):
    nw = len(shards)
    x, y, c = _place()
    zones = [lax.dynamic_update_slice(lax.empty((N_DEV,) + s.shape, s.dtype), s[None], (4 * x + 2 * y + c, 0, 0))
             for s in shards]

    def body(bufs, taken, new):
        for cp in _gather_first(bufs, nw, new[0], new[1]):
            cp.start()

    sems, zones, token = _split_call(name, shards, zones, [], [4 * nw, 4 * nw], body, after)
    return shards, sems, zones, token


def _gather_first(bufs, nw, send, recv):
    x, y, c, sibling, chips = _mesh_peers()
    out = []
    for w in range(nw):
        slot = bufs[nw + w].at[4 * x + 2 * y + c]
        for k, to in enumerate([sibling] + [(*ch, c) for ch in chips]):
            out.append(pltpu.make_async_remote_copy(
                src_ref=bufs[w], dst_ref=slot, send_sem=send.at[4 * w + k], recv_sem=recv.at[4 * w + k],
                device_id=to, device_id_type=MESH))
    return out


def _gather_slot_copy(bufs, nw, w, block, send_sem, recv_sem, to):
    px, py, pc = block
    slot = bufs[nw + w].at[4 * px + 2 * py + pc]
    return pltpu.make_async_remote_copy(src_ref=slot, dst_ref=slot, send_sem=send_sem, recv_sem=recv_sem,
                                        device_id=to, device_id_type=MESH)


def gather_forward(state, after, *, name):
    shards, sems, zones, _ = state
    nw = len(shards)

    def body(bufs, taken, new):
        x, y, c, sibling, chips = _mesh_peers()
        for j, ch in enumerate(chips):
            for w in range(nw):
                k = 4 * w + 1 + j
                _gather_slot_copy(bufs, nw, w, (*ch, c), taken[0].at[k], taken[1].at[k], (*ch, c)).wait_recv()
                _gather_slot_copy(bufs, nw, w, (*ch, c), new[0].at[3 * w + j], new[1].at[3 * w + j], sibling).start()
        for w in range(nw):
            _gather_slot_copy(bufs, nw, w, sibling, taken[0].at[4 * w], taken[1].at[4 * w], sibling).wait_recv()
        for cp in _gather_first(bufs, nw, taken[0], taken[1]):
            cp.wait_send()

    sems, zones, token = _split_call(name, shards, zones, sems, [3 * nw, 3 * nw], body, after)
    return shards, sems, zones, token


def gather_finish(state, after, *, name):
    shards, sems, zones, _ = state
    nw = len(shards)

    def body(bufs, taken, new):
        x, y, c, sibling, chips = _mesh_peers()
        for w in range(nw):
            for j, ch in enumerate(chips):
                cp = _gather_slot_copy(bufs, nw, w, (*ch, 1 - c), taken[0].at[3 * w + j], taken[1].at[3 * w + j], sibling)
                cp.wait_send()
                cp.wait_recv()

    _, zones, _ = _split_call(name, shards, zones, sems, [], body, after)
    return zones


def exchange_start(srcs, zone_shapes, copies, n, after, *, name):
    nw = len(srcs)
    zones = [lax.empty(z, s.dtype) for z, s in zip(zone_shapes, srcs)]

    def body(bufs, taken, new):
        for cp in copies(bufs[:nw], bufs[nw:], new[0], new[1]):
            cp.start()

    sems, zones, token = _split_call(name, srcs, zones, [], [n, n], body, after)
    return srcs, copies, sems, zones, token


def exchange_wait(state, after, *, name):
    srcs, copies, sems, zones, _ = state
    nw = len(srcs)

    def body(bufs, taken, new):
        for cp in copies(bufs[:nw], bufs[nw:], taken[0], taken[1]):
            cp.wait_send()
            cp.wait_recv()

    _, zones, _ = _split_call(name, srcs, zones, sems, [], body, after)
    return zones


def _core_copies(srcs, zones, send, recv):
    x, y, c = _place()
    return [pltpu.make_async_remote_copy(
        src_ref=srcs[w].at[:, 1 - c], dst_ref=zones[w], send_sem=send.at[w], recv_sem=recv.at[w],
        device_id=(x, y, 1 - c), device_id_type=MESH) for w in range(len(srcs))]


def _chip_copies(srcs, zones, send, recv):
    x, y, c = _place()
    chips = [(1 - x, y), (x, 1 - y), (1 - x, 1 - y)]
    return [pltpu.make_async_remote_copy(
        src_ref=srcs[w].at[2 * cx + cy], dst_ref=zones[w].at[j], send_sem=send.at[3 * w + j],
        recv_sem=recv.at[3 * w + j], device_id=(cx, cy, c), device_id_type=MESH)
        for w in range(len(srcs)) for j, (cx, cy) in enumerate(chips)]


def _blocked(fn, ins, outs, *, name, place=None, tr=256):
    k, n = outs[0][0]
    tr = _tile(k, tr, 16)
    if place is None:
        place = jnp.zeros((1,), jnp.int32)
    specs = []
    args = []
    for a in ins:
        if isinstance(a, tuple):
            arr, lead = a
            specs.append(pl.BlockSpec((None, tr, n), functools.partial(lambda i, s, lead: (*lead(i, s), 0), lead=lead)))
            args.append(arr)
        else:
            specs.append(pl.BlockSpec((tr, n), lambda i, s: (i, 0)))
            args.append(a)
    nin = len(args)

    def body(place_ref, *refs):
        res = fn(*[r[...] for r in refs[:nin]])
        for ref, val in zip(refs[nin:], res):
            ref[...] = val.astype(ref.dtype)

    return pl.pallas_call(
        body, out_shape=[jax.ShapeDtypeStruct(s, d) for s, d in outs],
        grid_spec=pltpu.PrefetchScalarGridSpec(
            num_scalar_prefetch=1, grid=(k // tr,), in_specs=specs,
            out_specs=[pl.BlockSpec((tr, n), lambda i, s: (i, 0)) for _ in outs]),
        compiler_params=_params(("parallel",)), name=name)(place, *args)


def _adamw(w, g, m, v):
    m = ADAM_B1 * m + (1.0 - ADAM_B1) * g
    v = ADAM_B2 * v + (1.0 - ADAM_B2) * (g * g)
    m_hat = m / (1.0 - ADAM_B1 ** ADAM_STEP)
    v_hat = v / (1.0 - ADAM_B2 ** ADAM_STEP)
    delta = -ADAM_LR * (m_hat * pl.reciprocal(jnp.sqrt(v_hat) + ADAM_EPS, approx=True) + ADAM_WD * w)
    return delta, m, v


def kernel(x, p, mix_norm_pre, w_in, lam_re, lam_im, log_dt, ssm_b_re, ssm_b_im, ssm_c_re, ssm_c_im, ssm_d, w_glu, b_glu, attn_out_norm, ssm_out_norm, w_out, mix_norm_post, mlp_norm_pre, w_up, w_down, mlp_norm_post, ple_norm_pre, w_ple_gate, w_ple_proj, ple_norm_post, loss_target, m_mix_norm_pre, m_w_in, m_lam_re, m_lam_im, m_log_dt, m_ssm_b_re, m_ssm_b_im, m_ssm_c_re, m_ssm_c_im, m_ssm_d, m_w_glu, m_b_glu, m_attn_out_norm, m_ssm_out_norm, m_w_out, m_mix_norm_post, m_mlp_norm_pre, m_w_up, m_w_down, m_mlp_norm_post, m_ple_norm_pre, m_w_ple_gate, m_w_ple_proj, m_ple_norm_post, v_mix_norm_pre, v_w_in, v_lam_re, v_lam_im, v_log_dt, v_ssm_b_re, v_ssm_b_im, v_ssm_c_re, v_ssm_c_im, v_ssm_d, v_w_glu, v_b_glu, v_attn_out_norm, v_ssm_out_norm, v_w_out, v_mix_norm_post, v_mlp_norm_pre, v_w_up, v_w_down, v_mlp_norm_post, v_ple_norm_pre, v_w_ple_gate, v_w_ple_proj, v_ple_norm_post):
    weights = dict(mix_norm_pre=mix_norm_pre, w_in=w_in, lam_re=lam_re, lam_im=lam_im, log_dt=log_dt, ssm_b_re=ssm_b_re, ssm_b_im=ssm_b_im, ssm_c_re=ssm_c_re, ssm_c_im=ssm_c_im, ssm_d=ssm_d, w_glu=w_glu, b_glu=b_glu, attn_out_norm=attn_out_norm, ssm_out_norm=ssm_out_norm, w_out=w_out, mix_norm_post=mix_norm_post, mlp_norm_pre=mlp_norm_pre, w_up=w_up, w_down=w_down, mlp_norm_post=mlp_norm_post, ple_norm_pre=ple_norm_pre, w_ple_gate=w_ple_gate, w_ple_proj=w_ple_proj, ple_norm_post=ple_norm_post)
    mom_m = dict(mix_norm_pre=m_mix_norm_pre, w_in=m_w_in, lam_re=m_lam_re, lam_im=m_lam_im, log_dt=m_log_dt, ssm_b_re=m_ssm_b_re, ssm_b_im=m_ssm_b_im, ssm_c_re=m_ssm_c_re, ssm_c_im=m_ssm_c_im, ssm_d=m_ssm_d, w_glu=m_w_glu, b_glu=m_b_glu, attn_out_norm=m_attn_out_norm, ssm_out_norm=m_ssm_out_norm, w_out=m_w_out, mix_norm_post=m_mix_norm_post, mlp_norm_pre=m_mlp_norm_pre, w_up=m_w_up, w_down=m_w_down, mlp_norm_post=m_mlp_norm_post, ple_norm_pre=m_ple_norm_pre, w_ple_gate=m_w_ple_gate, w_ple_proj=m_w_ple_proj, ple_norm_post=m_ple_norm_post)
    mom_v = dict(mix_norm_pre=v_mix_norm_pre, w_in=v_w_in, lam_re=v_lam_re, lam_im=v_lam_im, log_dt=v_log_dt, ssm_b_re=v_ssm_b_re, ssm_b_im=v_ssm_b_im, ssm_c_re=v_ssm_c_re, ssm_c_im=v_ssm_c_im, ssm_d=v_ssm_d, w_glu=v_w_glu, b_glu=v_b_glu, attn_out_norm=v_attn_out_norm, ssm_out_norm=v_ssm_out_norm, w_out=v_w_out, mix_norm_post=v_mix_norm_post, mlp_norm_pre=v_mlp_norm_pre, w_up=v_w_up, w_down=v_w_down, mlp_norm_post=v_mlp_norm_post, ple_norm_pre=v_ple_norm_pre, w_ple_gate=v_w_ple_gate, w_ple_proj=v_w_ple_proj, ple_norm_post=v_ple_norm_post)
    order = list(weights)
    big = ["w_in", "w_glu", "w_out", "w_up", "w_down", "w_ple_gate", "w_ple_proj"]
    col_sharded = {"w_in", "w_up", "w_ple_proj"}
    small = [n for n in order if n not in big]

    _, S, D = x.shape
    xs = x[0]
    tgt = loss_target[0]
    AW = attn_out_norm.shape[1]
    SW = ssm_d.shape[1]
    H = AW // HEAD_DIM
    G = SW // SSM_GROUP
    nslab = G // SLAB_GROUPS
    P_, C_ = SSM_STATE, SSM_GROUP

    shard = {n: weights[n][0].astype(BF16) for n in big}
    W, WT = {}, {}

    def arrived(names, gathered):
        for n, g in zip(names, gathered):
            W[n] = g if n in col_sharded else g.reshape(1, N_DEV * g.shape[1], g.shape[2])

    def transposed(g):
        return jnp.swapaxes(g, 1, 2).reshape(1, g.shape[0] * g.shape[2], g.shape[1])

    g1, g2, g3, g4, g5, g6 = (weights[n] for n in ("mix_norm_pre", "mix_norm_post", "mlp_norm_pre",
                                                      "mlp_norm_post", "ple_norm_pre", "ple_norm_post"))
    ga, gs = attn_out_norm, ssm_out_norm
    gather_in = gather_start([shard["w_in"]], shard["w_in"], name="gather_w_in_start")
    (hn1,) = rowwise(lambda a, g: (_rms(a, g),), [xs], [g1], [(D, BF16)], deps=(gather_in[-1],), name="norm_in")
    gather_in = gather_forward(gather_in, hn1, name="gather_w_in_forward")
    arrived(["w_in"], gather_finish(gather_in, gather_in[-1], name="gather_w_in_finish"))
    WT["w_in"] = transposed(W["w_in"])
    early, mid, late = ["w_glu", "w_out"], ["w_up"], ["w_down", "w_ple_gate", "w_ple_proj"]
    gather_early = gather_start([shard[n] for n in early], W["w_in"], name="gather_early_start")
    gather_mid = gather_start([shard[n] for n in mid], gather_early[-1], name="gather_mid_start")
    gather_late = gather_start([shard[n] for n in late], gather_mid[-1], name="gather_late_start")

    (proj,) = mm_nn(hn1, W["w_in"], [F32], deps=(gather_late[-1],), name="proj_in")
    attn, lse = attn_fwd(proj, H, name="attn_fwd")
    gather_early = gather_forward(gather_early, attn, name="gather_early_forward")
    (mix_a,) = rowwise(lambda a, g: (_rms(a, g),), [attn], [ga], [(AW, BF16)], deps=(gather_early[-1],),
                       name="attn_norm")
    arrived(early, gather_finish(gather_early, mix_a, name="gather_early_finish"))

    a_r, a_i, bb_r, bb_i = _discretise(lam_re[0], lam_im[0], log_dt[0], ssm_b_re[0], ssm_b_im[0])
    ssm_consts = (_block_diag(bb_r.swapaxes(1, 2), nslab).astype(BF16), _block_diag(bb_i.swapaxes(1, 2), nslab).astype(BF16),
                  a_r.reshape(nslab, 1, SLAB_STATES), a_i.reshape(nslab, 1, SLAB_STATES),
                  _block_diag(ssm_c_re[0].swapaxes(1, 2), nslab).astype(BF16),
                  _block_diag(ssm_c_im[0].swapaxes(1, 2), nslab).astype(BF16), ssm_d)
    u_seg = _to_segments(proj[:, 3 * AW:]).astype(BF16)
    y_pre, st_r, st_i = ssm_fwd(u_seg, *ssm_consts, name="ssm_fwd")
    gather_mid = gather_forward(gather_mid, y_pre, name="gather_mid_forward")
    (yg,) = rowwise(lambda a: (_gelu(a),), [y_pre], [], [(SW, BF16)], deps=(gather_mid[-1],), name="ssm_gelu")
    (gl1,) = mm_nn(yg, W["w_glu"], [BF16], epi=lambda acc, b: (acc + b,), bias=b_glu, name="glu_gate")
    (mix_s,) = rowwise(lambda yp, gl, g: (_rms(_gelu(yp) * _sigmoid(gl), g),), [y_pre, gl1], [gs], [(SW, BF16)],
                       name="ssm_glu_norm")
    mixed = jnp.concatenate([mix_a, _from_segments(mix_s)], axis=1)
    (mo,) = mm_nn(mixed, W["w_out"], [BF16], name="mix_out")

    def resid_norm(h, t, gpost, gpre):
        hh = h + _rms(t, gpost)
        return hh, _rms(hh, gpre)

    h1, hn2 = rowwise(resid_norm, [xs, mo], [g2, g3], [(D, F32), (D, BF16)], name="resid_mix")
    arrived(mid, gather_finish(gather_mid, hn2, name="gather_mid_finish"))
    gather_late = gather_forward(gather_late, W["w_up"], name="gather_late_forward")
    WT["w_up"] = transposed(W["w_up"])

    def relu2(acc):
        r = jnp.maximum(acc, 0.0)
        return acc, r * r

    up, act = mm_nn(hn2, W["w_up"], [BF16, BF16], epi=relu2, deps=(gather_late[-1],), name="mlp_up")
    arrived(late, gather_finish(gather_late, act, name="gather_late_finish"))
    (ff,) = mm_nn(act, W["w_down"], [BF16], name="mlp_down")
    h2, hn3 = rowwise(resid_norm, [h1, ff], [g4, g5], [(D, F32), (D, BF16)], name="resid_mlp")
    (gl2,) = mm_nn(hn3, W["w_ple_gate"], [BF16], name="ple_gate")
    pb = p[0, 0].astype(BF16)
    (emb,) = mm_nn(pb, W["w_ple_proj"], [BF16], name="ple_proj")

    def head(h, gl, e, t, g):
        sg = _sigmoid(gl)
        ge = sg * e
        r = lax.rsqrt(jnp.mean(ge * ge, axis=-1, keepdims=True) + RMS_EPS)
        xh = ge * r
        err = h + xh * g - t
        dh = err * (1.0 / D)
        dxh = dh * g
        dge = r * (dxh - xh * jnp.mean(dxh * xh, axis=-1, keepdims=True))
        return (dh, dge * e * sg * (1.0 - sg), dge * sg, jnp.sum(err * err, axis=0, keepdims=True),
                jnp.sum(dh * xh, axis=0, keepdims=True))

    dh3, dgl2, demb, loss_part, dg6 = rowwise(head, [h2, gl2, emb, tgt], [g6], [(D, F32), (D, BF16), (D, BF16)],
                                             [D, D], name="ple_loss_head")
    loss = lax.psum(0.5 / D * jnp.sum(loss_part), ("x", "y", "c"))

    x_i, y_i, c_i = _place()
    place = jnp.stack([c_i, 2 * x_i + y_i]).astype(jnp.int32)
    grads, out_g, out_d, out_m, out_v = {}, {}, {}, {}, {}

    def to_sibling(names, after, tag):
        chunks = []
        for n in names:
            g = grads[n]
            g = g if n in col_sharded else g.reshape(N_DEV, g.shape[1] // N_DEV, g.shape[2])
            chunks.append(g.reshape(4, 2, g.shape[1], g.shape[2]))
        return chunks, exchange_start(chunks, [(4,) + g.shape[2:] for g in chunks], _core_copies, len(chunks), after,
                                      name=f"grads_to_sibling_{tag}")

    def to_chips(names, sent, after, tag):
        chunks, state = sent
        sums = []
        for n, g, r in zip(names, chunks, exchange_wait(state, after, name=f"grads_from_sibling_{tag}")):
            k, nn = g.shape[2], g.shape[3]
            kb = k // _tile(k, 512, 16)

            def mine(i, s, kb=kb):
                return 2 * (i // kb) + s[0], i % kb

            (s,) = _blocked(lambda a, b: (a.astype(F32) + b.astype(F32),),
                            [(g.reshape(N_DEV, k, nn), mine), r.reshape(4 * k, nn)],
                            [((4 * k, nn), BF16)], place=place, tr=k // kb, name=f"chip_sum_{n}")
            sums.append(s.reshape(4, k, nn))
        return sums, exchange_start(sums, [(3,) + s.shape[1:] for s in sums], _chip_copies, 3 * len(sums), sums[-1],
                                    name=f"grads_to_chips_{tag}")

    def update(w_, m_, v_, own, r0, r1, r2):
        g = own.astype(F32) + r0.astype(F32) + r1.astype(F32) + r2.astype(F32)
        return (g,) + _adamw(w_, g, m_, v_)

    def finish(names, sent, after, tag):
        sums, state = sent
        for n, s, r in zip(names, sums, exchange_wait(state, after, name=f"grads_from_chips_{tag}")):
            shp = weights[n].shape
            res = _blocked(update, [weights[n][0], mom_m[n][0], mom_v[n][0], (s, lambda i, p_: (p_[1], i)),
                                    (r, lambda i, p_: (0, i)), (r, lambda i, p_: (1, i)), (r, lambda i, p_: (2, i))],
                           [(shp[1:], F32)] * 4, place=place, tr=max(16, min(shp[1] // 8, 262144 // shp[2])),
                           name=f"adamw_{n}")
            out_g[n], out_d[n], out_m[n], out_v[n] = (t.reshape(shp) for t in res)
        return out_v[names[-1]]

    grads["w_ple_proj"] = mm_tn(pb, demb, N_DEV, name="grad_w_ple_proj")
    dhn3 = mm_nt(dgl2, W["w_ple_gate"], BF16, name="back_ple_gate")
    grads["w_ple_gate"] = mm_tn(hn3, dgl2, 1, name="grad_w_ple_gate")

    def back_resid(dh, dhn, h, t, gpre, gpost):
        d1, dgpre = _rms_bwd(dhn, h, gpre)
        dhh = dh + d1
        dt, dgpost = _rms_bwd(dhh, t, gpost)
        return dhh, dt, dgpre, dgpost

    dh2, dff, dg5, dg4 = rowwise(back_resid, [dh3, dhn3, h2, ff], [g5, g4], [(D, F32), (D, BF16)], [D, D],
                                 name="back_resid_mlp")
    dup = mm_nt(dff, W["w_down"], BF16, epi=lambda acc, u_: (acc * 2.0 * jnp.maximum(u_.astype(F32), 0.0),),
                extra=up, name="back_mlp_down")
    grads["w_down"] = mm_tn(act, dff, 1, name="grad_w_down")
    group_a = ["w_ple_proj", "w_ple_gate", "w_down"]
    sent_a = to_sibling(group_a, grads["w_down"], "a")
    (dhn2,) = mm_nn(dup, WT["w_up"], [BF16], deps=(sent_a[1][-1],), name="back_mlp_up")
    sent_a = to_chips(group_a, sent_a, dhn2, "a")
    grads["w_up"] = mm_tn(hn2, dup, N_DEV, deps=(sent_a[1][-1],), name="grad_w_up")
    dh1, dmo, dg3, dg2 = rowwise(back_resid, [dh2, dhn2, h1, mo], [g3, g2], [(D, F32), (D, BF16)], [D, D],
                                 name="back_resid_mix")
    dmixed = mm_nt(dmo, W["w_out"], BF16, name="back_mix_out")
    grads["w_out"] = mm_tn(mixed, dmo, 1, name="grad_w_out")

    def back_glu(dm, yp, gl, g):
        ygf = _gelu(yp)
        sg = _sigmoid(gl)
        dssm, dg = _rms_bwd(dm, ygf * sg, g)
        dgl = dssm * ygf * sg * (1.0 - sg)
        return dgl, dssm * sg, dg, jnp.sum(dgl, axis=0, keepdims=True)

    dgl1, dyg_direct, dgs, db_glu = rowwise(back_glu, [_to_segments(dmixed[:, AW:]), y_pre, gl1], [gs],
                                            [(SW, BF16), (SW, F32)], [SW, SW], name="back_glu")
    dyg_gate = mm_nt(dgl1, W["w_glu"], BF16, name="back_glu_gate")
    grads["w_glu"] = mm_tn(yg, dgl1, 1, name="grad_w_glu")
    group_b = ["w_up", "w_out", "w_glu"]
    sent_b = to_sibling(group_b, grads["w_glu"], "b")
    done_a = finish(group_a, sent_a, sent_b[1][-1], "a")

    def back_gelu(d1, d2, yp, u_):
        dy = (d1 + d2) * _gelu_grad(yp)
        return dy, jnp.sum(dy * u_.astype(F32), axis=0, keepdims=True)

    dy_pre, d_skip = rowwise(back_gelu, [dyg_direct, dyg_gate, y_pre, u_seg], [], [(SW, F32)], [SW], deps=(done_a,),
                             name="back_gelu")
    du_seg, dbb_r, dbb_i, dcb_r, dcb_i, da_r, da_i = ssm_bwd(u_seg, dy_pre, st_r, st_i, *ssm_consts, name="ssm_bwd")
    sent_b = to_chips(group_b, sent_b, du_seg, "b")

    def back_attn_norm(dm, a, g):
        da, dg = _rms_bwd(dm, a, g)
        prod = da * a
        delta = jnp.concatenate(
            [jnp.broadcast_to(jnp.sum(prod[:, h * HEAD_DIM:(h + 1) * HEAD_DIM], axis=-1, keepdims=True),
                              (prod.shape[0], HEAD_DIM)) for h in range(H)], axis=1)
        return da, delta, dg

    dattn, delta, dga = rowwise(back_attn_norm, [(dmixed, AW, 0), attn], [ga], [(AW, F32), (AW, F32)], [AW],
                                deps=(sent_b[1][-1],), name="back_attn_norm")
    dq, dk, dv = attn_bwd(proj, dattn, lse, delta, H, name="attn_bwd")
    dproj = jnp.concatenate([dq, dk, dv, _from_segments(du_seg)], axis=1)
    (dhn1,) = mm_nn(dproj, WT["w_in"], [BF16], name="back_proj_in")

    def back_in(dh, dhn, a, g):
        d1, dg = _rms_bwd(dhn, a, g)
        return dh + d1, dg

    grad_x, dg1 = rowwise(back_in, [dh1, dhn1, xs], [g1], [(D, F32)], [D], name="back_norm_in")

    cot = dict(
        mix_norm_pre=dg1, mix_norm_post=dg2, mlp_norm_pre=dg3, mlp_norm_post=dg4, ple_norm_pre=dg5, ple_norm_post=dg6,
        attn_out_norm=dga, ssm_out_norm=dgs, b_glu=db_glu, ssm_d=d_skip,
        ssm_c_re=_block_diag_part(dcb_r, P_, C_).swapaxes(1, 2), ssm_c_im=_block_diag_part(dcb_i, P_, C_).swapaxes(1, 2),
        a_r=da_r.reshape(G, P_), a_i=da_i.reshape(G, P_),
        bb_r=_block_diag_part(dbb_r, C_, P_).swapaxes(1, 2), bb_i=_block_diag_part(dbb_i, C_, P_).swapaxes(1, 2))
    names = list(cot)
    flat = jnp.concatenate([cot[n].reshape(-1) for n in names])
    total = flat.shape[0]
    rows_ = -(-total // (LANES * 16)) * 16
    flat = jnp.pad(flat, (0, rows_ * LANES - total)).reshape(rows_, LANES)
    gather_small = gather_start([flat], flat, name="gather_small_start")
    grads["w_in"] = mm_tn(hn1, dproj, N_DEV, deps=(gather_small[-1],), tko=2048, name="grad_w_in")
    group_c = ["w_in"]
    sent_c = to_sibling(group_c, grads["w_in"], "c")
    done_b = finish(group_b, sent_b, sent_c[1][-1], "b")
    sent_c = to_chips(group_c, sent_c, done_b, "c")
    gather_small = gather_forward(gather_small, sent_c[1][-1], name="gather_small_forward")
    (every,) = gather_finish(gather_small, gather_small[-1], name="gather_small_finish")
    (summed,) = _blocked(lambda *t: (functools.reduce(lambda a, b: a + b, t),),
                         [(every, functools.partial(lambda i, p_, j: (j, i), j=j)) for j in range(N_DEV)],
                         [((rows_, LANES), F32)], name="sum_small_grads")
    summed = summed.reshape(-1)
    red, off = {}, 0
    for n in names:
        sz = cot[n].size
        red[n] = summed[off:off + sz].reshape(cot[n].shape)
        off += sz
    _, pull = jax.vjp(_discretise, lam_re[0], lam_im[0], log_dt[0], ssm_b_re[0], ssm_b_im[0])
    d_lre, d_lim, d_ldt, d_bre, d_bim = pull((red["a_r"], red["a_i"], red["bb_r"], red["bb_i"]))
    red.update(lam_re=d_lre, lam_im=d_lim, log_dt=d_ldt, ssm_b_re=d_bre, ssm_b_im=d_bim)

    def pack(d):
        t = jnp.concatenate([d[n].reshape(-1) for n in small])
        r_ = -(-t.shape[0] // (LANES * 16)) * 16
        return jnp.pad(t, (0, r_ * LANES - t.shape[0])).reshape(r_, LANES)

    sw, sg_, sm, sv = pack(weights), pack(red), pack(mom_m), pack(mom_v)
    sd, snm, snv = _blocked(lambda w_, g_, m_, v_: _adamw(w_, g_, m_, v_), [sw, sg_, sm, sv],
                            [(sw.shape, F32)] * 3, name="adamw_small")
    finish(group_c, sent_c, snv, "c")
    off = 0
    for n in small:
        sz = weights[n].size
        shp = weights[n].shape
        out_g[n] = red[n].reshape(shp)
        out_d[n] = sd.reshape(-1)[off:off + sz].reshape(shp)
        out_m[n] = snm.reshape(-1)[off:off + sz].reshape(shp)
        out_v[n] = snv.reshape(-1)[off:off + sz].reshape(shp)
        off += sz

    return (loss, grad_x[None], *[out_g[n] for n in order], *[out_d[n] for n in order],
            *[out_m[n] for n in order], *[out_v[n] for n in order])
```

```python
import functools
import math

import jax
import jax.numpy as jnp
from jax import lax
from jax.experimental import pallas as pl
from jax.experimental.pallas import tpu as pltpu

F32 = jnp.float32
BF16 = jnp.bfloat16
MESH = pl.DeviceIdType.MESH

N_DEV = 8
LANES = 128
SUBLANES = 8
VMEM_LIMIT = 48 * 1024 * 1024
VMEM_LIMIT_SCAN = 60 * 1024 * 1024

HEAD_DIM = 128
BLK = 128
DILATIONS = (1, 4, 16)
SSM_GROUP = 16
SSM_STATE = 64
SLAB_GROUPS = LANES // SSM_GROUP
SLAB_STATES = SLAB_GROUPS * SSM_STATE
SEGMENTS = SUBLANES
SCAN_UNROLL = 4
RMS_EPS = 1e-6
NEG_INF = -1e30

ADAM_LR = 0.001
ADAM_B1 = 0.9
ADAM_B2 = 0.999
ADAM_EPS = 1e-08
ADAM_WD = 0.01
ADAM_STEP = 10


def _tile(n, pref, unit=LANES):
    if n <= pref:
        return n
    t = (pref // unit) * unit
    while t > unit and n % t:
        t -= unit
    assert n % t == 0, (n, pref, unit)
    return t


def _params(sem=None, vmem=VMEM_LIMIT):
    return pltpu.CompilerParams(dimension_semantics=sem, vmem_limit_bytes=vmem)


_NN = (((1,), (0,)), ((), ()))
_NT = (((1,), (1,)), ((), ()))
_TN = (((0,), (0,)), ((), ()))


_ANY = pl.BlockSpec(memory_space=pl.ANY)


def _mm_call(dims, nk, n_extra, n_dep, n_out, epi, **kw):
    first_out = 2 + n_extra + n_dep
    kw["in_specs"] = list(kw["in_specs"]) + [_ANY] * n_dep

    def single(*refs):
        extra = refs[2:2 + n_extra]
        res = epi(lax.dot_general(refs[0][...], refs[1][...], dims, preferred_element_type=F32),
                  *[e[...] for e in extra])
        for o, r in zip(refs[first_out:first_out + n_out], res):
            o[...] = r.astype(o.dtype)

    if nk == 1:
        kw["scratch_shapes"] = []
        return pl.pallas_call(single, **kw)

    def body(*refs):
        a_ref, b_ref = refs[0], refs[1]
        extra = refs[2:2 + n_extra]
        outs = refs[first_out:first_out + n_out]
        acc = refs[-1]
        k = pl.program_id(2)

        @pl.when(k == 0)
        def _():
            acc[...] = jnp.zeros_like(acc)

        acc[...] += lax.dot_general(a_ref[...], b_ref[...], dims, preferred_element_type=F32)

        @pl.when(k == nk - 1)
        def _():
            res = epi(acc[...], *[e[...] for e in extra])
            for o, r in zip(outs, res):
                o[...] = r.astype(o.dtype)

    return pl.pallas_call(body, **kw)


def _identity_epi(acc):
    return (acc,)


def mm_nn(a, w, out_dtypes, *, name, epi=_identity_epi, bias=None, deps=(), tm=2048, tn=512, tk=2048):
    M, K = a.shape
    J, K2, n = w.shape
    assert K == K2
    tm, tn, tk = _tile(M, tm, 16), _tile(n, tn), _tile(K, tk)
    npj = n // tn
    nk = K // tk
    in_specs = [pl.BlockSpec((tm, tk), lambda i, j, k: (i, k)),
                pl.BlockSpec((None, tk, tn), lambda i, j, k: (j // npj, k, j % npj))]
    args = [a, w]
    if bias is not None:
        in_specs.append(pl.BlockSpec((1, tn), lambda i, j, k: (0, j)))
        args.append(bias)
    return _mm_call(
        _NN, nk, len(args) - 2, len(deps), len(out_dtypes), epi,
        out_shape=[jax.ShapeDtypeStruct((M, J * n), d) for d in out_dtypes],
        grid=(M // tm, J * npj, nk), in_specs=in_specs,
        out_specs=[pl.BlockSpec((tm, tn), lambda i, j, k: (i, j)) for _ in out_dtypes],
        scratch_shapes=[pltpu.VMEM((tm, tn), F32)],
        compiler_params=_params(("parallel", "parallel", "arbitrary")), name=name)(*args, *deps)


def mm_nt(a, w, out_dtype, *, name, epi=_identity_epi, extra=None, tm=2048, tko=512, tnr=2048):
    M, N = a.shape
    J, K, n = w.shape
    assert N == J * n
    tm, tko, tnr = _tile(M, tm, 16), _tile(K, tko), _tile(n, tnr)
    npj = n // tnr
    nk = N // tnr
    in_specs = [pl.BlockSpec((tm, tnr), lambda i, j, k: (i, k)),
                pl.BlockSpec((None, tko, tnr), lambda i, j, k: (k // npj, j, k % npj))]
    args = [a, w]
    if extra is not None:
        in_specs.append(pl.BlockSpec((tm, tko), lambda i, j, k: (i, j)))
        args.append(extra)
    return _mm_call(
        _NT, nk, len(args) - 2, 0, 1, epi,
        out_shape=[jax.ShapeDtypeStruct((M, K), out_dtype)],
        grid=(M // tm, K // tko, nk), in_specs=in_specs,
        out_specs=[pl.BlockSpec((tm, tko), lambda i, j, k: (i, j))],
        scratch_shapes=[pltpu.VMEM((tm, tko), F32)],
        compiler_params=_params(("parallel", "parallel", "arbitrary")), name=name)(*args)[0]


def mm_tn(a, b, J, *, name, deps=(), tko=1024, tn=1024, ts=2048):
    S, K = a.shape
    S2, N = b.shape
    assert S == S2 and N % J == 0
    n = N // J
    tko, tn, ts = _tile(K, tko), _tile(n, tn), _tile(S, ts)
    npj = n // tn
    nk = S // ts
    return _mm_call(
        _TN, nk, 0, len(deps), 1, _identity_epi,
        out_shape=[jax.ShapeDtypeStruct((J, K, n), BF16)],
        grid=(K // tko, J * npj, nk),
        in_specs=[pl.BlockSpec((ts, tko), lambda i, j, k: (k, i)),
                  pl.BlockSpec((ts, tn), lambda i, j, k: (k, j))],
        out_specs=[pl.BlockSpec((None, tko, tn), lambda i, j, k: (j // npj, i, j % npj))],
        scratch_shapes=[pltpu.VMEM((tko, tn), F32)],
        compiler_params=_params(("parallel", "parallel", "arbitrary")), name=name)(a, b, *deps)[0]


def rowwise(fn, rows, vecs, outs, accs=(), *, name, deps=(), ts=256):
    rows = [r if isinstance(r, tuple) else (r, r.shape[1], 0) for r in rows]
    S = rows[0][0].shape[0]
    ts = _tile(S, ts, 16)
    nr, nv, no, nd = len(rows), len(vecs), len(outs), len(deps)

    def body(*refs):
        r, v = refs[:nr], refs[nr:nr + nv]
        o, a = refs[nr + nv + nd:nr + nv + nd + no], refs[nr + nv + nd + no:]
        res = fn(*[t[...].astype(F32) for t in r], *[t[...] for t in v])
        for ref, val in zip(o, res[:no]):
            ref[...] = val.astype(ref.dtype)
        if a:
            @pl.when(pl.program_id(0) == 0)
            def _():
                for ref in a:
                    ref[...] = jnp.zeros_like(ref)

            for ref, val in zip(a, res[no:]):
                ref[...] += val

    in_specs = [pl.BlockSpec((ts, w), functools.partial(lambda i, cb: (i, cb), cb=cb)) for _, w, cb in rows]
    in_specs += [pl.BlockSpec(v.shape, lambda i: (0, 0)) for v in vecs] + [_ANY] * nd
    out_shape = [jax.ShapeDtypeStruct((S, w), d) for w, d in outs]
    out_shape += [jax.ShapeDtypeStruct((1, w), F32) for w in accs]
    out_specs = [pl.BlockSpec((ts, w), lambda i: (i, 0)) for w, _ in outs]
    out_specs += [pl.BlockSpec((1, w), lambda i: (0, 0)) for w in accs]
    return pl.pallas_call(body, out_shape=out_shape, grid=(S // ts,), in_specs=in_specs, out_specs=out_specs,
                          compiler_params=_params(("arbitrary",)), name=name)(*[r[0] for r in rows], *vecs, *deps)


def _rms(x, g):
    r = lax.rsqrt(jnp.mean(x * x, axis=-1, keepdims=True) + RMS_EPS)
    return x * r * g


def _rms_bwd(dy, x, g):
    r = lax.rsqrt(jnp.mean(x * x, axis=-1, keepdims=True) + RMS_EPS)
    xh = x * r
    dxh = dy * g
    dx = r * (dxh - xh * jnp.mean(dxh * xh, axis=-1, keepdims=True))
    return dx, jnp.sum(dy * xh, axis=0, keepdims=True)


def _sigmoid(x):
    return pl.reciprocal(1.0 + jnp.exp(-x), approx=True)


_GELU_C = math.sqrt(2.0 / math.pi)


def _gelu(x):
    return 0.5 * x * (1.0 + jnp.tanh(_GELU_C * (x + 0.044715 * x * x * x)))


def _gelu_grad(x):
    t = jnp.tanh(_GELU_C * (x + 0.044715 * x * x * x))
    return 0.5 * (1.0 + t) + 0.5 * x * (1.0 - t * t) * _GELU_C * (1.0 + 3.0 * 0.044715 * x * x)


ATTN_INTERLEAVE = 8
KEY_PAD = BLK * max(DILATIONS)


def _key_mask(n):
    ii = lax.broadcasted_iota(jnp.int32, (BLK, 2 * BLK), 0)
    jj = lax.broadcasted_iota(jnp.int32, (BLK, 2 * BLK), 1)
    return ((jj < BLK) & (jj >= ii) & (n > 0)) | ((jj >= BLK) & (jj - BLK <= ii))


def _units(d, nblk):
    nb = nblk // d
    if nb == 2:
        def unit(idx):
            ii = lax.broadcasted_iota(jnp.int32, (2 * BLK, 2 * BLK), 0)
            jj = lax.broadcasted_iota(jnp.int32, (2 * BLK, 2 * BLK), 1)
            return pl.ds(idx, 2 * BLK, stride=d), pl.ds(KEY_PAD + idx, 2 * BLK, stride=d), (jj <= ii) & (ii - jj <= BLK)
        return d, max(1, ATTN_INTERLEAVE // 4), unit

    def unit(idx):
        r, n = idx // nb, idx % nb
        cur = r + n * (BLK * d)
        keys = cur + (KEY_PAD - BLK * d)
        if d == 1:
            return pl.ds(pl.multiple_of(cur, BLK), BLK), pl.ds(pl.multiple_of(keys, BLK), 2 * BLK), _key_mask(n)
        return pl.ds(cur, BLK, stride=d), pl.ds(keys, 2 * BLK, stride=d), _key_mask(n)
    return nblk, ATTN_INTERLEAVE, unit


def _pad_keys(dst, src):
    dst[pl.ds(0, KEY_PAD), :] = jnp.zeros((KEY_PAD, dst.shape[1]), F32)

    def copy(c, carry):
        dst[pl.ds(pl.multiple_of(KEY_PAD + c * BLK, BLK), BLK), :] = src[pl.ds(pl.multiple_of(c * BLK, BLK), BLK), :]
        return carry

    lax.fori_loop(0, src.shape[0] // BLK, copy, 0)


def attn_fwd(proj, n_heads, *, name):
    S, WP = proj.shape
    assert S % (BLK * max(DILATIONS)) == 0
    nblk = S // BLK
    AW = n_heads * HEAD_DIM
    scale = 1.0 / math.sqrt(HEAD_DIM)

    def body(q_ref, k_ref, v_ref, o_ref, l_ref, acc, mrun, lrun, kp, vp):
        _pad_keys(kp, k_ref)
        _pad_keys(vp, v_ref)
        for first, d in zip((True, False, False), reversed(DILATIONS)):
            n_units, per_step, unit = _units(d, nblk)

            def step(it, carry, first=first, n_units=n_units, per_step=per_step, unit=unit):
                units = [unit(it + j * (n_units // per_step)) for j in range(per_step)]
                ss = [lax.dot_general(q_ref[cur, :].astype(BF16), kp[keys, :].astype(BF16), _NT,
                                      preferred_element_type=F32) * scale for cur, keys, _ in units]
                ss = [jnp.where(mask, s, NEG_INF) for s, (_, _, mask) in zip(ss, units)]
                ms = [jnp.max(s, axis=-1, keepdims=True) for s in ss]
                ps = [jnp.exp(s - m) for s, m in zip(ss, ms)]
                ls = [jnp.sum(p, axis=-1, keepdims=True) for p in ps]
                os_ = [jnp.dot(p.astype(BF16), vp[keys, :].astype(BF16), preferred_element_type=F32)
                       for p, (_, keys, _) in zip(ps, units)]
                for (cur, keys, mask), m, l, o in zip(units, ms, ls, os_):
                    m = jnp.broadcast_to(m, o.shape)
                    l = jnp.broadcast_to(l, o.shape)
                    if first:
                        acc[cur, :], mrun[cur, :], lrun[cur, :] = o, m, l
                    else:
                        m_old = mrun[cur, :]
                        m_new = jnp.maximum(m_old, m)
                        w_old, w_blk = jnp.exp(m_old - m_new), jnp.exp(m - m_new)
                        acc[cur, :] = w_old * acc[cur, :] + w_blk * o
                        lrun[cur, :] = w_old * lrun[cur, :] + w_blk * l
                        mrun[cur, :] = m_new
                return carry

            lax.fori_loop(0, n_units // per_step, step, 0)

        def finish(c, carry):
            r = pl.ds(pl.multiple_of(c * BLK, BLK), BLK)
            o_ref[r, :] = acc[r, :] / lrun[r, :]
            l_ref[r, :] = mrun[r, :] + jnp.log(lrun[r, :])
            return carry

        lax.fori_loop(0, nblk, finish, 0)

    def col(off):
        return pl.BlockSpec((S, HEAD_DIM), lambda h: (0, off + h))

    ospec = pl.BlockSpec((S, HEAD_DIM), lambda h: (0, h))
    return pl.pallas_call(
        body, out_shape=[jax.ShapeDtypeStruct((S, AW), F32)] * 2, grid=(n_heads,),
        in_specs=[col(0), col(n_heads), col(2 * n_heads)], out_specs=[ospec, ospec],
        scratch_shapes=[pltpu.VMEM((S, HEAD_DIM), F32)] * 3 + [pltpu.VMEM((KEY_PAD + S, HEAD_DIM), F32)] * 2,
        compiler_params=_params(("parallel",)), name=name)(proj, proj, proj)


def attn_bwd(proj, do, lse, delta, n_heads, *, name):
    S, WP = proj.shape
    nblk = S // BLK
    AW = n_heads * HEAD_DIM
    scale = 1.0 / math.sqrt(HEAD_DIM)

    def body(q_ref, k_ref, v_ref, do_ref, l_ref, dl_ref, dq_ref, dk_ref, dv_ref, dq_sc, dk_sc, dv_sc, kp, vp):
        _pad_keys(kp, k_ref)
        _pad_keys(vp, v_ref)
        dq_sc[...] = jnp.zeros_like(dq_sc)
        dk_sc[...] = jnp.zeros_like(dk_sc)
        dv_sc[...] = jnp.zeros_like(dv_sc)
        for d in DILATIONS:
            n_units, per_step, unit = _units(d, nblk)

            def step(it, carry, n_units=n_units, per_step=per_step, unit=unit):
                units = [unit(it + j * (n_units // per_step)) for j in range(per_step)]
                qs = [q_ref[cur, :].astype(BF16) for cur, _, _ in units]
                gs = [do_ref[cur, :].astype(BF16) for cur, _, _ in units]
                ks = [kp[keys, :].astype(BF16) for _, keys, _ in units]
                ss = [lax.dot_general(q, kb, _NT, preferred_element_type=F32) * scale for q, kb in zip(qs, ks)]
                dps = [lax.dot_general(g, vp[keys, :].astype(BF16), _NT, preferred_element_type=F32)
                       for g, (_, keys, _) in zip(gs, units)]
                ps = [jnp.where(mask, jnp.exp(s - l_ref[cur, :][:, :1]), 0.0) for s, (cur, _, mask) in zip(ss, units)]
                dss = [(p * (dp - dl_ref[cur, :][:, :1]) * scale).astype(BF16)
                       for p, dp, (cur, _, _) in zip(ps, dps, units)]
                for (cur, keys, _), q, g, kb, p, ds in zip(units, qs, gs, ks, ps, dss):
                    dq_sc[cur, :] += jnp.dot(ds, kb, preferred_element_type=F32)
                    dk_sc[keys, :] += lax.dot_general(ds, q, _TN, preferred_element_type=F32)
                    dv_sc[keys, :] += lax.dot_general(p.astype(BF16), g, _TN, preferred_element_type=F32)
                return carry

            lax.fori_loop(0, n_units // per_step, step, 0)
        rows = pl.ds(KEY_PAD, S)
        dq_ref[...] = dq_sc[...].astype(BF16)
        dk_ref[...] = dk_sc[rows, :].astype(BF16)
        dv_ref[...] = dv_sc[rows, :].astype(BF16)

    def col(off):
        return pl.BlockSpec((S, HEAD_DIM), lambda h: (0, off + h))

    ospec = pl.BlockSpec((S, HEAD_DIM), lambda h: (0, h))
    return pl.pallas_call(
        body, out_shape=[jax.ShapeDtypeStruct((S, AW), BF16)] * 3, grid=(n_heads,),
        in_specs=[col(0), col(n_heads), col(2 * n_heads), ospec, ospec, ospec], out_specs=[ospec] * 3,
        scratch_shapes=[pltpu.VMEM((S, HEAD_DIM), F32)] + [pltpu.VMEM((KEY_PAD + S, HEAD_DIM), F32)] * 4,
        compiler_params=_params(("parallel",), VMEM_LIMIT_SCAN), name=name)(proj, proj, proj, do, lse, delta)


def _to_segments(t):
    S, W = t.shape
    return t.reshape(SEGMENTS, S // SEGMENTS, W).swapaxes(0, 1).reshape(S, W)


def _from_segments(t):
    S, W = t.shape
    return t.reshape(S // SEGMENTS, SEGMENTS, W).swapaxes(0, 1).reshape(S, W)


def _cmul(ar, ai, br, bi):
    return ar * br - ai * bi, ar * bi + ai * br


def _power(ar, ai, log2n):
    for _ in range(log2n):
        ar, ai = _cmul(ar, ai, ar, ai)
    return ar, ai


def _shift_rows(x, up):
    row = lax.broadcasted_iota(jnp.int32, x.shape, 0)
    if up:
        return jnp.where(row == SEGMENTS - 1, 0.0, pltpu.roll(x, SEGMENTS - 1, 0))
    return jnp.where(row == 0, 0.0, pltpu.roll(x, 1, 0))


def _segment_carries(er, ei, pr, pi, up):
    cr = jnp.zeros_like(er)
    ci = jnp.zeros_like(ei)
    for _ in range(SEGMENTS - 1):
        tr, ti = _cmul(pr, pi, cr, ci)
        cr, ci = _shift_rows(er + tr, up), _shift_rows(ei + ti, up)
    return cr, ci


def _scan_states(sr, si, ar, ai, T, reverse):
    ns = sr.shape[1]
    ar8 = jnp.broadcast_to(ar, (SEGMENTS, ns))
    ai8 = jnp.broadcast_to(ai, (SEGMENTS, ns))

    def rows(t):
        k = (T - 1 - t) if reverse else t
        return pl.ds(pl.multiple_of(k * SEGMENTS, SEGMENTS), SEGMENTS)

    def advance(t, c):
        tr, ti = _cmul(ar8, ai8, c[0], c[1])
        return tr + sr[rows(t), :], ti + si[rows(t), :]

    def several(step):
        def trip(t, c):
            for j in range(SCAN_UNROLL):
                c = step(t * SCAN_UNROLL + j, c)
            return c
        return trip

    zero = jnp.zeros((SEGMENTS, ns), F32)
    er, ei = lax.fori_loop(0, T // SCAN_UNROLL, several(advance), (zero, zero))
    pr, pi = _power(ar, ai, T.bit_length() - 1)
    cr, ci = _segment_carries(er, ei, jnp.broadcast_to(pr, (SEGMENTS, ns)), jnp.broadcast_to(pi, (SEGMENTS, ns)), reverse)

    def store(t, c):
        nr, ni = advance(t, c)
        sr[rows(t), :] = nr
        si[rows(t), :] = ni
        return nr, ni

    lax.fori_loop(0, T // SCAN_UNROLL, several(store), (cr, ci))
    return cr, ci


def _slab_specs(ns):
    return [pl.BlockSpec((None, LANES, ns), lambda g: (g, 0, 0)),
            pl.BlockSpec((None, LANES, ns), lambda g: (g, 0, 0)),
            pl.BlockSpec((None, 1, ns), lambda g: (g, 0, 0)),
            pl.BlockSpec((None, 1, ns), lambda g: (g, 0, 0)),
            pl.BlockSpec((None, ns, LANES), lambda g: (g, 0, 0)),
            pl.BlockSpec((None, ns, LANES), lambda g: (g, 0, 0)),
            pl.BlockSpec((1, LANES), lambda g: (0, g))]


def _chunks(S):
    rc = _tile(S, 512, 16)
    return rc, S // rc


def ssm_fwd(u, bbr, bbi, ar, ai, cbr, cbi, dsk, *, name):
    S, SW = u.shape
    nslab, _, ns = bbr.shape
    T = S // SEGMENTS
    assert T & (T - 1) == 0
    rc, nc = _chunks(S)

    def body(u_ref, br_ref, bi_ref, ar_ref, ai_ref, cr_ref, ci_ref, d_ref, y_ref, yg_ref, str_ref, sti_ref, sr, si):
        def inputs(c, carry):
            r = pl.ds(pl.multiple_of(c * rc, rc), rc)
            sr[r, :] = jnp.dot(u_ref[r, :], br_ref[...], preferred_element_type=F32)
            si[r, :] = jnp.dot(u_ref[r, :], bi_ref[...], preferred_element_type=F32)
            return carry

        lax.fori_loop(0, nc, inputs, 0)
        _scan_states(sr, si, ar_ref[...], ai_ref[...], T, False)

        def outputs(c, carry):
            r = pl.ds(pl.multiple_of(c * rc, rc), rc)
            srb, sib = sr[r, :].astype(BF16), si[r, :].astype(BF16)
            str_ref[r, :] = srb
            sti_ref[r, :] = sib
            y = (jnp.dot(srb, cr_ref[...], preferred_element_type=F32)
                 - jnp.dot(sib, ci_ref[...], preferred_element_type=F32) + d_ref[...] * u_ref[r, :].astype(F32))
            y_ref[r, :] = y
            yg_ref[r, :] = _gelu(y).astype(BF16)
            return carry

        lax.fori_loop(0, nc, outputs, 0)

    slab = pl.BlockSpec((S, LANES), lambda g: (0, g))
    states = pl.BlockSpec((S, ns), lambda g: (0, g))
    return pl.pallas_call(
        body, out_shape=([jax.ShapeDtypeStruct((S, SW), F32), jax.ShapeDtypeStruct((S, SW), BF16)]
                         + [jax.ShapeDtypeStruct((S, nslab * ns), BF16)] * 2),
        grid=(nslab,), in_specs=[slab] + _slab_specs(ns), out_specs=[slab, slab, states, states],
        scratch_shapes=[pltpu.VMEM((S, ns), F32)] * 2,
        compiler_params=_params(("parallel",), VMEM_LIMIT_SCAN), name=name)(u, bbr, bbi, ar, ai, cbr, cbi, dsk)


def ssm_bwd(u, d_direct, d_gate, y, st_r, st_i, bbr, bbi, ar, ai, cbr, cbi, dsk, *, name, deps=()):
    S, SW = u.shape
    nslab, _, ns = bbr.shape
    T = S // SEGMENTS
    rc, nc = _chunks(S)
    pair_rows = 2 * SEGMENTS

    def body(*refs):
        (u_ref, d1_ref, d2_ref, y_ref, sr_ref, si_ref, br_ref, bi_ref, ar_ref, ai_ref, cr_ref, ci_ref,
         d_ref) = refs[:13]
        (du_ref, dbr_ref, dbi_ref, dcr_ref, dci_ref, dar_ref, dai_ref, dd_ref, lr, li,
         dy_ref) = refs[13 + len(deps):]

        def inputs(c, skip):
            r = pl.ds(pl.multiple_of(c * rc, rc), rc)
            dy = (d1_ref[r, :] + d2_ref[r, :].astype(F32)) * _gelu_grad(y_ref[r, :])
            dy_ref[r, :] = dy
            gb = dy.astype(BF16)
            lr[r, :] = lax.dot_general(gb, cr_ref[...], _NT, preferred_element_type=F32)
            li[r, :] = -lax.dot_general(gb, ci_ref[...], _NT, preferred_element_type=F32)
            return skip + jnp.sum(dy * u_ref[r, :].astype(F32), axis=0, keepdims=True)

        dd_ref[...] = lax.fori_loop(0, nc, inputs, jnp.zeros((1, LANES), F32))
        _scan_states(lr, li, ar_ref[...], -ai_ref[...], T, True)

        def steps(j):
            rows = pl.ds(pl.multiple_of(j * pair_rows, pair_rows), pair_rows)
            tr, ti = sr_ref[rows, :].astype(F32), si_ref[rows, :].astype(F32)
            return tr[:SEGMENTS], tr[SEGMENTS:], ti[:SEGMENTS], ti[SEGMENTS:]

        def pair(j, c):
            acc_r, acc_i, pr, pi = c
            lo_r, hi_r, lo_i, hi_i = steps(j)
            first = pl.ds(pl.multiple_of(j * pair_rows, SEGMENTS), SEGMENTS)
            second = pl.ds(pl.multiple_of(j * pair_rows + SEGMENTS, SEGMENTS), SEGMENTS)
            la_r, la_i, lb_r, lb_i = lr[first, :], li[first, :], lr[second, :], li[second, :]
            return (acc_r + la_r * pr + la_i * pi + lb_r * lo_r + lb_i * lo_i,
                    acc_i - la_r * pi + la_i * pr - lb_r * lo_i + lb_i * lo_r, hi_r, hi_i)

        def pairs(t, c):
            return pair(2 * t + 1, pair(2 * t, c))

        _, end_r, _, end_i = steps(T // 2 - 1)
        zero = jnp.zeros((SEGMENTS, ns), F32)
        acc = lax.fori_loop(0, T // 4, pairs, (zero, zero, _shift_rows(end_r, False), _shift_rows(end_i, False)))
        dar_ref[...] = jnp.sum(acc[0], axis=0, keepdims=True)
        dai_ref[...] = jnp.sum(acc[1], axis=0, keepdims=True)

        dbr_ref[...] = jnp.zeros_like(dbr_ref)
        dbi_ref[...] = jnp.zeros_like(dbi_ref)
        dcr_ref[...] = jnp.zeros_like(dcr_ref)
        dci_ref[...] = jnp.zeros_like(dci_ref)

        def outputs(c, carry):
            r = pl.ds(pl.multiple_of(c * rc, rc), rc)
            ub = u_ref[r, :]
            g = dy_ref[r, :]
            gb = g.astype(BF16)
            lrb = lr[r, :].astype(BF16)
            lib = li[r, :].astype(BF16)
            du_ref[r, :] = (lax.dot_general(lrb, br_ref[...], _NT, preferred_element_type=F32)
                            + lax.dot_general(lib, bi_ref[...], _NT, preferred_element_type=F32)
                            + d_ref[...] * g).astype(BF16)
            dbr_ref[...] += lax.dot_general(ub, lrb, _TN, preferred_element_type=F32)
            dbi_ref[...] += lax.dot_general(ub, lib, _TN, preferred_element_type=F32)
            dcr_ref[...] += lax.dot_general(sr_ref[r, :], gb, _TN, preferred_element_type=F32)
            dci_ref[...] -= lax.dot_general(si_ref[r, :], gb, _TN, preferred_element_type=F32)
            return carry

        lax.fori_loop(0, nc, outputs, 0)

    slab = pl.BlockSpec((S, LANES), lambda g: (0, g))
    states = pl.BlockSpec((S, ns), lambda g: (0, g))
    bspec = pl.BlockSpec((None, LANES, ns), lambda g: (g, 0, 0))
    cspec = pl.BlockSpec((None, ns, LANES), lambda g: (g, 0, 0))
    aspec = pl.BlockSpec((None, 1, ns), lambda g: (g, 0, 0))
    return pl.pallas_call(
        body,
        out_shape=[jax.ShapeDtypeStruct((S, SW), BF16),
                   jax.ShapeDtypeStruct((nslab, LANES, ns), F32), jax.ShapeDtypeStruct((nslab, LANES, ns), F32),
                   jax.ShapeDtypeStruct((nslab, ns, LANES), F32), jax.ShapeDtypeStruct((nslab, ns, LANES), F32),
                   jax.ShapeDtypeStruct((nslab, 1, ns), F32), jax.ShapeDtypeStruct((nslab, 1, ns), F32),
                   jax.ShapeDtypeStruct((1, SW), F32)],
        grid=(nslab,), in_specs=[slab, slab, slab, slab, states, states] + _slab_specs(ns) + [_ANY] * len(deps),
        out_specs=[slab, bspec, bspec, cspec, cspec, aspec, aspec, pl.BlockSpec((1, LANES), lambda g: (0, g))],
        scratch_shapes=[pltpu.VMEM((S, ns), F32)] * 2 + [pltpu.VMEM((S, LANES), F32)],
        compiler_params=_params(("parallel",), VMEM_LIMIT_SCAN), name=name)(
            u, d_direct, d_gate, y, st_r, st_i, bbr, bbi, ar, ai, cbr, cbi, dsk, *deps)


def _discretise(lam_re, lam_im, log_dt, b_re, b_im):
    dt = jnp.exp(log_dt)[:, None]
    mag = jnp.exp(lam_re * dt)
    ar = mag * jnp.cos(lam_im * dt)
    ai = mag * jnp.sin(lam_im * dt)
    nr, ni = ar - 1.0, ai
    den = lam_re * lam_re + lam_im * lam_im
    cr = ((nr * lam_re + ni * lam_im) / den)[..., None]
    ci = ((ni * lam_re - nr * lam_im) / den)[..., None]
    return ar, ai, cr * b_re - ci * b_im, cr * b_im + ci * b_re


def _block_diag(t, nslab):
    G, R, C = t.shape
    eye = jnp.eye(SLAB_GROUPS, dtype=t.dtype)
    t = t.reshape(nslab, SLAB_GROUPS, R, C)
    return jnp.einsum('sgrc,gh->sgrhc', t, eye).reshape(nslab, SLAB_GROUPS * R, SLAB_GROUPS * C)


def _block_diag_part(t, R, C):
    nslab = t.shape[0]
    eye = jnp.eye(SLAB_GROUPS, dtype=t.dtype)
    t = t.reshape(nslab, SLAB_GROUPS, R, SLAB_GROUPS, C)
    return jnp.einsum('sgrhc,gh->sgrc', t, eye).reshape(nslab * SLAB_GROUPS, R, C)


def _place():
    return lax.axis_index("x"), lax.axis_index("y"), lax.axis_index("c")


_HBM = pl.BlockSpec(memory_space=pltpu.HBM)
_SEM = pl.BlockSpec(memory_space=pltpu.SEMAPHORE)
_ORDERED_EFFECT = pltpu.SideEffectType.DATAFLOW_SIDE_EFFECTING


def _split_call(name, srcs, zones, sems_in, n_new, body_fn, after):
    nsrc, nz, ns, nn = len(srcs), len(zones), len(sems_in), len(n_new)
    nb = nsrc + nz

    def body(*refs):
        outs = refs[nb + ns + 1:]
        body_fn(refs[:nb], refs[nb:nb + ns], outs[:nn])
        outs[nn + nz][...] = jnp.zeros((SUBLANES, LANES), F32)

    res = pl.pallas_call(
        body, name=name,
        out_shape=([pltpu.SemaphoreType.DMA((n,)) for n in n_new] + [pltpu.HBM(b.shape, b.dtype) for b in zones]
                   + [jax.ShapeDtypeStruct((SUBLANES, LANES), F32)]),
        in_specs=[_HBM] * nb + [_SEM] * ns + [_ANY],
        out_specs=[_SEM] * nn + [_HBM] * nz + [pl.BlockSpec(memory_space=pltpu.VMEM)],
        input_output_aliases={nsrc + i: nn + i for i in range(nz)},
        compiler_params=pltpu.CompilerParams(has_side_effects=_ORDERED_EFFECT))(
            *[pltpu.with_memory_space_constraint(b, pltpu.HBM) for b in list(srcs) + list(zones)], *sems_in, after)
    return list(res[:nn]), list(res[nn:nn + nz]), res[-1]


def _mesh_peers():
    x, y, c = _place()
    return x, y, c, (x, y, 1 - c), [(1 - x, y), (x, 1 - y), (1 - x, 1 - y)]


def gather_start(shards, after, *, name):
    nw = len(shards)
    x, y, c = _place()
    zones = [lax.dynamic_update_slice(lax.empty((N_DEV,) + s.shape, s.dtype), s[None], (4 * x + 2 * y + c, 0, 0))
             for s in shards]

    def body(bufs, taken, new):
        for cp in _gather_first(bufs, nw, new[0], new[1]):
            cp.start()

    sems, zones, token = _split_call(name, shards, zones, [], [4 * nw, 4 * nw], body, after)
    return shards, sems, zones, token


def _gather_first(bufs, nw, send, recv):
    x, y, c, sibling, chips = _mesh_peers()
    out = []
    for w in range(nw):
        slot = bufs[nw + w].at[4 * x + 2 * y + c]
        for k, to in enumerate([sibling] + [(*ch, c) for ch in chips]):
            out.append(pltpu.make_async_remote_copy(
                src_ref=bufs[w], dst_ref=slot, send_sem=send.at[4 * w + k], recv_sem=recv.at[4 * w + k],
                device_id=to, device_id_type=MESH))
    return out


def _gather_slot_copy(bufs, nw, w, block, send_sem, recv_sem, to):
    px, py, pc = block
    slot = bufs[nw + w].at[4 * px + 2 * py + pc]
    return pltpu.make_async_remote_copy(src_ref=slot, dst_ref=slot, send_sem=send_sem, recv_sem=recv_sem,
                                        device_id=to, device_id_type=MESH)


def gather_forward(state, after, *, name):
    shards, sems, zones, _ = state
    nw = len(shards)

    def body(bufs, taken, new):
        x, y, c, sibling, chips = _mesh_peers()
        for j, ch in enumerate(chips):
            for w in range(nw):
                k = 4 * w + 1 + j
                _gather_slot_copy(bufs, nw, w, (*ch, c), taken[0].at[k], taken[1].at[k], (*ch, c)).wait_recv()
                _gather_slot_copy(bufs, nw, w, (*ch, c), new[0].at[3 * w + j], new[1].at[3 * w + j], sibling).start()
        for w in range(nw):
            _gather_slot_copy(bufs, nw, w, sibling, taken[0].at[4 * w], taken[1].at[4 * w], sibling).wait_recv()
        for cp in _gather_first(bufs, nw, taken[0], taken[1]):
            cp.wait_send()

    sems, zones, token = _split_call(name, shards, zones, sems, [3 * nw, 3 * nw], body, after)
    return shards, sems, zones, token


def gather_finish(state, after, *, name):
    shards, sems, zones, _ = state
    nw = len(shards)

    def body(bufs, taken, new):
        x, y, c, sibling, chips = _mesh_peers()
        for w in range(nw):
            for j, ch in enumerate(chips):
                cp = _gather_slot_copy(bufs, nw, w, (*ch, 1 - c), taken[0].at[3 * w + j], taken[1].at[3 * w + j], sibling)
                cp.wait_send()
                cp.wait_recv()

    _, zones, _ = _split_call(name, shards, zones, sems, [], body, after)
    return zones


def exchange_start(srcs, zone_shapes, copies, n, after, *, name):
    nw = len(srcs)
    zones = [lax.empty(z, s.dtype) for z, s in zip(zone_shapes, srcs)]

    def body(bufs, taken, new):
        for cp in copies(bufs[:nw], bufs[nw:], new[0], new[1]):
            cp.start()

    sems, zones, token = _split_call(name, srcs, zones, [], [n, n], body, after)
    return srcs, copies, sems, zones, token


def exchange_wait(state, after, *, name):
    srcs, copies, sems, zones, _ = state
    nw = len(srcs)

    def body(bufs, taken, new):
        for cp in copies(bufs[:nw], bufs[nw:], taken[0], taken[1]):
            cp.wait_send()
            cp.wait_recv()

    _, zones, _ = _split_call(name, srcs, zones, sems, [], body, after)
    return zones


def _core_copies(srcs, zones, send, recv):
    x, y, c = _place()
    return [pltpu.make_async_remote_copy(
        src_ref=srcs[w].at[:, 1 - c], dst_ref=zones[w], send_sem=send.at[w], recv_sem=recv.at[w],
        device_id=(x, y, 1 - c), device_id_type=MESH) for w in range(len(srcs))]


def _chip_copies(srcs, zones, send, recv):
    x, y, c = _place()
    chips = [(1 - x, y), (x, 1 - y), (1 - x, 1 - y)]
    return [pltpu.make_async_remote_copy(
        src_ref=srcs[w].at[2 * cx + cy], dst_ref=zones[w].at[j], send_sem=send.at[3 * w + j],
        recv_sem=recv.at[3 * w + j], device_id=(cx, cy, c), device_id_type=MESH)
        for w in range(len(srcs)) for j, (cx, cy) in enumerate(chips)]


def _blocked(fn, ins, outs, *, name, place=None, tr=256):
    k, n = outs[0][0]
    tr = _tile(k, tr, 16)
    if place is None:
        place = jnp.zeros((1,), jnp.int32)
    specs = []
    args = []
    for a in ins:
        if isinstance(a, tuple):
            arr, lead = a
            specs.append(pl.BlockSpec((None, tr, n), functools.partial(lambda i, s, lead: (*lead(i, s), 0), lead=lead)))
            args.append(arr)
        else:
            specs.append(pl.BlockSpec((tr, n), lambda i, s: (i, 0)))
            args.append(a)
    nin = len(args)

    def body(place_ref, *refs):
        res = fn(*[r[...] for r in refs[:nin]])
        for ref, val in zip(refs[nin:], res):
            ref[...] = val.astype(ref.dtype)

    return pl.pallas_call(
        body, out_shape=[jax.ShapeDtypeStruct(s, d) for s, d in outs],
        grid_spec=pltpu.PrefetchScalarGridSpec(
            num_scalar_prefetch=1, grid=(k // tr,), in_specs=specs,
            out_specs=[pl.BlockSpec((tr, n), lambda i, s: (i, 0)) for _ in outs]),
        compiler_params=_params(("parallel",)), name=name)(place, *args)


def _adamw(w, g, m, v):
    m = ADAM_B1 * m + (1.0 - ADAM_B1) * g
    v = ADAM_B2 * v + (1.0 - ADAM_B2) * (g * g)
    m_hat = m / (1.0 - ADAM_B1 ** ADAM_STEP)
    v_hat = v / (1.0 - ADAM_B2 ** ADAM_STEP)
    delta = -ADAM_LR * (m_hat * pl.reciprocal(jnp.sqrt(v_hat) + ADAM_EPS, approx=True) + ADAM_WD * w)
    return delta, m, v


def kernel(x, p, mix_norm_pre, w_in, lam_re, lam_im, log_dt, ssm_b_re, ssm_b_im, ssm_c_re, ssm_c_im, ssm_d, w_glu, b_glu, attn_out_norm, ssm_out_norm, w_out, mix_norm_post, mlp_norm_pre, w_up, w_down, mlp_norm_post, ple_norm_pre, w_ple_gate, w_ple_proj, ple_norm_post, loss_target, m_mix_norm_pre, m_w_in, m_lam_re, m_lam_im, m_log_dt, m_ssm_b_re, m_ssm_b_im, m_ssm_c_re, m_ssm_c_im, m_ssm_d, m_w_glu, m_b_glu, m_attn_out_norm, m_ssm_out_norm, m_w_out, m_mix_norm_post, m_mlp_norm_pre, m_w_up, m_w_down, m_mlp_norm_post, m_ple_norm_pre, m_w_ple_gate, m_w_ple_proj, m_ple_norm_post, v_mix_norm_pre, v_w_in, v_lam_re, v_lam_im, v_log_dt, v_ssm_b_re, v_ssm_b_im, v_ssm_c_re, v_ssm_c_im, v_ssm_d, v_w_glu, v_b_glu, v_attn_out_norm, v_ssm_out_norm, v_w_out, v_mix_norm_post, v_mlp_norm_pre, v_w_up, v_w_down, v_mlp_norm_post, v_ple_norm_pre, v_w_ple_gate, v_w_ple_proj, v_ple_norm_post):
    weights = dict(mix_norm_pre=mix_norm_pre, w_in=w_in, lam_re=lam_re, lam_im=lam_im, log_dt=log_dt, ssm_b_re=ssm_b_re, ssm_b_im=ssm_b_im, ssm_c_re=ssm_c_re, ssm_c_im=ssm_c_im, ssm_d=ssm_d, w_glu=w_glu, b_glu=b_glu, attn_out_norm=attn_out_norm, ssm_out_norm=ssm_out_norm, w_out=w_out, mix_norm_post=mix_norm_post, mlp_norm_pre=mlp_norm_pre, w_up=w_up, w_down=w_down, mlp_norm_post=mlp_norm_post, ple_norm_pre=ple_norm_pre, w_ple_gate=w_ple_gate, w_ple_proj=w_ple_proj, ple_norm_post=ple_norm_post)
    mom_m = dict(mix_norm_pre=m_mix_norm_pre, w_in=m_w_in, lam_re=m_lam_re, lam_im=m_lam_im, log_dt=m_log_dt, ssm_b_re=m_ssm_b_re, ssm_b_im=m_ssm_b_im, ssm_c_re=m_ssm_c_re, ssm_c_im=m_ssm_c_im, ssm_d=m_ssm_d, w_glu=m_w_glu, b_glu=m_b_glu, attn_out_norm=m_attn_out_norm, ssm_out_norm=m_ssm_out_norm, w_out=m_w_out, mix_norm_post=m_mix_norm_post, mlp_norm_pre=m_mlp_norm_pre, w_up=m_w_up, w_down=m_w_down, mlp_norm_post=m_mlp_norm_post, ple_norm_pre=m_ple_norm_pre, w_ple_gate=m_w_ple_gate, w_ple_proj=m_w_ple_proj, ple_norm_post=m_ple_norm_post)
    mom_v = dict(mix_norm_pre=v_mix_norm_pre, w_in=v_w_in, lam_re=v_lam_re, lam_im=v_lam_im, log_dt=v_log_dt, ssm_b_re=v_ssm_b_re, ssm_b_im=v_ssm_b_im, ssm_c_re=v_ssm_c_re, ssm_c_im=v_ssm_c_im, ssm_d=v_ssm_d, w_glu=v_w_glu, b_glu=v_b_glu, attn_out_norm=v_attn_out_norm, ssm_out_norm=v_ssm_out_norm, w_out=v_w_out, mix_norm_post=v_mix_norm_post, mlp_norm_pre=v_mlp_norm_pre, w_up=v_w_up, w_down=v_w_down, mlp_norm_post=v_mlp_norm_post, ple_norm_pre=v_ple_norm_pre, w_ple_gate=v_w_ple_gate, w_ple_proj=v_w_ple_proj, ple_norm_post=v_ple_norm_post)
    order = list(weights)
    big = ["w_in", "w_glu", "w_out", "w_up", "w_down", "w_ple_gate", "w_ple_proj"]
    col_sharded = {"w_in", "w_up", "w_ple_proj"}
    small = [n for n in order if n not in big]

    _, S, D = x.shape
    xs = x[0]
    tgt = loss_target[0]
    AW = attn_out_norm.shape[1]
    SW = ssm_d.shape[1]
    H = AW // HEAD_DIM
    G = SW // SSM_GROUP
    nslab = G // SLAB_GROUPS
    P_, C_ = SSM_STATE, SSM_GROUP

    shard = {n: weights[n][0].astype(BF16) for n in big}
    W, WT = {}, {}

    def arrived(names, gathered):
        for n, g in zip(names, gathered):
            W[n] = g if n in col_sharded else g.reshape(1, N_DEV * g.shape[1], g.shape[2])

    def transposed(g):
        return jnp.swapaxes(g, 1, 2).reshape(1, g.shape[0] * g.shape[2], g.shape[1])

    g1, g2, g3, g4, g5, g6 = (weights[n] for n in ("mix_norm_pre", "mix_norm_post", "mlp_norm_pre",
                                                      "mlp_norm_post", "ple_norm_pre", "ple_norm_post"))
    ga, gs = attn_out_norm, ssm_out_norm
    gather_in = gather_start([shard["w_in"]], shard["w_in"], name="gather_w_in_start")
    (hn1,) = rowwise(lambda a, g: (_rms(a, g),), [xs], [g1], [(D, BF16)], deps=(gather_in[-1],), name="norm_in")
    gather_in = gather_forward(gather_in, hn1, name="gather_w_in_forward")
    arrived(["w_in"], gather_finish(gather_in, gather_in[-1], name="gather_w_in_finish"))
    WT["w_in"] = transposed(W["w_in"])
    early, mid, late = ["w_glu", "w_out"], ["w_up"], ["w_down", "w_ple_gate", "w_ple_proj"]
    gather_early = gather_start([shard[n] for n in early], W["w_in"], name="gather_early_start")
    gather_mid = gather_start([shard[n] for n in mid], gather_early[-1], name="gather_mid_start")
    gather_late = gather_start([shard[n] for n in late], gather_mid[-1], name="gather_late_start")

    (proj,) = mm_nn(hn1, W["w_in"], [F32], deps=(gather_late[-1],), name="proj_in")
    attn, lse = attn_fwd(proj, H, name="attn_fwd")
    gather_early = gather_forward(gather_early, attn, name="gather_early_forward")
    (mix_a,) = rowwise(lambda a, g: (_rms(a, g),), [attn], [ga], [(AW, BF16)], deps=(gather_early[-1],),
                       name="attn_norm")
    arrived(early, gather_finish(gather_early, mix_a, name="gather_early_finish"))

    a_r, a_i, bb_r, bb_i = _discretise(lam_re[0], lam_im[0], log_dt[0], ssm_b_re[0], ssm_b_im[0])
    ssm_consts = (_block_diag(bb_r.swapaxes(1, 2), nslab).astype(BF16), _block_diag(bb_i.swapaxes(1, 2), nslab).astype(BF16),
                  a_r.reshape(nslab, 1, SLAB_STATES), a_i.reshape(nslab, 1, SLAB_STATES),
                  _block_diag(ssm_c_re[0].swapaxes(1, 2), nslab).astype(BF16),
                  _block_diag(ssm_c_im[0].swapaxes(1, 2), nslab).astype(BF16), ssm_d)
    u_seg = _to_segments(proj[:, 3 * AW:]).astype(BF16)
    y_pre, yg, st_r, st_i = ssm_fwd(u_seg, *ssm_consts, name="ssm_fwd")
    gather_mid = gather_forward(gather_mid, y_pre, name="gather_mid_forward")
    (gl1,) = mm_nn(yg, W["w_glu"], [BF16], epi=lambda acc, b: (acc + b,), bias=b_glu, deps=(gather_mid[-1],),
                   name="glu_gate")
    (mix_s,) = rowwise(lambda yp, gl, g: (_rms(_gelu(yp) * _sigmoid(gl), g),), [y_pre, gl1], [gs], [(SW, BF16)],
                       name="ssm_glu_norm")
    mixed = jnp.concatenate([mix_a, _from_segments(mix_s)], axis=1)
    (mo,) = mm_nn(mixed, W["w_out"], [BF16], name="mix_out")

    def resid_norm(h, t, gpost, gpre):
        hh = h + _rms(t, gpost)
        return hh, _rms(hh, gpre)

    h1, hn2 = rowwise(resid_norm, [xs, mo], [g2, g3], [(D, F32), (D, BF16)], name="resid_mix")
    arrived(mid, gather_finish(gather_mid, hn2, name="gather_mid_finish"))
    gather_late = gather_forward(gather_late, W["w_up"], name="gather_late_forward")
    WT["w_up"] = transposed(W["w_up"])

    def relu2(acc):
        r = jnp.maximum(acc, 0.0)
        return acc, r * r

    up, act = mm_nn(hn2, W["w_up"], [BF16, BF16], epi=relu2, deps=(gather_late[-1],), tm=1024, tn=1024, name="mlp_up")
    arrived(late, gather_finish(gather_late, act, name="gather_late_finish"))
    (ff,) = mm_nn(act, W["w_down"], [BF16], name="mlp_down")
    h2, hn3 = rowwise(resid_norm, [h1, ff], [g4, g5], [(D, F32), (D, BF16)], name="resid_mlp")
    (gl2,) = mm_nn(hn3, W["w_ple_gate"], [BF16], name="ple_gate")
    pb = p[0, 0].astype(BF16)
    (emb,) = mm_nn(pb, W["w_ple_proj"], [BF16], name="ple_proj")

    def head(h, gl, e, t, g):
        sg = _sigmoid(gl)
        ge = sg * e
        err = h + _rms(ge, g) - t
        dh = err * (1.0 / D)
        dge, dg = _rms_bwd(dh, ge, g)
        return dh, dge * e * sg * (1.0 - sg), dge * sg, jnp.sum(err * err, axis=0, keepdims=True), dg

    dh3, dgl2, demb, loss_part, dg6 = rowwise(head, [h2, gl2, emb, tgt], [g6], [(D, F32), (D, BF16), (D, BF16)],
                                             [D, D], name="ple_loss_head")
    loss = lax.psum(0.5 / D * jnp.sum(loss_part), ("x", "y", "c"))

    x_i, y_i, c_i = _place()
    place = jnp.stack([c_i, 2 * x_i + y_i]).astype(jnp.int32)
    grads, out_g, out_d, out_m, out_v = {}, {}, {}, {}, {}

    def to_sibling(names, after, tag):
        chunks = []
        for n in names:
            g = grads[n]
            g = g if n in col_sharded else g.reshape(N_DEV, g.shape[1] // N_DEV, g.shape[2])
            chunks.append(g.reshape(4, 2, g.shape[1], g.shape[2]))
        return chunks, exchange_start(chunks, [(4,) + g.shape[2:] for g in chunks], _core_copies, len(chunks), after,
                                      name=f"grads_to_sibling_{tag}")

    def to_chips(names, sent, after, tag):
        chunks, state = sent
        sums = []
        for n, g, r in zip(names, chunks, exchange_wait(state, after, name=f"grads_from_sibling_{tag}")):
            k, nn = g.shape[2], g.shape[3]
            kb = k // _tile(k, 512, 16)

            def mine(i, s, kb=kb):
                return 2 * (i // kb) + s[0], i % kb

            (s,) = _blocked(lambda a, b: (a.astype(F32) + b.astype(F32),),
                            [(g.reshape(N_DEV, k, nn), mine), r.reshape(4 * k, nn)],
                            [((4 * k, nn), BF16)], place=place, tr=k // kb, name=f"chip_sum_{n}")
            sums.append(s.reshape(4, k, nn))
        return sums, exchange_start(sums, [(3,) + s.shape[1:] for s in sums], _chip_copies, 3 * len(sums), sums[-1],
                                    name=f"grads_to_chips_{tag}")

    def update(w_, m_, v_, own, r0, r1, r2):
        g = own.astype(F32) + r0.astype(F32) + r1.astype(F32) + r2.astype(F32)
        return (g,) + _adamw(w_, g, m_, v_)

    def finish(names, sent, after, tag):
        sums, state = sent
        for n, s, r in zip(names, sums, exchange_wait(state, after, name=f"grads_from_chips_{tag}")):
            shp = weights[n].shape
            res = _blocked(update, [weights[n][0], mom_m[n][0], mom_v[n][0], (s, lambda i, p_: (p_[1], i)),
                                    (r, lambda i, p_: (0, i)), (r, lambda i, p_: (1, i)), (r, lambda i, p_: (2, i))],
                           [(shp[1:], F32)] * 4, place=place, tr=max(16, min(shp[1] // 8, 262144 // shp[2])),
                           name=f"adamw_{n}")
            out_g[n], out_d[n], out_m[n], out_v[n] = (t.reshape(shp) for t in res)
        return out_v[names[-1]]

    grads["w_ple_proj"] = mm_tn(pb, demb, N_DEV, name="grad_w_ple_proj")
    dhn3 = mm_nt(dgl2, W["w_ple_gate"], BF16, name="back_ple_gate")
    grads["w_ple_gate"] = mm_tn(hn3, dgl2, 1, name="grad_w_ple_gate")

    def back_resid(dh, dhn, h, t, gpre, gpost):
        d1, dgpre = _rms_bwd(dhn, h, gpre)
        dhh = dh + d1
        dt, dgpost = _rms_bwd(dhh, t, gpost)
        return dhh, dt, dgpre, dgpost

    dh2, dff, dg5, dg4 = rowwise(back_resid, [dh3, dhn3, h2, ff], [g5, g4], [(D, F32), (D, BF16)], [D, D],
                                 name="back_resid_mlp")
    dup = mm_nt(dff, W["w_down"], BF16, epi=lambda acc, u_: (acc * 2.0 * jnp.maximum(u_.astype(F32), 0.0),),
                extra=up, name="back_mlp_down")
    grads["w_down"] = mm_tn(act, dff, 1, name="grad_w_down")
    group_a = ["w_ple_proj", "w_ple_gate", "w_down"]
    sent_a = to_sibling(group_a, grads["w_down"], "a")
    (dhn2,) = mm_nn(dup, WT["w_up"], [BF16], deps=(sent_a[1][-1],), name="back_mlp_up")
    sent_a = to_chips(group_a, sent_a, dhn2, "a")
    grads["w_up"] = mm_tn(hn2, dup, N_DEV, deps=(sent_a[1][-1],), name="grad_w_up")
    dh1, dmo, dg3, dg2 = rowwise(back_resid, [dh2, dhn2, h1, mo], [g3, g2], [(D, F32), (D, BF16)], [D, D],
                                 name="back_resid_mix")
    dmixed = mm_nt(dmo, W["w_out"], BF16, name="back_mix_out")
    grads["w_out"] = mm_tn(mixed, dmo, 1, name="grad_w_out")

    def back_glu(dm, yp, gl, g):
        ygf = _gelu(yp)
        sg = _sigmoid(gl)
        dssm, dg = _rms_bwd(dm, ygf * sg, g)
        dgl = dssm * ygf * sg * (1.0 - sg)
        return dgl, dssm * sg, dg, jnp.sum(dgl, axis=0, keepdims=True)

    dgl1, dyg_direct, dgs, db_glu = rowwise(back_glu, [_to_segments(dmixed[:, AW:]), y_pre, gl1], [gs],
                                            [(SW, BF16), (SW, F32)], [SW, SW], name="back_glu")
    dyg_gate = mm_nt(dgl1, W["w_glu"], BF16, name="back_glu_gate")
    grads["w_glu"] = mm_tn(yg, dgl1, 1, name="grad_w_glu")
    group_b = ["w_up", "w_out", "w_glu"]
    sent_b = to_sibling(group_b, grads["w_glu"], "b")
    done_a = finish(group_a, sent_a, sent_b[1][-1], "a")

    du_seg, dbb_r, dbb_i, dcb_r, dcb_i, da_r, da_i, d_skip = ssm_bwd(
        u_seg, dyg_direct, dyg_gate, y_pre, st_r, st_i, *ssm_consts, deps=(done_a,), name="ssm_bwd")
    sent_b = to_chips(group_b, sent_b, du_seg, "b")

    def back_attn_norm(dm, a, g):
        da, dg = _rms_bwd(dm, a, g)
        prod = da * a
        delta = jnp.concatenate(
            [jnp.broadcast_to(jnp.sum(prod[:, h * HEAD_DIM:(h + 1) * HEAD_DIM], axis=-1, keepdims=True),
                              (prod.shape[0], HEAD_DIM)) for h in range(H)], axis=1)
        return da, delta, dg

    dattn, delta, dga = rowwise(back_attn_norm, [(dmixed, AW, 0), attn], [ga], [(AW, F32), (AW, F32)], [AW],
                                deps=(sent_b[1][-1],), name="back_attn_norm")
    dq, dk, dv = attn_bwd(proj, dattn, lse, delta, H, name="attn_bwd")
    dproj = jnp.concatenate([dq, dk, dv, _from_segments(du_seg)], axis=1)
    (dhn1,) = mm_nn(dproj, WT["w_in"], [BF16], name="back_proj_in")

    def back_in(dh, dhn, a, g):
        d1, dg = _rms_bwd(dhn, a, g)
        return dh + d1, dg

    grad_x, dg1 = rowwise(back_in, [dh1, dhn1, xs], [g1], [(D, F32)], [D], name="back_norm_in")

    cot = dict(
        mix_norm_pre=dg1, mix_norm_post=dg2, mlp_norm_pre=dg3, mlp_norm_post=dg4, ple_norm_pre=dg5, ple_norm_post=dg6,
        attn_out_norm=dga, ssm_out_norm=dgs, b_glu=db_glu, ssm_d=d_skip,
        ssm_c_re=_block_diag_part(dcb_r, P_, C_).swapaxes(1, 2), ssm_c_im=_block_diag_part(dcb_i, P_, C_).swapaxes(1, 2),
        a_r=da_r.reshape(G, P_), a_i=da_i.reshape(G, P_),
        bb_r=_block_diag_part(dbb_r, C_, P_).swapaxes(1, 2), bb_i=_block_diag_part(dbb_i, C_, P_).swapaxes(1, 2))
    names = list(cot)
    flat = jnp.concatenate([cot[n].reshape(-1) for n in names])
    total = flat.shape[0]
    rows_ = -(-total // (LANES * 16)) * 16
    flat = jnp.pad(flat, (0, rows_ * LANES - total)).reshape(rows_, LANES)
    gather_small = gather_start([flat], flat, name="gather_small_start")
    grads["w_in"] = mm_tn(hn1, dproj, N_DEV, deps=(gather_small[-1],), tko=2048, name="grad_w_in")
    group_c = ["w_in"]
    sent_c = to_sibling(group_c, grads["w_in"], "c")
    done_b = finish(group_b, sent_b, sent_c[1][-1], "b")
    sent_c = to_chips(group_c, sent_c, done_b, "c")
    gather_small = gather_forward(gather_small, sent_c[1][-1], name="gather_small_forward")
    (every,) = gather_finish(gather_small, gather_small[-1], name="gather_small_finish")
    (summed,) = _blocked(lambda *t: (functools.reduce(lambda a, b: a + b, t),),
                         [(every, functools.partial(lambda i, p_, j: (j, i), j=j)) for j in range(N_DEV)],
                         [((rows_, LANES), F32)], name="sum_small_grads")
    summed = summed.reshape(-1)
    red, off = {}, 0
    for n in names:
        sz = cot[n].size
        red[n] = summed[off:off + sz].reshape(cot[n].shape)
        off += sz
    _, pull = jax.vjp(_discretise, lam_re[0], lam_im[0], log_dt[0], ssm_b_re[0], ssm_b_im[0])
    d_lre, d_lim, d_ldt, d_bre, d_bim = pull((red["a_r"], red["a_i"], red["bb_r"], red["bb_i"]))
    red.update(lam_re=d_lre, lam_im=d_lim, log_dt=d_ldt, ssm_b_re=d_bre, ssm_b_im=d_bim)

    def pack(d):
        t = jnp.concatenate([d[n].reshape(-1) for n in small])
        r_ = -(-t.shape[0] // (LANES * 16)) * 16
        return jnp.pad(t, (0, r_ * LANES - t.shape[0])).reshape(r_, LANES)

    sw, sg_, sm, sv = pack(weights), pack(red), pack(mom_m), pack(mom_v)
    sd, snm, snv = _blocked(lambda w_, g_, m_, v_: _adamw(w_, g_, m_, v_), [sw, sg_, sm, sv],
                            [(sw.shape, F32)] * 3, name="adamw_small")
    finish(group_c, sent_c, snv, "c")
    off = 0
    for n in small:
        sz = weights[n].size
        shp = weights[n].shape
        out_g[n] = red[n].reshape(shp)
        out_d[n] = sd.reshape(-1)[off:off + sz].reshape(shp)
        out_m[n] = snm.reshape(-1)[off:off + sz].reshape(shp)
        out_v[n] = snv.reshape(-1)[off:off + sz].reshape(shp)
        off += sz

    return (loss, grad_x[None], *[out_g[n] for n in order], *[out_d[n] for n in order],
            *[out_m[n] for n in order], *[out_v[n] for n in order])
```

```python
import functools
import math

import jax
import jax.numpy as jnp
from jax import lax
from jax.experimental import pallas as pl
from jax.experimental.pallas import tpu as pltpu

F32 = jnp.float32
BF16 = jnp.bfloat16
MESH = pl.DeviceIdType.MESH

N_DEV = 8
LANES = 128
SUBLANES = 8
VMEM_LIMIT = 48 * 1024 * 1024
VMEM_LIMIT_SCAN = 60 * 1024 * 1024

HEAD_DIM = 128
BLK = 128
DILATIONS = (1, 4, 16)
SSM_GROUP = 16
SSM_STATE = 64
SLAB_GROUPS = LANES // SSM_GROUP
SLAB_STATES = SLAB_GROUPS * SSM_STATE
SEGMENTS = SUBLANES
SCAN_UNROLL = 4
RMS_EPS = 1e-6
NEG_INF = -1e30

ADAM_LR = 0.001
ADAM_B1 = 0.9
ADAM_B2 = 0.999
ADAM_EPS = 1e-08
ADAM_WD = 0.01
ADAM_STEP = 10


def _tile(n, pref, unit=LANES):
    if n <= pref:
        return n
    t = (pref // unit) * unit
    while t > unit and n % t:
        t -= unit
    assert n % t == 0, (n, pref, unit)
    return t


def _params(sem=None, vmem=VMEM_LIMIT):
    return pltpu.CompilerParams(dimension_semantics=sem, vmem_limit_bytes=vmem)


_NN = (((1,), (0,)), ((), ()))
_NT = (((1,), (1,)), ((), ()))
_TN = (((0,), (0,)), ((), ()))


_ANY = pl.BlockSpec(memory_space=pl.ANY)


def _mm_call(dims, nk, na, nb, pick, n_extra, n_dep, n_out, epi, **kw):
    first_extra = na + nb
    first_out = first_extra + n_extra + n_dep
    kw["in_specs"] = list(kw["in_specs"]) + [_ANY] * n_dep

    def single(*refs):
        extra = refs[first_extra:first_extra + n_extra]
        res = epi(lax.dot_general(refs[0][...], refs[1][...], dims, preferred_element_type=F32),
                  *[e[...] for e in extra])
        for o, r in zip(refs[first_out:first_out + n_out], res):
            o[...] = r.astype(o.dtype)

    if nk == 1:
        assert na == nb == 1
        kw["scratch_shapes"] = []
        return pl.pallas_call(single, **kw)

    def body(*refs):
        extra = refs[first_extra:first_extra + n_extra]
        outs = refs[first_out:first_out + n_out]
        acc = refs[-1]
        k = pl.program_id(2)

        @pl.when(k == 0)
        def _():
            acc[...] = jnp.zeros_like(acc)

        def add(a_ref, b_ref):
            acc[...] += lax.dot_general(a_ref[...], b_ref[...], dims, preferred_element_type=F32)

        if na == nb == 1:
            add(refs[0], refs[1])
        else:
            pa, pb = pick(pl.program_id(0), pl.program_id(1), k)
            for x in range(na):
                for y in range(nb):
                    pl.when((pa == x) & (pb == y))(functools.partial(add, refs[x], refs[na + y]))

        @pl.when(k == nk - 1)
        def _():
            res = epi(acc[...], *[e[...] for e in extra])
            for o, r in zip(outs, res):
                o[...] = r.astype(o.dtype)

    return pl.pallas_call(body, **kw)


def _identity_epi(acc):
    return (acc,)


def _parts(t):
    return list(t) if isinstance(t, (list, tuple)) else [t]


def _part_spec(block, part, which, index):
    def index_map(i, j, k):
        use = which(i, j, k) == part
        r, c = index(i, j, k)
        return jnp.where(use, r, 0), jnp.where(use, c, 0)
    return pl.BlockSpec(block, index_map)


def mm_nn(a, w, out_dtypes, *, name, epi=_identity_epi, bias=None, deps=(), tm=2048, tn=512, tk=2048):
    a = _parts(a)
    M, Kp = a[0].shape
    K = Kp * len(a)
    J, K2, n = w.shape
    assert K == K2
    tm, tn, tk = _tile(M, tm, 16), _tile(n, tn), (_tile(K, tk) if len(a) == 1 else Kp)
    npj = n // tn
    nk = K // tk
    if len(a) == 1:
        in_specs = [pl.BlockSpec((tm, tk), lambda i, j, k: (i, k))]
    else:
        in_specs = [pl.BlockSpec((tm, tk), lambda i, j, k: (i, 0)) for _ in a]
    in_specs.append(pl.BlockSpec((None, tk, tn), lambda i, j, k: (j // npj, k, j % npj)))
    args = a + [w]
    if bias is not None:
        in_specs.append(pl.BlockSpec((1, tn), lambda i, j, k: (0, j)))
        args.append(bias)
    return _mm_call(
        _NN, nk, len(a), 1, lambda i, j, k: (k, 0), len(args) - len(a) - 1, len(deps), len(out_dtypes), epi,
        out_shape=[jax.ShapeDtypeStruct((M, J * n), d) for d in out_dtypes],
        grid=(M // tm, J * npj, nk), in_specs=in_specs,
        out_specs=[pl.BlockSpec((tm, tn), lambda i, j, k: (i, j)) for _ in out_dtypes],
        scratch_shapes=[pltpu.VMEM((tm, tn), F32)],
        compiler_params=_params(("parallel", "parallel", "arbitrary")), name=name)(*args, *deps)


def mm_nt(a, w, out_dtype, *, name, epi=_identity_epi, extra=None, tm=2048, tko=512, tnr=2048):
    M, N = a.shape
    J, K, n = w.shape
    assert N == J * n
    tm, tko, tnr = _tile(M, tm, 16), _tile(K, tko), _tile(n, tnr)
    npj = n // tnr
    nk = N // tnr
    in_specs = [pl.BlockSpec((tm, tnr), lambda i, j, k: (i, k)),
                pl.BlockSpec((None, tko, tnr), lambda i, j, k: (k // npj, j, k % npj))]
    args = [a, w]
    if extra is not None:
        in_specs.append(pl.BlockSpec((tm, tko), lambda i, j, k: (i, j)))
        args.append(extra)
    return _mm_call(
        _NT, nk, 1, 1, None, len(args) - 2, 0, 1, epi,
        out_shape=[jax.ShapeDtypeStruct((M, K), out_dtype)],
        grid=(M // tm, K // tko, nk), in_specs=in_specs,
        out_specs=[pl.BlockSpec((tm, tko), lambda i, j, k: (i, j))],
        scratch_shapes=[pltpu.VMEM((tm, tko), F32)],
        compiler_params=_params(("parallel", "parallel", "arbitrary")), name=name)(*args)[0]


def mm_tn(a, b, J, *, name, deps=(), tko=1024, tn=1024, ts=2048):
    a, b = _parts(a), _parts(b)
    S, Kp = a[0].shape
    S2, Np = b[0].shape
    K, N = Kp * len(a), Np * len(b)
    assert S == S2 and N % J == 0
    n = N // J
    tko, tn, ts = _tile(Kp, tko), _tile(math.gcd(n, Np), tn), _tile(S, ts)
    npj = n // tn
    nk = S // ts
    ta, tb = Kp // tko, Np // tn
    assert nk > 1 or len(a) == len(b) == 1
    return _mm_call(
        _TN, nk, len(a), len(b), lambda i, j, k: (i // ta, j // tb), 0, len(deps), 1, _identity_epi,
        out_shape=[jax.ShapeDtypeStruct((J, K, n), BF16)],
        grid=(K // tko, J * npj, nk),
        in_specs=([_part_spec((ts, tko), x, lambda i, j, k: i // ta, lambda i, j, k: (k, i % ta)) for x in range(len(a))]
                  + [_part_spec((ts, tn), y, lambda i, j, k: j // tb, lambda i, j, k: (k, j % tb)) for y in range(len(b))]),
        out_specs=[pl.BlockSpec((None, tko, tn), lambda i, j, k: (j // npj, i, j % npj))],
        scratch_shapes=[pltpu.VMEM((tko, tn), F32)],
        compiler_params=_params(("parallel", "parallel", "arbitrary")), name=name)(*a, *b, *deps)[0]


def rowwise(fn, rows, vecs, outs, accs=(), *, name, deps=(), ts=256):
    rows = [r if isinstance(r, tuple) else (r, r.shape[1], 0) for r in rows]
    S = rows[0][0].shape[0]
    ts = _tile(S, ts, 16)
    nr, nv, no, nd = len(rows), len(vecs), len(outs), len(deps)

    def body(*refs):
        r, v = refs[:nr], refs[nr:nr + nv]
        o, a = refs[nr + nv + nd:nr + nv + nd + no], refs[nr + nv + nd + no:]
        res = fn(*[t[...].astype(F32) for t in r], *[t[...] for t in v])
        for ref, val in zip(o, res[:no]):
            ref[...] = val.astype(ref.dtype)
        if a:
            @pl.when(pl.program_id(0) == 0)
            def _():
                for ref in a:
                    ref[...] = jnp.zeros_like(ref)

            for ref, val in zip(a, res[no:]):
                ref[...] += val

    in_specs = [pl.BlockSpec((ts, w), functools.partial(lambda i, cb: (i, cb), cb=cb)) for _, w, cb in rows]
    in_specs += [pl.BlockSpec(v.shape, lambda i: (0, 0)) for v in vecs] + [_ANY] * nd
    out_shape = [jax.ShapeDtypeStruct((S, w), d) for w, d in outs]
    out_shape += [jax.ShapeDtypeStruct((1, w), F32) for w in accs]
    out_specs = [pl.BlockSpec((ts, w), lambda i: (i, 0)) for w, _ in outs]
    out_specs += [pl.BlockSpec((1, w), lambda i: (0, 0)) for w in accs]
    return pl.pallas_call(body, out_shape=out_shape, grid=(S // ts,), in_specs=in_specs, out_specs=out_specs,
                          compiler_params=_params(("arbitrary",)), name=name)(*[r[0] for r in rows], *vecs, *deps)


def _rms(x, g):
    r = lax.rsqrt(jnp.mean(x * x, axis=-1, keepdims=True) + RMS_EPS)
    return x * r * g


def _rms_bwd(dy, x, g):
    r = lax.rsqrt(jnp.mean(x * x, axis=-1, keepdims=True) + RMS_EPS)
    xh = x * r
    dxh = dy * g
    dx = r * (dxh - xh * jnp.mean(dxh * xh, axis=-1, keepdims=True))
    return dx, jnp.sum(dy * xh, axis=0, keepdims=True)


def _sigmoid(x):
    return pl.reciprocal(1.0 + jnp.exp(-x), approx=True)


_GELU_C = math.sqrt(2.0 / math.pi)


def _gelu(x):
    return 0.5 * x * (1.0 + jnp.tanh(_GELU_C * (x + 0.044715 * x * x * x)))


def _gelu_grad(x):
    t = jnp.tanh(_GELU_C * (x + 0.044715 * x * x * x))
    return 0.5 * (1.0 + t) + 0.5 * x * (1.0 - t * t) * _GELU_C * (1.0 + 3.0 * 0.044715 * x * x)


ATTN_INTERLEAVE = 8
KEY_PAD = BLK * max(DILATIONS)


def _key_mask(n):
    ii = lax.broadcasted_iota(jnp.int32, (BLK, 2 * BLK), 0)
    jj = lax.broadcasted_iota(jnp.int32, (BLK, 2 * BLK), 1)
    return ((jj < BLK) & (jj >= ii) & (n > 0)) | ((jj >= BLK) & (jj - BLK <= ii))


def _units(d, nblk):
    nb = nblk // d
    if nb == 2:
        def unit(idx):
            ii = lax.broadcasted_iota(jnp.int32, (2 * BLK, 2 * BLK), 0)
            jj = lax.broadcasted_iota(jnp.int32, (2 * BLK, 2 * BLK), 1)
            return pl.ds(idx, 2 * BLK, stride=d), pl.ds(KEY_PAD + idx, 2 * BLK, stride=d), (jj <= ii) & (ii - jj <= BLK)
        return d, max(1, ATTN_INTERLEAVE // 4), unit

    def unit(idx):
        r, n = idx // nb, idx % nb
        cur = r + n * (BLK * d)
        keys = cur + (KEY_PAD - BLK * d)
        if d == 1:
            return pl.ds(pl.multiple_of(cur, BLK), BLK), pl.ds(pl.multiple_of(keys, BLK), 2 * BLK), _key_mask(n)
        return pl.ds(cur, BLK, stride=d), pl.ds(keys, 2 * BLK, stride=d), _key_mask(n)
    return nblk, ATTN_INTERLEAVE, unit


def _pad_keys(dst, src):
    dst[pl.ds(0, KEY_PAD), :] = jnp.zeros((KEY_PAD, dst.shape[1]), F32)

    def copy(c, carry):
        dst[pl.ds(pl.multiple_of(KEY_PAD + c * BLK, BLK), BLK), :] = src[pl.ds(pl.multiple_of(c * BLK, BLK), BLK), :]
        return carry

    lax.fori_loop(0, src.shape[0] // BLK, copy, 0)


def attn_fwd(proj, n_heads, *, name):
    S, WP = proj.shape
    assert S % (BLK * max(DILATIONS)) == 0
    nblk = S // BLK
    AW = n_heads * HEAD_DIM
    scale = 1.0 / math.sqrt(HEAD_DIM)

    def body(q_ref, k_ref, v_ref, o_ref, l_ref, acc, mrun, lrun, kp, vp):
        _pad_keys(kp, k_ref)
        _pad_keys(vp, v_ref)
        for first, d in zip((True, False, False), reversed(DILATIONS)):
            n_units, per_step, unit = _units(d, nblk)

            def step(it, carry, first=first, n_units=n_units, per_step=per_step, unit=unit):
                units = [unit(it + j * (n_units // per_step)) for j in range(per_step)]
                ss = [lax.dot_general(q_ref[cur, :].astype(BF16), kp[keys, :].astype(BF16), _NT,
                                      preferred_element_type=F32) * scale for cur, keys, _ in units]
                ss = [jnp.where(mask, s, NEG_INF) for s, (_, _, mask) in zip(ss, units)]
                ms = [jnp.max(s, axis=-1, keepdims=True) for s in ss]
                ps = [jnp.exp(s - m) for s, m in zip(ss, ms)]
                ls = [jnp.sum(p, axis=-1, keepdims=True) for p in ps]
                os_ = [jnp.dot(p.astype(BF16), vp[keys, :].astype(BF16), preferred_element_type=F32)
                       for p, (_, keys, _) in zip(ps, units)]
                for (cur, keys, mask), m, l, o in zip(units, ms, ls, os_):
                    m = jnp.broadcast_to(m, o.shape)
                    l = jnp.broadcast_to(l, o.shape)
                    if first:
                        acc[cur, :], mrun[cur, :], lrun[cur, :] = o, m, l
                    else:
                        m_old = mrun[cur, :]
                        m_new = jnp.maximum(m_old, m)
                        w_old, w_blk = jnp.exp(m_old - m_new), jnp.exp(m - m_new)
                        acc[cur, :] = w_old * acc[cur, :] + w_blk * o
                        lrun[cur, :] = w_old * lrun[cur, :] + w_blk * l
                        mrun[cur, :] = m_new
                return carry

            lax.fori_loop(0, n_units // per_step, step, 0)

        def finish(c, carry):
            r = pl.ds(pl.multiple_of(c * BLK, BLK), BLK)
            o_ref[r, :] = acc[r, :] / lrun[r, :]
            l_ref[r, :] = mrun[r, :] + jnp.log(lrun[r, :])
            return carry

        lax.fori_loop(0, nblk, finish, 0)

    def col(off):
        return pl.BlockSpec((S, HEAD_DIM), lambda h: (0, off + h))

    ospec = pl.BlockSpec((S, HEAD_DIM), lambda h: (0, h))
    return pl.pallas_call(
        body, out_shape=[jax.ShapeDtypeStruct((S, AW), F32)] * 2, grid=(n_heads,),
        in_specs=[col(0), col(n_heads), col(2 * n_heads)], out_specs=[ospec, ospec],
        scratch_shapes=[pltpu.VMEM((S, HEAD_DIM), F32)] * 3 + [pltpu.VMEM((KEY_PAD + S, HEAD_DIM), F32)] * 2,
        compiler_params=_params(("parallel",)), name=name)(proj, proj, proj)


def attn_bwd(proj, do, lse, delta, n_heads, *, name):
    S, WP = proj.shape
    nblk = S // BLK
    AW = n_heads * HEAD_DIM
    scale = 1.0 / math.sqrt(HEAD_DIM)

    def body(q_ref, k_ref, v_ref, do_ref, l_ref, dl_ref, dq_ref, dk_ref, dv_ref, dq_sc, dk_sc, dv_sc, kp, vp):
        _pad_keys(kp, k_ref)
        _pad_keys(vp, v_ref)
        order = list(reversed(DILATIONS))
        assign_first = nblk // order[0] == 2
        if assign_first:
            dk_sc[pl.ds(0, KEY_PAD), :] = jnp.zeros((KEY_PAD, HEAD_DIM), F32)
            dv_sc[pl.ds(0, KEY_PAD), :] = jnp.zeros((KEY_PAD, HEAD_DIM), F32)
        else:
            dq_sc[...] = jnp.zeros_like(dq_sc)
            dk_sc[...] = jnp.zeros_like(dk_sc)
            dv_sc[...] = jnp.zeros_like(dv_sc)
        for assign, d in zip((assign_first, False, False), order):
            n_units, per_step, unit = _units(d, nblk)

            def step(it, carry, n_units=n_units, per_step=per_step, unit=unit, assign=assign):
                units = [unit(it + j * (n_units // per_step)) for j in range(per_step)]
                qs = [q_ref[cur, :].astype(BF16) for cur, _, _ in units]
                gs = [do_ref[cur, :].astype(BF16) for cur, _, _ in units]
                ks = [kp[keys, :].astype(BF16) for _, keys, _ in units]
                ss = [lax.dot_general(q, kb, _NT, preferred_element_type=F32) * scale for q, kb in zip(qs, ks)]
                dps = [lax.dot_general(g, vp[keys, :].astype(BF16), _NT, preferred_element_type=F32)
                       for g, (_, keys, _) in zip(gs, units)]
                ps = [jnp.where(mask, jnp.exp(s - l_ref[cur, :][:, :1]), 0.0) for s, (cur, _, mask) in zip(ss, units)]
                dss = [(p * (dp - dl_ref[cur, :][:, :1]) * scale).astype(BF16)
                       for p, dp, (cur, _, _) in zip(ps, dps, units)]
                for (cur, keys, _), q, g, kb, p, ds in zip(units, qs, gs, ks, ps, dss):
                    dq = jnp.dot(ds, kb, preferred_element_type=F32)
                    dk = lax.dot_general(ds, q, _TN, preferred_element_type=F32)
                    dv = lax.dot_general(p.astype(BF16), g, _TN, preferred_element_type=F32)
                    if assign:
                        dq_sc[cur, :], dk_sc[keys, :], dv_sc[keys, :] = dq, dk, dv
                    else:
                        dq_sc[cur, :] += dq
                        dk_sc[keys, :] += dk
                        dv_sc[keys, :] += dv
                return carry

            lax.fori_loop(0, n_units // per_step, step, 0)
        rows = pl.ds(KEY_PAD, S)
        dq_ref[...] = dq_sc[...].astype(BF16)
        dk_ref[...] = dk_sc[rows, :].astype(BF16)
        dv_ref[...] = dv_sc[rows, :].astype(BF16)

    def col(off):
        return pl.BlockSpec((S, HEAD_DIM), lambda h: (0, off + h))

    ospec = pl.BlockSpec((S, HEAD_DIM), lambda h: (0, h))
    return pl.pallas_call(
        body, out_shape=[jax.ShapeDtypeStruct((S, AW), BF16)] * 3, grid=(n_heads,),
        in_specs=[col(0), col(n_heads), col(2 * n_heads), ospec, ospec, ospec], out_specs=[ospec] * 3,
        scratch_shapes=[pltpu.VMEM((S, HEAD_DIM), F32)] + [pltpu.VMEM((KEY_PAD + S, HEAD_DIM), F32)] * 4,
        compiler_params=_params(("parallel",), VMEM_LIMIT_SCAN), name=name)(proj, proj, proj, do, lse, delta)


def _to_segments(t):
    S, W = t.shape
    return t.reshape(SEGMENTS, S // SEGMENTS, W).swapaxes(0, 1).reshape(S, W)


def _from_segments(t):
    S, W = t.shape
    return t.reshape(S // SEGMENTS, SEGMENTS, W).swapaxes(0, 1).reshape(S, W)


def _cmul(ar, ai, br, bi):
    return ar * br - ai * bi, ar * bi + ai * br


def _power(ar, ai, log2n):
    for _ in range(log2n):
        ar, ai = _cmul(ar, ai, ar, ai)
    return ar, ai


def _shift_rows(x, up):
    row = lax.broadcasted_iota(jnp.int32, x.shape, 0)
    if up:
        return jnp.where(row == SEGMENTS - 1, 0.0, pltpu.roll(x, SEGMENTS - 1, 0))
    return jnp.where(row == 0, 0.0, pltpu.roll(x, 1, 0))


def _segment_carries(er, ei, pr, pi, up):
    cr = jnp.zeros_like(er)
    ci = jnp.zeros_like(ei)
    for _ in range(SEGMENTS - 1):
        tr, ti = _cmul(pr, pi, cr, ci)
        cr, ci = _shift_rows(er + tr, up), _shift_rows(ei + ti, up)
    return cr, ci


def _scan_states(sr, si, ar, ai, T, reverse):
    ns = sr.shape[1]
    ar8 = jnp.broadcast_to(ar, (SEGMENTS, ns))
    ai8 = jnp.broadcast_to(ai, (SEGMENTS, ns))

    def rows(t):
        k = (T - 1 - t) if reverse else t
        return pl.ds(pl.multiple_of(k * SEGMENTS, SEGMENTS), SEGMENTS)

    def advance(t, c):
        tr, ti = _cmul(ar8, ai8, c[0], c[1])
        return tr + sr[rows(t), :], ti + si[rows(t), :]

    def several(step):
        def trip(t, c):
            for j in range(SCAN_UNROLL):
                c = step(t * SCAN_UNROLL + j, c)
            return c
        return trip

    zero = jnp.zeros((SEGMENTS, ns), F32)
    er, ei = lax.fori_loop(0, T // SCAN_UNROLL, several(advance), (zero, zero))
    pr, pi = _power(ar, ai, T.bit_length() - 1)
    cr, ci = _segment_carries(er, ei, jnp.broadcast_to(pr, (SEGMENTS, ns)), jnp.broadcast_to(pi, (SEGMENTS, ns)), reverse)

    def store(t, c):
        nr, ni = advance(t, c)
        sr[rows(t), :] = nr
        si[rows(t), :] = ni
        return nr, ni

    lax.fori_loop(0, T // SCAN_UNROLL, several(store), (cr, ci))
    return cr, ci


def _slab_specs(ns):
    return [pl.BlockSpec((None, LANES, ns), lambda g: (g, 0, 0)),
            pl.BlockSpec((None, LANES, ns), lambda g: (g, 0, 0)),
            pl.BlockSpec((None, 1, ns), lambda g: (g, 0, 0)),
            pl.BlockSpec((None, 1, ns), lambda g: (g, 0, 0)),
            pl.BlockSpec((None, ns, LANES), lambda g: (g, 0, 0)),
            pl.BlockSpec((None, ns, LANES), lambda g: (g, 0, 0)),
            pl.BlockSpec((1, LANES), lambda g: (0, g))]


def _chunks(S):
    rc = _tile(S, 512, 16)
    return rc, S // rc


def ssm_fwd(u, bbr, bbi, ar, ai, cbr, cbi, dsk, *, name):
    S, SW = u.shape
    nslab, _, ns = bbr.shape
    T = S // SEGMENTS
    assert T & (T - 1) == 0
    rc, nc = _chunks(S)

    def body(u_ref, br_ref, bi_ref, ar_ref, ai_ref, cr_ref, ci_ref, d_ref, y_ref, yg_ref, str_ref, sti_ref, sr, si):
        def inputs(c, carry):
            r = pl.ds(pl.multiple_of(c * rc, rc), rc)
            sr[r, :] = jnp.dot(u_ref[r, :], br_ref[...], preferred_element_type=F32)
            si[r, :] = jnp.dot(u_ref[r, :], bi_ref[...], preferred_element_type=F32)
            return carry

        lax.fori_loop(0, nc, inputs, 0)
        _scan_states(sr, si, ar_ref[...], ai_ref[...], T, False)

        def outputs(c, carry):
            r = pl.ds(pl.multiple_of(c * rc, rc), rc)
            srb, sib = sr[r, :].astype(BF16), si[r, :].astype(BF16)
            str_ref[r, :] = srb
            sti_ref[r, :] = sib
            y = (jnp.dot(srb, cr_ref[...], preferred_element_type=F32)
                 - jnp.dot(sib, ci_ref[...], preferred_element_type=F32) + d_ref[...] * u_ref[r, :].astype(F32))
            y_ref[r, :] = y
            yg_ref[r, :] = _gelu(y).astype(BF16)
            return carry

        lax.fori_loop(0, nc, outputs, 0)

    slab = pl.BlockSpec((S, LANES), lambda g: (0, g))
    states = pl.BlockSpec((S, ns), lambda g: (0, g))
    return pl.pallas_call(
        body, out_shape=([jax.ShapeDtypeStruct((S, SW), F32), jax.ShapeDtypeStruct((S, SW), BF16)]
                         + [jax.ShapeDtypeStruct((S, nslab * ns), BF16)] * 2),
        grid=(nslab,), in_specs=[slab] + _slab_specs(ns), out_specs=[slab, slab, states, states],
        scratch_shapes=[pltpu.VMEM((S, ns), F32)] * 2,
        compiler_params=_params(("parallel",), VMEM_LIMIT_SCAN), name=name)(u, bbr, bbi, ar, ai, cbr, cbi, dsk)


def ssm_bwd(u, d_direct, d_gate, y, st_r, st_i, bbr, bbi, ar, ai, cbr, cbi, dsk, *, name, deps=()):
    S, SW = u.shape
    nslab, _, ns = bbr.shape
    T = S // SEGMENTS
    rc, nc = _chunks(S)
    pair_rows = 2 * SEGMENTS

    def body(*refs):
        (u_ref, d1_ref, d2_ref, y_ref, sr_ref, si_ref, br_ref, bi_ref, ar_ref, ai_ref, cr_ref, ci_ref,
         d_ref) = refs[:13]
        (du_ref, dbr_ref, dbi_ref, dcr_ref, dci_ref, dar_ref, dai_ref, dd_ref, lr, li,
         dy_ref) = refs[13 + len(deps):]

        def inputs(c, skip):
            r = pl.ds(pl.multiple_of(c * rc, rc), rc)
            dy = (d1_ref[r, :] + d2_ref[r, :].astype(F32)) * _gelu_grad(y_ref[r, :])
            dy_ref[r, :] = dy
            gb = dy.astype(BF16)
            lr[r, :] = lax.dot_general(gb, cr_ref[...], _NT, preferred_element_type=F32)
            li[r, :] = -lax.dot_general(gb, ci_ref[...], _NT, preferred_element_type=F32)
            return skip + jnp.sum(dy * u_ref[r, :].astype(F32), axis=0, keepdims=True)

        dd_ref[...] = lax.fori_loop(0, nc, inputs, jnp.zeros((1, LANES), F32))
        _scan_states(lr, li, ar_ref[...], -ai_ref[...], T, True)

        def steps(j):
            rows = pl.ds(pl.multiple_of(j * pair_rows, pair_rows), pair_rows)
            tr, ti = sr_ref[rows, :].astype(F32), si_ref[rows, :].astype(F32)
            return tr[:SEGMENTS], tr[SEGMENTS:], ti[:SEGMENTS], ti[SEGMENTS:]

        def pair(j, c):
            acc_r, acc_i, pr, pi = c
            lo_r, hi_r, lo_i, hi_i = steps(j)
            first = pl.ds(pl.multiple_of(j * pair_rows, SEGMENTS), SEGMENTS)
            second = pl.ds(pl.multiple_of(j * pair_rows + SEGMENTS, SEGMENTS), SEGMENTS)
            la_r, la_i, lb_r, lb_i = lr[first, :], li[first, :], lr[second, :], li[second, :]
            return (acc_r + la_r * pr + la_i * pi + lb_r * lo_r + lb_i * lo_i,
                    acc_i - la_r * pi + la_i * pr - lb_r * lo_i + lb_i * lo_r, hi_r, hi_i)

        def pairs(t, c):
            return pair(2 * t + 1, pair(2 * t, c))

        _, end_r, _, end_i = steps(T // 2 - 1)
        zero = jnp.zeros((SEGMENTS, ns), F32)
        acc = lax.fori_loop(0, T // 4, pairs, (zero, zero, _shift_rows(end_r, False), _shift_rows(end_i, False)))
        dar_ref[...] = jnp.sum(acc[0], axis=0, keepdims=True)
        dai_ref[...] = jnp.sum(acc[1], axis=0, keepdims=True)

        dbr_ref[...] = jnp.zeros_like(dbr_ref)
        dbi_ref[...] = jnp.zeros_like(dbi_ref)
        dcr_ref[...] = jnp.zeros_like(dcr_ref)
        dci_ref[...] = jnp.zeros_like(dci_ref)

        def outputs(c, carry):
            r = pl.ds(pl.multiple_of(c * rc, rc), rc)
            ub = u_ref[r, :]
            g = dy_ref[r, :]
            gb = g.astype(BF16)
            lrb = lr[r, :].astype(BF16)
            lib = li[r, :].astype(BF16)
            du_ref[r, :] = (lax.dot_general(lrb, br_ref[...], _NT, preferred_element_type=F32)
                            + lax.dot_general(lib, bi_ref[...], _NT, preferred_element_type=F32)
                            + d_ref[...] * g).astype(BF16)
            dbr_ref[...] += lax.dot_general(ub, lrb, _TN, preferred_element_type=F32)
            dbi_ref[...] += lax.dot_general(ub, lib, _TN, preferred_element_type=F32)
            dcr_ref[...] += lax.dot_general(sr_ref[r, :], gb, _TN, preferred_element_type=F32)
            dci_ref[...] -= lax.dot_general(si_ref[r, :], gb, _TN, preferred_element_type=F32)
            return carry

        lax.fori_loop(0, nc, outputs, 0)

    slab = pl.BlockSpec((S, LANES), lambda g: (0, g))
    states = pl.BlockSpec((S, ns), lambda g: (0, g))
    bspec = pl.BlockSpec((None, LANES, ns), lambda g: (g, 0, 0))
    cspec = pl.BlockSpec((None, ns, LANES), lambda g: (g, 0, 0))
    aspec = pl.BlockSpec((None, 1, ns), lambda g: (g, 0, 0))
    return pl.pallas_call(
        body,
        out_shape=[jax.ShapeDtypeStruct((S, SW), BF16),
                   jax.ShapeDtypeStruct((nslab, LANES, ns), F32), jax.ShapeDtypeStruct((nslab, LANES, ns), F32),
                   jax.ShapeDtypeStruct((nslab, ns, LANES), F32), jax.ShapeDtypeStruct((nslab, ns, LANES), F32),
                   jax.ShapeDtypeStruct((nslab, 1, ns), F32), jax.ShapeDtypeStruct((nslab, 1, ns), F32),
                   jax.ShapeDtypeStruct((1, SW), F32)],
        grid=(nslab,), in_specs=[slab, slab, slab, slab, states, states] + _slab_specs(ns) + [_ANY] * len(deps),
        out_specs=[slab, bspec, bspec, cspec, cspec, aspec, aspec, pl.BlockSpec((1, LANES), lambda g: (0, g))],
        scratch_shapes=[pltpu.VMEM((S, ns), F32)] * 2 + [pltpu.VMEM((S, LANES), F32)],
        compiler_params=_params(("parallel",), VMEM_LIMIT_SCAN), name=name)(
            u, d_direct, d_gate, y, st_r, st_i, bbr, bbi, ar, ai, cbr, cbi, dsk, *deps)


def _discretise(lam_re, lam_im, log_dt, b_re, b_im):
    dt = jnp.exp(log_dt)[:, None]
    mag = jnp.exp(lam_re * dt)
    ar = mag * jnp.cos(lam_im * dt)
    ai = mag * jnp.sin(lam_im * dt)
    nr, ni = ar - 1.0, ai
    den = lam_re * lam_re + lam_im * lam_im
    cr = ((nr * lam_re + ni * lam_im) / den)[..., None]
    ci = ((ni * lam_re - nr * lam_im) / den)[..., None]
    return ar, ai, cr * b_re - ci * b_im, cr * b_im + ci * b_re


def _block_diag(t, nslab):
    G, R, C = t.shape
    eye = jnp.eye(SLAB_GROUPS, dtype=t.dtype)
    t = t.reshape(nslab, SLAB_GROUPS, R, C)
    return jnp.einsum('sgrc,gh->sgrhc', t, eye).reshape(nslab, SLAB_GROUPS * R, SLAB_GROUPS * C)


def _block_diag_part(t, R, C):
    nslab = t.shape[0]
    eye = jnp.eye(SLAB_GROUPS, dtype=t.dtype)
    t = t.reshape(nslab, SLAB_GROUPS, R, SLAB_GROUPS, C)
    return jnp.einsum('sgrhc,gh->sgrc', t, eye).reshape(nslab * SLAB_GROUPS, R, C)


def _place():
    return lax.axis_index("x"), lax.axis_index("y"), lax.axis_index("c")


_HBM = pl.BlockSpec(memory_space=pltpu.HBM)
_SEM = pl.BlockSpec(memory_space=pltpu.SEMAPHORE)
_ORDERED_EFFECT = pltpu.SideEffectType.DATAFLOW_SIDE_EFFECTING


def _split_call(name, srcs, zones, sems_in, n_new, body_fn, after):
    nsrc, nz, ns, nn = len(srcs), len(zones), len(sems_in), len(n_new)
    nb = nsrc + nz

    def body(*refs):
        outs = refs[nb + ns + 1:]
        body_fn(refs[:nb], refs[nb:nb + ns], outs[:nn])
        outs[nn + nz][...] = jnp.zeros((SUBLANES, LANES), F32)

    res = pl.pallas_call(
        body, name=name,
        out_shape=([pltpu.SemaphoreType.DMA((n,)) for n in n_new] + [pltpu.HBM(b.shape, b.dtype) for b in zones]
                   + [jax.ShapeDtypeStruct((SUBLANES, LANES), F32)]),
        in_specs=[_HBM] * nb + [_SEM] * ns + [_ANY],
        out_specs=[_SEM] * nn + [_HBM] * nz + [pl.BlockSpec(memory_space=pltpu.VMEM)],
        input_output_aliases={nsrc + i: nn + i for i in range(nz)},
        compiler_params=pltpu.CompilerParams(has_side_effects=_ORDERED_EFFECT))(
            *[pltpu.with_memory_space_constraint(b, pltpu.HBM) for b in list(srcs) + list(zones)], *sems_in, after)
    return list(res[:nn]), list(res[nn:nn + nz]), res[-1]


def _mesh_peers():
    x, y, c = _place()
    return x, y, c, (x, y, 1 - c), [(1 - x, y), (x, 1 - y), (1 - x, 1 - y)]


def gather_start(shards, after, *, name):
    nw = len(shards)
    x, y, c = _place()
    zones = [lax.dynamic_update_slice(lax.empty((N_DEV,) + s.shape, s.dtype), s[None], (4 * x + 2 * y + c, 0, 0))
             for s in shards]

    def body(bufs, taken, new):
        for cp in _gather_first(bufs, nw, new[0], new[1]):
            cp.start()

    sems, zones, token = _split_call(name, shards, zones, [], [4 * nw, 4 * nw], body, after)
    return shards, sems, zones, token


def _gather_first(bufs, nw, send, recv):
    x, y, c, sibling, chips = _mesh_peers()
    out = []
    for w in range(nw):
        slot = bufs[nw + w].at[4 * x + 2 * y + c]
        for k, to in enumerate([sibling] + [(*ch, c) for ch in chips]):
            out.append(pltpu.make_async_remote_copy(
                src_ref=bufs[w], dst_ref=slot, send_sem=send.at[4 * w + k], recv_sem=recv.at[4 * w + k],
                device_id=to, device_id_type=MESH))
    return out


def _gather_slot_copy(bufs, nw, w, block, send_sem, recv_sem, to):
    px, py, pc = block
    slot = bufs[nw + w].at[4 * px + 2 * py + pc]
    return pltpu.make_async_remote_copy(src_ref=slot, dst_ref=slot, send_sem=send_sem, recv_sem=recv_sem,
                                        device_id=to, device_id_type=MESH)


def gather_forward(state, after, *, name):
    shards, sems, zones, _ = state
    nw = len(shards)

    def body(bufs, taken, new):
        x, y, c, sibling, chips = _mesh_peers()
        for j, ch in enumerate(chips):
            for w in range(nw):
                k = 4 * w + 1 + j
                _gather_slot_copy(bufs, nw, w, (*ch, c), taken[0].at[k], taken[1].at[k], (*ch, c)).wait_recv()
                _gather_slot_copy(bufs, nw, w, (*ch, c), new[0].at[3 * w + j], new[1].at[3 * w + j], sibling).start()
        for w in range(nw):
            _gather_slot_copy(bufs, nw, w, sibling, taken[0].at[4 * w], taken[1].at[4 * w], sibling).wait_recv()
        for cp in _gather_first(bufs, nw, taken[0], taken[1]):
            cp.wait_send()

    sems, zones, token = _split_call(name, shards, zones, sems, [3 * nw, 3 * nw], body, after)
    return shards, sems, zones, token


def gather_finish(state, after, *, name):
    shards, sems, zones, _ = state
    nw = len(shards)

    def body(bufs, taken, new):
        x, y, c, sibling, chips = _mesh_peers()
        for w in range(nw):
            for j, ch in enumerate(chips):
                cp = _gather_slot_copy(bufs, nw, w, (*ch, 1 - c), taken[0].at[3 * w + j], taken[1].at[3 * w + j], sibling)
                cp.wait_send()
                cp.wait_recv()

    _, zones, _ = _split_call(name, shards, zones, sems, [], body, after)
    return zones


def exchange_start(srcs, zone_shapes, copies, n, after, *, name):
    nw = len(srcs)
    zones = [lax.empty(z, s.dtype) for z, s in zip(zone_shapes, srcs)]

    def body(bufs, taken, new):
        for cp in copies(bufs[:nw], bufs[nw:], new[0], new[1]):
            cp.start()

    sems, zones, token = _split_call(name, srcs, zones, [], [n, n], body, after)
    return srcs, copies, sems, zones, token


def exchange_wait(state, after, *, name):
    srcs, copies, sems, zones, _ = state
    nw = len(srcs)

    def body(bufs, taken, new):
        for cp in copies(bufs[:nw], bufs[nw:], taken[0], taken[1]):
            cp.wait_send()
            cp.wait_recv()

    _, zones, _ = _split_call(name, srcs, zones, sems, [], body, after)
    return zones


def _core_copies(srcs, zones, send, recv):
    x, y, c = _place()
    return [pltpu.make_async_remote_copy(
        src_ref=srcs[w].at[:, 1 - c], dst_ref=zones[w], send_sem=send.at[w], recv_sem=recv.at[w],
        device_id=(x, y, 1 - c), device_id_type=MESH) for w in range(len(srcs))]


def _chip_copies(srcs, zones, send, recv):
    x, y, c = _place()
    chips = [(1 - x, y), (x, 1 - y), (1 - x, 1 - y)]
    return [pltpu.make_async_remote_copy(
        src_ref=srcs[w].at[2 * cx + cy], dst_ref=zones[w].at[j], send_sem=send.at[3 * w + j],
        recv_sem=recv.at[3 * w + j], device_id=(cx, cy, c), device_id_type=MESH)
        for w in range(len(srcs)) for j, (cx, cy) in enumerate(chips)]


def _blocked(fn, ins, outs, *, name, place=None, tr=256):
    k, n = outs[0][0]
    tr = _tile(k, tr, 16)
    if place is None:
        place = jnp.zeros((1,), jnp.int32)
    specs = []
    args = []
    for a in ins:
        if isinstance(a, tuple):
            arr, lead = a
            specs.append(pl.BlockSpec((None, tr, n), functools.partial(lambda i, s, lead: (*lead(i, s), 0), lead=lead)))
            args.append(arr)
        else:
            specs.append(pl.BlockSpec((tr, n), lambda i, s: (i, 0)))
            args.append(a)
    nin = len(args)

    def body(place_ref, *refs):
        res = fn(*[r[...] for r in refs[:nin]])
        for ref, val in zip(refs[nin:], res):
            ref[...] = val.astype(ref.dtype)

    return pl.pallas_call(
        body, out_shape=[jax.ShapeDtypeStruct(s, d) for s, d in outs],
        grid_spec=pltpu.PrefetchScalarGridSpec(
            num_scalar_prefetch=1, grid=(k // tr,), in_specs=specs,
            out_specs=[pl.BlockSpec((tr, n), lambda i, s: (i, 0)) for _ in outs]),
        compiler_params=_params(("parallel",)), name=name)(place, *args)


def _adamw(w, g, m, v):
    m = ADAM_B1 * m + (1.0 - ADAM_B1) * g
    v = ADAM_B2 * v + (1.0 - ADAM_B2) * (g * g)
    m_hat = m / (1.0 - ADAM_B1 ** ADAM_STEP)
    v_hat = v / (1.0 - ADAM_B2 ** ADAM_STEP)
    delta = -ADAM_LR * (m_hat * pl.reciprocal(jnp.sqrt(v_hat) + ADAM_EPS, approx=True) + ADAM_WD * w)
    return delta, m, v


def kernel(x, p, mix_norm_pre, w_in, lam_re, lam_im, log_dt, ssm_b_re, ssm_b_im, ssm_c_re, ssm_c_im, ssm_d, w_glu, b_glu, attn_out_norm, ssm_out_norm, w_out, mix_norm_post, mlp_norm_pre, w_up, w_down, mlp_norm_post, ple_norm_pre, w_ple_gate, w_ple_proj, ple_norm_post, loss_target, m_mix_norm_pre, m_w_in, m_lam_re, m_lam_im, m_log_dt, m_ssm_b_re, m_ssm_b_im, m_ssm_c_re, m_ssm_c_im, m_ssm_d, m_w_glu, m_b_glu, m_attn_out_norm, m_ssm_out_norm, m_w_out, m_mix_norm_post, m_mlp_norm_pre, m_w_up, m_w_down, m_mlp_norm_post, m_ple_norm_pre, m_w_ple_gate, m_w_ple_proj, m_ple_norm_post, v_mix_norm_pre, v_w_in, v_lam_re, v_lam_im, v_log_dt, v_ssm_b_re, v_ssm_b_im, v_ssm_c_re, v_ssm_c_im, v_ssm_d, v_w_glu, v_b_glu, v_attn_out_norm, v_ssm_out_norm, v_w_out, v_mix_norm_post, v_mlp_norm_pre, v_w_up, v_w_down, v_mlp_norm_post, v_ple_norm_pre, v_w_ple_gate, v_w_ple_proj, v_ple_norm_post):
    weights = dict(mix_norm_pre=mix_norm_pre, w_in=w_in, lam_re=lam_re, lam_im=lam_im, log_dt=log_dt, ssm_b_re=ssm_b_re, ssm_b_im=ssm_b_im, ssm_c_re=ssm_c_re, ssm_c_im=ssm_c_im, ssm_d=ssm_d, w_glu=w_glu, b_glu=b_glu, attn_out_norm=attn_out_norm, ssm_out_norm=ssm_out_norm, w_out=w_out, mix_norm_post=mix_norm_post, mlp_norm_pre=mlp_norm_pre, w_up=w_up, w_down=w_down, mlp_norm_post=mlp_norm_post, ple_norm_pre=ple_norm_pre, w_ple_gate=w_ple_gate, w_ple_proj=w_ple_proj, ple_norm_post=ple_norm_post)
    mom_m = dict(mix_norm_pre=m_mix_norm_pre, w_in=m_w_in, lam_re=m_lam_re, lam_im=m_lam_im, log_dt=m_log_dt, ssm_b_re=m_ssm_b_re, ssm_b_im=m_ssm_b_im, ssm_c_re=m_ssm_c_re, ssm_c_im=m_ssm_c_im, ssm_d=m_ssm_d, w_glu=m_w_glu, b_glu=m_b_glu, attn_out_norm=m_attn_out_norm, ssm_out_norm=m_ssm_out_norm, w_out=m_w_out, mix_norm_post=m_mix_norm_post, mlp_norm_pre=m_mlp_norm_pre, w_up=m_w_up, w_down=m_w_down, mlp_norm_post=m_mlp_norm_post, ple_norm_pre=m_ple_norm_pre, w_ple_gate=m_w_ple_gate, w_ple_proj=m_w_ple_proj, ple_norm_post=m_ple_norm_post)
    mom_v = dict(mix_norm_pre=v_mix_norm_pre, w_in=v_w_in, lam_re=v_lam_re, lam_im=v_lam_im, log_dt=v_log_dt, ssm_b_re=v_ssm_b_re, ssm_b_im=v_ssm_b_im, ssm_c_re=v_ssm_c_re, ssm_c_im=v_ssm_c_im, ssm_d=v_ssm_d, w_glu=v_w_glu, b_glu=v_b_glu, attn_out_norm=v_attn_out_norm, ssm_out_norm=v_ssm_out_norm, w_out=v_w_out, mix_norm_post=v_mix_norm_post, mlp_norm_pre=v_mlp_norm_pre, w_up=v_w_up, w_down=v_w_down, mlp_norm_post=v_mlp_norm_post, ple_norm_pre=v_ple_norm_pre, w_ple_gate=v_w_ple_gate, w_ple_proj=v_w_ple_proj, ple_norm_post=v_ple_norm_post)
    order = list(weights)
    big = ["w_in", "w_glu", "w_out", "w_up", "w_down", "w_ple_gate", "w_ple_proj"]
    col_sharded = {"w_in", "w_up", "w_ple_proj"}
    small = [n for n in order if n not in big]

    _, S, D = x.shape
    xs = x[0]
    tgt = loss_target[0]
    AW = attn_out_norm.shape[1]
    SW = ssm_d.shape[1]
    H = AW // HEAD_DIM
    G = SW // SSM_GROUP
    nslab = G // SLAB_GROUPS
    P_, C_ = SSM_STATE, SSM_GROUP

    shard = {n: weights[n][0].astype(BF16) for n in big}
    W, WT = {}, {}

    def arrived(names, gathered):
        for n, g in zip(names, gathered):
            W[n] = g if n in col_sharded else g.reshape(1, N_DEV * g.shape[1], g.shape[2])

    def transposed(g):
        return jnp.swapaxes(g, 1, 2).reshape(1, g.shape[0] * g.shape[2], g.shape[1])

    g1, g2, g3, g4, g5, g6 = (weights[n] for n in ("mix_norm_pre", "mix_norm_post", "mlp_norm_pre",
                                                      "mlp_norm_post", "ple_norm_pre", "ple_norm_post"))
    ga, gs = attn_out_norm, ssm_out_norm
    gather_in = gather_start([shard["w_in"]], shard["w_in"], name="gather_w_in_start")
    (hn1,) = rowwise(lambda a, g: (_rms(a, g),), [xs], [g1], [(D, BF16)], deps=(gather_in[-1],), name="norm_in")
    gather_in = gather_forward(gather_in, hn1, name="gather_w_in_forward")
    arrived(["w_in"], gather_finish(gather_in, gather_in[-1], name="gather_w_in_finish"))
    WT["w_in"] = transposed(W["w_in"])
    early, mid, late = ["w_glu", "w_out"], ["w_up"], ["w_down", "w_ple_gate", "w_ple_proj"]
    gather_early = gather_start([shard[n] for n in early], W["w_in"], name="gather_early_start")
    gather_mid = gather_start([shard[n] for n in mid], gather_early[-1], name="gather_mid_start")
    gather_late = gather_start([shard[n] for n in late], gather_mid[-1], name="gather_late_start")

    (proj,) = mm_nn(hn1, W["w_in"], [F32], deps=(gather_late[-1],), name="proj_in")
    attn, lse = attn_fwd(proj, H, name="attn_fwd")
    gather_early = gather_forward(gather_early, attn, name="gather_early_forward")
    (mix_a,) = rowwise(lambda a, g: (_rms(a, g),), [attn], [ga], [(AW, BF16)], deps=(gather_early[-1],),
                       name="attn_norm")
    arrived(early, gather_finish(gather_early, mix_a, name="gather_early_finish"))

    a_r, a_i, bb_r, bb_i = _discretise(lam_re[0], lam_im[0], log_dt[0], ssm_b_re[0], ssm_b_im[0])
    ssm_consts = (_block_diag(bb_r.swapaxes(1, 2), nslab).astype(BF16), _block_diag(bb_i.swapaxes(1, 2), nslab).astype(BF16),
                  a_r.reshape(nslab, 1, SLAB_STATES), a_i.reshape(nslab, 1, SLAB_STATES),
                  _block_diag(ssm_c_re[0].swapaxes(1, 2), nslab).astype(BF16),
                  _block_diag(ssm_c_im[0].swapaxes(1, 2), nslab).astype(BF16), ssm_d)
    u_seg = _to_segments(proj[:, 3 * AW:]).astype(BF16)
    y_pre, yg, st_r, st_i = ssm_fwd(u_seg, *ssm_consts, name="ssm_fwd")
    gather_mid = gather_forward(gather_mid, y_pre, name="gather_mid_forward")
    (gl1,) = mm_nn(yg, W["w_glu"], [BF16], epi=lambda acc, b: (acc + b,), bias=b_glu, deps=(gather_mid[-1],),
                   name="glu_gate")
    (mix_s,) = rowwise(lambda yp, gl, g: (_rms(_gelu(yp) * _sigmoid(gl), g),), [y_pre, gl1], [gs], [(SW, BF16)],
                       name="ssm_glu_norm")
    mixed = [mix_a, _from_segments(mix_s)]
    (mo,) = mm_nn(mixed, W["w_out"], [BF16], name="mix_out")

    def resid_norm(h, t, gpost, gpre):
        hh = h + _rms(t, gpost)
        return hh, _rms(hh, gpre)

    h1, hn2 = rowwise(resid_norm, [xs, mo], [g2, g3], [(D, F32), (D, BF16)], name="resid_mix")
    arrived(mid, gather_finish(gather_mid, hn2, name="gather_mid_finish"))
    gather_late = gather_forward(gather_late, W["w_up"], name="gather_late_forward")
    WT["w_up"] = transposed(W["w_up"])

    def relu2(acc):
        r = jnp.maximum(acc, 0.0)
        return acc, r * r

    up, act = mm_nn(hn2, W["w_up"], [BF16, BF16], epi=relu2, deps=(gather_late[-1],), tm=1024, tn=1024, name="mlp_up")
    arrived(late, gather_finish(gather_late, act, name="gather_late_finish"))
    (ff,) = mm_nn(act, W["w_down"], [BF16], name="mlp_down")
    h2, hn3 = rowwise(resid_norm, [h1, ff], [g4, g5], [(D, F32), (D, BF16)], name="resid_mlp")
    (gl2,) = mm_nn(hn3, W["w_ple_gate"], [BF16], name="ple_gate")
    pb = p[0, 0].astype(BF16)
    (emb,) = mm_nn(pb, W["w_ple_proj"], [BF16], name="ple_proj")

    def head(h, gl, e, t, g):
        sg = _sigmoid(gl)
        ge = sg * e
        err = h + _rms(ge, g) - t
        dh = err * (1.0 / D)
        dge, dg = _rms_bwd(dh, ge, g)
        return dh, dge * e * sg * (1.0 - sg), dge * sg, jnp.sum(err * err, axis=0, keepdims=True), dg

    dh3, dgl2, demb, loss_part, dg6 = rowwise(head, [h2, gl2, emb, tgt], [g6], [(D, F32), (D, BF16), (D, BF16)],
                                             [D, D], name="ple_loss_head")
    loss = lax.psum(0.5 / D * jnp.sum(loss_part), ("x", "y", "c"))

    x_i, y_i, c_i = _place()
    place = jnp.stack([c_i, 2 * x_i + y_i]).astype(jnp.int32)
    grads, out_g, out_d, out_m, out_v = {}, {}, {}, {}, {}

    def to_sibling(names, after, tag):
        chunks = []
        for n in names:
            g = grads[n]
            g = g if n in col_sharded else g.reshape(N_DEV, g.shape[1] // N_DEV, g.shape[2])
            chunks.append(g.reshape(4, 2, g.shape[1], g.shape[2]))
        return chunks, exchange_start(chunks, [(4,) + g.shape[2:] for g in chunks], _core_copies, len(chunks), after,
                                      name=f"grads_to_sibling_{tag}")

    def to_chips(names, sent, after, tag):
        chunks, state = sent
        sums = []
        for n, g, r in zip(names, chunks, exchange_wait(state, after, name=f"grads_from_sibling_{tag}")):
            k, nn = g.shape[2], g.shape[3]
            kb = k // _tile(k, 512, 16)

            def mine(i, s, kb=kb):
                return 2 * (i // kb) + s[0], i % kb

            (s,) = _blocked(lambda a, b: (a.astype(F32) + b.astype(F32),),
                            [(g.reshape(N_DEV, k, nn), mine), r.reshape(4 * k, nn)],
                            [((4 * k, nn), BF16)], place=place, tr=k // kb, name=f"chip_sum_{n}")
            sums.append(s.reshape(4, k, nn))
        return sums, exchange_start(sums, [(3,) + s.shape[1:] for s in sums], _chip_copies, 3 * len(sums), sums[-1],
                                    name=f"grads_to_chips_{tag}")

    def update(w_, m_, v_, own, r0, r1, r2):
        g = own.astype(F32) + r0.astype(F32) + r1.astype(F32) + r2.astype(F32)
        return (g,) + _adamw(w_, g, m_, v_)

    def finish(names, sent, after, tag):
        sums, state = sent
        for n, s, r in zip(names, sums, exchange_wait(state, after, name=f"grads_from_chips_{tag}")):
            shp = weights[n].shape
            res = _blocked(update, [weights[n][0], mom_m[n][0], mom_v[n][0], (s, lambda i, p_: (p_[1], i)),
                                    (r, lambda i, p_: (0, i)), (r, lambda i, p_: (1, i)), (r, lambda i, p_: (2, i))],
                           [(shp[1:], F32)] * 4, place=place, tr=max(16, min(shp[1] // 8, 262144 // shp[2])),
                           name=f"adamw_{n}")
            out_g[n], out_d[n], out_m[n], out_v[n] = (t.reshape(shp) for t in res)
        return out_v[names[-1]]

    grads["w_ple_proj"] = mm_tn(pb, demb, N_DEV, name="grad_w_ple_proj")
    dhn3 = mm_nt(dgl2, W["w_ple_gate"], BF16, name="back_ple_gate")
    grads["w_ple_gate"] = mm_tn(hn3, dgl2, 1, name="grad_w_ple_gate")

    def back_resid(dh, dhn, h, t, gpre, gpost):
        d1, dgpre = _rms_bwd(dhn, h, gpre)
        dhh = dh + d1
        dt, dgpost = _rms_bwd(dhh, t, gpost)
        return dhh, dt, dgpre, dgpost

    dh2, dff, dg5, dg4 = rowwise(back_resid, [dh3, dhn3, h2, ff], [g5, g4], [(D, F32), (D, BF16)], [D, D],
                                 name="back_resid_mlp")
    dup = mm_nt(dff, W["w_down"], BF16, epi=lambda acc, u_: (acc * 2.0 * jnp.maximum(u_.astype(F32), 0.0),),
                extra=up, name="back_mlp_down")
    grads["w_down"] = mm_tn(act, dff, 1, name="grad_w_down")
    group_a = ["w_ple_proj", "w_ple_gate", "w_down"]
    sent_a = to_sibling(group_a, grads["w_down"], "a")
    (dhn2,) = mm_nn(dup, WT["w_up"], [BF16], deps=(sent_a[1][-1],), name="back_mlp_up")
    sent_a = to_chips(group_a, sent_a, dhn2, "a")
    grads["w_up"] = mm_tn(hn2, dup, N_DEV, deps=(sent_a[1][-1],), name="grad_w_up")
    dh1, dmo, dg3, dg2 = rowwise(back_resid, [dh2, dhn2, h1, mo], [g3, g2], [(D, F32), (D, BF16)], [D, D],
                                 name="back_resid_mix")
    dmixed = mm_nt(dmo, W["w_out"], BF16, name="back_mix_out")
    grads["w_out"] = mm_tn(mixed, dmo, 1, name="grad_w_out")

    def back_glu(dm, yp, gl, g):
        ygf = _gelu(yp)
        sg = _sigmoid(gl)
        dssm, dg = _rms_bwd(dm, ygf * sg, g)
        dgl = dssm * ygf * sg * (1.0 - sg)
        return dgl, dssm * sg, dg, jnp.sum(dgl, axis=0, keepdims=True)

    dgl1, dyg_direct, dgs, db_glu = rowwise(back_glu, [_to_segments(dmixed[:, AW:]), y_pre, gl1], [gs],
                                            [(SW, BF16), (SW, F32)], [SW, SW], name="back_glu")
    dyg_gate = mm_nt(dgl1, W["w_glu"], BF16, name="back_glu_gate")
    grads["w_glu"] = mm_tn(yg, dgl1, 1, name="grad_w_glu")
    group_b = ["w_up", "w_out", "w_glu"]
    sent_b = to_sibling(group_b, grads["w_glu"], "b")
    done_a = finish(group_a, sent_a, sent_b[1][-1], "a")

    du_seg, dbb_r, dbb_i, dcb_r, dcb_i, da_r, da_i, d_skip = ssm_bwd(
        u_seg, dyg_direct, dyg_gate, y_pre, st_r, st_i, *ssm_consts, deps=(done_a,), name="ssm_bwd")
    sent_b = to_chips(group_b, sent_b, du_seg, "b")

    def back_attn_norm(dm, a, g):
        da, dg = _rms_bwd(dm, a, g)
        prod = da * a
        delta = jnp.concatenate(
            [jnp.broadcast_to(jnp.sum(prod[:, h * HEAD_DIM:(h + 1) * HEAD_DIM], axis=-1, keepdims=True),
                              (prod.shape[0], HEAD_DIM)) for h in range(H)], axis=1)
        return da, delta, dg

    dattn, delta, dga = rowwise(back_attn_norm, [(dmixed, AW, 0), attn], [ga], [(AW, F32), (AW, F32)], [AW],
                                deps=(sent_b[1][-1],), name="back_attn_norm")
    dq, dk, dv = attn_bwd(proj, dattn, lse, delta, H, name="attn_bwd")
    dproj = [dq, dk, dv, _from_segments(du_seg)]
    (dhn1,) = mm_nn(dproj, WT["w_in"], [BF16], name="back_proj_in")

    def back_in(dh, dhn, a, g):
        d1, dg = _rms_bwd(dhn, a, g)
        return dh + d1, dg

    grad_x, dg1 = rowwise(back_in, [dh1, dhn1, xs], [g1], [(D, F32)], [D], name="back_norm_in")

    cot = dict(
        mix_norm_pre=dg1, mix_norm_post=dg2, mlp_norm_pre=dg3, mlp_norm_post=dg4, ple_norm_pre=dg5, ple_norm_post=dg6,
        attn_out_norm=dga, ssm_out_norm=dgs, b_glu=db_glu, ssm_d=d_skip,
        ssm_c_re=_block_diag_part(dcb_r, P_, C_).swapaxes(1, 2), ssm_c_im=_block_diag_part(dcb_i, P_, C_).swapaxes(1, 2),
        a_r=da_r.reshape(G, P_), a_i=da_i.reshape(G, P_),
        bb_r=_block_diag_part(dbb_r, C_, P_).swapaxes(1, 2), bb_i=_block_diag_part(dbb_i, C_, P_).swapaxes(1, 2))
    names = list(cot)
    flat = jnp.concatenate([cot[n].reshape(-1) for n in names])
    total = flat.shape[0]
    rows_ = -(-total // (LANES * 16)) * 16
    flat = jnp.pad(flat, (0, rows_ * LANES - total)).reshape(rows_, LANES)
    gather_small = gather_start([flat], flat, name="gather_small_start")
    grads["w_in"] = mm_tn(hn1, dproj, N_DEV, deps=(gather_small[-1],), tko=2048, name="grad_w_in")
    group_c = ["w_in"]
    sent_c = to_sibling(group_c, grads["w_in"], "c")
    done_b = finish(group_b, sent_b, sent_c[1][-1], "b")
    sent_c = to_chips(group_c, sent_c, done_b, "c")
    gather_small = gather_forward(gather_small, sent_c[1][-1], name="gather_small_forward")
    (every,) = gather_finish(gather_small, gather_small[-1], name="gather_small_finish")
    (summed,) = _blocked(lambda *t: (functools.reduce(lambda a, b: a + b, t),),
                         [(every, functools.partial(lambda i, p_, j: (j, i), j=j)) for j in range(N_DEV)],
                         [((rows_, LANES), F32)], name="sum_small_grads")
    summed = summed.reshape(-1)
    red, off = {}, 0
    for n in names:
        sz = cot[n].size
        red[n] = summed[off:off + sz].reshape(cot[n].shape)
        off += sz
    _, pull = jax.vjp(_discretise, lam_re[0], lam_im[0], log_dt[0], ssm_b_re[0], ssm_b_im[0])
    d_lre, d_lim, d_ldt, d_bre, d_bim = pull((red["a_r"], red["a_i"], red["bb_r"], red["bb_i"]))
    red.update(lam_re=d_lre, lam_im=d_lim, log_dt=d_ldt, ssm_b_re=d_bre, ssm_b_im=d_bim)

    def pack(d):
        t = jnp.concatenate([d[n].reshape(-1) for n in small])
        r_ = -(-t.shape[0] // (LANES * 16)) * 16
        return jnp.pad(t, (0, r_ * LANES - t.shape[0])).reshape(r_, LANES)

    sw, sg_, sm, sv = pack(weights), pack(red), pack(mom_m), pack(mom_v)
    sd, snm, snv = _blocked(lambda w_, g_, m_, v_: _adamw(w_, g_, m_, v_), [sw, sg_, sm, sv],
                            [(sw.shape, F32)] * 3, name="adamw_small")
    finish(group_c, sent_c, snv, "c")
    off = 0
    for n in small:
        sz = weights[n].size
        shp = weights[n].shape
        out_g[n] = red[n].reshape(shp)
        out_d[n] = sd.reshape(-1)[off:off + sz].reshape(shp)
        out_m[n] = snm.reshape(-1)[off:off + sz].reshape(shp)
        out_v[n] = snv.reshape(-1)[off:off + sz].reshape(shp)
        off += sz

    return (loss, grad_x[None], *[out_g[n] for n in order], *[out_d[n] for n in order],
            *[out_m[n] for n in order], *[out_v[n] for n in order])
```

```python
import functools
import math

import jax
import jax.numpy as jnp
from jax import lax
from jax.experimental import pallas as pl
from jax.experimental.pallas import tpu as pltpu

F32 = jnp.float32
BF16 = jnp.bfloat16
MESH = pl.DeviceIdType.MESH

N_DEV = 8
LANES = 128
SUBLANES = 8
VMEM_LIMIT = 48 * 1024 * 1024
VMEM_LIMIT_SCAN = 60 * 1024 * 1024

HEAD_DIM = 128
BLK = 128
DILATIONS = (1, 4, 16)
SSM_GROUP = 16
SSM_STATE = 64
SLAB_GROUPS = LANES // SSM_GROUP
SLAB_STATES = SLAB_GROUPS * SSM_STATE
SEGMENTS = SUBLANES
SCAN_UNROLL = 4
RMS_EPS = 1e-6
NEG_INF = -1e30

ADAM_LR = 0.001
ADAM_B1 = 0.9
ADAM_B2 = 0.999
ADAM_EPS = 1e-08
ADAM_WD = 0.01
ADAM_STEP = 10


def _tile(n, pref, unit=LANES):
    if n <= pref:
        return n
    t = (pref // unit) * unit
    while t > unit and n % t:
        t -= unit
    assert n % t == 0, (n, pref, unit)
    return t


def _params(sem=None, vmem=VMEM_LIMIT):
    return pltpu.CompilerParams(dimension_semantics=sem, vmem_limit_bytes=vmem)


_NN = (((1,), (0,)), ((), ()))
_NT = (((1,), (1,)), ((), ()))
_TN = (((0,), (0,)), ((), ()))


_ANY = pl.BlockSpec(memory_space=pl.ANY)


def _mm_call(dims, nk, na, nb, pick, n_extra, n_dep, n_out, epi, group=1, **kw):
    first_extra = na + nb
    first_out = first_extra + n_extra + n_dep
    kw["in_specs"] = list(kw["in_specs"]) + [_ANY] * n_dep

    def grouped(refs, step):
        if group == 1:
            return lax.dot_general(refs[0][...], refs[1][...], dims, preferred_element_type=F32)
        kp = refs[0].shape[1]
        return sum(lax.dot_general(refs[step * group + p][...], refs[na][pl.ds(p * kp, kp), :], dims,
                                   preferred_element_type=F32) for p in range(group))

    def single(*refs):
        extra = refs[first_extra:first_extra + n_extra]
        res = epi(grouped(refs, 0), *[e[...] for e in extra])
        for o, r in zip(refs[first_out:first_out + n_out], res):
            o[...] = r.astype(o.dtype)

    if nk == 1:
        assert na == group and nb == 1
        kw["scratch_shapes"] = []
        return pl.pallas_call(single, **kw)

    def body(*refs):
        extra = refs[first_extra:first_extra + n_extra]
        outs = refs[first_out:first_out + n_out]
        acc = refs[-1]
        k = pl.program_id(2)

        @pl.when(k == 0)
        def _():
            acc[...] = jnp.zeros_like(acc)

        def add(a_ref, b_ref):
            acc[...] += lax.dot_general(a_ref[...], b_ref[...], dims, preferred_element_type=F32)

        if na == nb == 1:
            add(refs[0], refs[1])
        elif group > 1:
            for step in range(nk):
                @pl.when(k == step)
                def _(step=step):
                    acc[...] += grouped(refs, step)
        else:
            pa, pb = pick(pl.program_id(0), pl.program_id(1), k)
            for x in range(na):
                for y in range(nb):
                    pl.when((pa == x) & (pb == y))(functools.partial(add, refs[x], refs[na + y]))

        @pl.when(k == nk - 1)
        def _():
            res = epi(acc[...], *[e[...] for e in extra])
            for o, r in zip(outs, res):
                o[...] = r.astype(o.dtype)

    return pl.pallas_call(body, **kw)


def _identity_epi(acc):
    return (acc,)


def _parts(t):
    return list(t) if isinstance(t, (list, tuple)) else [t]


def _part_spec(block, part, which, index):
    def index_map(i, j, k):
        use = which(i, j, k) == part
        r, c = index(i, j, k)
        return jnp.where(use, r, 0), jnp.where(use, c, 0)
    return pl.BlockSpec(block, index_map)


def mm_nn(a, w, out_dtypes, *, name, epi=_identity_epi, bias=None, deps=(), tm=2048, tn=512, tk=2048):
    a = _parts(a)
    M, Kp = a[0].shape
    K = Kp * len(a)
    J, K2, n = w.shape
    assert K == K2
    tm, tn, tk = _tile(M, tm, 16), _tile(n, tn), _tile(K, tk)
    npj = n // tn
    nk = K // tk
    group = 1
    if len(a) == 1:
        in_specs = [pl.BlockSpec((tm, tk), lambda i, j, k: (i, k))]
    else:
        assert tk % Kp == 0
        group = tk // Kp
        in_specs = [pl.BlockSpec((tm, Kp), lambda i, j, k: (i, 0)) for _ in a]
    in_specs.append(pl.BlockSpec((None, tk, tn), lambda i, j, k: (j // npj, k, j % npj)))
    args = a + [w]
    if bias is not None:
        in_specs.append(pl.BlockSpec((1, tn), lambda i, j, k: (0, j)))
        args.append(bias)
    return _mm_call(
        _NN, nk, len(a), 1, lambda i, j, k: (k, 0), len(args) - len(a) - 1, len(deps), len(out_dtypes), epi, group,
        out_shape=[jax.ShapeDtypeStruct((M, J * n), d) for d in out_dtypes],
        grid=(M // tm, J * npj, nk), in_specs=in_specs,
        out_specs=[pl.BlockSpec((tm, tn), lambda i, j, k: (i, j)) for _ in out_dtypes],
        scratch_shapes=[pltpu.VMEM((tm, tn), F32)],
        compiler_params=_params(("parallel", "parallel", "arbitrary")), name=name)(*args, *deps)


def mm_nt(a, w, out_dtype, *, name, epi=_identity_epi, extra=None, tm=2048, tko=512, tnr=2048):
    M, N = a.shape
    J, K, n = w.shape
    assert N == J * n
    tm, tko, tnr = _tile(M, tm, 16), _tile(K, tko), _tile(n, tnr)
    npj = n // tnr
    nk = N // tnr
    in_specs = [pl.BlockSpec((tm, tnr), lambda i, j, k: (i, k)),
                pl.BlockSpec((None, tko, tnr), lambda i, j, k: (k // npj, j, k % npj))]
    args = [a, w]
    if extra is not None:
        in_specs.append(pl.BlockSpec((tm, tko), lambda i, j, k: (i, j)))
        args.append(extra)
    return _mm_call(
        _NT, nk, 1, 1, None, len(args) - 2, 0, 1, epi,
        out_shape=[jax.ShapeDtypeStruct((M, K), out_dtype)],
        grid=(M // tm, K // tko, nk), in_specs=in_specs,
        out_specs=[pl.BlockSpec((tm, tko), lambda i, j, k: (i, j))],
        scratch_shapes=[pltpu.VMEM((tm, tko), F32)],
        compiler_params=_params(("parallel", "parallel", "arbitrary")), name=name)(*args)[0]


def mm_tn(a, b, J, *, name, deps=(), tko=1024, tn=1024, ts=2048):
    a, b = _parts(a), _parts(b)
    S, Kp = a[0].shape
    S2, Np = b[0].shape
    K, N = Kp * len(a), Np * len(b)
    assert S == S2 and N % J == 0
    n = N // J
    tko, tn, ts = _tile(Kp, tko), _tile(math.gcd(n, Np), tn), _tile(S, ts)
    npj = n // tn
    nk = S // ts
    ta, tb = Kp // tko, Np // tn
    assert nk > 1 or len(a) == len(b) == 1
    return _mm_call(
        _TN, nk, len(a), len(b), lambda i, j, k: (i // ta, j // tb), 0, len(deps), 1, _identity_epi,
        out_shape=[jax.ShapeDtypeStruct((J, K, n), BF16)],
        grid=(K // tko, J * npj, nk),
        in_specs=([_part_spec((ts, tko), x, lambda i, j, k: i // ta, lambda i, j, k: (k, i % ta)) for x in range(len(a))]
                  + [_part_spec((ts, tn), y, lambda i, j, k: j // tb, lambda i, j, k: (k, j % tb)) for y in range(len(b))]),
        out_specs=[pl.BlockSpec((None, tko, tn), lambda i, j, k: (j // npj, i, j % npj))],
        scratch_shapes=[pltpu.VMEM((tko, tn), F32)],
        compiler_params=_params(("parallel", "parallel", "arbitrary")), name=name)(*a, *b, *deps)[0]


def rowwise(fn, rows, vecs, outs, accs=(), *, name, deps=(), ts=256):
    rows = [r if isinstance(r, tuple) else (r, r.shape[1], 0) for r in rows]
    S = rows[0][0].shape[0]
    ts = _tile(S, ts, 16)
    nr, nv, no, nd = len(rows), len(vecs), len(outs), len(deps)

    def body(*refs):
        r, v = refs[:nr], refs[nr:nr + nv]
        o, a = refs[nr + nv + nd:nr + nv + nd + no], refs[nr + nv + nd + no:]
        res = fn(*[t[...].astype(F32) for t in r], *[t[...] for t in v])
        for ref, val in zip(o, res[:no]):
            ref[...] = val.astype(ref.dtype)
        if a:
            @pl.when(pl.program_id(0) == 0)
            def _():
                for ref in a:
                    ref[...] = jnp.zeros_like(ref)

            for ref, val in zip(a, res[no:]):
                ref[...] += val

    in_specs = [pl.BlockSpec((ts, w), functools.partial(lambda i, cb: (i, cb), cb=cb)) for _, w, cb in rows]
    in_specs += [pl.BlockSpec(v.shape, lambda i: (0, 0)) for v in vecs] + [_ANY] * nd
    out_shape = [jax.ShapeDtypeStruct((S, w), d) for w, d in outs]
    out_shape += [jax.ShapeDtypeStruct((1, w), F32) for w in accs]
    out_specs = [pl.BlockSpec((ts, w), lambda i: (i, 0)) for w, _ in outs]
    out_specs += [pl.BlockSpec((1, w), lambda i: (0, 0)) for w in accs]
    return pl.pallas_call(body, out_shape=out_shape, grid=(S // ts,), in_specs=in_specs, out_specs=out_specs,
                          compiler_params=_params(("arbitrary",)), name=name)(*[r[0] for r in rows], *vecs, *deps)


def _rms(x, g):
    r = lax.rsqrt(jnp.mean(x * x, axis=-1, keepdims=True) + RMS_EPS)
    return x * r * g


def _rms_bwd(dy, x, g):
    r = lax.rsqrt(jnp.mean(x * x, axis=-1, keepdims=True) + RMS_EPS)
    xh = x * r
    dxh = dy * g
    dx = r * (dxh - xh * jnp.mean(dxh * xh, axis=-1, keepdims=True))
    return dx, jnp.sum(dy * xh, axis=0, keepdims=True)


def _sigmoid(x):
    return pl.reciprocal(1.0 + jnp.exp(-x), approx=True)


_GELU_C = math.sqrt(2.0 / math.pi)


def _gelu(x):
    return 0.5 * x * (1.0 + jnp.tanh(_GELU_C * (x + 0.044715 * x * x * x)))


def _gelu_grad(x):
    t = jnp.tanh(_GELU_C * (x + 0.044715 * x * x * x))
    return 0.5 * (1.0 + t) + 0.5 * x * (1.0 - t * t) * _GELU_C * (1.0 + 3.0 * 0.044715 * x * x)


ATTN_INTERLEAVE = 8
KEY_PAD = BLK * max(DILATIONS)


def _key_mask(n):
    ii = lax.broadcasted_iota(jnp.int32, (BLK, 2 * BLK), 0)
    jj = lax.broadcasted_iota(jnp.int32, (BLK, 2 * BLK), 1)
    return ((jj < BLK) & (jj >= ii) & (n > 0)) | ((jj >= BLK) & (jj - BLK <= ii))


def _units(d, nblk):
    nb = nblk // d
    if nb == 2:
        def unit(idx):
            ii = lax.broadcasted_iota(jnp.int32, (2 * BLK, 2 * BLK), 0)
            jj = lax.broadcasted_iota(jnp.int32, (2 * BLK, 2 * BLK), 1)
            return pl.ds(idx, 2 * BLK, stride=d), pl.ds(KEY_PAD + idx, 2 * BLK, stride=d), (jj <= ii) & (ii - jj <= BLK)
        return d, max(1, ATTN_INTERLEAVE // 4), unit

    def unit(idx):
        r, n = idx // nb, idx % nb
        cur = r + n * (BLK * d)
        keys = cur + (KEY_PAD - BLK * d)
        if d == 1:
            return pl.ds(pl.multiple_of(cur, BLK), BLK), pl.ds(pl.multiple_of(keys, BLK), 2 * BLK), _key_mask(n)
        return pl.ds(cur, BLK, stride=d), pl.ds(keys, 2 * BLK, stride=d), _key_mask(n)
    return nblk, ATTN_INTERLEAVE, unit


def _pad_keys(dst, src):
    dst[pl.ds(0, KEY_PAD), :] = jnp.zeros((KEY_PAD, dst.shape[1]), F32)

    def copy(c, carry):
        dst[pl.ds(pl.multiple_of(KEY_PAD + c * BLK, BLK), BLK), :] = src[pl.ds(pl.multiple_of(c * BLK, BLK), BLK), :]
        return carry

    lax.fori_loop(0, src.shape[0] // BLK, copy, 0)


def attn_fwd(proj, n_heads, *, name):
    S, WP = proj.shape
    assert S % (BLK * max(DILATIONS)) == 0
    nblk = S // BLK
    AW = n_heads * HEAD_DIM
    scale = 1.0 / math.sqrt(HEAD_DIM)

    def body(q_ref, k_ref, v_ref, o_ref, l_ref, acc, mrun, lrun, kp, vp):
        _pad_keys(kp, k_ref)
        _pad_keys(vp, v_ref)
        for first, d in zip((True, False, False), reversed(DILATIONS)):
            n_units, per_step, unit = _units(d, nblk)

            def step(it, carry, first=first, n_units=n_units, per_step=per_step, unit=unit):
                units = [unit(it + j * (n_units // per_step)) for j in range(per_step)]
                ss = [lax.dot_general(q_ref[cur, :].astype(BF16), kp[keys, :].astype(BF16), _NT,
                                      preferred_element_type=F32) * scale for cur, keys, _ in units]
                ss = [jnp.where(mask, s, NEG_INF) for s, (_, _, mask) in zip(ss, units)]
                ms = [jnp.max(s, axis=-1, keepdims=True) for s in ss]
                ps = [jnp.exp(s - m) for s, m in zip(ss, ms)]
                ls = [jnp.sum(p, axis=-1, keepdims=True) for p in ps]
                os_ = [jnp.dot(p.astype(BF16), vp[keys, :].astype(BF16), preferred_element_type=F32)
                       for p, (_, keys, _) in zip(ps, units)]
                for (cur, keys, mask), m, l, o in zip(units, ms, ls, os_):
                    m = jnp.broadcast_to(m, o.shape)
                    l = jnp.broadcast_to(l, o.shape)
                    if first:
                        acc[cur, :], mrun[cur, :], lrun[cur, :] = o, m, l
                    else:
                        m_old = mrun[cur, :]
                        m_new = jnp.maximum(m_old, m)
                        w_old, w_blk = jnp.exp(m_old - m_new), jnp.exp(m - m_new)
                        acc[cur, :] = w_old * acc[cur, :] + w_blk * o
                        lrun[cur, :] = w_old * lrun[cur, :] + w_blk * l
                        mrun[cur, :] = m_new
                return carry

            lax.fori_loop(0, n_units // per_step, step, 0)

        def finish(c, carry):
            r = pl.ds(pl.multiple_of(c * BLK, BLK), BLK)
            o_ref[r, :] = acc[r, :] / lrun[r, :]
            l_ref[r, :] = mrun[r, :] + jnp.log(lrun[r, :])
            return carry

        lax.fori_loop(0, nblk, finish, 0)

    def col(off):
        return pl.BlockSpec((S, HEAD_DIM), lambda h: (0, off + h))

    ospec = pl.BlockSpec((S, HEAD_DIM), lambda h: (0, h))
    return pl.pallas_call(
        body, out_shape=[jax.ShapeDtypeStruct((S, AW), F32)] * 2, grid=(n_heads,),
        in_specs=[col(0), col(n_heads), col(2 * n_heads)], out_specs=[ospec, ospec],
        scratch_shapes=[pltpu.VMEM((S, HEAD_DIM), F32)] * 3 + [pltpu.VMEM((KEY_PAD + S, HEAD_DIM), F32)] * 2,
        compiler_params=_params(("parallel",)), name=name)(proj, proj, proj)


def attn_bwd(proj, do, lse, delta, n_heads, *, name):
    S, WP = proj.shape
    nblk = S // BLK
    AW = n_heads * HEAD_DIM
    scale = 1.0 / math.sqrt(HEAD_DIM)

    def body(q_ref, k_ref, v_ref, do_ref, l_ref, dl_ref, dq_ref, dk_ref, dv_ref, dq_sc, dk_sc, dv_sc, kp, vp):
        _pad_keys(kp, k_ref)
        _pad_keys(vp, v_ref)
        order = list(reversed(DILATIONS))
        assign_first = nblk // order[0] == 2
        if assign_first:
            dk_sc[pl.ds(0, KEY_PAD), :] = jnp.zeros((KEY_PAD, HEAD_DIM), F32)
            dv_sc[pl.ds(0, KEY_PAD), :] = jnp.zeros((KEY_PAD, HEAD_DIM), F32)
        else:
            dq_sc[...] = jnp.zeros_like(dq_sc)
            dk_sc[...] = jnp.zeros_like(dk_sc)
            dv_sc[...] = jnp.zeros_like(dv_sc)
        for assign, d in zip((assign_first, False, False), order):
            n_units, per_step, unit = _units(d, nblk)

            def step(it, carry, n_units=n_units, per_step=per_step, unit=unit, assign=assign):
                units = [unit(it + j * (n_units // per_step)) for j in range(per_step)]
                qs = [q_ref[cur, :].astype(BF16) for cur, _, _ in units]
                gs = [do_ref[cur, :].astype(BF16) for cur, _, _ in units]
                ks = [kp[keys, :].astype(BF16) for _, keys, _ in units]
                ss = [lax.dot_general(q, kb, _NT, preferred_element_type=F32) * scale for q, kb in zip(qs, ks)]
                dps = [lax.dot_general(g, vp[keys, :].astype(BF16), _NT, preferred_element_type=F32)
                       for g, (_, keys, _) in zip(gs, units)]
                ps = [jnp.where(mask, jnp.exp(s - l_ref[cur, :][:, :1]), 0.0) for s, (cur, _, mask) in zip(ss, units)]
                dss = [(p * (dp - dl_ref[cur, :][:, :1]) * scale).astype(BF16)
                       for p, dp, (cur, _, _) in zip(ps, dps, units)]
                for (cur, keys, _), q, g, kb, p, ds in zip(units, qs, gs, ks, ps, dss):
                    dq = jnp.dot(ds, kb, preferred_element_type=F32)
                    dk = lax.dot_general(ds, q, _TN, preferred_element_type=F32)
                    dv = lax.dot_general(p.astype(BF16), g, _TN, preferred_element_type=F32)
                    if assign:
                        dq_sc[cur, :], dk_sc[keys, :], dv_sc[keys, :] = dq, dk, dv
                    else:
                        dq_sc[cur, :] += dq
                        dk_sc[keys, :] += dk
                        dv_sc[keys, :] += dv
                return carry

            lax.fori_loop(0, n_units // per_step, step, 0)
        rows = pl.ds(KEY_PAD, S)
        dq_ref[...] = dq_sc[...].astype(BF16)
        dk_ref[...] = dk_sc[rows, :].astype(BF16)
        dv_ref[...] = dv_sc[rows, :].astype(BF16)

    def col(off):
        return pl.BlockSpec((S, HEAD_DIM), lambda h: (0, off + h))

    ospec = pl.BlockSpec((S, HEAD_DIM), lambda h: (0, h))
    return pl.pallas_call(
        body, out_shape=[jax.ShapeDtypeStruct((S, AW), BF16)] * 3, grid=(n_heads,),
        in_specs=[col(0), col(n_heads), col(2 * n_heads), ospec, ospec, ospec], out_specs=[ospec] * 3,
        scratch_shapes=[pltpu.VMEM((S, HEAD_DIM), F32)] + [pltpu.VMEM((KEY_PAD + S, HEAD_DIM), F32)] * 4,
        compiler_params=_params(("parallel",), VMEM_LIMIT_SCAN), name=name)(proj, proj, proj, do, lse, delta)


def _to_segments(t):
    S, W = t.shape
    return t.reshape(SEGMENTS, S // SEGMENTS, W).swapaxes(0, 1).reshape(S, W)


def _from_segments(t):
    S, W = t.shape
    return t.reshape(S // SEGMENTS, SEGMENTS, W).swapaxes(0, 1).reshape(S, W)


def _cmul(ar, ai, br, bi):
    return ar * br - ai * bi, ar * bi + ai * br


def _power(ar, ai, log2n):
    for _ in range(log2n):
        ar, ai = _cmul(ar, ai, ar, ai)
    return ar, ai


def _shift_rows(x, up):
    row = lax.broadcasted_iota(jnp.int32, x.shape, 0)
    if up:
        return jnp.where(row == SEGMENTS - 1, 0.0, pltpu.roll(x, SEGMENTS - 1, 0))
    return jnp.where(row == 0, 0.0, pltpu.roll(x, 1, 0))


def _segment_carries(er, ei, pr, pi, up):
    cr = jnp.zeros_like(er)
    ci = jnp.zeros_like(ei)
    for _ in range(SEGMENTS - 1):
        tr, ti = _cmul(pr, pi, cr, ci)
        cr, ci = _shift_rows(er + tr, up), _shift_rows(ei + ti, up)
    return cr, ci


def _scan_states(sr, si, ar, ai, T, reverse):
    ns = sr.shape[1]
    ar8 = jnp.broadcast_to(ar, (SEGMENTS, ns))
    ai8 = jnp.broadcast_to(ai, (SEGMENTS, ns))

    def rows(t):
        k = (T - 1 - t) if reverse else t
        return pl.ds(pl.multiple_of(k * SEGMENTS, SEGMENTS), SEGMENTS)

    def advance(t, c):
        tr, ti = _cmul(ar8, ai8, c[0], c[1])
        return tr + sr[rows(t), :], ti + si[rows(t), :]

    def several(step):
        def trip(t, c):
            for j in range(SCAN_UNROLL):
                c = step(t * SCAN_UNROLL + j, c)
            return c
        return trip

    zero = jnp.zeros((SEGMENTS, ns), F32)
    er, ei = lax.fori_loop(0, T // SCAN_UNROLL, several(advance), (zero, zero))
    pr, pi = _power(ar, ai, T.bit_length() - 1)
    cr, ci = _segment_carries(er, ei, jnp.broadcast_to(pr, (SEGMENTS, ns)), jnp.broadcast_to(pi, (SEGMENTS, ns)), reverse)

    def store(t, c):
        nr, ni = advance(t, c)
        sr[rows(t), :] = nr
        si[rows(t), :] = ni
        return nr, ni

    lax.fori_loop(0, T // SCAN_UNROLL, several(store), (cr, ci))
    return cr, ci


def _slab_specs(ns):
    return [pl.BlockSpec((None, LANES, ns), lambda g: (g, 0, 0)),
            pl.BlockSpec((None, LANES, ns), lambda g: (g, 0, 0)),
            pl.BlockSpec((None, 1, ns), lambda g: (g, 0, 0)),
            pl.BlockSpec((None, 1, ns), lambda g: (g, 0, 0)),
            pl.BlockSpec((None, ns, LANES), lambda g: (g, 0, 0)),
            pl.BlockSpec((None, ns, LANES), lambda g: (g, 0, 0)),
            pl.BlockSpec((1, LANES), lambda g: (0, g))]


def _chunks(S):
    rc = _tile(S, 512, 16)
    return rc, S // rc


def ssm_fwd(u, bbr, bbi, ar, ai, cbr, cbi, dsk, *, name):
    S, SW = u.shape
    nslab, _, ns = bbr.shape
    T = S // SEGMENTS
    assert T & (T - 1) == 0
    rc, nc = _chunks(S)

    def body(u_ref, br_ref, bi_ref, ar_ref, ai_ref, cr_ref, ci_ref, d_ref, y_ref, yg_ref, str_ref, sti_ref, sr, si):
        def inputs(c, carry):
            r = pl.ds(pl.multiple_of(c * rc, rc), rc)
            sr[r, :] = jnp.dot(u_ref[r, :], br_ref[...], preferred_element_type=F32)
            si[r, :] = jnp.dot(u_ref[r, :], bi_ref[...], preferred_element_type=F32)
            return carry

        lax.fori_loop(0, nc, inputs, 0)
        _scan_states(sr, si, ar_ref[...], ai_ref[...], T, False)

        def outputs(c, carry):
            r = pl.ds(pl.multiple_of(c * rc, rc), rc)
            srb, sib = sr[r, :].astype(BF16), si[r, :].astype(BF16)
            str_ref[r, :] = srb
            sti_ref[r, :] = sib
            y = (jnp.dot(srb, cr_ref[...], preferred_element_type=F32)
                 - jnp.dot(sib, ci_ref[...], preferred_element_type=F32) + d_ref[...] * u_ref[r, :].astype(F32))
            y_ref[r, :] = y
            yg_ref[r, :] = _gelu(y).astype(BF16)
            return carry

        lax.fori_loop(0, nc, outputs, 0)

    slab = pl.BlockSpec((S, LANES), lambda g: (0, g))
    states = pl.BlockSpec((S, ns), lambda g: (0, g))
    return pl.pallas_call(
        body, out_shape=([jax.ShapeDtypeStruct((S, SW), F32), jax.ShapeDtypeStruct((S, SW), BF16)]
                         + [jax.ShapeDtypeStruct((S, nslab * ns), BF16)] * 2),
        grid=(nslab,), in_specs=[slab] + _slab_specs(ns), out_specs=[slab, slab, states, states],
        scratch_shapes=[pltpu.VMEM((S, ns), F32)] * 2,
        compiler_params=_params(("parallel",), VMEM_LIMIT_SCAN), name=name)(u, bbr, bbi, ar, ai, cbr, cbi, dsk)


def ssm_bwd(u, d_direct, d_gate, y, st_r, st_i, bbr, bbi, ar, ai, cbr, cbi, dsk, *, name, deps=()):
    S, SW = u.shape
    nslab, _, ns = bbr.shape
    T = S // SEGMENTS
    rc, nc = _chunks(S)
    pair_rows = 2 * SEGMENTS

    def body(*refs):
        (u_ref, d1_ref, d2_ref, y_ref, sr_ref, si_ref, br_ref, bi_ref, ar_ref, ai_ref, cr_ref, ci_ref,
         d_ref) = refs[:13]
        (du_ref, dbr_ref, dbi_ref, dcr_ref, dci_ref, dar_ref, dai_ref, dd_ref, lr, li,
         dy_ref) = refs[13 + len(deps):]

        def inputs(c, skip):
            r = pl.ds(pl.multiple_of(c * rc, rc), rc)
            dy = (d1_ref[r, :] + d2_ref[r, :].astype(F32)) * _gelu_grad(y_ref[r, :])
            dy_ref[r, :] = dy
            gb = dy.astype(BF16)
            lr[r, :] = lax.dot_general(gb, cr_ref[...], _NT, preferred_element_type=F32)
            li[r, :] = -lax.dot_general(gb, ci_ref[...], _NT, preferred_element_type=F32)
            return skip + jnp.sum(dy * u_ref[r, :].astype(F32), axis=0, keepdims=True)

        dd_ref[...] = lax.fori_loop(0, nc, inputs, jnp.zeros((1, LANES), F32))
        _scan_states(lr, li, ar_ref[...], -ai_ref[...], T, True)

        def steps(j):
            rows = pl.ds(pl.multiple_of(j * pair_rows, pair_rows), pair_rows)
            tr, ti = sr_ref[rows, :].astype(F32), si_ref[rows, :].astype(F32)
            return tr[:SEGMENTS], tr[SEGMENTS:], ti[:SEGMENTS], ti[SEGMENTS:]

        def pair(j, c):
            acc_r, acc_i, pr, pi = c
            lo_r, hi_r, lo_i, hi_i = steps(j)
            first = pl.ds(pl.multiple_of(j * pair_rows, SEGMENTS), SEGMENTS)
            second = pl.ds(pl.multiple_of(j * pair_rows + SEGMENTS, SEGMENTS), SEGMENTS)
            la_r, la_i, lb_r, lb_i = lr[first, :], li[first, :], lr[second, :], li[second, :]
            return (acc_r + la_r * pr + la_i * pi + lb_r * lo_r + lb_i * lo_i,
                    acc_i - la_r * pi + la_i * pr - lb_r * lo_i + lb_i * lo_r, hi_r, hi_i)

        def pairs(t, c):
            return pair(2 * t + 1, pair(2 * t, c))

        _, end_r, _, end_i = steps(T // 2 - 1)
        zero = jnp.zeros((SEGMENTS, ns), F32)
        acc = lax.fori_loop(0, T // 4, pairs, (zero, zero, _shift_rows(end_r, False), _shift_rows(end_i, False)))
        dar_ref[...] = jnp.sum(acc[0], axis=0, keepdims=True)
        dai_ref[...] = jnp.sum(acc[1], axis=0, keepdims=True)

        dbr_ref[...] = jnp.zeros_like(dbr_ref)
        dbi_ref[...] = jnp.zeros_like(dbi_ref)
        dcr_ref[...] = jnp.zeros_like(dcr_ref)
        dci_ref[...] = jnp.zeros_like(dci_ref)

        def outputs(c, carry):
            r = pl.ds(pl.multiple_of(c * rc, rc), rc)
            ub = u_ref[r, :]
            g = dy_ref[r, :]
            gb = g.astype(BF16)
            lrb = lr[r, :].astype(BF16)
            lib = li[r, :].astype(BF16)
            du_ref[r, :] = (lax.dot_general(lrb, br_ref[...], _NT, preferred_element_type=F32)
                            + lax.dot_general(lib, bi_ref[...], _NT, preferred_element_type=F32)
                            + d_ref[...] * g).astype(BF16)
            dbr_ref[...] += lax.dot_general(ub, lrb, _TN, preferred_element_type=F32)
            dbi_ref[...] += lax.dot_general(ub, lib, _TN, preferred_element_type=F32)
            dcr_ref[...] += lax.dot_general(sr_ref[r, :], gb, _TN, preferred_element_type=F32)
            dci_ref[...] -= lax.dot_general(si_ref[r, :], gb, _TN, preferred_element_type=F32)
            return carry

        lax.fori_loop(0, nc, outputs, 0)

    slab = pl.BlockSpec((S, LANES), lambda g: (0, g))
    states = pl.BlockSpec((S, ns), lambda g: (0, g))
    bspec = pl.BlockSpec((None, LANES, ns), lambda g: (g, 0, 0))
    cspec = pl.BlockSpec((None, ns, LANES), lambda g: (g, 0, 0))
    aspec = pl.BlockSpec((None, 1, ns), lambda g: (g, 0, 0))
    return pl.pallas_call(
        body,
        out_shape=[jax.ShapeDtypeStruct((S, SW), BF16),
                   jax.ShapeDtypeStruct((nslab, LANES, ns), F32), jax.ShapeDtypeStruct((nslab, LANES, ns), F32),
                   jax.ShapeDtypeStruct((nslab, ns, LANES), F32), jax.ShapeDtypeStruct((nslab, ns, LANES), F32),
                   jax.ShapeDtypeStruct((nslab, 1, ns), F32), jax.ShapeDtypeStruct((nslab, 1, ns), F32),
                   jax.ShapeDtypeStruct((1, SW), F32)],
        grid=(nslab,), in_specs=[slab, slab, slab, slab, states, states] + _slab_specs(ns) + [_ANY] * len(deps),
        out_specs=[slab, bspec, bspec, cspec, cspec, aspec, aspec, pl.BlockSpec((1, LANES), lambda g: (0, g))],
        scratch_shapes=[pltpu.VMEM((S, ns), F32)] * 2 + [pltpu.VMEM((S, LANES), F32)],
        compiler_params=_params(("parallel",), VMEM_LIMIT_SCAN), name=name)(
            u, d_direct, d_gate, y, st_r, st_i, bbr, bbi, ar, ai, cbr, cbi, dsk, *deps)


def _discretise(lam_re, lam_im, log_dt, b_re, b_im):
    dt = jnp.exp(log_dt)[:, None]
    mag = jnp.exp(lam_re * dt)
    ar = mag * jnp.cos(lam_im * dt)
    ai = mag * jnp.sin(lam_im * dt)
    nr, ni = ar - 1.0, ai
    den = lam_re * lam_re + lam_im * lam_im
    cr = ((nr * lam_re + ni * lam_im) / den)[..., None]
    ci = ((ni * lam_re - nr * lam_im) / den)[..., None]
    return ar, ai, cr * b_re - ci * b_im, cr * b_im + ci * b_re


def _block_diag(t, nslab):
    G, R, C = t.shape
    eye = jnp.eye(SLAB_GROUPS, dtype=t.dtype)
    t = t.reshape(nslab, SLAB_GROUPS, R, C)
    return jnp.einsum('sgrc,gh->sgrhc', t, eye).reshape(nslab, SLAB_GROUPS * R, SLAB_GROUPS * C)


def _block_diag_part(t, R, C):
    nslab = t.shape[0]
    eye = jnp.eye(SLAB_GROUPS, dtype=t.dtype)
    t = t.reshape(nslab, SLAB_GROUPS, R, SLAB_GROUPS, C)
    return jnp.einsum('sgrhc,gh->sgrc', t, eye).reshape(nslab * SLAB_GROUPS, R, C)


def _place():
    return lax.axis_index("x"), lax.axis_index("y"), lax.axis_index("c")


_HBM = pl.BlockSpec(memory_space=pltpu.HBM)
_SEM = pl.BlockSpec(memory_space=pltpu.SEMAPHORE)
_ORDERED_EFFECT = pltpu.SideEffectType.DATAFLOW_SIDE_EFFECTING


def _split_call(name, srcs, zones, sems_in, n_new, body_fn, after):
    nsrc, nz, ns, nn = len(srcs), len(zones), len(sems_in), len(n_new)
    nb = nsrc + nz

    def body(*refs):
        outs = refs[nb + ns + 1:]
        body_fn(refs[:nb], refs[nb:nb + ns], outs[:nn])
        outs[nn + nz][...] = jnp.zeros((SUBLANES, LANES), F32)

    res = pl.pallas_call(
        body, name=name,
        out_shape=([pltpu.SemaphoreType.DMA((n,)) for n in n_new] + [pltpu.HBM(b.shape, b.dtype) for b in zones]
                   + [jax.ShapeDtypeStruct((SUBLANES, LANES), F32)]),
        in_specs=[_HBM] * nb + [_SEM] * ns + [_ANY],
        out_specs=[_SEM] * nn + [_HBM] * nz + [pl.BlockSpec(memory_space=pltpu.VMEM)],
        input_output_aliases={nsrc + i: nn + i for i in range(nz)},
        compiler_params=pltpu.CompilerParams(has_side_effects=_ORDERED_EFFECT))(
            *[pltpu.with_memory_space_constraint(b, pltpu.HBM) for b in list(srcs) + list(zones)], *sems_in, after)
    return list(res[:nn]), list(res[nn:nn + nz]), res[-1]


def _mesh_peers():
    x, y, c = _place()
    return x, y, c, (x, y, 1 - c), [(1 - x, y), (x, 1 - y), (1 - x, 1 - y)]


def gather_start(shards, after, *, name):
    nw = len(shards)
    x, y, c = _place()
    zones = [lax.dynamic_update_slice(lax.empty((N_DEV,) + s.shape, s.dtype), s[None], (4 * x + 2 * y + c, 0, 0))
             for s in shards]

    def body(bufs, taken, new):
        for cp in _gather_first(bufs, nw, new[0], new[1]):
            cp.start()

    sems, zones, token = _split_call(name, shards, zones, [], [4 * nw, 4 * nw], body, after)
    return shards, sems, zones, token


def _gather_first(bufs, nw, send, recv):
    x, y, c, sibling, chips = _mesh_peers()
    out = []
    for w in range(nw):
        slot = bufs[nw + w].at[4 * x + 2 * y + c]
        for k, to in enumerate([sibling] + [(*ch, c) for ch in chips]):
            out.append(pltpu.make_async_remote_copy(
                src_ref=bufs[w], dst_ref=slot, send_sem=send.at[4 * w + k], recv_sem=recv.at[4 * w + k],
                device_id=to, device_id_type=MESH))
    return out


def _gather_slot_copy(bufs, nw, w, block, send_sem, recv_sem, to):
    px, py, pc = block
    slot = bufs[nw + w].at[4 * px + 2 * py + pc]
    return pltpu.make_async_remote_copy(src_ref=slot, dst_ref=slot, send_sem=send_sem, recv_sem=recv_sem,
                                        device_id=to, device_id_type=MESH)


def gather_forward(state, after, *, name):
    shards, sems, zones, _ = state
    nw = len(shards)

    def body(bufs, taken, new):
        x, y, c, sibling, chips = _mesh_peers()
        for j, ch in enumerate(chips):
            for w in range(nw):
                k = 4 * w + 1 + j
                _gather_slot_copy(bufs, nw, w, (*ch, c), taken[0].at[k], taken[1].at[k], (*ch, c)).wait_recv()
                _gather_slot_copy(bufs, nw, w, (*ch, c), new[0].at[3 * w + j], new[1].at[3 * w + j], sibling).start()
        for w in range(nw):
            _gather_slot_copy(bufs, nw, w, sibling, taken[0].at[4 * w], taken[1].at[4 * w], sibling).wait_recv()
        for cp in _gather_first(bufs, nw, taken[0], taken[1]):
            cp.wait_send()

    sems, zones, token = _split_call(name, shards, zones, sems, [3 * nw, 3 * nw], body, after)
    return shards, sems, zones, token


def gather_finish(state, after, *, name):
    shards, sems, zones, _ = state
    nw = len(shards)

    def body(bufs, taken, new):
        x, y, c, sibling, chips = _mesh_peers()
        for w in range(nw):
            for j, ch in enumerate(chips):
                cp = _gather_slot_copy(bufs, nw, w, (*ch, 1 - c), taken[0].at[3 * w + j], taken[1].at[3 * w + j], sibling)
                cp.wait_send()
                cp.wait_recv()

    _, zones, _ = _split_call(name, shards, zones, sems, [], body, after)
    return zones


def exchange_start(srcs, zone_shapes, copies, n, after, *, name):
    nw = len(srcs)
    zones = [lax.empty(z, s.dtype) for z, s in zip(zone_shapes, srcs)]

    def body(bufs, taken, new):
        for cp in copies(bufs[:nw], bufs[nw:], new[0], new[1]):
            cp.start()

    sems, zones, token = _split_call(name, srcs, zones, [], [n, n], body, after)
    return srcs, copies, sems, zones, token


def exchange_wait(state, after, *, name):
    srcs, copies, sems, zones, _ = state
    nw = len(srcs)

    def body(bufs, taken, new):
        for cp in copies(bufs[:nw], bufs[nw:], taken[0], taken[1]):
            cp.wait_send()
            cp.wait_recv()

    _, zones, _ = _split_call(name, srcs, zones, sems, [], body, after)
    return zones


def _core_copies(srcs, zones, send, recv):
    x, y, c = _place()
    return [pltpu.make_async_remote_copy(
        src_ref=srcs[w].at[:, 1 - c], dst_ref=zones[w], send_sem=send.at[w], recv_sem=recv.at[w],
        device_id=(x, y, 1 - c), device_id_type=MESH) for w in range(len(srcs))]


def _chip_copies(srcs, zones, send, recv):
    x, y, c = _place()
    chips = [(1 - x, y), (x, 1 - y), (1 - x, 1 - y)]
    return [pltpu.make_async_remote_copy(
        src_ref=srcs[w].at[2 * cx + cy], dst_ref=zones[w].at[j], send_sem=send.at[3 * w + j],
        recv_sem=recv.at[3 * w + j], device_id=(cx, cy, c), device_id_type=MESH)
        for w in range(len(srcs)) for j, (cx, cy) in enumerate(chips)]


def _blocked(fn, ins, outs, *, name, place=None, tr=256):
    k, n = outs[0][0]
    tr = _tile(k, tr, 16)
    if place is None:
        place = jnp.zeros((1,), jnp.int32)
    specs = []
    args = []
    for a in ins:
        if isinstance(a, tuple):
            arr, lead = a
            specs.append(pl.BlockSpec((None, tr, n), functools.partial(lambda i, s, lead: (*lead(i, s), 0), lead=lead)))
            args.append(arr)
        else:
            specs.append(pl.BlockSpec((tr, n), lambda i, s: (i, 0)))
            args.append(a)
    nin = len(args)

    def body(place_ref, *refs):
        res = fn(*[r[...] for r in refs[:nin]])
        for ref, val in zip(refs[nin:], res):
            ref[...] = val.astype(ref.dtype)

    return pl.pallas_call(
        body, out_shape=[jax.ShapeDtypeStruct(s, d) for s, d in outs],
        grid_spec=pltpu.PrefetchScalarGridSpec(
            num_scalar_prefetch=1, grid=(k // tr,), in_specs=specs,
            out_specs=[pl.BlockSpec((tr, n), lambda i, s: (i, 0)) for _ in outs]),
        compiler_params=_params(("parallel",)), name=name)(place, *args)


def _adamw(w, g, m, v):
    m = ADAM_B1 * m + (1.0 - ADAM_B1) * g
    v = ADAM_B2 * v + (1.0 - ADAM_B2) * (g * g)
    m_hat = m / (1.0 - ADAM_B1 ** ADAM_STEP)
    v_hat = v / (1.0 - ADAM_B2 ** ADAM_STEP)
    delta = -ADAM_LR * (m_hat * pl.reciprocal(jnp.sqrt(v_hat) + ADAM_EPS, approx=True) + ADAM_WD * w)
    return delta, m, v


def kernel(x, p, mix_norm_pre, w_in, lam_re, lam_im, log_dt, ssm_b_re, ssm_b_im, ssm_c_re, ssm_c_im, ssm_d, w_glu, b_glu, attn_out_norm, ssm_out_norm, w_out, mix_norm_post, mlp_norm_pre, w_up, w_down, mlp_norm_post, ple_norm_pre, w_ple_gate, w_ple_proj, ple_norm_post, loss_target, m_mix_norm_pre, m_w_in, m_lam_re, m_lam_im, m_log_dt, m_ssm_b_re, m_ssm_b_im, m_ssm_c_re, m_ssm_c_im, m_ssm_d, m_w_glu, m_b_glu, m_attn_out_norm, m_ssm_out_norm, m_w_out, m_mix_norm_post, m_mlp_norm_pre, m_w_up, m_w_down, m_mlp_norm_post, m_ple_norm_pre, m_w_ple_gate, m_w_ple_proj, m_ple_norm_post, v_mix_norm_pre, v_w_in, v_lam_re, v_lam_im, v_log_dt, v_ssm_b_re, v_ssm_b_im, v_ssm_c_re, v_ssm_c_im, v_ssm_d, v_w_glu, v_b_glu, v_attn_out_norm, v_ssm_out_norm, v_w_out, v_mix_norm_post, v_mlp_norm_pre, v_w_up, v_w_down, v_mlp_norm_post, v_ple_norm_pre, v_w_ple_gate, v_w_ple_proj, v_ple_norm_post):
    weights = dict(mix_norm_pre=mix_norm_pre, w_in=w_in, lam_re=lam_re, lam_im=lam_im, log_dt=log_dt, ssm_b_re=ssm_b_re, ssm_b_im=ssm_b_im, ssm_c_re=ssm_c_re, ssm_c_im=ssm_c_im, ssm_d=ssm_d, w_glu=w_glu, b_glu=b_glu, attn_out_norm=attn_out_norm, ssm_out_norm=ssm_out_norm, w_out=w_out, mix_norm_post=mix_norm_post, mlp_norm_pre=mlp_norm_pre, w_up=w_up, w_down=w_down, mlp_norm_post=mlp_norm_post, ple_norm_pre=ple_norm_pre, w_ple_gate=w_ple_gate, w_ple_proj=w_ple_proj, ple_norm_post=ple_norm_post)
    mom_m = dict(mix_norm_pre=m_mix_norm_pre, w_in=m_w_in, lam_re=m_lam_re, lam_im=m_lam_im, log_dt=m_log_dt, ssm_b_re=m_ssm_b_re, ssm_b_im=m_ssm_b_im, ssm_c_re=m_ssm_c_re, ssm_c_im=m_ssm_c_im, ssm_d=m_ssm_d, w_glu=m_w_glu, b_glu=m_b_glu, attn_out_norm=m_attn_out_norm, ssm_out_norm=m_ssm_out_norm, w_out=m_w_out, mix_norm_post=m_mix_norm_post, mlp_norm_pre=m_mlp_norm_pre, w_up=m_w_up, w_down=m_w_down, mlp_norm_post=m_mlp_norm_post, ple_norm_pre=m_ple_norm_pre, w_ple_gate=m_w_ple_gate, w_ple_proj=m_w_ple_proj, ple_norm_post=m_ple_norm_post)
    mom_v = dict(mix_norm_pre=v_mix_norm_pre, w_in=v_w_in, lam_re=v_lam_re, lam_im=v_lam_im, log_dt=v_log_dt, ssm_b_re=v_ssm_b_re, ssm_b_im=v_ssm_b_im, ssm_c_re=v_ssm_c_re, ssm_c_im=v_ssm_c_im, ssm_d=v_ssm_d, w_glu=v_w_glu, b_glu=v_b_glu, attn_out_norm=v_attn_out_norm, ssm_out_norm=v_ssm_out_norm, w_out=v_w_out, mix_norm_post=v_mix_norm_post, mlp_norm_pre=v_mlp_norm_pre, w_up=v_w_up, w_down=v_w_down, mlp_norm_post=v_mlp_norm_post, ple_norm_pre=v_ple_norm_pre, w_ple_gate=v_w_ple_gate, w_ple_proj=v_w_ple_proj, ple_norm_post=v_ple_norm_post)
    order = list(weights)
    big = ["w_in", "w_glu", "w_out", "w_up", "w_down", "w_ple_gate", "w_ple_proj"]
    col_sharded = {"w_in", "w_up", "w_ple_proj"}
    small = [n for n in order if n not in big]

    _, S, D = x.shape
    xs = x[0]
    tgt = loss_target[0]
    AW = attn_out_norm.shape[1]
    SW = ssm_d.shape[1]
    H = AW // HEAD_DIM
    G = SW // SSM_GROUP
    nslab = G // SLAB_GROUPS
    P_, C_ = SSM_STATE, SSM_GROUP

    shard = {n: weights[n][0].astype(BF16) for n in big}
    W, WT = {}, {}

    def arrived(names, gathered):
        for n, g in zip(names, gathered):
            W[n] = g if n in col_sharded else g.reshape(1, N_DEV * g.shape[1], g.shape[2])

    def transposed(g):
        return jnp.swapaxes(g, 1, 2).reshape(1, g.shape[0] * g.shape[2], g.shape[1])

    g1, g2, g3, g4, g5, g6 = (weights[n] for n in ("mix_norm_pre", "mix_norm_post", "mlp_norm_pre",
                                                      "mlp_norm_post", "ple_norm_pre", "ple_norm_post"))
    ga, gs = attn_out_norm, ssm_out_norm
    gather_in = gather_start([shard["w_in"]], shard["w_in"], name="gather_w_in_start")
    (hn1,) = rowwise(lambda a, g: (_rms(a, g),), [xs], [g1], [(D, BF16)], deps=(gather_in[-1],), name="norm_in")
    gather_in = gather_forward(gather_in, hn1, name="gather_w_in_forward")
    arrived(["w_in"], gather_finish(gather_in, gather_in[-1], name="gather_w_in_finish"))
    WT["w_in"] = transposed(W["w_in"])
    early, mid, late = ["w_glu", "w_out"], ["w_up"], ["w_down", "w_ple_gate", "w_ple_proj"]
    gather_early = gather_start([shard[n] for n in early], W["w_in"], name="gather_early_start")
    gather_mid = gather_start([shard[n] for n in mid], gather_early[-1], name="gather_mid_start")
    gather_late = gather_start([shard[n] for n in late], gather_mid[-1], name="gather_late_start")

    (proj,) = mm_nn(hn1, W["w_in"], [F32], deps=(gather_late[-1],), name="proj_in")
    attn, lse = attn_fwd(proj, H, name="attn_fwd")
    gather_early = gather_forward(gather_early, attn, name="gather_early_forward")
    (mix_a,) = rowwise(lambda a, g: (_rms(a, g),), [attn], [ga], [(AW, BF16)], deps=(gather_early[-1],),
                       name="attn_norm")
    arrived(early, gather_finish(gather_early, mix_a, name="gather_early_finish"))

    a_r, a_i, bb_r, bb_i = _discretise(lam_re[0], lam_im[0], log_dt[0], ssm_b_re[0], ssm_b_im[0])
    ssm_consts = (_block_diag(bb_r.swapaxes(1, 2), nslab).astype(BF16), _block_diag(bb_i.swapaxes(1, 2), nslab).astype(BF16),
                  a_r.reshape(nslab, 1, SLAB_STATES), a_i.reshape(nslab, 1, SLAB_STATES),
                  _block_diag(ssm_c_re[0].swapaxes(1, 2), nslab).astype(BF16),
                  _block_diag(ssm_c_im[0].swapaxes(1, 2), nslab).astype(BF16), ssm_d)
    u_seg = _to_segments(proj[:, 3 * AW:]).astype(BF16)
    y_pre, yg, st_r, st_i = ssm_fwd(u_seg, *ssm_consts, name="ssm_fwd")
    gather_mid = gather_forward(gather_mid, y_pre, name="gather_mid_forward")
    (gl1,) = mm_nn(yg, W["w_glu"], [BF16], epi=lambda acc, b: (acc + b,), bias=b_glu, deps=(gather_mid[-1],),
                   name="glu_gate")
    (mix_s,) = rowwise(lambda yp, gl, g: (_rms(_gelu(yp) * _sigmoid(gl), g),), [y_pre, gl1], [gs], [(SW, BF16)],
                       name="ssm_glu_norm")
    mixed = [mix_a, _from_segments(mix_s)]
    (mo,) = mm_nn(mixed, W["w_out"], [BF16], name="mix_out")

    def resid_norm(h, t, gpost, gpre):
        hh = h + _rms(t, gpost)
        return hh, _rms(hh, gpre)

    h1, hn2 = rowwise(resid_norm, [xs, mo], [g2, g3], [(D, F32), (D, BF16)], name="resid_mix")
    arrived(mid, gather_finish(gather_mid, hn2, name="gather_mid_finish"))
    gather_late = gather_forward(gather_late, W["w_up"], name="gather_late_forward")
    WT["w_up"] = transposed(W["w_up"])

    def relu2(acc):
        r = jnp.maximum(acc, 0.0)
        return acc, r * r

    up, act = mm_nn(hn2, W["w_up"], [BF16, BF16], epi=relu2, deps=(gather_late[-1],), tm=1024, tn=1024, name="mlp_up")
    arrived(late, gather_finish(gather_late, act, name="gather_late_finish"))
    (ff,) = mm_nn(act, W["w_down"], [BF16], name="mlp_down")
    h2, hn3 = rowwise(resid_norm, [h1, ff], [g4, g5], [(D, F32), (D, BF16)], name="resid_mlp")
    (gl2,) = mm_nn(hn3, W["w_ple_gate"], [BF16], name="ple_gate")
    pb = p[0, 0].astype(BF16)
    (emb,) = mm_nn(pb, W["w_ple_proj"], [BF16], name="ple_proj")

    def head(h, gl, e, t, g):
        sg = _sigmoid(gl)
        ge = sg * e
        err = h + _rms(ge, g) - t
        dh = err * (1.0 / D)
        dge, dg = _rms_bwd(dh, ge, g)
        return dh, dge * e * sg * (1.0 - sg), dge * sg, jnp.sum(err * err, axis=0, keepdims=True), dg

    dh3, dgl2, demb, loss_part, dg6 = rowwise(head, [h2, gl2, emb, tgt], [g6], [(D, F32), (D, BF16), (D, BF16)],
                                             [D, D], name="ple_loss_head")
    loss_here = (0.5 / D * jnp.sum(loss_part)).reshape(1)

    x_i, y_i, c_i = _place()
    place = jnp.stack([c_i, 2 * x_i + y_i]).astype(jnp.int32)
    grads, out_g, out_d, out_m, out_v = {}, {}, {}, {}, {}

    def to_sibling(names, after, tag):
        chunks = []
        for n in names:
            g = grads[n]
            g = g if n in col_sharded else g.reshape(N_DEV, g.shape[1] // N_DEV, g.shape[2])
            chunks.append(g.reshape(4, 2, g.shape[1], g.shape[2]))
        return chunks, exchange_start(chunks, [(4,) + g.shape[2:] for g in chunks], _core_copies, len(chunks), after,
                                      name=f"grads_to_sibling_{tag}")

    def to_chips(names, sent, after, tag):
        chunks, state = sent
        sums = []
        for n, g, r in zip(names, chunks, exchange_wait(state, after, name=f"grads_from_sibling_{tag}")):
            k, nn = g.shape[2], g.shape[3]
            kb = k // _tile(k, 512, 16)

            def mine(i, s, kb=kb):
                return 2 * (i // kb) + s[0], i % kb

            (s,) = _blocked(lambda a, b: (a.astype(F32) + b.astype(F32),),
                            [(g.reshape(N_DEV, k, nn), mine), r.reshape(4 * k, nn)],
                            [((4 * k, nn), BF16)], place=place, tr=k // kb, name=f"chip_sum_{n}")
            sums.append(s.reshape(4, k, nn))
        return sums, exchange_start(sums, [(3,) + s.shape[1:] for s in sums], _chip_copies, 3 * len(sums), sums[-1],
                                    name=f"grads_to_chips_{tag}")

    def update(w_, m_, v_, own, r0, r1, r2):
        g = own.astype(F32) + r0.astype(F32) + r1.astype(F32) + r2.astype(F32)
        return (g,) + _adamw(w_, g, m_, v_)

    def finish(names, sent, after, tag):
        sums, state = sent
        for n, s, r in zip(names, sums, exchange_wait(state, after, name=f"grads_from_chips_{tag}")):
            shp = weights[n].shape
            res = _blocked(update, [weights[n][0], mom_m[n][0], mom_v[n][0], (s, lambda i, p_: (p_[1], i)),
                                    (r, lambda i, p_: (0, i)), (r, lambda i, p_: (1, i)), (r, lambda i, p_: (2, i))],
                           [(shp[1:], F32)] * 4, place=place, tr=max(16, min(shp[1] // 8, 262144 // shp[2])),
                           name=f"adamw_{n}")
            out_g[n], out_d[n], out_m[n], out_v[n] = (t.reshape(shp) for t in res)
        return out_v[names[-1]]

    grads["w_ple_proj"] = mm_tn(pb, demb, N_DEV, name="grad_w_ple_proj")
    dhn3 = mm_nt(dgl2, W["w_ple_gate"], BF16, name="back_ple_gate")
    grads["w_ple_gate"] = mm_tn(hn3, dgl2, 1, name="grad_w_ple_gate")

    def back_resid(dh, dhn, h, t, gpre, gpost):
        d1, dgpre = _rms_bwd(dhn, h, gpre)
        dhh = dh + d1
        dt, dgpost = _rms_bwd(dhh, t, gpost)
        return dhh, dt, dgpre, dgpost

    dh2, dff, dg5, dg4 = rowwise(back_resid, [dh3, dhn3, h2, ff], [g5, g4], [(D, F32), (D, BF16)], [D, D],
                                 name="back_resid_mlp")
    dup = mm_nt(dff, W["w_down"], BF16, epi=lambda acc, u_: (acc * 2.0 * jnp.maximum(u_.astype(F32), 0.0),),
                extra=up, name="back_mlp_down")
    grads["w_down"] = mm_tn(act, dff, 1, name="grad_w_down")
    group_a = ["w_ple_proj", "w_ple_gate", "w_down"]
    sent_a = to_sibling(group_a, grads["w_down"], "a")
    (dhn2,) = mm_nn(dup, WT["w_up"], [BF16], deps=(sent_a[1][-1],), name="back_mlp_up")
    sent_a = to_chips(group_a, sent_a, dhn2, "a")
    grads["w_up"] = mm_tn(hn2, dup, N_DEV, deps=(sent_a[1][-1],), name="grad_w_up")
    dh1, dmo, dg3, dg2 = rowwise(back_resid, [dh2, dhn2, h1, mo], [g3, g2], [(D, F32), (D, BF16)], [D, D],
                                 name="back_resid_mix")
    dmixed = mm_nt(dmo, W["w_out"], BF16, name="back_mix_out")
    grads["w_out"] = mm_tn(mixed, dmo, 1, name="grad_w_out")

    def back_glu(dm, yp, gl, g):
        ygf = _gelu(yp)
        sg = _sigmoid(gl)
        dssm, dg = _rms_bwd(dm, ygf * sg, g)
        dgl = dssm * ygf * sg * (1.0 - sg)
        return dgl, dssm * sg, dg, jnp.sum(dgl, axis=0, keepdims=True)

    dgl1, dyg_direct, dgs, db_glu = rowwise(back_glu, [_to_segments(dmixed[:, AW:]), y_pre, gl1], [gs],
                                            [(SW, BF16), (SW, F32)], [SW, SW], name="back_glu")
    dyg_gate = mm_nt(dgl1, W["w_glu"], BF16, name="back_glu_gate")
    grads["w_glu"] = mm_tn(yg, dgl1, 1, name="grad_w_glu")
    group_b = ["w_up", "w_out", "w_glu"]
    sent_b = to_sibling(group_b, grads["w_glu"], "b")
    done_a = finish(group_a, sent_a, sent_b[1][-1], "a")

    du_seg, dbb_r, dbb_i, dcb_r, dcb_i, da_r, da_i, d_skip = ssm_bwd(
        u_seg, dyg_direct, dyg_gate, y_pre, st_r, st_i, *ssm_consts, deps=(done_a,), name="ssm_bwd")
    sent_b = to_chips(group_b, sent_b, du_seg, "b")

    def back_attn_norm(dm, a, g):
        da, dg = _rms_bwd(dm, a, g)
        prod = da * a
        delta = jnp.concatenate(
            [jnp.broadcast_to(jnp.sum(prod[:, h * HEAD_DIM:(h + 1) * HEAD_DIM], axis=-1, keepdims=True),
                              (prod.shape[0], HEAD_DIM)) for h in range(H)], axis=1)
        return da, delta, dg

    dattn, delta, dga = rowwise(back_attn_norm, [(dmixed, AW, 0), attn], [ga], [(AW, F32), (AW, F32)], [AW],
                                deps=(sent_b[1][-1],), name="back_attn_norm")
    dq, dk, dv = attn_bwd(proj, dattn, lse, delta, H, name="attn_bwd")
    dproj = [dq, dk, dv, _from_segments(du_seg)]
    (dhn1,) = mm_nn(dproj, WT["w_in"], [BF16], name="back_proj_in")

    def back_in(dh, dhn, a, g):
        d1, dg = _rms_bwd(dhn, a, g)
        return dh + d1, dg

    grad_x, dg1 = rowwise(back_in, [dh1, dhn1, xs], [g1], [(D, F32)], [D], name="back_norm_in")

    cot = dict(
        mix_norm_pre=dg1, mix_norm_post=dg2, mlp_norm_pre=dg3, mlp_norm_post=dg4, ple_norm_pre=dg5, ple_norm_post=dg6,
        attn_out_norm=dga, ssm_out_norm=dgs, b_glu=db_glu, ssm_d=d_skip, loss=loss_here,
        ssm_c_re=_block_diag_part(dcb_r, P_, C_).swapaxes(1, 2), ssm_c_im=_block_diag_part(dcb_i, P_, C_).swapaxes(1, 2),
        a_r=da_r.reshape(G, P_), a_i=da_i.reshape(G, P_),
        bb_r=_block_diag_part(dbb_r, C_, P_).swapaxes(1, 2), bb_i=_block_diag_part(dbb_i, C_, P_).swapaxes(1, 2))
    names = list(cot)
    flat = jnp.concatenate([cot[n].reshape(-1) for n in names])
    total = flat.shape[0]
    rows_ = -(-total // (LANES * 16)) * 16
    flat = jnp.pad(flat, (0, rows_ * LANES - total)).reshape(rows_, LANES)
    gather_small = gather_start([flat], flat, name="gather_small_start")
    grads["w_in"] = mm_tn(hn1, dproj, N_DEV, deps=(gather_small[-1],), tko=2048, name="grad_w_in")
    group_c = ["w_in"]
    sent_c = to_sibling(group_c, grads["w_in"], "c")
    done_b = finish(group_b, sent_b, sent_c[1][-1], "b")
    sent_c = to_chips(group_c, sent_c, done_b, "c")
    gather_small = gather_forward(gather_small, sent_c[1][-1], name="gather_small_forward")
    (every,) = gather_finish(gather_small, gather_small[-1], name="gather_small_finish")
    (summed,) = _blocked(lambda *t: (functools.reduce(lambda a, b: a + b, t),),
                         [(every, functools.partial(lambda i, p_, j: (j, i), j=j)) for j in range(N_DEV)],
                         [((rows_, LANES), F32)], name="sum_small_grads")
    summed = summed.reshape(-1)
    red, off = {}, 0
    for n in names:
        sz = cot[n].size
        red[n] = summed[off:off + sz].reshape(cot[n].shape)
        off += sz
    loss = red["loss"].reshape(())
    _, pull = jax.vjp(_discretise, lam_re[0], lam_im[0], log_dt[0], ssm_b_re[0], ssm_b_im[0])
    d_lre, d_lim, d_ldt, d_bre, d_bim = pull((red["a_r"], red["a_i"], red["bb_r"], red["bb_i"]))
    red.update(lam_re=d_lre, lam_im=d_lim, log_dt=d_ldt, ssm_b_re=d_bre, ssm_b_im=d_bim)

    def pack(d):
        t = jnp.concatenate([d[n].reshape(-1) for n in small])
        r_ = -(-t.shape[0] // (LANES * 16)) * 16
        return jnp.pad(t, (0, r_ * LANES - t.shape[0])).reshape(r_, LANES)

    sw, sg_, sm, sv = pack(weights), pack(red), pack(mom_m), pack(mom_v)
    sd, snm, snv = _blocked(lambda w_, g_, m_, v_: _adamw(w_, g_, m_, v_), [sw, sg_, sm, sv],
                            [(sw.shape, F32)] * 3, name="adamw_small")
    finish(group_c, sent_c, snv, "c")
    off = 0
    for n in small:
        sz = weights[n].size
        shp = weights[n].shape
        out_g[n] = red[n].reshape(shp)
        out_d[n] = sd.reshape(-1)[off:off + sz].reshape(shp)
        out_m[n] = snm.reshape(-1)[off:off + sz].reshape(shp)
        out_v[n] = snv.reshape(-1)[off:off + sz].reshape(shp)
        off += sz

    return (loss, grad_x[None], *[out_g[n] for n in order], *[out_d[n] for n in order],
            *[out_m[n] for n in order], *[out_v[n] for n in order])
```

```python
import functools
import math

import jax
import jax.numpy as jnp
from jax import lax
from jax.experimental import pallas as pl
from jax.experimental.pallas import tpu as pltpu

F32 = jnp.float32
BF16 = jnp.bfloat16
MESH = pl.DeviceIdType.MESH

N_DEV = 8
LANES = 128
SUBLANES = 8
VMEM_LIMIT = 48 * 1024 * 1024
VMEM_LIMIT_SCAN = 60 * 1024 * 1024

HEAD_DIM = 128
BLK = 128
DILATIONS = (1, 4, 16)
SSM_GROUP = 16
SSM_STATE = 64
SLAB_GROUPS = LANES // SSM_GROUP
SLAB_STATES = SLAB_GROUPS * SSM_STATE
SEGMENTS = SUBLANES
SCAN_UNROLL = 4
RMS_EPS = 1e-6
NEG_INF = -1e30

ADAM_LR = 0.001
ADAM_B1 = 0.9
ADAM_B2 = 0.999
ADAM_EPS = 1e-08
ADAM_WD = 0.01
ADAM_STEP = 10


def _tile(n, pref, unit=LANES):
    if n <= pref:
        return n
    t = (pref // unit) * unit
    while t > unit and n % t:
        t -= unit
    assert n % t == 0, (n, pref, unit)
    return t


def _params(sem=None, vmem=VMEM_LIMIT):
    return pltpu.CompilerParams(dimension_semantics=sem, vmem_limit_bytes=vmem)


_NN = (((1,), (0,)), ((), ()))
_NT = (((1,), (1,)), ((), ()))
_TN = (((0,), (0,)), ((), ()))


_ANY = pl.BlockSpec(memory_space=pl.ANY)


def _mm_call(dims, nk, na, nb, pick, n_extra, n_dep, n_out, epi, group=1, **kw):
    first_extra = na + nb
    first_out = first_extra + n_extra + n_dep
    kw["in_specs"] = list(kw["in_specs"]) + [_ANY] * n_dep

    def grouped(refs, step):
        if group == 1:
            return lax.dot_general(refs[0][...], refs[1][...], dims, preferred_element_type=F32)
        kp = refs[0].shape[1]
        return sum(lax.dot_general(refs[step * group + p][...], refs[na][pl.ds(p * kp, kp), :], dims,
                                   preferred_element_type=F32) for p in range(group))

    def single(*refs):
        extra = refs[first_extra:first_extra + n_extra]
        res = epi(grouped(refs, 0), *[e[...] for e in extra])
        for o, r in zip(refs[first_out:first_out + n_out], res):
            o[...] = r.astype(o.dtype)

    if nk == 1:
        assert na == group and nb == 1
        kw["scratch_shapes"] = []
        return pl.pallas_call(single, **kw)

    def body(*refs):
        extra = refs[first_extra:first_extra + n_extra]
        outs = refs[first_out:first_out + n_out]
        acc = refs[-1]
        k = pl.program_id(2)

        @pl.when(k == 0)
        def _():
            acc[...] = jnp.zeros_like(acc)

        def add(a_ref, b_ref):
            acc[...] += lax.dot_general(a_ref[...], b_ref[...], dims, preferred_element_type=F32)

        if na == nb == 1:
            add(refs[0], refs[1])
        elif group > 1:
            for step in range(nk):
                @pl.when(k == step)
                def _(step=step):
                    acc[...] += grouped(refs, step)
        else:
            pa, pb = pick(pl.program_id(0), pl.program_id(1), k)
            for x in range(na):
                for y in range(nb):
                    pl.when((pa == x) & (pb == y))(functools.partial(add, refs[x], refs[na + y]))

        @pl.when(k == nk - 1)
        def _():
            res = epi(acc[...], *[e[...] for e in extra])
            for o, r in zip(outs, res):
                o[...] = r.astype(o.dtype)

    return pl.pallas_call(body, **kw)


def _identity_epi(acc):
    return (acc,)


def _parts(t):
    return list(t) if isinstance(t, (list, tuple)) else [t]


def _part_spec(block, part, which, index):
    def index_map(i, j, k):
        use = which(i, j, k) == part
        r, c = index(i, j, k)
        return jnp.where(use, r, 0), jnp.where(use, c, 0)
    return pl.BlockSpec(block, index_map)


def mm_nn(a, w, out_dtypes, *, name, epi=_identity_epi, bias=None, deps=(), tm=2048, tn=512, tk=2048):
    a = _parts(a)
    M, Kp = a[0].shape
    K = Kp * len(a)
    J, K2, n = w.shape
    assert K == K2
    tm, tn, tk = _tile(M, tm, 16), _tile(n, tn), _tile(K, tk)
    npj = n // tn
    nk = K // tk
    group = 1
    if len(a) == 1:
        in_specs = [pl.BlockSpec((tm, tk), lambda i, j, k: (i, k))]
    else:
        assert tk % Kp == 0
        group = tk // Kp
        in_specs = [pl.BlockSpec((tm, Kp), lambda i, j, k: (i, 0)) for _ in a]
    in_specs.append(pl.BlockSpec((None, tk, tn), lambda i, j, k: (j // npj, k, j % npj)))
    args = a + [w]
    if bias is not None:
        in_specs.append(pl.BlockSpec((1, tn), lambda i, j, k: (0, j)))
        args.append(bias)
    return _mm_call(
        _NN, nk, len(a), 1, lambda i, j, k: (k, 0), len(args) - len(a) - 1, len(deps), len(out_dtypes), epi, group,
        out_shape=[jax.ShapeDtypeStruct((M, J * n), d) for d in out_dtypes],
        grid=(M // tm, J * npj, nk), in_specs=in_specs,
        out_specs=[pl.BlockSpec((tm, tn), lambda i, j, k: (i, j)) for _ in out_dtypes],
        scratch_shapes=[pltpu.VMEM((tm, tn), F32)],
        compiler_params=_params(("parallel", "parallel", "arbitrary")), name=name)(*args, *deps)


def mm_nt(a, w, out_dtype, *, name, epi=_identity_epi, extra=None, tm=2048, tko=512, tnr=2048):
    M, N = a.shape
    J, K, n = w.shape
    assert N == J * n
    tm, tko, tnr = _tile(M, tm, 16), _tile(K, tko), _tile(n, tnr)
    npj = n // tnr
    nk = N // tnr
    in_specs = [pl.BlockSpec((tm, tnr), lambda i, j, k: (i, k)),
                pl.BlockSpec((None, tko, tnr), lambda i, j, k: (k // npj, j, k % npj))]
    args = [a, w]
    if extra is not None:
        in_specs.append(pl.BlockSpec((tm, tko), lambda i, j, k: (i, j)))
        args.append(extra)
    return _mm_call(
        _NT, nk, 1, 1, None, len(args) - 2, 0, 1, epi,
        out_shape=[jax.ShapeDtypeStruct((M, K), out_dtype)],
        grid=(M // tm, K // tko, nk), in_specs=in_specs,
        out_specs=[pl.BlockSpec((tm, tko), lambda i, j, k: (i, j))],
        scratch_shapes=[pltpu.VMEM((tm, tko), F32)],
        compiler_params=_params(("parallel", "parallel", "arbitrary")), name=name)(*args)[0]


def mm_tn(a, b, J, *, name, deps=(), tko=1024, tn=1024, ts=2048):
    a, b = _parts(a), _parts(b)
    S, Kp = a[0].shape
    S2, Np = b[0].shape
    K, N = Kp * len(a), Np * len(b)
    assert S == S2 and N % J == 0
    n = N // J
    tko, tn, ts = _tile(Kp, tko), _tile(math.gcd(n, Np), tn), _tile(S, ts)
    npj = n // tn
    nk = S // ts
    ta, tb = Kp // tko, Np // tn
    assert nk > 1 or len(a) == len(b) == 1
    return _mm_call(
        _TN, nk, len(a), len(b), lambda i, j, k: (i // ta, j // tb), 0, len(deps), 1, _identity_epi,
        out_shape=[jax.ShapeDtypeStruct((J, K, n), BF16)],
        grid=(K // tko, J * npj, nk),
        in_specs=([_part_spec((ts, tko), x, lambda i, j, k: i // ta, lambda i, j, k: (k, i % ta)) for x in range(len(a))]
                  + [_part_spec((ts, tn), y, lambda i, j, k: j // tb, lambda i, j, k: (k, j % tb)) for y in range(len(b))]),
        out_specs=[pl.BlockSpec((None, tko, tn), lambda i, j, k: (j // npj, i, j % npj))],
        scratch_shapes=[pltpu.VMEM((tko, tn), F32)],
        compiler_params=_params(("parallel", "parallel", "arbitrary")), name=name)(*a, *b, *deps)[0]


def rowwise(fn, rows, vecs, outs, accs=(), *, name, deps=(), ts=256):
    rows = [r if isinstance(r, tuple) else (r, r.shape[1], 0) for r in rows]
    S = rows[0][0].shape[0]
    ts = _tile(S, ts, 16)
    nr, nv, no, nd = len(rows), len(vecs), len(outs), len(deps)

    def body(*refs):
        r, v = refs[:nr], refs[nr:nr + nv]
        o, a = refs[nr + nv + nd:nr + nv + nd + no], refs[nr + nv + nd + no:]
        res = fn(*[t[...].astype(F32) for t in r], *[t[...] for t in v])
        for ref, val in zip(o, res[:no]):
            ref[...] = val.astype(ref.dtype)
        if a:
            @pl.when(pl.program_id(0) == 0)
            def _():
                for ref in a:
                    ref[...] = jnp.zeros_like(ref)

            for ref, val in zip(a, res[no:]):
                ref[...] += val

    in_specs = [pl.BlockSpec((ts, w), functools.partial(lambda i, cb: (i, cb), cb=cb)) for _, w, cb in rows]
    in_specs += [pl.BlockSpec(v.shape, lambda i: (0, 0)) for v in vecs] + [_ANY] * nd
    out_shape = [jax.ShapeDtypeStruct((S, w), d) for w, d in outs]
    out_shape += [jax.ShapeDtypeStruct((1, w), F32) for w in accs]
    out_specs = [pl.BlockSpec((ts, w), lambda i: (i, 0)) for w, _ in outs]
    out_specs += [pl.BlockSpec((1, w), lambda i: (0, 0)) for w in accs]
    return pl.pallas_call(body, out_shape=out_shape, grid=(S // ts,), in_specs=in_specs, out_specs=out_specs,
                          compiler_params=_params(("arbitrary",)), name=name)(*[r[0] for r in rows], *vecs, *deps)


def _rms(x, g):
    r = lax.rsqrt(jnp.mean(x * x, axis=-1, keepdims=True) + RMS_EPS)
    return x * r * g


def _rms_bwd(dy, x, g):
    r = lax.rsqrt(jnp.mean(x * x, axis=-1, keepdims=True) + RMS_EPS)
    xh = x * r
    dxh = dy * g
    dx = r * (dxh - xh * jnp.mean(dxh * xh, axis=-1, keepdims=True))
    return dx, jnp.sum(dy * xh, axis=0, keepdims=True)


def _sigmoid(x):
    return pl.reciprocal(1.0 + jnp.exp(-x), approx=True)


_GELU_C = math.sqrt(2.0 / math.pi)


def _gelu(x):
    return 0.5 * x * (1.0 + jnp.tanh(_GELU_C * (x + 0.044715 * x * x * x)))


def _gelu_grad(x):
    t = jnp.tanh(_GELU_C * (x + 0.044715 * x * x * x))
    return 0.5 * (1.0 + t) + 0.5 * x * (1.0 - t * t) * _GELU_C * (1.0 + 3.0 * 0.044715 * x * x)


ATTN_INTERLEAVE = 8
KEY_PAD = BLK * max(DILATIONS)


def _key_mask(n):
    ii = lax.broadcasted_iota(jnp.int32, (BLK, 2 * BLK), 0)
    jj = lax.broadcasted_iota(jnp.int32, (BLK, 2 * BLK), 1)
    return ((jj < BLK) & (jj >= ii) & (n > 0)) | ((jj >= BLK) & (jj - BLK <= ii))


def _units(d, nblk):
    nb = nblk // d
    if nb == 2:
        def unit(idx):
            ii = lax.broadcasted_iota(jnp.int32, (2 * BLK, 2 * BLK), 0)
            jj = lax.broadcasted_iota(jnp.int32, (2 * BLK, 2 * BLK), 1)
            return pl.ds(idx, 2 * BLK, stride=d), pl.ds(KEY_PAD + idx, 2 * BLK, stride=d), (jj <= ii) & (ii - jj <= BLK)
        return d, max(1, ATTN_INTERLEAVE // 4), unit

    def unit(idx):
        r, n = idx // nb, idx % nb
        cur = r + n * (BLK * d)
        keys = cur + (KEY_PAD - BLK * d)
        if d == 1:
            return pl.ds(pl.multiple_of(cur, BLK), BLK), pl.ds(pl.multiple_of(keys, BLK), 2 * BLK), _key_mask(n)
        return pl.ds(cur, BLK, stride=d), pl.ds(keys, 2 * BLK, stride=d), _key_mask(n)
    return nblk, ATTN_INTERLEAVE, unit


def _pad_keys(dst, src):
    dst[pl.ds(0, KEY_PAD), :] = jnp.zeros((KEY_PAD, dst.shape[1]), F32)

    def copy(c, carry):
        dst[pl.ds(pl.multiple_of(KEY_PAD + c * BLK, BLK), BLK), :] = src[pl.ds(pl.multiple_of(c * BLK, BLK), BLK), :]
        return carry

    lax.fori_loop(0, src.shape[0] // BLK, copy, 0)


def attn_fwd(proj, n_heads, *, name):
    S, WP = proj.shape
    assert S % (BLK * max(DILATIONS)) == 0
    nblk = S // BLK
    AW = n_heads * HEAD_DIM
    scale = 1.0 / math.sqrt(HEAD_DIM)

    def body(q_ref, k_ref, v_ref, o_ref, l_ref, acc, mrun, lrun, kp, vp):
        _pad_keys(kp, k_ref)
        _pad_keys(vp, v_ref)
        for first, d in zip((True, False, False), reversed(DILATIONS)):
            n_units, per_step, unit = _units(d, nblk)

            def step(it, carry, first=first, n_units=n_units, per_step=per_step, unit=unit):
                units = [unit(it + j * (n_units // per_step)) for j in range(per_step)]
                ss = [lax.dot_general(q_ref[cur, :].astype(BF16), kp[keys, :].astype(BF16), _NT,
                                      preferred_element_type=F32) * scale for cur, keys, _ in units]
                ss = [jnp.where(mask, s, NEG_INF) for s, (_, _, mask) in zip(ss, units)]
                ms = [jnp.max(s, axis=-1, keepdims=True) for s in ss]
                ps = [jnp.exp(s - m) for s, m in zip(ss, ms)]
                ls = [jnp.sum(p, axis=-1, keepdims=True) for p in ps]
                os_ = [jnp.dot(p.astype(BF16), vp[keys, :].astype(BF16), preferred_element_type=F32)
                       for p, (_, keys, _) in zip(ps, units)]
                for (cur, keys, mask), m, l, o in zip(units, ms, ls, os_):
                    m = jnp.broadcast_to(m, o.shape)
                    l = jnp.broadcast_to(l, o.shape)
                    if first:
                        acc[cur, :], mrun[cur, :], lrun[cur, :] = o, m, l
                    else:
                        m_old = mrun[cur, :]
                        m_new = jnp.maximum(m_old, m)
                        w_old, w_blk = jnp.exp(m_old - m_new), jnp.exp(m - m_new)
                        acc[cur, :] = w_old * acc[cur, :] + w_blk * o
                        lrun[cur, :] = w_old * lrun[cur, :] + w_blk * l
                        mrun[cur, :] = m_new
                return carry

            lax.fori_loop(0, n_units // per_step, step, 0)

        def finish(c, carry):
            r = pl.ds(pl.multiple_of(c * BLK, BLK), BLK)
            o_ref[r, :] = acc[r, :] / lrun[r, :]
            l_ref[r, :] = mrun[r, :] + jnp.log(lrun[r, :])
            return carry

        lax.fori_loop(0, nblk, finish, 0)

    def col(off):
        return pl.BlockSpec((S, HEAD_DIM), lambda h: (0, off + h))

    ospec = pl.BlockSpec((S, HEAD_DIM), lambda h: (0, h))
    return pl.pallas_call(
        body, out_shape=[jax.ShapeDtypeStruct((S, AW), F32)] * 2, grid=(n_heads,),
        in_specs=[col(0), col(n_heads), col(2 * n_heads)], out_specs=[ospec, ospec],
        scratch_shapes=[pltpu.VMEM((S, HEAD_DIM), F32)] * 3 + [pltpu.VMEM((KEY_PAD + S, HEAD_DIM), F32)] * 2,
        compiler_params=_params(("parallel",)), name=name)(proj, proj, proj)


def attn_bwd(proj, do, lse, delta, n_heads, *, name):
    S, WP = proj.shape
    nblk = S // BLK
    AW = n_heads * HEAD_DIM
    scale = 1.0 / math.sqrt(HEAD_DIM)

    def body(q_ref, k_ref, v_ref, do_ref, l_ref, dl_ref, dq_ref, dk_ref, dv_ref, dq_sc, dk_sc, dv_sc, kp, vp):
        _pad_keys(kp, k_ref)
        _pad_keys(vp, v_ref)
        order = list(reversed(DILATIONS))
        assign_first = nblk // order[0] == 2
        if assign_first:
            dk_sc[pl.ds(0, KEY_PAD), :] = jnp.zeros((KEY_PAD, HEAD_DIM), F32)
            dv_sc[pl.ds(0, KEY_PAD), :] = jnp.zeros((KEY_PAD, HEAD_DIM), F32)
        else:
            dq_sc[...] = jnp.zeros_like(dq_sc)
            dk_sc[...] = jnp.zeros_like(dk_sc)
            dv_sc[...] = jnp.zeros_like(dv_sc)
        for assign, d in zip((assign_first, False, False), order):
            n_units, per_step, unit = _units(d, nblk)

            def step(it, carry, n_units=n_units, per_step=per_step, unit=unit, assign=assign):
                units = [unit(it + j * (n_units // per_step)) for j in range(per_step)]
                qs = [q_ref[cur, :].astype(BF16) for cur, _, _ in units]
                gs = [do_ref[cur, :].astype(BF16) for cur, _, _ in units]
                ks = [kp[keys, :].astype(BF16) for _, keys, _ in units]
                ss = [lax.dot_general(q, kb, _NT, preferred_element_type=F32) * scale for q, kb in zip(qs, ks)]
                dps = [lax.dot_general(g, vp[keys, :].astype(BF16), _NT, preferred_element_type=F32)
                       for g, (_, keys, _) in zip(gs, units)]
                ps = [jnp.where(mask, jnp.exp(s - l_ref[cur, :][:, :1]), 0.0) for s, (cur, _, mask) in zip(ss, units)]
                dss = [(p * (dp - dl_ref[cur, :][:, :1]) * scale).astype(BF16)
                       for p, dp, (cur, _, _) in zip(ps, dps, units)]
                for (cur, keys, _), q, g, kb, p, ds in zip(units, qs, gs, ks, ps, dss):
                    dq = jnp.dot(ds, kb, preferred_element_type=F32)
                    dk = lax.dot_general(ds, q, _TN, preferred_element_type=F32)
                    dv = lax.dot_general(p.astype(BF16), g, _TN, preferred_element_type=F32)
                    if assign:
                        dq_sc[cur, :], dk_sc[keys, :], dv_sc[keys, :] = dq, dk, dv
                    else:
                        dq_sc[cur, :] += dq
                        dk_sc[keys, :] += dk
                        dv_sc[keys, :] += dv
                return carry

            lax.fori_loop(0, n_units // per_step, step, 0)
        rows = pl.ds(KEY_PAD, S)
        dq_ref[...] = dq_sc[...].astype(BF16)
        dk_ref[...] = dk_sc[rows, :].astype(BF16)
        dv_ref[...] = dv_sc[rows, :].astype(BF16)

    def col(off):
        return pl.BlockSpec((S, HEAD_DIM), lambda h: (0, off + h))

    ospec = pl.BlockSpec((S, HEAD_DIM), lambda h: (0, h))
    return pl.pallas_call(
        body, out_shape=[jax.ShapeDtypeStruct((S, AW), BF16)] * 3, grid=(n_heads,),
        in_specs=[col(0), col(n_heads), col(2 * n_heads), ospec, ospec, ospec], out_specs=[ospec] * 3,
        scratch_shapes=[pltpu.VMEM((S, HEAD_DIM), F32)] + [pltpu.VMEM((KEY_PAD + S, HEAD_DIM), F32)] * 4,
        compiler_params=_params(("parallel",), VMEM_LIMIT_SCAN), name=name)(proj, proj, proj, do, lse, delta)


def _to_segments(t):
    S, W = t.shape
    return t.reshape(SEGMENTS, S // SEGMENTS, W).swapaxes(0, 1).reshape(S, W)


def _from_segments(t):
    S, W = t.shape
    return t.reshape(S // SEGMENTS, SEGMENTS, W).swapaxes(0, 1).reshape(S, W)


def _cmul(ar, ai, br, bi):
    return ar * br - ai * bi, ar * bi + ai * br


def _power(ar, ai, log2n):
    for _ in range(log2n):
        ar, ai = _cmul(ar, ai, ar, ai)
    return ar, ai


def _shift_rows(x, up):
    row = lax.broadcasted_iota(jnp.int32, x.shape, 0)
    if up:
        return jnp.where(row == SEGMENTS - 1, 0.0, pltpu.roll(x, SEGMENTS - 1, 0))
    return jnp.where(row == 0, 0.0, pltpu.roll(x, 1, 0))


def _segment_carries(er, ei, pr, pi, up):
    cr = jnp.zeros_like(er)
    ci = jnp.zeros_like(ei)
    for _ in range(SEGMENTS - 1):
        tr, ti = _cmul(pr, pi, cr, ci)
        cr, ci = _shift_rows(er + tr, up), _shift_rows(ei + ti, up)
    return cr, ci


def _scan_states(sr, si, ar, ai, T, reverse):
    ns = sr.shape[1]
    ar8 = jnp.broadcast_to(ar, (SEGMENTS, ns))
    ai8 = jnp.broadcast_to(ai, (SEGMENTS, ns))

    def rows(t):
        k = (T - 1 - t) if reverse else t
        return pl.ds(pl.multiple_of(k * SEGMENTS, SEGMENTS), SEGMENTS)

    def advance(t, c):
        tr, ti = _cmul(ar8, ai8, c[0], c[1])
        return tr + sr[rows(t), :], ti + si[rows(t), :]

    def several(step):
        def trip(t, c):
            for j in range(SCAN_UNROLL):
                c = step(t * SCAN_UNROLL + j, c)
            return c
        return trip

    zero = jnp.zeros((SEGMENTS, ns), F32)
    er, ei = lax.fori_loop(0, T // SCAN_UNROLL, several(advance), (zero, zero))
    pr, pi = _power(ar, ai, T.bit_length() - 1)
    cr, ci = _segment_carries(er, ei, jnp.broadcast_to(pr, (SEGMENTS, ns)), jnp.broadcast_to(pi, (SEGMENTS, ns)), reverse)

    def store(t, c):
        nr, ni = advance(t, c)
        sr[rows(t), :] = nr
        si[rows(t), :] = ni
        return nr, ni

    lax.fori_loop(0, T // SCAN_UNROLL, several(store), (cr, ci))
    return cr, ci


def _slab_specs(ns):
    return [pl.BlockSpec((None, LANES, ns), lambda g: (g, 0, 0)),
            pl.BlockSpec((None, LANES, ns), lambda g: (g, 0, 0)),
            pl.BlockSpec((None, 1, ns), lambda g: (g, 0, 0)),
            pl.BlockSpec((None, 1, ns), lambda g: (g, 0, 0)),
            pl.BlockSpec((None, ns, LANES), lambda g: (g, 0, 0)),
            pl.BlockSpec((None, ns, LANES), lambda g: (g, 0, 0)),
            pl.BlockSpec((1, LANES), lambda g: (0, g))]


def _chunks(S):
    rc = _tile(S, 512, 16)
    return rc, S // rc


def ssm_fwd(u, bbr, bbi, ar, ai, cbr, cbi, dsk, *, name):
    S, SW = u.shape
    nslab, _, ns = bbr.shape
    T = S // SEGMENTS
    assert T & (T - 1) == 0
    rc, nc = _chunks(S)

    def body(u_ref, br_ref, bi_ref, ar_ref, ai_ref, cr_ref, ci_ref, d_ref, y_ref, yg_ref, str_ref, sti_ref, sr, si):
        def inputs(c, carry):
            r = pl.ds(pl.multiple_of(c * rc, rc), rc)
            sr[r, :] = jnp.dot(u_ref[r, :], br_ref[...], preferred_element_type=F32)
            si[r, :] = jnp.dot(u_ref[r, :], bi_ref[...], preferred_element_type=F32)
            return carry

        lax.fori_loop(0, nc, inputs, 0)
        _scan_states(sr, si, ar_ref[...], ai_ref[...], T, False)

        def outputs(c, carry):
            r = pl.ds(pl.multiple_of(c * rc, rc), rc)
            srb, sib = sr[r, :].astype(BF16), si[r, :].astype(BF16)
            str_ref[r, :] = srb
            sti_ref[r, :] = sib
            y = (jnp.dot(srb, cr_ref[...], preferred_element_type=F32)
                 - jnp.dot(sib, ci_ref[...], preferred_element_type=F32) + d_ref[...] * u_ref[r, :].astype(F32))
            y_ref[r, :] = y
            yg_ref[r, :] = _gelu(y).astype(BF16)
            return carry

        lax.fori_loop(0, nc, outputs, 0)

    slab = pl.BlockSpec((S, LANES), lambda g: (0, g))
    states = pl.BlockSpec((S, ns), lambda g: (0, g))
    return pl.pallas_call(
        body, out_shape=([jax.ShapeDtypeStruct((S, SW), F32), jax.ShapeDtypeStruct((S, SW), BF16)]
                         + [jax.ShapeDtypeStruct((S, nslab * ns), BF16)] * 2),
        grid=(nslab,), in_specs=[slab] + _slab_specs(ns), out_specs=[slab, slab, states, states],
        scratch_shapes=[pltpu.VMEM((S, ns), F32)] * 2,
        compiler_params=_params(("parallel",), VMEM_LIMIT_SCAN), name=name)(u, bbr, bbi, ar, ai, cbr, cbi, dsk)


def ssm_bwd(u, d_direct, d_gate, y, st_r, st_i, bbr, bbi, ar, ai, cbr, cbi, dsk, *, name, deps=()):
    S, SW = u.shape
    nslab, _, ns = bbr.shape
    T = S // SEGMENTS
    rc, nc = _chunks(S)
    pair_rows = 2 * SEGMENTS

    def body(*refs):
        (u_ref, d1_ref, d2_ref, y_ref, sr_ref, si_ref, br_ref, bi_ref, ar_ref, ai_ref, cr_ref, ci_ref,
         d_ref) = refs[:13]
        (du_ref, dbr_ref, dbi_ref, dcr_ref, dci_ref, dar_ref, dai_ref, dd_ref, lr, li,
         dy_ref) = refs[13 + len(deps):]

        def inputs(c, skip):
            r = pl.ds(pl.multiple_of(c * rc, rc), rc)
            dy = (d1_ref[r, :] + d2_ref[r, :].astype(F32)) * _gelu_grad(y_ref[r, :])
            dy_ref[r, :] = dy
            gb = dy.astype(BF16)
            lr[r, :] = lax.dot_general(gb, cr_ref[...], _NT, preferred_element_type=F32)
            li[r, :] = -lax.dot_general(gb, ci_ref[...], _NT, preferred_element_type=F32)
            return skip + jnp.sum(dy * u_ref[r, :].astype(F32), axis=0, keepdims=True)

        dd_ref[...] = lax.fori_loop(0, nc, inputs, jnp.zeros((1, LANES), F32))
        _scan_states(lr, li, ar_ref[...], -ai_ref[...], T, True)

        def steps(j):
            rows = pl.ds(pl.multiple_of(j * pair_rows, pair_rows), pair_rows)
            tr, ti = sr_ref[rows, :].astype(F32), si_ref[rows, :].astype(F32)
            return tr[:SEGMENTS], tr[SEGMENTS:], ti[:SEGMENTS], ti[SEGMENTS:]

        def pair(j, c):
            acc_r, acc_i, pr, pi = c
            lo_r, hi_r, lo_i, hi_i = steps(j)
            first = pl.ds(pl.multiple_of(j * pair_rows, SEGMENTS), SEGMENTS)
            second = pl.ds(pl.multiple_of(j * pair_rows + SEGMENTS, SEGMENTS), SEGMENTS)
            la_r, la_i, lb_r, lb_i = lr[first, :], li[first, :], lr[second, :], li[second, :]
            return (acc_r + la_r * pr + la_i * pi + lb_r * lo_r + lb_i * lo_i,
                    acc_i - la_r * pi + la_i * pr - lb_r * lo_i + lb_i * lo_r, hi_r, hi_i)

        def pairs(t, c):
            return pair(2 * t + 1, pair(2 * t, c))

        _, end_r, _, end_i = steps(T // 2 - 1)
        zero = jnp.zeros((SEGMENTS, ns), F32)
        acc = lax.fori_loop(0, T // 4, pairs, (zero, zero, _shift_rows(end_r, False), _shift_rows(end_i, False)))
        dar_ref[...] = jnp.sum(acc[0], axis=0, keepdims=True)
        dai_ref[...] = jnp.sum(acc[1], axis=0, keepdims=True)

        dbr_ref[...] = jnp.zeros_like(dbr_ref)
        dbi_ref[...] = jnp.zeros_like(dbi_ref)
        dcr_ref[...] = jnp.zeros_like(dcr_ref)
        dci_ref[...] = jnp.zeros_like(dci_ref)

        def outputs(c, carry):
            r = pl.ds(pl.multiple_of(c * rc, rc), rc)
            ub = u_ref[r, :]
            g = dy_ref[r, :]
            gb = g.astype(BF16)
            lrb = lr[r, :].astype(BF16)
            lib = li[r, :].astype(BF16)
            du_ref[r, :] = (lax.dot_general(lrb, br_ref[...], _NT, preferred_element_type=F32)
                            + lax.dot_general(lib, bi_ref[...], _NT, preferred_element_type=F32)
                            + d_ref[...] * g).astype(BF16)
            dbr_ref[...] += lax.dot_general(ub, lrb, _TN, preferred_element_type=F32)
            dbi_ref[...] += lax.dot_general(ub, lib, _TN, preferred_element_type=F32)
            dcr_ref[...] += lax.dot_general(sr_ref[r, :], gb, _TN, preferred_element_type=F32)
            dci_ref[...] -= lax.dot_general(si_ref[r, :], gb, _TN, preferred_element_type=F32)
            return carry

        lax.fori_loop(0, nc, outputs, 0)

    slab = pl.BlockSpec((S, LANES), lambda g: (0, g))
    states = pl.BlockSpec((S, ns), lambda g: (0, g))
    bspec = pl.BlockSpec((None, LANES, ns), lambda g: (g, 0, 0))
    cspec = pl.BlockSpec((None, ns, LANES), lambda g: (g, 0, 0))
    aspec = pl.BlockSpec((None, 1, ns), lambda g: (g, 0, 0))
    return pl.pallas_call(
        body,
        out_shape=[jax.ShapeDtypeStruct((S, SW), BF16),
                   jax.ShapeDtypeStruct((nslab, LANES, ns), F32), jax.ShapeDtypeStruct((nslab, LANES, ns), F32),
                   jax.ShapeDtypeStruct((nslab, ns, LANES), F32), jax.ShapeDtypeStruct((nslab, ns, LANES), F32),
                   jax.ShapeDtypeStruct((nslab, 1, ns), F32), jax.ShapeDtypeStruct((nslab, 1, ns), F32),
                   jax.ShapeDtypeStruct((1, SW), F32)],
        grid=(nslab,), in_specs=[slab, slab, slab, slab, states, states] + _slab_specs(ns) + [_ANY] * len(deps),
        out_specs=[slab, bspec, bspec, cspec, cspec, aspec, aspec, pl.BlockSpec((1, LANES), lambda g: (0, g))],
        scratch_shapes=[pltpu.VMEM((S, ns), F32)] * 2 + [pltpu.VMEM((S, LANES), F32)],
        compiler_params=_params(("parallel",), VMEM_LIMIT_SCAN), name=name)(
            u, d_direct, d_gate, y, st_r, st_i, bbr, bbi, ar, ai, cbr, cbi, dsk, *deps)


def _discretise(lam_re, lam_im, log_dt, b_re, b_im):
    dt = jnp.exp(log_dt)[:, None]
    mag = jnp.exp(lam_re * dt)
    ar = mag * jnp.cos(lam_im * dt)
    ai = mag * jnp.sin(lam_im * dt)
    nr, ni = ar - 1.0, ai
    den = lam_re * lam_re + lam_im * lam_im
    cr = ((nr * lam_re + ni * lam_im) / den)[..., None]
    ci = ((ni * lam_re - nr * lam_im) / den)[..., None]
    return ar, ai, cr * b_re - ci * b_im, cr * b_im + ci * b_re


def _block_diag(t, nslab):
    G, R, C = t.shape
    eye = jnp.eye(SLAB_GROUPS, dtype=t.dtype)
    t = t.reshape(nslab, SLAB_GROUPS, R, C)
    return jnp.einsum('sgrc,gh->sgrhc', t, eye).reshape(nslab, SLAB_GROUPS * R, SLAB_GROUPS * C)


def _block_diag_part(t, R, C):
    nslab = t.shape[0]
    eye = jnp.eye(SLAB_GROUPS, dtype=t.dtype)
    t = t.reshape(nslab, SLAB_GROUPS, R, SLAB_GROUPS, C)
    return jnp.einsum('sgrhc,gh->sgrc', t, eye).reshape(nslab * SLAB_GROUPS, R, C)


def _place():
    return lax.axis_index("x"), lax.axis_index("y"), lax.axis_index("c")


_HBM = pl.BlockSpec(memory_space=pltpu.HBM)
_SEM = pl.BlockSpec(memory_space=pltpu.SEMAPHORE)
_ORDERED_EFFECT = pltpu.SideEffectType.DATAFLOW_SIDE_EFFECTING


def _split_call(name, srcs, zones, sems_in, n_new, body_fn, after):
    nsrc, nz, ns, nn = len(srcs), len(zones), len(sems_in), len(n_new)
    nb = nsrc + nz

    def body(*refs):
        outs = refs[nb + ns + 1:]
        body_fn(refs[:nb], refs[nb:nb + ns], outs[:nn])
        outs[nn + nz][...] = jnp.zeros((SUBLANES, LANES), F32)

    res = pl.pallas_call(
        body, name=name,
        out_shape=([pltpu.SemaphoreType.DMA((n,)) for n in n_new] + [pltpu.HBM(b.shape, b.dtype) for b in zones]
                   + [jax.ShapeDtypeStruct((SUBLANES, LANES), F32)]),
        in_specs=[_HBM] * nb + [_SEM] * ns + [_ANY],
        out_specs=[_SEM] * nn + [_HBM] * nz + [pl.BlockSpec(memory_space=pltpu.VMEM)],
        input_output_aliases={nsrc + i: nn + i for i in range(nz)},
        compiler_params=pltpu.CompilerParams(has_side_effects=_ORDERED_EFFECT))(
            *[pltpu.with_memory_space_constraint(b, pltpu.HBM) for b in list(srcs) + list(zones)], *sems_in, after)
    return list(res[:nn]), list(res[nn:nn + nz]), res[-1]


def _mesh_peers():
    x, y, c = _place()
    return x, y, c, (x, y, 1 - c), [(1 - x, y), (x, 1 - y), (1 - x, 1 - y)]


def gather_start(shards, after, *, name):
    nw = len(shards)
    x, y, c = _place()
    zones = [lax.dynamic_update_slice(lax.empty((N_DEV,) + s.shape, s.dtype), s[None], (4 * x + 2 * y + c, 0, 0))
             for s in shards]

    def body(bufs, taken, new):
        for cp in _gather_first(bufs, nw, new[0], new[1]):
            cp.start()

    sems, zones, token = _split_call(name, shards, zones, [], [4 * nw, 4 * nw], body, after)
    return shards, sems, zones, token


def _gather_first(bufs, nw, send, recv):
    x, y, c, sibling, chips = _mesh_peers()
    out = []
    for w in range(nw):
        slot = bufs[nw + w].at[4 * x + 2 * y + c]
        for k, to in enumerate([sibling] + [(*ch, c) for ch in chips]):
            out.append(pltpu.make_async_remote_copy(
                src_ref=bufs[w], dst_ref=slot, send_sem=send.at[4 * w + k], recv_sem=recv.at[4 * w + k],
                device_id=to, device_id_type=MESH))
    return out


def _gather_slot_copy(bufs, nw, w, block, send_sem, recv_sem, to):
    px, py, pc = block
    slot = bufs[nw + w].at[4 * px + 2 * py + pc]
    return pltpu.make_async_remote_copy(src_ref=slot, dst_ref=slot, send_sem=send_sem, recv_sem=recv_sem,
                                        device_id=to, device_id_type=MESH)


def gather_forward(state, after, *, name):
    shards, sems, zones, _ = state
    nw = len(shards)

    def body(bufs, taken, new):
        x, y, c, sibling, chips = _mesh_peers()
        for j, ch in enumerate(chips):
            for w in range(nw):
                k = 4 * w + 1 + j
                _gather_slot_copy(bufs, nw, w, (*ch, c), taken[0].at[k], taken[1].at[k], (*ch, c)).wait_recv()
                _gather_slot_copy(bufs, nw, w, (*ch, c), new[0].at[3 * w + j], new[1].at[3 * w + j], sibling).start()
        for w in range(nw):
            _gather_slot_copy(bufs, nw, w, sibling, taken[0].at[4 * w], taken[1].at[4 * w], sibling).wait_recv()
        for cp in _gather_first(bufs, nw, taken[0], taken[1]):
            cp.wait_send()

    sems, zones, token = _split_call(name, shards, zones, sems, [3 * nw, 3 * nw], body, after)
    return shards, sems, zones, token


def gather_finish(state, after, *, name):
    shards, sems, zones, _ = state
    nw = len(shards)

    def body(bufs, taken, new):
        x, y, c, sibling, chips = _mesh_peers()
        for w in range(nw):
            for j, ch in enumerate(chips):
                cp = _gather_slot_copy(bufs, nw, w, (*ch, 1 - c), taken[0].at[3 * w + j], taken[1].at[3 * w + j], sibling)
                cp.wait_send()
                cp.wait_recv()

    _, zones, _ = _split_call(name, shards, zones, sems, [], body, after)
    return zones


def exchange_start(srcs, zone_shapes, copies, n, after, *, name):
    nw = len(srcs)
    zones = [lax.empty(z, s.dtype) for z, s in zip(zone_shapes, srcs)]

    def body(bufs, taken, new):
        for cp in copies(bufs[:nw], bufs[nw:], new[0], new[1]):
            cp.start()

    sems, zones, token = _split_call(name, srcs, zones, [], [n, n], body, after)
    return srcs, copies, sems, zones, token


def exchange_wait(state, after, *, name):
    srcs, copies, sems, zones, _ = state
    nw = len(srcs)

    def body(bufs, taken, new):
        for cp in copies(bufs[:nw], bufs[nw:], taken[0], taken[1]):
            cp.wait_send()
            cp.wait_recv()

    _, zones, _ = _split_call(name, srcs, zones, sems, [], body, after)
    return zones


def _core_copies(srcs, zones, send, recv):
    x, y, c = _place()
    return [pltpu.make_async_remote_copy(
        src_ref=srcs[w].at[:, 1 - c], dst_ref=zones[w], send_sem=send.at[w], recv_sem=recv.at[w],
        device_id=(x, y, 1 - c), device_id_type=MESH) for w in range(len(srcs))]


def _chip_copies(srcs, zones, send, recv):
    x, y, c = _place()
    chips = [(1 - x, y), (x, 1 - y), (1 - x, 1 - y)]
    return [pltpu.make_async_remote_copy(
        src_ref=srcs[w].at[2 * cx + cy], dst_ref=zones[w].at[j], send_sem=send.at[3 * w + j],
        recv_sem=recv.at[3 * w + j], device_id=(cx, cy, c), device_id_type=MESH)
        for w in range(len(srcs)) for j, (cx, cy) in enumerate(chips)]


def _blocked(fn, ins, outs, *, name, place=None, tr=256):
    k, n = outs[0][0]
    tr = _tile(k, tr, 16)
    if place is None:
        place = jnp.zeros((1,), jnp.int32)
    specs = []
    args = []
    for a in ins:
        if isinstance(a, tuple):
            arr, lead = a
            specs.append(pl.BlockSpec((None, tr, n), functools.partial(lambda i, s, lead: (*lead(i, s), 0), lead=lead)))
            args.append(arr)
        else:
            specs.append(pl.BlockSpec((tr, n), lambda i, s: (i, 0)))
            args.append(a)
    nin = len(args)

    def body(place_ref, *refs):
        res = fn(*[r[...] for r in refs[:nin]])
        for ref, val in zip(refs[nin:], res):
            ref[...] = val.astype(ref.dtype)

    return pl.pallas_call(
        body, out_shape=[jax.ShapeDtypeStruct(s, d) for s, d in outs],
        grid_spec=pltpu.PrefetchScalarGridSpec(
            num_scalar_prefetch=1, grid=(k // tr,), in_specs=specs,
            out_specs=[pl.BlockSpec((tr, n), lambda i, s: (i, 0)) for _ in outs]),
        compiler_params=_params(("parallel",)), name=name)(place, *args)


def _adamw(w, g, m, v):
    m = ADAM_B1 * m + (1.0 - ADAM_B1) * g
    v = ADAM_B2 * v + (1.0 - ADAM_B2) * (g * g)
    m_hat = m / (1.0 - ADAM_B1 ** ADAM_STEP)
    v_hat = v / (1.0 - ADAM_B2 ** ADAM_STEP)
    delta = -ADAM_LR * (m_hat * pl.reciprocal(jnp.sqrt(v_hat) + ADAM_EPS, approx=True) + ADAM_WD * w)
    return delta, m, v


def kernel(x, p, mix_norm_pre, w_in, lam_re, lam_im, log_dt, ssm_b_re, ssm_b_im, ssm_c_re, ssm_c_im, ssm_d, w_glu, b_glu, attn_out_norm, ssm_out_norm, w_out, mix_norm_post, mlp_norm_pre, w_up, w_down, mlp_norm_post, ple_norm_pre, w_ple_gate, w_ple_proj, ple_norm_post, loss_target, m_mix_norm_pre, m_w_in, m_lam_re, m_lam_im, m_log_dt, m_ssm_b_re, m_ssm_b_im, m_ssm_c_re, m_ssm_c_im, m_ssm_d, m_w_glu, m_b_glu, m_attn_out_norm, m_ssm_out_norm, m_w_out, m_mix_norm_post, m_mlp_norm_pre, m_w_up, m_w_down, m_mlp_norm_post, m_ple_norm_pre, m_w_ple_gate, m_w_ple_proj, m_ple_norm_post, v_mix_norm_pre, v_w_in, v_lam_re, v_lam_im, v_log_dt, v_ssm_b_re, v_ssm_b_im, v_ssm_c_re, v_ssm_c_im, v_ssm_d, v_w_glu, v_b_glu, v_attn_out_norm, v_ssm_out_norm, v_w_out, v_mix_norm_post, v_mlp_norm_pre, v_w_up, v_w_down, v_mlp_norm_post, v_ple_norm_pre, v_w_ple_gate, v_w_ple_proj, v_ple_norm_post):
    weights = dict(mix_norm_pre=mix_norm_pre, w_in=w_in, lam_re=lam_re, lam_im=lam_im, log_dt=log_dt, ssm_b_re=ssm_b_re, ssm_b_im=ssm_b_im, ssm_c_re=ssm_c_re, ssm_c_im=ssm_c_im, ssm_d=ssm_d, w_glu=w_glu, b_glu=b_glu, attn_out_norm=attn_out_norm, ssm_out_norm=ssm_out_norm, w_out=w_out, mix_norm_post=mix_norm_post, mlp_norm_pre=mlp_norm_pre, w_up=w_up, w_down=w_down, mlp_norm_post=mlp_norm_post, ple_norm_pre=ple_norm_pre, w_ple_gate=w_ple_gate, w_ple_proj=w_ple_proj, ple_norm_post=ple_norm_post)
    mom_m = dict(mix_norm_pre=m_mix_norm_pre, w_in=m_w_in, lam_re=m_lam_re, lam_im=m_lam_im, log_dt=m_log_dt, ssm_b_re=m_ssm_b_re, ssm_b_im=m_ssm_b_im, ssm_c_re=m_ssm_c_re, ssm_c_im=m_ssm_c_im, ssm_d=m_ssm_d, w_glu=m_w_glu, b_glu=m_b_glu, attn_out_norm=m_attn_out_norm, ssm_out_norm=m_ssm_out_norm, w_out=m_w_out, mix_norm_post=m_mix_norm_post, mlp_norm_pre=m_mlp_norm_pre, w_up=m_w_up, w_down=m_w_down, mlp_norm_post=m_mlp_norm_post, ple_norm_pre=m_ple_norm_pre, w_ple_gate=m_w_ple_gate, w_ple_proj=m_w_ple_proj, ple_norm_post=m_ple_norm_post)
    mom_v = dict(mix_norm_pre=v_mix_norm_pre, w_in=v_w_in, lam_re=v_lam_re, lam_im=v_lam_im, log_dt=v_log_dt, ssm_b_re=v_ssm_b_re, ssm_b_im=v_ssm_b_im, ssm_c_re=v_ssm_c_re, ssm_c_im=v_ssm_c_im, ssm_d=v_ssm_d, w_glu=v_w_glu, b_glu=v_b_glu, attn_out_norm=v_attn_out_norm, ssm_out_norm=v_ssm_out_norm, w_out=v_w_out, mix_norm_post=v_mix_norm_post, mlp_norm_pre=v_mlp_norm_pre, w_up=v_w_up, w_down=v_w_down, mlp_norm_post=v_mlp_norm_post, ple_norm_pre=v_ple_norm_pre, w_ple_gate=v_w_ple_gate, w_ple_proj=v_w_ple_proj, ple_norm_post=v_ple_norm_post)
    order = list(weights)
    big = ["w_in", "w_glu", "w_out", "w_up", "w_down", "w_ple_gate", "w_ple_proj"]
    col_sharded = {"w_in", "w_up", "w_ple_proj"}
    small = [n for n in order if n not in big]

    _, S, D = x.shape
    xs = x[0]
    tgt = loss_target[0]
    AW = attn_out_norm.shape[1]
    SW = ssm_d.shape[1]
    H = AW // HEAD_DIM
    G = SW // SSM_GROUP
    nslab = G // SLAB_GROUPS
    P_, C_ = SSM_STATE, SSM_GROUP

    shard = {n: weights[n][0].astype(BF16) for n in big}
    W, WT = {}, {}

    def arrived(names, gathered):
        for n, g in zip(names, gathered):
            W[n] = g if n in col_sharded else g.reshape(1, N_DEV * g.shape[1], g.shape[2])

    def transposed(g):
        return jnp.swapaxes(g, 1, 2).reshape(1, g.shape[0] * g.shape[2], g.shape[1])

    g1, g2, g3, g4, g5, g6 = (weights[n] for n in ("mix_norm_pre", "mix_norm_post", "mlp_norm_pre",
                                                      "mlp_norm_post", "ple_norm_pre", "ple_norm_post"))
    ga, gs = attn_out_norm, ssm_out_norm
    gather_in = gather_start([shard["w_in"]], shard["w_in"], name="gather_w_in_start")
    (hn1,) = rowwise(lambda a, g: (_rms(a, g),), [xs], [g1], [(D, BF16)], deps=(gather_in[-1],), name="norm_in")
    gather_in = gather_forward(gather_in, hn1, name="gather_w_in_forward")
    arrived(["w_in"], gather_finish(gather_in, gather_in[-1], name="gather_w_in_finish"))
    WT["w_in"] = transposed(W["w_in"])
    early, mid, late = ["w_glu", "w_out"], ["w_up"], ["w_down", "w_ple_gate", "w_ple_proj"]
    gather_early = gather_start([shard[n] for n in early], W["w_in"], name="gather_early_start")
    gather_mid = gather_start([shard[n] for n in mid], gather_early[-1], name="gather_mid_start")
    gather_late = gather_start([shard[n] for n in late], gather_mid[-1], name="gather_late_start")

    (proj,) = mm_nn(hn1, W["w_in"], [F32], deps=(gather_late[-1],), name="proj_in")
    attn, lse = attn_fwd(proj, H, name="attn_fwd")
    gather_early = gather_forward(gather_early, attn, name="gather_early_forward")
    (mix_a,) = rowwise(lambda a, g: (_rms(a, g),), [attn], [ga], [(AW, BF16)], deps=(gather_early[-1],),
                       name="attn_norm")
    arrived(early, gather_finish(gather_early, mix_a, name="gather_early_finish"))

    a_r, a_i, bb_r, bb_i = _discretise(lam_re[0], lam_im[0], log_dt[0], ssm_b_re[0], ssm_b_im[0])
    ssm_consts = (_block_diag(bb_r.swapaxes(1, 2), nslab).astype(BF16), _block_diag(bb_i.swapaxes(1, 2), nslab).astype(BF16),
                  a_r.reshape(nslab, 1, SLAB_STATES), a_i.reshape(nslab, 1, SLAB_STATES),
                  _block_diag(ssm_c_re[0].swapaxes(1, 2), nslab).astype(BF16),
                  _block_diag(ssm_c_im[0].swapaxes(1, 2), nslab).astype(BF16), ssm_d)
    u_seg = _to_segments(proj[:, 3 * AW:]).astype(BF16)
    y_pre, yg, st_r, st_i = ssm_fwd(u_seg, *ssm_consts, name="ssm_fwd")
    gather_mid = gather_forward(gather_mid, y_pre, name="gather_mid_forward")
    (gl1,) = mm_nn(yg, W["w_glu"], [BF16], epi=lambda acc, b: (acc + b,), bias=b_glu, deps=(gather_mid[-1],),
                   name="glu_gate")
    (mix_s,) = rowwise(lambda yp, gl, g: (_rms(_gelu(yp) * _sigmoid(gl), g),), [y_pre, gl1], [gs], [(SW, BF16)],
                       name="ssm_glu_norm")
    mixed = [mix_a, _from_segments(mix_s)]
    (mo,) = mm_nn(mixed, W["w_out"], [BF16], name="mix_out")

    def resid_norm(h, t, gpost, gpre):
        hh = h + _rms(t, gpost)
        return hh, _rms(hh, gpre)

    h1, hn2 = rowwise(resid_norm, [xs, mo], [g2, g3], [(D, F32), (D, BF16)], name="resid_mix")
    arrived(mid, gather_finish(gather_mid, hn2, name="gather_mid_finish"))
    gather_late = gather_forward(gather_late, W["w_up"], name="gather_late_forward")
    WT["w_up"] = transposed(W["w_up"])

    def relu2(acc):
        r = jnp.maximum(acc, 0.0)
        return acc, r * r

    up, act = mm_nn(hn2, W["w_up"], [BF16, BF16], epi=relu2, deps=(gather_late[-1],), tm=1024, tn=1024, name="mlp_up")
    arrived(late, gather_finish(gather_late, act, name="gather_late_finish"))
    (ff,) = mm_nn(act, W["w_down"], [BF16], name="mlp_down")
    h2, hn3 = rowwise(resid_norm, [h1, ff], [g4, g5], [(D, F32), (D, BF16)], name="resid_mlp")
    (gl2,) = mm_nn(hn3, W["w_ple_gate"], [BF16], name="ple_gate")
    pb = p[0, 0].astype(BF16)
    (emb,) = mm_nn(pb, W["w_ple_proj"], [BF16], name="ple_proj")

    def head(h, gl, e, t, g):
        sg = _sigmoid(gl)
        ge = sg * e
        err = h + _rms(ge, g) - t
        dh = err * (1.0 / D)
        dge, dg = _rms_bwd(dh, ge, g)
        return dh, dge * e * sg * (1.0 - sg), dge * sg, jnp.sum(err * err, axis=0, keepdims=True), dg

    dh3, dgl2, demb, loss_part, dg6 = rowwise(head, [h2, gl2, emb, tgt], [g6], [(D, F32), (D, BF16), (D, BF16)],
                                             [D, D], name="ple_loss_head")
    loss = lax.psum(0.5 / D * jnp.sum(loss_part), ("x", "y", "c"))

    x_i, y_i, c_i = _place()
    place = jnp.stack([c_i, 2 * x_i + y_i]).astype(jnp.int32)
    grads, out_g, out_d, out_m, out_v = {}, {}, {}, {}, {}

    def to_sibling(names, after, tag):
        chunks = []
        for n in names:
            g = grads[n]
            g = g if n in col_sharded else g.reshape(N_DEV, g.shape[1] // N_DEV, g.shape[2])
            chunks.append(g.reshape(4, 2, g.shape[1], g.shape[2]))
        return chunks, exchange_start(chunks, [(4,) + g.shape[2:] for g in chunks], _core_copies, len(chunks), after,
                                      name=f"grads_to_sibling_{tag}")

    def to_chips(names, sent, after, tag):
        chunks, state = sent
        sums = []
        for n, g, r in zip(names, chunks, exchange_wait(state, after, name=f"grads_from_sibling_{tag}")):
            k, nn = g.shape[2], g.shape[3]
            kb = k // _tile(k, 512, 16)

            def mine(i, s, kb=kb):
                return 2 * (i // kb) + s[0], i % kb

            (s,) = _blocked(lambda a, b: (a.astype(F32) + b.astype(F32),),
                            [(g.reshape(N_DEV, k, nn), mine), r.reshape(4 * k, nn)],
                            [((4 * k, nn), BF16)], place=place, tr=k // kb, name=f"chip_sum_{n}")
            sums.append(s.reshape(4, k, nn))
        return sums, exchange_start(sums, [(3,) + s.shape[1:] for s in sums], _chip_copies, 3 * len(sums), sums[-1],
                                    name=f"grads_to_chips_{tag}")

    def update(w_, m_, v_, own, r0, r1, r2):
        g = own.astype(F32) + r0.astype(F32) + r1.astype(F32) + r2.astype(F32)
        return (g,) + _adamw(w_, g, m_, v_)

    def finish(names, sent, after, tag):
        sums, state = sent
        for n, s, r in zip(names, sums, exchange_wait(state, after, name=f"grads_from_chips_{tag}")):
            shp = weights[n].shape
            res = _blocked(update, [weights[n][0], mom_m[n][0], mom_v[n][0], (s, lambda i, p_: (p_[1], i)),
                                    (r, lambda i, p_: (0, i)), (r, lambda i, p_: (1, i)), (r, lambda i, p_: (2, i))],
                           [(shp[1:], F32)] * 4, place=place, tr=max(16, min(shp[1] // 8, 262144 // shp[2])),
                           name=f"adamw_{n}")
            out_g[n], out_d[n], out_m[n], out_v[n] = (t.reshape(shp) for t in res)
        return out_v[names[-1]]

    grads["w_ple_proj"] = mm_tn(pb, demb, N_DEV, name="grad_w_ple_proj")
    dhn3 = mm_nt(dgl2, W["w_ple_gate"], BF16, name="back_ple_gate")
    grads["w_ple_gate"] = mm_tn(hn3, dgl2, 1, name="grad_w_ple_gate")

    def back_resid(dh, dhn, h, t, gpre, gpost):
        d1, dgpre = _rms_bwd(dhn, h, gpre)
        dhh = dh + d1
        dt, dgpost = _rms_bwd(dhh, t, gpost)
        return dhh, dt, dgpre, dgpost

    dh2, dff, dg5, dg4 = rowwise(back_resid, [dh3, dhn3, h2, ff], [g5, g4], [(D, F32), (D, BF16)], [D, D],
                                 name="back_resid_mlp")
    dup = mm_nt(dff, W["w_down"], BF16, epi=lambda acc, u_: (acc * 2.0 * jnp.maximum(u_.astype(F32), 0.0),),
                extra=up, name="back_mlp_down")
    grads["w_down"] = mm_tn(act, dff, 1, name="grad_w_down")
    group_a = ["w_ple_proj", "w_ple_gate", "w_down"]
    sent_a = to_sibling(group_a, grads["w_down"], "a")
    (dhn2,) = mm_nn(dup, WT["w_up"], [BF16], deps=(sent_a[1][-1],), name="back_mlp_up")
    sent_a = to_chips(group_a, sent_a, dhn2, "a")
    grads["w_up"] = mm_tn(hn2, dup, N_DEV, deps=(sent_a[1][-1],), name="grad_w_up")
    dh1, dmo, dg3, dg2 = rowwise(back_resid, [dh2, dhn2, h1, mo], [g3, g2], [(D, F32), (D, BF16)], [D, D],
                                 name="back_resid_mix")
    dmixed = mm_nt(dmo, W["w_out"], BF16, name="back_mix_out")
    grads["w_out"] = mm_tn(mixed, dmo, 1, name="grad_w_out")

    def back_glu(dm, yp, gl, g):
        ygf = _gelu(yp)
        sg = _sigmoid(gl)
        dssm, dg = _rms_bwd(dm, ygf * sg, g)
        dgl = dssm * ygf * sg * (1.0 - sg)
        return dgl, dssm * sg, dg, jnp.sum(dgl, axis=0, keepdims=True)

    dgl1, dyg_direct, dgs, db_glu = rowwise(back_glu, [_to_segments(dmixed[:, AW:]), y_pre, gl1], [gs],
                                            [(SW, BF16), (SW, F32)], [SW, SW], name="back_glu")
    dyg_gate = mm_nt(dgl1, W["w_glu"], BF16, name="back_glu_gate")
    grads["w_glu"] = mm_tn(yg, dgl1, 1, name="grad_w_glu")
    group_b = ["w_up", "w_out", "w_glu"]
    sent_b = to_sibling(group_b, grads["w_glu"], "b")
    done_a = finish(group_a, sent_a, sent_b[1][-1], "a")

    du_seg, dbb_r, dbb_i, dcb_r, dcb_i, da_r, da_i, d_skip = ssm_bwd(
        u_seg, dyg_direct, dyg_gate, y_pre, st_r, st_i, *ssm_consts, deps=(done_a,), name="ssm_bwd")
    sent_b = to_chips(group_b, sent_b, du_seg, "b")

    def back_attn_norm(dm, a, g):
        da, dg = _rms_bwd(dm, a, g)
        prod = da * a
        delta = jnp.concatenate(
            [jnp.broadcast_to(jnp.sum(prod[:, h * HEAD_DIM:(h + 1) * HEAD_DIM], axis=-1, keepdims=True),
                              (prod.shape[0], HEAD_DIM)) for h in range(H)], axis=1)
        return da, delta, dg

    dattn, delta, dga = rowwise(back_attn_norm, [(dmixed, AW, 0), attn], [ga], [(AW, F32), (AW, F32)], [AW],
                                deps=(sent_b[1][-1],), name="back_attn_norm")
    dq, dk, dv = attn_bwd(proj, dattn, lse, delta, H, name="attn_bwd")
    dproj = [dq, dk, dv, _from_segments(du_seg)]
    (dhn1,) = mm_nn(dproj, WT["w_in"], [BF16], name="back_proj_in")

    def back_in(dh, dhn, a, g):
        d1, dg = _rms_bwd(dhn, a, g)
        return dh + d1, dg

    grad_x, dg1 = rowwise(back_in, [dh1, dhn1, xs], [g1], [(D, F32)], [D], name="back_norm_in")

    cot = dict(
        mix_norm_pre=dg1, mix_norm_post=dg2, mlp_norm_pre=dg3, mlp_norm_post=dg4, ple_norm_pre=dg5, ple_norm_post=dg6,
        attn_out_norm=dga, ssm_out_norm=dgs, b_glu=db_glu, ssm_d=d_skip,
        ssm_c_re=_block_diag_part(dcb_r, P_, C_).swapaxes(1, 2), ssm_c_im=_block_diag_part(dcb_i, P_, C_).swapaxes(1, 2),
        a_r=da_r.reshape(G, P_), a_i=da_i.reshape(G, P_),
        bb_r=_block_diag_part(dbb_r, C_, P_).swapaxes(1, 2), bb_i=_block_diag_part(dbb_i, C_, P_).swapaxes(1, 2))
    names = list(cot)
    flat = jnp.concatenate([cot[n].reshape(-1) for n in names])
    total = flat.shape[0]
    rows_ = -(-total // (LANES * 16)) * 16
    flat = jnp.pad(flat, (0, rows_ * LANES - total)).reshape(rows_, LANES)
    gather_small = gather_start([flat], flat, name="gather_small_start")
    grads["w_in"] = mm_tn(hn1, dproj, N_DEV, deps=(gather_small[-1],), tko=2048, name="grad_w_in")
    group_c = ["w_in"]
    sent_c = to_sibling(group_c, grads["w_in"], "c")
    done_b = finish(group_b, sent_b, sent_c[1][-1], "b")
    sent_c = to_chips(group_c, sent_c, done_b, "c")
    gather_small = gather_forward(gather_small, sent_c[1][-1], name="gather_small_forward")
    (every,) = gather_finish(gather_small, gather_small[-1], name="gather_small_finish")
    (summed,) = _blocked(lambda *t: (functools.reduce(lambda a, b: a + b, t),),
                         [(every, functools.partial(lambda i, p_, j: (j, i), j=j)) for j in range(N_DEV)],
                         [((rows_, LANES), F32)], name="sum_small_grads")
    summed = summed.reshape(-1)
    red, off = {}, 0
    for n in names:
        sz = cot[n].size
        red[n] = summed[off:off + sz].reshape(cot[n].shape)
        off += sz
    _, pull = jax.vjp(_discretise, lam_re[0], lam_im[0], log_dt[0], ssm_b_re[0], ssm_b_im[0])
    d_lre, d_lim, d_ldt, d_bre, d_bim = pull((red["a_r"], red["a_i"], red["bb_r"], red["bb_i"]))
    red.update(lam_re=d_lre, lam_im=d_lim, log_dt=d_ldt, ssm_b_re=d_bre, ssm_b_im=d_bim)

    def pack(d):
        t = jnp.concatenate([d[n].reshape(-1) for n in small])
        r_ = -(-t.shape[0] // (LANES * 16)) * 16
        return jnp.pad(t, (0, r_ * LANES - t.shape[0])).reshape(r_, LANES)

    sw, sg_, sm, sv = pack(weights), pack(red), pack(mom_m), pack(mom_v)
    sd, snm, snv = _blocked(lambda w_, g_, m_, v_: _adamw(w_, g_, m_, v_), [sw, sg_, sm, sv],
                            [(sw.shape, F32)] * 3, name="adamw_small")
    finish(group_c, sent_c, snv, "c")
    off = 0
    for n in small:
        sz = weights[n].size
        shp = weights[n].shape
        out_g[n] = red[n].reshape(shp)
        out_d[n] = sd.reshape(-1)[off:off + sz].reshape(shp)
        out_m[n] = snm.reshape(-1)[off:off + sz].reshape(shp)
        out_v[n] = snv.reshape(-1)[off:off + sz].reshape(shp)
        off += sz

    return (loss, grad_x[None], *[out_g[n] for n in order], *[out_d[n] for n in order],
            *[out_m[n] for n in order], *[out_v[n] for n in order])
```

```python
import functools
import math

import jax
import jax.numpy as jnp
from jax import lax
from jax.experimental import pallas as pl
from jax.experimental.pallas import tpu as pltpu

F32 = jnp.float32
BF16 = jnp.bfloat16
MESH = pl.DeviceIdType.MESH

N_DEV = 8
LANES = 128
SUBLANES = 8
VMEM_LIMIT = 48 * 1024 * 1024
VMEM_LIMIT_SCAN = 60 * 1024 * 1024

HEAD_DIM = 128
BLK = 128
DILATIONS = (1, 4, 16)
SSM_GROUP = 16
SSM_STATE = 64
SLAB_GROUPS = LANES // SSM_GROUP
SLAB_STATES = SLAB_GROUPS * SSM_STATE
SEGMENTS = SUBLANES
SCAN_UNROLL = 4
RMS_EPS = 1e-6
NEG_INF = -1e30

ADAM_LR = 0.001
ADAM_B1 = 0.9
ADAM_B2 = 0.999
ADAM_EPS = 1e-08
ADAM_WD = 0.01
ADAM_STEP = 10
UPDATE_BLOCK = 256 * 1024


def _tile(n, pref, unit=LANES):
    if n <= pref:
        return n
    t = (pref // unit) * unit
    while t > unit and n % t:
        t -= unit
    assert n % t == 0, (n, pref, unit)
    return t


def _params(sem=None, vmem=VMEM_LIMIT):
    return pltpu.CompilerParams(dimension_semantics=sem, vmem_limit_bytes=vmem)


_NN = (((1,), (0,)), ((), ()))
_NT = (((1,), (1,)), ((), ()))
_TN = (((0,), (0,)), ((), ()))


_ANY = pl.BlockSpec(memory_space=pl.ANY)


def _mm_call(dims, nk, na, nb, pick, n_extra, n_dep, n_out, epi, group=1, **kw):
    first_extra = na + nb
    first_out = first_extra + n_extra + n_dep
    kw["in_specs"] = list(kw["in_specs"]) + [_ANY] * n_dep

    def grouped(refs, step):
        if group == 1:
            return lax.dot_general(refs[0][...], refs[1][...], dims, preferred_element_type=F32)
        kp = refs[0].shape[1]
        return sum(lax.dot_general(refs[step * group + p][...], refs[na][pl.ds(p * kp, kp), :], dims,
                                   preferred_element_type=F32) for p in range(group))

    def single(*refs):
        extra = refs[first_extra:first_extra + n_extra]
        res = epi(grouped(refs, 0), *[e[...] for e in extra])
        for o, r in zip(refs[first_out:first_out + n_out], res):
            o[...] = r.astype(o.dtype)

    if nk == 1:
        assert na == group and nb == 1
        kw["scratch_shapes"] = []
        return pl.pallas_call(single, **kw)

    def body(*refs):
        extra = refs[first_extra:first_extra + n_extra]
        outs = refs[first_out:first_out + n_out]
        acc = refs[-1]
        k = pl.program_id(2)

        @pl.when(k == 0)
        def _():
            acc[...] = jnp.zeros_like(acc)

        def add(a_ref, b_ref):
            acc[...] += lax.dot_general(a_ref[...], b_ref[...], dims, preferred_element_type=F32)

        if na == nb == 1:
            add(refs[0], refs[1])
        elif group > 1:
            for step in range(nk):
                @pl.when(k == step)
                def _(step=step):
                    acc[...] += grouped(refs, step)
        else:
            pa, pb = pick(pl.program_id(0), pl.program_id(1), k)
            for x in range(na):
                for y in range(nb):
                    pl.when((pa == x) & (pb == y))(functools.partial(add, refs[x], refs[na + y]))

        @pl.when(k == nk - 1)
        def _():
            res = epi(acc[...], *[e[...] for e in extra])
            for o, r in zip(outs, res):
                o[...] = r.astype(o.dtype)

    return pl.pallas_call(body, **kw)


def _identity_epi(acc):
    return (acc,)


def _parts(t):
    return list(t) if isinstance(t, (list, tuple)) else [t]


def _part_spec(block, part, which, index):
    def index_map(i, j, k):
        use = which(i, j, k) == part
        r, c = index(i, j, k)
        return jnp.where(use, r, 0), jnp.where(use, c, 0)
    return pl.BlockSpec(block, index_map)


def mm_nn(a, w, out_dtypes, *, name, epi=_identity_epi, bias=None, deps=(), tm=2048, tn=512, tk=2048):
    a = _parts(a)
    M, Kp = a[0].shape
    K = Kp * len(a)
    J, K2, n = w.shape
    assert K == K2
    tm, tn, tk = _tile(M, tm, 16), _tile(n, tn), _tile(K, tk)
    npj = n // tn
    nk = K // tk
    group = 1
    if len(a) == 1:
        in_specs = [pl.BlockSpec((tm, tk), lambda i, j, k: (i, k))]
    else:
        assert tk % Kp == 0
        group = tk // Kp
        in_specs = [pl.BlockSpec((tm, Kp), lambda i, j, k: (i, 0)) for _ in a]
    in_specs.append(pl.BlockSpec((None, tk, tn), lambda i, j, k: (j // npj, k, j % npj)))
    args = a + [w]
    if bias is not None:
        in_specs.append(pl.BlockSpec((1, tn), lambda i, j, k: (0, j)))
        args.append(bias)
    return _mm_call(
        _NN, nk, len(a), 1, lambda i, j, k: (k, 0), len(args) - len(a) - 1, len(deps), len(out_dtypes), epi, group,
        out_shape=[jax.ShapeDtypeStruct((M, J * n), d) for d in out_dtypes],
        grid=(M // tm, J * npj, nk), in_specs=in_specs,
        out_specs=[pl.BlockSpec((tm, tn), lambda i, j, k: (i, j)) for _ in out_dtypes],
        scratch_shapes=[pltpu.VMEM((tm, tn), F32)],
        compiler_params=_params(("parallel", "parallel", "arbitrary")), name=name)(*args, *deps)


def mm_nt(a, w, out_dtype, *, name, epi=_identity_epi, extra=None, tm=2048, tko=512, tnr=2048):
    M, N = a.shape
    J, K, n = w.shape
    assert N == J * n
    tm, tko, tnr = _tile(M, tm, 16), _tile(K, tko), _tile(n, tnr)
    npj = n // tnr
    nk = N // tnr
    in_specs = [pl.BlockSpec((tm, tnr), lambda i, j, k: (i, k)),
                pl.BlockSpec((None, tko, tnr), lambda i, j, k: (k // npj, j, k % npj))]
    args = [a, w]
    if extra is not None:
        in_specs.append(pl.BlockSpec((tm, tko), lambda i, j, k: (i, j)))
        args.append(extra)
    return _mm_call(
        _NT, nk, 1, 1, None, len(args) - 2, 0, 1, epi,
        out_shape=[jax.ShapeDtypeStruct((M, K), out_dtype)],
        grid=(M // tm, K // tko, nk), in_specs=in_specs,
        out_specs=[pl.BlockSpec((tm, tko), lambda i, j, k: (i, j))],
        scratch_shapes=[pltpu.VMEM((tm, tko), F32)],
        compiler_params=_params(("parallel", "parallel", "arbitrary")), name=name)(*args)[0]


def mm_tn(a, b, J, *, name, deps=(), tko=1024, tn=1024, ts=2048):
    a, b = _parts(a), _parts(b)
    S, Kp = a[0].shape
    S2, Np = b[0].shape
    K, N = Kp * len(a), Np * len(b)
    assert S == S2 and N % J == 0
    n = N // J
    tko, tn, ts = _tile(Kp, tko), _tile(math.gcd(n, Np), tn), _tile(S, ts)
    npj = n // tn
    nk = S // ts
    ta, tb = Kp // tko, Np // tn
    assert nk > 1 or len(a) == len(b) == 1
    return _mm_call(
        _TN, nk, len(a), len(b), lambda i, j, k: (i // ta, j // tb), 0, len(deps), 1, _identity_epi,
        out_shape=[jax.ShapeDtypeStruct((J, K, n), BF16)],
        grid=(K // tko, J * npj, nk),
        in_specs=([_part_spec((ts, tko), x, lambda i, j, k: i // ta, lambda i, j, k: (k, i % ta)) for x in range(len(a))]
                  + [_part_spec((ts, tn), y, lambda i, j, k: j // tb, lambda i, j, k: (k, j % tb)) for y in range(len(b))]),
        out_specs=[pl.BlockSpec((None, tko, tn), lambda i, j, k: (j // npj, i, j % npj))],
        scratch_shapes=[pltpu.VMEM((tko, tn), F32)],
        compiler_params=_params(("parallel", "parallel", "arbitrary")), name=name)(*a, *b, *deps)[0]


def rowwise(fn, rows, vecs, outs, accs=(), *, name, deps=(), ts=256):
    rows = [r if isinstance(r, tuple) else (r, r.shape[1], 0) for r in rows]
    S = rows[0][0].shape[0]
    ts = _tile(S, ts, 16)
    nr, nv, no, nd = len(rows), len(vecs), len(outs), len(deps)

    def body(*refs):
        r, v = refs[:nr], refs[nr:nr + nv]
        o, a = refs[nr + nv + nd:nr + nv + nd + no], refs[nr + nv + nd + no:]
        res = fn(*[t[...].astype(F32) for t in r], *[t[...] for t in v])
        for ref, val in zip(o, res[:no]):
            ref[...] = val.astype(ref.dtype)
        if a:
            @pl.when(pl.program_id(0) == 0)
            def _():
                for ref in a:
                    ref[...] = jnp.zeros_like(ref)

            for ref, val in zip(a, res[no:]):
                ref[...] += val

    in_specs = [pl.BlockSpec((ts, w), functools.partial(lambda i, cb: (i, cb), cb=cb)) for _, w, cb in rows]
    in_specs += [pl.BlockSpec(v.shape, lambda i: (0, 0)) for v in vecs] + [_ANY] * nd
    out_shape = [jax.ShapeDtypeStruct((S, w), d) for w, d in outs]
    out_shape += [jax.ShapeDtypeStruct((1, w), F32) for w in accs]
    out_specs = [pl.BlockSpec((ts, w), lambda i: (i, 0)) for w, _ in outs]
    out_specs += [pl.BlockSpec((1, w), lambda i: (0, 0)) for w in accs]
    return pl.pallas_call(body, out_shape=out_shape, grid=(S // ts,), in_specs=in_specs, out_specs=out_specs,
                          compiler_params=_params(("arbitrary",)), name=name)(*[r[0] for r in rows], *vecs, *deps)


def _rms(x, g):
    r = lax.rsqrt(jnp.mean(x * x, axis=-1, keepdims=True) + RMS_EPS)
    return x * r * g


def _rms_bwd(dy, x, g):
    r = lax.rsqrt(jnp.mean(x * x, axis=-1, keepdims=True) + RMS_EPS)
    xh = x * r
    dxh = dy * g
    dx = r * (dxh - xh * jnp.mean(dxh * xh, axis=-1, keepdims=True))
    return dx, jnp.sum(dy * xh, axis=0, keepdims=True)


def _sigmoid(x):
    return pl.reciprocal(1.0 + jnp.exp(-x), approx=True)


_GELU_C = math.sqrt(2.0 / math.pi)


def _gelu(x):
    return 0.5 * x * (1.0 + jnp.tanh(_GELU_C * (x + 0.044715 * x * x * x)))


def _gelu_grad(x):
    t = jnp.tanh(_GELU_C * (x + 0.044715 * x * x * x))
    return 0.5 * (1.0 + t) + 0.5 * x * (1.0 - t * t) * _GELU_C * (1.0 + 3.0 * 0.044715 * x * x)


ATTN_INTERLEAVE = 8
KEY_PAD = BLK * max(DILATIONS)


def _key_mask(n):
    ii = lax.broadcasted_iota(jnp.int32, (BLK, 2 * BLK), 0)
    jj = lax.broadcasted_iota(jnp.int32, (BLK, 2 * BLK), 1)
    return ((jj < BLK) & (jj >= ii) & (n > 0)) | ((jj >= BLK) & (jj - BLK <= ii))


def _units(d, nblk):
    nb = nblk // d
    if nb == 2:
        def unit(idx):
            ii = lax.broadcasted_iota(jnp.int32, (2 * BLK, 2 * BLK), 0)
            jj = lax.broadcasted_iota(jnp.int32, (2 * BLK, 2 * BLK), 1)
            return pl.ds(idx, 2 * BLK, stride=d), pl.ds(KEY_PAD + idx, 2 * BLK, stride=d), (jj <= ii) & (ii - jj <= BLK)
        return d, max(1, ATTN_INTERLEAVE // 4), unit

    def unit(idx):
        r, n = idx // nb, idx % nb
        cur = r + n * (BLK * d)
        keys = cur + (KEY_PAD - BLK * d)
        if d == 1:
            return pl.ds(pl.multiple_of(cur, BLK), BLK), pl.ds(pl.multiple_of(keys, BLK), 2 * BLK), _key_mask(n)
        return pl.ds(cur, BLK, stride=d), pl.ds(keys, 2 * BLK, stride=d), _key_mask(n)
    return nblk, ATTN_INTERLEAVE, unit


def _pad_keys(dst, src):
    dst[pl.ds(0, KEY_PAD), :] = jnp.zeros((KEY_PAD, dst.shape[1]), F32)

    def copy(c, carry):
        dst[pl.ds(pl.multiple_of(KEY_PAD + c * BLK, BLK), BLK), :] = src[pl.ds(pl.multiple_of(c * BLK, BLK), BLK), :]
        return carry

    lax.fori_loop(0, src.shape[0] // BLK, copy, 0)


def attn_fwd(proj, n_heads, *, name):
    S, WP = proj.shape
    assert S % (BLK * max(DILATIONS)) == 0
    nblk = S // BLK
    AW = n_heads * HEAD_DIM
    scale = 1.0 / math.sqrt(HEAD_DIM)

    def body(q_ref, k_ref, v_ref, o_ref, l_ref, acc, mrun, lrun, kp, vp):
        _pad_keys(kp, k_ref)
        _pad_keys(vp, v_ref)
        for first, d in zip((True, False, False), reversed(DILATIONS)):
            n_units, per_step, unit = _units(d, nblk)

            def step(it, carry, first=first, n_units=n_units, per_step=per_step, unit=unit):
                units = [unit(it + j * (n_units // per_step)) for j in range(per_step)]
                ss = [lax.dot_general(q_ref[cur, :].astype(BF16), kp[keys, :].astype(BF16), _NT,
                                      preferred_element_type=F32) * scale for cur, keys, _ in units]
                ss = [jnp.where(mask, s, NEG_INF) for s, (_, _, mask) in zip(ss, units)]
                ms = [jnp.max(s, axis=-1, keepdims=True) for s in ss]
                ps = [jnp.exp(s - m) for s, m in zip(ss, ms)]
                ls = [jnp.sum(p, axis=-1, keepdims=True) for p in ps]
                os_ = [jnp.dot(p.astype(BF16), vp[keys, :].astype(BF16), preferred_element_type=F32)
                       for p, (_, keys, _) in zip(ps, units)]
                for (cur, keys, mask), m, l, o in zip(units, ms, ls, os_):
                    m = jnp.broadcast_to(m, o.shape)
                    l = jnp.broadcast_to(l, o.shape)
                    if first:
                        acc[cur, :], mrun[cur, :], lrun[cur, :] = o, m, l
                    else:
                        m_old = mrun[cur, :]
                        m_new = jnp.maximum(m_old, m)
                        w_old, w_blk = jnp.exp(m_old - m_new), jnp.exp(m - m_new)
                        acc[cur, :] = w_old * acc[cur, :] + w_blk * o
                        lrun[cur, :] = w_old * lrun[cur, :] + w_blk * l
                        mrun[cur, :] = m_new
                return carry

            lax.fori_loop(0, n_units // per_step, step, 0)

        def finish(c, carry):
            r = pl.ds(pl.multiple_of(c * BLK, BLK), BLK)
            o_ref[r, :] = acc[r, :] / lrun[r, :]
            l_ref[r, :] = mrun[r, :] + jnp.log(lrun[r, :])
            return carry

        lax.fori_loop(0, nblk, finish, 0)

    def col(off):
        return pl.BlockSpec((S, HEAD_DIM), lambda h: (0, off + h))

    ospec = pl.BlockSpec((S, HEAD_DIM), lambda h: (0, h))
    return pl.pallas_call(
        body, out_shape=[jax.ShapeDtypeStruct((S, AW), F32)] * 2, grid=(n_heads,),
        in_specs=[col(0), col(n_heads), col(2 * n_heads)], out_specs=[ospec, ospec],
        scratch_shapes=[pltpu.VMEM((S, HEAD_DIM), F32)] * 3 + [pltpu.VMEM((KEY_PAD + S, HEAD_DIM), F32)] * 2,
        compiler_params=_params(("parallel",)), name=name)(proj, proj, proj)


def attn_bwd(proj, do, stats, n_heads, *, name):
    S, WP = proj.shape
    nblk = S // BLK
    AW = n_heads * HEAD_DIM
    scale = 1.0 / math.sqrt(HEAD_DIM)

    def body(q_ref, k_ref, v_ref, do_ref, st_ref, dq_ref, dk_ref, dv_ref, dq_sc, dk_sc, dv_sc, kp, vp):
        _pad_keys(kp, k_ref)
        _pad_keys(vp, v_ref)
        order = list(reversed(DILATIONS))
        assign_first = nblk // order[0] == 2
        if assign_first:
            dk_sc[pl.ds(0, KEY_PAD), :] = jnp.zeros((KEY_PAD, HEAD_DIM), F32)
            dv_sc[pl.ds(0, KEY_PAD), :] = jnp.zeros((KEY_PAD, HEAD_DIM), F32)
        else:
            dq_sc[...] = jnp.zeros_like(dq_sc)
            dk_sc[...] = jnp.zeros_like(dk_sc)
            dv_sc[...] = jnp.zeros_like(dv_sc)
        for assign, d in zip((assign_first, False, False), order):
            n_units, per_step, unit = _units(d, nblk)

            def step(it, carry, n_units=n_units, per_step=per_step, unit=unit, assign=assign):
                units = [unit(it + j * (n_units // per_step)) for j in range(per_step)]
                qs = [q_ref[cur, :].astype(BF16) for cur, _, _ in units]
                gs = [do_ref[cur, :].astype(BF16) for cur, _, _ in units]
                ks = [kp[keys, :].astype(BF16) for _, keys, _ in units]
                ss = [lax.dot_general(q, kb, _NT, preferred_element_type=F32) * scale for q, kb in zip(qs, ks)]
                dps = [lax.dot_general(g, vp[keys, :].astype(BF16), _NT, preferred_element_type=F32)
                       for g, (_, keys, _) in zip(gs, units)]
                sts = [st_ref[cur, :] for cur, _, _ in units]
                ps = [jnp.where(mask, jnp.exp(s - st[:, :1]), 0.0) for s, st, (_, _, mask) in zip(ss, sts, units)]
                dss = [(p * (dp - st[:, HEAD_DIM // 2:HEAD_DIM // 2 + 1]) * scale).astype(BF16)
                       for p, dp, st in zip(ps, dps, sts)]
                for (cur, keys, _), q, g, kb, p, ds in zip(units, qs, gs, ks, ps, dss):
                    dq = jnp.dot(ds, kb, preferred_element_type=F32)
                    dk = lax.dot_general(ds, q, _TN, preferred_element_type=F32)
                    dv = lax.dot_general(p.astype(BF16), g, _TN, preferred_element_type=F32)
                    if assign:
                        dq_sc[cur, :], dk_sc[keys, :], dv_sc[keys, :] = dq, dk, dv
                    else:
                        dq_sc[cur, :] += dq
                        dk_sc[keys, :] += dk
                        dv_sc[keys, :] += dv
                return carry

            lax.fori_loop(0, n_units // per_step, step, 0)
        rows = pl.ds(KEY_PAD, S)
        dq_ref[...] = dq_sc[...].astype(BF16)
        dk_ref[...] = dk_sc[rows, :].astype(BF16)
        dv_ref[...] = dv_sc[rows, :].astype(BF16)

    def col(off):
        return pl.BlockSpec((S, HEAD_DIM), lambda h: (0, off + h))

    ospec = pl.BlockSpec((S, HEAD_DIM), lambda h: (0, h))
    return pl.pallas_call(
        body, out_shape=[jax.ShapeDtypeStruct((S, AW), BF16)] * 3, grid=(n_heads,),
        in_specs=[col(0), col(n_heads), col(2 * n_heads), ospec, ospec], out_specs=[ospec] * 3,
        scratch_shapes=[pltpu.VMEM((S, HEAD_DIM), F32)] + [pltpu.VMEM((KEY_PAD + S, HEAD_DIM), F32)] * 4,
        compiler_params=_params(("parallel",), VMEM_LIMIT_SCAN), name=name)(proj, proj, proj, do, stats)


def _to_segments(t):
    S, W = t.shape
    return t.reshape(SEGMENTS, S // SEGMENTS, W).swapaxes(0, 1).reshape(S, W)


def _from_segments(t):
    S, W = t.shape
    return t.reshape(S // SEGMENTS, SEGMENTS, W).swapaxes(0, 1).reshape(S, W)


def _cmul(ar, ai, br, bi):
    return ar * br - ai * bi, ar * bi + ai * br


def _power(ar, ai, log2n):
    for _ in range(log2n):
        ar, ai = _cmul(ar, ai, ar, ai)
    return ar, ai


def _shift_rows(x, up):
    row = lax.broadcasted_iota(jnp.int32, x.shape, 0)
    if up:
        return jnp.where(row == SEGMENTS - 1, 0.0, pltpu.roll(x, SEGMENTS - 1, 0))
    return jnp.where(row == 0, 0.0, pltpu.roll(x, 1, 0))


def _segment_carries(er, ei, pr, pi, up):
    cr = jnp.zeros_like(er)
    ci = jnp.zeros_like(ei)
    for _ in range(SEGMENTS - 1):
        tr, ti = _cmul(pr, pi, cr, ci)
        cr, ci = _shift_rows(er + tr, up), _shift_rows(ei + ti, up)
    return cr, ci


def _scan_states(sr, si, ar, ai, T, reverse):
    ns = sr.shape[1]
    ar8 = jnp.broadcast_to(ar, (SEGMENTS, ns))
    ai8 = jnp.broadcast_to(ai, (SEGMENTS, ns))

    def rows(t):
        k = (T - 1 - t) if reverse else t
        return pl.ds(pl.multiple_of(k * SEGMENTS, SEGMENTS), SEGMENTS)

    def advance(t, c):
        tr, ti = _cmul(ar8, ai8, c[0], c[1])
        return tr + sr[rows(t), :], ti + si[rows(t), :]

    def several(step):
        def trip(t, c):
            for j in range(SCAN_UNROLL):
                c = step(t * SCAN_UNROLL + j, c)
            return c
        return trip

    zero = jnp.zeros((SEGMENTS, ns), F32)
    er, ei = lax.fori_loop(0, T // SCAN_UNROLL, several(advance), (zero, zero))
    pr, pi = _power(ar, ai, T.bit_length() - 1)
    cr, ci = _segment_carries(er, ei, jnp.broadcast_to(pr, (SEGMENTS, ns)), jnp.broadcast_to(pi, (SEGMENTS, ns)), reverse)

    def store(t, c):
        nr, ni = advance(t, c)
        sr[rows(t), :] = nr
        si[rows(t), :] = ni
        return nr, ni

    lax.fori_loop(0, T // SCAN_UNROLL, several(store), (cr, ci))
    return cr, ci


def _slab_specs(ns):
    return [pl.BlockSpec((None, LANES, ns), lambda g: (g, 0, 0)),
            pl.BlockSpec((None, LANES, ns), lambda g: (g, 0, 0)),
            pl.BlockSpec((None, 1, ns), lambda g: (g, 0, 0)),
            pl.BlockSpec((None, 1, ns), lambda g: (g, 0, 0)),
            pl.BlockSpec((None, ns, LANES), lambda g: (g, 0, 0)),
            pl.BlockSpec((None, ns, LANES), lambda g: (g, 0, 0)),
            pl.BlockSpec((1, LANES), lambda g: (0, g))]


def _chunks(S):
    rc = _tile(S, 512, 16)
    return rc, S // rc


def ssm_fwd(u, bbr, bbi, ar, ai, cbr, cbi, dsk, *, name):
    S, SW = u.shape
    nslab, _, ns = bbr.shape
    T = S // SEGMENTS
    assert T & (T - 1) == 0
    rc, nc = _chunks(S)

    def body(u_ref, br_ref, bi_ref, ar_ref, ai_ref, cr_ref, ci_ref, d_ref, y_ref, yg_ref, str_ref, sti_ref, sr, si):
        def inputs(c, carry):
            r = pl.ds(pl.multiple_of(c * rc, rc), rc)
            sr[r, :] = jnp.dot(u_ref[r, :], br_ref[...], preferred_element_type=F32)
            si[r, :] = jnp.dot(u_ref[r, :], bi_ref[...], preferred_element_type=F32)
            return carry

        lax.fori_loop(0, nc, inputs, 0)
        _scan_states(sr, si, ar_ref[...], ai_ref[...], T, False)

        def outputs(c, carry):
            r = pl.ds(pl.multiple_of(c * rc, rc), rc)
            srb, sib = sr[r, :].astype(BF16), si[r, :].astype(BF16)
            str_ref[r, :] = srb
            sti_ref[r, :] = sib
            y = (jnp.dot(srb, cr_ref[...], preferred_element_type=F32)
                 - jnp.dot(sib, ci_ref[...], preferred_element_type=F32) + d_ref[...] * u_ref[r, :].astype(F32))
            y_ref[r, :] = y
            yg_ref[r, :] = _gelu(y).astype(BF16)
            return carry

        lax.fori_loop(0, nc, outputs, 0)

    slab = pl.BlockSpec((S, LANES), lambda g: (0, g))
    states = pl.BlockSpec((S, ns), lambda g: (0, g))
    return pl.pallas_call(
        body, out_shape=([jax.ShapeDtypeStruct((S, SW), F32), jax.ShapeDtypeStruct((S, SW), BF16)]
                         + [jax.ShapeDtypeStruct((S, nslab * ns), BF16)] * 2),
        grid=(nslab,), in_specs=[slab] + _slab_specs(ns), out_specs=[slab, slab, states, states],
        scratch_shapes=[pltpu.VMEM((S, ns), F32)] * 2,
        compiler_params=_params(("parallel",), VMEM_LIMIT_SCAN), name=name)(u, bbr, bbi, ar, ai, cbr, cbi, dsk)


def ssm_bwd(u, d_direct, d_gate, y, st_r, st_i, bbr, bbi, ar, ai, cbr, cbi, dsk, *, name, deps=()):
    S, SW = u.shape
    nslab, _, ns = bbr.shape
    T = S // SEGMENTS
    rc, nc = _chunks(S)
    pair_rows = 2 * SEGMENTS

    def body(*refs):
        (u_ref, d1_ref, d2_ref, y_ref, sr_ref, si_ref, br_ref, bi_ref, ar_ref, ai_ref, cr_ref, ci_ref,
         d_ref) = refs[:13]
        (du_ref, dbr_ref, dbi_ref, dcr_ref, dci_ref, dar_ref, dai_ref, dd_ref, lr, li,
         dy_ref) = refs[13 + len(deps):]

        def inputs(c, skip):
            r = pl.ds(pl.multiple_of(c * rc, rc), rc)
            dy = (d1_ref[r, :] + d2_ref[r, :].astype(F32)) * _gelu_grad(y_ref[r, :])
            dy_ref[r, :] = dy
            gb = dy.astype(BF16)
            lr[r, :] = lax.dot_general(gb, cr_ref[...], _NT, preferred_element_type=F32)
            li[r, :] = -lax.dot_general(gb, ci_ref[...], _NT, preferred_element_type=F32)
            return skip + jnp.sum(dy * u_ref[r, :].astype(F32), axis=0, keepdims=True)

        dd_ref[...] = lax.fori_loop(0, nc, inputs, jnp.zeros((1, LANES), F32))
        _scan_states(lr, li, ar_ref[...], -ai_ref[...], T, True)

        def steps(j):
            rows = pl.ds(pl.multiple_of(j * pair_rows, pair_rows), pair_rows)
            tr, ti = sr_ref[rows, :].astype(F32), si_ref[rows, :].astype(F32)
            return tr[:SEGMENTS], tr[SEGMENTS:], ti[:SEGMENTS], ti[SEGMENTS:]

        def pair(j, c):
            acc_r, acc_i, pr, pi = c
            lo_r, hi_r, lo_i, hi_i = steps(j)
            first = pl.ds(pl.multiple_of(j * pair_rows, SEGMENTS), SEGMENTS)
            second = pl.ds(pl.multiple_of(j * pair_rows + SEGMENTS, SEGMENTS), SEGMENTS)
            la_r, la_i, lb_r, lb_i = lr[first, :], li[first, :], lr[second, :], li[second, :]
            return (acc_r + la_r * pr + la_i * pi + lb_r * lo_r + lb_i * lo_i,
                    acc_i - la_r * pi + la_i * pr - lb_r * lo_i + lb_i * lo_r, hi_r, hi_i)

        def pairs(t, c):
            return pair(2 * t + 1, pair(2 * t, c))

        _, end_r, _, end_i = steps(T // 2 - 1)
        zero = jnp.zeros((SEGMENTS, ns), F32)
        acc = lax.fori_loop(0, T // 4, pairs, (zero, zero, _shift_rows(end_r, False), _shift_rows(end_i, False)))
        dar_ref[...] = jnp.sum(acc[0], axis=0, keepdims=True)
        dai_ref[...] = jnp.sum(acc[1], axis=0, keepdims=True)

        dbr_ref[...] = jnp.zeros_like(dbr_ref)
        dbi_ref[...] = jnp.zeros_like(dbi_ref)
        dcr_ref[...] = jnp.zeros_like(dcr_ref)
        dci_ref[...] = jnp.zeros_like(dci_ref)

        def outputs(c, carry):
            r = pl.ds(pl.multiple_of(c * rc, rc), rc)
            ub = u_ref[r, :]
            g = dy_ref[r, :]
            gb = g.astype(BF16)
            lrb = lr[r, :].astype(BF16)
            lib = li[r, :].astype(BF16)
            du_ref[r, :] = (lax.dot_general(lrb, br_ref[...], _NT, preferred_element_type=F32)
                            + lax.dot_general(lib, bi_ref[...], _NT, preferred_element_type=F32)
                            + d_ref[...] * g).astype(BF16)
            dbr_ref[...] += lax.dot_general(ub, lrb, _TN, preferred_element_type=F32)
            dbi_ref[...] += lax.dot_general(ub, lib, _TN, preferred_element_type=F32)
            dcr_ref[...] += lax.dot_general(sr_ref[r, :], gb, _TN, preferred_element_type=F32)
            dci_ref[...] -= lax.dot_general(si_ref[r, :], gb, _TN, preferred_element_type=F32)
            return carry

        lax.fori_loop(0, nc, outputs, 0)

    slab = pl.BlockSpec((S, LANES), lambda g: (0, g))
    states = pl.BlockSpec((S, ns), lambda g: (0, g))
    bspec = pl.BlockSpec((None, LANES, ns), lambda g: (g, 0, 0))
    cspec = pl.BlockSpec((None, ns, LANES), lambda g: (g, 0, 0))
    aspec = pl.BlockSpec((None, 1, ns), lambda g: (g, 0, 0))
    return pl.pallas_call(
        body,
        out_shape=[jax.ShapeDtypeStruct((S, SW), BF16),
                   jax.ShapeDtypeStruct((nslab, LANES, ns), F32), jax.ShapeDtypeStruct((nslab, LANES, ns), F32),
                   jax.ShapeDtypeStruct((nslab, ns, LANES), F32), jax.ShapeDtypeStruct((nslab, ns, LANES), F32),
                   jax.ShapeDtypeStruct((nslab, 1, ns), F32), jax.ShapeDtypeStruct((nslab, 1, ns), F32),
                   jax.ShapeDtypeStruct((1, SW), F32)],
        grid=(nslab,), in_specs=[slab, slab, slab, slab, states, states] + _slab_specs(ns) + [_ANY] * len(deps),
        out_specs=[slab, bspec, bspec, cspec, cspec, aspec, aspec, pl.BlockSpec((1, LANES), lambda g: (0, g))],
        scratch_shapes=[pltpu.VMEM((S, ns), F32)] * 2 + [pltpu.VMEM((S, LANES), F32)],
        compiler_params=_params(("parallel",), VMEM_LIMIT_SCAN), name=name)(
            u, d_direct, d_gate, y, st_r, st_i, bbr, bbi, ar, ai, cbr, cbi, dsk, *deps)


def _discretise(lam_re, lam_im, log_dt, b_re, b_im):
    dt = jnp.exp(log_dt)[:, None]
    mag = jnp.exp(lam_re * dt)
    ar = mag * jnp.cos(lam_im * dt)
    ai = mag * jnp.sin(lam_im * dt)
    nr, ni = ar - 1.0, ai
    den = lam_re * lam_re + lam_im * lam_im
    cr = ((nr * lam_re + ni * lam_im) / den)[..., None]
    ci = ((ni * lam_re - nr * lam_im) / den)[..., None]
    return ar, ai, cr * b_re - ci * b_im, cr * b_im + ci * b_re


def _block_diag(t, nslab):
    G, R, C = t.shape
    eye = jnp.eye(SLAB_GROUPS, dtype=t.dtype)
    t = t.reshape(nslab, SLAB_GROUPS, R, C)
    return jnp.einsum('sgrc,gh->sgrhc', t, eye).reshape(nslab, SLAB_GROUPS * R, SLAB_GROUPS * C)


def _block_diag_part(t, R, C):
    nslab = t.shape[0]
    eye = jnp.eye(SLAB_GROUPS, dtype=t.dtype)
    t = t.reshape(nslab, SLAB_GROUPS, R, SLAB_GROUPS, C)
    return jnp.einsum('sgrhc,gh->sgrc', t, eye).reshape(nslab * SLAB_GROUPS, R, C)


def _place():
    return lax.axis_index("x"), lax.axis_index("y"), lax.axis_index("c")


_HBM = pl.BlockSpec(memory_space=pltpu.HBM)
_SEM = pl.BlockSpec(memory_space=pltpu.SEMAPHORE)
_ORDERED_EFFECT = pltpu.SideEffectType.DATAFLOW_SIDE_EFFECTING


def _split_call(name, srcs, zones, sems_in, n_new, body_fn, after):
    nsrc, nz, ns, nn = len(srcs), len(zones), len(sems_in), len(n_new)
    nb = nsrc + nz

    def body(*refs):
        outs = refs[nb + ns + 1:]
        body_fn(refs[:nb], refs[nb:nb + ns], outs[:nn])
        outs[nn + nz][...] = jnp.zeros((SUBLANES, LANES), F32)

    res = pl.pallas_call(
        body, name=name,
        out_shape=([pltpu.SemaphoreType.DMA((n,)) for n in n_new] + [pltpu.HBM(b.shape, b.dtype) for b in zones]
                   + [jax.ShapeDtypeStruct((SUBLANES, LANES), F32)]),
        in_specs=[_HBM] * nb + [_SEM] * ns + [_ANY],
        out_specs=[_SEM] * nn + [_HBM] * nz + [pl.BlockSpec(memory_space=pltpu.VMEM)],
        input_output_aliases={nsrc + i: nn + i for i in range(nz)},
        compiler_params=pltpu.CompilerParams(has_side_effects=_ORDERED_EFFECT))(
            *[pltpu.with_memory_space_constraint(b, pltpu.HBM) for b in list(srcs) + list(zones)], *sems_in, after)
    return list(res[:nn]), list(res[nn:nn + nz]), res[-1]


def _mesh_peers():
    x, y, c = _place()
    return x, y, c, (x, y, 1 - c), [(1 - x, y), (x, 1 - y), (1 - x, 1 - y)]


def gather_start(shards, after, *, name):
    nw = len(shards)
    x, y, c = _place()
    zones = [lax.dynamic_update_slice(lax.empty((N_DEV,) + s.shape, s.dtype), s[None], (4 * x + 2 * y + c, 0, 0))
             for s in shards]

    def body(bufs, taken, new):
        for cp in _gather_first(bufs, nw, new[0], new[1]):
            cp.start()

    sems, zones, token = _split_call(name, shards, zones, [], [4 * nw, 4 * nw], body, after)
    return shards, sems, zones, token


def _gather_first(bufs, nw, send, recv):
    x, y, c, sibling, chips = _mesh_peers()
    out = []
    for w in range(nw):
        slot = bufs[nw + w].at[4 * x + 2 * y + c]
        for k, to in enumerate([sibling] + [(*ch, c) for ch in chips]):
            out.append(pltpu.make_async_remote_copy(
                src_ref=bufs[w], dst_ref=slot, send_sem=send.at[4 * w + k], recv_sem=recv.at[4 * w + k],
                device_id=to, device_id_type=MESH))
    return out


def _gather_slot_copy(bufs, nw, w, block, send_sem, recv_sem, to):
    px, py, pc = block
    slot = bufs[nw + w].at[4 * px + 2 * py + pc]
    return pltpu.make_async_remote_copy(src_ref=slot, dst_ref=slot, send_sem=send_sem, recv_sem=recv_sem,
                                        device_id=to, device_id_type=MESH)


def gather_forward(state, after, *, name):
    shards, sems, zones, _ = state
    nw = len(shards)

    def body(bufs, taken, new):
        x, y, c, sibling, chips = _mesh_peers()
        for j, ch in enumerate(chips):
            for w in range(nw):
                k = 4 * w + 1 + j
                _gather_slot_copy(bufs, nw, w, (*ch, c), taken[0].at[k], taken[1].at[k], (*ch, c)).wait_recv()
                _gather_slot_copy(bufs, nw, w, (*ch, c), new[0].at[3 * w + j], new[1].at[3 * w + j], sibling).start()
        for w in range(nw):
            _gather_slot_copy(bufs, nw, w, sibling, taken[0].at[4 * w], taken[1].at[4 * w], sibling).wait_recv()
        for cp in _gather_first(bufs, nw, taken[0], taken[1]):
            cp.wait_send()

    sems, zones, token = _split_call(name, shards, zones, sems, [3 * nw, 3 * nw], body, after)
    return shards, sems, zones, token


def gather_finish(state, after, *, name):
    shards, sems, zones, _ = state
    nw = len(shards)

    def body(bufs, taken, new):
        x, y, c, sibling, chips = _mesh_peers()
        for w in range(nw):
            for j, ch in enumerate(chips):
                cp = _gather_slot_copy(bufs, nw, w, (*ch, 1 - c), taken[0].at[3 * w + j], taken[1].at[3 * w + j], sibling)
                cp.wait_send()
                cp.wait_recv()

    _, zones, _ = _split_call(name, shards, zones, sems, [], body, after)
    return zones


def exchange_start(srcs, zone_shapes, copies, n, after, *, name):
    nw = len(srcs)
    zones = [lax.empty(z, s.dtype) for z, s in zip(zone_shapes, srcs)]

    def body(bufs, taken, new):
        for cp in copies(bufs[:nw], bufs[nw:], new[0], new[1]):
            cp.start()

    sems, zones, token = _split_call(name, srcs, zones, [], [n, n], body, after)
    return srcs, copies, sems, zones, token


def exchange_wait(state, after, *, name):
    srcs, copies, sems, zones, _ = state
    nw = len(srcs)

    def body(bufs, taken, new):
        for cp in copies(bufs[:nw], bufs[nw:], taken[0], taken[1]):
            cp.wait_send()
            cp.wait_recv()

    _, zones, _ = _split_call(name, srcs, zones, sems, [], body, after)
    return zones


def _core_copies(srcs, zones, send, recv):
    x, y, c = _place()
    return [pltpu.make_async_remote_copy(
        src_ref=srcs[w].at[:, 1 - c], dst_ref=zones[w], send_sem=send.at[w], recv_sem=recv.at[w],
        device_id=(x, y, 1 - c), device_id_type=MESH) for w in range(len(srcs))]


def _chip_copies(srcs, zones, send, recv):
    x, y, c = _place()
    chips = [(1 - x, y), (x, 1 - y), (1 - x, 1 - y)]
    return [pltpu.make_async_remote_copy(
        src_ref=srcs[w].at[2 * cx + cy], dst_ref=zones[w].at[j], send_sem=send.at[3 * w + j],
        recv_sem=recv.at[3 * w + j], device_id=(cx, cy, c), device_id_type=MESH)
        for w in range(len(srcs)) for j, (cx, cy) in enumerate(chips)]


def _blocked(fn, ins, outs, *, name, place=None, tr=256):
    k, n = outs[0][0]
    tr = _tile(k, tr, 16)
    if place is None:
        place = jnp.zeros((1,), jnp.int32)
    specs = []
    args = []
    for a in ins:
        if isinstance(a, tuple):
            arr, lead = a
            specs.append(pl.BlockSpec((None, tr, n), functools.partial(lambda i, s, lead: (*lead(i, s), 0), lead=lead)))
            args.append(arr)
        else:
            specs.append(pl.BlockSpec((tr, n), lambda i, s: (i, 0)))
            args.append(a)
    nin = len(args)

    def body(place_ref, *refs):
        res = fn(*[r[...] for r in refs[:nin]])
        for ref, val in zip(refs[nin:], res):
            ref[...] = val.astype(ref.dtype)

    return pl.pallas_call(
        body, out_shape=[jax.ShapeDtypeStruct(s, d) for s, d in outs],
        grid_spec=pltpu.PrefetchScalarGridSpec(
            num_scalar_prefetch=1, grid=(k // tr,), in_specs=specs,
            out_specs=[pl.BlockSpec((tr, n), lambda i, s: (i, 0)) for _ in outs]),
        compiler_params=_params(("parallel",)), name=name)(place, *args)


def _adamw(w, g, m, v):
    m = ADAM_B1 * m + (1.0 - ADAM_B1) * g
    v = ADAM_B2 * v + (1.0 - ADAM_B2) * (g * g)
    m_hat = m / (1.0 - ADAM_B1 ** ADAM_STEP)
    v_hat = v / (1.0 - ADAM_B2 ** ADAM_STEP)
    delta = -ADAM_LR * (m_hat * pl.reciprocal(jnp.sqrt(v_hat) + ADAM_EPS, approx=True) + ADAM_WD * w)
    return delta, m, v


def kernel(x, p, mix_norm_pre, w_in, lam_re, lam_im, log_dt, ssm_b_re, ssm_b_im, ssm_c_re, ssm_c_im, ssm_d, w_glu, b_glu, attn_out_norm, ssm_out_norm, w_out, mix_norm_post, mlp_norm_pre, w_up, w_down, mlp_norm_post, ple_norm_pre, w_ple_gate, w_ple_proj, ple_norm_post, loss_target, m_mix_norm_pre, m_w_in, m_lam_re, m_lam_im, m_log_dt, m_ssm_b_re, m_ssm_b_im, m_ssm_c_re, m_ssm_c_im, m_ssm_d, m_w_glu, m_b_glu, m_attn_out_norm, m_ssm_out_norm, m_w_out, m_mix_norm_post, m_mlp_norm_pre, m_w_up, m_w_down, m_mlp_norm_post, m_ple_norm_pre, m_w_ple_gate, m_w_ple_proj, m_ple_norm_post, v_mix_norm_pre, v_w_in, v_lam_re, v_lam_im, v_log_dt, v_ssm_b_re, v_ssm_b_im, v_ssm_c_re, v_ssm_c_im, v_ssm_d, v_w_glu, v_b_glu, v_attn_out_norm, v_ssm_out_norm, v_w_out, v_mix_norm_post, v_mlp_norm_pre, v_w_up, v_w_down, v_mlp_norm_post, v_ple_norm_pre, v_w_ple_gate, v_w_ple_proj, v_ple_norm_post):
    weights = dict(mix_norm_pre=mix_norm_pre, w_in=w_in, lam_re=lam_re, lam_im=lam_im, log_dt=log_dt, ssm_b_re=ssm_b_re, ssm_b_im=ssm_b_im, ssm_c_re=ssm_c_re, ssm_c_im=ssm_c_im, ssm_d=ssm_d, w_glu=w_glu, b_glu=b_glu, attn_out_norm=attn_out_norm, ssm_out_norm=ssm_out_norm, w_out=w_out, mix_norm_post=mix_norm_post, mlp_norm_pre=mlp_norm_pre, w_up=w_up, w_down=w_down, mlp_norm_post=mlp_norm_post, ple_norm_pre=ple_norm_pre, w_ple_gate=w_ple_gate, w_ple_proj=w_ple_proj, ple_norm_post=ple_norm_post)
    mom_m = dict(mix_norm_pre=m_mix_norm_pre, w_in=m_w_in, lam_re=m_lam_re, lam_im=m_lam_im, log_dt=m_log_dt, ssm_b_re=m_ssm_b_re, ssm_b_im=m_ssm_b_im, ssm_c_re=m_ssm_c_re, ssm_c_im=m_ssm_c_im, ssm_d=m_ssm_d, w_glu=m_w_glu, b_glu=m_b_glu, attn_out_norm=m_attn_out_norm, ssm_out_norm=m_ssm_out_norm, w_out=m_w_out, mix_norm_post=m_mix_norm_post, mlp_norm_pre=m_mlp_norm_pre, w_up=m_w_up, w_down=m_w_down, mlp_norm_post=m_mlp_norm_post, ple_norm_pre=m_ple_norm_pre, w_ple_gate=m_w_ple_gate, w_ple_proj=m_w_ple_proj, ple_norm_post=m_ple_norm_post)
    mom_v = dict(mix_norm_pre=v_mix_norm_pre, w_in=v_w_in, lam_re=v_lam_re, lam_im=v_lam_im, log_dt=v_log_dt, ssm_b_re=v_ssm_b_re, ssm_b_im=v_ssm_b_im, ssm_c_re=v_ssm_c_re, ssm_c_im=v_ssm_c_im, ssm_d=v_ssm_d, w_glu=v_w_glu, b_glu=v_b_glu, attn_out_norm=v_attn_out_norm, ssm_out_norm=v_ssm_out_norm, w_out=v_w_out, mix_norm_post=v_mix_norm_post, mlp_norm_pre=v_mlp_norm_pre, w_up=v_w_up, w_down=v_w_down, mlp_norm_post=v_mlp_norm_post, ple_norm_pre=v_ple_norm_pre, w_ple_gate=v_w_ple_gate, w_ple_proj=v_w_ple_proj, ple_norm_post=v_ple_norm_post)
    order = list(weights)
    big = ["w_in", "w_glu", "w_out", "w_up", "w_down", "w_ple_gate", "w_ple_proj"]
    col_sharded = {"w_in", "w_up", "w_ple_proj"}
    small = [n for n in order if n not in big]

    _, S, D = x.shape
    xs = x[0]
    tgt = loss_target[0]
    AW = attn_out_norm.shape[1]
    SW = ssm_d.shape[1]
    H = AW // HEAD_DIM
    G = SW // SSM_GROUP
    nslab = G // SLAB_GROUPS
    P_, C_ = SSM_STATE, SSM_GROUP

    shard = {n: weights[n][0].astype(BF16) for n in big}
    W, WT = {}, {}

    def arrived(names, gathered):
        for n, g in zip(names, gathered):
            W[n] = g if n in col_sharded else g.reshape(1, N_DEV * g.shape[1], g.shape[2])

    def transposed(g):
        return jnp.swapaxes(g, 1, 2).reshape(1, g.shape[0] * g.shape[2], g.shape[1])

    g1, g2, g3, g4, g5, g6 = (weights[n] for n in ("mix_norm_pre", "mix_norm_post", "mlp_norm_pre",
                                                      "mlp_norm_post", "ple_norm_pre", "ple_norm_post"))
    ga, gs = attn_out_norm, ssm_out_norm
    gather_in = gather_start([shard["w_in"]], shard["w_in"], name="gather_w_in_start")
    (hn1,) = rowwise(lambda a, g: (_rms(a, g),), [xs], [g1], [(D, BF16)], deps=(gather_in[-1],), name="norm_in")
    gather_in = gather_forward(gather_in, hn1, name="gather_w_in_forward")
    arrived(["w_in"], gather_finish(gather_in, gather_in[-1], name="gather_w_in_finish"))
    WT["w_in"] = transposed(W["w_in"])
    early, mid, late = ["w_glu", "w_out"], ["w_up"], ["w_down", "w_ple_gate", "w_ple_proj"]
    gather_early = gather_start([shard[n] for n in early], W["w_in"], name="gather_early_start")
    gather_mid = gather_start([shard[n] for n in mid], gather_early[-1], name="gather_mid_start")
    gather_late = gather_start([shard[n] for n in late], gather_mid[-1], name="gather_late_start")

    (proj,) = mm_nn(hn1, W["w_in"], [F32], deps=(gather_late[-1],), name="proj_in")
    attn, lse = attn_fwd(proj, H, name="attn_fwd")
    gather_early = gather_forward(gather_early, attn, name="gather_early_forward")
    (mix_a,) = rowwise(lambda a, g: (_rms(a, g),), [attn], [ga], [(AW, BF16)], deps=(gather_early[-1],),
                       name="attn_norm")
    arrived(early, gather_finish(gather_early, mix_a, name="gather_early_finish"))

    a_r, a_i, bb_r, bb_i = _discretise(lam_re[0], lam_im[0], log_dt[0], ssm_b_re[0], ssm_b_im[0])
    ssm_consts = (_block_diag(bb_r.swapaxes(1, 2), nslab).astype(BF16), _block_diag(bb_i.swapaxes(1, 2), nslab).astype(BF16),
                  a_r.reshape(nslab, 1, SLAB_STATES), a_i.reshape(nslab, 1, SLAB_STATES),
                  _block_diag(ssm_c_re[0].swapaxes(1, 2), nslab).astype(BF16),
                  _block_diag(ssm_c_im[0].swapaxes(1, 2), nslab).astype(BF16), ssm_d)
    u_seg = _to_segments(proj[:, 3 * AW:]).astype(BF16)
    y_pre, yg, st_r, st_i = ssm_fwd(u_seg, *ssm_consts, name="ssm_fwd")
    gather_mid = gather_forward(gather_mid, y_pre, name="gather_mid_forward")
    (gl1,) = mm_nn(yg, W["w_glu"], [BF16], epi=lambda acc, b: (acc + b,), bias=b_glu, deps=(gather_mid[-1],),
                   name="glu_gate")
    (mix_s,) = rowwise(lambda yp, gl, g: (_rms(_gelu(yp) * _sigmoid(gl), g),), [y_pre, gl1], [gs], [(SW, BF16)],
                       name="ssm_glu_norm")
    mixed = [mix_a, _from_segments(mix_s)]
    (mo,) = mm_nn(mixed, W["w_out"], [BF16], name="mix_out")

    def resid_norm(h, t, gpost, gpre):
        hh = h + _rms(t, gpost)
        return hh, _rms(hh, gpre)

    h1, hn2 = rowwise(resid_norm, [xs, mo], [g2, g3], [(D, F32), (D, BF16)], name="resid_mix")
    arrived(mid, gather_finish(gather_mid, hn2, name="gather_mid_finish"))
    gather_late = gather_forward(gather_late, W["w_up"], name="gather_late_forward")
    WT["w_up"] = transposed(W["w_up"])

    def relu2(acc):
        r = jnp.maximum(acc, 0.0)
        return acc, r * r

    up, act = mm_nn(hn2, W["w_up"], [BF16, BF16], epi=relu2, deps=(gather_late[-1],), tm=1024, tn=1024, name="mlp_up")
    arrived(late, gather_finish(gather_late, act, name="gather_late_finish"))
    (ff,) = mm_nn(act, W["w_down"], [BF16], name="mlp_down")
    h2, hn3 = rowwise(resid_norm, [h1, ff], [g4, g5], [(D, F32), (D, BF16)], name="resid_mlp")
    (gl2,) = mm_nn(hn3, W["w_ple_gate"], [BF16], name="ple_gate")
    pb = p[0, 0].astype(BF16)
    (emb,) = mm_nn(pb, W["w_ple_proj"], [BF16], name="ple_proj")

    def head(h, gl, e, t, g):
        sg = _sigmoid(gl)
        ge = sg * e
        err = h + _rms(ge, g) - t
        dh = err * (1.0 / D)
        dge, dg = _rms_bwd(dh, ge, g)
        return dh, dge * e * sg * (1.0 - sg), dge * sg, jnp.sum(err * err, axis=0, keepdims=True), dg

    dh3, dgl2, demb, loss_part, dg6 = rowwise(head, [h2, gl2, emb, tgt], [g6], [(D, F32), (D, BF16), (D, BF16)],
                                             [D, D], name="ple_loss_head")
    loss = lax.psum(0.5 / D * jnp.sum(loss_part), ("x", "y", "c"))

    x_i, y_i, c_i = _place()
    place = jnp.stack([c_i, 2 * x_i + y_i]).astype(jnp.int32)
    grads, out_g, out_d, out_m, out_v = {}, {}, {}, {}, {}

    def to_sibling(names, after, tag):
        chunks = []
        for n in names:
            g = grads[n]
            g = g if n in col_sharded else g.reshape(N_DEV, g.shape[1] // N_DEV, g.shape[2])
            chunks.append(g.reshape(4, 2, g.shape[1], g.shape[2]))
        return chunks, exchange_start(chunks, [(4,) + g.shape[2:] for g in chunks], _core_copies, len(chunks), after,
                                      name=f"grads_to_sibling_{tag}")

    def to_chips(names, sent, after, tag):
        chunks, state = sent
        sums = []
        for n, g, r in zip(names, chunks, exchange_wait(state, after, name=f"grads_from_sibling_{tag}")):
            k, nn = g.shape[2], g.shape[3]
            kb = k // _tile(k, 512, 16)

            def mine(i, s, kb=kb):
                return 2 * (i // kb) + s[0], i % kb

            (s,) = _blocked(lambda a, b: (a.astype(F32) + b.astype(F32),),
                            [(g.reshape(N_DEV, k, nn), mine), r.reshape(4 * k, nn)],
                            [((4 * k, nn), BF16)], place=place, tr=k // kb, name=f"chip_sum_{n}")
            sums.append(s.reshape(4, k, nn))
        return sums, exchange_start(sums, [(3,) + s.shape[1:] for s in sums], _chip_copies, 3 * len(sums), sums[-1],
                                    name=f"grads_to_chips_{tag}")

    def update(w_, m_, v_, own, r0, r1, r2):
        g = own.astype(F32) + r0.astype(F32) + r1.astype(F32) + r2.astype(F32)
        return (g,) + _adamw(w_, g, m_, v_)

    def finish(names, sent, after, tag):
        sums, state = sent
        for n, s, r in zip(names, sums, exchange_wait(state, after, name=f"grads_from_chips_{tag}")):
            shp = weights[n].shape
            res = _blocked(update, [weights[n][0], mom_m[n][0], mom_v[n][0], (s, lambda i, p_: (p_[1], i)),
                                    (r, lambda i, p_: (0, i)), (r, lambda i, p_: (1, i)), (r, lambda i, p_: (2, i))],
                           [(shp[1:], F32)] * 4, place=place, tr=max(16, min(shp[1] // 8, UPDATE_BLOCK // shp[2])),
                           name=f"adamw_{n}")
            out_g[n], out_d[n], out_m[n], out_v[n] = (t.reshape(shp) for t in res)
        return out_v[names[-1]]

    grads["w_ple_proj"] = mm_tn(pb, demb, N_DEV, name="grad_w_ple_proj")
    dhn3 = mm_nt(dgl2, W["w_ple_gate"], BF16, name="back_ple_gate")
    grads["w_ple_gate"] = mm_tn(hn3, dgl2, 1, name="grad_w_ple_gate")

    def back_resid(dh, dhn, h, t, gpre, gpost):
        d1, dgpre = _rms_bwd(dhn, h, gpre)
        dhh = dh + d1
        dt, dgpost = _rms_bwd(dhh, t, gpost)
        return dhh, dt, dgpre, dgpost

    dh2, dff, dg5, dg4 = rowwise(back_resid, [dh3, dhn3, h2, ff], [g5, g4], [(D, F32), (D, BF16)], [D, D],
                                 name="back_resid_mlp")
    dup = mm_nt(dff, W["w_down"], BF16, epi=lambda acc, u_: (acc * 2.0 * jnp.maximum(u_.astype(F32), 0.0),),
                extra=up, name="back_mlp_down")
    grads["w_down"] = mm_tn(act, dff, 1, name="grad_w_down")
    group_a = ["w_ple_proj", "w_ple_gate", "w_down"]
    sent_a = to_sibling(group_a, grads["w_down"], "a")
    (dhn2,) = mm_nn(dup, WT["w_up"], [BF16], deps=(sent_a[1][-1],), name="back_mlp_up")
    sent_a = to_chips(group_a, sent_a, dhn2, "a")
    grads["w_up"] = mm_tn(hn2, dup, N_DEV, deps=(sent_a[1][-1],), name="grad_w_up")
    dh1, dmo, dg3, dg2 = rowwise(back_resid, [dh2, dhn2, h1, mo], [g3, g2], [(D, F32), (D, BF16)], [D, D],
                                 name="back_resid_mix")
    dmixed = mm_nt(dmo, W["w_out"], BF16, name="back_mix_out")
    grads["w_out"] = mm_tn(mixed, dmo, 1, name="grad_w_out")

    def back_glu(dm, yp, gl, g):
        ygf = _gelu(yp)
        sg = _sigmoid(gl)
        dssm, dg = _rms_bwd(dm, ygf * sg, g)
        dgl = dssm * ygf * sg * (1.0 - sg)
        return dgl, dssm * sg, dg, jnp.sum(dgl, axis=0, keepdims=True)

    dgl1, dyg_direct, dgs, db_glu = rowwise(back_glu, [_to_segments(dmixed[:, AW:]), y_pre, gl1], [gs],
                                            [(SW, BF16), (SW, F32)], [SW, SW], name="back_glu")
    dyg_gate = mm_nt(dgl1, W["w_glu"], BF16, name="back_glu_gate")
    grads["w_glu"] = mm_tn(yg, dgl1, 1, name="grad_w_glu")
    group_b = ["w_up", "w_out", "w_glu"]
    sent_b = to_sibling(group_b, grads["w_glu"], "b")
    done_a = finish(group_a, sent_a, sent_b[1][-1], "a")

    du_seg, dbb_r, dbb_i, dcb_r, dcb_i, da_r, da_i, d_skip = ssm_bwd(
        u_seg, dyg_direct, dyg_gate, y_pre, st_r, st_i, *ssm_consts, deps=(done_a,), name="ssm_bwd")
    sent_b = to_chips(group_b, sent_b, du_seg, "b")

    def back_attn_norm(dm, a, l, g):
        da, dg = _rms_bwd(dm, a, g)
        prod = da * a
        delta = jnp.concatenate(
            [jnp.broadcast_to(jnp.sum(prod[:, h * HEAD_DIM:(h + 1) * HEAD_DIM], axis=-1, keepdims=True),
                              (prod.shape[0], HEAD_DIM)) for h in range(H)], axis=1)
        first_half = lax.broadcasted_iota(jnp.int32, l.shape, 1) % HEAD_DIM < HEAD_DIM // 2
        return da, jnp.where(first_half, l, delta), dg

    dattn, stats, dga = rowwise(back_attn_norm, [(dmixed, AW, 0), attn, lse], [ga], [(AW, F32), (AW, F32)], [AW],
                                deps=(sent_b[1][-1],), name="back_attn_norm")
    dq, dk, dv = attn_bwd(proj, dattn, stats, H, name="attn_bwd")
    dproj = [dq, dk, dv, _from_segments(du_seg)]
    (dhn1,) = mm_nn(dproj, WT["w_in"], [BF16], name="back_proj_in")

    def back_in(dh, dhn, a, g):
        d1, dg = _rms_bwd(dhn, a, g)
        return dh + d1, dg

    grad_x, dg1 = rowwise(back_in, [dh1, dhn1, xs], [g1], [(D, F32)], [D], name="back_norm_in")

    cot = dict(
        mix_norm_pre=dg1, mix_norm_post=dg2, mlp_norm_pre=dg3, mlp_norm_post=dg4, ple_norm_pre=dg5, ple_norm_post=dg6,
        attn_out_norm=dga, ssm_out_norm=dgs, b_glu=db_glu, ssm_d=d_skip,
        ssm_c_re=_block_diag_part(dcb_r, P_, C_).swapaxes(1, 2), ssm_c_im=_block_diag_part(dcb_i, P_, C_).swapaxes(1, 2),
        a_r=da_r.reshape(G, P_), a_i=da_i.reshape(G, P_),
        bb_r=_block_diag_part(dbb_r, C_, P_).swapaxes(1, 2), bb_i=_block_diag_part(dbb_i, C_, P_).swapaxes(1, 2))
    names = list(cot)
    flat = jnp.concatenate([cot[n].reshape(-1) for n in names])
    total = flat.shape[0]
    rows_ = -(-total // (LANES * 16)) * 16
    flat = jnp.pad(flat, (0, rows_ * LANES - total)).reshape(rows_, LANES)
    gather_small = gather_start([flat], flat, name="gather_small_start")
    grads["w_in"] = mm_tn(hn1, dproj, N_DEV, deps=(gather_small[-1],), tko=2048, name="grad_w_in")
    group_c = ["w_in"]
    sent_c = to_sibling(group_c, grads["w_in"], "c")
    done_b = finish(group_b, sent_b, sent_c[1][-1], "b")
    sent_c = to_chips(group_c, sent_c, done_b, "c")
    gather_small = gather_forward(gather_small, sent_c[1][-1], name="gather_small_forward")
    (every,) = gather_finish(gather_small, gather_small[-1], name="gather_small_finish")
    (summed,) = _blocked(lambda *t: (functools.reduce(lambda a, b: a + b, t),),
                         [(every, functools.partial(lambda i, p_, j: (j, i), j=j)) for j in range(N_DEV)],
                         [((rows_, LANES), F32)], name="sum_small_grads")
    summed = summed.reshape(-1)
    red, off = {}, 0
    for n in names:
        sz = cot[n].size
        red[n] = summed[off:off + sz].reshape(cot[n].shape)
        off += sz
    _, pull = jax.vjp(_discretise, lam_re[0], lam_im[0], log_dt[0], ssm_b_re[0], ssm_b_im[0])
    d_lre, d_lim, d_ldt, d_bre, d_bim = pull((red["a_r"], red["a_i"], red["bb_r"], red["bb_i"]))
    red.update(lam_re=d_lre, lam_im=d_lim, log_dt=d_ldt, ssm_b_re=d_bre, ssm_b_im=d_bim)

    def pack(d):
        t = jnp.concatenate([d[n].reshape(-1) for n in small])
        r_ = -(-t.shape[0] // (LANES * 16)) * 16
        return jnp.pad(t, (0, r_ * LANES - t.shape[0])).reshape(r_, LANES)

    sw, sg_, sm, sv = pack(weights), pack(red), pack(mom_m), pack(mom_v)
    sd, snm, snv = _blocked(lambda w_, g_, m_, v_: _adamw(w_, g_, m_, v_), [sw, sg_, sm, sv],
                            [(sw.shape, F32)] * 3, name="adamw_small")
    finish(group_c, sent_c, snv, "c")
    off = 0
    for n in small:
        sz = weights[n].size
        shp = weights[n].shape
        out_g[n] = red[n].reshape(shp)
        out_d[n] = sd.reshape(-1)[off:off + sz].reshape(shp)
        out_m[n] = snm.reshape(-1)[off:off + sz].reshape(shp)
        out_v[n] = snv.reshape(-1)[off:off + sz].reshape(shp)
        off += sz

    return (loss, grad_x[None], *[out_g[n] for n in order], *[out_d[n] for n in order],
            *[out_m[n] for n in order], *[out_v[n] for n in order])
```

```python
import functools
import math

import jax
import jax.numpy as jnp
from jax import lax
from jax.experimental import pallas as pl
from jax.experimental.pallas import tpu as pltpu

F32 = jnp.float32
BF16 = jnp.bfloat16
MESH = pl.DeviceIdType.MESH

N_DEV = 8
LANES = 128
SUBLANES = 8
VMEM_LIMIT = 48 * 1024 * 1024
VMEM_LIMIT_SCAN = 60 * 1024 * 1024

HEAD_DIM = 128
BLK = 128
DILATIONS = (1, 4, 16)
SSM_GROUP = 16
SSM_STATE = 64
SLAB_GROUPS = LANES // SSM_GROUP
SLAB_STATES = SLAB_GROUPS * SSM_STATE
SEGMENTS = SUBLANES
SCAN_UNROLL = 4
RMS_EPS = 1e-6
NEG_INF = -1e30

ADAM_LR = 0.001
ADAM_B1 = 0.9
ADAM_B2 = 0.999
ADAM_EPS = 1e-08
ADAM_WD = 0.01
ADAM_STEP = 10
UPDATE_BLOCK = 256 * 1024


def _tile(n, pref, unit=LANES):
    if n <= pref:
        return n
    t = (pref // unit) * unit
    while t > unit and n % t:
        t -= unit
    assert n % t == 0, (n, pref, unit)
    return t


def _params(sem=None, vmem=VMEM_LIMIT):
    return pltpu.CompilerParams(dimension_semantics=sem, vmem_limit_bytes=vmem)


_NN = (((1,), (0,)), ((), ()))
_NT = (((1,), (1,)), ((), ()))
_TN = (((0,), (0,)), ((), ()))


_ANY = pl.BlockSpec(memory_space=pl.ANY)


def _mm_call(dims, nk, na, nb, pick, n_extra, n_dep, n_out, epi, group=1, **kw):
    first_extra = na + nb
    first_out = first_extra + n_extra + n_dep
    kw["in_specs"] = list(kw["in_specs"]) + [_ANY] * n_dep

    def grouped(refs, step):
        if group == 1:
            return lax.dot_general(refs[0][...], refs[1][...], dims, preferred_element_type=F32)
        kp = refs[0].shape[1]
        return sum(lax.dot_general(refs[step * group + p][...], refs[na][pl.ds(p * kp, kp), :], dims,
                                   preferred_element_type=F32) for p in range(group))

    def single(*refs):
        extra = refs[first_extra:first_extra + n_extra]
        res = epi(grouped(refs, 0), *[e[...] for e in extra])
        for o, r in zip(refs[first_out:first_out + n_out], res):
            o[...] = r.astype(o.dtype)

    if nk == 1:
        assert na == group and nb == 1
        kw["scratch_shapes"] = []
        return pl.pallas_call(single, **kw)

    def body(*refs):
        extra = refs[first_extra:first_extra + n_extra]
        outs = refs[first_out:first_out + n_out]
        acc = refs[-1]
        k = pl.program_id(2)

        @pl.when(k == 0)
        def _():
            acc[...] = jnp.zeros_like(acc)

        def add(a_ref, b_ref):
            acc[...] += lax.dot_general(a_ref[...], b_ref[...], dims, preferred_element_type=F32)

        if na == nb == 1:
            add(refs[0], refs[1])
        elif group > 1:
            for step in range(nk):
                @pl.when(k == step)
                def _(step=step):
                    acc[...] += grouped(refs, step)
        else:
            pa, pb = pick(pl.program_id(0), pl.program_id(1), k)
            for x in range(na):
                for y in range(nb):
                    pl.when((pa == x) & (pb == y))(functools.partial(add, refs[x], refs[na + y]))

        @pl.when(k == nk - 1)
        def _():
            res = epi(acc[...], *[e[...] for e in extra])
            for o, r in zip(outs, res):
                o[...] = r.astype(o.dtype)

    return pl.pallas_call(body, **kw)


def _identity_epi(acc):
    return (acc,)


def _parts(t):
    return list(t) if isinstance(t, (list, tuple)) else [t]


def _part_spec(block, part, which, index):
    def index_map(i, j, k):
        use = which(i, j, k) == part
        r, c = index(i, j, k)
        return jnp.where(use, r, 0), jnp.where(use, c, 0)
    return pl.BlockSpec(block, index_map)


def mm_nn(a, w, out_dtypes, *, name, epi=_identity_epi, bias=None, deps=(), tm=2048, tn=512, tk=2048):
    a = _parts(a)
    M, Kp = a[0].shape
    K = Kp * len(a)
    J, K2, n = w.shape
    assert K == K2
    tm, tn, tk = _tile(M, tm, 16), _tile(n, tn), _tile(K, tk)
    npj = n // tn
    nk = K // tk
    group = 1
    if len(a) == 1:
        in_specs = [pl.BlockSpec((tm, tk), lambda i, j, k: (i, k))]
    else:
        assert tk % Kp == 0
        group = tk // Kp
        in_specs = [pl.BlockSpec((tm, Kp), lambda i, j, k: (i, 0)) for _ in a]
    in_specs.append(pl.BlockSpec((None, tk, tn), lambda i, j, k: (j // npj, k, j % npj)))
    args = a + [w]
    if bias is not None:
        in_specs.append(pl.BlockSpec((1, tn), lambda i, j, k: (0, j)))
        args.append(bias)
    return _mm_call(
        _NN, nk, len(a), 1, lambda i, j, k: (k, 0), len(args) - len(a) - 1, len(deps), len(out_dtypes), epi, group,
        out_shape=[jax.ShapeDtypeStruct((M, J * n), d) for d in out_dtypes],
        grid=(M // tm, J * npj, nk), in_specs=in_specs,
        out_specs=[pl.BlockSpec((tm, tn), lambda i, j, k: (i, j)) for _ in out_dtypes],
        scratch_shapes=[pltpu.VMEM((tm, tn), F32)],
        compiler_params=_params(("parallel", "parallel", "arbitrary")), name=name)(*args, *deps)


def mm_nt(a, w, out_dtype, *, name, epi=_identity_epi, extra=None, tm=2048, tko=512, tnr=2048):
    M, N = a.shape
    J, K, n = w.shape
    assert N == J * n
    tm, tko, tnr = _tile(M, tm, 16), _tile(K, tko), _tile(n, tnr)
    npj = n // tnr
    nk = N // tnr
    in_specs = [pl.BlockSpec((tm, tnr), lambda i, j, k: (i, k)),
                pl.BlockSpec((None, tko, tnr), lambda i, j, k: (k // npj, j, k % npj))]
    args = [a, w]
    if extra is not None:
        in_specs.append(pl.BlockSpec((tm, tko), lambda i, j, k: (i, j)))
        args.append(extra)
    return _mm_call(
        _NT, nk, 1, 1, None, len(args) - 2, 0, 1, epi,
        out_shape=[jax.ShapeDtypeStruct((M, K), out_dtype)],
        grid=(M // tm, K // tko, nk), in_specs=in_specs,
        out_specs=[pl.BlockSpec((tm, tko), lambda i, j, k: (i, j))],
        scratch_shapes=[pltpu.VMEM((tm, tko), F32)],
        compiler_params=_params(("parallel", "parallel", "arbitrary")), name=name)(*args)[0]


def mm_tn(a, b, J, *, name, deps=(), tko=1024, tn=1024, ts=2048):
    a, b = _parts(a), _parts(b)
    S, Kp = a[0].shape
    S2, Np = b[0].shape
    K, N = Kp * len(a), Np * len(b)
    assert S == S2 and N % J == 0
    n = N // J
    tko, tn, ts = _tile(Kp, tko), _tile(math.gcd(n, Np), tn), _tile(S, ts)
    npj = n // tn
    nk = S // ts
    ta, tb = Kp // tko, Np // tn
    assert nk > 1 or len(a) == len(b) == 1
    return _mm_call(
        _TN, nk, len(a), len(b), lambda i, j, k: (i // ta, j // tb), 0, len(deps), 1, _identity_epi,
        out_shape=[jax.ShapeDtypeStruct((J, K, n), BF16)],
        grid=(K // tko, J * npj, nk),
        in_specs=([_part_spec((ts, tko), x, lambda i, j, k: i // ta, lambda i, j, k: (k, i % ta)) for x in range(len(a))]
                  + [_part_spec((ts, tn), y, lambda i, j, k: j // tb, lambda i, j, k: (k, j % tb)) for y in range(len(b))]),
        out_specs=[pl.BlockSpec((None, tko, tn), lambda i, j, k: (j // npj, i, j % npj))],
        scratch_shapes=[pltpu.VMEM((tko, tn), F32)],
        compiler_params=_params(("parallel", "parallel", "arbitrary")), name=name)(*a, *b, *deps)[0]


def rowwise(fn, rows, vecs, outs, accs=(), *, name, deps=(), ts=256):
    rows = [r if isinstance(r, tuple) else (r, r.shape[1], 0) for r in rows]
    S = rows[0][0].shape[0]
    ts = _tile(S, ts, 16)
    nr, nv, no, nd = len(rows), len(vecs), len(outs), len(deps)

    def body(*refs):
        r, v = refs[:nr], refs[nr:nr + nv]
        o, a = refs[nr + nv + nd:nr + nv + nd + no], refs[nr + nv + nd + no:]
        res = fn(*[t[...].astype(F32) for t in r], *[t[...] for t in v])
        for ref, val in zip(o, res[:no]):
            ref[...] = val.astype(ref.dtype)
        if a:
            @pl.when(pl.program_id(0) == 0)
            def _():
                for ref in a:
                    ref[...] = jnp.zeros_like(ref)

            for ref, val in zip(a, res[no:]):
                ref[...] += val

    in_specs = [pl.BlockSpec((ts, w), functools.partial(lambda i, cb: (i, cb), cb=cb)) for _, w, cb in rows]
    in_specs += [pl.BlockSpec(v.shape, lambda i: (0, 0)) for v in vecs] + [_ANY] * nd
    out_shape = [jax.ShapeDtypeStruct((S, w), d) for w, d in outs]
    out_shape += [jax.ShapeDtypeStruct((1, w), F32) for w in accs]
    out_specs = [pl.BlockSpec((ts, w), lambda i: (i, 0)) for w, _ in outs]
    out_specs += [pl.BlockSpec((1, w), lambda i: (0, 0)) for w in accs]
    return pl.pallas_call(body, out_shape=out_shape, grid=(S // ts,), in_specs=in_specs, out_specs=out_specs,
                          compiler_params=_params(("arbitrary",)), name=name)(*[r[0] for r in rows], *vecs, *deps)


def _rms(x, g):
    r = lax.rsqrt(jnp.mean(x * x, axis=-1, keepdims=True) + RMS_EPS)
    return x * r * g


def _rms_bwd(dy, x, g):
    r = lax.rsqrt(jnp.mean(x * x, axis=-1, keepdims=True) + RMS_EPS)
    xh = x * r
    dxh = dy * g
    dx = r * (dxh - xh * jnp.mean(dxh * xh, axis=-1, keepdims=True))
    return dx, jnp.sum(dy * xh, axis=0, keepdims=True)


def _sigmoid(x):
    return pl.reciprocal(1.0 + jnp.exp(-x), approx=True)


_GELU_C = math.sqrt(2.0 / math.pi)


def _gelu(x):
    return 0.5 * x * (1.0 + jnp.tanh(_GELU_C * (x + 0.044715 * x * x * x)))


def _gelu_grad(x):
    t = jnp.tanh(_GELU_C * (x + 0.044715 * x * x * x))
    return 0.5 * (1.0 + t) + 0.5 * x * (1.0 - t * t) * _GELU_C * (1.0 + 3.0 * 0.044715 * x * x)


ATTN_INTERLEAVE = 8
KEY_PAD = BLK * max(DILATIONS)


def _key_mask(n):
    ii = lax.broadcasted_iota(jnp.int32, (BLK, 2 * BLK), 0)
    jj = lax.broadcasted_iota(jnp.int32, (BLK, 2 * BLK), 1)
    return ((jj < BLK) & (jj >= ii) & (n > 0)) | ((jj >= BLK) & (jj - BLK <= ii))


def _units(d, nblk):
    nb = nblk // d
    if nb == 2:
        def unit(idx):
            ii = lax.broadcasted_iota(jnp.int32, (2 * BLK, 2 * BLK), 0)
            jj = lax.broadcasted_iota(jnp.int32, (2 * BLK, 2 * BLK), 1)
            return pl.ds(idx, 2 * BLK, stride=d), pl.ds(KEY_PAD + idx, 2 * BLK, stride=d), (jj <= ii) & (ii - jj <= BLK)
        return d, max(1, ATTN_INTERLEAVE // 4), unit

    def unit(idx):
        r, n = idx // nb, idx % nb
        cur = r + n * (BLK * d)
        keys = cur + (KEY_PAD - BLK * d)
        if d == 1:
            return pl.ds(pl.multiple_of(cur, BLK), BLK), pl.ds(pl.multiple_of(keys, BLK), 2 * BLK), _key_mask(n)
        return pl.ds(cur, BLK, stride=d), pl.ds(keys, 2 * BLK, stride=d), _key_mask(n)
    return nblk, ATTN_INTERLEAVE, unit


def _pad_keys(dst, src):
    dst[pl.ds(0, KEY_PAD), :] = jnp.zeros((KEY_PAD, dst.shape[1]), F32)

    def copy(c, carry):
        dst[pl.ds(pl.multiple_of(KEY_PAD + c * BLK, BLK), BLK), :] = src[pl.ds(pl.multiple_of(c * BLK, BLK), BLK), :]
        return carry

    lax.fori_loop(0, src.shape[0] // BLK, copy, 0)


def attn_fwd(proj, n_heads, *, name):
    S, WP = proj.shape
    assert S % (BLK * max(DILATIONS)) == 0
    nblk = S // BLK
    AW = n_heads * HEAD_DIM
    scale = 1.0 / math.sqrt(HEAD_DIM)

    def body(q_ref, k_ref, v_ref, o_ref, l_ref, acc, mrun, lrun, kp, vp):
        _pad_keys(kp, k_ref)
        _pad_keys(vp, v_ref)
        for first, d in zip((True, False, False), reversed(DILATIONS)):
            n_units, per_step, unit = _units(d, nblk)

            def step(it, carry, first=first, n_units=n_units, per_step=per_step, unit=unit):
                units = [unit(it + j * (n_units // per_step)) for j in range(per_step)]
                ss = [lax.dot_general(q_ref[cur, :].astype(BF16), kp[keys, :].astype(BF16), _NT,
                                      preferred_element_type=F32) * scale for cur, keys, _ in units]
                ss = [jnp.where(mask, s, NEG_INF) for s, (_, _, mask) in zip(ss, units)]
                ms = [jnp.max(s, axis=-1, keepdims=True) for s in ss]
                ps = [jnp.exp(s - m) for s, m in zip(ss, ms)]
                ls = [jnp.sum(p, axis=-1, keepdims=True) for p in ps]
                os_ = [jnp.dot(p.astype(BF16), vp[keys, :].astype(BF16), preferred_element_type=F32)
                       for p, (_, keys, _) in zip(ps, units)]
                for (cur, keys, mask), m, l, o in zip(units, ms, ls, os_):
                    m = jnp.broadcast_to(m, o.shape)
                    l = jnp.broadcast_to(l, o.shape)
                    if first:
                        acc[cur, :], mrun[cur, :], lrun[cur, :] = o, m, l
                    else:
                        m_old = mrun[cur, :]
                        m_new = jnp.maximum(m_old, m)
                        w_old, w_blk = jnp.exp(m_old - m_new), jnp.exp(m - m_new)
                        acc[cur, :] = w_old * acc[cur, :] + w_blk * o
                        lrun[cur, :] = w_old * lrun[cur, :] + w_blk * l
                        mrun[cur, :] = m_new
                return carry

            lax.fori_loop(0, n_units // per_step, step, 0)

        def finish(c, carry):
            r = pl.ds(pl.multiple_of(c * BLK, BLK), BLK)
            o_ref[r, :] = acc[r, :] / lrun[r, :]
            l_ref[r, :] = mrun[r, :] + jnp.log(lrun[r, :])
            return carry

        lax.fori_loop(0, nblk, finish, 0)

    def col(off):
        return pl.BlockSpec((S, HEAD_DIM), lambda h: (0, off + h))

    ospec = pl.BlockSpec((S, HEAD_DIM), lambda h: (0, h))
    return pl.pallas_call(
        body, out_shape=[jax.ShapeDtypeStruct((S, AW), F32)] * 2, grid=(n_heads,),
        in_specs=[col(0), col(n_heads), col(2 * n_heads)], out_specs=[ospec, ospec],
        scratch_shapes=[pltpu.VMEM((S, HEAD_DIM), F32)] * 3 + [pltpu.VMEM((KEY_PAD + S, HEAD_DIM), F32)] * 2,
        compiler_params=_params(("parallel",)), name=name)(proj, proj, proj)


def attn_bwd(proj, do, stats, n_heads, *, name):
    S, WP = proj.shape
    nblk = S // BLK
    AW = n_heads * HEAD_DIM
    scale = 1.0 / math.sqrt(HEAD_DIM)

    def body(q_ref, k_ref, v_ref, do_ref, st_ref, dq_ref, dk_ref, dv_ref, dq_sc, dk_sc, dv_sc, kp, vp):
        _pad_keys(kp, k_ref)
        _pad_keys(vp, v_ref)
        order = list(reversed(DILATIONS))
        assign_first = nblk // order[0] == 2
        if assign_first:
            dk_sc[pl.ds(0, KEY_PAD), :] = jnp.zeros((KEY_PAD, HEAD_DIM), F32)
            dv_sc[pl.ds(0, KEY_PAD), :] = jnp.zeros((KEY_PAD, HEAD_DIM), F32)
        else:
            dq_sc[...] = jnp.zeros_like(dq_sc)
            dk_sc[...] = jnp.zeros_like(dk_sc)
            dv_sc[...] = jnp.zeros_like(dv_sc)
        for assign, d in zip((assign_first, False, False), order):
            n_units, per_step, unit = _units(d, nblk)

            def step(it, carry, n_units=n_units, per_step=per_step, unit=unit, assign=assign):
                units = [unit(it + j * (n_units // per_step)) for j in range(per_step)]
                qs = [q_ref[cur, :].astype(BF16) for cur, _, _ in units]
                gs = [do_ref[cur, :].astype(BF16) for cur, _, _ in units]
                ks = [kp[keys, :].astype(BF16) for _, keys, _ in units]
                ss = [lax.dot_general(q, kb, _NT, preferred_element_type=F32) * scale for q, kb in zip(qs, ks)]
                dps = [lax.dot_general(g, vp[keys, :].astype(BF16), _NT, preferred_element_type=F32)
                       for g, (_, keys, _) in zip(gs, units)]
                sts = [st_ref[cur, :] for cur, _, _ in units]
                ps = [jnp.where(mask, jnp.exp(s - st[:, :1]), 0.0) for s, st, (_, _, mask) in zip(ss, sts, units)]
                dss = [(p * (dp - st[:, HEAD_DIM // 2:HEAD_DIM // 2 + 1]) * scale).astype(BF16)
                       for p, dp, st in zip(ps, dps, sts)]
                for (cur, keys, _), q, g, kb, p, ds in zip(units, qs, gs, ks, ps, dss):
                    dq = jnp.dot(ds, kb, preferred_element_type=F32)
                    dk = lax.dot_general(ds, q, _TN, preferred_element_type=F32)
                    dv = lax.dot_general(p.astype(BF16), g, _TN, preferred_element_type=F32)
                    if assign:
                        dq_sc[cur, :], dk_sc[keys, :], dv_sc[keys, :] = dq, dk, dv
                    else:
                        dq_sc[cur, :] += dq
                        dk_sc[keys, :] += dk
                        dv_sc[keys, :] += dv
                return carry

            lax.fori_loop(0, n_units // per_step, step, 0)
        rows = pl.ds(KEY_PAD, S)
        dq_ref[...] = dq_sc[...].astype(BF16)
        dk_ref[...] = dk_sc[rows, :].astype(BF16)
        dv_ref[...] = dv_sc[rows, :].astype(BF16)

    def col(off):
        return pl.BlockSpec((S, HEAD_DIM), lambda h: (0, off + h))

    ospec = pl.BlockSpec((S, HEAD_DIM), lambda h: (0, h))
    return pl.pallas_call(
        body, out_shape=[jax.ShapeDtypeStruct((S, AW), BF16)] * 3, grid=(n_heads,),
        in_specs=[col(0), col(n_heads), col(2 * n_heads), ospec, ospec], out_specs=[ospec] * 3,
        scratch_shapes=[pltpu.VMEM((S, HEAD_DIM), F32)] + [pltpu.VMEM((KEY_PAD + S, HEAD_DIM), F32)] * 4,
        compiler_params=_params(("parallel",), VMEM_LIMIT_SCAN), name=name)(proj, proj, proj, do, stats)


def _to_segments(t):
    S, W = t.shape
    return t.reshape(SEGMENTS, S // SEGMENTS, W).swapaxes(0, 1).reshape(S, W)


def _from_segments(t):
    S, W = t.shape
    return t.reshape(S // SEGMENTS, SEGMENTS, W).swapaxes(0, 1).reshape(S, W)


def _cmul(ar, ai, br, bi):
    return ar * br - ai * bi, ar * bi + ai * br


def _power(ar, ai, log2n):
    for _ in range(log2n):
        ar, ai = _cmul(ar, ai, ar, ai)
    return ar, ai


def _shift_rows(x, up):
    row = lax.broadcasted_iota(jnp.int32, x.shape, 0)
    if up:
        return jnp.where(row == SEGMENTS - 1, 0.0, pltpu.roll(x, SEGMENTS - 1, 0))
    return jnp.where(row == 0, 0.0, pltpu.roll(x, 1, 0))


def _segment_carries(er, ei, pr, pi, up):
    cr = jnp.zeros_like(er)
    ci = jnp.zeros_like(ei)
    for _ in range(SEGMENTS - 1):
        tr, ti = _cmul(pr, pi, cr, ci)
        cr, ci = _shift_rows(er + tr, up), _shift_rows(ei + ti, up)
    return cr, ci


def _scan_states(sr, si, ar, ai, T, reverse):
    ns = sr.shape[1]
    ar8 = jnp.broadcast_to(ar, (SEGMENTS, ns))
    ai8 = jnp.broadcast_to(ai, (SEGMENTS, ns))

    def rows(t):
        k = (T - 1 - t) if reverse else t
        return pl.ds(pl.multiple_of(k * SEGMENTS, SEGMENTS), SEGMENTS)

    def advance(t, c):
        tr, ti = _cmul(ar8, ai8, c[0], c[1])
        return tr + sr[rows(t), :], ti + si[rows(t), :]

    def several(step):
        def trip(t, c):
            for j in range(SCAN_UNROLL):
                c = step(t * SCAN_UNROLL + j, c)
            return c
        return trip

    zero = jnp.zeros((SEGMENTS, ns), F32)
    er, ei = lax.fori_loop(0, T // SCAN_UNROLL, several(advance), (zero, zero))
    pr, pi = _power(ar, ai, T.bit_length() - 1)
    cr, ci = _segment_carries(er, ei, jnp.broadcast_to(pr, (SEGMENTS, ns)), jnp.broadcast_to(pi, (SEGMENTS, ns)), reverse)

    def store(t, c):
        nr, ni = advance(t, c)
        sr[rows(t), :] = nr
        si[rows(t), :] = ni
        return nr, ni

    lax.fori_loop(0, T // SCAN_UNROLL, several(store), (cr, ci))
    return cr, ci


def _slab_specs(ns):
    return [pl.BlockSpec((None, LANES, ns), lambda g: (g, 0, 0)),
            pl.BlockSpec((None, LANES, ns), lambda g: (g, 0, 0)),
            pl.BlockSpec((None, 1, ns), lambda g: (g, 0, 0)),
            pl.BlockSpec((None, 1, ns), lambda g: (g, 0, 0)),
            pl.BlockSpec((None, ns, LANES), lambda g: (g, 0, 0)),
            pl.BlockSpec((None, ns, LANES), lambda g: (g, 0, 0)),
            pl.BlockSpec((1, LANES), lambda g: (0, g))]


def _chunks(S):
    rc = _tile(S, 512, 16)
    return rc, S // rc


def ssm_fwd(u, bbr, bbi, ar, ai, cbr, cbi, dsk, *, name):
    S, SW = u.shape
    nslab, _, ns = bbr.shape
    T = S // SEGMENTS
    assert T & (T - 1) == 0
    rc, nc = _chunks(S)

    def body(u_ref, br_ref, bi_ref, ar_ref, ai_ref, cr_ref, ci_ref, d_ref, y_ref, yg_ref, str_ref, sti_ref, sr, si):
        def inputs(c, carry):
            r = pl.ds(pl.multiple_of(c * rc, rc), rc)
            sr[r, :] = jnp.dot(u_ref[r, :], br_ref[...], preferred_element_type=F32)
            si[r, :] = jnp.dot(u_ref[r, :], bi_ref[...], preferred_element_type=F32)
            return carry

        lax.fori_loop(0, nc, inputs, 0)
        _scan_states(sr, si, ar_ref[...], ai_ref[...], T, False)

        def outputs(c, carry):
            r = pl.ds(pl.multiple_of(c * rc, rc), rc)
            srb, sib = sr[r, :].astype(BF16), si[r, :].astype(BF16)
            str_ref[r, :] = srb
            sti_ref[r, :] = sib
            y = (jnp.dot(srb, cr_ref[...], preferred_element_type=F32)
                 - jnp.dot(sib, ci_ref[...], preferred_element_type=F32) + d_ref[...] * u_ref[r, :].astype(F32))
            y_ref[r, :] = y
            yg_ref[r, :] = _gelu(y).astype(BF16)
            return carry

        lax.fori_loop(0, nc, outputs, 0)

    slab = pl.BlockSpec((S, LANES), lambda g: (0, g))
    states = pl.BlockSpec((S, ns), lambda g: (0, g))
    return pl.pallas_call(
        body, out_shape=([jax.ShapeDtypeStruct((S, SW), F32), jax.ShapeDtypeStruct((S, SW), BF16)]
                         + [jax.ShapeDtypeStruct((S, nslab * ns), BF16)] * 2),
        grid=(nslab,), in_specs=[slab] + _slab_specs(ns), out_specs=[slab, slab, states, states],
        scratch_shapes=[pltpu.VMEM((S, ns), F32)] * 2,
        compiler_params=_params(("parallel",), VMEM_LIMIT_SCAN), name=name)(u, bbr, bbi, ar, ai, cbr, cbi, dsk)


def ssm_bwd(u, d_direct, d_gate, y, st_r, st_i, bbr, bbi, ar, ai, cbr, cbi, dsk, *, name, deps=()):
    S, SW = u.shape
    nslab, ns, _ = bbr.shape
    T = S // SEGMENTS
    rc, nc = _chunks(S)
    pair_rows = 2 * SEGMENTS

    def body(*refs):
        (u_ref, d1_ref, d2_ref, y_ref, sr_ref, si_ref, br_ref, bi_ref, ar_ref, ai_ref, cr_ref, ci_ref,
         d_ref) = refs[:13]
        (du_ref, dbr_ref, dbi_ref, dcr_ref, dci_ref, dar_ref, dai_ref, dd_ref, lr, li,
         dy_ref) = refs[13 + len(deps):]

        def inputs(c, skip):
            r = pl.ds(pl.multiple_of(c * rc, rc), rc)
            dy = (d1_ref[r, :] + d2_ref[r, :].astype(F32)) * _gelu_grad(y_ref[r, :])
            dy_ref[r, :] = dy
            gb = dy.astype(BF16)
            lr[r, :] = jnp.dot(gb, cr_ref[...], preferred_element_type=F32)
            li[r, :] = -jnp.dot(gb, ci_ref[...], preferred_element_type=F32)
            return skip + jnp.sum(dy * u_ref[r, :].astype(F32), axis=0, keepdims=True)

        dd_ref[...] = lax.fori_loop(0, nc, inputs, jnp.zeros((1, LANES), F32))
        _scan_states(lr, li, ar_ref[...], -ai_ref[...], T, True)

        def steps(j):
            rows = pl.ds(pl.multiple_of(j * pair_rows, pair_rows), pair_rows)
            tr, ti = sr_ref[rows, :].astype(F32), si_ref[rows, :].astype(F32)
            return tr[:SEGMENTS], tr[SEGMENTS:], ti[:SEGMENTS], ti[SEGMENTS:]

        def pair(j, c):
            acc_r, acc_i, pr, pi = c
            lo_r, hi_r, lo_i, hi_i = steps(j)
            first = pl.ds(pl.multiple_of(j * pair_rows, SEGMENTS), SEGMENTS)
            second = pl.ds(pl.multiple_of(j * pair_rows + SEGMENTS, SEGMENTS), SEGMENTS)
            la_r, la_i, lb_r, lb_i = lr[first, :], li[first, :], lr[second, :], li[second, :]
            return (acc_r + la_r * pr + la_i * pi + lb_r * lo_r + lb_i * lo_i,
                    acc_i - la_r * pi + la_i * pr - lb_r * lo_i + lb_i * lo_r, hi_r, hi_i)

        def pairs(t, c):
            return pair(2 * t + 1, pair(2 * t, c))

        _, end_r, _, end_i = steps(T // 2 - 1)
        zero = jnp.zeros((SEGMENTS, ns), F32)
        acc = lax.fori_loop(0, T // 4, pairs, (zero, zero, _shift_rows(end_r, False), _shift_rows(end_i, False)))
        dar_ref[...] = jnp.sum(acc[0], axis=0, keepdims=True)
        dai_ref[...] = jnp.sum(acc[1], axis=0, keepdims=True)

        dbr_ref[...] = jnp.zeros_like(dbr_ref)
        dbi_ref[...] = jnp.zeros_like(dbi_ref)
        dcr_ref[...] = jnp.zeros_like(dcr_ref)
        dci_ref[...] = jnp.zeros_like(dci_ref)

        def outputs(c, carry):
            r = pl.ds(pl.multiple_of(c * rc, rc), rc)
            ub = u_ref[r, :]
            g = dy_ref[r, :]
            gb = g.astype(BF16)
            lrb = lr[r, :].astype(BF16)
            lib = li[r, :].astype(BF16)
            du_ref[r, :] = (jnp.dot(lrb, br_ref[...], preferred_element_type=F32)
                            + jnp.dot(lib, bi_ref[...], preferred_element_type=F32) + d_ref[...] * g).astype(BF16)
            dbr_ref[...] += lax.dot_general(ub, lrb, _TN, preferred_element_type=F32)
            dbi_ref[...] += lax.dot_general(ub, lib, _TN, preferred_element_type=F32)
            dcr_ref[...] += lax.dot_general(sr_ref[r, :], gb, _TN, preferred_element_type=F32)
            dci_ref[...] -= lax.dot_general(si_ref[r, :], gb, _TN, preferred_element_type=F32)
            return carry

        lax.fori_loop(0, nc, outputs, 0)

    slab = pl.BlockSpec((S, LANES), lambda g: (0, g))
    states = pl.BlockSpec((S, ns), lambda g: (0, g))
    bspec = pl.BlockSpec((None, LANES, ns), lambda g: (g, 0, 0))
    cspec = pl.BlockSpec((None, ns, LANES), lambda g: (g, 0, 0))
    aspec = pl.BlockSpec((None, 1, ns), lambda g: (g, 0, 0))
    return pl.pallas_call(
        body,
        out_shape=[jax.ShapeDtypeStruct((S, SW), BF16),
                   jax.ShapeDtypeStruct((nslab, LANES, ns), F32), jax.ShapeDtypeStruct((nslab, LANES, ns), F32),
                   jax.ShapeDtypeStruct((nslab, ns, LANES), F32), jax.ShapeDtypeStruct((nslab, ns, LANES), F32),
                   jax.ShapeDtypeStruct((nslab, 1, ns), F32), jax.ShapeDtypeStruct((nslab, 1, ns), F32),
                   jax.ShapeDtypeStruct((1, SW), F32)],
        grid=(nslab,),
        in_specs=([slab, slab, slab, slab, states, states, cspec, cspec, aspec, aspec, bspec, bspec,
                   pl.BlockSpec((1, LANES), lambda g: (0, g))] + [_ANY] * len(deps)),
        out_specs=[slab, bspec, bspec, cspec, cspec, aspec, aspec, pl.BlockSpec((1, LANES), lambda g: (0, g))],
        scratch_shapes=[pltpu.VMEM((S, ns), F32)] * 2 + [pltpu.VMEM((S, LANES), F32)],
        compiler_params=_params(("parallel",), VMEM_LIMIT_SCAN), name=name)(
            u, d_direct, d_gate, y, st_r, st_i, bbr, bbi, ar, ai, cbr, cbi, dsk, *deps)


def _discretise(lam_re, lam_im, log_dt, b_re, b_im):
    dt = jnp.exp(log_dt)[:, None]
    mag = jnp.exp(lam_re * dt)
    ar = mag * jnp.cos(lam_im * dt)
    ai = mag * jnp.sin(lam_im * dt)
    nr, ni = ar - 1.0, ai
    den = lam_re * lam_re + lam_im * lam_im
    cr = ((nr * lam_re + ni * lam_im) / den)[..., None]
    ci = ((ni * lam_re - nr * lam_im) / den)[..., None]
    return ar, ai, cr * b_re - ci * b_im, cr * b_im + ci * b_re


def _block_diag(t, nslab):
    G, R, C = t.shape
    eye = jnp.eye(SLAB_GROUPS, dtype=t.dtype)
    t = t.reshape(nslab, SLAB_GROUPS, R, C)
    return jnp.einsum('sgrc,gh->sgrhc', t, eye).reshape(nslab, SLAB_GROUPS * R, SLAB_GROUPS * C)


def _block_diag_part(t, R, C):
    nslab = t.shape[0]
    eye = jnp.eye(SLAB_GROUPS, dtype=t.dtype)
    t = t.reshape(nslab, SLAB_GROUPS, R, SLAB_GROUPS, C)
    return jnp.einsum('sgrhc,gh->sgrc', t, eye).reshape(nslab * SLAB_GROUPS, R, C)


def _place():
    return lax.axis_index("x"), lax.axis_index("y"), lax.axis_index("c")


_HBM = pl.BlockSpec(memory_space=pltpu.HBM)
_SEM = pl.BlockSpec(memory_space=pltpu.SEMAPHORE)
_ORDERED_EFFECT = pltpu.SideEffectType.DATAFLOW_SIDE_EFFECTING


def _split_call(name, srcs, zones, sems_in, n_new, body_fn, after):
    nsrc, nz, ns, nn = len(srcs), len(zones), len(sems_in), len(n_new)
    nb = nsrc + nz

    def body(*refs):
        outs = refs[nb + ns + 1:]
        body_fn(refs[:nb], refs[nb:nb + ns], outs[:nn])
        outs[nn + nz][...] = jnp.zeros((SUBLANES, LANES), F32)

    res = pl.pallas_call(
        body, name=name,
        out_shape=([pltpu.SemaphoreType.DMA((n,)) for n in n_new] + [pltpu.HBM(b.shape, b.dtype) for b in zones]
                   + [jax.ShapeDtypeStruct((SUBLANES, LANES), F32)]),
        in_specs=[_HBM] * nb + [_SEM] * ns + [_ANY],
        out_specs=[_SEM] * nn + [_HBM] * nz + [pl.BlockSpec(memory_space=pltpu.VMEM)],
        input_output_aliases={nsrc + i: nn + i for i in range(nz)},
        compiler_params=pltpu.CompilerParams(has_side_effects=_ORDERED_EFFECT))(
            *[pltpu.with_memory_space_constraint(b, pltpu.HBM) for b in list(srcs) + list(zones)], *sems_in, after)
    return list(res[:nn]), list(res[nn:nn + nz]), res[-1]


def _mesh_peers():
    x, y, c = _place()
    return x, y, c, (x, y, 1 - c), [(1 - x, y), (x, 1 - y), (1 - x, 1 - y)]


def gather_start(shards, after, *, name):
    nw = len(shards)
    x, y, c = _place()
    zones = [lax.dynamic_update_slice(lax.empty((N_DEV,) + s.shape, s.dtype), s[None], (4 * x + 2 * y + c, 0, 0))
             for s in shards]

    def body(bufs, taken, new):
        for cp in _gather_first(bufs, nw, new[0], new[1]):
            cp.start()

    sems, zones, token = _split_call(name, shards, zones, [], [4 * nw, 4 * nw], body, after)
    return shards, sems, zones, token


def _gather_first(bufs, nw, send, recv):
    x, y, c, sibling, chips = _mesh_peers()
    out = []
    for w in range(nw):
        slot = bufs[nw + w].at[4 * x + 2 * y + c]
        for k, to in enumerate([sibling] + [(*ch, c) for ch in chips]):
            out.append(pltpu.make_async_remote_copy(
                src_ref=bufs[w], dst_ref=slot, send_sem=send.at[4 * w + k], recv_sem=recv.at[4 * w + k],
                device_id=to, device_id_type=MESH))
    return out


def _gather_slot_copy(bufs, nw, w, block, send_sem, recv_sem, to):
    px, py, pc = block
    slot = bufs[nw + w].at[4 * px + 2 * py + pc]
    return pltpu.make_async_remote_copy(src_ref=slot, dst_ref=slot, send_sem=send_sem, recv_sem=recv_sem,
                                        device_id=to, device_id_type=MESH)


def gather_forward(state, after, *, name):
    shards, sems, zones, _ = state
    nw = len(shards)

    def body(bufs, taken, new):
        x, y, c, sibling, chips = _mesh_peers()
        for j, ch in enumerate(chips):
            for w in range(nw):
                k = 4 * w + 1 + j
                _gather_slot_copy(bufs, nw, w, (*ch, c), taken[0].at[k], taken[1].at[k], (*ch, c)).wait_recv()
                _gather_slot_copy(bufs, nw, w, (*ch, c), new[0].at[3 * w + j], new[1].at[3 * w + j], sibling).start()
        for w in range(nw):
            _gather_slot_copy(bufs, nw, w, sibling, taken[0].at[4 * w], taken[1].at[4 * w], sibling).wait_recv()
        for cp in _gather_first(bufs, nw, taken[0], taken[1]):
            cp.wait_send()

    sems, zones, token = _split_call(name, shards, zones, sems, [3 * nw, 3 * nw], body, after)
    return shards, sems, zones, token


def gather_finish(state, after, *, name):
    shards, sems, zones, _ = state
    nw = len(shards)

    def body(bufs, taken, new):
        x, y, c, sibling, chips = _mesh_peers()
        for w in range(nw):
            for j, ch in enumerate(chips):
                cp = _gather_slot_copy(bufs, nw, w, (*ch, 1 - c), taken[0].at[3 * w + j], taken[1].at[3 * w + j], sibling)
                cp.wait_send()
                cp.wait_recv()

    _, zones, _ = _split_call(name, shards, zones, sems, [], body, after)
    return zones


def exchange_start(srcs, zone_shapes, copies, n, after, *, name):
    nw = len(srcs)
    zones = [lax.empty(z, s.dtype) for z, s in zip(zone_shapes, srcs)]

    def body(bufs, taken, new):
        for cp in copies(bufs[:nw], bufs[nw:], new[0], new[1]):
            cp.start()

    sems, zones, token = _split_call(name, srcs, zones, [], [n, n], body, after)
    return srcs, copies, sems, zones, token


def exchange_wait(state, after, *, name):
    srcs, copies, sems, zones, _ = state
    nw = len(srcs)

    def body(bufs, taken, new):
        for cp in copies(bufs[:nw], bufs[nw:], taken[0], taken[1]):
            cp.wait_send()
            cp.wait_recv()

    _, zones, _ = _split_call(name, srcs, zones, sems, [], body, after)
    return zones


def _core_copies(srcs, zones, send, recv):
    x, y, c = _place()
    return [pltpu.make_async_remote_copy(
        src_ref=srcs[w].at[:, 1 - c], dst_ref=zones[w], send_sem=send.at[w], recv_sem=recv.at[w],
        device_id=(x, y, 1 - c), device_id_type=MESH) for w in range(len(srcs))]


def _chip_copies(srcs, zones, send, recv):
    x, y, c = _place()
    chips = [(1 - x, y), (x, 1 - y), (1 - x, 1 - y)]
    return [pltpu.make_async_remote_copy(
        src_ref=srcs[w].at[2 * cx + cy], dst_ref=zones[w].at[j], send_sem=send.at[3 * w + j],
        recv_sem=recv.at[3 * w + j], device_id=(cx, cy, c), device_id_type=MESH)
        for w in range(len(srcs)) for j, (cx, cy) in enumerate(chips)]


def _blocked(fn, ins, outs, *, name, place=None, tr=256):
    k, n = outs[0][0]
    tr = _tile(k, tr, 16)
    if place is None:
        place = jnp.zeros((1,), jnp.int32)
    specs = []
    args = []
    for a in ins:
        if isinstance(a, tuple):
            arr, lead = a
            specs.append(pl.BlockSpec((None, tr, n), functools.partial(lambda i, s, lead: (*lead(i, s), 0), lead=lead)))
            args.append(arr)
        else:
            specs.append(pl.BlockSpec((tr, n), lambda i, s: (i, 0)))
            args.append(a)
    nin = len(args)

    def body(place_ref, *refs):
        res = fn(*[r[...] for r in refs[:nin]])
        for ref, val in zip(refs[nin:], res):
            ref[...] = val.astype(ref.dtype)

    return pl.pallas_call(
        body, out_shape=[jax.ShapeDtypeStruct(s, d) for s, d in outs],
        grid_spec=pltpu.PrefetchScalarGridSpec(
            num_scalar_prefetch=1, grid=(k // tr,), in_specs=specs,
            out_specs=[pl.BlockSpec((tr, n), lambda i, s: (i, 0)) for _ in outs]),
        compiler_params=_params(("parallel",)), name=name)(place, *args)


def _adamw(w, g, m, v):
    m = ADAM_B1 * m + (1.0 - ADAM_B1) * g
    v = ADAM_B2 * v + (1.0 - ADAM_B2) * (g * g)
    m_hat = m / (1.0 - ADAM_B1 ** ADAM_STEP)
    v_hat = v / (1.0 - ADAM_B2 ** ADAM_STEP)
    delta = -ADAM_LR * (m_hat * pl.reciprocal(jnp.sqrt(v_hat) + ADAM_EPS, approx=True) + ADAM_WD * w)
    return delta, m, v


def kernel(x, p, mix_norm_pre, w_in, lam_re, lam_im, log_dt, ssm_b_re, ssm_b_im, ssm_c_re, ssm_c_im, ssm_d, w_glu, b_glu, attn_out_norm, ssm_out_norm, w_out, mix_norm_post, mlp_norm_pre, w_up, w_down, mlp_norm_post, ple_norm_pre, w_ple_gate, w_ple_proj, ple_norm_post, loss_target, m_mix_norm_pre, m_w_in, m_lam_re, m_lam_im, m_log_dt, m_ssm_b_re, m_ssm_b_im, m_ssm_c_re, m_ssm_c_im, m_ssm_d, m_w_glu, m_b_glu, m_attn_out_norm, m_ssm_out_norm, m_w_out, m_mix_norm_post, m_mlp_norm_pre, m_w_up, m_w_down, m_mlp_norm_post, m_ple_norm_pre, m_w_ple_gate, m_w_ple_proj, m_ple_norm_post, v_mix_norm_pre, v_w_in, v_lam_re, v_lam_im, v_log_dt, v_ssm_b_re, v_ssm_b_im, v_ssm_c_re, v_ssm_c_im, v_ssm_d, v_w_glu, v_b_glu, v_attn_out_norm, v_ssm_out_norm, v_w_out, v_mix_norm_post, v_mlp_norm_pre, v_w_up, v_w_down, v_mlp_norm_post, v_ple_norm_pre, v_w_ple_gate, v_w_ple_proj, v_ple_norm_post):
    weights = dict(mix_norm_pre=mix_norm_pre, w_in=w_in, lam_re=lam_re, lam_im=lam_im, log_dt=log_dt, ssm_b_re=ssm_b_re, ssm_b_im=ssm_b_im, ssm_c_re=ssm_c_re, ssm_c_im=ssm_c_im, ssm_d=ssm_d, w_glu=w_glu, b_glu=b_glu, attn_out_norm=attn_out_norm, ssm_out_norm=ssm_out_norm, w_out=w_out, mix_norm_post=mix_norm_post, mlp_norm_pre=mlp_norm_pre, w_up=w_up, w_down=w_down, mlp_norm_post=mlp_norm_post, ple_norm_pre=ple_norm_pre, w_ple_gate=w_ple_gate, w_ple_proj=w_ple_proj, ple_norm_post=ple_norm_post)
    mom_m = dict(mix_norm_pre=m_mix_norm_pre, w_in=m_w_in, lam_re=m_lam_re, lam_im=m_lam_im, log_dt=m_log_dt, ssm_b_re=m_ssm_b_re, ssm_b_im=m_ssm_b_im, ssm_c_re=m_ssm_c_re, ssm_c_im=m_ssm_c_im, ssm_d=m_ssm_d, w_glu=m_w_glu, b_glu=m_b_glu, attn_out_norm=m_attn_out_norm, ssm_out_norm=m_ssm_out_norm, w_out=m_w_out, mix_norm_post=m_mix_norm_post, mlp_norm_pre=m_mlp_norm_pre, w_up=m_w_up, w_down=m_w_down, mlp_norm_post=m_mlp_norm_post, ple_norm_pre=m_ple_norm_pre, w_ple_gate=m_w_ple_gate, w_ple_proj=m_w_ple_proj, ple_norm_post=m_ple_norm_post)
    mom_v = dict(mix_norm_pre=v_mix_norm_pre, w_in=v_w_in, lam_re=v_lam_re, lam_im=v_lam_im, log_dt=v_log_dt, ssm_b_re=v_ssm_b_re, ssm_b_im=v_ssm_b_im, ssm_c_re=v_ssm_c_re, ssm_c_im=v_ssm_c_im, ssm_d=v_ssm_d, w_glu=v_w_glu, b_glu=v_b_glu, attn_out_norm=v_attn_out_norm, ssm_out_norm=v_ssm_out_norm, w_out=v_w_out, mix_norm_post=v_mix_norm_post, mlp_norm_pre=v_mlp_norm_pre, w_up=v_w_up, w_down=v_w_down, mlp_norm_post=v_mlp_norm_post, ple_norm_pre=v_ple_norm_pre, w_ple_gate=v_w_ple_gate, w_ple_proj=v_w_ple_proj, ple_norm_post=v_ple_norm_post)
    order = list(weights)
    big = ["w_in", "w_glu", "w_out", "w_up", "w_down", "w_ple_gate", "w_ple_proj"]
    col_sharded = {"w_in", "w_up", "w_ple_proj"}
    small = [n for n in order if n not in big]

    _, S, D = x.shape
    xs = x[0]
    tgt = loss_target[0]
    AW = attn_out_norm.shape[1]
    SW = ssm_d.shape[1]
    H = AW // HEAD_DIM
    G = SW // SSM_GROUP
    nslab = G // SLAB_GROUPS
    P_, C_ = SSM_STATE, SSM_GROUP

    shard = {n: weights[n][0].astype(BF16) for n in big}
    W, WT = {}, {}

    def arrived(names, gathered):
        for n, g in zip(names, gathered):
            W[n] = g if n in col_sharded else g.reshape(1, N_DEV * g.shape[1], g.shape[2])

    def transposed(g):
        return jnp.swapaxes(g, 1, 2).reshape(1, g.shape[0] * g.shape[2], g.shape[1])

    g1, g2, g3, g4, g5, g6 = (weights[n] for n in ("mix_norm_pre", "mix_norm_post", "mlp_norm_pre",
                                                      "mlp_norm_post", "ple_norm_pre", "ple_norm_post"))
    ga, gs = attn_out_norm, ssm_out_norm
    gather_in = gather_start([shard["w_in"]], shard["w_in"], name="gather_w_in_start")
    (hn1,) = rowwise(lambda a, g: (_rms(a, g),), [xs], [g1], [(D, BF16)], deps=(gather_in[-1],), name="norm_in")
    gather_in = gather_forward(gather_in, hn1, name="gather_w_in_forward")
    arrived(["w_in"], gather_finish(gather_in, gather_in[-1], name="gather_w_in_finish"))
    WT["w_in"] = transposed(W["w_in"])
    early, mid, late = ["w_glu", "w_out"], ["w_up"], ["w_down", "w_ple_gate", "w_ple_proj"]
    gather_early = gather_start([shard[n] for n in early], W["w_in"], name="gather_early_start")
    gather_mid = gather_start([shard[n] for n in mid], gather_early[-1], name="gather_mid_start")
    gather_late = gather_start([shard[n] for n in late], gather_mid[-1], name="gather_late_start")

    (proj,) = mm_nn(hn1, W["w_in"], [F32], deps=(gather_late[-1],), name="proj_in")
    attn, lse = attn_fwd(proj, H, name="attn_fwd")
    gather_early = gather_forward(gather_early, attn, name="gather_early_forward")
    (mix_a,) = rowwise(lambda a, g: (_rms(a, g),), [attn], [ga], [(AW, BF16)], deps=(gather_early[-1],),
                       name="attn_norm")
    arrived(early, gather_finish(gather_early, mix_a, name="gather_early_finish"))

    a_r, a_i, bb_r, bb_i = _discretise(lam_re[0], lam_im[0], log_dt[0], ssm_b_re[0], ssm_b_im[0])
    ssm_consts = (_block_diag(bb_r.swapaxes(1, 2), nslab).astype(BF16), _block_diag(bb_i.swapaxes(1, 2), nslab).astype(BF16),
                  a_r.reshape(nslab, 1, SLAB_STATES), a_i.reshape(nslab, 1, SLAB_STATES),
                  _block_diag(ssm_c_re[0].swapaxes(1, 2), nslab).astype(BF16),
                  _block_diag(ssm_c_im[0].swapaxes(1, 2), nslab).astype(BF16), ssm_d)
    u_seg = _to_segments(proj[:, 3 * AW:]).astype(BF16)
    y_pre, yg, st_r, st_i = ssm_fwd(u_seg, *ssm_consts, name="ssm_fwd")
    gather_mid = gather_forward(gather_mid, y_pre, name="gather_mid_forward")
    (gl1,) = mm_nn(yg, W["w_glu"], [BF16], epi=lambda acc, b: (acc + b,), bias=b_glu, deps=(gather_mid[-1],),
                   name="glu_gate")
    (mix_s,) = rowwise(lambda yp, gl, g: (_rms(_gelu(yp) * _sigmoid(gl), g),), [y_pre, gl1], [gs], [(SW, BF16)],
                       name="ssm_glu_norm")
    mixed = [mix_a, _from_segments(mix_s)]
    (mo,) = mm_nn(mixed, W["w_out"], [BF16], tm=1024, name="mix_out")

    def resid_norm(h, t, gpost, gpre):
        hh = h + _rms(t, gpost)
        return hh, _rms(hh, gpre)

    h1, hn2 = rowwise(resid_norm, [xs, mo], [g2, g3], [(D, F32), (D, BF16)], name="resid_mix")
    arrived(mid, gather_finish(gather_mid, hn2, name="gather_mid_finish"))
    gather_late = gather_forward(gather_late, W["w_up"], name="gather_late_forward")
    WT["w_up"] = transposed(W["w_up"])

    def relu2(acc):
        r = jnp.maximum(acc, 0.0)
        return acc, r * r

    up, act = mm_nn(hn2, W["w_up"], [BF16, BF16], epi=relu2, deps=(gather_late[-1],), tm=1024, tn=1024, name="mlp_up")
    arrived(late, gather_finish(gather_late, act, name="gather_late_finish"))
    (ff,) = mm_nn(act, W["w_down"], [BF16], name="mlp_down")
    h2, hn3 = rowwise(resid_norm, [h1, ff], [g4, g5], [(D, F32), (D, BF16)], name="resid_mlp")
    (gl2,) = mm_nn(hn3, W["w_ple_gate"], [BF16], tm=1024, name="ple_gate")
    pb = p[0, 0].astype(BF16)
    (emb,) = mm_nn(pb, W["w_ple_proj"], [BF16], name="ple_proj")

    def head(h, gl, e, t, g):
        sg = _sigmoid(gl)
        ge = sg * e
        err = h + _rms(ge, g) - t
        dh = err * (1.0 / D)
        dge, dg = _rms_bwd(dh, ge, g)
        return dh, dge * e * sg * (1.0 - sg), dge * sg, jnp.sum(err * err, axis=0, keepdims=True), dg

    dh3, dgl2, demb, loss_part, dg6 = rowwise(head, [h2, gl2, emb, tgt], [g6], [(D, F32), (D, BF16), (D, BF16)],
                                             [D, D], name="ple_loss_head")
    loss = lax.psum(0.5 / D * jnp.sum(loss_part), ("x", "y", "c"))

    x_i, y_i, c_i = _place()
    place = jnp.stack([c_i, 2 * x_i + y_i]).astype(jnp.int32)
    grads, out_g, out_d, out_m, out_v = {}, {}, {}, {}, {}

    def to_sibling(names, after, tag):
        chunks = []
        for n in names:
            g = grads[n]
            g = g if n in col_sharded else g.reshape(N_DEV, g.shape[1] // N_DEV, g.shape[2])
            chunks.append(g.reshape(4, 2, g.shape[1], g.shape[2]))
        return chunks, exchange_start(chunks, [(4,) + g.shape[2:] for g in chunks], _core_copies, len(chunks), after,
                                      name=f"grads_to_sibling_{tag}")

    def to_chips(names, sent, after, tag):
        chunks, state = sent
        sums = []
        for n, g, r in zip(names, chunks, exchange_wait(state, after, name=f"grads_from_sibling_{tag}")):
            k, nn = g.shape[2], g.shape[3]
            kb = k // _tile(k, 512, 16)

            def mine(i, s, kb=kb):
                return 2 * (i // kb) + s[0], i % kb

            (s,) = _blocked(lambda a, b: (a.astype(F32) + b.astype(F32),),
                            [(g.reshape(N_DEV, k, nn), mine), r.reshape(4 * k, nn)],
                            [((4 * k, nn), BF16)], place=place, tr=k // kb, name=f"chip_sum_{n}")
            sums.append(s.reshape(4, k, nn))
        return sums, exchange_start(sums, [(3,) + s.shape[1:] for s in sums], _chip_copies, 3 * len(sums), sums[-1],
                                    name=f"grads_to_chips_{tag}")

    def update(w_, m_, v_, own, r0, r1, r2):
        g = own.astype(F32) + r0.astype(F32) + r1.astype(F32) + r2.astype(F32)
        return (g,) + _adamw(w_, g, m_, v_)

    def finish(names, sent, after, tag):
        sums, state = sent
        for n, s, r in zip(names, sums, exchange_wait(state, after, name=f"grads_from_chips_{tag}")):
            shp = weights[n].shape
            res = _blocked(update, [weights[n][0], mom_m[n][0], mom_v[n][0], (s, lambda i, p_: (p_[1], i)),
                                    (r, lambda i, p_: (0, i)), (r, lambda i, p_: (1, i)), (r, lambda i, p_: (2, i))],
                           [(shp[1:], F32)] * 4, place=place, tr=max(16, min(shp[1] // 8, UPDATE_BLOCK // shp[2])),
                           name=f"adamw_{n}")
            out_g[n], out_d[n], out_m[n], out_v[n] = (t.reshape(shp) for t in res)
        return out_v[names[-1]]

    grads["w_ple_proj"] = mm_tn(pb, demb, N_DEV, name="grad_w_ple_proj")
    dhn3 = mm_nt(dgl2, W["w_ple_gate"], BF16, tm=1024, name="back_ple_gate")
    grads["w_ple_gate"] = mm_tn(hn3, dgl2, 1, ts=1024, name="grad_w_ple_gate")

    def back_resid(dh, dhn, h, t, gpre, gpost):
        d1, dgpre = _rms_bwd(dhn, h, gpre)
        dhh = dh + d1
        dt, dgpost = _rms_bwd(dhh, t, gpost)
        return dhh, dt, dgpre, dgpost

    dh2, dff, dg5, dg4 = rowwise(back_resid, [dh3, dhn3, h2, ff], [g5, g4], [(D, F32), (D, BF16)], [D, D],
                                 name="back_resid_mlp")
    dup = mm_nt(dff, W["w_down"], BF16, epi=lambda acc, u_: (acc * 2.0 * jnp.maximum(u_.astype(F32), 0.0),),
                extra=up, name="back_mlp_down")
    grads["w_down"] = mm_tn(act, dff, 1, name="grad_w_down")
    group_a = ["w_ple_proj", "w_ple_gate", "w_down"]
    sent_a = to_sibling(group_a, grads["w_down"], "a")
    (dhn2,) = mm_nn(dup, WT["w_up"], [BF16], deps=(sent_a[1][-1],), name="back_mlp_up")
    sent_a = to_chips(group_a, sent_a, dhn2, "a")
    grads["w_up"] = mm_tn(hn2, dup, N_DEV, deps=(sent_a[1][-1],), name="grad_w_up")
    dh1, dmo, dg3, dg2 = rowwise(back_resid, [dh2, dhn2, h1, mo], [g3, g2], [(D, F32), (D, BF16)], [D, D],
                                 name="back_resid_mix")
    dmixed = mm_nt(dmo, W["w_out"], BF16, tm=1024, name="back_mix_out")
    grads["w_out"] = mm_tn(mixed, dmo, 1, ts=1024, name="grad_w_out")

    def back_glu(dm, yp, gl, g):
        ygf = _gelu(yp)
        sg = _sigmoid(gl)
        dssm, dg = _rms_bwd(dm, ygf * sg, g)
        dgl = dssm * ygf * sg * (1.0 - sg)
        return dgl, dssm * sg, dg, jnp.sum(dgl, axis=0, keepdims=True)

    dgl1, dyg_direct, dgs, db_glu = rowwise(back_glu, [_to_segments(dmixed[:, AW:]), y_pre, gl1], [gs],
                                            [(SW, BF16), (SW, F32)], [SW, SW], name="back_glu")
    dyg_gate = mm_nt(dgl1, W["w_glu"], BF16, name="back_glu_gate")
    grads["w_glu"] = mm_tn(yg, dgl1, 1, name="grad_w_glu")
    group_b = ["w_up", "w_out", "w_glu"]
    sent_b = to_sibling(group_b, grads["w_glu"], "b")
    done_a = finish(group_a, sent_a, sent_b[1][-1], "a")

    transposed_consts = [jnp.swapaxes(t, 1, 2) if t.ndim == 3 and t.shape[1] != 1 else t for t in ssm_consts]
    du_seg, dbb_r, dbb_i, dcb_r, dcb_i, da_r, da_i, d_skip = ssm_bwd(
        u_seg, dyg_direct, dyg_gate, y_pre, st_r, st_i, *transposed_consts, deps=(done_a,), name="ssm_bwd")
    sent_b = to_chips(group_b, sent_b, du_seg, "b")

    def back_attn_norm(dm, a, l, g):
        da, dg = _rms_bwd(dm, a, g)
        prod = da * a
        delta = jnp.concatenate(
            [jnp.broadcast_to(jnp.sum(prod[:, h * HEAD_DIM:(h + 1) * HEAD_DIM], axis=-1, keepdims=True),
                              (prod.shape[0], HEAD_DIM)) for h in range(H)], axis=1)
        first_half = lax.broadcasted_iota(jnp.int32, l.shape, 1) % HEAD_DIM < HEAD_DIM // 2
        return da, jnp.where(first_half, l, delta), dg

    dattn, stats, dga = rowwise(back_attn_norm, [(dmixed, AW, 0), attn, lse], [ga], [(AW, F32), (AW, F32)], [AW],
                                deps=(sent_b[1][-1],), name="back_attn_norm")
    dq, dk, dv = attn_bwd(proj, dattn, stats, H, name="attn_bwd")
    dproj = [dq, dk, dv, _from_segments(du_seg)]
    (dhn1,) = mm_nn(dproj, WT["w_in"], [BF16], name="back_proj_in")

    def back_in(dh, dhn, a, g):
        d1, dg = _rms_bwd(dhn, a, g)
        return dh + d1, dg

    grad_x, dg1 = rowwise(back_in, [dh1, dhn1, xs], [g1], [(D, F32)], [D], name="back_norm_in")

    cot = dict(
        mix_norm_pre=dg1, mix_norm_post=dg2, mlp_norm_pre=dg3, mlp_norm_post=dg4, ple_norm_pre=dg5, ple_norm_post=dg6,
        attn_out_norm=dga, ssm_out_norm=dgs, b_glu=db_glu, ssm_d=d_skip,
        ssm_c_re=_block_diag_part(dcb_r, P_, C_).swapaxes(1, 2), ssm_c_im=_block_diag_part(dcb_i, P_, C_).swapaxes(1, 2),
        a_r=da_r.reshape(G, P_), a_i=da_i.reshape(G, P_),
        bb_r=_block_diag_part(dbb_r, C_, P_).swapaxes(1, 2), bb_i=_block_diag_part(dbb_i, C_, P_).swapaxes(1, 2))
    names = list(cot)
    flat = jnp.concatenate([cot[n].reshape(-1) for n in names])
    total = flat.shape[0]
    rows_ = -(-total // (LANES * 16)) * 16
    flat = jnp.pad(flat, (0, rows_ * LANES - total)).reshape(rows_, LANES)
    gather_small = gather_start([flat], flat, name="gather_small_start")
    grads["w_in"] = mm_tn(hn1, dproj, N_DEV, deps=(gather_small[-1],), tko=2048, name="grad_w_in")
    group_c = ["w_in"]
    sent_c = to_sibling(group_c, grads["w_in"], "c")
    done_b = finish(group_b, sent_b, sent_c[1][-1], "b")
    sent_c = to_chips(group_c, sent_c, done_b, "c")
    gather_small = gather_forward(gather_small, sent_c[1][-1], name="gather_small_forward")
    (every,) = gather_finish(gather_small, gather_small[-1], name="gather_small_finish")
    (summed,) = _blocked(lambda *t: (functools.reduce(lambda a, b: a + b, t),),
                         [(every, functools.partial(lambda i, p_, j: (j, i), j=j)) for j in range(N_DEV)],
                         [((rows_, LANES), F32)], name="sum_small_grads")
    summed = summed.reshape(-1)
    red, off = {}, 0
    for n in names:
        sz = cot[n].size
        red[n] = summed[off:off + sz].reshape(cot[n].shape)
        off += sz
    _, pull = jax.vjp(_discretise, lam_re[0], lam_im[0], log_dt[0], ssm_b_re[0], ssm_b_im[0])
    d_lre, d_lim, d_ldt, d_bre, d_bim = pull((red["a_r"], red["a_i"], red["bb_r"], red["bb_i"]))
    red.update(lam_re=d_lre, lam_im=d_lim, log_dt=d_ldt, ssm_b_re=d_bre, ssm_b_im=d_bim)

    def pack(d):
        t = jnp.concatenate([d[n].reshape(-1) for n in small])
        r_ = -(-t.shape[0] // (LANES * 16)) * 16
        return jnp.pad(t, (0, r_ * LANES - t.shape[0])).reshape(r_, LANES)

    sw, sg_, sm, sv = pack(weights), pack(red), pack(mom_m), pack(mom_v)
    sd, snm, snv = _blocked(lambda w_, g_, m_, v_: _adamw(w_, g_, m_, v_), [sw, sg_, sm, sv],
                            [(sw.shape, F32)] * 3, name="adamw_small")
    finish(group_c, sent_c, snv, "c")
    off = 0
    for n in small:
        sz = weights[n].size
        shp = weights[n].shape
        out_g[n] = red[n].reshape(shp)
        out_d[n] = sd.reshape(-1)[off:off + sz].reshape(shp)
        out_m[n] = snm.reshape(-1)[off:off + sz].reshape(shp)
        out_v[n] = snv.reshape(-1)[off:off + sz].reshape(shp)
        off += sz

    return (loss, grad_x[None], *[out_g[n] for n in order], *[out_d[n] for n in order],
            *[out_m[n] for n in order], *[out_v[n] for n in order])
```

```python
import functools
import math

import jax
import jax.numpy as jnp
from jax import lax
from jax.experimental import pallas as pl
from jax.experimental.pallas import tpu as pltpu

F32 = jnp.float32
BF16 = jnp.bfloat16
MESH = pl.DeviceIdType.MESH

N_DEV = 8
LANES = 128
SUBLANES = 8
VMEM_LIMIT = 48 * 1024 * 1024
VMEM_LIMIT_SCAN = 60 * 1024 * 1024

HEAD_DIM = 128
BLK = 128
DILATIONS = (1, 4, 16)
SSM_GROUP = 16
SSM_STATE = 64
SLAB_GROUPS = LANES // SSM_GROUP
SLAB_STATES = SLAB_GROUPS * SSM_STATE
SEGMENTS = SUBLANES
SCAN_UNROLL = 4
RMS_EPS = 1e-6
NEG_INF = -1e30

ADAM_LR = 0.001
ADAM_B1 = 0.9
ADAM_B2 = 0.999
ADAM_EPS = 1e-08
ADAM_WD = 0.01
ADAM_STEP = 10
UPDATE_BLOCK = 256 * 1024


def _tile(n, pref, unit=LANES):
    if n <= pref:
        return n
    t = (pref // unit) * unit
    while t > unit and n % t:
        t -= unit
    assert n % t == 0, (n, pref, unit)
    return t


def _params(sem=None, vmem=VMEM_LIMIT):
    return pltpu.CompilerParams(dimension_semantics=sem, vmem_limit_bytes=vmem)


_NN = (((1,), (0,)), ((), ()))
_NT = (((1,), (1,)), ((), ()))
_TN = (((0,), (0,)), ((), ()))


_ANY = pl.BlockSpec(memory_space=pl.ANY)


def _mm_call(dims, nk, na, nb, pick, n_extra, n_dep, n_out, epi, group=1, **kw):
    first_extra = na + nb
    first_out = first_extra + n_extra + n_dep
    kw["in_specs"] = list(kw["in_specs"]) + [_ANY] * n_dep

    def grouped(refs, step):
        if group == 1:
            return lax.dot_general(refs[0][...], refs[1][...], dims, preferred_element_type=F32)
        kp = refs[0].shape[1]
        return sum(lax.dot_general(refs[step * group + p][...], refs[na][pl.ds(p * kp, kp), :], dims,
                                   preferred_element_type=F32) for p in range(group))

    def single(*refs):
        extra = refs[first_extra:first_extra + n_extra]
        res = epi(grouped(refs, 0), *[e[...] for e in extra])
        for o, r in zip(refs[first_out:first_out + n_out], res):
            o[...] = r.astype(o.dtype)

    if nk == 1:
        assert na == group and nb == 1
        kw["scratch_shapes"] = []
        return pl.pallas_call(single, **kw)

    def body(*refs):
        extra = refs[first_extra:first_extra + n_extra]
        outs = refs[first_out:first_out + n_out]
        acc = refs[-1]
        k = pl.program_id(2)

        @pl.when(k == 0)
        def _():
            acc[...] = jnp.zeros_like(acc)

        def add(a_ref, b_ref):
            acc[...] += lax.dot_general(a_ref[...], b_ref[...], dims, preferred_element_type=F32)

        if na == nb == 1:
            add(refs[0], refs[1])
        elif group > 1:
            for step in range(nk):
                @pl.when(k == step)
                def _(step=step):
                    acc[...] += grouped(refs, step)
        else:
            pa, pb = pick(pl.program_id(0), pl.program_id(1), k)
            for x in range(na):
                for y in range(nb):
                    pl.when((pa == x) & (pb == y))(functools.partial(add, refs[x], refs[na + y]))

        @pl.when(k == nk - 1)
        def _():
            res = epi(acc[...], *[e[...] for e in extra])
            for o, r in zip(outs, res):
                o[...] = r.astype(o.dtype)

    return pl.pallas_call(body, **kw)


def _identity_epi(acc):
    return (acc,)


def _parts(t):
    return list(t) if isinstance(t, (list, tuple)) else [t]


def _part_spec(block, part, which, index):
    def index_map(i, j, k):
        use = which(i, j, k) == part
        r, c = index(i, j, k)
        return jnp.where(use, r, 0), jnp.where(use, c, 0)
    return pl.BlockSpec(block, index_map)


def mm_nn(a, w, out_dtypes, *, name, epi=_identity_epi, bias=None, deps=(), tm=2048, tn=512, tk=2048):
    a = _parts(a)
    M, Kp = a[0].shape
    K = Kp * len(a)
    J, K2, n = w.shape
    assert K == K2
    tm, tn, tk = _tile(M, tm, 16), _tile(n, tn), _tile(K, tk)
    npj = n // tn
    nk = K // tk
    group = 1
    if len(a) == 1:
        in_specs = [pl.BlockSpec((tm, tk), lambda i, j, k: (i, k))]
    else:
        assert tk % Kp == 0
        group = tk // Kp
        in_specs = [pl.BlockSpec((tm, Kp), lambda i, j, k: (i, 0)) for _ in a]
    in_specs.append(pl.BlockSpec((None, tk, tn), lambda i, j, k: (j // npj, k, j % npj)))
    args = a + [w]
    if bias is not None:
        in_specs.append(pl.BlockSpec((1, tn), lambda i, j, k: (0, j)))
        args.append(bias)
    return _mm_call(
        _NN, nk, len(a), 1, lambda i, j, k: (k, 0), len(args) - len(a) - 1, len(deps), len(out_dtypes), epi, group,
        out_shape=[jax.ShapeDtypeStruct((M, J * n), d) for d in out_dtypes],
        grid=(M // tm, J * npj, nk), in_specs=in_specs,
        out_specs=[pl.BlockSpec((tm, tn), lambda i, j, k: (i, j)) for _ in out_dtypes],
        scratch_shapes=[pltpu.VMEM((tm, tn), F32)],
        compiler_params=_params(("parallel", "parallel", "arbitrary")), name=name)(*args, *deps)


def mm_nt(a, w, out_dtype, *, name, epi=_identity_epi, extra=None, tm=2048, tko=512, tnr=2048):
    M, N = a.shape
    J, K, n = w.shape
    assert N == J * n
    tm, tko, tnr = _tile(M, tm, 16), _tile(K, tko), _tile(n, tnr)
    npj = n // tnr
    nk = N // tnr
    in_specs = [pl.BlockSpec((tm, tnr), lambda i, j, k: (i, k)),
                pl.BlockSpec((None, tko, tnr), lambda i, j, k: (k // npj, j, k % npj))]
    args = [a, w]
    if extra is not None:
        in_specs.append(pl.BlockSpec((tm, tko), lambda i, j, k: (i, j)))
        args.append(extra)
    return _mm_call(
        _NT, nk, 1, 1, None, len(args) - 2, 0, 1, epi,
        out_shape=[jax.ShapeDtypeStruct((M, K), out_dtype)],
        grid=(M // tm, K // tko, nk), in_specs=in_specs,
        out_specs=[pl.BlockSpec((tm, tko), lambda i, j, k: (i, j))],
        scratch_shapes=[pltpu.VMEM((tm, tko), F32)],
        compiler_params=_params(("parallel", "parallel", "arbitrary")), name=name)(*args)[0]


def mm_tn(a, b, J, *, name, deps=(), tko=1024, tn=1024, ts=2048):
    a, b = _parts(a), _parts(b)
    S, Kp = a[0].shape
    S2, Np = b[0].shape
    K, N = Kp * len(a), Np * len(b)
    assert S == S2 and N % J == 0
    n = N // J
    tko, tn, ts = _tile(Kp, tko), _tile(math.gcd(n, Np), tn), _tile(S, ts)
    npj = n // tn
    nk = S // ts
    ta, tb = Kp // tko, Np // tn
    assert nk > 1 or len(a) == len(b) == 1
    return _mm_call(
        _TN, nk, len(a), len(b), lambda i, j, k: (i // ta, j // tb), 0, len(deps), 1, _identity_epi,
        out_shape=[jax.ShapeDtypeStruct((J, K, n), BF16)],
        grid=(K // tko, J * npj, nk),
        in_specs=([_part_spec((ts, tko), x, lambda i, j, k: i // ta, lambda i, j, k: (k, i % ta)) for x in range(len(a))]
                  + [_part_spec((ts, tn), y, lambda i, j, k: j // tb, lambda i, j, k: (k, j % tb)) for y in range(len(b))]),
        out_specs=[pl.BlockSpec((None, tko, tn), lambda i, j, k: (j // npj, i, j % npj))],
        scratch_shapes=[pltpu.VMEM((tko, tn), F32)],
        compiler_params=_params(("parallel", "parallel", "arbitrary")), name=name)(*a, *b, *deps)[0]


def rowwise(fn, rows, vecs, outs, accs=(), *, name, deps=(), ts=256):
    rows = [r if isinstance(r, tuple) else (r, r.shape[1], 0) for r in rows]
    S = rows[0][0].shape[0]
    ts = _tile(S, ts, 16)
    nr, nv, no, nd = len(rows), len(vecs), len(outs), len(deps)

    def body(*refs):
        r, v = refs[:nr], refs[nr:nr + nv]
        o, a = refs[nr + nv + nd:nr + nv + nd + no], refs[nr + nv + nd + no:]
        res = fn(*[t[...].astype(F32) for t in r], *[t[...] for t in v])
        for ref, val in zip(o, res[:no]):
            ref[...] = val.astype(ref.dtype)
        if a:
            @pl.when(pl.program_id(0) == 0)
            def _():
                for ref in a:
                    ref[...] = jnp.zeros_like(ref)

            for ref, val in zip(a, res[no:]):
                ref[...] += val

    in_specs = [pl.BlockSpec((ts, w), functools.partial(lambda i, cb: (i, cb), cb=cb)) for _, w, cb in rows]
    in_specs += [pl.BlockSpec(v.shape, lambda i: (0, 0)) for v in vecs] + [_ANY] * nd
    out_shape = [jax.ShapeDtypeStruct((S, w), d) for w, d in outs]
    out_shape += [jax.ShapeDtypeStruct((1, w), F32) for w in accs]
    out_specs = [pl.BlockSpec((ts, w), lambda i: (i, 0)) for w, _ in outs]
    out_specs += [pl.BlockSpec((1, w), lambda i: (0, 0)) for w in accs]
    return pl.pallas_call(body, out_shape=out_shape, grid=(S // ts,), in_specs=in_specs, out_specs=out_specs,
                          compiler_params=_params(("arbitrary",)), name=name)(*[r[0] for r in rows], *vecs, *deps)


def _rms(x, g):
    r = lax.rsqrt(jnp.mean(x * x, axis=-1, keepdims=True) + RMS_EPS)
    return x * r * g


def _rms_bwd(dy, x, g):
    r = lax.rsqrt(jnp.mean(x * x, axis=-1, keepdims=True) + RMS_EPS)
    xh = x * r
    dxh = dy * g
    dx = r * (dxh - xh * jnp.mean(dxh * xh, axis=-1, keepdims=True))
    return dx, jnp.sum(dy * xh, axis=0, keepdims=True)


def _sigmoid(x):
    return pl.reciprocal(1.0 + jnp.exp(-x), approx=True)


_GELU_C = math.sqrt(2.0 / math.pi)


def _gelu(x):
    return 0.5 * x * (1.0 + jnp.tanh(_GELU_C * (x + 0.044715 * x * x * x)))


def _gelu_grad(x):
    t = jnp.tanh(_GELU_C * (x + 0.044715 * x * x * x))
    return 0.5 * (1.0 + t) + 0.5 * x * (1.0 - t * t) * _GELU_C * (1.0 + 3.0 * 0.044715 * x * x)


ATTN_INTERLEAVE = 8
KEY_PAD = BLK * max(DILATIONS)


def _key_mask(n):
    ii = lax.broadcasted_iota(jnp.int32, (BLK, 2 * BLK), 0)
    jj = lax.broadcasted_iota(jnp.int32, (BLK, 2 * BLK), 1)
    return ((jj < BLK) & (jj >= ii) & (n > 0)) | ((jj >= BLK) & (jj - BLK <= ii))


def _units(d, nblk):
    nb = nblk // d
    if nb == 2:
        def unit(idx):
            ii = lax.broadcasted_iota(jnp.int32, (2 * BLK, 2 * BLK), 0)
            jj = lax.broadcasted_iota(jnp.int32, (2 * BLK, 2 * BLK), 1)
            return pl.ds(idx, 2 * BLK, stride=d), pl.ds(KEY_PAD + idx, 2 * BLK, stride=d), (jj <= ii) & (ii - jj <= BLK)
        return d, max(1, ATTN_INTERLEAVE // 4), unit

    def unit(idx):
        r, n = idx // nb, idx % nb
        cur = r + n * (BLK * d)
        keys = cur + (KEY_PAD - BLK * d)
        if d == 1:
            return pl.ds(pl.multiple_of(cur, BLK), BLK), pl.ds(pl.multiple_of(keys, BLK), 2 * BLK), _key_mask(n)
        return pl.ds(cur, BLK, stride=d), pl.ds(keys, 2 * BLK, stride=d), _key_mask(n)
    return nblk, ATTN_INTERLEAVE, unit


def _pad_keys(dst, src):
    dst[pl.ds(0, KEY_PAD), :] = jnp.zeros((KEY_PAD, dst.shape[1]), F32)

    def copy(c, carry):
        dst[pl.ds(pl.multiple_of(KEY_PAD + c * BLK, BLK), BLK), :] = src[pl.ds(pl.multiple_of(c * BLK, BLK), BLK), :]
        return carry

    lax.fori_loop(0, src.shape[0] // BLK, copy, 0)


def attn_fwd(proj, n_heads, *, name):
    S, WP = proj.shape
    assert S % (BLK * max(DILATIONS)) == 0
    nblk = S // BLK
    AW = n_heads * HEAD_DIM
    scale = 1.0 / math.sqrt(HEAD_DIM)

    def body(q_ref, k_ref, v_ref, o_ref, l_ref, acc, mrun, lrun, kp, vp):
        _pad_keys(kp, k_ref)
        _pad_keys(vp, v_ref)
        for first, d in zip((True, False, False), reversed(DILATIONS)):
            n_units, per_step, unit = _units(d, nblk)

            def step(it, carry, first=first, n_units=n_units, per_step=per_step, unit=unit):
                units = [unit(it + j * (n_units // per_step)) for j in range(per_step)]
                ss = [lax.dot_general(q_ref[cur, :].astype(BF16), kp[keys, :].astype(BF16), _NT,
                                      preferred_element_type=F32) * scale for cur, keys, _ in units]
                ss = [jnp.where(mask, s, NEG_INF) for s, (_, _, mask) in zip(ss, units)]
                ms = [jnp.max(s, axis=-1, keepdims=True) for s in ss]
                ps = [jnp.exp(s - m) for s, m in zip(ss, ms)]
                ls = [jnp.sum(p, axis=-1, keepdims=True) for p in ps]
                os_ = [jnp.dot(p.astype(BF16), vp[keys, :].astype(BF16), preferred_element_type=F32)
                       for p, (_, keys, _) in zip(ps, units)]
                for (cur, keys, mask), m, l, o in zip(units, ms, ls, os_):
                    m = jnp.broadcast_to(m, o.shape)
                    l = jnp.broadcast_to(l, o.shape)
                    if first:
                        acc[cur, :], mrun[cur, :], lrun[cur, :] = o, m, l
                    else:
                        m_old = mrun[cur, :]
                        m_new = jnp.maximum(m_old, m)
                        w_old, w_blk = jnp.exp(m_old - m_new), jnp.exp(m - m_new)
                        acc[cur, :] = w_old * acc[cur, :] + w_blk * o
                        lrun[cur, :] = w_old * lrun[cur, :] + w_blk * l
                        mrun[cur, :] = m_new
                return carry

            lax.fori_loop(0, n_units // per_step, step, 0)

        def finish(c, carry):
            r = pl.ds(pl.multiple_of(c * BLK, BLK), BLK)
            o_ref[r, :] = acc[r, :] / lrun[r, :]
            l_ref[r, :] = mrun[r, :] + jnp.log(lrun[r, :])
            return carry

        lax.fori_loop(0, nblk, finish, 0)

    def col(off):
        return pl.BlockSpec((S, HEAD_DIM), lambda h: (0, off + h))

    ospec = pl.BlockSpec((S, HEAD_DIM), lambda h: (0, h))
    return pl.pallas_call(
        body, out_shape=[jax.ShapeDtypeStruct((S, AW), F32)] * 2, grid=(n_heads,),
        in_specs=[col(0), col(n_heads), col(2 * n_heads)], out_specs=[ospec, ospec],
        scratch_shapes=[pltpu.VMEM((S, HEAD_DIM), F32)] * 3 + [pltpu.VMEM((KEY_PAD + S, HEAD_DIM), F32)] * 2,
        compiler_params=_params(("parallel",)), name=name)(proj, proj, proj)


def attn_bwd(proj, do, stats, n_heads, *, name):
    S, WP = proj.shape
    nblk = S // BLK
    AW = n_heads * HEAD_DIM
    scale = 1.0 / math.sqrt(HEAD_DIM)

    def body(q_ref, k_ref, v_ref, do_ref, st_ref, dq_ref, dk_ref, dv_ref, dq_sc, dk_sc, dv_sc, kp, vp):
        _pad_keys(kp, k_ref)
        _pad_keys(vp, v_ref)
        order = list(reversed(DILATIONS))
        assign_first = nblk // order[0] == 2
        if assign_first:
            dk_sc[pl.ds(0, KEY_PAD), :] = jnp.zeros((KEY_PAD, HEAD_DIM), F32)
            dv_sc[pl.ds(0, KEY_PAD), :] = jnp.zeros((KEY_PAD, HEAD_DIM), F32)
        else:
            dq_sc[...] = jnp.zeros_like(dq_sc)
            dk_sc[...] = jnp.zeros_like(dk_sc)
            dv_sc[...] = jnp.zeros_like(dv_sc)
        for assign, d in zip((assign_first, False, False), order):
            n_units, per_step, unit = _units(d, nblk)

            def step(it, carry, n_units=n_units, per_step=per_step, unit=unit, assign=assign):
                units = [unit(it + j * (n_units // per_step)) for j in range(per_step)]
                qs = [q_ref[cur, :].astype(BF16) for cur, _, _ in units]
                gs = [do_ref[cur, :].astype(BF16) for cur, _, _ in units]
                ks = [kp[keys, :].astype(BF16) for _, keys, _ in units]
                ss = [lax.dot_general(q, kb, _NT, preferred_element_type=F32) * scale for q, kb in zip(qs, ks)]
                dps = [lax.dot_general(g, vp[keys, :].astype(BF16), _NT, preferred_element_type=F32)
                       for g, (_, keys, _) in zip(gs, units)]
                sts = [st_ref[cur, :] for cur, _, _ in units]
                ps = [jnp.where(mask, jnp.exp(s - st[:, :1]), 0.0) for s, st, (_, _, mask) in zip(ss, sts, units)]
                dss = [(p * (dp - st[:, HEAD_DIM // 2:HEAD_DIM // 2 + 1]) * scale).astype(BF16)
                       for p, dp, st in zip(ps, dps, sts)]
                for (cur, keys, _), q, g, kb, p, ds in zip(units, qs, gs, ks, ps, dss):
                    dq = jnp.dot(ds, kb, preferred_element_type=F32)
                    dk = lax.dot_general(ds, q, _TN, preferred_element_type=F32)
                    dv = lax.dot_general(p.astype(BF16), g, _TN, preferred_element_type=F32)
                    if assign:
                        dq_sc[cur, :], dk_sc[keys, :], dv_sc[keys, :] = dq, dk, dv
                    else:
                        dq_sc[cur, :] += dq
                        dk_sc[keys, :] += dk
                        dv_sc[keys, :] += dv
                return carry

            lax.fori_loop(0, n_units // per_step, step, 0)
        rows = pl.ds(KEY_PAD, S)
        dq_ref[...] = dq_sc[...].astype(BF16)
        dk_ref[...] = dk_sc[rows, :].astype(BF16)
        dv_ref[...] = dv_sc[rows, :].astype(BF16)

    def col(off):
        return pl.BlockSpec((S, HEAD_DIM), lambda h: (0, off + h))

    ospec = pl.BlockSpec((S, HEAD_DIM), lambda h: (0, h))
    return pl.pallas_call(
        body, out_shape=[jax.ShapeDtypeStruct((S, AW), BF16)] * 3, grid=(n_heads,),
        in_specs=[col(0), col(n_heads), col(2 * n_heads), ospec, ospec], out_specs=[ospec] * 3,
        scratch_shapes=[pltpu.VMEM((S, HEAD_DIM), F32)] + [pltpu.VMEM((KEY_PAD + S, HEAD_DIM), F32)] * 4,
        compiler_params=_params(("parallel",), VMEM_LIMIT_SCAN), name=name)(proj, proj, proj, do, stats)


def _to_segments(t):
    S, W = t.shape
    return t.reshape(SEGMENTS, S // SEGMENTS, W).swapaxes(0, 1).reshape(S, W)


def _from_segments(t):
    S, W = t.shape
    return t.reshape(S // SEGMENTS, SEGMENTS, W).swapaxes(0, 1).reshape(S, W)


def _cmul(ar, ai, br, bi):
    return ar * br - ai * bi, ar * bi + ai * br


def _power(ar, ai, log2n):
    for _ in range(log2n):
        ar, ai = _cmul(ar, ai, ar, ai)
    return ar, ai


def _shift_rows(x, up):
    row = lax.broadcasted_iota(jnp.int32, x.shape, 0)
    if up:
        return jnp.where(row == SEGMENTS - 1, 0.0, pltpu.roll(x, SEGMENTS - 1, 0))
    return jnp.where(row == 0, 0.0, pltpu.roll(x, 1, 0))


def _segment_carries(er, ei, pr, pi, up):
    cr = jnp.zeros_like(er)
    ci = jnp.zeros_like(ei)
    for _ in range(SEGMENTS - 1):
        tr, ti = _cmul(pr, pi, cr, ci)
        cr, ci = _shift_rows(er + tr, up), _shift_rows(ei + ti, up)
    return cr, ci


def _scan_states(sr, si, ar, ai, T, reverse):
    ns = sr.shape[1]
    ar8 = jnp.broadcast_to(ar, (SEGMENTS, ns))
    ai8 = jnp.broadcast_to(ai, (SEGMENTS, ns))

    def rows(t):
        k = (T - 1 - t) if reverse else t
        return pl.ds(pl.multiple_of(k * SEGMENTS, SEGMENTS), SEGMENTS)

    def advance(t, c):
        tr, ti = _cmul(ar8, ai8, c[0], c[1])
        return tr + sr[rows(t), :], ti + si[rows(t), :]

    def several(step):
        def trip(t, c):
            for j in range(SCAN_UNROLL):
                c = step(t * SCAN_UNROLL + j, c)
            return c
        return trip

    zero = jnp.zeros((SEGMENTS, ns), F32)
    er, ei = lax.fori_loop(0, T // SCAN_UNROLL, several(advance), (zero, zero))
    pr, pi = _power(ar, ai, T.bit_length() - 1)
    cr, ci = _segment_carries(er, ei, jnp.broadcast_to(pr, (SEGMENTS, ns)), jnp.broadcast_to(pi, (SEGMENTS, ns)), reverse)

    def store(t, c):
        nr, ni = advance(t, c)
        sr[rows(t), :] = nr
        si[rows(t), :] = ni
        return nr, ni

    lax.fori_loop(0, T // SCAN_UNROLL, several(store), (cr, ci))
    return cr, ci


def _slab_specs(ns):
    return [pl.BlockSpec((None, LANES, ns), lambda g: (g, 0, 0)),
            pl.BlockSpec((None, LANES, ns), lambda g: (g, 0, 0)),
            pl.BlockSpec((None, 1, ns), lambda g: (g, 0, 0)),
            pl.BlockSpec((None, 1, ns), lambda g: (g, 0, 0)),
            pl.BlockSpec((None, ns, LANES), lambda g: (g, 0, 0)),
            pl.BlockSpec((None, ns, LANES), lambda g: (g, 0, 0)),
            pl.BlockSpec((1, LANES), lambda g: (0, g))]


def _chunks(S):
    rc = _tile(S, 512, 16)
    return rc, S // rc


def ssm_fwd(u, bbr, bbi, ar, ai, cbr, cbi, dsk, *, name):
    S, SW = u.shape
    nslab, _, ns = bbr.shape
    T = S // SEGMENTS
    assert T & (T - 1) == 0
    rc, nc = _chunks(S)

    def body(u_ref, br_ref, bi_ref, ar_ref, ai_ref, cr_ref, ci_ref, d_ref, y_ref, yg_ref, str_ref, sti_ref, sr, si):
        def inputs(c, carry):
            r = pl.ds(pl.multiple_of(c * rc, rc), rc)
            sr[r, :] = jnp.dot(u_ref[r, :], br_ref[...], preferred_element_type=F32)
            si[r, :] = jnp.dot(u_ref[r, :], bi_ref[...], preferred_element_type=F32)
            return carry

        lax.fori_loop(0, nc, inputs, 0)
        _scan_states(sr, si, ar_ref[...], ai_ref[...], T, False)

        def outputs(c, carry):
            r = pl.ds(pl.multiple_of(c * rc, rc), rc)
            srb, sib = sr[r, :].astype(BF16), si[r, :].astype(BF16)
            str_ref[r, :] = srb
            sti_ref[r, :] = sib
            y = (jnp.dot(srb, cr_ref[...], preferred_element_type=F32)
                 - jnp.dot(sib, ci_ref[...], preferred_element_type=F32) + d_ref[...] * u_ref[r, :].astype(F32))
            y_ref[r, :] = y
            yg_ref[r, :] = _gelu(y).astype(BF16)
            return carry

        lax.fori_loop(0, nc, outputs, 0)

    slab = pl.BlockSpec((S, LANES), lambda g: (0, g))
    states = pl.BlockSpec((S, ns), lambda g: (0, g))
    return pl.pallas_call(
        body, out_shape=([jax.ShapeDtypeStruct((S, SW), F32), jax.ShapeDtypeStruct((S, SW), BF16)]
                         + [jax.ShapeDtypeStruct((S, nslab * ns), BF16)] * 2),
        grid=(nslab,), in_specs=[slab] + _slab_specs(ns), out_specs=[slab, slab, states, states],
        scratch_shapes=[pltpu.VMEM((S, ns), F32)] * 2,
        compiler_params=_params(("parallel",), VMEM_LIMIT_SCAN), name=name)(u, bbr, bbi, ar, ai, cbr, cbi, dsk)


def ssm_bwd(u, d_direct, d_gate, y, st_r, st_i, bbr, bbi, ar, ai, cbr, cbi, dsk, *, name, deps=()):
    S, SW = u.shape
    nslab, _, ns = bbr.shape
    T = S // SEGMENTS
    rc, nc = _chunks(S)
    pair_rows = 2 * SEGMENTS

    def body(*refs):
        (u_ref, d1_ref, d2_ref, y_ref, sr_ref, si_ref, br_ref, bi_ref, ar_ref, ai_ref, cr_ref, ci_ref,
         d_ref) = refs[:13]
        (du_ref, dbr_ref, dbi_ref, dcr_ref, dci_ref, dar_ref, dai_ref, dd_ref, lr, li,
         dy_ref) = refs[13 + len(deps):]

        def inputs(c, skip):
            r = pl.ds(pl.multiple_of(c * rc, rc), rc)
            dy = (d1_ref[r, :] + d2_ref[r, :].astype(F32)) * _gelu_grad(y_ref[r, :])
            dy_ref[r, :] = dy
            gb = dy.astype(BF16)
            lr[r, :] = lax.dot_general(gb, cr_ref[...], _NT, preferred_element_type=F32)
            li[r, :] = -lax.dot_general(gb, ci_ref[...], _NT, preferred_element_type=F32)
            return skip + jnp.sum(dy * u_ref[r, :].astype(F32), axis=0, keepdims=True)

        dd_ref[...] = lax.fori_loop(0, nc, inputs, jnp.zeros((1, LANES), F32))
        _scan_states(lr, li, ar_ref[...], -ai_ref[...], T, True)

        def steps(j):
            rows = pl.ds(pl.multiple_of(j * pair_rows, pair_rows), pair_rows)
            tr, ti = sr_ref[rows, :].astype(F32), si_ref[rows, :].astype(F32)
            return tr[:SEGMENTS], tr[SEGMENTS:], ti[:SEGMENTS], ti[SEGMENTS:]

        def pair(j, c):
            acc_r, acc_i, pr, pi = c
            lo_r, hi_r, lo_i, hi_i = steps(j)
            first = pl.ds(pl.multiple_of(j * pair_rows, SEGMENTS), SEGMENTS)
            second = pl.ds(pl.multiple_of(j * pair_rows + SEGMENTS, SEGMENTS), SEGMENTS)
            la_r, la_i, lb_r, lb_i = lr[first, :], li[first, :], lr[second, :], li[second, :]
            return (acc_r + la_r * pr + la_i * pi + lb_r * lo_r + lb_i * lo_i,
                    acc_i - la_r * pi + la_i * pr - lb_r * lo_i + lb_i * lo_r, hi_r, hi_i)

        def pairs(t, c):
            return pair(2 * t + 1, pair(2 * t, c))

        _, end_r, _, end_i = steps(T // 2 - 1)
        zero = jnp.zeros((SEGMENTS, ns), F32)
        acc = lax.fori_loop(0, T // 4, pairs, (zero, zero, _shift_rows(end_r, False), _shift_rows(end_i, False)))
        dar_ref[...] = jnp.sum(acc[0], axis=0, keepdims=True)
        dai_ref[...] = jnp.sum(acc[1], axis=0, keepdims=True)

        dbr_ref[...] = jnp.zeros_like(dbr_ref)
        dbi_ref[...] = jnp.zeros_like(dbi_ref)
        dcr_ref[...] = jnp.zeros_like(dcr_ref)
        dci_ref[...] = jnp.zeros_like(dci_ref)

        def outputs(c, carry):
            r = pl.ds(pl.multiple_of(c * rc, rc), rc)
            ub = u_ref[r, :]
            g = dy_ref[r, :]
            gb = g.astype(BF16)
            lrb = lr[r, :].astype(BF16)
            lib = li[r, :].astype(BF16)
            du_ref[r, :] = (lax.dot_general(lrb, br_ref[...], _NT, preferred_element_type=F32)
                            + lax.dot_general(lib, bi_ref[...], _NT, preferred_element_type=F32)
                            + d_ref[...] * g).astype(BF16)
            dbr_ref[...] += lax.dot_general(ub, lrb, _TN, preferred_element_type=F32)
            dbi_ref[...] += lax.dot_general(ub, lib, _TN, preferred_element_type=F32)
            dcr_ref[...] += lax.dot_general(sr_ref[r, :], gb, _TN, preferred_element_type=F32)
            dci_ref[...] -= lax.dot_general(si_ref[r, :], gb, _TN, preferred_element_type=F32)
            return carry

        lax.fori_loop(0, nc, outputs, 0)

    slab = pl.BlockSpec((S, LANES), lambda g: (0, g))
    states = pl.BlockSpec((S, ns), lambda g: (0, g))
    bspec = pl.BlockSpec((None, LANES, ns), lambda g: (g, 0, 0))
    cspec = pl.BlockSpec((None, ns, LANES), lambda g: (g, 0, 0))
    aspec = pl.BlockSpec((None, 1, ns), lambda g: (g, 0, 0))
    return pl.pallas_call(
        body,
        out_shape=[jax.ShapeDtypeStruct((S, SW), BF16),
                   jax.ShapeDtypeStruct((nslab, LANES, ns), F32), jax.ShapeDtypeStruct((nslab, LANES, ns), F32),
                   jax.ShapeDtypeStruct((nslab, ns, LANES), F32), jax.ShapeDtypeStruct((nslab, ns, LANES), F32),
                   jax.ShapeDtypeStruct((nslab, 1, ns), F32), jax.ShapeDtypeStruct((nslab, 1, ns), F32),
                   jax.ShapeDtypeStruct((1, SW), F32)],
        grid=(nslab,), in_specs=[slab, slab, slab, slab, states, states] + _slab_specs(ns) + [_ANY] * len(deps),
        out_specs=[slab, bspec, bspec, cspec, cspec, aspec, aspec, pl.BlockSpec((1, LANES), lambda g: (0, g))],
        scratch_shapes=[pltpu.VMEM((S, ns), F32)] * 2 + [pltpu.VMEM((S, LANES), F32)],
        compiler_params=_params(("parallel",), VMEM_LIMIT_SCAN), name=name)(
            u, d_direct, d_gate, y, st_r, st_i, bbr, bbi, ar, ai, cbr, cbi, dsk, *deps)


def _discretise(lam_re, lam_im, log_dt, b_re, b_im):
    dt = jnp.exp(log_dt)[:, None]
    mag = jnp.exp(lam_re * dt)
    ar = mag * jnp.cos(lam_im * dt)
    ai = mag * jnp.sin(lam_im * dt)
    nr, ni = ar - 1.0, ai
    den = lam_re * lam_re + lam_im * lam_im
    cr = ((nr * lam_re + ni * lam_im) / den)[..., None]
    ci = ((ni * lam_re - nr * lam_im) / den)[..., None]
    return ar, ai, cr * b_re - ci * b_im, cr * b_im + ci * b_re


def _block_diag(t, nslab):
    G, R, C = t.shape
    eye = jnp.eye(SLAB_GROUPS, dtype=t.dtype)
    t = t.reshape(nslab, SLAB_GROUPS, R, C)
    return jnp.einsum('sgrc,gh->sgrhc', t, eye).reshape(nslab, SLAB_GROUPS * R, SLAB_GROUPS * C)


def _block_diag_part(t, R, C):
    nslab = t.shape[0]
    eye = jnp.eye(SLAB_GROUPS, dtype=t.dtype)
    t = t.reshape(nslab, SLAB_GROUPS, R, SLAB_GROUPS, C)
    return jnp.einsum('sgrhc,gh->sgrc', t, eye).reshape(nslab * SLAB_GROUPS, R, C)


def _place():
    return lax.axis_index("x"), lax.axis_index("y"), lax.axis_index("c")


_HBM = pl.BlockSpec(memory_space=pltpu.HBM)
_SEM = pl.BlockSpec(memory_space=pltpu.SEMAPHORE)
_ORDERED_EFFECT = pltpu.SideEffectType.DATAFLOW_SIDE_EFFECTING


def _split_call(name, srcs, zones, sems_in, n_new, body_fn, after):
    nsrc, nz, ns, nn = len(srcs), len(zones), len(sems_in), len(n_new)
    nb = nsrc + nz

    def body(*refs):
        outs = refs[nb + ns + 1:]
        body_fn(refs[:nb], refs[nb:nb + ns], outs[:nn])
        outs[nn + nz][...] = jnp.zeros((SUBLANES, LANES), F32)

    res = pl.pallas_call(
        body, name=name,
        out_shape=([pltpu.SemaphoreType.DMA((n,)) for n in n_new] + [pltpu.HBM(b.shape, b.dtype) for b in zones]
                   + [jax.ShapeDtypeStruct((SUBLANES, LANES), F32)]),
        in_specs=[_HBM] * nb + [_SEM] * ns + [_ANY],
        out_specs=[_SEM] * nn + [_HBM] * nz + [pl.BlockSpec(memory_space=pltpu.VMEM)],
        input_output_aliases={nsrc + i: nn + i for i in range(nz)},
        compiler_params=pltpu.CompilerParams(has_side_effects=_ORDERED_EFFECT))(
            *[pltpu.with_memory_space_constraint(b, pltpu.HBM) for b in list(srcs) + list(zones)], *sems_in, after)
    return list(res[:nn]), list(res[nn:nn + nz]), res[-1]


def _mesh_peers():
    x, y, c = _place()
    return x, y, c, (x, y, 1 - c), [(1 - x, y), (x, 1 - y), (1 - x, 1 - y)]


def gather_start(shards, after, *, name):
    nw = len(shards)
    x, y, c = _place()
    zones = [lax.dynamic_update_slice(lax.empty((N_DEV,) + s.shape, s.dtype), s[None], (4 * x + 2 * y + c, 0, 0))
             for s in shards]

    def body(bufs, taken, new):
        for cp in _gather_first(bufs, nw, new[0], new[1]):
            cp.start()

    sems, zones, token = _split_call(name, shards, zones, [], [4 * nw, 4 * nw], body, after)
    return shards, sems, zones, token


def _gather_first(bufs, nw, send, recv):
    x, y, c, sibling, chips = _mesh_peers()
    out = []
    for w in range(nw):
        slot = bufs[nw + w].at[4 * x + 2 * y + c]
        for k, to in enumerate([sibling] + [(*ch, c) for ch in chips]):
            out.append(pltpu.make_async_remote_copy(
                src_ref=bufs[w], dst_ref=slot, send_sem=send.at[4 * w + k], recv_sem=recv.at[4 * w + k],
                device_id=to, device_id_type=MESH))
    return out


def _gather_slot_copy(bufs, nw, w, block, send_sem, recv_sem, to):
    px, py, pc = block
    slot = bufs[nw + w].at[4 * px + 2 * py + pc]
    return pltpu.make_async_remote_copy(src_ref=slot, dst_ref=slot, send_sem=send_sem, recv_sem=recv_sem,
                                        device_id=to, device_id_type=MESH)


def gather_forward(state, after, *, name):
    shards, sems, zones, _ = state
    nw = len(shards)

    def body(bufs, taken, new):
        x, y, c, sibling, chips = _mesh_peers()
        for j, ch in enumerate(chips):
            for w in range(nw):
                k = 4 * w + 1 + j
                _gather_slot_copy(bufs, nw, w, (*ch, c), taken[0].at[k], taken[1].at[k], (*ch, c)).wait_recv()
                _gather_slot_copy(bufs, nw, w, (*ch, c), new[0].at[3 * w + j], new[1].at[3 * w + j], sibling).start()
        for w in range(nw):
            _gather_slot_copy(bufs, nw, w, sibling, taken[0].at[4 * w], taken[1].at[4 * w], sibling).wait_recv()
        for cp in _gather_first(bufs, nw, taken[0], taken[1]):
            cp.wait_send()

    sems, zones, token = _split_call(name, shards, zones, sems, [3 * nw, 3 * nw], body, after)
    return shards, sems, zones, token


def gather_finish(state, after, *, name):
    shards, sems, zones, _ = state
    nw = len(shards)

    def body(bufs, taken, new):
        x, y, c, sibling, chips = _mesh_peers()
        for w in range(nw):
            for j, ch in enumerate(chips):
                cp = _gather_slot_copy(bufs, nw, w, (*ch, 1 - c), taken[0].at[3 * w + j], taken[1].at[3 * w + j], sibling)
                cp.wait_send()
                cp.wait_recv()

    _, zones, _ = _split_call(name, shards, zones, sems, [], body, after)
    return zones


def exchange_start(srcs, zone_shapes, copies, n, after, *, name):
    nw = len(srcs)
    zones = [lax.empty(z, s.dtype) for z, s in zip(zone_shapes, srcs)]

    def body(bufs, taken, new):
        for cp in copies(bufs[:nw], bufs[nw:], new[0], new[1]):
            cp.start()

    sems, zones, token = _split_call(name, srcs, zones, [], [n, n], body, after)
    return srcs, copies, sems, zones, token


def exchange_wait(state, after, *, name):
    srcs, copies, sems, zones, _ = state
    nw = len(srcs)

    def body(bufs, taken, new):
        for cp in copies(bufs[:nw], bufs[nw:], taken[0], taken[1]):
            cp.wait_send()
            cp.wait_recv()

    _, zones, _ = _split_call(name, srcs, zones, sems, [], body, after)
    return zones


def _core_copies(srcs, zones, send, recv):
    x, y, c = _place()
    return [pltpu.make_async_remote_copy(
        src_ref=srcs[w].at[:, 1 - c], dst_ref=zones[w], send_sem=send.at[w], recv_sem=recv.at[w],
        device_id=(x, y, 1 - c), device_id_type=MESH) for w in range(len(srcs))]


def _chip_copies(srcs, zones, send, recv):
    x, y, c = _place()
    chips = [(1 - x, y), (x, 1 - y), (1 - x, 1 - y)]
    return [pltpu.make_async_remote_copy(
        src_ref=srcs[w].at[2 * cx + cy], dst_ref=zones[w].at[j], send_sem=send.at[3 * w + j],
        recv_sem=recv.at[3 * w + j], device_id=(cx, cy, c), device_id_type=MESH)
        for w in range(len(srcs)) for j, (cx, cy) in enumerate(chips)]


def _blocked(fn, ins, outs, *, name, place=None, tr=256):
    k, n = outs[0][0]
    tr = _tile(k, tr, 16)
    if place is None:
        place = jnp.zeros((1,), jnp.int32)
    specs = []
    args = []
    for a in ins:
        if isinstance(a, tuple):
            arr, lead = a
            specs.append(pl.BlockSpec((None, tr, n), functools.partial(lambda i, s, lead: (*lead(i, s), 0), lead=lead)))
            args.append(arr)
        else:
            specs.append(pl.BlockSpec((tr, n), lambda i, s: (i, 0)))
            args.append(a)
    nin = len(args)

    def body(place_ref, *refs):
        res = fn(*[r[...] for r in refs[:nin]])
        for ref, val in zip(refs[nin:], res):
            ref[...] = val.astype(ref.dtype)

    return pl.pallas_call(
        body, out_shape=[jax.ShapeDtypeStruct(s, d) for s, d in outs],
        grid_spec=pltpu.PrefetchScalarGridSpec(
            num_scalar_prefetch=1, grid=(k // tr,), in_specs=specs,
            out_specs=[pl.BlockSpec((tr, n), lambda i, s: (i, 0)) for _ in outs]),
        compiler_params=_params(("parallel",)), name=name)(place, *args)


def _adamw(w, g, m, v):
    m = ADAM_B1 * m + (1.0 - ADAM_B1) * g
    v = ADAM_B2 * v + (1.0 - ADAM_B2) * (g * g)
    m_hat = m / (1.0 - ADAM_B1 ** ADAM_STEP)
    v_hat = v / (1.0 - ADAM_B2 ** ADAM_STEP)
    delta = -ADAM_LR * (m_hat * pl.reciprocal(jnp.sqrt(v_hat) + ADAM_EPS, approx=True) + ADAM_WD * w)
    return delta, m, v


def kernel(x, p, mix_norm_pre, w_in, lam_re, lam_im, log_dt, ssm_b_re, ssm_b_im, ssm_c_re, ssm_c_im, ssm_d, w_glu, b_glu, attn_out_norm, ssm_out_norm, w_out, mix_norm_post, mlp_norm_pre, w_up, w_down, mlp_norm_post, ple_norm_pre, w_ple_gate, w_ple_proj, ple_norm_post, loss_target, m_mix_norm_pre, m_w_in, m_lam_re, m_lam_im, m_log_dt, m_ssm_b_re, m_ssm_b_im, m_ssm_c_re, m_ssm_c_im, m_ssm_d, m_w_glu, m_b_glu, m_attn_out_norm, m_ssm_out_norm, m_w_out, m_mix_norm_post, m_mlp_norm_pre, m_w_up, m_w_down, m_mlp_norm_post, m_ple_norm_pre, m_w_ple_gate, m_w_ple_proj, m_ple_norm_post, v_mix_norm_pre, v_w_in, v_lam_re, v_lam_im, v_log_dt, v_ssm_b_re, v_ssm_b_im, v_ssm_c_re, v_ssm_c_im, v_ssm_d, v_w_glu, v_b_glu, v_attn_out_norm, v_ssm_out_norm, v_w_out, v_mix_norm_post, v_mlp_norm_pre, v_w_up, v_w_down, v_mlp_norm_post, v_ple_norm_pre, v_w_ple_gate, v_w_ple_proj, v_ple_norm_post):
    weights = dict(mix_norm_pre=mix_norm_pre, w_in=w_in, lam_re=lam_re, lam_im=lam_im, log_dt=log_dt, ssm_b_re=ssm_b_re, ssm_b_im=ssm_b_im, ssm_c_re=ssm_c_re, ssm_c_im=ssm_c_im, ssm_d=ssm_d, w_glu=w_glu, b_glu=b_glu, attn_out_norm=attn_out_norm, ssm_out_norm=ssm_out_norm, w_out=w_out, mix_norm_post=mix_norm_post, mlp_norm_pre=mlp_norm_pre, w_up=w_up, w_down=w_down, mlp_norm_post=mlp_norm_post, ple_norm_pre=ple_norm_pre, w_ple_gate=w_ple_gate, w_ple_proj=w_ple_proj, ple_norm_post=ple_norm_post)
    mom_m = dict(mix_norm_pre=m_mix_norm_pre, w_in=m_w_in, lam_re=m_lam_re, lam_im=m_lam_im, log_dt=m_log_dt, ssm_b_re=m_ssm_b_re, ssm_b_im=m_ssm_b_im, ssm_c_re=m_ssm_c_re, ssm_c_im=m_ssm_c_im, ssm_d=m_ssm_d, w_glu=m_w_glu, b_glu=m_b_glu, attn_out_norm=m_attn_out_norm, ssm_out_norm=m_ssm_out_norm, w_out=m_w_out, mix_norm_post=m_mix_norm_post, mlp_norm_pre=m_mlp_norm_pre, w_up=m_w_up, w_down=m_w_down, mlp_norm_post=m_mlp_norm_post, ple_norm_pre=m_ple_norm_pre, w_ple_gate=m_w_ple_gate, w_ple_proj=m_w_ple_proj, ple_norm_post=m_ple_norm_post)
    mom_v = dict(mix_norm_pre=v_mix_norm_pre, w_in=v_w_in, lam_re=v_lam_re, lam_im=v_lam_im, log_dt=v_log_dt, ssm_b_re=v_ssm_b_re, ssm_b_im=v_ssm_b_im, ssm_c_re=v_ssm_c_re, ssm_c_im=v_ssm_c_im, ssm_d=v_ssm_d, w_glu=v_w_glu, b_glu=v_b_glu, attn_out_norm=v_attn_out_norm, ssm_out_norm=v_ssm_out_norm, w_out=v_w_out, mix_norm_post=v_mix_norm_post, mlp_norm_pre=v_mlp_norm_pre, w_up=v_w_up, w_down=v_w_down, mlp_norm_post=v_mlp_norm_post, ple_norm_pre=v_ple_norm_pre, w_ple_gate=v_w_ple_gate, w_ple_proj=v_w_ple_proj, ple_norm_post=v_ple_norm_post)
    order = list(weights)
    big = ["w_in", "w_glu", "w_out", "w_up", "w_down", "w_ple_gate", "w_ple_proj"]
    col_sharded = {"w_in", "w_up", "w_ple_proj"}
    small = [n for n in order if n not in big]

    _, S, D = x.shape
    xs = x[0]
    tgt = loss_target[0]
    AW = attn_out_norm.shape[1]
    SW = ssm_d.shape[1]
    H = AW // HEAD_DIM
    G = SW // SSM_GROUP
    nslab = G // SLAB_GROUPS
    P_, C_ = SSM_STATE, SSM_GROUP

    shard = {n: weights[n][0].astype(BF16) for n in big}
    W, WT = {}, {}

    def arrived(names, gathered):
        for n, g in zip(names, gathered):
            W[n] = g if n in col_sharded else g.reshape(1, N_DEV * g.shape[1], g.shape[2])

    def transposed(g):
        return jnp.swapaxes(g, 1, 2).reshape(1, g.shape[0] * g.shape[2], g.shape[1])

    g1, g2, g3, g4, g5, g6 = (weights[n] for n in ("mix_norm_pre", "mix_norm_post", "mlp_norm_pre",
                                                      "mlp_norm_post", "ple_norm_pre", "ple_norm_post"))
    ga, gs = attn_out_norm, ssm_out_norm
    gather_in = gather_start([shard["w_in"]], shard["w_in"], name="gather_w_in_start")
    (hn1,) = rowwise(lambda a, g: (_rms(a, g),), [xs], [g1], [(D, BF16)], deps=(gather_in[-1],), name="norm_in")
    gather_in = gather_forward(gather_in, hn1, name="gather_w_in_forward")
    arrived(["w_in"], gather_finish(gather_in, gather_in[-1], name="gather_w_in_finish"))
    WT["w_in"] = transposed(W["w_in"])
    early, mid, late = ["w_glu", "w_out"], ["w_up"], ["w_down", "w_ple_gate", "w_ple_proj"]
    gather_early = gather_start([shard[n] for n in early], W["w_in"], name="gather_early_start")
    gather_mid = gather_start([shard[n] for n in mid], gather_early[-1], name="gather_mid_start")
    gather_late = gather_start([shard[n] for n in late], gather_mid[-1], name="gather_late_start")

    (proj,) = mm_nn(hn1, W["w_in"], [F32], deps=(gather_late[-1],), name="proj_in")
    attn, lse = attn_fwd(proj, H, name="attn_fwd")
    gather_early = gather_forward(gather_early, attn, name="gather_early_forward")
    (mix_a,) = rowwise(lambda a, g: (_rms(a, g),), [attn], [ga], [(AW, BF16)], deps=(gather_early[-1],),
                       name="attn_norm")
    arrived(early, gather_finish(gather_early, mix_a, name="gather_early_finish"))

    a_r, a_i, bb_r, bb_i = _discretise(lam_re[0], lam_im[0], log_dt[0], ssm_b_re[0], ssm_b_im[0])
    ssm_consts = (_block_diag(bb_r.swapaxes(1, 2), nslab).astype(BF16), _block_diag(bb_i.swapaxes(1, 2), nslab).astype(BF16),
                  a_r.reshape(nslab, 1, SLAB_STATES), a_i.reshape(nslab, 1, SLAB_STATES),
                  _block_diag(ssm_c_re[0].swapaxes(1, 2), nslab).astype(BF16),
                  _block_diag(ssm_c_im[0].swapaxes(1, 2), nslab).astype(BF16), ssm_d)
    u_seg = _to_segments(proj[:, 3 * AW:]).astype(BF16)
    y_pre, yg, st_r, st_i = ssm_fwd(u_seg, *ssm_consts, name="ssm_fwd")
    gather_mid = gather_forward(gather_mid, y_pre, name="gather_mid_forward")
    (gl1,) = mm_nn(yg, W["w_glu"], [BF16], epi=lambda acc, b: (acc + b,), bias=b_glu, deps=(gather_mid[-1],),
                   name="glu_gate")
    (mix_s,) = rowwise(lambda yp, gl, g: (_rms(_gelu(yp) * _sigmoid(gl), g),), [y_pre, gl1], [gs], [(SW, BF16)],
                       name="ssm_glu_norm")
    mixed = [mix_a, _from_segments(mix_s)]
    (mo,) = mm_nn(mixed, W["w_out"], [BF16], tn=1024, name="mix_out")

    def resid_norm(h, t, gpost, gpre):
        hh = h + _rms(t, gpost)
        return hh, _rms(hh, gpre)

    h1, hn2 = rowwise(resid_norm, [xs, mo], [g2, g3], [(D, F32), (D, BF16)], name="resid_mix")
    arrived(mid, gather_finish(gather_mid, hn2, name="gather_mid_finish"))
    gather_late = gather_forward(gather_late, W["w_up"], name="gather_late_forward")
    WT["w_up"] = transposed(W["w_up"])

    def relu2(acc):
        r = jnp.maximum(acc, 0.0)
        return acc, r * r

    up, act = mm_nn(hn2, W["w_up"], [BF16, BF16], epi=relu2, deps=(gather_late[-1],), tm=1024, tn=1024, name="mlp_up")
    arrived(late, gather_finish(gather_late, act, name="gather_late_finish"))
    (ff,) = mm_nn(act, W["w_down"], [BF16], name="mlp_down")
    h2, hn3 = rowwise(resid_norm, [h1, ff], [g4, g5], [(D, F32), (D, BF16)], name="resid_mlp")
    (gl2,) = mm_nn(hn3, W["w_ple_gate"], [BF16], tn=1024, name="ple_gate")
    pb = p[0, 0].astype(BF16)
    (emb,) = mm_nn(pb, W["w_ple_proj"], [BF16], name="ple_proj")

    def head(h, gl, e, t, g):
        sg = _sigmoid(gl)
        ge = sg * e
        err = h + _rms(ge, g) - t
        dh = err * (1.0 / D)
        dge, dg = _rms_bwd(dh, ge, g)
        return dh, dge * e * sg * (1.0 - sg), dge * sg, jnp.sum(err * err, axis=0, keepdims=True), dg

    dh3, dgl2, demb, loss_part, dg6 = rowwise(head, [h2, gl2, emb, tgt], [g6], [(D, F32), (D, BF16), (D, BF16)],
                                             [D, D], name="ple_loss_head")
    loss = lax.psum(0.5 / D * jnp.sum(loss_part), ("x", "y", "c"))

    x_i, y_i, c_i = _place()
    place = jnp.stack([c_i, 2 * x_i + y_i]).astype(jnp.int32)
    grads, out_g, out_d, out_m, out_v = {}, {}, {}, {}, {}

    def to_sibling(names, after, tag):
        chunks = []
        for n in names:
            g = grads[n]
            g = g if n in col_sharded else g.reshape(N_DEV, g.shape[1] // N_DEV, g.shape[2])
            chunks.append(g.reshape(4, 2, g.shape[1], g.shape[2]))
        return chunks, exchange_start(chunks, [(4,) + g.shape[2:] for g in chunks], _core_copies, len(chunks), after,
                                      name=f"grads_to_sibling_{tag}")

    def to_chips(names, sent, after, tag):
        chunks, state = sent
        sums = []
        for n, g, r in zip(names, chunks, exchange_wait(state, after, name=f"grads_from_sibling_{tag}")):
            k, nn = g.shape[2], g.shape[3]
            kb = k // _tile(k, 512, 16)

            def mine(i, s, kb=kb):
                return 2 * (i // kb) + s[0], i % kb

            (s,) = _blocked(lambda a, b: (a.astype(F32) + b.astype(F32),),
                            [(g.reshape(N_DEV, k, nn), mine), r.reshape(4 * k, nn)],
                            [((4 * k, nn), BF16)], place=place, tr=k // kb, name=f"chip_sum_{n}")
            sums.append(s.reshape(4, k, nn))
        return sums, exchange_start(sums, [(3,) + s.shape[1:] for s in sums], _chip_copies, 3 * len(sums), sums[-1],
                                    name=f"grads_to_chips_{tag}")

    def update(w_, m_, v_, own, r0, r1, r2):
        g = own.astype(F32) + r0.astype(F32) + r1.astype(F32) + r2.astype(F32)
        return (g,) + _adamw(w_, g, m_, v_)

    def finish(names, sent, after, tag):
        sums, state = sent
        for n, s, r in zip(names, sums, exchange_wait(state, after, name=f"grads_from_chips_{tag}")):
            shp = weights[n].shape
            res = _blocked(update, [weights[n][0], mom_m[n][0], mom_v[n][0], (s, lambda i, p_: (p_[1], i)),
                                    (r, lambda i, p_: (0, i)), (r, lambda i, p_: (1, i)), (r, lambda i, p_: (2, i))],
                           [(shp[1:], F32)] * 4, place=place, tr=max(16, min(shp[1] // 8, UPDATE_BLOCK // shp[2])),
                           name=f"adamw_{n}")
            out_g[n], out_d[n], out_m[n], out_v[n] = (t.reshape(shp) for t in res)
        return out_v[names[-1]]

    grads["w_ple_proj"] = mm_tn(pb, demb, N_DEV, name="grad_w_ple_proj")
    dhn3 = mm_nt(dgl2, W["w_ple_gate"], BF16, tko=1024, name="back_ple_gate")
    grads["w_ple_gate"] = mm_tn(hn3, dgl2, 1, name="grad_w_ple_gate")

    def back_resid(dh, dhn, h, t, gpre, gpost):
        d1, dgpre = _rms_bwd(dhn, h, gpre)
        dhh = dh + d1
        dt, dgpost = _rms_bwd(dhh, t, gpost)
        return dhh, dt, dgpre, dgpost

    dh2, dff, dg5, dg4 = rowwise(back_resid, [dh3, dhn3, h2, ff], [g5, g4], [(D, F32), (D, BF16)], [D, D],
                                 name="back_resid_mlp")
    dup = mm_nt(dff, W["w_down"], BF16, epi=lambda acc, u_: (acc * 2.0 * jnp.maximum(u_.astype(F32), 0.0),),
                extra=up, name="back_mlp_down")
    grads["w_down"] = mm_tn(act, dff, 1, name="grad_w_down")
    group_a = ["w_ple_proj", "w_ple_gate", "w_down"]
    sent_a = to_sibling(group_a, grads["w_down"], "a")
    (dhn2,) = mm_nn(dup, WT["w_up"], [BF16], deps=(sent_a[1][-1],), name="back_mlp_up")
    sent_a = to_chips(group_a, sent_a, dhn2, "a")
    grads["w_up"] = mm_tn(hn2, dup, N_DEV, deps=(sent_a[1][-1],), name="grad_w_up")
    dh1, dmo, dg3, dg2 = rowwise(back_resid, [dh2, dhn2, h1, mo], [g3, g2], [(D, F32), (D, BF16)], [D, D],
                                 name="back_resid_mix")
    dmixed = mm_nt(dmo, W["w_out"], BF16, tko=1024, name="back_mix_out")
    grads["w_out"] = mm_tn(mixed, dmo, 1, name="grad_w_out")

    def back_glu(dm, yp, gl, g):
        ygf = _gelu(yp)
        sg = _sigmoid(gl)
        dssm, dg = _rms_bwd(dm, ygf * sg, g)
        dgl = dssm * ygf * sg * (1.0 - sg)
        return dgl, dssm * sg, dg, jnp.sum(dgl, axis=0, keepdims=True)

    dgl1, dyg_direct, dgs, db_glu = rowwise(back_glu, [_to_segments(dmixed[:, AW:]), y_pre, gl1], [gs],
                                            [(SW, BF16), (SW, F32)], [SW, SW], name="back_glu")
    dyg_gate = mm_nt(dgl1, W["w_glu"], BF16, name="back_glu_gate")
    grads["w_glu"] = mm_tn(yg, dgl1, 1, name="grad_w_glu")
    group_b = ["w_up", "w_out", "w_glu"]
    sent_b = to_sibling(group_b, grads["w_glu"], "b")
    done_a = finish(group_a, sent_a, sent_b[1][-1], "a")

    du_seg, dbb_r, dbb_i, dcb_r, dcb_i, da_r, da_i, d_skip = ssm_bwd(
        u_seg, dyg_direct, dyg_gate, y_pre, st_r, st_i, *ssm_consts, deps=(done_a,), name="ssm_bwd")
    sent_b = to_chips(group_b, sent_b, du_seg, "b")

    def back_attn_norm(dm, a, l, g):
        da, dg = _rms_bwd(dm, a, g)
        prod = da * a
        delta = jnp.concatenate(
            [jnp.broadcast_to(jnp.sum(prod[:, h * HEAD_DIM:(h + 1) * HEAD_DIM], axis=-1, keepdims=True),
                              (prod.shape[0], HEAD_DIM)) for h in range(H)], axis=1)
        first_half = lax.broadcasted_iota(jnp.int32, l.shape, 1) % HEAD_DIM < HEAD_DIM // 2
        return da, jnp.where(first_half, l, delta), dg

    dattn, stats, dga = rowwise(back_attn_norm, [(dmixed, AW, 0), attn, lse], [ga], [(AW, F32), (AW, F32)], [AW],
                                deps=(sent_b[1][-1],), name="back_attn_norm")
    dq, dk, dv = attn_bwd(proj, dattn, stats, H, name="attn_bwd")
    dproj = [dq, dk, dv, _from_segments(du_seg)]
    (dhn1,) = mm_nn(dproj, WT["w_in"], [BF16], name="back_proj_in")

    def back_in(dh, dhn, a, g):
        d1, dg = _rms_bwd(dhn, a, g)
        return dh + d1, dg

    grad_x, dg1 = rowwise(back_in, [dh1, dhn1, xs], [g1], [(D, F32)], [D], name="back_norm_in")

    cot = dict(
        mix_norm_pre=dg1, mix_norm_post=dg2, mlp_norm_pre=dg3, mlp_norm_post=dg4, ple_norm_pre=dg5, ple_norm_post=dg6,
        attn_out_norm=dga, ssm_out_norm=dgs, b_glu=db_glu, ssm_d=d_skip,
        ssm_c_re=_block_diag_part(dcb_r, P_, C_).swapaxes(1, 2), ssm_c_im=_block_diag_part(dcb_i, P_, C_).swapaxes(1, 2),
        a_r=da_r.reshape(G, P_), a_i=da_i.reshape(G, P_),
        bb_r=_block_diag_part(dbb_r, C_, P_).swapaxes(1, 2), bb_i=_block_diag_part(dbb_i, C_, P_).swapaxes(1, 2))
    names = list(cot)
    flat = jnp.concatenate([cot[n].reshape(-1) for n in names])
    total = flat.shape[0]
    rows_ = -(-total // (LANES * 16)) * 16
    flat = jnp.pad(flat, (0, rows_ * LANES - total)).reshape(rows_, LANES)
    gather_small = gather_start([flat], flat, name="gather_small_start")
    grads["w_in"] = mm_tn(hn1, dproj, N_DEV, deps=(gather_small[-1],), tko=2048, name="grad_w_in")
    group_c = ["w_in"]
    sent_c = to_sibling(group_c, grads["w_in"], "c")
    done_b = finish(group_b, sent_b, sent_c[1][-1], "b")
    sent_c = to_chips(group_c, sent_c, done_b, "c")
    gather_small = gather_forward(gather_small, sent_c[1][-1], name="gather_small_forward")
    (every,) = gather_finish(gather_small, gather_small[-1], name="gather_small_finish")
    (summed,) = _blocked(lambda *t: (functools.reduce(lambda a, b: a + b, t),),
                         [(every, functools.partial(lambda i, p_, j: (j, i), j=j)) for j in range(N_DEV)],
                         [((rows_, LANES), F32)], name="sum_small_grads")
    summed = summed.reshape(-1)
    red, off = {}, 0
    for n in names:
        sz = cot[n].size
        red[n] = summed[off:off + sz].reshape(cot[n].shape)
        off += sz
    _, pull = jax.vjp(_discretise, lam_re[0], lam_im[0], log_dt[0], ssm_b_re[0], ssm_b_im[0])
    d_lre, d_lim, d_ldt, d_bre, d_bim = pull((red["a_r"], red["a_i"], red["bb_r"], red["bb_i"]))
    red.update(lam_re=d_lre, lam_im=d_lim, log_dt=d_ldt, ssm_b_re=d_bre, ssm_b_im=d_bim)

    def pack(d):
        t = jnp.concatenate([d[n].reshape(-1) for n in small])
        r_ = -(-t.shape[0] // (LANES * 16)) * 16
        return jnp.pad(t, (0, r_ * LANES - t.shape[0])).reshape(r_, LANES)

    sw, sg_, sm, sv = pack(weights), pack(red), pack(mom_m), pack(mom_v)
    sd, snm, snv = _blocked(lambda w_, g_, m_, v_: _adamw(w_, g_, m_, v_), [sw, sg_, sm, sv],
                            [(sw.shape, F32)] * 3, name="adamw_small")
    finish(group_c, sent_c, snv, "c")
    off = 0
    for n in small:
        sz = weights[n].size
        shp = weights[n].shape
        out_g[n] = red[n].reshape(shp)
        out_d[n] = sd.reshape(-1)[off:off + sz].reshape(shp)
        out_m[n] = snm.reshape(-1)[off:off + sz].reshape(shp)
        out_v[n] = snv.reshape(-1)[off:off + sz].reshape(shp)
        off += sz

    return (loss, grad_x[None], *[out_g[n] for n in order], *[out_d[n] for n in order],
            *[out_m[n] for n in order], *[out_v[n] for n in order])
```

```python
import functools
import math

import jax
import jax.numpy as jnp
from jax import lax
from jax.experimental import pallas as pl
from jax.experimental.pallas import tpu as pltpu

F32 = jnp.float32
BF16 = jnp.bfloat16
MESH = pl.DeviceIdType.MESH

N_DEV = 8
LANES = 128
SUBLANES = 8
VMEM_LIMIT = 48 * 1024 * 1024
VMEM_LIMIT_SCAN = 60 * 1024 * 1024

HEAD_DIM = 128
BLK = 128
DILATIONS = (1, 4, 16)
SSM_GROUP = 16
SSM_STATE = 64
SLAB_GROUPS = LANES // SSM_GROUP
SLAB_STATES = SLAB_GROUPS * SSM_STATE
SEGMENTS = SUBLANES
SCAN_UNROLL = 4
RMS_EPS = 1e-6
NEG_INF = -1e30

ADAM_LR = 0.001
ADAM_B1 = 0.9
ADAM_B2 = 0.999
ADAM_EPS = 1e-08
ADAM_WD = 0.01
ADAM_STEP = 10


def _tile(n, pref, unit=LANES):
    if n <= pref:
        return n
    t = (pref // unit) * unit
    while t > unit and n % t:
        t -= unit
    assert n % t == 0, (n, pref, unit)
    return t


def _params(sem=None, vmem=VMEM_LIMIT):
    return pltpu.CompilerParams(dimension_semantics=sem, vmem_limit_bytes=vmem)


_NN = (((1,), (0,)), ((), ()))
_NT = (((1,), (1,)), ((), ()))
_TN = (((0,), (0,)), ((), ()))


_ANY = pl.BlockSpec(memory_space=pl.ANY)


def _mm_call(dims, nk, na, nb, pick, n_extra, n_dep, n_out, epi, group=1, **kw):
    first_extra = na + nb
    first_out = first_extra + n_extra + n_dep
    kw["in_specs"] = list(kw["in_specs"]) + [_ANY] * n_dep

    def grouped(refs, step):
        if group == 1:
            return lax.dot_general(refs[0][...], refs[1][...], dims, preferred_element_type=F32)
        kp = refs[0].shape[1]
        return sum(lax.dot_general(refs[step * group + p][...], refs[na][pl.ds(p * kp, kp), :], dims,
                                   preferred_element_type=F32) for p in range(group))

    def single(*refs):
        extra = refs[first_extra:first_extra + n_extra]
        res = epi(grouped(refs, 0), *[e[...] for e in extra])
        for o, r in zip(refs[first_out:first_out + n_out], res):
            o[...] = r.astype(o.dtype)

    if nk == 1:
        assert na == group and nb == 1
        kw["scratch_shapes"] = []
        return pl.pallas_call(single, **kw)

    def body(*refs):
        extra = refs[first_extra:first_extra + n_extra]
        outs = refs[first_out:first_out + n_out]
        acc = refs[-1]
        k = pl.program_id(2)

        @pl.when(k == 0)
        def _():
            acc[...] = jnp.zeros_like(acc)

        def add(a_ref, b_ref):
            acc[...] += lax.dot_general(a_ref[...], b_ref[...], dims, preferred_element_type=F32)

        if na == nb == 1:
            add(refs[0], refs[1])
        elif group > 1:
            for step in range(nk):
                @pl.when(k == step)
                def _(step=step):
                    acc[...] += grouped(refs, step)
        else:
            pa, pb = pick(pl.program_id(0), pl.program_id(1), k)
            for x in range(na):
                for y in range(nb):
                    pl.when((pa == x) & (pb == y))(functools.partial(add, refs[x], refs[na + y]))

        @pl.when(k == nk - 1)
        def _():
            res = epi(acc[...], *[e[...] for e in extra])
            for o, r in zip(outs, res):
                o[...] = r.astype(o.dtype)

    return pl.pallas_call(body, **kw)


def _identity_epi(acc):
    return (acc,)


def _parts(t):
    return list(t) if isinstance(t, (list, tuple)) else [t]


def _part_spec(block, part, which, index):
    def index_map(i, j, k):
        use = which(i, j, k) == part
        r, c = index(i, j, k)
        return jnp.where(use, r, 0), jnp.where(use, c, 0)
    return pl.BlockSpec(block, index_map)


def mm_nn(a, w, out_dtypes, *, name, epi=_identity_epi, bias=None, deps=(), tm=2048, tn=512, tk=2048):
    a = _parts(a)
    M, Kp = a[0].shape
    K = Kp * len(a)
    J, K2, n = w.shape
    assert K == K2
    tm, tn, tk = _tile(M, tm, 16), _tile(n, tn), _tile(K, tk)
    npj = n // tn
    nk = K // tk
    group = 1
    if len(a) == 1:
        in_specs = [pl.BlockSpec((tm, tk), lambda i, j, k: (i, k))]
    else:
        assert tk % Kp == 0
        group = tk // Kp
        in_specs = [pl.BlockSpec((tm, Kp), lambda i, j, k: (i, 0)) for _ in a]
    in_specs.append(pl.BlockSpec((None, tk, tn), lambda i, j, k: (j // npj, k, j % npj)))
    args = a + [w]
    if bias is not None:
        in_specs.append(pl.BlockSpec((1, tn), lambda i, j, k: (0, j)))
        args.append(bias)
    return _mm_call(
        _NN, nk, len(a), 1, lambda i, j, k: (k, 0), len(args) - len(a) - 1, len(deps), len(out_dtypes), epi, group,
        out_shape=[jax.ShapeDtypeStruct((M, J * n), d) for d in out_dtypes],
        grid=(M // tm, J * npj, nk), in_specs=in_specs,
        out_specs=[pl.BlockSpec((tm, tn), lambda i, j, k: (i, j)) for _ in out_dtypes],
        scratch_shapes=[pltpu.VMEM((tm, tn), F32)],
        compiler_params=_params(("parallel", "parallel", "arbitrary")), name=name)(*args, *deps)


def mm_nt(a, w, out_dtype, *, name, epi=_identity_epi, extra=None, tm=2048, tko=512, tnr=2048):
    M, N = a.shape
    J, K, n = w.shape
    assert N == J * n
    tm, tko, tnr = _tile(M, tm, 16), _tile(K, tko), _tile(n, tnr)
    npj = n // tnr
    nk = N // tnr
    in_specs = [pl.BlockSpec((tm, tnr), lambda i, j, k: (i, k)),
                pl.BlockSpec((None, tko, tnr), lambda i, j, k: (k // npj, j, k % npj))]
    args = [a, w]
    if extra is not None:
        in_specs.append(pl.BlockSpec((tm, tko), lambda i, j, k: (i, j)))
        args.append(extra)
    return _mm_call(
        _NT, nk, 1, 1, None, len(args) - 2, 0, 1, epi,
        out_shape=[jax.ShapeDtypeStruct((M, K), out_dtype)],
        grid=(M // tm, K // tko, nk), in_specs=in_specs,
        out_specs=[pl.BlockSpec((tm, tko), lambda i, j, k: (i, j))],
        scratch_shapes=[pltpu.VMEM((tm, tko), F32)],
        compiler_params=_params(("parallel", "parallel", "arbitrary")), name=name)(*args)[0]


def mm_tn(a, b, J, *, name, deps=(), tko=1024, tn=1024, ts=2048):
    a, b = _parts(a), _parts(b)
    S, Kp = a[0].shape
    S2, Np = b[0].shape
    K, N = Kp * len(a), Np * len(b)
    assert S == S2 and N % J == 0
    n = N // J
    tko, tn, ts = _tile(Kp, tko), _tile(math.gcd(n, Np), tn), _tile(S, ts)
    npj = n // tn
    nk = S // ts
    ta, tb = Kp // tko, Np // tn
    assert nk > 1 or len(a) == len(b) == 1
    return _mm_call(
        _TN, nk, len(a), len(b), lambda i, j, k: (i // ta, j // tb), 0, len(deps), 1, _identity_epi,
        out_shape=[jax.ShapeDtypeStruct((J, K, n), BF16)],
        grid=(K // tko, J * npj, nk),
        in_specs=([_part_spec((ts, tko), x, lambda i, j, k: i // ta, lambda i, j, k: (k, i % ta)) for x in range(len(a))]
                  + [_part_spec((ts, tn), y, lambda i, j, k: j // tb, lambda i, j, k: (k, j % tb)) for y in range(len(b))]),
        out_specs=[pl.BlockSpec((None, tko, tn), lambda i, j, k: (j // npj, i, j % npj))],
        scratch_shapes=[pltpu.VMEM((tko, tn), F32)],
        compiler_params=_params(("parallel", "parallel", "arbitrary")), name=name)(*a, *b, *deps)[0]


ROW_BUFFERS = 3


def rowwise(fn, rows, vecs, outs, accs=(), *, name, deps=(), ts=256):
    rows = [r if isinstance(r, tuple) else (r, r.shape[1], 0) for r in rows]
    S = rows[0][0].shape[0]
    ts = _tile(S, ts, 16)
    n_steps = S // ts
    nr, nv, no, nd, na = len(rows), len(vecs), len(outs), len(deps), len(accs)

    def body(*refs):
        r, v = refs[:nr], refs[nr:nr + nv]
        o, a = refs[nr + nv + nd:nr + nv + nd + no], refs[nr + nv + nd + no:nr + nv + nd + no + na]
        bufs, sems = refs[-2 * nr:-nr], refs[-nr:]
        i = pl.program_id(0)

        def fetch(step, slot):
            src = pl.ds(pl.multiple_of(step * ts, ts), ts)
            return [pltpu.make_async_copy(r[x].at[src, pl.ds(cb * w, w)], bufs[x].at[slot], sems[x].at[slot])
                    for x, (_, w, cb) in enumerate(rows)]

        @pl.when(i == 0)
        def _():
            for step in range(min(ROW_BUFFERS - 1, n_steps)):
                for cp in fetch(step, step):
                    cp.start()

        ahead = i + (ROW_BUFFERS - 1)

        @pl.when(ahead < n_steps)
        def _():
            for cp in fetch(ahead, ahead % ROW_BUFFERS):
                cp.start()

        slot = i % ROW_BUFFERS
        for cp in fetch(i, slot):
            cp.wait()
        res = fn(*[b[slot].astype(F32) for b in bufs], *[t[...] for t in v])
        for ref, val in zip(o, res[:no]):
            ref[...] = val.astype(ref.dtype)
        if a:
            @pl.when(pl.program_id(0) == 0)
            def _():
                for ref in a:
                    ref[...] = jnp.zeros_like(ref)

            for ref, val in zip(a, res[no:]):
                ref[...] += val

    in_specs = [_ANY] * nr + [pl.BlockSpec(v.shape, lambda i: (0, 0)) for v in vecs] + [_ANY] * nd
    out_shape = [jax.ShapeDtypeStruct((S, w), d) for w, d in outs]
    out_shape += [jax.ShapeDtypeStruct((1, w), F32) for w in accs]
    out_specs = [pl.BlockSpec((ts, w), lambda i: (i, 0)) for w, _ in outs]
    out_specs += [pl.BlockSpec((1, w), lambda i: (0, 0)) for w in accs]
    scratch = ([pltpu.VMEM((ROW_BUFFERS, ts, w), arr.dtype) for arr, w, _ in rows]
               + [pltpu.SemaphoreType.DMA((ROW_BUFFERS,)) for _ in rows])
    return pl.pallas_call(body, out_shape=out_shape, grid=(n_steps,), in_specs=in_specs, out_specs=out_specs,
                          scratch_shapes=scratch, compiler_params=_params(("arbitrary",)),
                          name=name)(*[r[0] for r in rows], *vecs, *deps)


def _rms(x, g):
    r = lax.rsqrt(jnp.mean(x * x, axis=-1, keepdims=True) + RMS_EPS)
    return x * r * g


def _rms_bwd(dy, x, g):
    r = lax.rsqrt(jnp.mean(x * x, axis=-1, keepdims=True) + RMS_EPS)
    xh = x * r
    dxh = dy * g
    dx = r * (dxh - xh * jnp.mean(dxh * xh, axis=-1, keepdims=True))
    return dx, jnp.sum(dy * xh, axis=0, keepdims=True)


def _sigmoid(x):
    return pl.reciprocal(1.0 + jnp.exp(-x), approx=True)


_GELU_C = math.sqrt(2.0 / math.pi)


def _gelu(x):
    return 0.5 * x * (1.0 + jnp.tanh(_GELU_C * (x + 0.044715 * x * x * x)))


def _gelu_grad(x):
    t = jnp.tanh(_GELU_C * (x + 0.044715 * x * x * x))
    return 0.5 * (1.0 + t) + 0.5 * x * (1.0 - t * t) * _GELU_C * (1.0 + 3.0 * 0.044715 * x * x)


ATTN_INTERLEAVE = 8
KEY_PAD = BLK * max(DILATIONS)


def _key_mask(n):
    ii = lax.broadcasted_iota(jnp.int32, (BLK, 2 * BLK), 0)
    jj = lax.broadcasted_iota(jnp.int32, (BLK, 2 * BLK), 1)
    return ((jj < BLK) & (jj >= ii) & (n > 0)) | ((jj >= BLK) & (jj - BLK <= ii))


def _units(d, nblk):
    nb = nblk // d
    if nb == 2:
        def unit(idx):
            ii = lax.broadcasted_iota(jnp.int32, (2 * BLK, 2 * BLK), 0)
            jj = lax.broadcasted_iota(jnp.int32, (2 * BLK, 2 * BLK), 1)
            return pl.ds(idx, 2 * BLK, stride=d), pl.ds(KEY_PAD + idx, 2 * BLK, stride=d), (jj <= ii) & (ii - jj <= BLK)
        return d, max(1, ATTN_INTERLEAVE // 4), unit

    def unit(idx):
        r, n = idx // nb, idx % nb
        cur = r + n * (BLK * d)
        keys = cur + (KEY_PAD - BLK * d)
        if d == 1:
            return pl.ds(pl.multiple_of(cur, BLK), BLK), pl.ds(pl.multiple_of(keys, BLK), 2 * BLK), _key_mask(n)
        return pl.ds(cur, BLK, stride=d), pl.ds(keys, 2 * BLK, stride=d), _key_mask(n)
    return nblk, ATTN_INTERLEAVE, unit


def _pad_keys(dst, src):
    dst[pl.ds(0, KEY_PAD), :] = jnp.zeros((KEY_PAD, dst.shape[1]), F32)

    def copy(c, carry):
        dst[pl.ds(pl.multiple_of(KEY_PAD + c * BLK, BLK), BLK), :] = src[pl.ds(pl.multiple_of(c * BLK, BLK), BLK), :]
        return carry

    lax.fori_loop(0, src.shape[0] // BLK, copy, 0)


def attn_fwd(proj, n_heads, *, name):
    S, WP = proj.shape
    assert S % (BLK * max(DILATIONS)) == 0
    nblk = S // BLK
    AW = n_heads * HEAD_DIM
    scale = 1.0 / math.sqrt(HEAD_DIM)

    def body(q_ref, k_ref, v_ref, o_ref, l_ref, acc, mrun, lrun, kp, vp):
        _pad_keys(kp, k_ref)
        _pad_keys(vp, v_ref)
        for first, d in zip((True, False, False), reversed(DILATIONS)):
            n_units, per_step, unit = _units(d, nblk)

            def step(it, carry, first=first, n_units=n_units, per_step=per_step, unit=unit):
                units = [unit(it + j * (n_units // per_step)) for j in range(per_step)]
                ss = [lax.dot_general(q_ref[cur, :].astype(BF16), kp[keys, :].astype(BF16), _NT,
                                      preferred_element_type=F32) * scale for cur, keys, _ in units]
                ss = [jnp.where(mask, s, NEG_INF) for s, (_, _, mask) in zip(ss, units)]
                ms = [jnp.max(s, axis=-1, keepdims=True) for s in ss]
                ps = [jnp.exp(s - m) for s, m in zip(ss, ms)]
                ls = [jnp.sum(p, axis=-1, keepdims=True) for p in ps]
                os_ = [jnp.dot(p.astype(BF16), vp[keys, :].astype(BF16), preferred_element_type=F32)
                       for p, (_, keys, _) in zip(ps, units)]
                for (cur, keys, mask), m, l, o in zip(units, ms, ls, os_):
                    m = jnp.broadcast_to(m, o.shape)
                    l = jnp.broadcast_to(l, o.shape)
                    if first:
                        acc[cur, :], mrun[cur, :], lrun[cur, :] = o, m, l
                    else:
                        m_old = mrun[cur, :]
                        m_new = jnp.maximum(m_old, m)
                        w_old, w_blk = jnp.exp(m_old - m_new), jnp.exp(m - m_new)
                        acc[cur, :] = w_old * acc[cur, :] + w_blk * o
                        lrun[cur, :] = w_old * lrun[cur, :] + w_blk * l
                        mrun[cur, :] = m_new
                return carry

            lax.fori_loop(0, n_units // per_step, step, 0)

        def finish(c, carry):
            r = pl.ds(pl.multiple_of(c * BLK, BLK), BLK)
            o_ref[r, :] = acc[r, :] / lrun[r, :]
            l_ref[r, :] = mrun[r, :] + jnp.log(lrun[r, :])
            return carry

        lax.fori_loop(0, nblk, finish, 0)

    def col(off):
        return pl.BlockSpec((S, HEAD_DIM), lambda h: (0, off + h))

    ospec = pl.BlockSpec((S, HEAD_DIM), lambda h: (0, h))
    return pl.pallas_call(
        body, out_shape=[jax.ShapeDtypeStruct((S, AW), F32)] * 2, grid=(n_heads,),
        in_specs=[col(0), col(n_heads), col(2 * n_heads)], out_specs=[ospec, ospec],
        scratch_shapes=[pltpu.VMEM((S, HEAD_DIM), F32)] * 3 + [pltpu.VMEM((KEY_PAD + S, HEAD_DIM), F32)] * 2,
        compiler_params=_params(("parallel",)), name=name)(proj, proj, proj)


def attn_bwd(proj, do, lse, delta, n_heads, *, name):
    S, WP = proj.shape
    nblk = S // BLK
    AW = n_heads * HEAD_DIM
    scale = 1.0 / math.sqrt(HEAD_DIM)

    def body(q_ref, k_ref, v_ref, do_ref, l_ref, dl_ref, dq_ref, dk_ref, dv_ref, dq_sc, dk_sc, dv_sc, kp, vp):
        _pad_keys(kp, k_ref)
        _pad_keys(vp, v_ref)
        order = list(reversed(DILATIONS))
        assign_first = nblk // order[0] == 2
        if assign_first:
            dk_sc[pl.ds(0, KEY_PAD), :] = jnp.zeros((KEY_PAD, HEAD_DIM), F32)
            dv_sc[pl.ds(0, KEY_PAD), :] = jnp.zeros((KEY_PAD, HEAD_DIM), F32)
        else:
            dq_sc[...] = jnp.zeros_like(dq_sc)
            dk_sc[...] = jnp.zeros_like(dk_sc)
            dv_sc[...] = jnp.zeros_like(dv_sc)
        for assign, d in zip((assign_first, False, False), order):
            n_units, per_step, unit = _units(d, nblk)

            def step(it, carry, n_units=n_units, per_step=per_step, unit=unit, assign=assign):
                units = [unit(it + j * (n_units // per_step)) for j in range(per_step)]
                qs = [q_ref[cur, :].astype(BF16) for cur, _, _ in units]
                gs = [do_ref[cur, :].astype(BF16) for cur, _, _ in units]
                ks = [kp[keys, :].astype(BF16) for _, keys, _ in units]
                ss = [lax.dot_general(q, kb, _NT, preferred_element_type=F32) * scale for q, kb in zip(qs, ks)]
                dps = [lax.dot_general(g, vp[keys, :].astype(BF16), _NT, preferred_element_type=F32)
                       for g, (_, keys, _) in zip(gs, units)]
                ps = [jnp.where(mask, jnp.exp(s - l_ref[cur, :][:, :1]), 0.0) for s, (cur, _, mask) in zip(ss, units)]
                dss = [(p * (dp - dl_ref[cur, :][:, :1]) * scale).astype(BF16)
                       for p, dp, (cur, _, _) in zip(ps, dps, units)]
                for (cur, keys, _), q, g, kb, p, ds in zip(units, qs, gs, ks, ps, dss):
                    dq = jnp.dot(ds, kb, preferred_element_type=F32)
                    dk = lax.dot_general(ds, q, _TN, preferred_element_type=F32)
                    dv = lax.dot_general(p.astype(BF16), g, _TN, preferred_element_type=F32)
                    if assign:
                        dq_sc[cur, :], dk_sc[keys, :], dv_sc[keys, :] = dq, dk, dv
                    else:
                        dq_sc[cur, :] += dq
                        dk_sc[keys, :] += dk
                        dv_sc[keys, :] += dv
                return carry

            lax.fori_loop(0, n_units // per_step, step, 0)
        rows = pl.ds(KEY_PAD, S)
        dq_ref[...] = dq_sc[...].astype(BF16)
        dk_ref[...] = dk_sc[rows, :].astype(BF16)
        dv_ref[...] = dv_sc[rows, :].astype(BF16)

    def col(off):
        return pl.BlockSpec((S, HEAD_DIM), lambda h: (0, off + h))

    ospec = pl.BlockSpec((S, HEAD_DIM), lambda h: (0, h))
    return pl.pallas_call(
        body, out_shape=[jax.ShapeDtypeStruct((S, AW), BF16)] * 3, grid=(n_heads,),
        in_specs=[col(0), col(n_heads), col(2 * n_heads), ospec, ospec, ospec], out_specs=[ospec] * 3,
        scratch_shapes=[pltpu.VMEM((S, HEAD_DIM), F32)] + [pltpu.VMEM((KEY_PAD + S, HEAD_DIM), F32)] * 4,
        compiler_params=_params(("parallel",), VMEM_LIMIT_SCAN), name=name)(proj, proj, proj, do, lse, delta)


def _to_segments(t):
    S, W = t.shape
    return t.reshape(SEGMENTS, S // SEGMENTS, W).swapaxes(0, 1).reshape(S, W)


def _from_segments(t):
    S, W = t.shape
    return t.reshape(S // SEGMENTS, SEGMENTS, W).swapaxes(0, 1).reshape(S, W)


def _cmul(ar, ai, br, bi):
    return ar * br - ai * bi, ar * bi + ai * br


def _power(ar, ai, log2n):
    for _ in range(log2n):
        ar, ai = _cmul(ar, ai, ar, ai)
    return ar, ai


def _shift_rows(x, up):
    row = lax.broadcasted_iota(jnp.int32, x.shape, 0)
    if up:
        return jnp.where(row == SEGMENTS - 1, 0.0, pltpu.roll(x, SEGMENTS - 1, 0))
    return jnp.where(row == 0, 0.0, pltpu.roll(x, 1, 0))


def _segment_carries(er, ei, pr, pi, up):
    cr = jnp.zeros_like(er)
    ci = jnp.zeros_like(ei)
    for _ in range(SEGMENTS - 1):
        tr, ti = _cmul(pr, pi, cr, ci)
        cr, ci = _shift_rows(er + tr, up), _shift_rows(ei + ti, up)
    return cr, ci


def _scan_states(sr, si, ar, ai, T, reverse):
    ns = sr.shape[1]
    ar8 = jnp.broadcast_to(ar, (SEGMENTS, ns))
    ai8 = jnp.broadcast_to(ai, (SEGMENTS, ns))

    def rows(t):
        k = (T - 1 - t) if reverse else t
        return pl.ds(pl.multiple_of(k * SEGMENTS, SEGMENTS), SEGMENTS)

    def advance(t, c):
        tr, ti = _cmul(ar8, ai8, c[0], c[1])
        return tr + sr[rows(t), :], ti + si[rows(t), :]

    def several(step):
        def trip(t, c):
            for j in range(SCAN_UNROLL):
                c = step(t * SCAN_UNROLL + j, c)
            return c
        return trip

    zero = jnp.zeros((SEGMENTS, ns), F32)
    er, ei = lax.fori_loop(0, T // SCAN_UNROLL, several(advance), (zero, zero))
    pr, pi = _power(ar, ai, T.bit_length() - 1)
    cr, ci = _segment_carries(er, ei, jnp.broadcast_to(pr, (SEGMENTS, ns)), jnp.broadcast_to(pi, (SEGMENTS, ns)), reverse)

    def store(t, c):
        nr, ni = advance(t, c)
        sr[rows(t), :] = nr
        si[rows(t), :] = ni
        return nr, ni

    lax.fori_loop(0, T // SCAN_UNROLL, several(store), (cr, ci))
    return cr, ci


def _slab_specs(ns):
    return [pl.BlockSpec((None, LANES, ns), lambda g: (g, 0, 0)),
            pl.BlockSpec((None, LANES, ns), lambda g: (g, 0, 0)),
            pl.BlockSpec((None, 1, ns), lambda g: (g, 0, 0)),
            pl.BlockSpec((None, 1, ns), lambda g: (g, 0, 0)),
            pl.BlockSpec((None, ns, LANES), lambda g: (g, 0, 0)),
            pl.BlockSpec((None, ns, LANES), lambda g: (g, 0, 0)),
            pl.BlockSpec((1, LANES), lambda g: (0, g))]


def _chunks(S):
    rc = _tile(S, 512, 16)
    return rc, S // rc


def ssm_fwd(u, bbr, bbi, ar, ai, cbr, cbi, dsk, *, name):
    S, SW = u.shape
    nslab, _, ns = bbr.shape
    T = S // SEGMENTS
    assert T & (T - 1) == 0
    rc, nc = _chunks(S)

    def body(u_ref, br_ref, bi_ref, ar_ref, ai_ref, cr_ref, ci_ref, d_ref, y_ref, yg_ref, str_ref, sti_ref, sr, si):
        def inputs(c, carry):
            r = pl.ds(pl.multiple_of(c * rc, rc), rc)
            sr[r, :] = jnp.dot(u_ref[r, :], br_ref[...], preferred_element_type=F32)
            si[r, :] = jnp.dot(u_ref[r, :], bi_ref[...], preferred_element_type=F32)
            return carry

        lax.fori_loop(0, nc, inputs, 0)
        _scan_states(sr, si, ar_ref[...], ai_ref[...], T, False)

        def outputs(c, carry):
            r = pl.ds(pl.multiple_of(c * rc, rc), rc)
            srb, sib = sr[r, :].astype(BF16), si[r, :].astype(BF16)
            str_ref[r, :] = srb
            sti_ref[r, :] = sib
            y = (jnp.dot(srb, cr_ref[...], preferred_element_type=F32)
                 - jnp.dot(sib, ci_ref[...], preferred_element_type=F32) + d_ref[...] * u_ref[r, :].astype(F32))
            y_ref[r, :] = y
            yg_ref[r, :] = _gelu(y).astype(BF16)
            return carry

        lax.fori_loop(0, nc, outputs, 0)

    slab = pl.BlockSpec((S, LANES), lambda g: (0, g))
    states = pl.BlockSpec((S, ns), lambda g: (0, g))
    return pl.pallas_call(
        body, out_shape=([jax.ShapeDtypeStruct((S, SW), F32), jax.ShapeDtypeStruct((S, SW), BF16)]
                         + [jax.ShapeDtypeStruct((S, nslab * ns), BF16)] * 2),
        grid=(nslab,), in_specs=[slab] + _slab_specs(ns), out_specs=[slab, slab, states, states],
        scratch_shapes=[pltpu.VMEM((S, ns), F32)] * 2,
        compiler_params=_params(("parallel",), VMEM_LIMIT_SCAN), name=name)(u, bbr, bbi, ar, ai, cbr, cbi, dsk)


def ssm_bwd(u, d_direct, d_gate, y, st_r, st_i, bbr, bbi, ar, ai, cbr, cbi, dsk, *, name, deps=()):
    S, SW = u.shape
    nslab, _, ns = bbr.shape
    T = S // SEGMENTS
    rc, nc = _chunks(S)
    pair_rows = 2 * SEGMENTS

    def body(*refs):
        (u_ref, d1_ref, d2_ref, y_ref, sr_ref, si_ref, br_ref, bi_ref, ar_ref, ai_ref, cr_ref, ci_ref,
         d_ref) = refs[:13]
        (du_ref, dbr_ref, dbi_ref, dcr_ref, dci_ref, dar_ref, dai_ref, dd_ref, lr, li,
         dy_ref) = refs[13 + len(deps):]

        def inputs(c, skip):
            r = pl.ds(pl.multiple_of(c * rc, rc), rc)
            dy = (d1_ref[r, :] + d2_ref[r, :].astype(F32)) * _gelu_grad(y_ref[r, :])
            dy_ref[r, :] = dy
            gb = dy.astype(BF16)
            lr[r, :] = lax.dot_general(gb, cr_ref[...], _NT, preferred_element_type=F32)
            li[r, :] = -lax.dot_general(gb, ci_ref[...], _NT, preferred_element_type=F32)
            return skip + jnp.sum(dy * u_ref[r, :].astype(F32), axis=0, keepdims=True)

        dd_ref[...] = lax.fori_loop(0, nc, inputs, jnp.zeros((1, LANES), F32))
        _scan_states(lr, li, ar_ref[...], -ai_ref[...], T, True)

        def steps(j):
            rows = pl.ds(pl.multiple_of(j * pair_rows, pair_rows), pair_rows)
            tr, ti = sr_ref[rows, :].astype(F32), si_ref[rows, :].astype(F32)
            return tr[:SEGMENTS], tr[SEGMENTS:], ti[:SEGMENTS], ti[SEGMENTS:]

        def pair(j, c):
            acc_r, acc_i, pr, pi = c
            lo_r, hi_r, lo_i, hi_i = steps(j)
            first = pl.ds(pl.multiple_of(j * pair_rows, SEGMENTS), SEGMENTS)
            second = pl.ds(pl.multiple_of(j * pair_rows + SEGMENTS, SEGMENTS), SEGMENTS)
            la_r, la_i, lb_r, lb_i = lr[first, :], li[first, :], lr[second, :], li[second, :]
            return (acc_r + la_r * pr + la_i * pi + lb_r * lo_r + lb_i * lo_i,
                    acc_i - la_r * pi + la_i * pr - lb_r * lo_i + lb_i * lo_r, hi_r, hi_i)

        def pairs(t, c):
            return pair(2 * t + 1, pair(2 * t, c))

        _, end_r, _, end_i = steps(T // 2 - 1)
        zero = jnp.zeros((SEGMENTS, ns), F32)
        acc = lax.fori_loop(0, T // 4, pairs, (zero, zero, _shift_rows(end_r, False), _shift_rows(end_i, False)))
        dar_ref[...] = jnp.sum(acc[0], axis=0, keepdims=True)
        dai_ref[...] = jnp.sum(acc[1], axis=0, keepdims=True)

        dbr_ref[...] = jnp.zeros_like(dbr_ref)
        dbi_ref[...] = jnp.zeros_like(dbi_ref)
        dcr_ref[...] = jnp.zeros_like(dcr_ref)
        dci_ref[...] = jnp.zeros_like(dci_ref)

        def outputs(c, carry):
            r = pl.ds(pl.multiple_of(c * rc, rc), rc)
            ub = u_ref[r, :]
            g = dy_ref[r, :]
            gb = g.astype(BF16)
            lrb = lr[r, :].astype(BF16)
            lib = li[r, :].astype(BF16)
            du_ref[r, :] = (lax.dot_general(lrb, br_ref[...], _NT, preferred_element_type=F32)
                            + lax.dot_general(lib, bi_ref[...], _NT, preferred_element_type=F32)
                            + d_ref[...] * g).astype(BF16)
            dbr_ref[...] += lax.dot_general(ub, lrb, _TN, preferred_element_type=F32)
            dbi_ref[...] += lax.dot_general(ub, lib, _TN, preferred_element_type=F32)
            dcr_ref[...] += lax.dot_general(sr_ref[r, :], gb, _TN, preferred_element_type=F32)
            dci_ref[...] -= lax.dot_general(si_ref[r, :], gb, _TN, preferred_element_type=F32)
            return carry

        lax.fori_loop(0, nc, outputs, 0)

    slab = pl.BlockSpec((S, LANES), lambda g: (0, g))
    states = pl.BlockSpec((S, ns), lambda g: (0, g))
    bspec = pl.BlockSpec((None, LANES, ns), lambda g: (g, 0, 0))
    cspec = pl.BlockSpec((None, ns, LANES), lambda g: (g, 0, 0))
    aspec = pl.BlockSpec((None, 1, ns), lambda g: (g, 0, 0))
    return pl.pallas_call(
        body,
        out_shape=[jax.ShapeDtypeStruct((S, SW), BF16),
                   jax.ShapeDtypeStruct((nslab, LANES, ns), F32), jax.ShapeDtypeStruct((nslab, LANES, ns), F32),
                   jax.ShapeDtypeStruct((nslab, ns, LANES), F32), jax.ShapeDtypeStruct((nslab, ns, LANES), F32),
                   jax.ShapeDtypeStruct((nslab, 1, ns), F32), jax.ShapeDtypeStruct((nslab, 1, ns), F32),
                   jax.ShapeDtypeStruct((1, SW), F32)],
        grid=(nslab,), in_specs=[slab, slab, slab, slab, states, states] + _slab_specs(ns) + [_ANY] * len(deps),
        out_specs=[slab, bspec, bspec, cspec, cspec, aspec, aspec, pl.BlockSpec((1, LANES), lambda g: (0, g))],
        scratch_shapes=[pltpu.VMEM((S, ns), F32)] * 2 + [pltpu.VMEM((S, LANES), F32)],
        compiler_params=_params(("parallel",), VMEM_LIMIT_SCAN), name=name)(
            u, d_direct, d_gate, y, st_r, st_i, bbr, bbi, ar, ai, cbr, cbi, dsk, *deps)


def _discretise(lam_re, lam_im, log_dt, b_re, b_im):
    dt = jnp.exp(log_dt)[:, None]
    mag = jnp.exp(lam_re * dt)
    ar = mag * jnp.cos(lam_im * dt)
    ai = mag * jnp.sin(lam_im * dt)
    nr, ni = ar - 1.0, ai
    den = lam_re * lam_re + lam_im * lam_im
    cr = ((nr * lam_re + ni * lam_im) / den)[..., None]
    ci = ((ni * lam_re - nr * lam_im) / den)[..., None]
    return ar, ai, cr * b_re - ci * b_im, cr * b_im + ci * b_re


def _block_diag(t, nslab):
    G, R, C = t.shape
    eye = jnp.eye(SLAB_GROUPS, dtype=t.dtype)
    t = t.reshape(nslab, SLAB_GROUPS, R, C)
    return jnp.einsum('sgrc,gh->sgrhc', t, eye).reshape(nslab, SLAB_GROUPS * R, SLAB_GROUPS * C)


def _block_diag_part(t, R, C):
    nslab = t.shape[0]
    eye = jnp.eye(SLAB_GROUPS, dtype=t.dtype)
    t = t.reshape(nslab, SLAB_GROUPS, R, SLAB_GROUPS, C)
    return jnp.einsum('sgrhc,gh->sgrc', t, eye).reshape(nslab * SLAB_GROUPS, R, C)


def _place():
    return lax.axis_index("x"), lax.axis_index("y"), lax.axis_index("c")


_HBM = pl.BlockSpec(memory_space=pltpu.HBM)
_SEM = pl.BlockSpec(memory_space=pltpu.SEMAPHORE)
_ORDERED_EFFECT = pltpu.SideEffectType.DATAFLOW_SIDE_EFFECTING


def _split_call(name, srcs, zones, sems_in, n_new, body_fn, after):
    nsrc, nz, ns, nn = len(srcs), len(zones), len(sems_in), len(n_new)
    nb = nsrc + nz

    def body(*refs):
        outs = refs[nb + ns + 1:]
        body_fn(refs[:nb], refs[nb:nb + ns], outs[:nn])
        outs[nn + nz][...] = jnp.zeros((SUBLANES, LANES), F32)

    res = pl.pallas_call(
        body, name=name,
        out_shape=([pltpu.SemaphoreType.DMA((n,)) for n in n_new] + [pltpu.HBM(b.shape, b.dtype) for b in zones]
                   + [jax.ShapeDtypeStruct((SUBLANES, LANES), F32)]),
        in_specs=[_HBM] * nb + [_SEM] * ns + [_ANY],
        out_specs=[_SEM] * nn + [_HBM] * nz + [pl.BlockSpec(memory_space=pltpu.VMEM)],
        input_output_aliases={nsrc + i: nn + i for i in range(nz)},
        compiler_params=pltpu.CompilerParams(has_side_effects=_ORDERED_EFFECT))(
            *[pltpu.with_memory_space_constraint(b, pltpu.HBM) for b in list(srcs) + list(zones)], *sems_in, after)
    return list(res[:nn]), list(res[nn:nn + nz]), res[-1]


def _mesh_peers():
    x, y, c = _place()
    return x, y, c, (x, y, 1 - c), [(1 - x, y), (x, 1 - y), (1 - x, 1 - y)]


def gather_start(shards, after, *, name):
    nw = len(shards)
    x, y, c = _place()
    zones = [lax.dynamic_update_slice(lax.empty((N_DEV,) + s.shape, s.dtype), s[None], (4 * x + 2 * y + c, 0, 0))
             for s in shards]

    def body(bufs, taken, new):
        for cp in _gather_first(bufs, nw, new[0], new[1]):
            cp.start()

    sems, zones, token = _split_call(name, shards, zones, [], [4 * nw, 4 * nw], body, after)
    return shards, sems, zones, token


def _gather_first(bufs, nw, send, recv):
    x, y, c, sibling, chips = _mesh_peers()
    out = []
    for w in range(nw):
        slot = bufs[nw + w].at[4 * x + 2 * y + c]
        for k, to in enumerate([sibling] + [(*ch, c) for ch in chips]):
            out.append(pltpu.make_async_remote_copy(
                src_ref=bufs[w], dst_ref=slot, send_sem=send.at[4 * w + k], recv_sem=recv.at[4 * w + k],
                device_id=to, device_id_type=MESH))
    return out


def _gather_slot_copy(bufs, nw, w, block, send_sem, recv_sem, to):
    px, py, pc = block
    slot = bufs[nw + w].at[4 * px + 2 * py + pc]
    return pltpu.make_async_remote_copy(src_ref=slot, dst_ref=slot, send_sem=send_sem, recv_sem=recv_sem,
                                        device_id=to, device_id_type=MESH)


def gather_forward(state, after, *, name):
    shards, sems, zones, _ = state
    nw = len(shards)

    def body(bufs, taken, new):
        x, y, c, sibling, chips = _mesh_peers()
        for j, ch in enumerate(chips):
            for w in range(nw):
                k = 4 * w + 1 + j
                _gather_slot_copy(bufs, nw, w, (*ch, c), taken[0].at[k], taken[1].at[k], (*ch, c)).wait_recv()
                _gather_slot_copy(bufs, nw, w, (*ch, c), new[0].at[3 * w + j], new[1].at[3 * w + j], sibling).start()
        for w in range(nw):
            _gather_slot_copy(bufs, nw, w, sibling, taken[0].at[4 * w], taken[1].at[4 * w], sibling).wait_recv()
        for cp in _gather_first(bufs, nw, taken[0], taken[1]):
            cp.wait_send()

    sems, zones, token = _split_call(name, shards, zones, sems, [3 * nw, 3 * nw], body, after)
    return shards, sems, zones, token


def gather_finish(state, after, *, name):
    shards, sems, zones, _ = state
    nw = len(shards)

    def body(bufs, taken, new):
        x, y, c, sibling, chips = _mesh_peers()
        for w in range(nw):
            for j, ch in enumerate(chips):
                cp = _gather_slot_copy(bufs, nw, w, (*ch, 1 - c), taken[0].at[3 * w + j], taken[1].at[3 * w + j], sibling)
                cp.wait_send()
                cp.wait_recv()

    _, zones, _ = _split_call(name, shards, zones, sems, [], body, after)
    return zones


def exchange_start(srcs, zone_shapes, copies, n, after, *, name):
    nw = len(srcs)
    zones = [lax.empty(z, s.dtype) for z, s in zip(zone_shapes, srcs)]

    def body(bufs, taken, new):
        for cp in copies(bufs[:nw], bufs[nw:], new[0], new[1]):
            cp.start()

    sems, zones, token = _split_call(name, srcs, zones, [], [n, n], body, after)
    return srcs, copies, sems, zones, token


def exchange_wait(state, after, *, name):
    srcs, copies, sems, zones, _ = state
    nw = len(srcs)

    def body(bufs, taken, new):
        for cp in copies(bufs[:nw], bufs[nw:], taken[0], taken[1]):
            cp.wait_send()
            cp.wait_recv()

    _, zones, _ = _split_call(name, srcs, zones, sems, [], body, after)
    return zones


def _core_copies(srcs, zones, send, recv):
    x, y, c = _place()
    return [pltpu.make_async_remote_copy(
        src_ref=srcs[w].at[:, 1 - c], dst_ref=zones[w], send_sem=send.at[w], recv_sem=recv.at[w],
        device_id=(x, y, 1 - c), device_id_type=MESH) for w in range(len(srcs))]


def _chip_copies(srcs, zones, send, recv):
    x, y, c = _place()
    chips = [(1 - x, y), (x, 1 - y), (1 - x, 1 - y)]
    return [pltpu.make_async_remote_copy(
        src_ref=srcs[w].at[2 * cx + cy], dst_ref=zones[w].at[j], send_sem=send.at[3 * w + j],
        recv_sem=recv.at[3 * w + j], device_id=(cx, cy, c), device_id_type=MESH)
        for w in range(len(srcs)) for j, (cx, cy) in enumerate(chips)]


def _blocked(fn, ins, outs, *, name, place=None, tr=256):
    k, n = outs[0][0]
    tr = _tile(k, tr, 16)
    if place is None:
        place = jnp.zeros((1,), jnp.int32)
    specs = []
    args = []
    for a in ins:
        if isinstance(a, tuple):
            arr, lead = a
            specs.append(pl.BlockSpec((None, tr, n), functools.partial(lambda i, s, lead: (*lead(i, s), 0), lead=lead)))
            args.append(arr)
        else:
            specs.append(pl.BlockSpec((tr, n), lambda i, s: (i, 0)))
            args.append(a)
    nin = len(args)

    def body(place_ref, *refs):
        res = fn(*[r[...] for r in refs[:nin]])
        for ref, val in zip(refs[nin:], res):
            ref[...] = val.astype(ref.dtype)

    return pl.pallas_call(
        body, out_shape=[jax.ShapeDtypeStruct(s, d) for s, d in outs],
        grid_spec=pltpu.PrefetchScalarGridSpec(
            num_scalar_prefetch=1, grid=(k // tr,), in_specs=specs,
            out_specs=[pl.BlockSpec((tr, n), lambda i, s: (i, 0)) for _ in outs]),
        compiler_params=_params(("parallel",)), name=name)(place, *args)


def _adamw(w, g, m, v):
    m = ADAM_B1 * m + (1.0 - ADAM_B1) * g
    v = ADAM_B2 * v + (1.0 - ADAM_B2) * (g * g)
    m_hat = m / (1.0 - ADAM_B1 ** ADAM_STEP)
    v_hat = v / (1.0 - ADAM_B2 ** ADAM_STEP)
    delta = -ADAM_LR * (m_hat * pl.reciprocal(jnp.sqrt(v_hat) + ADAM_EPS, approx=True) + ADAM_WD * w)
    return delta, m, v


def kernel(x, p, mix_norm_pre, w_in, lam_re, lam_im, log_dt, ssm_b_re, ssm_b_im, ssm_c_re, ssm_c_im, ssm_d, w_glu, b_glu, attn_out_norm, ssm_out_norm, w_out, mix_norm_post, mlp_norm_pre, w_up, w_down, mlp_norm_post, ple_norm_pre, w_ple_gate, w_ple_proj, ple_norm_post, loss_target, m_mix_norm_pre, m_w_in, m_lam_re, m_lam_im, m_log_dt, m_ssm_b_re, m_ssm_b_im, m_ssm_c_re, m_ssm_c_im, m_ssm_d, m_w_glu, m_b_glu, m_attn_out_norm, m_ssm_out_norm, m_w_out, m_mix_norm_post, m_mlp_norm_pre, m_w_up, m_w_down, m_mlp_norm_post, m_ple_norm_pre, m_w_ple_gate, m_w_ple_proj, m_ple_norm_post, v_mix_norm_pre, v_w_in, v_lam_re, v_lam_im, v_log_dt, v_ssm_b_re, v_ssm_b_im, v_ssm_c_re, v_ssm_c_im, v_ssm_d, v_w_glu, v_b_glu, v_attn_out_norm, v_ssm_out_norm, v_w_out, v_mix_norm_post, v_mlp_norm_pre, v_w_up, v_w_down, v_mlp_norm_post, v_ple_norm_pre, v_w_ple_gate, v_w_ple_proj, v_ple_norm_post):
    weights = dict(mix_norm_pre=mix_norm_pre, w_in=w_in, lam_re=lam_re, lam_im=lam_im, log_dt=log_dt, ssm_b_re=ssm_b_re, ssm_b_im=ssm_b_im, ssm_c_re=ssm_c_re, ssm_c_im=ssm_c_im, ssm_d=ssm_d, w_glu=w_glu, b_glu=b_glu, attn_out_norm=attn_out_norm, ssm_out_norm=ssm_out_norm, w_out=w_out, mix_norm_post=mix_norm_post, mlp_norm_pre=mlp_norm_pre, w_up=w_up, w_down=w_down, mlp_norm_post=mlp_norm_post, ple_norm_pre=ple_norm_pre, w_ple_gate=w_ple_gate, w_ple_proj=w_ple_proj, ple_norm_post=ple_norm_post)
    mom_m = dict(mix_norm_pre=m_mix_norm_pre, w_in=m_w_in, lam_re=m_lam_re, lam_im=m_lam_im, log_dt=m_log_dt, ssm_b_re=m_ssm_b_re, ssm_b_im=m_ssm_b_im, ssm_c_re=m_ssm_c_re, ssm_c_im=m_ssm_c_im, ssm_d=m_ssm_d, w_glu=m_w_glu, b_glu=m_b_glu, attn_out_norm=m_attn_out_norm, ssm_out_norm=m_ssm_out_norm, w_out=m_w_out, mix_norm_post=m_mix_norm_post, mlp_norm_pre=m_mlp_norm_pre, w_up=m_w_up, w_down=m_w_down, mlp_norm_post=m_mlp_norm_post, ple_norm_pre=m_ple_norm_pre, w_ple_gate=m_w_ple_gate, w_ple_proj=m_w_ple_proj, ple_norm_post=m_ple_norm_post)
    mom_v = dict(mix_norm_pre=v_mix_norm_pre, w_in=v_w_in, lam_re=v_lam_re, lam_im=v_lam_im, log_dt=v_log_dt, ssm_b_re=v_ssm_b_re, ssm_b_im=v_ssm_b_im, ssm_c_re=v_ssm_c_re, ssm_c_im=v_ssm_c_im, ssm_d=v_ssm_d, w_glu=v_w_glu, b_glu=v_b_glu, attn_out_norm=v_attn_out_norm, ssm_out_norm=v_ssm_out_norm, w_out=v_w_out, mix_norm_post=v_mix_norm_post, mlp_norm_pre=v_mlp_norm_pre, w_up=v_w_up, w_down=v_w_down, mlp_norm_post=v_mlp_norm_post, ple_norm_pre=v_ple_norm_pre, w_ple_gate=v_w_ple_gate, w_ple_proj=v_w_ple_proj, ple_norm_post=v_ple_norm_post)
    order = list(weights)
    big = ["w_in", "w_glu", "w_out", "w_up", "w_down", "w_ple_gate", "w_ple_proj"]
    col_sharded = {"w_in", "w_up", "w_ple_proj"}
    small = [n for n in order if n not in big]

    _, S, D = x.shape
    xs = x[0]
    tgt = loss_target[0]
    AW = attn_out_norm.shape[1]
    SW = ssm_d.shape[1]
    H = AW // HEAD_DIM
    G = SW // SSM_GROUP
    nslab = G // SLAB_GROUPS
    P_, C_ = SSM_STATE, SSM_GROUP

    shard = {n: weights[n][0].astype(BF16) for n in big}
    W, WT = {}, {}

    def arrived(names, gathered):
        for n, g in zip(names, gathered):
            W[n] = g if n in col_sharded else g.reshape(1, N_DEV * g.shape[1], g.shape[2])

    def transposed(g):
        return jnp.swapaxes(g, 1, 2).reshape(1, g.shape[0] * g.shape[2], g.shape[1])

    g1, g2, g3, g4, g5, g6 = (weights[n] for n in ("mix_norm_pre", "mix_norm_post", "mlp_norm_pre",
                                                      "mlp_norm_post", "ple_norm_pre", "ple_norm_post"))
    ga, gs = attn_out_norm, ssm_out_norm
    gather_in = gather_start([shard["w_in"]], shard["w_in"], name="gather_w_in_start")
    (hn1,) = rowwise(lambda a, g: (_rms(a, g),), [xs], [g1], [(D, BF16)], deps=(gather_in[-1],), name="norm_in")
    gather_in = gather_forward(gather_in, hn1, name="gather_w_in_forward")
    arrived(["w_in"], gather_finish(gather_in, gather_in[-1], name="gather_w_in_finish"))
    WT["w_in"] = transposed(W["w_in"])
    early, mid, late = ["w_glu", "w_out"], ["w_up"], ["w_down", "w_ple_gate", "w_ple_proj"]
    gather_early = gather_start([shard[n] for n in early], W["w_in"], name="gather_early_start")
    gather_mid = gather_start([shard[n] for n in mid], gather_early[-1], name="gather_mid_start")
    gather_late = gather_start([shard[n] for n in late], gather_mid[-1], name="gather_late_start")

    (proj,) = mm_nn(hn1, W["w_in"], [F32], deps=(gather_late[-1],), name="proj_in")
    attn, lse = attn_fwd(proj, H, name="attn_fwd")
    gather_early = gather_forward(gather_early, attn, name="gather_early_forward")
    (mix_a,) = rowwise(lambda a, g: (_rms(a, g),), [attn], [ga], [(AW, BF16)], deps=(gather_early[-1],),
                       name="attn_norm")
    arrived(early, gather_finish(gather_early, mix_a, name="gather_early_finish"))

    a_r, a_i, bb_r, bb_i = _discretise(lam_re[0], lam_im[0], log_dt[0], ssm_b_re[0], ssm_b_im[0])
    ssm_consts = (_block_diag(bb_r.swapaxes(1, 2), nslab).astype(BF16), _block_diag(bb_i.swapaxes(1, 2), nslab).astype(BF16),
                  a_r.reshape(nslab, 1, SLAB_STATES), a_i.reshape(nslab, 1, SLAB_STATES),
                  _block_diag(ssm_c_re[0].swapaxes(1, 2), nslab).astype(BF16),
                  _block_diag(ssm_c_im[0].swapaxes(1, 2), nslab).astype(BF16), ssm_d)
    u_seg = _to_segments(proj[:, 3 * AW:]).astype(BF16)
    y_pre, yg, st_r, st_i = ssm_fwd(u_seg, *ssm_consts, name="ssm_fwd")
    gather_mid = gather_forward(gather_mid, y_pre, name="gather_mid_forward")
    (gl1,) = mm_nn(yg, W["w_glu"], [BF16], epi=lambda acc, b: (acc + b,), bias=b_glu, deps=(gather_mid[-1],),
                   name="glu_gate")
    (mix_s,) = rowwise(lambda yp, gl, g: (_rms(_gelu(yp) * _sigmoid(gl), g),), [y_pre, gl1], [gs], [(SW, BF16)],
                       name="ssm_glu_norm")
    mixed = [mix_a, _from_segments(mix_s)]
    (mo,) = mm_nn(mixed, W["w_out"], [BF16], name="mix_out")

    def resid_norm(h, t, gpost, gpre):
        hh = h + _rms(t, gpost)
        return hh, _rms(hh, gpre)

    h1, hn2 = rowwise(resid_norm, [xs, mo], [g2, g3], [(D, F32), (D, BF16)], name="resid_mix")
    arrived(mid, gather_finish(gather_mid, hn2, name="gather_mid_finish"))
    gather_late = gather_forward(gather_late, W["w_up"], name="gather_late_forward")
    WT["w_up"] = transposed(W["w_up"])

    def relu2(acc):
        r = jnp.maximum(acc, 0.0)
        return acc, r * r

    up, act = mm_nn(hn2, W["w_up"], [BF16, BF16], epi=relu2, deps=(gather_late[-1],), tm=1024, tn=1024, name="mlp_up")
    arrived(late, gather_finish(gather_late, act, name="gather_late_finish"))
    (ff,) = mm_nn(act, W["w_down"], [BF16], name="mlp_down")
    h2, hn3 = rowwise(resid_norm, [h1, ff], [g4, g5], [(D, F32), (D, BF16)], name="resid_mlp")
    (gl2,) = mm_nn(hn3, W["w_ple_gate"], [BF16], name="ple_gate")
    pb = p[0, 0].astype(BF16)
    (emb,) = mm_nn(pb, W["w_ple_proj"], [BF16], name="ple_proj")

    def head(h, gl, e, t, g):
        sg = _sigmoid(gl)
        ge = sg * e
        err = h + _rms(ge, g) - t
        dh = err * (1.0 / D)
        dge, dg = _rms_bwd(dh, ge, g)
        return dh, dge * e * sg * (1.0 - sg), dge * sg, jnp.sum(err * err, axis=0, keepdims=True), dg

    dh3, dgl2, demb, loss_part, dg6 = rowwise(head, [h2, gl2, emb, tgt], [g6], [(D, F32), (D, BF16), (D, BF16)],
                                             [D, D], name="ple_loss_head")
    loss = lax.psum(0.5 / D * jnp.sum(loss_part), ("x", "y", "c"))

    x_i, y_i, c_i = _place()
    place = jnp.stack([c_i, 2 * x_i + y_i]).astype(jnp.int32)
    grads, out_g, out_d, out_m, out_v = {}, {}, {}, {}, {}

    def to_sibling(names, after, tag):
        chunks = []
        for n in names:
            g = grads[n]
            g = g if n in col_sharded else g.reshape(N_DEV, g.shape[1] // N_DEV, g.shape[2])
            chunks.append(g.reshape(4, 2, g.shape[1], g.shape[2]))
        return chunks, exchange_start(chunks, [(4,) + g.shape[2:] for g in chunks], _core_copies, len(chunks), after,
                                      name=f"grads_to_sibling_{tag}")

    def to_chips(names, sent, after, tag):
        chunks, state = sent
        sums = []
        for n, g, r in zip(names, chunks, exchange_wait(state, after, name=f"grads_from_sibling_{tag}")):
            k, nn = g.shape[2], g.shape[3]
            kb = k // _tile(k, 512, 16)

            def mine(i, s, kb=kb):
                return 2 * (i // kb) + s[0], i % kb

            (s,) = _blocked(lambda a, b: (a.astype(F32) + b.astype(F32),),
                            [(g.reshape(N_DEV, k, nn), mine), r.reshape(4 * k, nn)],
                            [((4 * k, nn), BF16)], place=place, tr=k // kb, name=f"chip_sum_{n}")
            sums.append(s.reshape(4, k, nn))
        return sums, exchange_start(sums, [(3,) + s.shape[1:] for s in sums], _chip_copies, 3 * len(sums), sums[-1],
                                    name=f"grads_to_chips_{tag}")

    def update(w_, m_, v_, own, r0, r1, r2):
        g = own.astype(F32) + r0.astype(F32) + r1.astype(F32) + r2.astype(F32)
        return (g,) + _adamw(w_, g, m_, v_)

    def finish(names, sent, after, tag):
        sums, state = sent
        for n, s, r in zip(names, sums, exchange_wait(state, after, name=f"grads_from_chips_{tag}")):
            shp = weights[n].shape
            res = _blocked(update, [weights[n][0], mom_m[n][0], mom_v[n][0], (s, lambda i, p_: (p_[1], i)),
                                    (r, lambda i, p_: (0, i)), (r, lambda i, p_: (1, i)), (r, lambda i, p_: (2, i))],
                           [(shp[1:], F32)] * 4, place=place, tr=max(16, min(shp[1] // 8, 262144 // shp[2])),
                           name=f"adamw_{n}")
            out_g[n], out_d[n], out_m[n], out_v[n] = (t.reshape(shp) for t in res)
        return out_v[names[-1]]

    grads["w_ple_proj"] = mm_tn(pb, demb, N_DEV, name="grad_w_ple_proj")
    dhn3 = mm_nt(dgl2, W["w_ple_gate"], BF16, name="back_ple_gate")
    grads["w_ple_gate"] = mm_tn(hn3, dgl2, 1, name="grad_w_ple_gate")

    def back_resid(dh, dhn, h, t, gpre, gpost):
        d1, dgpre = _rms_bwd(dhn, h, gpre)
        dhh = dh + d1
        dt, dgpost = _rms_bwd(dhh, t, gpost)
        return dhh, dt, dgpre, dgpost

    dh2, dff, dg5, dg4 = rowwise(back_resid, [dh3, dhn3, h2, ff], [g5, g4], [(D, F32), (D, BF16)], [D, D],
                                 name="back_resid_mlp")
    dup = mm_nt(dff, W["w_down"], BF16, epi=lambda acc, u_: (acc * 2.0 * jnp.maximum(u_.astype(F32), 0.0),),
                extra=up, name="back_mlp_down")
    grads["w_down"] = mm_tn(act, dff, 1, name="grad_w_down")
    group_a = ["w_ple_proj", "w_ple_gate", "w_down"]
    sent_a = to_sibling(group_a, grads["w_down"], "a")
    (dhn2,) = mm_nn(dup, WT["w_up"], [BF16], deps=(sent_a[1][-1],), name="back_mlp_up")
    sent_a = to_chips(group_a, sent_a, dhn2, "a")
    grads["w_up"] = mm_tn(hn2, dup, N_DEV, deps=(sent_a[1][-1],), name="grad_w_up")
    dh1, dmo, dg3, dg2 = rowwise(back_resid, [dh2, dhn2, h1, mo], [g3, g2], [(D, F32), (D, BF16)], [D, D],
                                 name="back_resid_mix")
    dmixed = mm_nt(dmo, W["w_out"], BF16, name="back_mix_out")
    grads["w_out"] = mm_tn(mixed, dmo, 1, name="grad_w_out")

    def back_glu(dm, yp, gl, g):
        ygf = _gelu(yp)
        sg = _sigmoid(gl)
        dssm, dg = _rms_bwd(dm, ygf * sg, g)
        dgl = dssm * ygf * sg * (1.0 - sg)
        return dgl, dssm * sg, dg, jnp.sum(dgl, axis=0, keepdims=True)

    dgl1, dyg_direct, dgs, db_glu = rowwise(back_glu, [_to_segments(dmixed[:, AW:]), y_pre, gl1], [gs],
                                            [(SW, BF16), (SW, F32)], [SW, SW], name="back_glu")
    dyg_gate = mm_nt(dgl1, W["w_glu"], BF16, name="back_glu_gate")
    grads["w_glu"] = mm_tn(yg, dgl1, 1, name="grad_w_glu")
    group_b = ["w_up", "w_out", "w_glu"]
    sent_b = to_sibling(group_b, grads["w_glu"], "b")
    done_a = finish(group_a, sent_a, sent_b[1][-1], "a")

    du_seg, dbb_r, dbb_i, dcb_r, dcb_i, da_r, da_i, d_skip = ssm_bwd(
        u_seg, dyg_direct, dyg_gate, y_pre, st_r, st_i, *ssm_consts, deps=(done_a,), name="ssm_bwd")
    sent_b = to_chips(group_b, sent_b, du_seg, "b")

    def back_attn_norm(dm, a, g):
        da, dg = _rms_bwd(dm, a, g)
        prod = da * a
        delta = jnp.concatenate(
            [jnp.broadcast_to(jnp.sum(prod[:, h * HEAD_DIM:(h + 1) * HEAD_DIM], axis=-1, keepdims=True),
                              (prod.shape[0], HEAD_DIM)) for h in range(H)], axis=1)
        return da, delta, dg

    dattn, delta, dga = rowwise(back_attn_norm, [(dmixed, AW, 0), attn], [ga], [(AW, F32), (AW, F32)], [AW],
                                deps=(sent_b[1][-1],), name="back_attn_norm")
    dq, dk, dv = attn_bwd(proj, dattn, lse, delta, H, name="attn_bwd")
    dproj = [dq, dk, dv, _from_segments(du_seg)]
    (dhn1,) = mm_nn(dproj, WT["w_in"], [BF16], name="back_proj_in")

    def back_in(dh, dhn, a, g):
        d1, dg = _rms_bwd(dhn, a, g)
        return dh + d1, dg

    grad_x, dg1 = rowwise(back_in, [dh1, dhn1, xs], [g1], [(D, F32)], [D], name="back_norm_in")

    cot = dict(
        mix_norm_pre=dg1, mix_norm_post=dg2, mlp_norm_pre=dg3, mlp_norm_post=dg4, ple_norm_pre=dg5, ple_norm_post=dg6,
        attn_out_norm=dga, ssm_out_norm=dgs, b_glu=db_glu, ssm_d=d_skip,
        ssm_c_re=_block_diag_part(dcb_r, P_, C_).swapaxes(1, 2), ssm_c_im=_block_diag_part(dcb_i, P_, C_).swapaxes(1, 2),
        a_r=da_r.reshape(G, P_), a_i=da_i.reshape(G, P_),
        bb_r=_block_diag_part(dbb_r, C_, P_).swapaxes(1, 2), bb_i=_block_diag_part(dbb_i, C_, P_).swapaxes(1, 2))
    names = list(cot)
    flat = jnp.concatenate([cot[n].reshape(-1) for n in names])
    total = flat.shape[0]
    rows_ = -(-total // (LANES * 16)) * 16
    flat = jnp.pad(flat, (0, rows_ * LANES - total)).reshape(rows_, LANES)
    gather_small = gather_start([flat], flat, name="gather_small_start")
    grads["w_in"] = mm_tn(hn1, dproj, N_DEV, deps=(gather_small[-1],), tko=2048, name="grad_w_in")
    group_c = ["w_in"]
    sent_c = to_sibling(group_c, grads["w_in"], "c")
    done_b = finish(group_b, sent_b, sent_c[1][-1], "b")
    sent_c = to_chips(group_c, sent_c, done_b, "c")
    gather_small = gather_forward(gather_small, sent_c[1][-1], name="gather_small_forward")
    (every,) = gather_finish(gather_small, gather_small[-1], name="gather_small_finish")
    (summed,) = _blocked(lambda *t: (functools.reduce(lambda a, b: a + b, t),),
                         [(every, functools.partial(lambda i, p_, j: (j, i), j=j)) for j in range(N_DEV)],
                         [((rows_, LANES), F32)], name="sum_small_grads")
    summed = summed.reshape(-1)
    red, off = {}, 0
    for n in names:
        sz = cot[n].size
        red[n] = summed[off:off + sz].reshape(cot[n].shape)
        off += sz
    _, pull = jax.vjp(_discretise, lam_re[0], lam_im[0], log_dt[0], ssm_b_re[0], ssm_b_im[0])
    d_lre, d_lim, d_ldt, d_bre, d_bim = pull((red["a_r"], red["a_i"], red["bb_r"], red["bb_i"]))
    red.update(lam_re=d_lre, lam_im=d_lim, log_dt=d_ldt, ssm_b_re=d_bre, ssm_b_im=d_bim)

    def pack(d):
        t = jnp.concatenate([d[n].reshape(-1) for n in small])
        r_ = -(-t.shape[0] // (LANES * 16)) * 16
        return jnp.pad(t, (0, r_ * LANES - t.shape[0])).reshape(r_, LANES)

    sw, sg_, sm, sv = pack(weights), pack(red), pack(mom_m), pack(mom_v)
    sd, snm, snv = _blocked(lambda w_, g_, m_, v_: _adamw(w_, g_, m_, v_), [sw, sg_, sm, sv],
                            [(sw.shape, F32)] * 3, name="adamw_small")
    finish(group_c, sent_c, snv, "c")
    off = 0
    for n in small:
        sz = weights[n].size
        shp = weights[n].shape
        out_g[n] = red[n].reshape(shp)
        out_d[n] = sd.reshape(-1)[off:off + sz].reshape(shp)
        out_m[n] = snm.reshape(-1)[off:off + sz].reshape(shp)
        out_v[n] = snv.reshape(-1)[off:off + sz].reshape(shp)
        off += sz

    return (loss, grad_x[None], *[out_g[n] for n in order], *[out_d[n] for n in order],
            *[out_m[n] for n in order], *[out_v[n] for n in order])
```

```python
import functools
import math

import jax
import jax.numpy as jnp
from jax import lax
from jax.experimental import pallas as pl
from jax.experimental.pallas import tpu as pltpu

F32 = jnp.float32
BF16 = jnp.bfloat16
MESH = pl.DeviceIdType.MESH

N_DEV = 8
LANES = 128
SUBLANES = 8
VMEM_LIMIT = 48 * 1024 * 1024
VMEM_LIMIT_SCAN = 60 * 1024 * 1024

HEAD_DIM = 128
BLK = 128
DILATIONS = (1, 4, 16)
SSM_GROUP = 16
SSM_STATE = 64
SLAB_GROUPS = LANES // SSM_GROUP
SLAB_STATES = SLAB_GROUPS * SSM_STATE
SEGMENTS = SUBLANES
SCAN_UNROLL = 4
RMS_EPS = 1e-6
NEG_INF = -1e30

ADAM_LR = 0.001
ADAM_B1 = 0.9
ADAM_B2 = 0.999
ADAM_EPS = 1e-08
ADAM_WD = 0.01
ADAM_STEP = 10


def _tile(n, pref, unit=LANES):
    if n <= pref:
        return n
    t = (pref // unit) * unit
    while t > unit and n % t:
        t -= unit
    assert n % t == 0, (n, pref, unit)
    return t


def _params(sem=None, vmem=VMEM_LIMIT):
    return pltpu.CompilerParams(dimension_semantics=sem, vmem_limit_bytes=vmem)


_NN = (((1,), (0,)), ((), ()))
_NT = (((1,), (1,)), ((), ()))
_TN = (((0,), (0,)), ((), ()))


_ANY = pl.BlockSpec(memory_space=pl.ANY)


def _mm_call(dims, nk, na, nb, pick, n_extra, n_dep, n_out, epi, group=1, **kw):
    first_extra = na + nb
    first_out = first_extra + n_extra + n_dep
    kw["in_specs"] = list(kw["in_specs"]) + [_ANY] * n_dep

    def grouped(refs, step):
        if group == 1:
            return lax.dot_general(refs[0][...], refs[1][...], dims, preferred_element_type=F32)
        kp = refs[0].shape[1]
        return sum(lax.dot_general(refs[step * group + p][...], refs[na][pl.ds(p * kp, kp), :], dims,
                                   preferred_element_type=F32) for p in range(group))

    def single(*refs):
        extra = refs[first_extra:first_extra + n_extra]
        res = epi(grouped(refs, 0), *[e[...] for e in extra])
        for o, r in zip(refs[first_out:first_out + n_out], res):
            o[...] = r.astype(o.dtype)

    if nk == 1:
        assert na == group and nb == 1
        kw["scratch_shapes"] = []
        return pl.pallas_call(single, **kw)

    def body(*refs):
        extra = refs[first_extra:first_extra + n_extra]
        outs = refs[first_out:first_out + n_out]
        acc = refs[-1]
        k = pl.program_id(2)

        @pl.when(k == 0)
        def _():
            acc[...] = jnp.zeros_like(acc)

        def add(a_ref, b_ref):
            acc[...] += lax.dot_general(a_ref[...], b_ref[...], dims, preferred_element_type=F32)

        if na == nb == 1:
            add(refs[0], refs[1])
        elif group > 1:
            for step in range(nk):
                @pl.when(k == step)
                def _(step=step):
                    acc[...] += grouped(refs, step)
        else:
            pa, pb = pick(pl.program_id(0), pl.program_id(1), k)
            for x in range(na):
                for y in range(nb):
                    pl.when((pa == x) & (pb == y))(functools.partial(add, refs[x], refs[na + y]))

        @pl.when(k == nk - 1)
        def _():
            res = epi(acc[...], *[e[...] for e in extra])
            for o, r in zip(outs, res):
                o[...] = r.astype(o.dtype)

    return pl.pallas_call(body, **kw)


def _identity_epi(acc):
    return (acc,)


def _parts(t):
    return list(t) if isinstance(t, (list, tuple)) else [t]


def _part_spec(block, part, which, index):
    def index_map(i, j, k):
        use = which(i, j, k) == part
        r, c = index(i, j, k)
        return jnp.where(use, r, 0), jnp.where(use, c, 0)
    return pl.BlockSpec(block, index_map)


def mm_nn(a, w, out_dtypes, *, name, epi=_identity_epi, bias=None, deps=(), tm=2048, tn=512, tk=2048):
    a = _parts(a)
    M, Kp = a[0].shape
    K = Kp * len(a)
    J, K2, n = w.shape
    assert K == K2
    tm, tn, tk = _tile(M, tm, 16), _tile(n, tn), _tile(K, tk)
    npj = n // tn
    nk = K // tk
    group = 1
    if len(a) == 1:
        in_specs = [pl.BlockSpec((tm, tk), lambda i, j, k: (i, k))]
    else:
        assert tk % Kp == 0
        group = tk // Kp
        in_specs = [pl.BlockSpec((tm, Kp), lambda i, j, k: (i, 0)) for _ in a]
    in_specs.append(pl.BlockSpec((None, tk, tn), lambda i, j, k: (j // npj, k, j % npj)))
    args = a + [w]
    if bias is not None:
        in_specs.append(pl.BlockSpec((1, tn), lambda i, j, k: (0, j)))
        args.append(bias)
    return _mm_call(
        _NN, nk, len(a), 1, lambda i, j, k: (k, 0), len(args) - len(a) - 1, len(deps), len(out_dtypes), epi, group,
        out_shape=[jax.ShapeDtypeStruct((M, J * n), d) for d in out_dtypes],
        grid=(M // tm, J * npj, nk), in_specs=in_specs,
        out_specs=[pl.BlockSpec((tm, tn), lambda i, j, k: (i, j)) for _ in out_dtypes],
        scratch_shapes=[pltpu.VMEM((tm, tn), F32)],
        compiler_params=_params(("parallel", "parallel", "arbitrary")), name=name)(*args, *deps)


def mm_nt(a, w, out_dtype, *, name, epi=_identity_epi, extra=None, tm=2048, tko=512, tnr=2048):
    M, N = a.shape
    J, K, n = w.shape
    assert N == J * n
    tm, tko, tnr = _tile(M, tm, 16), _tile(K, tko), _tile(n, tnr)
    npj = n // tnr
    nk = N // tnr
    in_specs = [pl.BlockSpec((tm, tnr), lambda i, j, k: (i, k)),
                pl.BlockSpec((None, tko, tnr), lambda i, j, k: (k // npj, j, k % npj))]
    args = [a, w]
    if extra is not None:
        in_specs.append(pl.BlockSpec((tm, tko), lambda i, j, k: (i, j)))
        args.append(extra)
    return _mm_call(
        _NT, nk, 1, 1, None, len(args) - 2, 0, 1, epi,
        out_shape=[jax.ShapeDtypeStruct((M, K), out_dtype)],
        grid=(M // tm, K // tko, nk), in_specs=in_specs,
        out_specs=[pl.BlockSpec((tm, tko), lambda i, j, k: (i, j))],
        scratch_shapes=[pltpu.VMEM((tm, tko), F32)],
        compiler_params=_params(("parallel", "parallel", "arbitrary")), name=name)(*args)[0]


def mm_tn(a, b, J, *, name, deps=(), tko=1024, tn=1024, ts=2048):
    a, b = _parts(a), _parts(b)
    S, Kp = a[0].shape
    S2, Np = b[0].shape
    K, N = Kp * len(a), Np * len(b)
    assert S == S2 and N % J == 0
    n = N // J
    tko, tn, ts = _tile(Kp, tko), _tile(math.gcd(n, Np), tn), _tile(S, ts)
    npj = n // tn
    nk = S // ts
    ta, tb = Kp // tko, Np // tn
    assert nk > 1 or len(a) == len(b) == 1
    return _mm_call(
        _TN, nk, len(a), len(b), lambda i, j, k: (i // ta, j // tb), 0, len(deps), 1, _identity_epi,
        out_shape=[jax.ShapeDtypeStruct((J, K, n), BF16)],
        grid=(K // tko, J * npj, nk),
        in_specs=([_part_spec((ts, tko), x, lambda i, j, k: i // ta, lambda i, j, k: (k, i % ta)) for x in range(len(a))]
                  + [_part_spec((ts, tn), y, lambda i, j, k: j // tb, lambda i, j, k: (k, j % tb)) for y in range(len(b))]),
        out_specs=[pl.BlockSpec((None, tko, tn), lambda i, j, k: (j // npj, i, j % npj))],
        scratch_shapes=[pltpu.VMEM((tko, tn), F32)],
        compiler_params=_params(("parallel", "parallel", "arbitrary")), name=name)(*a, *b, *deps)[0]


ROW_BUFFERS = 3


def rowwise(fn, rows, vecs, outs, accs=(), *, name, deps=(), ts=256):
    rows = [r if isinstance(r, tuple) else (r, r.shape[1], 0) for r in rows]
    S = rows[0][0].shape[0]
    ts = _tile(S, ts, 16)
    n_steps = S // ts
    nr, nv, no, nd, na = len(rows), len(vecs), len(outs), len(deps), len(accs)

    def body(*refs):
        r, v = refs[:nr], refs[nr:nr + nv]
        o, a = refs[nr + nv + nd:nr + nv + nd + no], refs[nr + nv + nd + no:nr + nv + nd + no + na]
        bufs, sems = refs[-2 * nr:-nr], refs[-nr:]
        i = pl.program_id(0)

        def fetch(step, slot):
            src = pl.ds(pl.multiple_of(step * ts, ts), ts)
            return [pltpu.make_async_copy(r[x].at[src, pl.ds(cb * w, w)], bufs[x].at[slot], sems[x].at[slot])
                    for x, (_, w, cb) in enumerate(rows)]

        @pl.when(i == 0)
        def _():
            for step in range(min(ROW_BUFFERS - 1, n_steps)):
                for cp in fetch(step, step):
                    cp.start()

        ahead = i + (ROW_BUFFERS - 1)

        @pl.when(ahead < n_steps)
        def _():
            for cp in fetch(ahead, ahead % ROW_BUFFERS):
                cp.start()

        slot = i % ROW_BUFFERS
        for cp in fetch(i, slot):
            cp.wait()
        res = fn(*[b[slot].astype(F32) for b in bufs], *[t[...] for t in v])
        for ref, val in zip(o, res[:no]):
            ref[...] = val.astype(ref.dtype)
        if a:
            @pl.when(pl.program_id(0) == 0)
            def _():
                for ref in a:
                    ref[...] = jnp.zeros_like(ref)

            for ref, val in zip(a, res[no:]):
                ref[...] += val

    in_specs = [_ANY] * nr + [pl.BlockSpec(v.shape, lambda i: (0, 0)) for v in vecs] + [_ANY] * nd
    out_shape = [jax.ShapeDtypeStruct((S, w), d) for w, d in outs]
    out_shape += [jax.ShapeDtypeStruct((1, w), F32) for w in accs]
    out_specs = [pl.BlockSpec((ts, w), lambda i: (i, 0)) for w, _ in outs]
    out_specs += [pl.BlockSpec((1, w), lambda i: (0, 0)) for w in accs]
    scratch = ([pltpu.VMEM((ROW_BUFFERS, ts, w), arr.dtype) for arr, w, _ in rows]
               + [pltpu.SemaphoreType.DMA((ROW_BUFFERS,)) for _ in rows])
    return pl.pallas_call(body, out_shape=out_shape, grid=(n_steps,), in_specs=in_specs, out_specs=out_specs,
                          scratch_shapes=scratch, compiler_params=_params(("arbitrary",)),
                          name=name)(*[r[0] for r in rows], *vecs, *deps)


def _rms(x, g):
    r = lax.rsqrt(jnp.mean(x * x, axis=-1, keepdims=True) + RMS_EPS)
    return x * r * g


def _rms_bwd(dy, x, g):
    r = lax.rsqrt(jnp.mean(x * x, axis=-1, keepdims=True) + RMS_EPS)
    xh = x * r
    dxh = dy * g
    dx = r * (dxh - xh * jnp.mean(dxh * xh, axis=-1, keepdims=True))
    return dx, jnp.sum(dy * xh, axis=0, keepdims=True)


def _sigmoid(x):
    return pl.reciprocal(1.0 + jnp.exp(-x), approx=True)


_GELU_C = math.sqrt(2.0 / math.pi)


def _gelu(x):
    return 0.5 * x * (1.0 + jnp.tanh(_GELU_C * (x + 0.044715 * x * x * x)))


def _gelu_grad(x):
    t = jnp.tanh(_GELU_C * (x + 0.044715 * x * x * x))
    return 0.5 * (1.0 + t) + 0.5 * x * (1.0 - t * t) * _GELU_C * (1.0 + 3.0 * 0.044715 * x * x)


ATTN_INTERLEAVE = 8
KEY_PAD = BLK * max(DILATIONS)


def _key_mask(n):
    ii = lax.broadcasted_iota(jnp.int32, (BLK, 2 * BLK), 0)
    jj = lax.broadcasted_iota(jnp.int32, (BLK, 2 * BLK), 1)
    return ((jj < BLK) & (jj >= ii) & (n > 0)) | ((jj >= BLK) & (jj - BLK <= ii))


def _units(d, nblk):
    nb = nblk // d
    if nb == 2:
        def unit(idx):
            ii = lax.broadcasted_iota(jnp.int32, (2 * BLK, 2 * BLK), 0)
            jj = lax.broadcasted_iota(jnp.int32, (2 * BLK, 2 * BLK), 1)
            return pl.ds(idx, 2 * BLK, stride=d), pl.ds(KEY_PAD + idx, 2 * BLK, stride=d), (jj <= ii) & (ii - jj <= BLK)
        return d, max(1, ATTN_INTERLEAVE // 4), unit

    def unit(idx):
        r, n = idx // nb, idx % nb
        cur = r + n * (BLK * d)
        keys = cur + (KEY_PAD - BLK * d)
        if d == 1:
            return pl.ds(pl.multiple_of(cur, BLK), BLK), pl.ds(pl.multiple_of(keys, BLK), 2 * BLK), _key_mask(n)
        return pl.ds(cur, BLK, stride=d), pl.ds(keys, 2 * BLK, stride=d), _key_mask(n)
    return nblk, ATTN_INTERLEAVE, unit


def _pad_keys(dst, src):
    dst[pl.ds(0, KEY_PAD), :] = jnp.zeros((KEY_PAD, dst.shape[1]), F32)

    def copy(c, carry):
        dst[pl.ds(pl.multiple_of(KEY_PAD + c * BLK, BLK), BLK), :] = src[pl.ds(pl.multiple_of(c * BLK, BLK), BLK), :]
        return carry

    lax.fori_loop(0, src.shape[0] // BLK, copy, 0)


def attn_fwd(proj, n_heads, *, name):
    S, WP = proj.shape
    assert S % (BLK * max(DILATIONS)) == 0
    nblk = S // BLK
    AW = n_heads * HEAD_DIM
    scale = 1.0 / math.sqrt(HEAD_DIM)

    def body(q_ref, k_ref, v_ref, o_ref, l_ref, acc, mrun, lrun, kp, vp):
        _pad_keys(kp, k_ref)
        _pad_keys(vp, v_ref)
        for first, d in zip((True, False, False), reversed(DILATIONS)):
            n_units, per_step, unit = _units(d, nblk)

            def step(it, carry, first=first, n_units=n_units, per_step=per_step, unit=unit):
                units = [unit(it + j * (n_units // per_step)) for j in range(per_step)]
                ss = [lax.dot_general(q_ref[cur, :].astype(BF16), kp[keys, :].astype(BF16), _NT,
                                      preferred_element_type=F32) * scale for cur, keys, _ in units]
                ss = [jnp.where(mask, s, NEG_INF) for s, (_, _, mask) in zip(ss, units)]
                ms = [jnp.max(s, axis=-1, keepdims=True) for s in ss]
                ps = [jnp.exp(s - m) for s, m in zip(ss, ms)]
                ls = [jnp.sum(p, axis=-1, keepdims=True) for p in ps]
                os_ = [jnp.dot(p.astype(BF16), vp[keys, :].astype(BF16), preferred_element_type=F32)
                       for p, (_, keys, _) in zip(ps, units)]
                for (cur, keys, mask), m, l, o in zip(units, ms, ls, os_):
                    m = jnp.broadcast_to(m, o.shape)
                    l = jnp.broadcast_to(l, o.shape)
                    if first:
                        acc[cur, :], mrun[cur, :], lrun[cur, :] = o, m, l
                    else:
                        m_old = mrun[cur, :]
                        m_new = jnp.maximum(m_old, m)
                        w_old, w_blk = jnp.exp(m_old - m_new), jnp.exp(m - m_new)
                        acc[cur, :] = w_old * acc[cur, :] + w_blk * o
                        lrun[cur, :] = w_old * lrun[cur, :] + w_blk * l
                        mrun[cur, :] = m_new
                return carry

            lax.fori_loop(0, n_units // per_step, step, 0)

        def finish(c, carry):
            r = pl.ds(pl.multiple_of(c * BLK, BLK), BLK)
            o_ref[r, :] = acc[r, :] / lrun[r, :]
            l_ref[r, :] = mrun[r, :] + jnp.log(lrun[r, :])
            return carry

        lax.fori_loop(0, nblk, finish, 0)

    def col(off):
        return pl.BlockSpec((S, HEAD_DIM), lambda h: (0, off + h))

    ospec = pl.BlockSpec((S, HEAD_DIM), lambda h: (0, h))
    return pl.pallas_call(
        body, out_shape=[jax.ShapeDtypeStruct((S, AW), F32)] * 2, grid=(n_heads,),
        in_specs=[col(0), col(n_heads), col(2 * n_heads)], out_specs=[ospec, ospec],
        scratch_shapes=[pltpu.VMEM((S, HEAD_DIM), F32)] * 3 + [pltpu.VMEM((KEY_PAD + S, HEAD_DIM), F32)] * 2,
        compiler_params=_params(("parallel",)), name=name)(proj, proj, proj)


def attn_bwd(proj, do, lse, delta, n_heads, *, name):
    S, WP = proj.shape
    nblk = S // BLK
    AW = n_heads * HEAD_DIM
    scale = 1.0 / math.sqrt(HEAD_DIM)

    def body(q_ref, k_ref, v_ref, do_ref, l_ref, dl_ref, dq_ref, dk_ref, dv_ref, dq_sc, dk_sc, dv_sc, kp, vp):
        _pad_keys(kp, k_ref)
        _pad_keys(vp, v_ref)
        order = list(reversed(DILATIONS))
        assign_first = nblk // order[0] == 2
        if assign_first:
            dk_sc[pl.ds(0, KEY_PAD), :] = jnp.zeros((KEY_PAD, HEAD_DIM), F32)
            dv_sc[pl.ds(0, KEY_PAD), :] = jnp.zeros((KEY_PAD, HEAD_DIM), F32)
        else:
            dq_sc[...] = jnp.zeros_like(dq_sc)
            dk_sc[...] = jnp.zeros_like(dk_sc)
            dv_sc[...] = jnp.zeros_like(dv_sc)
        for assign, d in zip((assign_first, False, False), order):
            n_units, per_step, unit = _units(d, nblk)

            def step(it, carry, n_units=n_units, per_step=per_step, unit=unit, assign=assign):
                units = [unit(it + j * (n_units // per_step)) for j in range(per_step)]
                qs = [q_ref[cur, :].astype(BF16) for cur, _, _ in units]
                gs = [do_ref[cur, :].astype(BF16) for cur, _, _ in units]
                ks = [kp[keys, :].astype(BF16) for _, keys, _ in units]
                ss = [lax.dot_general(q, kb, _NT, preferred_element_type=F32) * scale for q, kb in zip(qs, ks)]
                dps = [lax.dot_general(g, vp[keys, :].astype(BF16), _NT, preferred_element_type=F32)
                       for g, (_, keys, _) in zip(gs, units)]
                ps = [jnp.where(mask, jnp.exp(s - l_ref[cur, :][:, :1]), 0.0) for s, (cur, _, mask) in zip(ss, units)]
                dss = [(p * (dp - dl_ref[cur, :][:, :1]) * scale).astype(BF16)
                       for p, dp, (cur, _, _) in zip(ps, dps, units)]
                for (cur, keys, _), q, g, kb, p, ds in zip(units, qs, gs, ks, ps, dss):
                    dq = jnp.dot(ds, kb, preferred_element_type=F32)
                    dk = lax.dot_general(ds, q, _TN, preferred_element_type=F32)
                    dv = lax.dot_general(p.astype(BF16), g, _TN, preferred_element_type=F32)
                    if assign:
                        dq_sc[cur, :], dk_sc[keys, :], dv_sc[keys, :] = dq, dk, dv
                    else:
                        dq_sc[cur, :] += dq
                        dk_sc[keys, :] += dk
                        dv_sc[keys, :] += dv
                return carry

            lax.fori_loop(0, n_units // per_step, step, 0)
        rows = pl.ds(KEY_PAD, S)
        dq_ref[...] = dq_sc[...].astype(BF16)
        dk_ref[...] = dk_sc[rows, :].astype(BF16)
        dv_ref[...] = dv_sc[rows, :].astype(BF16)

    def col(off):
        return pl.BlockSpec((S, HEAD_DIM), lambda h: (0, off + h))

    ospec = pl.BlockSpec((S, HEAD_DIM), lambda h: (0, h))
    return pl.pallas_call(
        body, out_shape=[jax.ShapeDtypeStruct((S, AW), BF16)] * 3, grid=(n_heads,),
        in_specs=[col(0), col(n_heads), col(2 * n_heads), ospec, ospec, ospec], out_specs=[ospec] * 3,
        scratch_shapes=[pltpu.VMEM((S, HEAD_DIM), F32)] + [pltpu.VMEM((KEY_PAD + S, HEAD_DIM), F32)] * 4,
        compiler_params=_params(("parallel",), VMEM_LIMIT_SCAN), name=name)(proj, proj, proj, do, lse, delta)


def _to_segments(t):
    S, W = t.shape
    return t.reshape(SEGMENTS, S // SEGMENTS, W).swapaxes(0, 1).reshape(S, W)


def _from_segments(t):
    S, W = t.shape
    return t.reshape(S // SEGMENTS, SEGMENTS, W).swapaxes(0, 1).reshape(S, W)


def _cmul(ar, ai, br, bi):
    return ar * br - ai * bi, ar * bi + ai * br


def _power(ar, ai, log2n):
    for _ in range(log2n):
        ar, ai = _cmul(ar, ai, ar, ai)
    return ar, ai


def _shift_rows(x, up):
    row = lax.broadcasted_iota(jnp.int32, x.shape, 0)
    if up:
        return jnp.where(row == SEGMENTS - 1, 0.0, pltpu.roll(x, SEGMENTS - 1, 0))
    return jnp.where(row == 0, 0.0, pltpu.roll(x, 1, 0))


def _segment_carries(er, ei, pr, pi, up):
    cr = jnp.zeros_like(er)
    ci = jnp.zeros_like(ei)
    for _ in range(SEGMENTS - 1):
        tr, ti = _cmul(pr, pi, cr, ci)
        cr, ci = _shift_rows(er + tr, up), _shift_rows(ei + ti, up)
    return cr, ci


def _scan_states(sr, si, ar, ai, T, reverse):
    ns = sr.shape[1]
    ar8 = jnp.broadcast_to(ar, (SEGMENTS, ns))
    ai8 = jnp.broadcast_to(ai, (SEGMENTS, ns))

    def rows(t):
        k = (T - 1 - t) if reverse else t
        return pl.ds(pl.multiple_of(k * SEGMENTS, SEGMENTS), SEGMENTS)

    def advance(t, c):
        tr, ti = _cmul(ar8, ai8, c[0], c[1])
        return tr + sr[rows(t), :], ti + si[rows(t), :]

    def several(step):
        def trip(t, c):
            for j in range(SCAN_UNROLL):
                c = step(t * SCAN_UNROLL + j, c)
            return c
        return trip

    zero = jnp.zeros((SEGMENTS, ns), F32)
    er, ei = lax.fori_loop(0, T // SCAN_UNROLL, several(advance), (zero, zero))
    pr, pi = _power(ar, ai, T.bit_length() - 1)
    cr, ci = _segment_carries(er, ei, jnp.broadcast_to(pr, (SEGMENTS, ns)), jnp.broadcast_to(pi, (SEGMENTS, ns)), reverse)

    def store(t, c):
        nr, ni = advance(t, c)
        sr[rows(t), :] = nr
        si[rows(t), :] = ni
        return nr, ni

    lax.fori_loop(0, T // SCAN_UNROLL, several(store), (cr, ci))
    return cr, ci


def _slab_specs(ns):
    return [pl.BlockSpec((None, LANES, ns), lambda g: (g, 0, 0)),
            pl.BlockSpec((None, LANES, ns), lambda g: (g, 0, 0)),
            pl.BlockSpec((None, 1, ns), lambda g: (g, 0, 0)),
            pl.BlockSpec((None, 1, ns), lambda g: (g, 0, 0)),
            pl.BlockSpec((None, ns, LANES), lambda g: (g, 0, 0)),
            pl.BlockSpec((None, ns, LANES), lambda g: (g, 0, 0)),
            pl.BlockSpec((1, LANES), lambda g: (0, g))]


def _chunks(S):
    rc = _tile(S, 512, 16)
    return rc, S // rc


def ssm_fwd(u, bbr, bbi, ar, ai, cbr, cbi, dsk, *, name):
    S, SW = u.shape
    nslab, _, ns = bbr.shape
    T = S // SEGMENTS
    assert T & (T - 1) == 0
    rc, nc = _chunks(S)

    def body(u_ref, br_ref, bi_ref, ar_ref, ai_ref, cr_ref, ci_ref, d_ref, y_ref, yg_ref, str_ref, sti_ref, sr, si):
        def inputs(c, carry):
            r = pl.ds(pl.multiple_of(c * rc, rc), rc)
            sr[r, :] = jnp.dot(u_ref[r, :], br_ref[...], preferred_element_type=F32)
            si[r, :] = jnp.dot(u_ref[r, :], bi_ref[...], preferred_element_type=F32)
            return carry

        lax.fori_loop(0, nc, inputs, 0)
        _scan_states(sr, si, ar_ref[...], ai_ref[...], T, False)

        def outputs(c, carry):
            r = pl.ds(pl.multiple_of(c * rc, rc), rc)
            srb, sib = sr[r, :].astype(BF16), si[r, :].astype(BF16)
            str_ref[r, :] = srb
            sti_ref[r, :] = sib
            y = (jnp.dot(srb, cr_ref[...], preferred_element_type=F32)
                 - jnp.dot(sib, ci_ref[...], preferred_element_type=F32) + d_ref[...] * u_ref[r, :].astype(F32))
            y_ref[r, :] = y
            yg_ref[r, :] = _gelu(y).astype(BF16)
            return carry

        lax.fori_loop(0, nc, outputs, 0)

    slab = pl.BlockSpec((S, LANES), lambda g: (0, g))
    states = pl.BlockSpec((S, ns), lambda g: (0, g))
    return pl.pallas_call(
        body, out_shape=([jax.ShapeDtypeStruct((S, SW), F32), jax.ShapeDtypeStruct((S, SW), BF16)]
                         + [jax.ShapeDtypeStruct((S, nslab * ns), BF16)] * 2),
        grid=(nslab,), in_specs=[slab] + _slab_specs(ns), out_specs=[slab, slab, states, states],
        scratch_shapes=[pltpu.VMEM((S, ns), F32)] * 2,
        compiler_params=_params(("parallel",), VMEM_LIMIT_SCAN), name=name)(u, bbr, bbi, ar, ai, cbr, cbi, dsk)


def ssm_bwd(u, d_direct, d_gate, y, st_r, st_i, bbr, bbi, ar, ai, cbr, cbi, dsk, *, name, deps=()):
    S, SW = u.shape
    nslab, _, ns = bbr.shape
    T = S // SEGMENTS
    rc, nc = _chunks(S)
    pair_rows = 2 * SEGMENTS

    def body(*refs):
        (u_ref, d1_ref, d2_ref, y_ref, sr_ref, si_ref, br_ref, bi_ref, ar_ref, ai_ref, cr_ref, ci_ref,
         d_ref) = refs[:13]
        (du_ref, dbr_ref, dbi_ref, dcr_ref, dci_ref, dar_ref, dai_ref, dd_ref, lr, li,
         dy_ref) = refs[13 + len(deps):]

        def inputs(c, skip):
            r = pl.ds(pl.multiple_of(c * rc, rc), rc)
            dy = (d1_ref[r, :] + d2_ref[r, :].astype(F32)) * _gelu_grad(y_ref[r, :])
            dy_ref[r, :] = dy
            gb = dy.astype(BF16)
            lr[r, :] = lax.dot_general(gb, cr_ref[...], _NT, preferred_element_type=F32)
            li[r, :] = -lax.dot_general(gb, ci_ref[...], _NT, preferred_element_type=F32)
            return skip + jnp.sum(dy * u_ref[r, :].astype(F32), axis=0, keepdims=True)

        dd_ref[...] = lax.fori_loop(0, nc, inputs, jnp.zeros((1, LANES), F32))
        _scan_states(lr, li, ar_ref[...], -ai_ref[...], T, True)

        def steps(j):
            rows = pl.ds(pl.multiple_of(j * pair_rows, pair_rows), pair_rows)
            tr, ti = sr_ref[rows, :].astype(F32), si_ref[rows, :].astype(F32)
            return tr[:SEGMENTS], tr[SEGMENTS:], ti[:SEGMENTS], ti[SEGMENTS:]

        def pair(j, c):
            acc_r, acc_i, pr, pi = c
            lo_r, hi_r, lo_i, hi_i = steps(j)
            first = pl.ds(pl.multiple_of(j * pair_rows, SEGMENTS), SEGMENTS)
            second = pl.ds(pl.multiple_of(j * pair_rows + SEGMENTS, SEGMENTS), SEGMENTS)
            la_r, la_i, lb_r, lb_i = lr[first, :], li[first, :], lr[second, :], li[second, :]
            return (acc_r + la_r * pr + la_i * pi + lb_r * lo_r + lb_i * lo_i,
                    acc_i - la_r * pi + la_i * pr - lb_r * lo_i + lb_i * lo_r, hi_r, hi_i)

        def pairs(t, c):
            return pair(2 * t + 1, pair(2 * t, c))

        _, end_r, _, end_i = steps(T // 2 - 1)
        zero = jnp.zeros((SEGMENTS, ns), F32)
        acc = lax.fori_loop(0, T // 4, pairs, (zero, zero, _shift_rows(end_r, False), _shift_rows(end_i, False)))
        dar_ref[...] = jnp.sum(acc[0], axis=0, keepdims=True)
        dai_ref[...] = jnp.sum(acc[1], axis=0, keepdims=True)

        dbr_ref[...] = jnp.zeros_like(dbr_ref)
        dbi_ref[...] = jnp.zeros_like(dbi_ref)
        dcr_ref[...] = jnp.zeros_like(dcr_ref)
        dci_ref[...] = jnp.zeros_like(dci_ref)

        def outputs(c, carry):
            r = pl.ds(pl.multiple_of(c * rc, rc), rc)
            ub = u_ref[r, :]
            g = dy_ref[r, :]
            gb = g.astype(BF16)
            lrb = lr[r, :].astype(BF16)
            lib = li[r, :].astype(BF16)
            du_ref[r, :] = (lax.dot_general(lrb, br_ref[...], _NT, preferred_element_type=F32)
                            + lax.dot_general(lib, bi_ref[...], _NT, preferred_element_type=F32)
                            + d_ref[...] * g).astype(BF16)
            dbr_ref[...] += lax.dot_general(ub, lrb, _TN, preferred_element_type=F32)
            dbi_ref[...] += lax.dot_general(ub, lib, _TN, preferred_element_type=F32)
            dcr_ref[...] += lax.dot_general(sr_ref[r, :], gb, _TN, preferred_element_type=F32)
            dci_ref[...] -= lax.dot_general(si_ref[r, :], gb, _TN, preferred_element_type=F32)
            return carry

        lax.fori_loop(0, nc, outputs, 0)

    slab = pl.BlockSpec((S, LANES), lambda g: (0, g))
    states = pl.BlockSpec((S, ns), lambda g: (0, g))
    bspec = pl.BlockSpec((None, LANES, ns), lambda g: (g, 0, 0))
    cspec = pl.BlockSpec((None, ns, LANES), lambda g: (g, 0, 0))
    aspec = pl.BlockSpec((None, 1, ns), lambda g: (g, 0, 0))
    return pl.pallas_call(
        body,
        out_shape=[jax.ShapeDtypeStruct((S, SW), BF16),
                   jax.ShapeDtypeStruct((nslab, LANES, ns), F32), jax.ShapeDtypeStruct((nslab, LANES, ns), F32),
                   jax.ShapeDtypeStruct((nslab, ns, LANES), F32), jax.ShapeDtypeStruct((nslab, ns, LANES), F32),
                   jax.ShapeDtypeStruct((nslab, 1, ns), F32), jax.ShapeDtypeStruct((nslab, 1, ns), F32),
                   jax.ShapeDtypeStruct((1, SW), F32)],
        grid=(nslab,), in_specs=[slab, slab, slab, slab, states, states] + _slab_specs(ns) + [_ANY] * len(deps),
        out_specs=[slab, bspec, bspec, cspec, cspec, aspec, aspec, pl.BlockSpec((1, LANES), lambda g: (0, g))],
        scratch_shapes=[pltpu.VMEM((S, ns), F32)] * 2 + [pltpu.VMEM((S, LANES), F32)],
        compiler_params=_params(("parallel",), VMEM_LIMIT_SCAN), name=name)(
            u, d_direct, d_gate, y, st_r, st_i, bbr, bbi, ar, ai, cbr, cbi, dsk, *deps)


def _discretise(lam_re, lam_im, log_dt, b_re, b_im):
    dt = jnp.exp(log_dt)[:, None]
    mag = jnp.exp(lam_re * dt)
    ar = mag * jnp.cos(lam_im * dt)
    ai = mag * jnp.sin(lam_im * dt)
    nr, ni = ar - 1.0, ai
    den = lam_re * lam_re + lam_im * lam_im
    cr = ((nr * lam_re + ni * lam_im) / den)[..., None]
    ci = ((ni * lam_re - nr * lam_im) / den)[..., None]
    return ar, ai, cr * b_re - ci * b_im, cr * b_im + ci * b_re


def _block_diag(t, nslab):
    G, R, C = t.shape
    eye = jnp.eye(SLAB_GROUPS, dtype=t.dtype)
    t = t.reshape(nslab, SLAB_GROUPS, R, C)
    return jnp.einsum('sgrc,gh->sgrhc', t, eye).reshape(nslab, SLAB_GROUPS * R, SLAB_GROUPS * C)


def _block_diag_part(t, R, C):
    nslab = t.shape[0]
    eye = jnp.eye(SLAB_GROUPS, dtype=t.dtype)
    t = t.reshape(nslab, SLAB_GROUPS, R, SLAB_GROUPS, C)
    return jnp.einsum('sgrhc,gh->sgrc', t, eye).reshape(nslab * SLAB_GROUPS, R, C)


def _place():
    return lax.axis_index("x"), lax.axis_index("y"), lax.axis_index("c")


_HBM = pl.BlockSpec(memory_space=pltpu.HBM)
_SEM = pl.BlockSpec(memory_space=pltpu.SEMAPHORE)
_ORDERED_EFFECT = pltpu.SideEffectType.DATAFLOW_SIDE_EFFECTING


def _split_call(name, srcs, zones, sems_in, n_new, body_fn, after):
    nsrc, nz, ns, nn = len(srcs), len(zones), len(sems_in), len(n_new)
    nb = nsrc + nz

    def body(*refs):
        outs = refs[nb + ns + 1:]
        body_fn(refs[:nb], refs[nb:nb + ns], outs[:nn])
        outs[nn + nz][...] = jnp.zeros((SUBLANES, LANES), F32)

    res = pl.pallas_call(
        body, name=name,
        out_shape=([pltpu.SemaphoreType.DMA((n,)) for n in n_new] + [pltpu.HBM(b.shape, b.dtype) for b in zones]
                   + [jax.ShapeDtypeStruct((SUBLANES, LANES), F32)]),
        in_specs=[_HBM] * nb + [_SEM] * ns + [_ANY],
        out_specs=[_SEM] * nn + [_HBM] * nz + [pl.BlockSpec(memory_space=pltpu.VMEM)],
        input_output_aliases={nsrc + i: nn + i for i in range(nz)},
        compiler_params=pltpu.CompilerParams(has_side_effects=_ORDERED_EFFECT))(
            *[pltpu.with_memory_space_constraint(b, pltpu.HBM) for b in list(srcs) + list(zones)], *sems_in, after)
    return list(res[:nn]), list(res[nn:nn + nz]), res[-1]


def _mesh_peers():
    x, y, c = _place()
    return x, y, c, (x, y, 1 - c), [(1 - x, y), (x, 1 - y), (1 - x, 1 - y)]


def gather_start(shards, after, *, name):
    nw = len(shards)
    x, y, c = _place()
    zones = [lax.dynamic_update_slice(lax.empty((N_DEV,) + s.shape, s.dtype), s[None], (4 * x + 2 * y + c, 0, 0))
             for s in shards]

    def body(bufs, taken, new):
        for cp in _gather_first(bufs, nw, new[0], new[1]):
            cp.start()

    sems, zones, token = _split_call(name, shards, zones, [], [4 * nw, 4 * nw], body, after)
    return shards, sems, zones, token


def _gather_first(bufs, nw, send, recv):
    x, y, c, sibling, chips = _mesh_peers()
    out = []
    for w in range(nw):
        slot = bufs[nw + w].at[4 * x + 2 * y + c]
        for k, to in enumerate([sibling] + [(*ch, c) for ch in chips]):
            out.append(pltpu.make_async_remote_copy(
                src_ref=bufs[w], dst_ref=slot, send_sem=send.at[4 * w + k], recv_sem=recv.at[4 * w + k],
                device_id=to, device_id_type=MESH))
    return out


def _gather_slot_copy(bufs, nw, w, block, send_sem, recv_sem, to):
    px, py, pc = block
    slot = bufs[nw + w].at[4 * px + 2 * py + pc]
    return pltpu.make_async_remote_copy(src_ref=slot, dst_ref=slot, send_sem=send_sem, recv_sem=recv_sem,
                                        device_id=to, device_id_type=MESH)


def gather_forward(state, after, *, name):
    shards, sems, zones, _ = state
    nw = len(shards)

    def body(bufs, taken, new):
        x, y, c, sibling, chips = _mesh_peers()
        for j, ch in enumerate(chips):
            for w in range(nw):
                k = 4 * w + 1 + j
                _gather_slot_copy(bufs, nw, w, (*ch, c), taken[0].at[k], taken[1].at[k], (*ch, c)).wait_recv()
                _gather_slot_copy(bufs, nw, w, (*ch, c), new[0].at[3 * w + j], new[1].at[3 * w + j], sibling).start()
        for w in range(nw):
            _gather_slot_copy(bufs, nw, w, sibling, taken[0].at[4 * w], taken[1].at[4 * w], sibling).wait_recv()
        for cp in _gather_first(bufs, nw, taken[0], taken[1]):
            cp.wait_send()

    sems, zones, token = _split_call(name, shards, zones, sems, [3 * nw, 3 * nw], body, after)
    return shards, sems, zones, token


def gather_finish(state, after, *, name):
    shards, sems, zones, _ = state
    nw = len(shards)

    def body(bufs, taken, new):
        x, y, c, sibling, chips = _mesh_peers()
        for w in range(nw):
            for j, ch in enumerate(chips):
                cp = _gather_slot_copy(bufs, nw, w, (*ch, 1 - c), taken[0].at[3 * w + j], taken[1].at[3 * w + j], sibling)
                cp.wait_send()
                cp.wait_recv()

    _, zones, _ = _split_call(name, shards, zones, sems, [], body, after)
    return zones


def exchange_start(srcs, zone_shapes, copies, n, after, *, name):
    nw = len(srcs)
    zones = [lax.empty(z, s.dtype) for z, s in zip(zone_shapes, srcs)]

    def body(bufs, taken, new):
        for cp in copies(bufs[:nw], bufs[nw:], new[0], new[1]):
            cp.start()

    sems, zones, token = _split_call(name, srcs, zones, [], [n, n], body, after)
    return srcs, copies, sems, zones, token


def exchange_wait(state, after, *, name):
    srcs, copies, sems, zones, _ = state
    nw = len(srcs)

    def body(bufs, taken, new):
        for cp in copies(bufs[:nw], bufs[nw:], taken[0], taken[1]):
            cp.wait_send()
            cp.wait_recv()

    _, zones, _ = _split_call(name, srcs, zones, sems, [], body, after)
    return zones


def _core_copies(srcs, zones, send, recv):
    x, y, c = _place()
    return [pltpu.make_async_remote_copy(
        src_ref=srcs[w].at[:, 1 - c], dst_ref=zones[w], send_sem=send.at[w], recv_sem=recv.at[w],
        device_id=(x, y, 1 - c), device_id_type=MESH) for w in range(len(srcs))]


def _chip_copies(srcs, zones, send, recv):
    x, y, c = _place()
    chips = [(1 - x, y), (x, 1 - y), (1 - x, 1 - y)]
    return [pltpu.make_async_remote_copy(
        src_ref=srcs[w].at[2 * cx + cy], dst_ref=zones[w].at[j], send_sem=send.at[3 * w + j],
        recv_sem=recv.at[3 * w + j], device_id=(cx, cy, c), device_id_type=MESH)
        for w in range(len(srcs)) for j, (cx, cy) in enumerate(chips)]


def _blocked(fn, ins, outs, *, name, place=None, tr=256):
    k, n = outs[0][0]
    tr = _tile(k, tr, 16)
    if place is None:
        place = jnp.zeros((1,), jnp.int32)
    args = [a[0] if isinstance(a, tuple) else a for a in ins]
    leads = [a[1] if isinstance(a, tuple) else None for a in ins]
    nin, nout, n_steps = len(args), len(outs), k // tr

    def body(place_ref, *refs):
        srcs, o = refs[:nin], refs[nin:nin + nout]
        bufs, sems = refs[nin + nout:2 * nin + nout], refs[2 * nin + nout:]
        i = pl.program_id(0)

        def fetch(step, slot):
            copies = []
            for x, lead in enumerate(leads):
                if lead is None:
                    src = srcs[x].at[pl.ds(pl.multiple_of(step * tr, tr), tr), :]
                else:
                    which, block = lead(step, place_ref)
                    src = srcs[x].at[which, pl.ds(pl.multiple_of(block * tr, tr), tr), :]
                copies.append(pltpu.make_async_copy(src, bufs[x].at[slot], sems[x].at[slot]))
            return copies

        @pl.when(i == 0)
        def _():
            for step in range(min(ROW_BUFFERS - 1, n_steps)):
                for cp in fetch(step, step):
                    cp.start()

        ahead = i + (ROW_BUFFERS - 1)

        @pl.when(ahead < n_steps)
        def _():
            for cp in fetch(ahead, ahead % ROW_BUFFERS):
                cp.start()

        slot = i % ROW_BUFFERS
        for cp in fetch(i, slot):
            cp.wait()
        res = fn(*[b[slot] for b in bufs])
        for ref, val in zip(o, res):
            ref[...] = val.astype(ref.dtype)

    return pl.pallas_call(
        body, out_shape=[jax.ShapeDtypeStruct(s, d) for s, d in outs],
        grid_spec=pltpu.PrefetchScalarGridSpec(
            num_scalar_prefetch=1, grid=(n_steps,), in_specs=[_ANY] * nin,
            out_specs=[pl.BlockSpec((tr, n), lambda i, s: (i, 0)) for _ in outs],
            scratch_shapes=([pltpu.VMEM((ROW_BUFFERS, tr, n), a.dtype) for a in args]
                            + [pltpu.SemaphoreType.DMA((ROW_BUFFERS,)) for _ in args])),
        compiler_params=_params(("arbitrary",)), name=name)(place, *args)


def _adamw(w, g, m, v):
    m = ADAM_B1 * m + (1.0 - ADAM_B1) * g
    v = ADAM_B2 * v + (1.0 - ADAM_B2) * (g * g)
    m_hat = m / (1.0 - ADAM_B1 ** ADAM_STEP)
    v_hat = v / (1.0 - ADAM_B2 ** ADAM_STEP)
    delta = -ADAM_LR * (m_hat * pl.reciprocal(jnp.sqrt(v_hat) + ADAM_EPS, approx=True) + ADAM_WD * w)
    return delta, m, v


def kernel(x, p, mix_norm_pre, w_in, lam_re, lam_im, log_dt, ssm_b_re, ssm_b_im, ssm_c_re, ssm_c_im, ssm_d, w_glu, b_glu, attn_out_norm, ssm_out_norm, w_out, mix_norm_post, mlp_norm_pre, w_up, w_down, mlp_norm_post, ple_norm_pre, w_ple_gate, w_ple_proj, ple_norm_post, loss_target, m_mix_norm_pre, m_w_in, m_lam_re, m_lam_im, m_log_dt, m_ssm_b_re, m_ssm_b_im, m_ssm_c_re, m_ssm_c_im, m_ssm_d, m_w_glu, m_b_glu, m_attn_out_norm, m_ssm_out_norm, m_w_out, m_mix_norm_post, m_mlp_norm_pre, m_w_up, m_w_down, m_mlp_norm_post, m_ple_norm_pre, m_w_ple_gate, m_w_ple_proj, m_ple_norm_post, v_mix_norm_pre, v_w_in, v_lam_re, v_lam_im, v_log_dt, v_ssm_b_re, v_ssm_b_im, v_ssm_c_re, v_ssm_c_im, v_ssm_d, v_w_glu, v_b_glu, v_attn_out_norm, v_ssm_out_norm, v_w_out, v_mix_norm_post, v_mlp_norm_pre, v_w_up, v_w_down, v_mlp_norm_post, v_ple_norm_pre, v_w_ple_gate, v_w_ple_proj, v_ple_norm_post):
    weights = dict(mix_norm_pre=mix_norm_pre, w_in=w_in, lam_re=lam_re, lam_im=lam_im, log_dt=log_dt, ssm_b_re=ssm_b_re, ssm_b_im=ssm_b_im, ssm_c_re=ssm_c_re, ssm_c_im=ssm_c_im, ssm_d=ssm_d, w_glu=w_glu, b_glu=b_glu, attn_out_norm=attn_out_norm, ssm_out_norm=ssm_out_norm, w_out=w_out, mix_norm_post=mix_norm_post, mlp_norm_pre=mlp_norm_pre, w_up=w_up, w_down=w_down, mlp_norm_post=mlp_norm_post, ple_norm_pre=ple_norm_pre, w_ple_gate=w_ple_gate, w_ple_proj=w_ple_proj, ple_norm_post=ple_norm_post)
    mom_m = dict(mix_norm_pre=m_mix_norm_pre, w_in=m_w_in, lam_re=m_lam_re, lam_im=m_lam_im, log_dt=m_log_dt, ssm_b_re=m_ssm_b_re, ssm_b_im=m_ssm_b_im, ssm_c_re=m_ssm_c_re, ssm_c_im=m_ssm_c_im, ssm_d=m_ssm_d, w_glu=m_w_glu, b_glu=m_b_glu, attn_out_norm=m_attn_out_norm, ssm_out_norm=m_ssm_out_norm, w_out=m_w_out, mix_norm_post=m_mix_norm_post, mlp_norm_pre=m_mlp_norm_pre, w_up=m_w_up, w_down=m_w_down, mlp_norm_post=m_mlp_norm_post, ple_norm_pre=m_ple_norm_pre, w_ple_gate=m_w_ple_gate, w_ple_proj=m_w_ple_proj, ple_norm_post=m_ple_norm_post)
    mom_v = dict(mix_norm_pre=v_mix_norm_pre, w_in=v_w_in, lam_re=v_lam_re, lam_im=v_lam_im, log_dt=v_log_dt, ssm_b_re=v_ssm_b_re, ssm_b_im=v_ssm_b_im, ssm_c_re=v_ssm_c_re, ssm_c_im=v_ssm_c_im, ssm_d=v_ssm_d, w_glu=v_w_glu, b_glu=v_b_glu, attn_out_norm=v_attn_out_norm, ssm_out_norm=v_ssm_out_norm, w_out=v_w_out, mix_norm_post=v_mix_norm_post, mlp_norm_pre=v_mlp_norm_pre, w_up=v_w_up, w_down=v_w_down, mlp_norm_post=v_mlp_norm_post, ple_norm_pre=v_ple_norm_pre, w_ple_gate=v_w_ple_gate, w_ple_proj=v_w_ple_proj, ple_norm_post=v_ple_norm_post)
    order = list(weights)
    big = ["w_in", "w_glu", "w_out", "w_up", "w_down", "w_ple_gate", "w_ple_proj"]
    col_sharded = {"w_in", "w_up", "w_ple_proj"}
    small = [n for n in order if n not in big]

    _, S, D = x.shape
    xs = x[0]
    tgt = loss_target[0]
    AW = attn_out_norm.shape[1]
    SW = ssm_d.shape[1]
    H = AW // HEAD_DIM
    G = SW // SSM_GROUP
    nslab = G // SLAB_GROUPS
    P_, C_ = SSM_STATE, SSM_GROUP

    shard = {n: weights[n][0].astype(BF16) for n in big}
    W, WT = {}, {}

    def arrived(names, gathered):
        for n, g in zip(names, gathered):
            W[n] = g if n in col_sharded else g.reshape(1, N_DEV * g.shape[1], g.shape[2])

    def transposed(g):
        return jnp.swapaxes(g, 1, 2).reshape(1, g.shape[0] * g.shape[2], g.shape[1])

    g1, g2, g3, g4, g5, g6 = (weights[n] for n in ("mix_norm_pre", "mix_norm_post", "mlp_norm_pre",
                                                      "mlp_norm_post", "ple_norm_pre", "ple_norm_post"))
    ga, gs = attn_out_norm, ssm_out_norm
    gather_in = gather_start([shard["w_in"]], shard["w_in"], name="gather_w_in_start")
    (hn1,) = rowwise(lambda a, g: (_rms(a, g),), [xs], [g1], [(D, BF16)], deps=(gather_in[-1],), name="norm_in")
    gather_in = gather_forward(gather_in, hn1, name="gather_w_in_forward")
    arrived(["w_in"], gather_finish(gather_in, gather_in[-1], name="gather_w_in_finish"))
    WT["w_in"] = transposed(W["w_in"])
    early, mid, late = ["w_glu", "w_out"], ["w_up"], ["w_down", "w_ple_gate", "w_ple_proj"]
    gather_early = gather_start([shard[n] for n in early], W["w_in"], name="gather_early_start")
    gather_mid = gather_start([shard[n] for n in mid], gather_early[-1], name="gather_mid_start")
    gather_late = gather_start([shard[n] for n in late], gather_mid[-1], name="gather_late_start")

    (proj,) = mm_nn(hn1, W["w_in"], [F32], deps=(gather_late[-1],), name="proj_in")
    attn, lse = attn_fwd(proj, H, name="attn_fwd")
    gather_early = gather_forward(gather_early, attn, name="gather_early_forward")
    (mix_a,) = rowwise(lambda a, g: (_rms(a, g),), [attn], [ga], [(AW, BF16)], deps=(gather_early[-1],),
                       name="attn_norm")
    arrived(early, gather_finish(gather_early, mix_a, name="gather_early_finish"))

    a_r, a_i, bb_r, bb_i = _discretise(lam_re[0], lam_im[0], log_dt[0], ssm_b_re[0], ssm_b_im[0])
    ssm_consts = (_block_diag(bb_r.swapaxes(1, 2), nslab).astype(BF16), _block_diag(bb_i.swapaxes(1, 2), nslab).astype(BF16),
                  a_r.reshape(nslab, 1, SLAB_STATES), a_i.reshape(nslab, 1, SLAB_STATES),
                  _block_diag(ssm_c_re[0].swapaxes(1, 2), nslab).astype(BF16),
                  _block_diag(ssm_c_im[0].swapaxes(1, 2), nslab).astype(BF16), ssm_d)
    u_seg = _to_segments(proj[:, 3 * AW:]).astype(BF16)
    y_pre, yg, st_r, st_i = ssm_fwd(u_seg, *ssm_consts, name="ssm_fwd")
    gather_mid = gather_forward(gather_mid, y_pre, name="gather_mid_forward")
    (gl1,) = mm_nn(yg, W["w_glu"], [BF16], epi=lambda acc, b: (acc + b,), bias=b_glu, deps=(gather_mid[-1],),
                   name="glu_gate")
    (mix_s,) = rowwise(lambda yp, gl, g: (_rms(_gelu(yp) * _sigmoid(gl), g),), [y_pre, gl1], [gs], [(SW, BF16)],
                       name="ssm_glu_norm")
    mixed = [mix_a, _from_segments(mix_s)]
    (mo,) = mm_nn(mixed, W["w_out"], [BF16], name="mix_out")

    def resid_norm(h, t, gpost, gpre):
        hh = h + _rms(t, gpost)
        return hh, _rms(hh, gpre)

    h1, hn2 = rowwise(resid_norm, [xs, mo], [g2, g3], [(D, F32), (D, BF16)], name="resid_mix")
    arrived(mid, gather_finish(gather_mid, hn2, name="gather_mid_finish"))
    gather_late = gather_forward(gather_late, W["w_up"], name="gather_late_forward")
    WT["w_up"] = transposed(W["w_up"])

    def relu2(acc):
        r = jnp.maximum(acc, 0.0)
        return acc, r * r

    up, act = mm_nn(hn2, W["w_up"], [BF16, BF16], epi=relu2, deps=(gather_late[-1],), tm=1024, tn=1024, name="mlp_up")
    arrived(late, gather_finish(gather_late, act, name="gather_late_finish"))
    (ff,) = mm_nn(act, W["w_down"], [BF16], name="mlp_down")
    h2, hn3 = rowwise(resid_norm, [h1, ff], [g4, g5], [(D, F32), (D, BF16)], name="resid_mlp")
    (gl2,) = mm_nn(hn3, W["w_ple_gate"], [BF16], name="ple_gate")
    pb = p[0, 0].astype(BF16)
    (emb,) = mm_nn(pb, W["w_ple_proj"], [BF16], name="ple_proj")

    def head(h, gl, e, t, g):
        sg = _sigmoid(gl)
        ge = sg * e
        err = h + _rms(ge, g) - t
        dh = err * (1.0 / D)
        dge, dg = _rms_bwd(dh, ge, g)
        return dh, dge * e * sg * (1.0 - sg), dge * sg, jnp.sum(err * err, axis=0, keepdims=True), dg

    dh3, dgl2, demb, loss_part, dg6 = rowwise(head, [h2, gl2, emb, tgt], [g6], [(D, F32), (D, BF16), (D, BF16)],
                                             [D, D], name="ple_loss_head")
    loss = lax.psum(0.5 / D * jnp.sum(loss_part), ("x", "y", "c"))

    x_i, y_i, c_i = _place()
    place = jnp.stack([c_i, 2 * x_i + y_i]).astype(jnp.int32)
    grads, out_g, out_d, out_m, out_v = {}, {}, {}, {}, {}

    def to_sibling(names, after, tag):
        chunks = []
        for n in names:
            g = grads[n]
            g = g if n in col_sharded else g.reshape(N_DEV, g.shape[1] // N_DEV, g.shape[2])
            chunks.append(g.reshape(4, 2, g.shape[1], g.shape[2]))
        return chunks, exchange_start(chunks, [(4,) + g.shape[2:] for g in chunks], _core_copies, len(chunks), after,
                                      name=f"grads_to_sibling_{tag}")

    def to_chips(names, sent, after, tag):
        chunks, state = sent
        sums = []
        for n, g, r in zip(names, chunks, exchange_wait(state, after, name=f"grads_from_sibling_{tag}")):
            k, nn = g.shape[2], g.shape[3]
            kb = k // _tile(k, 512, 16)

            def mine(i, s, kb=kb):
                return 2 * (i // kb) + s[0], i % kb

            (s,) = _blocked(lambda a, b: (a.astype(F32) + b.astype(F32),),
                            [(g.reshape(N_DEV, k, nn), mine), r.reshape(4 * k, nn)],
                            [((4 * k, nn), BF16)], place=place, tr=k // kb, name=f"chip_sum_{n}")
            sums.append(s.reshape(4, k, nn))
        return sums, exchange_start(sums, [(3,) + s.shape[1:] for s in sums], _chip_copies, 3 * len(sums), sums[-1],
                                    name=f"grads_to_chips_{tag}")

    def update(w_, m_, v_, own, r0, r1, r2):
        g = own.astype(F32) + r0.astype(F32) + r1.astype(F32) + r2.astype(F32)
        return (g,) + _adamw(w_, g, m_, v_)

    def finish(names, sent, after, tag):
        sums, state = sent
        for n, s, r in zip(names, sums, exchange_wait(state, after, name=f"grads_from_chips_{tag}")):
            shp = weights[n].shape
            res = _blocked(update, [weights[n][0], mom_m[n][0], mom_v[n][0], (s, lambda i, p_: (p_[1], i)),
                                    (r, lambda i, p_: (0, i)), (r, lambda i, p_: (1, i)), (r, lambda i, p_: (2, i))],
                           [(shp[1:], F32)] * 4, place=place, tr=max(16, min(shp[1] // 8, 262144 // shp[2])),
                           name=f"adamw_{n}")
            out_g[n], out_d[n], out_m[n], out_v[n] = (t.reshape(shp) for t in res)
        return out_v[names[-1]]

    grads["w_ple_proj"] = mm_tn(pb, demb, N_DEV, name="grad_w_ple_proj")
    dhn3 = mm_nt(dgl2, W["w_ple_gate"], BF16, name="back_ple_gate")
    grads["w_ple_gate"] = mm_tn(hn3, dgl2, 1, name="grad_w_ple_gate")

    def back_resid(dh, dhn, h, t, gpre, gpost):
        d1, dgpre = _rms_bwd(dhn, h, gpre)
        dhh = dh + d1
        dt, dgpost = _rms_bwd(dhh, t, gpost)
        return dhh, dt, dgpre, dgpost

    dh2, dff, dg5, dg4 = rowwise(back_resid, [dh3, dhn3, h2, ff], [g5, g4], [(D, F32), (D, BF16)], [D, D],
                                 name="back_resid_mlp")
    dup = mm_nt(dff, W["w_down"], BF16, epi=lambda acc, u_: (acc * 2.0 * jnp.maximum(u_.astype(F32), 0.0),),
                extra=up, name="back_mlp_down")
    grads["w_down"] = mm_tn(act, dff, 1, name="grad_w_down")
    group_a = ["w_ple_proj", "w_ple_gate", "w_down"]
    sent_a = to_sibling(group_a, grads["w_down"], "a")
    (dhn2,) = mm_nn(dup, WT["w_up"], [BF16], deps=(sent_a[1][-1],), name="back_mlp_up")
    sent_a = to_chips(group_a, sent_a, dhn2, "a")
    grads["w_up"] = mm_tn(hn2, dup, N_DEV, deps=(sent_a[1][-1],), name="grad_w_up")
    dh1, dmo, dg3, dg2 = rowwise(back_resid, [dh2, dhn2, h1, mo], [g3, g2], [(D, F32), (D, BF16)], [D, D],
                                 name="back_resid_mix")
    dmixed = mm_nt(dmo, W["w_out"], BF16, name="back_mix_out")
    grads["w_out"] = mm_tn(mixed, dmo, 1, name="grad_w_out")

    def back_glu(dm, yp, gl, g):
        ygf = _gelu(yp)
        sg = _sigmoid(gl)
        dssm, dg = _rms_bwd(dm, ygf * sg, g)
        dgl = dssm * ygf * sg * (1.0 - sg)
        return dgl, dssm * sg, dg, jnp.sum(dgl, axis=0, keepdims=True)

    dgl1, dyg_direct, dgs, db_glu = rowwise(back_glu, [_to_segments(dmixed[:, AW:]), y_pre, gl1], [gs],
                                            [(SW, BF16), (SW, F32)], [SW, SW], name="back_glu")
    dyg_gate = mm_nt(dgl1, W["w_glu"], BF16, name="back_glu_gate")
    grads["w_glu"] = mm_tn(yg, dgl1, 1, name="grad_w_glu")
    group_b = ["w_up", "w_out", "w_glu"]
    sent_b = to_sibling(group_b, grads["w_glu"], "b")
    done_a = finish(group_a, sent_a, sent_b[1][-1], "a")

    du_seg, dbb_r, dbb_i, dcb_r, dcb_i, da_r, da_i, d_skip = ssm_bwd(
        u_seg, dyg_direct, dyg_gate, y_pre, st_r, st_i, *ssm_consts, deps=(done_a,), name="ssm_bwd")
    sent_b = to_chips(group_b, sent_b, du_seg, "b")

    def back_attn_norm(dm, a, g):
        da, dg = _rms_bwd(dm, a, g)
        prod = da * a
        delta = jnp.concatenate(
            [jnp.broadcast_to(jnp.sum(prod[:, h * HEAD_DIM:(h + 1) * HEAD_DIM], axis=-1, keepdims=True),
                              (prod.shape[0], HEAD_DIM)) for h in range(H)], axis=1)
        return da, delta, dg

    dattn, delta, dga = rowwise(back_attn_norm, [(dmixed, AW, 0), attn], [ga], [(AW, F32), (AW, F32)], [AW],
                                deps=(sent_b[1][-1],), name="back_attn_norm")
    dq, dk, dv = attn_bwd(proj, dattn, lse, delta, H, name="attn_bwd")
    dproj = [dq, dk, dv, _from_segments(du_seg)]
    (dhn1,) = mm_nn(dproj, WT["w_in"], [BF16], name="back_proj_in")

    def back_in(dh, dhn, a, g):
        d1, dg = _rms_bwd(dhn, a, g)
        return dh + d1, dg

    grad_x, dg1 = rowwise(back_in, [dh1, dhn1, xs], [g1], [(D, F32)], [D], name="back_norm_in")

    cot = dict(
        mix_norm_pre=dg1, mix_norm_post=dg2, mlp_norm_pre=dg3, mlp_norm_post=dg4, ple_norm_pre=dg5, ple_norm_post=dg6,
        attn_out_norm=dga, ssm_out_norm=dgs, b_glu=db_glu, ssm_d=d_skip,
        ssm_c_re=_block_diag_part(dcb_r, P_, C_).swapaxes(1, 2), ssm_c_im=_block_diag_part(dcb_i, P_, C_).swapaxes(1, 2),
        a_r=da_r.reshape(G, P_), a_i=da_i.reshape(G, P_),
        bb_r=_block_diag_part(dbb_r, C_, P_).swapaxes(1, 2), bb_i=_block_diag_part(dbb_i, C_, P_).swapaxes(1, 2))
    names = list(cot)
    flat = jnp.concatenate([cot[n].reshape(-1) for n in names])
    total = flat.shape[0]
    rows_ = -(-total // (LANES * 16)) * 16
    flat = jnp.pad(flat, (0, rows_ * LANES - total)).reshape(rows_, LANES)
    gather_small = gather_start([flat], flat, name="gather_small_start")
    grads["w_in"] = mm_tn(hn1, dproj, N_DEV, deps=(gather_small[-1],), tko=2048, name="grad_w_in")
    group_c = ["w_in"]
    sent_c = to_sibling(group_c, grads["w_in"], "c")
    done_b = finish(group_b, sent_b, sent_c[1][-1], "b")
    sent_c = to_chips(group_c, sent_c, done_b, "c")
    gather_small = gather_forward(gather_small, sent_c[1][-1], name="gather_small_forward")
    (every,) = gather_finish(gather_small, gather_small[-1], name="gather_small_finish")
    (summed,) = _blocked(lambda *t: (functools.reduce(lambda a, b: a + b, t),),
                         [(every, functools.partial(lambda i, p_, j: (j, i), j=j)) for j in range(N_DEV)],
                         [((rows_, LANES), F32)], name="sum_small_grads")
    summed = summed.reshape(-1)
    red, off = {}, 0
    for n in names:
        sz = cot[n].size
        red[n] = summed[off:off + sz].reshape(cot[n].shape)
        off += sz
    _, pull = jax.vjp(_discretise, lam_re[0], lam_im[0], log_dt[0], ssm_b_re[0], ssm_b_im[0])
    d_lre, d_lim, d_ldt, d_bre, d_bim = pull((red["a_r"], red["a_i"], red["bb_r"], red["bb_i"]))
    red.update(lam_re=d_lre, lam_im=d_lim, log_dt=d_ldt, ssm_b_re=d_bre, ssm_b_im=d_bim)

    def pack(d):
        t = jnp.concatenate([d[n].reshape(-1) for n in small])
        r_ = -(-t.shape[0] // (LANES * 16)) * 16
        return jnp.pad(t, (0, r_ * LANES - t.shape[0])).reshape(r_, LANES)

    sw, sg_, sm, sv = pack(weights), pack(red), pack(mom_m), pack(mom_v)
    sd, snm, snv = _blocked(lambda w_, g_, m_, v_: _adamw(w_, g_, m_, v_), [sw, sg_, sm, sv],
                            [(sw.shape, F32)] * 3, name="adamw_small")
    finish(group_c, sent_c, snv, "c")
    off = 0
    for n in small:
        sz = weights[n].size
        shp = weights[n].shape
        out_g[n] = red[n].reshape(shp)
        out_d[n] = sd.reshape(-1)[off:off + sz].reshape(shp)
        out_m[n] = snm.reshape(-1)[off:off + sz].reshape(shp)
        out_v[n] = snv.reshape(-1)[off:off + sz].reshape(shp)
        off += sz

    return (loss, grad_x[None], *[out_g[n] for n in order], *[out_d[n] for n in order],
            *[out_m[n] for n in order], *[out_v[n] for n in order])
```

```python
import functools
import math

import jax
import jax.numpy as jnp
from jax import lax
from jax.experimental import pallas as pl
from jax.experimental.pallas import tpu as pltpu

F32 = jnp.float32
BF16 = jnp.bfloat16
MESH = pl.DeviceIdType.MESH

N_DEV = 8
LANES = 128
SUBLANES = 8
VMEM_LIMIT = 48 * 1024 * 1024
VMEM_LIMIT_SCAN = 60 * 1024 * 1024

HEAD_DIM = 128
BLK = 128
DILATIONS = (1, 4, 16)
SSM_GROUP = 16
SSM_STATE = 64
SLAB_GROUPS = LANES // SSM_GROUP
SLAB_STATES = SLAB_GROUPS * SSM_STATE
SEGMENTS = SUBLANES
SCAN_UNROLL = 4
RMS_EPS = 1e-6
NEG_INF = -1e30

ADAM_LR = 0.001
ADAM_B1 = 0.9
ADAM_B2 = 0.999
ADAM_EPS = 1e-08
ADAM_WD = 0.01
ADAM_STEP = 10


def _tile(n, pref, unit=LANES):
    if n <= pref:
        return n
    t = (pref // unit) * unit
    while t > unit and n % t:
        t -= unit
    assert n % t == 0, (n, pref, unit)
    return t


def _params(sem=None, vmem=VMEM_LIMIT):
    return pltpu.CompilerParams(dimension_semantics=sem, vmem_limit_bytes=vmem)


_NN = (((1,), (0,)), ((), ()))
_NT = (((1,), (1,)), ((), ()))
_TN = (((0,), (0,)), ((), ()))


_ANY = pl.BlockSpec(memory_space=pl.ANY)


def _mm_call(dims, nk, na, nb, pick, n_extra, n_dep, n_out, epi, group=1, **kw):
    first_extra = na + nb
    first_out = first_extra + n_extra + n_dep
    kw["in_specs"] = list(kw["in_specs"]) + [_ANY] * n_dep

    def grouped(refs, step):
        if group == 1:
            return lax.dot_general(refs[0][...], refs[1][...], dims, preferred_element_type=F32)
        kp = refs[0].shape[1]
        return sum(lax.dot_general(refs[step * group + p][...], refs[na][pl.ds(p * kp, kp), :], dims,
                                   preferred_element_type=F32) for p in range(group))

    def single(*refs):
        extra = refs[first_extra:first_extra + n_extra]
        res = epi(grouped(refs, 0), *[e[...] for e in extra])
        for o, r in zip(refs[first_out:first_out + n_out], res):
            o[...] = r.astype(o.dtype)

    if nk == 1:
        assert na == group and nb == 1
        kw["scratch_shapes"] = []
        return pl.pallas_call(single, **kw)

    def body(*refs):
        extra = refs[first_extra:first_extra + n_extra]
        outs = refs[first_out:first_out + n_out]
        acc = refs[-1]
        k = pl.program_id(2)

        @pl.when(k == 0)
        def _():
            acc[...] = jnp.zeros_like(acc)

        def add(a_ref, b_ref):
            acc[...] += lax.dot_general(a_ref[...], b_ref[...], dims, preferred_element_type=F32)

        if na == nb == 1:
            add(refs[0], refs[1])
        elif group > 1:
            for step in range(nk):
                @pl.when(k == step)
                def _(step=step):
                    acc[...] += grouped(refs, step)
        else:
            pa, pb = pick(pl.program_id(0), pl.program_id(1), k)
            for x in range(na):
                for y in range(nb):
                    pl.when((pa == x) & (pb == y))(functools.partial(add, refs[x], refs[na + y]))

        @pl.when(k == nk - 1)
        def _():
            res = epi(acc[...], *[e[...] for e in extra])
            for o, r in zip(outs, res):
                o[...] = r.astype(o.dtype)

    return pl.pallas_call(body, **kw)


def _identity_epi(acc):
    return (acc,)


def _parts(t):
    return list(t) if isinstance(t, (list, tuple)) else [t]


def _part_spec(block, part, which, index):
    def index_map(i, j, k):
        use = which(i, j, k) == part
        r, c = index(i, j, k)
        return jnp.where(use, r, 0), jnp.where(use, c, 0)
    return pl.BlockSpec(block, index_map)


def mm_nn(a, w, out_dtypes, *, name, epi=_identity_epi, bias=None, deps=(), tm=2048, tn=512, tk=2048):
    a = _parts(a)
    M, Kp = a[0].shape
    K = Kp * len(a)
    J, K2, n = w.shape
    assert K == K2
    tm, tn, tk = _tile(M, tm, 16), _tile(n, tn), _tile(K, tk)
    npj = n // tn
    nk = K // tk
    group = 1
    if len(a) == 1:
        in_specs = [pl.BlockSpec((tm, tk), lambda i, j, k: (i, k))]
    else:
        assert tk % Kp == 0
        group = tk // Kp
        in_specs = [pl.BlockSpec((tm, Kp), lambda i, j, k: (i, 0)) for _ in a]
    in_specs.append(pl.BlockSpec((None, tk, tn), lambda i, j, k: (j // npj, k, j % npj)))
    args = a + [w]
    if bias is not None:
        in_specs.append(pl.BlockSpec((1, tn), lambda i, j, k: (0, j)))
        args.append(bias)
    return _mm_call(
        _NN, nk, len(a), 1, lambda i, j, k: (k, 0), len(args) - len(a) - 1, len(deps), len(out_dtypes), epi, group,
        out_shape=[jax.ShapeDtypeStruct((M, J * n), d) for d in out_dtypes],
        grid=(M // tm, J * npj, nk), in_specs=in_specs,
        out_specs=[pl.BlockSpec((tm, tn), lambda i, j, k: (i, j)) for _ in out_dtypes],
        scratch_shapes=[pltpu.VMEM((tm, tn), F32)],
        compiler_params=_params(("parallel", "parallel", "arbitrary")), name=name)(*args, *deps)


def mm_nt(a, w, out_dtype, *, name, epi=_identity_epi, extra=None, tm=2048, tko=512, tnr=2048):
    M, N = a.shape
    J, K, n = w.shape
    assert N == J * n
    tm, tko, tnr = _tile(M, tm, 16), _tile(K, tko), _tile(n, tnr)
    npj = n // tnr
    nk = N // tnr
    in_specs = [pl.BlockSpec((tm, tnr), lambda i, j, k: (i, k)),
                pl.BlockSpec((None, tko, tnr), lambda i, j, k: (k // npj, j, k % npj))]
    args = [a, w]
    if extra is not None:
        in_specs.append(pl.BlockSpec((tm, tko), lambda i, j, k: (i, j)))
        args.append(extra)
    return _mm_call(
        _NT, nk, 1, 1, None, len(args) - 2, 0, 1, epi,
        out_shape=[jax.ShapeDtypeStruct((M, K), out_dtype)],
        grid=(M // tm, K // tko, nk), in_specs=in_specs,
        out_specs=[pl.BlockSpec((tm, tko), lambda i, j, k: (i, j))],
        scratch_shapes=[pltpu.VMEM((tm, tko), F32)],
        compiler_params=_params(("parallel", "parallel", "arbitrary")), name=name)(*args)[0]


def mm_tn(a, b, J, *, name, deps=(), tko=1024, tn=1024, ts=2048):
    a, b = _parts(a), _parts(b)
    S, Kp = a[0].shape
    S2, Np = b[0].shape
    K, N = Kp * len(a), Np * len(b)
    assert S == S2 and N % J == 0
    n = N // J
    tko, tn, ts = _tile(Kp, tko), _tile(math.gcd(n, Np), tn), _tile(S, ts)
    npj = n // tn
    nk = S // ts
    ta, tb = Kp // tko, Np // tn
    assert nk > 1 or len(a) == len(b) == 1
    return _mm_call(
        _TN, nk, len(a), len(b), lambda i, j, k: (i // ta, j // tb), 0, len(deps), 1, _identity_epi,
        out_shape=[jax.ShapeDtypeStruct((J, K, n), BF16)],
        grid=(K // tko, J * npj, nk),
        in_specs=([_part_spec((ts, tko), x, lambda i, j, k: i // ta, lambda i, j, k: (k, i % ta)) for x in range(len(a))]
                  + [_part_spec((ts, tn), y, lambda i, j, k: j // tb, lambda i, j, k: (k, j % tb)) for y in range(len(b))]),
        out_specs=[pl.BlockSpec((None, tko, tn), lambda i, j, k: (j // npj, i, j % npj))],
        scratch_shapes=[pltpu.VMEM((tko, tn), F32)],
        compiler_params=_params(("parallel", "parallel", "arbitrary")), name=name)(*a, *b, *deps)[0]


ROW_BUFFERS = 4


def rowwise(fn, rows, vecs, outs, accs=(), *, name, deps=(), ts=256):
    rows = [r if isinstance(r, tuple) else (r, r.shape[1], 0) for r in rows]
    S = rows[0][0].shape[0]
    ts = _tile(S, ts, 16)
    n_steps = S // ts
    nr, nv, no, nd, na = len(rows), len(vecs), len(outs), len(deps), len(accs)

    def body(*refs):
        r, v = refs[:nr], refs[nr:nr + nv]
        o, a = refs[nr + nv + nd:nr + nv + nd + no], refs[nr + nv + nd + no:nr + nv + nd + no + na]
        bufs, sems = refs[-2 * nr:-nr], refs[-nr:]
        i = pl.program_id(0)

        def fetch(step, slot):
            src = pl.ds(pl.multiple_of(step * ts, ts), ts)
            return [pltpu.make_async_copy(r[x].at[src, pl.ds(cb * w, w)], bufs[x].at[slot], sems[x].at[slot])
                    for x, (_, w, cb) in enumerate(rows)]

        @pl.when(i == 0)
        def _():
            for step in range(min(ROW_BUFFERS - 1, n_steps)):
                for cp in fetch(step, step):
                    cp.start()

        ahead = i + (ROW_BUFFERS - 1)

        @pl.when(ahead < n_steps)
        def _():
            for cp in fetch(ahead, ahead % ROW_BUFFERS):
                cp.start()

        slot = i % ROW_BUFFERS
        for cp in fetch(i, slot):
            cp.wait()
        res = fn(*[b[slot].astype(F32) for b in bufs], *[t[...] for t in v])
        for ref, val in zip(o, res[:no]):
            ref[...] = val.astype(ref.dtype)
        if a:
            @pl.when(pl.program_id(0) == 0)
            def _():
                for ref in a:
                    ref[...] = jnp.zeros_like(ref)

            for ref, val in zip(a, res[no:]):
                ref[...] += val

    in_specs = [_ANY] * nr + [pl.BlockSpec(v.shape, lambda i: (0, 0)) for v in vecs] + [_ANY] * nd
    out_shape = [jax.ShapeDtypeStruct((S, w), d) for w, d in outs]
    out_shape += [jax.ShapeDtypeStruct((1, w), F32) for w in accs]
    out_specs = [pl.BlockSpec((ts, w), lambda i: (i, 0)) for w, _ in outs]
    out_specs += [pl.BlockSpec((1, w), lambda i: (0, 0)) for w in accs]
    scratch = ([pltpu.VMEM((ROW_BUFFERS, ts, w), arr.dtype) for arr, w, _ in rows]
               + [pltpu.SemaphoreType.DMA((ROW_BUFFERS,)) for _ in rows])
    return pl.pallas_call(body, out_shape=out_shape, grid=(n_steps,), in_specs=in_specs, out_specs=out_specs,
                          scratch_shapes=scratch, compiler_params=_params(("arbitrary",)),
                          name=name)(*[r[0] for r in rows], *vecs, *deps)


def _rms(x, g):
    r = lax.rsqrt(jnp.mean(x * x, axis=-1, keepdims=True) + RMS_EPS)
    return x * r * g


def _rms_bwd(dy, x, g):
    r = lax.rsqrt(jnp.mean(x * x, axis=-1, keepdims=True) + RMS_EPS)
    xh = x * r
    dxh = dy * g
    dx = r * (dxh - xh * jnp.mean(dxh * xh, axis=-1, keepdims=True))
    return dx, jnp.sum(dy * xh, axis=0, keepdims=True)


def _sigmoid(x):
    return pl.reciprocal(1.0 + jnp.exp(-x), approx=True)


_GELU_C = math.sqrt(2.0 / math.pi)


def _gelu(x):
    return 0.5 * x * (1.0 + jnp.tanh(_GELU_C * (x + 0.044715 * x * x * x)))


def _gelu_grad(x):
    t = jnp.tanh(_GELU_C * (x + 0.044715 * x * x * x))
    return 0.5 * (1.0 + t) + 0.5 * x * (1.0 - t * t) * _GELU_C * (1.0 + 3.0 * 0.044715 * x * x)


ATTN_INTERLEAVE = 8
KEY_PAD = BLK * max(DILATIONS)


def _key_mask(n):
    ii = lax.broadcasted_iota(jnp.int32, (BLK, 2 * BLK), 0)
    jj = lax.broadcasted_iota(jnp.int32, (BLK, 2 * BLK), 1)
    return ((jj < BLK) & (jj >= ii) & (n > 0)) | ((jj >= BLK) & (jj - BLK <= ii))


def _units(d, nblk):
    nb = nblk // d
    if nb == 2:
        def unit(idx):
            ii = lax.broadcasted_iota(jnp.int32, (2 * BLK, 2 * BLK), 0)
            jj = lax.broadcasted_iota(jnp.int32, (2 * BLK, 2 * BLK), 1)
            return pl.ds(idx, 2 * BLK, stride=d), pl.ds(KEY_PAD + idx, 2 * BLK, stride=d), (jj <= ii) & (ii - jj <= BLK)
        return d, max(1, ATTN_INTERLEAVE // 4), unit

    def unit(idx):
        r, n = idx // nb, idx % nb
        cur = r + n * (BLK * d)
        keys = cur + (KEY_PAD - BLK * d)
        if d == 1:
            return pl.ds(pl.multiple_of(cur, BLK), BLK), pl.ds(pl.multiple_of(keys, BLK), 2 * BLK), _key_mask(n)
        return pl.ds(cur, BLK, stride=d), pl.ds(keys, 2 * BLK, stride=d), _key_mask(n)
    return nblk, ATTN_INTERLEAVE, unit


def _pad_keys(dst, src):
    dst[pl.ds(0, KEY_PAD), :] = jnp.zeros((KEY_PAD, dst.shape[1]), F32)

    def copy(c, carry):
        dst[pl.ds(pl.multiple_of(KEY_PAD + c * BLK, BLK), BLK), :] = src[pl.ds(pl.multiple_of(c * BLK, BLK), BLK), :]
        return carry

    lax.fori_loop(0, src.shape[0] // BLK, copy, 0)


def attn_fwd(proj, n_heads, *, name):
    S, WP = proj.shape
    assert S % (BLK * max(DILATIONS)) == 0
    nblk = S // BLK
    AW = n_heads * HEAD_DIM
    scale = 1.0 / math.sqrt(HEAD_DIM)

    def body(q_ref, k_ref, v_ref, o_ref, l_ref, acc, mrun, lrun, kp, vp):
        _pad_keys(kp, k_ref)
        _pad_keys(vp, v_ref)
        for first, d in zip((True, False, False), reversed(DILATIONS)):
            n_units, per_step, unit = _units(d, nblk)

            def step(it, carry, first=first, n_units=n_units, per_step=per_step, unit=unit):
                units = [unit(it + j * (n_units // per_step)) for j in range(per_step)]
                ss = [lax.dot_general(q_ref[cur, :].astype(BF16), kp[keys, :].astype(BF16), _NT,
                                      preferred_element_type=F32) * scale for cur, keys, _ in units]
                ss = [jnp.where(mask, s, NEG_INF) for s, (_, _, mask) in zip(ss, units)]
                ms = [jnp.max(s, axis=-1, keepdims=True) for s in ss]
                ps = [jnp.exp(s - m) for s, m in zip(ss, ms)]
                ls = [jnp.sum(p, axis=-1, keepdims=True) for p in ps]
                os_ = [jnp.dot(p.astype(BF16), vp[keys, :].astype(BF16), preferred_element_type=F32)
                       for p, (_, keys, _) in zip(ps, units)]
                for (cur, keys, mask), m, l, o in zip(units, ms, ls, os_):
                    m = jnp.broadcast_to(m, o.shape)
                    l = jnp.broadcast_to(l, o.shape)
                    if first:
                        acc[cur, :], mrun[cur, :], lrun[cur, :] = o, m, l
                    else:
                        m_old = mrun[cur, :]
                        m_new = jnp.maximum(m_old, m)
                        w_old, w_blk = jnp.exp(m_old - m_new), jnp.exp(m - m_new)
                        acc[cur, :] = w_old * acc[cur, :] + w_blk * o
                        lrun[cur, :] = w_old * lrun[cur, :] + w_blk * l
                        mrun[cur, :] = m_new
                return carry

            lax.fori_loop(0, n_units // per_step, step, 0)

        def finish(c, carry):
            r = pl.ds(pl.multiple_of(c * BLK, BLK), BLK)
            o_ref[r, :] = acc[r, :] / lrun[r, :]
            l_ref[r, :] = mrun[r, :] + jnp.log(lrun[r, :])
            return carry

        lax.fori_loop(0, nblk, finish, 0)

    def col(off):
        return pl.BlockSpec((S, HEAD_DIM), lambda h: (0, off + h))

    ospec = pl.BlockSpec((S, HEAD_DIM), lambda h: (0, h))
    return pl.pallas_call(
        body, out_shape=[jax.ShapeDtypeStruct((S, AW), F32)] * 2, grid=(n_heads,),
        in_specs=[col(0), col(n_heads), col(2 * n_heads)], out_specs=[ospec, ospec],
        scratch_shapes=[pltpu.VMEM((S, HEAD_DIM), F32)] * 3 + [pltpu.VMEM((KEY_PAD + S, HEAD_DIM), F32)] * 2,
        compiler_params=_params(("parallel",)), name=name)(proj, proj, proj)


def attn_bwd(proj, do, lse, delta, n_heads, *, name):
    S, WP = proj.shape
    nblk = S // BLK
    AW = n_heads * HEAD_DIM
    scale = 1.0 / math.sqrt(HEAD_DIM)

    def body(q_ref, k_ref, v_ref, do_ref, l_ref, dl_ref, dq_ref, dk_ref, dv_ref, dq_sc, dk_sc, dv_sc, kp, vp):
        _pad_keys(kp, k_ref)
        _pad_keys(vp, v_ref)
        order = list(reversed(DILATIONS))
        assign_first = nblk // order[0] == 2
        if assign_first:
            dk_sc[pl.ds(0, KEY_PAD), :] = jnp.zeros((KEY_PAD, HEAD_DIM), F32)
            dv_sc[pl.ds(0, KEY_PAD), :] = jnp.zeros((KEY_PAD, HEAD_DIM), F32)
        else:
            dq_sc[...] = jnp.zeros_like(dq_sc)
            dk_sc[...] = jnp.zeros_like(dk_sc)
            dv_sc[...] = jnp.zeros_like(dv_sc)
        for assign, d in zip((assign_first, False, False), order):
            n_units, per_step, unit = _units(d, nblk)

            def step(it, carry, n_units=n_units, per_step=per_step, unit=unit, assign=assign):
                units = [unit(it + j * (n_units // per_step)) for j in range(per_step)]
                qs = [q_ref[cur, :].astype(BF16) for cur, _, _ in units]
                gs = [do_ref[cur, :].astype(BF16) for cur, _, _ in units]
                ks = [kp[keys, :].astype(BF16) for _, keys, _ in units]
                ss = [lax.dot_general(q, kb, _NT, preferred_element_type=F32) * scale for q, kb in zip(qs, ks)]
                dps = [lax.dot_general(g, vp[keys, :].astype(BF16), _NT, preferred_element_type=F32)
                       for g, (_, keys, _) in zip(gs, units)]
                ps = [jnp.where(mask, jnp.exp(s - l_ref[cur, :][:, :1]), 0.0) for s, (cur, _, mask) in zip(ss, units)]
                dss = [(p * (dp - dl_ref[cur, :][:, :1]) * scale).astype(BF16)
                       for p, dp, (cur, _, _) in zip(ps, dps, units)]
                for (cur, keys, _), q, g, kb, p, ds in zip(units, qs, gs, ks, ps, dss):
                    dq = jnp.dot(ds, kb, preferred_element_type=F32)
                    dk = lax.dot_general(ds, q, _TN, preferred_element_type=F32)
                    dv = lax.dot_general(p.astype(BF16), g, _TN, preferred_element_type=F32)
                    if assign:
                        dq_sc[cur, :], dk_sc[keys, :], dv_sc[keys, :] = dq, dk, dv
                    else:
                        dq_sc[cur, :] += dq
                        dk_sc[keys, :] += dk
                        dv_sc[keys, :] += dv
                return carry

            lax.fori_loop(0, n_units // per_step, step, 0)
        rows = pl.ds(KEY_PAD, S)
        dq_ref[...] = dq_sc[...].astype(BF16)
        dk_ref[...] = dk_sc[rows, :].astype(BF16)
        dv_ref[...] = dv_sc[rows, :].astype(BF16)

    def col(off):
        return pl.BlockSpec((S, HEAD_DIM), lambda h: (0, off + h))

    ospec = pl.BlockSpec((S, HEAD_DIM), lambda h: (0, h))
    return pl.pallas_call(
        body, out_shape=[jax.ShapeDtypeStruct((S, AW), BF16)] * 3, grid=(n_heads,),
        in_specs=[col(0), col(n_heads), col(2 * n_heads), ospec, ospec, ospec], out_specs=[ospec] * 3,
        scratch_shapes=[pltpu.VMEM((S, HEAD_DIM), F32)] + [pltpu.VMEM((KEY_PAD + S, HEAD_DIM), F32)] * 4,
        compiler_params=_params(("parallel",), VMEM_LIMIT_SCAN), name=name)(proj, proj, proj, do, lse, delta)


def _to_segments(t):
    S, W = t.shape
    return t.reshape(SEGMENTS, S // SEGMENTS, W).swapaxes(0, 1).reshape(S, W)


def _from_segments(t):
    S, W = t.shape
    return t.reshape(S // SEGMENTS, SEGMENTS, W).swapaxes(0, 1).reshape(S, W)


def _cmul(ar, ai, br, bi):
    return ar * br - ai * bi, ar * bi + ai * br


def _power(ar, ai, log2n):
    for _ in range(log2n):
        ar, ai = _cmul(ar, ai, ar, ai)
    return ar, ai


def _shift_rows(x, up):
    row = lax.broadcasted_iota(jnp.int32, x.shape, 0)
    if up:
        return jnp.where(row == SEGMENTS - 1, 0.0, pltpu.roll(x, SEGMENTS - 1, 0))
    return jnp.where(row == 0, 0.0, pltpu.roll(x, 1, 0))


def _segment_carries(er, ei, pr, pi, up):
    cr = jnp.zeros_like(er)
    ci = jnp.zeros_like(ei)
    for _ in range(SEGMENTS - 1):
        tr, ti = _cmul(pr, pi, cr, ci)
        cr, ci = _shift_rows(er + tr, up), _shift_rows(ei + ti, up)
    return cr, ci


def _scan_states(sr, si, ar, ai, T, reverse):
    ns = sr.shape[1]
    ar8 = jnp.broadcast_to(ar, (SEGMENTS, ns))
    ai8 = jnp.broadcast_to(ai, (SEGMENTS, ns))

    def rows(t):
        k = (T - 1 - t) if reverse else t
        return pl.ds(pl.multiple_of(k * SEGMENTS, SEGMENTS), SEGMENTS)

    def advance(t, c):
        tr, ti = _cmul(ar8, ai8, c[0], c[1])
        return tr + sr[rows(t), :], ti + si[rows(t), :]

    def several(step):
        def trip(t, c):
            for j in range(SCAN_UNROLL):
                c = step(t * SCAN_UNROLL + j, c)
            return c
        return trip

    zero = jnp.zeros((SEGMENTS, ns), F32)
    er, ei = lax.fori_loop(0, T // SCAN_UNROLL, several(advance), (zero, zero))
    pr, pi = _power(ar, ai, T.bit_length() - 1)
    cr, ci = _segment_carries(er, ei, jnp.broadcast_to(pr, (SEGMENTS, ns)), jnp.broadcast_to(pi, (SEGMENTS, ns)), reverse)

    def store(t, c):
        nr, ni = advance(t, c)
        sr[rows(t), :] = nr
        si[rows(t), :] = ni
        return nr, ni

    lax.fori_loop(0, T // SCAN_UNROLL, several(store), (cr, ci))
    return cr, ci


def _slab_specs(ns):
    return [pl.BlockSpec((None, LANES, ns), lambda g: (g, 0, 0)),
            pl.BlockSpec((None, LANES, ns), lambda g: (g, 0, 0)),
            pl.BlockSpec((None, 1, ns), lambda g: (g, 0, 0)),
            pl.BlockSpec((None, 1, ns), lambda g: (g, 0, 0)),
            pl.BlockSpec((None, ns, LANES), lambda g: (g, 0, 0)),
            pl.BlockSpec((None, ns, LANES), lambda g: (g, 0, 0)),
            pl.BlockSpec((1, LANES), lambda g: (0, g))]


def _chunks(S):
    rc = _tile(S, 512, 16)
    return rc, S // rc


def ssm_fwd(u, bbr, bbi, ar, ai, cbr, cbi, dsk, *, name):
    S, SW = u.shape
    nslab, _, ns = bbr.shape
    T = S // SEGMENTS
    assert T & (T - 1) == 0
    rc, nc = _chunks(S)

    def body(u_ref, br_ref, bi_ref, ar_ref, ai_ref, cr_ref, ci_ref, d_ref, y_ref, yg_ref, str_ref, sti_ref, sr, si):
        def inputs(c, carry):
            r = pl.ds(pl.multiple_of(c * rc, rc), rc)
            sr[r, :] = jnp.dot(u_ref[r, :], br_ref[...], preferred_element_type=F32)
            si[r, :] = jnp.dot(u_ref[r, :], bi_ref[...], preferred_element_type=F32)
            return carry

        lax.fori_loop(0, nc, inputs, 0)
        _scan_states(sr, si, ar_ref[...], ai_ref[...], T, False)

        def outputs(c, carry):
            r = pl.ds(pl.multiple_of(c * rc, rc), rc)
            srb, sib = sr[r, :].astype(BF16), si[r, :].astype(BF16)
            str_ref[r, :] = srb
            sti_ref[r, :] = sib
            y = (jnp.dot(srb, cr_ref[...], preferred_element_type=F32)
                 - jnp.dot(sib, ci_ref[...], preferred_element_type=F32) + d_ref[...] * u_ref[r, :].astype(F32))
            y_ref[r, :] = y
            yg_ref[r, :] = _gelu(y).astype(BF16)
            return carry

        lax.fori_loop(0, nc, outputs, 0)

    slab = pl.BlockSpec((S, LANES), lambda g: (0, g))
    states = pl.BlockSpec((S, ns), lambda g: (0, g))
    return pl.pallas_call(
        body, out_shape=([jax.ShapeDtypeStruct((S, SW), F32), jax.ShapeDtypeStruct((S, SW), BF16)]
                         + [jax.ShapeDtypeStruct((S, nslab * ns), BF16)] * 2),
        grid=(nslab,), in_specs=[slab] + _slab_specs(ns), out_specs=[slab, slab, states, states],
        scratch_shapes=[pltpu.VMEM((S, ns), F32)] * 2,
        compiler_params=_params(("parallel",), VMEM_LIMIT_SCAN), name=name)(u, bbr, bbi, ar, ai, cbr, cbi, dsk)


def ssm_bwd(u, d_direct, d_gate, y, st_r, st_i, bbr, bbi, ar, ai, cbr, cbi, dsk, *, name, deps=()):
    S, SW = u.shape
    nslab, _, ns = bbr.shape
    T = S // SEGMENTS
    rc, nc = _chunks(S)
    pair_rows = 2 * SEGMENTS

    def body(*refs):
        (u_ref, d1_ref, d2_ref, y_ref, sr_ref, si_ref, br_ref, bi_ref, ar_ref, ai_ref, cr_ref, ci_ref,
         d_ref) = refs[:13]
        (du_ref, dbr_ref, dbi_ref, dcr_ref, dci_ref, dar_ref, dai_ref, dd_ref, lr, li,
         dy_ref) = refs[13 + len(deps):]

        def inputs(c, skip):
            r = pl.ds(pl.multiple_of(c * rc, rc), rc)
            dy = (d1_ref[r, :] + d2_ref[r, :].astype(F32)) * _gelu_grad(y_ref[r, :])
            dy_ref[r, :] = dy
            gb = dy.astype(BF16)
            lr[r, :] = lax.dot_general(gb, cr_ref[...], _NT, preferred_element_type=F32)
            li[r, :] = -lax.dot_general(gb, ci_ref[...], _NT, preferred_element_type=F32)
            return skip + jnp.sum(dy * u_ref[r, :].astype(F32), axis=0, keepdims=True)

        dd_ref[...] = lax.fori_loop(0, nc, inputs, jnp.zeros((1, LANES), F32))
        _scan_states(lr, li, ar_ref[...], -ai_ref[...], T, True)

        def steps(j):
            rows = pl.ds(pl.multiple_of(j * pair_rows, pair_rows), pair_rows)
            tr, ti = sr_ref[rows, :].astype(F32), si_ref[rows, :].astype(F32)
            return tr[:SEGMENTS], tr[SEGMENTS:], ti[:SEGMENTS], ti[SEGMENTS:]

        def pair(j, c):
            acc_r, acc_i, pr, pi = c
            lo_r, hi_r, lo_i, hi_i = steps(j)
            first = pl.ds(pl.multiple_of(j * pair_rows, SEGMENTS), SEGMENTS)
            second = pl.ds(pl.multiple_of(j * pair_rows + SEGMENTS, SEGMENTS), SEGMENTS)
            la_r, la_i, lb_r, lb_i = lr[first, :], li[first, :], lr[second, :], li[second, :]
            return (acc_r + la_r * pr + la_i * pi + lb_r * lo_r + lb_i * lo_i,
                    acc_i - la_r * pi + la_i * pr - lb_r * lo_i + lb_i * lo_r, hi_r, hi_i)

        def pairs(t, c):
            return pair(2 * t + 1, pair(2 * t, c))

        _, end_r, _, end_i = steps(T // 2 - 1)
        zero = jnp.zeros((SEGMENTS, ns), F32)
        acc = lax.fori_loop(0, T // 4, pairs, (zero, zero, _shift_rows(end_r, False), _shift_rows(end_i, False)))
        dar_ref[...] = jnp.sum(acc[0], axis=0, keepdims=True)
        dai_ref[...] = jnp.sum(acc[1], axis=0, keepdims=True)

        dbr_ref[...] = jnp.zeros_like(dbr_ref)
        dbi_ref[...] = jnp.zeros_like(dbi_ref)
        dcr_ref[...] = jnp.zeros_like(dcr_ref)
        dci_ref[...] = jnp.zeros_like(dci_ref)

        def outputs(c, carry):
            r = pl.ds(pl.multiple_of(c * rc, rc), rc)
            ub = u_ref[r, :]
            g = dy_ref[r, :]
            gb = g.astype(BF16)
            lrb = lr[r, :].astype(BF16)
            lib = li[r, :].astype(BF16)
            du_ref[r, :] = (lax.dot_general(lrb, br_ref[...], _NT, preferred_element_type=F32)
                            + lax.dot_general(lib, bi_ref[...], _NT, preferred_element_type=F32)
                            + d_ref[...] * g).astype(BF16)
            dbr_ref[...] += lax.dot_general(ub, lrb, _TN, preferred_element_type=F32)
            dbi_ref[...] += lax.dot_general(ub, lib, _TN, preferred_element_type=F32)
            dcr_ref[...] += lax.dot_general(sr_ref[r, :], gb, _TN, preferred_element_type=F32)
            dci_ref[...] -= lax.dot_general(si_ref[r, :], gb, _TN, preferred_element_type=F32)
            return carry

        lax.fori_loop(0, nc, outputs, 0)

    slab = pl.BlockSpec((S, LANES), lambda g: (0, g))
    states = pl.BlockSpec((S, ns), lambda g: (0, g))
    bspec = pl.BlockSpec((None, LANES, ns), lambda g: (g, 0, 0))
    cspec = pl.BlockSpec((None, ns, LANES), lambda g: (g, 0, 0))
    aspec = pl.BlockSpec((None, 1, ns), lambda g: (g, 0, 0))
    return pl.pallas_call(
        body,
        out_shape=[jax.ShapeDtypeStruct((S, SW), BF16),
                   jax.ShapeDtypeStruct((nslab, LANES, ns), F32), jax.ShapeDtypeStruct((nslab, LANES, ns), F32),
                   jax.ShapeDtypeStruct((nslab, ns, LANES), F32), jax.ShapeDtypeStruct((nslab, ns, LANES), F32),
                   jax.ShapeDtypeStruct((nslab, 1, ns), F32), jax.ShapeDtypeStruct((nslab, 1, ns), F32),
                   jax.ShapeDtypeStruct((1, SW), F32)],
        grid=(nslab,), in_specs=[slab, slab, slab, slab, states, states] + _slab_specs(ns) + [_ANY] * len(deps),
        out_specs=[slab, bspec, bspec, cspec, cspec, aspec, aspec, pl.BlockSpec((1, LANES), lambda g: (0, g))],
        scratch_shapes=[pltpu.VMEM((S, ns), F32)] * 2 + [pltpu.VMEM((S, LANES), F32)],
        compiler_params=_params(("parallel",), VMEM_LIMIT_SCAN), name=name)(
            u, d_direct, d_gate, y, st_r, st_i, bbr, bbi, ar, ai, cbr, cbi, dsk, *deps)


def _discretise(lam_re, lam_im, log_dt, b_re, b_im):
    dt = jnp.exp(log_dt)[:, None]
    mag = jnp.exp(lam_re * dt)
    ar = mag * jnp.cos(lam_im * dt)
    ai = mag * jnp.sin(lam_im * dt)
    nr, ni = ar - 1.0, ai
    den = lam_re * lam_re + lam_im * lam_im
    cr = ((nr * lam_re + ni * lam_im) / den)[..., None]
    ci = ((ni * lam_re - nr * lam_im) / den)[..., None]
    return ar, ai, cr * b_re - ci * b_im, cr * b_im + ci * b_re


def _block_diag(t, nslab):
    G, R, C = t.shape
    eye = jnp.eye(SLAB_GROUPS, dtype=t.dtype)
    t = t.reshape(nslab, SLAB_GROUPS, R, C)
    return jnp.einsum('sgrc,gh->sgrhc', t, eye).reshape(nslab, SLAB_GROUPS * R, SLAB_GROUPS * C)


def _block_diag_part(t, R, C):
    nslab = t.shape[0]
    eye = jnp.eye(SLAB_GROUPS, dtype=t.dtype)
    t = t.reshape(nslab, SLAB_GROUPS, R, SLAB_GROUPS, C)
    return jnp.einsum('sgrhc,gh->sgrc', t, eye).reshape(nslab * SLAB_GROUPS, R, C)


def _place():
    return lax.axis_index("x"), lax.axis_index("y"), lax.axis_index("c")


_HBM = pl.BlockSpec(memory_space=pltpu.HBM)
_SEM = pl.BlockSpec(memory_space=pltpu.SEMAPHORE)
_ORDERED_EFFECT = pltpu.SideEffectType.DATAFLOW_SIDE_EFFECTING


def _split_call(name, srcs, zones, sems_in, n_new, body_fn, after):
    nsrc, nz, ns, nn = len(srcs), len(zones), len(sems_in), len(n_new)
    nb = nsrc + nz

    def body(*refs):
        outs = refs[nb + ns + 1:]
        body_fn(refs[:nb], refs[nb:nb + ns], outs[:nn])
        outs[nn + nz][...] = jnp.zeros((SUBLANES, LANES), F32)

    res = pl.pallas_call(
        body, name=name,
        out_shape=([pltpu.SemaphoreType.DMA((n,)) for n in n_new] + [pltpu.HBM(b.shape, b.dtype) for b in zones]
                   + [jax.ShapeDtypeStruct((SUBLANES, LANES), F32)]),
        in_specs=[_HBM] * nb + [_SEM] * ns + [_ANY],
        out_specs=[_SEM] * nn + [_HBM] * nz + [pl.BlockSpec(memory_space=pltpu.VMEM)],
        input_output_aliases={nsrc + i: nn + i for i in range(nz)},
        compiler_params=pltpu.CompilerParams(has_side_effects=_ORDERED_EFFECT))(
            *[pltpu.with_memory_space_constraint(b, pltpu.HBM) for b in list(srcs) + list(zones)], *sems_in, after)
    return list(res[:nn]), list(res[nn:nn + nz]), res[-1]


def _mesh_peers():
    x, y, c = _place()
    return x, y, c, (x, y, 1 - c), [(1 - x, y), (x, 1 - y), (1 - x, 1 - y)]


def gather_start(shards, after, *, name):
    nw = len(shards)
    x, y, c = _place()
    zones = [lax.dynamic_update_slice(lax.empty((N_DEV,) + s.shape, s.dtype), s[None], (4 * x + 2 * y + c, 0, 0))
             for s in shards]

    def body(bufs, taken, new):
        for cp in _gather_first(bufs, nw, new[0], new[1]):
            cp.start()

    sems, zones, token = _split_call(name, shards, zones, [], [4 * nw, 4 * nw], body, after)
    return shards, sems, zones, token


def _gather_first(bufs, nw, send, recv):
    x, y, c, sibling, chips = _mesh_peers()
    out = []
    for w in range(nw):
        slot = bufs[nw + w].at[4 * x + 2 * y + c]
        for k, to in enumerate([sibling] + [(*ch, c) for ch in chips]):
            out.append(pltpu.make_async_remote_copy(
                src_ref=bufs[w], dst_ref=slot, send_sem=send.at[4 * w + k], recv_sem=recv.at[4 * w + k],
                device_id=to, device_id_type=MESH))
    return out


def _gather_slot_copy(bufs, nw, w, block, send_sem, recv_sem, to):
    px, py, pc = block
    slot = bufs[nw + w].at[4 * px + 2 * py + pc]
    return pltpu.make_async_remote_copy(src_ref=slot, dst_ref=slot, send_sem=send_sem, recv_sem=recv_sem,
                                        device_id=to, device_id_type=MESH)


def gather_forward(state, after, *, name):
    shards, sems, zones, _ = state
    nw = len(shards)

    def body(bufs, taken, new):
        x, y, c, sibling, chips = _mesh_peers()
        for j, ch in enumerate(chips):
            for w in range(nw):
                k = 4 * w + 1 + j
                _gather_slot_copy(bufs, nw, w, (*ch, c), taken[0].at[k], taken[1].at[k], (*ch, c)).wait_recv()
                _gather_slot_copy(bufs, nw, w, (*ch, c), new[0].at[3 * w + j], new[1].at[3 * w + j], sibling).start()
        for w in range(nw):
            _gather_slot_copy(bufs, nw, w, sibling, taken[0].at[4 * w], taken[1].at[4 * w], sibling).wait_recv()
        for cp in _gather_first(bufs, nw, taken[0], taken[1]):
            cp.wait_send()

    sems, zones, token = _split_call(name, shards, zones, sems, [3 * nw, 3 * nw], body, after)
    return shards, sems, zones, token


def gather_finish(state, after, *, name):
    shards, sems, zones, _ = state
    nw = len(shards)

    def body(bufs, taken, new):
        x, y, c, sibling, chips = _mesh_peers()
        for w in range(nw):
            for j, ch in enumerate(chips):
                cp = _gather_slot_copy(bufs, nw, w, (*ch, 1 - c), taken[0].at[3 * w + j], taken[1].at[3 * w + j], sibling)
                cp.wait_send()
                cp.wait_recv()

    _, zones, _ = _split_call(name, shards, zones, sems, [], body, after)
    return zones


def exchange_start(srcs, zone_shapes, copies, n, after, *, name):
    nw = len(srcs)
    zones = [lax.empty(z, s.dtype) for z, s in zip(zone_shapes, srcs)]

    def body(bufs, taken, new):
        for cp in copies(bufs[:nw], bufs[nw:], new[0], new[1]):
            cp.start()

    sems, zones, token = _split_call(name, srcs, zones, [], [n, n], body, after)
    return srcs, copies, sems, zones, token


def exchange_wait(state, after, *, name):
    srcs, copies, sems, zones, _ = state
    nw = len(srcs)

    def body(bufs, taken, new):
        for cp in copies(bufs[:nw], bufs[nw:], taken[0], taken[1]):
            cp.wait_send()
            cp.wait_recv()

    _, zones, _ = _split_call(name, srcs, zones, sems, [], body, after)
    return zones


def _core_copies(srcs, zones, send, recv):
    x, y, c = _place()
    return [pltpu.make_async_remote_copy(
        src_ref=srcs[w].at[:, 1 - c], dst_ref=zones[w], send_sem=send.at[w], recv_sem=recv.at[w],
        device_id=(x, y, 1 - c), device_id_type=MESH) for w in range(len(srcs))]


def _chip_copies(srcs, zones, send, recv):
    x, y, c = _place()
    chips = [(1 - x, y), (x, 1 - y), (1 - x, 1 - y)]
    return [pltpu.make_async_remote_copy(
        src_ref=srcs[w].at[2 * cx + cy], dst_ref=zones[w].at[j], send_sem=send.at[3 * w + j],
        recv_sem=recv.at[3 * w + j], device_id=(cx, cy, c), device_id_type=MESH)
        for w in range(len(srcs)) for j, (cx, cy) in enumerate(chips)]


def _blocked(fn, ins, outs, *, name, place=None, tr=256):
    k, n = outs[0][0]
    tr = _tile(k, tr, 16)
    if place is None:
        place = jnp.zeros((1,), jnp.int32)
    args = [a[0] if isinstance(a, tuple) else a for a in ins]
    leads = [a[1] if isinstance(a, tuple) else None for a in ins]
    nin, nout, n_steps = len(args), len(outs), k // tr

    def body(place_ref, *refs):
        srcs, o = refs[:nin], refs[nin:nin + nout]
        bufs, sems = refs[nin + nout:2 * nin + nout], refs[2 * nin + nout:]
        i = pl.program_id(0)

        def fetch(step, slot):
            copies = []
            for x, lead in enumerate(leads):
                if lead is None:
                    src = srcs[x].at[pl.ds(pl.multiple_of(step * tr, tr), tr), :]
                else:
                    which, block = lead(step, place_ref)
                    src = srcs[x].at[which, pl.ds(pl.multiple_of(block * tr, tr), tr), :]
                copies.append(pltpu.make_async_copy(src, bufs[x].at[slot], sems[x].at[slot]))
            return copies

        @pl.when(i == 0)
        def _():
            for step in range(min(ROW_BUFFERS - 1, n_steps)):
                for cp in fetch(step, step):
                    cp.start()

        ahead = i + (ROW_BUFFERS - 1)

        @pl.when(ahead < n_steps)
        def _():
            for cp in fetch(ahead, ahead % ROW_BUFFERS):
                cp.start()

        slot = i % ROW_BUFFERS
        for cp in fetch(i, slot):
            cp.wait()
        res = fn(*[b[slot] for b in bufs])
        for ref, val in zip(o, res):
            ref[...] = val.astype(ref.dtype)

    return pl.pallas_call(
        body, out_shape=[jax.ShapeDtypeStruct(s, d) for s, d in outs],
        grid_spec=pltpu.PrefetchScalarGridSpec(
            num_scalar_prefetch=1, grid=(n_steps,), in_specs=[_ANY] * nin,
            out_specs=[pl.BlockSpec((tr, n), lambda i, s: (i, 0)) for _ in outs],
            scratch_shapes=([pltpu.VMEM((ROW_BUFFERS, tr, n), a.dtype) for a in args]
                            + [pltpu.SemaphoreType.DMA((ROW_BUFFERS,)) for _ in args])),
        compiler_params=_params(("arbitrary",)), name=name)(place, *args)


def _adamw(w, g, m, v):
    m = ADAM_B1 * m + (1.0 - ADAM_B1) * g
    v = ADAM_B2 * v + (1.0 - ADAM_B2) * (g * g)
    m_hat = m / (1.0 - ADAM_B1 ** ADAM_STEP)
    v_hat = v / (1.0 - ADAM_B2 ** ADAM_STEP)
    delta = -ADAM_LR * (m_hat * pl.reciprocal(jnp.sqrt(v_hat) + ADAM_EPS, approx=True) + ADAM_WD * w)
    return delta, m, v


def kernel(x, p, mix_norm_pre, w_in, lam_re, lam_im, log_dt, ssm_b_re, ssm_b_im, ssm_c_re, ssm_c_im, ssm_d, w_glu, b_glu, attn_out_norm, ssm_out_norm, w_out, mix_norm_post, mlp_norm_pre, w_up, w_down, mlp_norm_post, ple_norm_pre, w_ple_gate, w_ple_proj, ple_norm_post, loss_target, m_mix_norm_pre, m_w_in, m_lam_re, m_lam_im, m_log_dt, m_ssm_b_re, m_ssm_b_im, m_ssm_c_re, m_ssm_c_im, m_ssm_d, m_w_glu, m_b_glu, m_attn_out_norm, m_ssm_out_norm, m_w_out, m_mix_norm_post, m_mlp_norm_pre, m_w_up, m_w_down, m_mlp_norm_post, m_ple_norm_pre, m_w_ple_gate, m_w_ple_proj, m_ple_norm_post, v_mix_norm_pre, v_w_in, v_lam_re, v_lam_im, v_log_dt, v_ssm_b_re, v_ssm_b_im, v_ssm_c_re, v_ssm_c_im, v_ssm_d, v_w_glu, v_b_glu, v_attn_out_norm, v_ssm_out_norm, v_w_out, v_mix_norm_post, v_mlp_norm_pre, v_w_up, v_w_down, v_mlp_norm_post, v_ple_norm_pre, v_w_ple_gate, v_w_ple_proj, v_ple_norm_post):
    weights = dict(mix_norm_pre=mix_norm_pre, w_in=w_in, lam_re=lam_re, lam_im=lam_im, log_dt=log_dt, ssm_b_re=ssm_b_re, ssm_b_im=ssm_b_im, ssm_c_re=ssm_c_re, ssm_c_im=ssm_c_im, ssm_d=ssm_d, w_glu=w_glu, b_glu=b_glu, attn_out_norm=attn_out_norm, ssm_out_norm=ssm_out_norm, w_out=w_out, mix_norm_post=mix_norm_post, mlp_norm_pre=mlp_norm_pre, w_up=w_up, w_down=w_down, mlp_norm_post=mlp_norm_post, ple_norm_pre=ple_norm_pre, w_ple_gate=w_ple_gate, w_ple_proj=w_ple_proj, ple_norm_post=ple_norm_post)
    mom_m = dict(mix_norm_pre=m_mix_norm_pre, w_in=m_w_in, lam_re=m_lam_re, lam_im=m_lam_im, log_dt=m_log_dt, ssm_b_re=m_ssm_b_re, ssm_b_im=m_ssm_b_im, ssm_c_re=m_ssm_c_re, ssm_c_im=m_ssm_c_im, ssm_d=m_ssm_d, w_glu=m_w_glu, b_glu=m_b_glu, attn_out_norm=m_attn_out_norm, ssm_out_norm=m_ssm_out_norm, w_out=m_w_out, mix_norm_post=m_mix_norm_post, mlp_norm_pre=m_mlp_norm_pre, w_up=m_w_up, w_down=m_w_down, mlp_norm_post=m_mlp_norm_post, ple_norm_pre=m_ple_norm_pre, w_ple_gate=m_w_ple_gate, w_ple_proj=m_w_ple_proj, ple_norm_post=m_ple_norm_post)
    mom_v = dict(mix_norm_pre=v_mix_norm_pre, w_in=v_w_in, lam_re=v_lam_re, lam_im=v_lam_im, log_dt=v_log_dt, ssm_b_re=v_ssm_b_re, ssm_b_im=v_ssm_b_im, ssm_c_re=v_ssm_c_re, ssm_c_im=v_ssm_c_im, ssm_d=v_ssm_d, w_glu=v_w_glu, b_glu=v_b_glu, attn_out_norm=v_attn_out_norm, ssm_out_norm=v_ssm_out_norm, w_out=v_w_out, mix_norm_post=v_mix_norm_post, mlp_norm_pre=v_mlp_norm_pre, w_up=v_w_up, w_down=v_w_down, mlp_norm_post=v_mlp_norm_post, ple_norm_pre=v_ple_norm_pre, w_ple_gate=v_w_ple_gate, w_ple_proj=v_w_ple_proj, ple_norm_post=v_ple_norm_post)
    order = list(weights)
    big = ["w_in", "w_glu", "w_out", "w_up", "w_down", "w_ple_gate", "w_ple_proj"]
    col_sharded = {"w_in", "w_up", "w_ple_proj"}
    small = [n for n in order if n not in big]

    _, S, D = x.shape
    xs = x[0]
    tgt = loss_target[0]
    AW = attn_out_norm.shape[1]
    SW = ssm_d.shape[1]
    H = AW // HEAD_DIM
    G = SW // SSM_GROUP
    nslab = G // SLAB_GROUPS
    P_, C_ = SSM_STATE, SSM_GROUP

    shard = {n: weights[n][0].astype(BF16) for n in big}
    W, WT = {}, {}

    def arrived(names, gathered):
        for n, g in zip(names, gathered):
            W[n] = g if n in col_sharded else g.reshape(1, N_DEV * g.shape[1], g.shape[2])

    def transposed(g):
        return jnp.swapaxes(g, 1, 2).reshape(1, g.shape[0] * g.shape[2], g.shape[1])

    g1, g2, g3, g4, g5, g6 = (weights[n] for n in ("mix_norm_pre", "mix_norm_post", "mlp_norm_pre",
                                                      "mlp_norm_post", "ple_norm_pre", "ple_norm_post"))
    ga, gs = attn_out_norm, ssm_out_norm
    gather_in = gather_start([shard["w_in"]], shard["w_in"], name="gather_w_in_start")
    (hn1,) = rowwise(lambda a, g: (_rms(a, g),), [xs], [g1], [(D, BF16)], deps=(gather_in[-1],), name="norm_in")
    gather_in = gather_forward(gather_in, hn1, name="gather_w_in_forward")
    arrived(["w_in"], gather_finish(gather_in, gather_in[-1], name="gather_w_in_finish"))
    WT["w_in"] = transposed(W["w_in"])
    early, mid, late = ["w_glu", "w_out"], ["w_up"], ["w_down", "w_ple_gate", "w_ple_proj"]
    gather_early = gather_start([shard[n] for n in early], W["w_in"], name="gather_early_start")
    gather_mid = gather_start([shard[n] for n in mid], gather_early[-1], name="gather_mid_start")
    gather_late = gather_start([shard[n] for n in late], gather_mid[-1], name="gather_late_start")

    (proj,) = mm_nn(hn1, W["w_in"], [F32], deps=(gather_late[-1],), name="proj_in")
    attn, lse = attn_fwd(proj, H, name="attn_fwd")
    gather_early = gather_forward(gather_early, attn, name="gather_early_forward")
    (mix_a,) = rowwise(lambda a, g: (_rms(a, g),), [attn], [ga], [(AW, BF16)], deps=(gather_early[-1],),
                       name="attn_norm")
    arrived(early, gather_finish(gather_early, mix_a, name="gather_early_finish"))

    a_r, a_i, bb_r, bb_i = _discretise(lam_re[0], lam_im[0], log_dt[0], ssm_b_re[0], ssm_b_im[0])
    ssm_consts = (_block_diag(bb_r.swapaxes(1, 2), nslab).astype(BF16), _block_diag(bb_i.swapaxes(1, 2), nslab).astype(BF16),
                  a_r.reshape(nslab, 1, SLAB_STATES), a_i.reshape(nslab, 1, SLAB_STATES),
                  _block_diag(ssm_c_re[0].swapaxes(1, 2), nslab).astype(BF16),
                  _block_diag(ssm_c_im[0].swapaxes(1, 2), nslab).astype(BF16), ssm_d)
    u_seg = _to_segments(proj[:, 3 * AW:]).astype(BF16)
    y_pre, yg, st_r, st_i = ssm_fwd(u_seg, *ssm_consts, name="ssm_fwd")
    gather_mid = gather_forward(gather_mid, y_pre, name="gather_mid_forward")
    (gl1,) = mm_nn(yg, W["w_glu"], [BF16], epi=lambda acc, b: (acc + b,), bias=b_glu, deps=(gather_mid[-1],),
                   name="glu_gate")
    (mix_s,) = rowwise(lambda yp, gl, g: (_rms(_gelu(yp) * _sigmoid(gl), g),), [y_pre, gl1], [gs], [(SW, BF16)],
                       name="ssm_glu_norm")
    mixed = [mix_a, _from_segments(mix_s)]
    (mo,) = mm_nn(mixed, W["w_out"], [BF16], name="mix_out")

    def resid_norm(h, t, gpost, gpre):
        hh = h + _rms(t, gpost)
        return hh, _rms(hh, gpre)

    h1, hn2 = rowwise(resid_norm, [xs, mo], [g2, g3], [(D, F32), (D, BF16)], name="resid_mix")
    arrived(mid, gather_finish(gather_mid, hn2, name="gather_mid_finish"))
    gather_late = gather_forward(gather_late, W["w_up"], name="gather_late_forward")
    WT["w_up"] = transposed(W["w_up"])

    def relu2(acc):
        r = jnp.maximum(acc, 0.0)
        return acc, r * r

    up, act = mm_nn(hn2, W["w_up"], [BF16, BF16], epi=relu2, deps=(gather_late[-1],), tm=1024, tn=1024, name="mlp_up")
    arrived(late, gather_finish(gather_late, act, name="gather_late_finish"))
    (ff,) = mm_nn(act, W["w_down"], [BF16], name="mlp_down")
    h2, hn3 = rowwise(resid_norm, [h1, ff], [g4, g5], [(D, F32), (D, BF16)], name="resid_mlp")
    (gl2,) = mm_nn(hn3, W["w_ple_gate"], [BF16], name="ple_gate")
    pb = p[0, 0].astype(BF16)
    (emb,) = mm_nn(pb, W["w_ple_proj"], [BF16], name="ple_proj")

    def head(h, gl, e, t, g):
        sg = _sigmoid(gl)
        ge = sg * e
        err = h + _rms(ge, g) - t
        dh = err * (1.0 / D)
        dge, dg = _rms_bwd(dh, ge, g)
        return dh, dge * e * sg * (1.0 - sg), dge * sg, jnp.sum(err * err, axis=0, keepdims=True), dg

    dh3, dgl2, demb, loss_part, dg6 = rowwise(head, [h2, gl2, emb, tgt], [g6], [(D, F32), (D, BF16), (D, BF16)],
                                             [D, D], name="ple_loss_head")
    loss = lax.psum(0.5 / D * jnp.sum(loss_part), ("x", "y", "c"))

    x_i, y_i, c_i = _place()
    place = jnp.stack([c_i, 2 * x_i + y_i]).astype(jnp.int32)
    grads, out_g, out_d, out_m, out_v = {}, {}, {}, {}, {}

    def to_sibling(names, after, tag):
        chunks = []
        for n in names:
            g = grads[n]
            g = g if n in col_sharded else g.reshape(N_DEV, g.shape[1] // N_DEV, g.shape[2])
            chunks.append(g.reshape(4, 2, g.shape[1], g.shape[2]))
        return chunks, exchange_start(chunks, [(4,) + g.shape[2:] for g in chunks], _core_copies, len(chunks), after,
                                      name=f"grads_to_sibling_{tag}")

    def to_chips(names, sent, after, tag):
        chunks, state = sent
        sums = []
        for n, g, r in zip(names, chunks, exchange_wait(state, after, name=f"grads_from_sibling_{tag}")):
            k, nn = g.shape[2], g.shape[3]
            kb = k // _tile(k, 512, 16)

            def mine(i, s, kb=kb):
                return 2 * (i // kb) + s[0], i % kb

            (s,) = _blocked(lambda a, b: (a.astype(F32) + b.astype(F32),),
                            [(g.reshape(N_DEV, k, nn), mine), r.reshape(4 * k, nn)],
                            [((4 * k, nn), BF16)], place=place, tr=k // kb, name=f"chip_sum_{n}")
            sums.append(s.reshape(4, k, nn))
        return sums, exchange_start(sums, [(3,) + s.shape[1:] for s in sums], _chip_copies, 3 * len(sums), sums[-1],
                                    name=f"grads_to_chips_{tag}")

    def update(w_, m_, v_, own, r0, r1, r2):
        g = own.astype(F32) + r0.astype(F32) + r1.astype(F32) + r2.astype(F32)
        return (g,) + _adamw(w_, g, m_, v_)

    def finish(names, sent, after, tag):
        sums, state = sent
        for n, s, r in zip(names, sums, exchange_wait(state, after, name=f"grads_from_chips_{tag}")):
            shp = weights[n].shape
            res = _blocked(update, [weights[n][0], mom_m[n][0], mom_v[n][0], (s, lambda i, p_: (p_[1], i)),
                                    (r, lambda i, p_: (0, i)), (r, lambda i, p_: (1, i)), (r, lambda i, p_: (2, i))],
                           [(shp[1:], F32)] * 4, place=place, tr=max(16, min(shp[1] // 8, 262144 // shp[2])),
                           name=f"adamw_{n}")
            out_g[n], out_d[n], out_m[n], out_v[n] = (t.reshape(shp) for t in res)
        return out_v[names[-1]]

    grads["w_ple_proj"] = mm_tn(pb, demb, N_DEV, name="grad_w_ple_proj")
    dhn3 = mm_nt(dgl2, W["w_ple_gate"], BF16, name="back_ple_gate")
    grads["w_ple_gate"] = mm_tn(hn3, dgl2, 1, name="grad_w_ple_gate")

    def back_resid(dh, dhn, h, t, gpre, gpost):
        d1, dgpre = _rms_bwd(dhn, h, gpre)
        dhh = dh + d1
        dt, dgpost = _rms_bwd(dhh, t, gpost)
        return dhh, dt, dgpre, dgpost

    dh2, dff, dg5, dg4 = rowwise(back_resid, [dh3, dhn3, h2, ff], [g5, g4], [(D, F32), (D, BF16)], [D, D],
                                 name="back_resid_mlp")
    dup = mm_nt(dff, W["w_down"], BF16, epi=lambda acc, u_: (acc * 2.0 * jnp.maximum(u_.astype(F32), 0.0),),
                extra=up, name="back_mlp_down")
    grads["w_down"] = mm_tn(act, dff, 1, name="grad_w_down")
    group_a = ["w_ple_proj", "w_ple_gate", "w_down"]
    sent_a = to_sibling(group_a, grads["w_down"], "a")
    (dhn2,) = mm_nn(dup, WT["w_up"], [BF16], deps=(sent_a[1][-1],), name="back_mlp_up")
    sent_a = to_chips(group_a, sent_a, dhn2, "a")
    grads["w_up"] = mm_tn(hn2, dup, N_DEV, deps=(sent_a[1][-1],), name="grad_w_up")
    dh1, dmo, dg3, dg2 = rowwise(back_resid, [dh2, dhn2, h1, mo], [g3, g2], [(D, F32), (D, BF16)], [D, D],
                                 name="back_resid_mix")
    dmixed = mm_nt(dmo, W["w_out"], BF16, name="back_mix_out")
    grads["w_out"] = mm_tn(mixed, dmo, 1, name="grad_w_out")

    def back_glu(dm, yp, gl, g):
        ygf = _gelu(yp)
        sg = _sigmoid(gl)
        dssm, dg = _rms_bwd(dm, ygf * sg, g)
        dgl = dssm * ygf * sg * (1.0 - sg)
        return dgl, dssm * sg, dg, jnp.sum(dgl, axis=0, keepdims=True)

    dgl1, dyg_direct, dgs, db_glu = rowwise(back_glu, [_to_segments(dmixed[:, AW:]), y_pre, gl1], [gs],
                                            [(SW, BF16), (SW, F32)], [SW, SW], name="back_glu")
    dyg_gate = mm_nt(dgl1, W["w_glu"], BF16, name="back_glu_gate")
    grads["w_glu"] = mm_tn(yg, dgl1, 1, name="grad_w_glu")
    group_b = ["w_up", "w_out", "w_glu"]
    sent_b = to_sibling(group_b, grads["w_glu"], "b")
    done_a = finish(group_a, sent_a, sent_b[1][-1], "a")

    du_seg, dbb_r, dbb_i, dcb_r, dcb_i, da_r, da_i, d_skip = ssm_bwd(
        u_seg, dyg_direct, dyg_gate, y_pre, st_r, st_i, *ssm_consts, deps=(done_a,), name="ssm_bwd")
    sent_b = to_chips(group_b, sent_b, du_seg, "b")

    def back_attn_norm(dm, a, g):
        da, dg = _rms_bwd(dm, a, g)
        prod = da * a
        delta = jnp.concatenate(
            [jnp.broadcast_to(jnp.sum(prod[:, h * HEAD_DIM:(h + 1) * HEAD_DIM], axis=-1, keepdims=True),
                              (prod.shape[0], HEAD_DIM)) for h in range(H)], axis=1)
        return da, delta, dg

    dattn, delta, dga = rowwise(back_attn_norm, [(dmixed, AW, 0), attn], [ga], [(AW, F32), (AW, F32)], [AW],
                                deps=(sent_b[1][-1],), name="back_attn_norm")
    dq, dk, dv = attn_bwd(proj, dattn, lse, delta, H, name="attn_bwd")
    dproj = [dq, dk, dv, _from_segments(du_seg)]
    (dhn1,) = mm_nn(dproj, WT["w_in"], [BF16], name="back_proj_in")

    def back_in(dh, dhn, a, g):
        d1, dg = _rms_bwd(dhn, a, g)
        return dh + d1, dg

    grad_x, dg1 = rowwise(back_in, [dh1, dhn1, xs], [g1], [(D, F32)], [D], name="back_norm_in")

    cot = dict(
        mix_norm_pre=dg1, mix_norm_post=dg2, mlp_norm_pre=dg3, mlp_norm_post=dg4, ple_norm_pre=dg5, ple_norm_post=dg6,
        attn_out_norm=dga, ssm_out_norm=dgs, b_glu=db_glu, ssm_d=d_skip,
        ssm_c_re=_block_diag_part(dcb_r, P_, C_).swapaxes(1, 2), ssm_c_im=_block_diag_part(dcb_i, P_, C_).swapaxes(1, 2),
        a_r=da_r.reshape(G, P_), a_i=da_i.reshape(G, P_),
        bb_r=_block_diag_part(dbb_r, C_, P_).swapaxes(1, 2), bb_i=_block_diag_part(dbb_i, C_, P_).swapaxes(1, 2))
    names = list(cot)
    flat = jnp.concatenate([cot[n].reshape(-1) for n in names])
    total = flat.shape[0]
    rows_ = -(-total // (LANES * 16)) * 16
    flat = jnp.pad(flat, (0, rows_ * LANES - total)).reshape(rows_, LANES)
    gather_small = gather_start([flat], flat, name="gather_small_start")
    grads["w_in"] = mm_tn(hn1, dproj, N_DEV, deps=(gather_small[-1],), tko=2048, name="grad_w_in")
    group_c = ["w_in"]
    sent_c = to_sibling(group_c, grads["w_in"], "c")
    done_b = finish(group_b, sent_b, sent_c[1][-1], "b")
    sent_c = to_chips(group_c, sent_c, done_b, "c")
    gather_small = gather_forward(gather_small, sent_c[1][-1], name="gather_small_forward")
    (every,) = gather_finish(gather_small, gather_small[-1], name="gather_small_finish")
    (summed,) = _blocked(lambda *t: (functools.reduce(lambda a, b: a + b, t),),
                         [(every, functools.partial(lambda i, p_, j: (j, i), j=j)) for j in range(N_DEV)],
                         [((rows_, LANES), F32)], name="sum_small_grads")
    summed = summed.reshape(-1)
    red, off = {}, 0
    for n in names:
        sz = cot[n].size
        red[n] = summed[off:off + sz].reshape(cot[n].shape)
        off += sz
    _, pull = jax.vjp(_discretise, lam_re[0], lam_im[0], log_dt[0], ssm_b_re[0], ssm_b_im[0])
    d_lre, d_lim, d_ldt, d_bre, d_bim = pull((red["a_r"], red["a_i"], red["bb_r"], red["bb_i"]))
    red.update(lam_re=d_lre, lam_im=d_lim, log_dt=d_ldt, ssm_b_re=d_bre, ssm_b_im=d_bim)

    def pack(d):
        t = jnp.concatenate([d[n].reshape(-1) for n in small])
        r_ = -(-t.shape[0] // (LANES * 16)) * 16
        return jnp.pad(t, (0, r_ * LANES - t.shape[0])).reshape(r_, LANES)

    sw, sg_, sm, sv = pack(weights), pack(red), pack(mom_m), pack(mom_v)
    sd, snm, snv = _blocked(lambda w_, g_, m_, v_: _adamw(w_, g_, m_, v_), [sw, sg_, sm, sv],
                            [(sw.shape, F32)] * 3, name="adamw_small")
    finish(group_c, sent_c, snv, "c")
    off = 0
    for n in small:
        sz = weights[n].size
        shp = weights[n].shape
        out_g[n] = red[n].reshape(shp)
        out_d[n] = sd.reshape(-1)[off:off + sz].reshape(shp)
        out_m[n] = snm.reshape(-1)[off:off + sz].reshape(shp)
        out_v[n] = snv.reshape(-1)[off:off + sz].reshape(shp)
        off += sz

    return (loss, grad_x[None], *[out_g[n] for n in order], *[out_d[n] for n in order],
            *[out_m[n] for n in order], *[out_v[n] for n in order])
```
